```python
import math
import jax, jax.numpy as jnp
from jax import lax
import numpy as np

D_MODEL = 1024
BATCH = 16
SEQ = 4096
DEPTH = 1

HEAD_DIM = 64
DIL_GROUPS = ((128, 1), (512, 4), (2048, 16))
N_DIL = len(DIL_GROUPS)
DIL_HEADS_PER_GROUP = 4
DIL_HEADS = N_DIL * DIL_HEADS_PER_GROUP
DIL_WIDTH = DIL_HEADS * HEAD_DIM
DIL_OUT = DIL_HEADS_PER_GROUP * HEAD_DIM
FOX_HEADS = 8
FOX_WIDTH = FOX_HEADS * HEAD_DIM
BLOCK = 128
D_FF = 2816
N_MOD = 9
RMS_EPS = 1e-6
ALIBI_MAX_BIAS = 8.0
NEG_INF = -1e30
IN_SIZES = (DIL_WIDTH, DIL_WIDTH, DIL_WIDTH, FOX_WIDTH, FOX_WIDTH, FOX_WIDTH,
            FOX_HEADS, D_MODEL, D_MODEL)
N_IN = sum(IN_SIZES)

kernel_name = "hybrid_dilated_fox_macaron_block"


def rms_norm(x, g):
    xf = x.astype(jnp.float32)
    y = xf * lax.rsqrt(jnp.mean(xf * xf, axis=-1, keepdims=True) + RMS_EPS)
    return (y * g.astype(jnp.float32)).astype(x.dtype)


def modulate(h, shift, scale):
    return h * (1 + scale[:, None, :]) + shift[:, None, :]


def swiglu(h, w_gate, w_up, w_down):
    return (jax.nn.silu(h @ w_gate) * (h @ w_up)) @ w_down


def alibi_slopes():
    s = np.array([2.0 ** (-ALIBI_MAX_BIAS * (i + 1) / DIL_HEADS) for i in range(DIL_HEADS)],
                 dtype=np.float32)
    return jnp.asarray(s.reshape(N_DIL, DIL_HEADS_PER_GROUP))


def dilated_window_attention(q, k, v, slopes, window, dilation):
    bsz, seq, nh, e = q.shape
    n_back = window // dilation
    span = dilation * BLOCK
    s_pad = -(-seq // span) * span
    sub_len = s_pad // dilation
    nb = sub_len // BLOCK

    def to_sub(t):
        t = jnp.pad(t, ((0, 0), (0, s_pad - seq), (0, 0), (0, 0)))
        t = t.reshape(bsz, sub_len, dilation, nh, e).transpose(0, 2, 3, 1, 4)
        return t.reshape(bsz, dilation, nh, nb, BLOCK, e)

    def with_prev(t):
        prev = jnp.pad(t[:, :, :, :-1], ((0, 0), (0, 0), (0, 0), (1, 0), (0, 0), (0, 0)))
        return jnp.concatenate([prev, t], axis=4)

    qs = to_sub(q)
    kb, vb = with_prev(to_sub(k)), with_prev(to_sub(v))
    scores = jnp.einsum('bdhnqe,bdhnke->bdhnqk', qs, kb).astype(jnp.float32) / math.sqrt(e)
    qi = jnp.arange(BLOCK)[:, None]
    ki = jnp.arange(2 * BLOCK)[None, :]
    rel = BLOCK + qi - ki
    blk = jnp.arange(nb)[:, None, None]
    valid = (rel >= 0) & (rel <= n_back) & ((blk > 0) | (ki >= BLOCK))
    bias = -slopes.astype(jnp.float32)[:, None, None] * (rel * dilation).astype(jnp.float32)
    scores = jnp.where(valid, scores + bias[:, None], NEG_INF)
    lse = jax.nn.logsumexp(scores, axis=-1)
    p = jnp.exp(scores - lse[..., None])
    o = jnp.einsum('bdhnqk,bdhnke->bdhnqe', p.astype(v.dtype), vb)

    def from_sub(t):
        t = t.reshape(bsz, dilation, nh, sub_len, -1).transpose(0, 3, 1, 2, 4)
        return t.reshape(bsz, s_pad, nh, -1)[:, :seq]

    return from_sub(o), from_sub(lse[..., None])[..., 0]


def forgetting_attention(q, k, v, f_logit):
    bsz, seq, nh, e = q.shape
    nb = seq // BLOCK
    cum = jnp.cumsum(jax.nn.log_sigmoid(f_logit.astype(jnp.float32)), axis=1).transpose(0, 2, 1)
    kt = k.transpose(0, 2, 1, 3)
    vt = v.transpose(0, 2, 1, 3)
    q_blocks = q.reshape(bsz, nb, BLOCK, nh, e).transpose(1, 0, 3, 2, 4)
    cum_blocks = cum.reshape(bsz, nh, nb, BLOCK).transpose(2, 0, 1, 3)
    key_pos = jnp.arange(seq)
    scale = 1.0 / math.sqrt(e)

    def one_block(args):
        qb, cb, b = args
        s = jnp.einsum('bhqe,bhke->bhqk', qb, kt).astype(jnp.float32) * scale
        s = s + cb[..., None] - cum[:, :, None, :]
        q_pos = b * BLOCK + jnp.arange(BLOCK)
        s = jnp.where(key_pos[None, :] <= q_pos[:, None], s, NEG_INF)
        p = jax.nn.softmax(s, axis=-1)
        return jnp.einsum('bhqk,bhke->bhqe', p.astype(vt.dtype), vt)

    out = lax.map(one_block, (q_blocks, cum_blocks, jnp.arange(nb)))
    return out.transpose(1, 0, 3, 2, 4).reshape(bsz, seq, nh * e)


def hybrid_mixer(h, w_in, forget_bias, w_branch_a, w_branch_b, w_out):
    bsz, seq, _ = h.shape
    proj = h @ w_in
    offsets = [int(o) for o in np.cumsum(IN_SIZES)[:-1]]
    qa, ka, va, qb, kb, vb, f_logit, ga, gb = jnp.split(proj, offsets, axis=-1)

    qa = qa.reshape(bsz, seq, N_DIL, DIL_HEADS_PER_GROUP, HEAD_DIM)
    ka = ka.reshape(bsz, seq, N_DIL, DIL_HEADS_PER_GROUP, HEAD_DIM)
    va = va.reshape(bsz, seq, N_DIL, DIL_HEADS_PER_GROUP, HEAD_DIM)
    slopes = alibi_slopes()
    outs, lses = [], []
    for g, (window, dilation) in enumerate(DIL_GROUPS):
        o, lse = dilated_window_attention(qa[:, :, g], ka[:, :, g], va[:, :, g],
                                          slopes[g], window, dilation)
        outs.append(o)
        lses.append(lse)
    weights = jax.nn.softmax(jnp.stack(lses, axis=0), axis=0)
    y_a = jnp.sum(weights[..., None].astype(h.dtype) * jnp.stack(outs, axis=0), axis=0)
    y_a = y_a.reshape(bsz, seq, DIL_OUT) @ w_branch_a

    f_logit = f_logit + forget_bias
    y_b = forgetting_attention(qb.reshape(bsz, seq, FOX_HEADS, HEAD_DIM),
                               kb.reshape(bsz, seq, FOX_HEADS, HEAD_DIM),
                               vb.reshape(bsz, seq, FOX_HEADS, HEAD_DIM), f_logit)
    y_b = y_b @ w_branch_b

    merged = jax.nn.sigmoid(ga) * y_a + jax.nn.sigmoid(gb) * y_b
    return merged @ w_out


def _fwd_setup_inputs(seed: int = 0) -> dict:
    key = jax.random.key(seed)
    ks = jax.random.split(key, 20)
    f32 = jnp.float32

    def w(k, shape, fan_in):
        return jax.random.normal(k, shape, f32) * fan_in ** -0.5

    def gain(k, shape):
        return 1.0 + 0.02 * jax.random.normal(k, shape, f32)

    return {
        "x": jax.random.normal(ks[0], (BATCH, SEQ, D_MODEL), f32),
        "c": jax.random.normal(ks[1], (BATCH, D_MODEL), f32),
        "ada_w": w(ks[2], (DEPTH, D_MODEL, N_MOD * D_MODEL), D_MODEL),
        "ada_b": 0.02 * jax.random.normal(ks[3], (DEPTH, N_MOD * D_MODEL), f32),
        "norm_ffn1": gain(ks[4], (DEPTH, D_MODEL)),
        "ffn1_w_gate": w(ks[5], (DEPTH, D_MODEL, D_FF), D_MODEL),
        "ffn1_w_up": w(ks[6], (DEPTH, D_MODEL, D_FF), D_MODEL),
        "ffn1_w_down": w(ks[7], (DEPTH, D_FF, D_MODEL), D_FF),
        "norm_mix": gain(ks[8], (DEPTH, D_MODEL)),
        "w_in": w(ks[9], (DEPTH, D_MODEL, N_IN), D_MODEL),
        "forget_bias": jnp.linspace(1.0, 6.0, FOX_HEADS, dtype=f32)[None, :]
                       + 0.1 * jax.random.normal(ks[10], (DEPTH, FOX_HEADS), f32),
        "w_branch_a": w(ks[11], (DEPTH, DIL_OUT, D_MODEL), DIL_OUT),
        "w_branch_b": w(ks[12], (DEPTH, FOX_WIDTH, D_MODEL), FOX_WIDTH),
        "w_out": w(ks[13], (DEPTH, D_MODEL, D_MODEL), D_MODEL),
        "norm_ffn2": gain(ks[14], (DEPTH, D_MODEL)),
        "ffn2_w_gate": w(ks[15], (DEPTH, D_MODEL, D_FF), D_MODEL),
        "ffn2_w_up": w(ks[16], (DEPTH, D_MODEL, D_FF), D_MODEL),
        "ffn2_w_down": w(ks[17], (DEPTH, D_FF, D_MODEL), D_FF),
        "norm_final": gain(ks[18], (D_MODEL,)),
    }


def _fwd_reference(x, c, ada_w, ada_b, norm_ffn1, ffn1_w_gate, ffn1_w_up, ffn1_w_down,
              norm_mix, w_in, forget_bias, w_branch_a, w_branch_b, w_out,
              norm_ffn2, ffn2_w_gate, ffn2_w_up, ffn2_w_down, norm_final):
    c_act = jax.nn.silu(c)
    for layer in range(DEPTH):
        mod = c_act @ ada_w[layer] + ada_b[layer]
        (sh1, sc1, gt1, sh2, sc2, gt2, sh3, sc3, gt3) = jnp.split(mod, N_MOD, axis=-1)
        h = modulate(rms_norm(x, norm_ffn1[layer]), sh1, sc1)
        x = x + 0.5 * gt1[:, None, :] * swiglu(h, ffn1_w_gate[layer], ffn1_w_up[layer],
                                               ffn1_w_down[layer])
        h = modulate(rms_norm(x, norm_mix[layer]), sh2, sc2)
        x = x + gt2[:, None, :] * hybrid_mixer(h, w_in[layer], forget_bias[layer],
                                               w_branch_a[layer], w_branch_b[layer],
                                               w_out[layer])
        h = modulate(rms_norm(x, norm_ffn2[layer]), sh3, sc3)
        x = x + 0.5 * gt3[:, None, :] * swiglu(h, ffn2_w_gate[layer], ffn2_w_up[layer],
                                               ffn2_w_down[layer])
    return rms_norm(x, norm_final)


import jax as _jax
import jax.numpy as _jnp

TWIN_FORMAT = 'train_step'
FWD_PARAMS = ['x', 'c', 'ada_w', 'ada_b', 'norm_ffn1', 'ffn1_w_gate', 'ffn1_w_up', 'ffn1_w_down', 'norm_mix', 'w_in', 'forget_bias', 'w_branch_a', 'w_branch_b', 'w_out', 'norm_ffn2', 'ffn2_w_gate', 'ffn2_w_up', 'ffn2_w_down', 'norm_final']
TWIN_WEIGHTS = ['ada_w', 'ada_b', 'norm_ffn1', 'ffn1_w_gate', 'ffn1_w_up', 'ffn1_w_down', 'norm_mix', 'w_in', 'forget_bias', 'w_branch_a', 'w_branch_b', 'w_out', 'norm_ffn2', 'ffn2_w_gate', 'ffn2_w_up', 'ffn2_w_down', 'norm_final']
TWIN_DIFF_INPUT = 'x'
TWIN_INPUTS = ['x', 'c', 'ada_w', 'ada_b', 'norm_ffn1', 'ffn1_w_gate', 'ffn1_w_up', 'ffn1_w_down', 'norm_mix', 'w_in', 'forget_bias', 'w_branch_a', 'w_branch_b', 'w_out', 'norm_ffn2', 'ffn2_w_gate', 'ffn2_w_up', 'ffn2_w_down', 'norm_final', 'loss_target', 'm_ada_w', 'm_ada_b', 'm_norm_ffn1', 'm_ffn1_w_gate', 'm_ffn1_w_up', 'm_ffn1_w_down', 'm_norm_mix', 'm_w_in', 'm_forget_bias', 'm_w_branch_a', 'm_w_branch_b', 'm_w_out', 'm_norm_ffn2', 'm_ffn2_w_gate', 'm_ffn2_w_up', 'm_ffn2_w_down', 'm_norm_final', 'v_ada_w', 'v_ada_b', 'v_norm_ffn1', 'v_ffn1_w_gate', 'v_ffn1_w_up', 'v_ffn1_w_down', 'v_norm_mix', 'v_w_in', 'v_forget_bias', 'v_w_branch_a', 'v_w_branch_b', 'v_w_out', 'v_norm_ffn2', 'v_ffn2_w_gate', 'v_ffn2_w_up', 'v_ffn2_w_down', 'v_norm_final']
TWIN_OUTPUTS = ['loss', 'grad_x', 'grad_ada_w', 'grad_ada_b', 'grad_norm_ffn1', 'grad_ffn1_w_gate', 'grad_ffn1_w_up', 'grad_ffn1_w_down', 'grad_norm_mix', 'grad_w_in', 'grad_forget_bias', 'grad_w_branch_a', 'grad_w_branch_b', 'grad_w_out', 'grad_norm_ffn2', 'grad_ffn2_w_gate', 'grad_ffn2_w_up', 'grad_ffn2_w_down', 'grad_norm_final', 'delta_ada_w', 'delta_ada_b', 'delta_norm_ffn1', 'delta_ffn1_w_gate', 'delta_ffn1_w_up', 'delta_ffn1_w_down', 'delta_norm_mix', 'delta_w_in', 'delta_forget_bias', 'delta_w_branch_a', 'delta_w_branch_b', 'delta_w_out', 'delta_norm_ffn2', 'delta_ffn2_w_gate', 'delta_ffn2_w_up', 'delta_ffn2_w_down', 'delta_norm_final', 'new_m_ada_w', 'new_m_ada_b', 'new_m_norm_ffn1', 'new_m_ffn1_w_gate', 'new_m_ffn1_w_up', 'new_m_ffn1_w_down', 'new_m_norm_mix', 'new_m_w_in', 'new_m_forget_bias', 'new_m_w_branch_a', 'new_m_w_branch_b', 'new_m_w_out', 'new_m_norm_ffn2', 'new_m_ffn2_w_gate', 'new_m_ffn2_w_up', 'new_m_ffn2_w_down', 'new_m_norm_final', 'new_v_ada_w', 'new_v_ada_b', 'new_v_norm_ffn1', 'new_v_ffn1_w_gate', 'new_v_ffn1_w_up', 'new_v_ffn1_w_down', 'new_v_norm_mix', 'new_v_w_in', 'new_v_forget_bias', 'new_v_w_branch_a', 'new_v_w_branch_b', 'new_v_w_out', 'new_v_norm_ffn2', 'new_v_ffn2_w_gate', 'new_v_ffn2_w_up', 'new_v_ffn2_w_down', 'new_v_norm_final']
TWIN_LEAF_KINDS = {'loss': 'loss', 'grad_x': 'grad_x', 'grad_ada_w': 'grad_w', 'grad_ada_b': 'grad_w', 'grad_norm_ffn1': 'grad_w', 'grad_ffn1_w_gate': 'grad_w', 'grad_ffn1_w_up': 'grad_w', 'grad_ffn1_w_down': 'grad_w', 'grad_norm_mix': 'grad_w', 'grad_w_in': 'grad_w', 'grad_forget_bias': 'grad_w', 'grad_w_branch_a': 'grad_w', 'grad_w_branch_b': 'grad_w', 'grad_w_out': 'grad_w', 'grad_norm_ffn2': 'grad_w', 'grad_ffn2_w_gate': 'grad_w', 'grad_ffn2_w_up': 'grad_w', 'grad_ffn2_w_down': 'grad_w', 'grad_norm_final': 'grad_w', 'delta_ada_w': 'delta_w', 'delta_ada_b': 'delta_w', 'delta_norm_ffn1': 'delta_w', 'delta_ffn1_w_gate': 'delta_w', 'delta_ffn1_w_up': 'delta_w', 'delta_ffn1_w_down': 'delta_w', 'delta_norm_mix': 'delta_w', 'delta_w_in': 'delta_w', 'delta_forget_bias': 'delta_w', 'delta_w_branch_a': 'delta_w', 'delta_w_branch_b': 'delta_w', 'delta_w_out': 'delta_w', 'delta_norm_ffn2': 'delta_w', 'delta_ffn2_w_gate': 'delta_w', 'delta_ffn2_w_up': 'delta_w', 'delta_ffn2_w_down': 'delta_w', 'delta_norm_final': 'delta_w', 'new_m_ada_w': 'new_m', 'new_m_ada_b': 'new_m', 'new_m_norm_ffn1': 'new_m', 'new_m_ffn1_w_gate': 'new_m', 'new_m_ffn1_w_up': 'new_m', 'new_m_ffn1_w_down': 'new_m', 'new_m_norm_mix': 'new_m', 'new_m_w_in': 'new_m', 'new_m_forget_bias': 'new_m', 'new_m_w_branch_a': 'new_m', 'new_m_w_branch_b': 'new_m', 'new_m_w_out': 'new_m', 'new_m_norm_ffn2': 'new_m', 'new_m_ffn2_w_gate': 'new_m', 'new_m_ffn2_w_up': 'new_m', 'new_m_ffn2_w_down': 'new_m', 'new_m_norm_final': 'new_m', 'new_v_ada_w': 'new_v', 'new_v_ada_b': 'new_v', 'new_v_norm_ffn1': 'new_v', 'new_v_ffn1_w_gate': 'new_v', 'new_v_ffn1_w_up': 'new_v', 'new_v_ffn1_w_down': 'new_v', 'new_v_norm_mix': 'new_v', 'new_v_w_in': 'new_v', 'new_v_forget_bias': 'new_v', 'new_v_w_branch_a': 'new_v', 'new_v_w_branch_b': 'new_v', 'new_v_w_out': 'new_v', 'new_v_norm_ffn2': 'new_v', 'new_v_ffn2_w_gate': 'new_v', 'new_v_ffn2_w_up': 'new_v', 'new_v_ffn2_w_down': 'new_v', 'new_v_norm_final': 'new_v'}


def _forward(args):
    return _fwd_reference(*[args[k] for k in FWD_PARAMS])


def _output_shape():
    out = _jax.eval_shape(lambda: _forward(_fwd_setup_inputs(0)))
    return out.shape, out.dtype

N_MICROBATCH = 1
ADAM_LR = 0.001
ADAM_B1 = 0.9
ADAM_B2 = 0.999
ADAM_EPS = 1e-08
ADAM_WD = 0.01
ADAM_STEP = 10
PER_EXAMPLE_BATCH_AXIS = {'x': 0, 'c': 0, 'loss_target': 0}
SHARED_INPUTS = []
_WEIGHT_DTYPES = {'ada_w': _jnp.float32, 'ada_b': _jnp.float32, 'norm_ffn1': _jnp.float32, 'ffn1_w_gate': _jnp.float32, 'ffn1_w_up': _jnp.float32, 'ffn1_w_down': _jnp.float32, 'norm_mix': _jnp.float32, 'w_in': _jnp.float32, 'forget_bias': _jnp.float32, 'w_branch_a': _jnp.float32, 'w_branch_b': _jnp.float32, 'w_out': _jnp.float32, 'norm_ffn2': _jnp.float32, 'ffn2_w_gate': _jnp.float32, 'ffn2_w_up': _jnp.float32, 'ffn2_w_down': _jnp.float32, 'norm_final': _jnp.float32}
MOMENT_SCALE = {'ada_w': 7.174788e-02, 'ada_b': 1.208331e-01, 'norm_ffn1': 1.095594e-01, 'ffn1_w_gate': 5.230766e-02, 'ffn1_w_up': 5.059329e-02, 'ffn1_w_down': 8.407088e-02, 'norm_mix': 1.093013e-01, 'w_in': 5.594312e-02, 'forget_bias': 7.323215e-01, 'w_branch_a': 5.588806e-02, 'w_branch_b': 7.031502e-02, 'w_out': 8.876460e-02, 'norm_ffn2': 1.024248e-01, 'ffn2_w_gate': 4.647280e-02, 'ffn2_w_up': 4.490026e-02, 'ffn2_w_down': 7.464134e-02, 'norm_final': 6.441044e+01}


def _to_microbatches(a, axis):
    t = _jnp.moveaxis(a, axis, 0)
    t = t.reshape((N_MICROBATCH, t.shape[0] // N_MICROBATCH) + t.shape[1:])
    return _jnp.moveaxis(t, 1, axis + 1)


def setup_inputs(seed: int = 0) -> dict:
    inp = _fwd_setup_inputs(seed)
    key = _jax.random.fold_in(_jax.random.key(seed), 7919)
    shape, _ = _output_shape()
    out = dict(inp)
    out["loss_target"] = _jax.random.normal(_jax.random.fold_in(key, 0), shape, _jnp.float32)
    for i, name in enumerate(TWIN_WEIGHTS):
        w = inp[name].astype(_jnp.float32)
        if MOMENT_SCALE is None:
            s = _jnp.sqrt(_jnp.mean(_jnp.square(w)) + 1e-30)
        else:
            s = MOMENT_SCALE[name]
        km, kv = _jax.random.split(_jax.random.fold_in(key, i + 1))
        out[name] = w
        out["m_" + name] = s * _jax.random.normal(km, w.shape, _jnp.float32)
        out["v_" + name] = (s * s) * _jax.random.uniform(kv, w.shape, _jnp.float32, 0.5, 1.5)
    if N_MICROBATCH > 1:
        for name, axis in PER_EXAMPLE_BATCH_AXIS.items():
            out[name] = _to_microbatches(out[name], axis)
    return {'x': out['x'], 'c': out['c'], 'ada_w': out['ada_w'], 'ada_b': out['ada_b'], 'norm_ffn1': out['norm_ffn1'], 'ffn1_w_gate': out['ffn1_w_gate'], 'ffn1_w_up': out['ffn1_w_up'], 'ffn1_w_down': out['ffn1_w_down'], 'norm_mix': out['norm_mix'], 'w_in': out['w_in'], 'forget_bias': out['forget_bias'], 'w_branch_a': out['w_branch_a'], 'w_branch_b': out['w_branch_b'], 'w_out': out['w_out'], 'norm_ffn2': out['norm_ffn2'], 'ffn2_w_gate': out['ffn2_w_gate'], 'ffn2_w_up': out['ffn2_w_up'], 'ffn2_w_down': out['ffn2_w_down'], 'norm_final': out['norm_final'], 'loss_target': out['loss_target'], 'm_ada_w': out['m_ada_w'], 'm_ada_b': out['m_ada_b'], 'm_norm_ffn1': out['m_norm_ffn1'], 'm_ffn1_w_gate': out['m_ffn1_w_gate'], 'm_ffn1_w_up': out['m_ffn1_w_up'], 'm_ffn1_w_down': out['m_ffn1_w_down'], 'm_norm_mix': out['m_norm_mix'], 'm_w_in': out['m_w_in'], 'm_forget_bias': out['m_forget_bias'], 'm_w_branch_a': out['m_w_branch_a'], 'm_w_branch_b': out['m_w_branch_b'], 'm_w_out': out['m_w_out'], 'm_norm_ffn2': out['m_norm_ffn2'], 'm_ffn2_w_gate': out['m_ffn2_w_gate'], 'm_ffn2_w_up': out['m_ffn2_w_up'], 'm_ffn2_w_down': out['m_ffn2_w_down'], 'm_norm_final': out['m_norm_final'], 'v_ada_w': out['v_ada_w'], 'v_ada_b': out['v_ada_b'], 'v_norm_ffn1': out['v_norm_ffn1'], 'v_ffn1_w_gate': out['v_ffn1_w_gate'], 'v_ffn1_w_up': out['v_ffn1_w_up'], 'v_ffn1_w_down': out['v_ffn1_w_down'], 'v_norm_mix': out['v_norm_mix'], 'v_w_in': out['v_w_in'], 'v_forget_bias': out['v_forget_bias'], 'v_w_branch_a': out['v_w_branch_a'], 'v_w_branch_b': out['v_w_branch_b'], 'v_w_out': out['v_w_out'], 'v_norm_ffn2': out['v_norm_ffn2'], 'v_ffn2_w_gate': out['v_ffn2_w_gate'], 'v_ffn2_w_up': out['v_ffn2_w_up'], 'v_ffn2_w_down': out['v_ffn2_w_down'], 'v_norm_final': out['v_norm_final']}


def _loss(weights, diff, rest, loss_target):
    with _jax.named_scope("forward"):
        args = {**rest, TWIN_DIFF_INPUT: diff, **{k: w.astype(_WEIGHT_DTYPES[k]) for k, w in weights.items()}}
        y = _forward(args)
    with _jax.named_scope("loss_head"):
        err = _jnp.square(y.astype(_jnp.float32) - loss_target)
        return 0.5 * _jnp.sum(_jnp.mean(err, axis=-1)) if err.ndim else 0.5 * err


def _adamw(w, g, m, v):
    m = ADAM_B1 * m + (1.0 - ADAM_B1) * g
    v = ADAM_B2 * v + (1.0 - ADAM_B2) * _jnp.square(g)
    m_hat = m / (1.0 - ADAM_B1 ** ADAM_STEP)
    v_hat = v / (1.0 - ADAM_B2 ** ADAM_STEP)
    delta = -ADAM_LR * (m_hat / (_jnp.sqrt(v_hat) + ADAM_EPS) + ADAM_WD * w)
    return delta, m, v


def reference(x, c, ada_w, ada_b, norm_ffn1, ffn1_w_gate, ffn1_w_up, ffn1_w_down, norm_mix, w_in, forget_bias, w_branch_a, w_branch_b, w_out, norm_ffn2, ffn2_w_gate, ffn2_w_up, ffn2_w_down, norm_final, loss_target, m_ada_w, m_ada_b, m_norm_ffn1, m_ffn1_w_gate, m_ffn1_w_up, m_ffn1_w_down, m_norm_mix, m_w_in, m_forget_bias, m_w_branch_a, m_w_branch_b, m_w_out, m_norm_ffn2, m_ffn2_w_gate, m_ffn2_w_up, m_ffn2_w_down, m_norm_final, v_ada_w, v_ada_b, v_norm_ffn1, v_ffn1_w_gate, v_ffn1_w_up, v_ffn1_w_down, v_norm_mix, v_w_in, v_forget_bias, v_w_branch_a, v_w_branch_b, v_w_out, v_norm_ffn2, v_ffn2_w_gate, v_ffn2_w_up, v_ffn2_w_down, v_norm_final):
    given = dict(x=x, c=c, ada_w=ada_w, ada_b=ada_b, norm_ffn1=norm_ffn1, ffn1_w_gate=ffn1_w_gate, ffn1_w_up=ffn1_w_up, ffn1_w_down=ffn1_w_down, norm_mix=norm_mix, w_in=w_in, forget_bias=forget_bias, w_branch_a=w_branch_a, w_branch_b=w_branch_b, w_out=w_out, norm_ffn2=norm_ffn2, ffn2_w_gate=ffn2_w_gate, ffn2_w_up=ffn2_w_up, ffn2_w_down=ffn2_w_down, norm_final=norm_final, loss_target=loss_target, m_ada_w=m_ada_w, m_ada_b=m_ada_b, m_norm_ffn1=m_norm_ffn1, m_ffn1_w_gate=m_ffn1_w_gate, m_ffn1_w_up=m_ffn1_w_up, m_ffn1_w_down=m_ffn1_w_down, m_norm_mix=m_norm_mix, m_w_in=m_w_in, m_forget_bias=m_forget_bias, m_w_branch_a=m_w_branch_a, m_w_branch_b=m_w_branch_b, m_w_out=m_w_out, m_norm_ffn2=m_norm_ffn2, m_ffn2_w_gate=m_ffn2_w_gate, m_ffn2_w_up=m_ffn2_w_up, m_ffn2_w_down=m_ffn2_w_down, m_norm_final=m_norm_final, v_ada_w=v_ada_w, v_ada_b=v_ada_b, v_norm_ffn1=v_norm_ffn1, v_ffn1_w_gate=v_ffn1_w_gate, v_ffn1_w_up=v_ffn1_w_up, v_ffn1_w_down=v_ffn1_w_down, v_norm_mix=v_norm_mix, v_w_in=v_w_in, v_forget_bias=v_forget_bias, v_w_branch_a=v_w_branch_a, v_w_branch_b=v_w_branch_b, v_w_out=v_w_out, v_norm_ffn2=v_norm_ffn2, v_ffn2_w_gate=v_ffn2_w_gate, v_ffn2_w_up=v_ffn2_w_up, v_ffn2_w_down=v_ffn2_w_down, v_norm_final=v_norm_final)
    weights = {n: given[n] for n in TWIN_WEIGHTS}
    shared = {n: given[n] for n in SHARED_INPUTS}
    per_example = {n: given[n] for n in ['x', 'c']}
    grad_fn = _jax.value_and_grad(_loss, argnums=(0, 1))

    def one_microbatch(ex, loss_target):
        ex = dict(ex)
        diff = ex.pop(TWIN_DIFF_INPUT)
        return grad_fn(weights, diff, {**shared, **ex}, loss_target)

    if N_MICROBATCH == 1:
        loss, (grad_w, grad_x) = one_microbatch(per_example, given["loss_target"])
    else:
        def body(carry, xs):
            loss_sum, grad_sum = carry
            l_k, (gw_k, gx_k) = one_microbatch(xs[0], xs[1])
            with _jax.named_scope("update"):
                return (loss_sum + l_k, _jax.tree.map(_jnp.add, grad_sum, gw_k)), gx_k

        init = (_jnp.zeros((), _jnp.float32), _jax.tree.map(_jnp.zeros_like, weights))
        (loss, grad_w), grad_x = _jax.lax.scan(body, init, (per_example, given["loss_target"]))
    with _jax.named_scope("update"):
        delta_w, new_m, new_v = {}, {}, {}
        for n in TWIN_WEIGHTS:
            delta_w[n], new_m[n], new_v[n] = _adamw(weights[n], grad_w[n], given["m_" + n], given["v_" + n])
    return (loss, grad_x, *[grad_w[n] for n in TWIN_WEIGHTS], *[delta_w[n] for n in TWIN_WEIGHTS],
            *[new_m[n] for n in TWIN_WEIGHTS], *[new_v[n] for n in TWIN_WEIGHTS])
```

```python
import functools
import math

import jax
import jax.numpy as jnp
from jax import lax
from jax.experimental import pallas as pl
from jax.experimental.pallas import tpu as pltpu

F32 = jnp.float32
BF16 = jnp.bfloat16
MESH = pl.DeviceIdType.MESH
ANY = pl.BlockSpec(memory_space=pl.ANY)
VMEM_SPEC = pl.BlockSpec(memory_space=pltpu.VMEM)

N_DEV = 8
HEAD_DIM = 64
BLOCK = 128
DIL_GROUPS = ((128, 1), (512, 4), (2048, 16))
N_DIL = len(DIL_GROUPS)
DIL_HPG = 4
DIL_GW = DIL_HPG * HEAD_DIM
DIL_W = N_DIL * DIL_GW
FOX_HEADS = 8
FOX_W = FOX_HEADS * HEAD_DIM
N_MOD = 9
RMS_EPS = 1e-6
ALIBI_MAX_BIAS = 8.0
NEG_INF = -1e30
ADAM_LR, ADAM_B1, ADAM_B2, ADAM_EPS, ADAM_WD, ADAM_STEP = 0.001, 0.9, 0.999, 1e-08, 0.01, 10
V7X_VMEM_LIMIT = 52 * 1024 * 1024
LANES = 128
ROW_ALIGN = 16
SCALE = 1.0 / math.sqrt(HEAD_DIM)


def _div(dim, target, quantum):
    best = None
    for t in range(quantum, min(dim, target) + 1, quantum):
        if dim % t == 0:
            best = t
    return best or dim


def _params(sem=None):
    return pltpu.CompilerParams(dimension_semantics=sem, vmem_limit_bytes=V7X_VMEM_LIMIT)


def _sigmoid(x):
    return 1.0 / (1.0 + jnp.exp(-x))


def _position():
    x, y, c = lax.axis_index("x"), lax.axis_index("y"), lax.axis_index("c")
    return x, y, c


def _small_allgather(v, name):
    rows, cols = v.shape

    def body(v_ref, out_ref, send_sems, recv_sems):
        x, y, c = _position()
        me = 4 * x + 2 * y + c
        out_ref[me] = v_ref[...]

        def peer(k):
            return (1 - x if k & 4 else x, 1 - y if k & 2 else y, 1 - c if k & 1 else c)

        def copy(k, slot):
            return pltpu.make_async_remote_copy(
                src_ref=v_ref, dst_ref=out_ref.at[slot], send_sem=send_sems.at[k - 1], recv_sem=recv_sems.at[k - 1],
                device_id=peer(k), device_id_type=MESH)

        sends = [copy(k, me) for k in range(1, N_DEV)]
        for cp in sends:
            cp.start()
        for k in range(1, N_DEV):
            px, py, pc = peer(k)
            copy(k, 4 * px + 2 * py + pc).wait_recv()
        for cp in sends:
            cp.wait_send()

    return pl.pallas_call(
        body, name=name,
        out_shape=jax.ShapeDtypeStruct((N_DEV, rows, cols), v.dtype),
        in_specs=[VMEM_SPEC], out_specs=VMEM_SPEC,
        scratch_shapes=[pltpu.SemaphoreType.DMA((N_DEV - 1,)), pltpu.SemaphoreType.DMA((N_DEV - 1,))],
    )(v)


def _weight_allgather(p):
    rows, cols = p.shape

    def body(p_ref, out_ref, send_sems, recv_sems, local_sem):
        x, y, c = _position()
        me, sibling = (x, y, c), (x, y, 1 - c)
        chips = [(1 - x, y), (x, 1 - y), (1 - x, 1 - y)]

        def slot(px, py, pc):
            return out_ref.at[4 * px + 2 * py + pc]

        def copy(k, block, to, src=None):
            return pltpu.make_async_remote_copy(
                src_ref=slot(*block) if src is None else src, dst_ref=slot(*block),
                send_sem=send_sems.at[k], recv_sem=recv_sems.at[k], device_id=to, device_id_type=MESH)

        mine = pltpu.make_async_copy(p_ref, slot(*me), local_sem)
        mine.start()
        first = [copy(0, me, sibling, src=p_ref)]
        first += [copy(1 + j, me, (*chip, c), src=p_ref) for j, chip in enumerate(chips)]
        for cp in first:
            cp.start()
        passed = [copy(4 + j, (*chip, c), sibling) for j, chip in enumerate(chips)]
        for j, chip in enumerate(chips):
            copy(1 + j, (*chip, c), me).wait_recv()
            passed[j].start()
        copy(0, sibling, me).wait_recv()
        for j, chip in enumerate(chips):
            copy(4 + j, (*chip, 1 - c), me).wait_recv()
        for cp in first + passed:
            cp.wait_send()
        mine.wait()

    return pl.pallas_call(
        body, name="weight_allgather",
        out_shape=jax.ShapeDtypeStruct((N_DEV, rows, cols), p.dtype),
        in_specs=[ANY], out_specs=ANY,
        scratch_shapes=[pltpu.SemaphoreType.DMA((7,)), pltpu.SemaphoreType.DMA((7,)), pltpu.SemaphoreType.DMA],
    )(p)


def _grad_exchange_sibling(g):
    _, rows, cols = g.shape

    def body(g_ref, out_ref, send_sems, recv_sems):
        x, y, c = _position()
        sibling = (x, y, 1 - c)

        def copy(q):
            px, py = q >> 1, q & 1
            return pltpu.make_async_remote_copy(
                src_ref=g_ref.at[4 * px + 2 * py + (1 - c)], dst_ref=out_ref.at[q],
                send_sem=send_sems.at[q], recv_sem=recv_sems.at[q], device_id=sibling, device_id_type=MESH)

        copies = [copy(q) for q in range(4)]
        for cp in copies:
            cp.start()
        for cp in copies:
            cp.wait_recv()
        for cp in copies:
            cp.wait_send()

    return pl.pallas_call(
        body, name="grad_exchange_sibling",
        out_shape=jax.ShapeDtypeStruct((4, rows, cols), g.dtype),
        in_specs=[ANY], out_specs=ANY,
        scratch_shapes=[pltpu.SemaphoreType.DMA((4,)), pltpu.SemaphoreType.DMA((4,))],
    )(g)


def _grad_exchange_chips(s):
    _, rows, cols = s.shape

    def body(s_ref, out_ref, send_sems, recv_sems):
        x, y, c = _position()
        chips = [(1 - x, y), (x, 1 - y), (1 - x, 1 - y)]

        def copy(k):
            return pltpu.make_async_remote_copy(
                src_ref=s_ref.at[k], dst_ref=out_ref.at[k], send_sem=send_sems.at[k], recv_sem=recv_sems.at[k],
                device_id=(*chips[k], c), device_id_type=MESH)

        copies = [copy(k) for k in range(3)]
        for cp in copies:
            cp.start()
        for cp in copies:
            cp.wait_recv()
        for cp in copies:
            cp.wait_send()

    return pl.pallas_call(
        body, name="grad_exchange_chips",
        out_shape=jax.ShapeDtypeStruct((3, rows, cols), s.dtype),
        in_specs=[ANY], out_specs=ANY,
        scratch_shapes=[pltpu.SemaphoreType.DMA((3,)), pltpu.SemaphoreType.DMA((3,))],
    )(s)


def _chip_partial_sums(g, recv_sib, jj, qq):
    _, rows, cols = g.shape
    tr = _div(rows, 512, ROW_ALIGN)

    def body(jj_ref, qq_ref, g_ref, r_ref, o_ref):
        o_ref[...] = (g_ref[...] + r_ref[...]).astype(o_ref.dtype)

    return pl.pallas_call(
        body, name="chip_partial_sums",
        out_shape=jax.ShapeDtypeStruct((3, rows, cols), BF16),
        grid_spec=pltpu.PrefetchScalarGridSpec(
            num_scalar_prefetch=2, grid=(3, rows // tr),
            in_specs=[pl.BlockSpec((None, tr, cols), lambda k, i, jj, qq: (jj[k], i, 0)),
                      pl.BlockSpec((None, tr, cols), lambda k, i, jj, qq: (qq[k], i, 0))],
            out_specs=pl.BlockSpec((None, tr, cols), lambda k, i, jj, qq: (k, i, 0))),
        compiler_params=_params(("arbitrary", "arbitrary")),
    )(jj, qq, g, recv_sib)


def _own_partial_sum(g, recv_sib, jj, qq):
    _, rows, cols = g.shape
    tr = _div(rows, 512, ROW_ALIGN)

    def body(jj_ref, qq_ref, g_ref, r_ref, o_ref):
        o_ref[...] = g_ref[...] + r_ref[...]

    return pl.pallas_call(
        body, name="own_partial_sum",
        out_shape=jax.ShapeDtypeStruct((rows, cols), F32),
        grid_spec=pltpu.PrefetchScalarGridSpec(
            num_scalar_prefetch=2, grid=(rows // tr,),
            in_specs=[pl.BlockSpec((None, tr, cols), lambda i, jj, qq: (jj[0], i, 0)),
                      pl.BlockSpec((None, tr, cols), lambda i, jj, qq: (qq[0], i, 0))],
            out_specs=pl.BlockSpec((tr, cols), lambda i, jj, qq: (i, 0))),
        compiler_params=_params(("arbitrary",)),
    )(jj, qq, g, recv_sib)


def _final_grad_sum(own, recv):
    rows, cols = own.shape
    tr = _div(rows, 512, ROW_ALIGN)

    def body(o_ref, r_ref, out_ref):
        out_ref[...] = ((o_ref[...] + r_ref[0].astype(F32)) + r_ref[1].astype(F32)) + r_ref[2].astype(F32)

    return pl.pallas_call(
        body, name="final_grad_sum",
        out_shape=jax.ShapeDtypeStruct((rows, cols), F32),
        grid=(rows // tr,),
        in_specs=[pl.BlockSpec((tr, cols), lambda i: (i, 0)), pl.BlockSpec((3, tr, cols), lambda i: (0, i, 0))],
        out_specs=pl.BlockSpec((tr, cols), lambda i: (i, 0)),
        compiler_params=_params(("arbitrary",)),
    )(own, recv)


def _reduce_scatter(g):
    x, y, c = _position()
    chips = [(1 - x, y), (x, 1 - y), (1 - x, 1 - y)]
    jj = jnp.stack([4 * px + 2 * py + c for px, py in chips]).astype(jnp.int32)
    qq = jnp.stack([2 * px + py for px, py in chips]).astype(jnp.int32)
    jme = jnp.reshape(4 * x + 2 * y + c, (1,)).astype(jnp.int32)
    qme = jnp.reshape(2 * x + y, (1,)).astype(jnp.int32)
    recv_sib = _grad_exchange_sibling(g)
    sums = _chip_partial_sums(g, recv_sib, jj, qq)
    own = _own_partial_sum(g, recv_sib, jme, qme)
    recv = _grad_exchange_chips(sums)
    return _final_grad_sum(own, recv)


def _matmul(name, form, prods, M, N, K, tm, tn, tk, out_dtypes, extras=(), epilogue=None, rows_per_example=None):
    nk = K // tk
    n_acc = len(prods)
    flat = [ab for group in prods for ab in group]
    dims = {"nn": (((1,), (0,)), ((), ())), "nt": (((1,), (1,)), ((), ())), "tn": (((0,), (0,)), ((), ()))}[form]
    if form == "tn":
        a_spec = pl.BlockSpec((tk, tm), lambda i, j, k: (k, i))
    else:
        a_spec = pl.BlockSpec((tm, tk), lambda i, j, k: (i, k))
    if form == "nt":
        b_spec = pl.BlockSpec((tn, tk), lambda i, j, k: (j, k))
    else:
        b_spec = pl.BlockSpec((tk, tn), lambda i, j, k: (k, j))
    in_specs, operands = [], []
    for a, b in flat:
        in_specs += [a_spec, b_spec]
        operands += [a, b]
    for arr, kind, off in extras:
        if kind == "tile":
            assert off % tn == 0
            in_specs.append(pl.BlockSpec((tm, tn), functools.partial(lambda i, j, k, o: (i, j + o), o=off // tn)))
        else:
            tiles = rows_per_example // tm
            in_specs.append(pl.BlockSpec((None, 1, tn), functools.partial(lambda i, j, k, t: (i // t, 0, j), t=tiles)))
        operands.append(arr)
    n_in, n_out = len(operands), len(out_dtypes)

    def body(*refs):
        in_refs, out_refs, acc_refs = refs[:n_in], refs[n_in:n_in + n_out], refs[n_in + n_out:]
        k = pl.program_id(2)
        partials, p = [], 0
        for group in prods:
            tot = None
            for _ in group:
                d = lax.dot_general(in_refs[2 * p][...], in_refs[2 * p + 1][...], dims, preferred_element_type=F32)
                tot = d if tot is None else tot + d
                p += 1
            partials.append(tot)

        def finish(accs):
            ex = [r[...] for r in in_refs[2 * len(flat):]]
            outs = epilogue(accs, ex) if epilogue is not None else accs
            for r, o in zip(out_refs, outs):
                r[...] = o.astype(r.dtype)

        if nk == 1:
            finish(partials)
        else:
            @pl.when(k == 0)
            def _():
                for r, v in zip(acc_refs, partials):
                    r[...] = v

            @pl.when(k > 0)
            def _():
                for r, v in zip(acc_refs, partials):
                    r[...] += v

            @pl.when(k == nk - 1)
            def _():
                finish([r[...] for r in acc_refs])

    outs = pl.pallas_call(
        body, name=name,
        out_shape=[jax.ShapeDtypeStruct((M, N), dt) for dt in out_dtypes],
        grid=(M // tm, N // tn, nk),
        in_specs=in_specs,
        out_specs=[pl.BlockSpec((tm, tn), lambda i, j, k: (i, j)) for _ in out_dtypes],
        scratch_shapes=[pltpu.VMEM((tm, tn), F32) for _ in range(n_acc)] if nk > 1 else [],
        compiler_params=_params(("parallel", "parallel", "arbitrary")),
    )(*operands)
    return outs


def _rowwise(name, fn, T, tm, ins, outs, rows_per_example):
    tiles = rows_per_example // tm
    n_ex = T // rows_per_example
    in_specs, operands = [], []
    for arr, kind, arg in ins:
        if kind == "row":
            if arg is None:
                in_specs.append(pl.BlockSpec((tm, arr.shape[1]), lambda i: (i, 0)))
            else:
                in_specs.append(pl.BlockSpec((tm, arg[0]), functools.partial(lambda i, cb: (i, cb), cb=arg[1])))
        elif kind == "bvec":
            in_specs.append(pl.BlockSpec((None, 1, arr.shape[2]), lambda i: (i // tiles, 0, 0)))
        else:
            in_specs.append(pl.BlockSpec((1, arr.shape[1]), lambda i: (0, 0)))
        operands.append(arr)
    out_shape, out_specs = [], []
    for kind, cols, dt in outs:
        if kind == "row":
            out_shape.append(jax.ShapeDtypeStruct((T, cols), dt))
            out_specs.append(pl.BlockSpec((tm, cols), lambda i: (i, 0)))
        else:
            out_shape.append(jax.ShapeDtypeStruct((n_ex, 1, cols), F32))
            out_specs.append(pl.BlockSpec((None, 1, cols), lambda i: (i // tiles, 0, 0)))
    n_in = len(operands)

    def body(*refs):
        i = pl.program_id(0)
        vals = fn(*[r[...] for r in refs[:n_in]])
        for (kind, _, _), r, v in zip(outs, refs[n_in:], vals):
            if kind == "row":
                r[...] = v.astype(r.dtype)
            else:
                @pl.when(i % tiles == 0)
                def _():
                    r[...] = jnp.zeros_like(r)

                r[...] += v

    return pl.pallas_call(
        body, name=name, out_shape=out_shape, grid=(T // tm,), in_specs=in_specs, out_specs=out_specs,
        compiler_params=_params(("arbitrary",)),
    )(*operands)


def _colsum(v):
    return jnp.sum(v, axis=0, keepdims=True)


def _rms_parts(x):
    rstd = lax.rsqrt(jnp.mean(x * x, axis=-1, keepdims=True) + RMS_EPS)
    return x * rstd, rstd


def _normmod(name, x, g, sc, sh, S):
    T, D = x.shape

    def fn(xv, gv, scv, shv):
        xhat, _ = _rms_parts(xv)
        return [(xhat * gv) * (1.0 + scv) + shv]

    return _rowwise(name, fn, T, _div(S, 512, 8), [(x, "row", None), (g, "vec", None), (sc, "bvec", None), (sh, "bvec", None)],
                    [("row", D, BF16)], S)[0]


def _normmod_bwd(name, x, g, sc, dh, dres, S):
    T, D = x.shape

    def fn(xv, gv, scv, dhv, drv):
        xhat, rstd = _rms_parts(xv)
        n = xhat * gv
        dn = dhv * (1.0 + scv)
        dxh = dn * gv
        dx = rstd * (dxh - xhat * jnp.mean(dxh * xhat, axis=-1, keepdims=True))
        return [drv + dx, _colsum(dhv), _colsum(dhv * n), _colsum(dn * xhat)]

    return _rowwise(name, fn, T, _div(S, 256, 8),
                    [(x, "row", None), (g, "vec", None), (sc, "bvec", None), (dh, "row", None), (dres, "row", None)],
                    [("row", D, F32), ("bacc", D, F32), ("bacc", D, F32), ("bacc", D, F32)], S)


def _gate_grad(name, dx, y, gt, coeff, S):
    T, D = dx.shape

    def fn(dxv, yv, gtv):
        return [coeff * gtv * dxv, _colsum(coeff * dxv * yv.astype(F32))]

    return _rowwise(name, fn, T, _div(S, 512, 8), [(dx, "row", None), (y, "row", None), (gt, "bvec", None)],
                    [("row", D, BF16), ("bacc", D, F32)], S)


def _ffn_forward(tag, x, g, sh, sc, gt, wgT, wuT, wd, S):
    T, D = x.shape
    F = wd.shape[0]
    h = _normmod(f"{tag}_normmod", x, g, sc, sh, S)

    def gateup(accs, ex):
        a, u = accs
        return [a, u, a * _sigmoid(a) * u]

    a, u, s = _matmul(f"{tag}_gateup", "nt", [[(h, wgT)], [(h, wuT)]], T, F, D, _div(T, 1024, 8), _div(F, 256, LANES), D,
                      [BF16, BF16, BF16], epilogue=gateup)

    def down(accs, ex):
        xv, gtv = ex
        return [xv + 0.5 * gtv * accs[0], accs[0]]

    tmd = _div(S, 512, 8)
    x_new, y = _matmul(f"{tag}_down", "nn", [[(s, wd)]], T, D, F, tmd, _div(D, 512, LANES), F, [F32, BF16],
                       extras=[(x, "tile", 0), (gt, "brow", 0)], epilogue=down, rows_per_example=S)
    return x_new, (x, h, a, u, s, y)


def _ffn_backward(tag, dx_out, saved, g, sc, gt, wgT, wuT, wd, S):
    x, h, a, u, s, y = saved
    T, D = x.shape
    F = wd.shape[0]
    dy, dgt = _gate_grad(f"{tag}_gate_grad", dx_out, y, gt, 0.5, S)

    def act_grad(accs, ex):
        ds = accs[0]
        av, uv = ex[0].astype(F32), ex[1].astype(F32)
        sg = _sigmoid(av)
        return [ds * uv * (sg * (1.0 + av * (1.0 - sg))), ds * (av * sg)]

    da, du = _matmul(f"{tag}_act_grad", "nt", [[(dy, wd)]], T, F, D, _div(T, 1024, 8), _div(F, 256, LANES), D, [BF16, BF16],
                     extras=[(a, "tile", 0), (u, "tile", 0)], epilogue=act_grad)
    tmw = _div(F, 1408, LANES)
    tkw = _div(T, 1024, LANES)
    dwd = _matmul(f"{tag}_dw_down", "tn", [[(s, dy)]], F, D, T, tmw, D, tkw, [F32])[0]
    dwgT = _matmul(f"{tag}_dw_gate", "tn", [[(da, h)]], F, D, T, tmw, D, tkw, [F32])[0]
    dwuT = _matmul(f"{tag}_dw_up", "tn", [[(du, h)]], F, D, T, tmw, D, tkw, [F32])[0]
    dh = _matmul(f"{tag}_dh", "nn", [[(da, wgT), (du, wuT)]], T, D, F, _div(T, 512, 8), _div(D, 512, LANES), F, [F32])[0]
    dx_in, dsh, dsc, dg = _normmod_bwd(f"{tag}_normmod_bwd", x, g, sc, dh, dx_out, S)
    return dx_in, (dsh, dsc, dgt, dg), (dwgT, dwuT, dwd)


def _loss_head(x, tgt, g, S):
    T, D = x.shape

    def fn(xv, tv, gv):
        xhat, rstd = _rms_parts(xv)
        e = xhat * gv - tv
        loss = jnp.broadcast_to(0.5 / D * jnp.sum(_colsum(e * e), axis=1, keepdims=True), (1, LANES))
        dy = e * (1.0 / D)
        dxh = dy * gv
        dx = rstd * (dxh - xhat * jnp.mean(dxh * xhat, axis=-1, keepdims=True))
        return [dx, loss, _colsum(dy * xhat)]

    return _rowwise("loss_head", fn, T, _div(S, 512, 8), [(x, "row", None), (tgt, "row", None), (g, "vec", None)],
                    [("row", D, F32), ("bacc", LANES, F32), ("bacc", D, F32)], S)


def _cumsum(v):
    B, S, _ = v.shape

    def body(x_ref, o_ref, carry):
        i = pl.program_id(1)

        @pl.when(i == 0)
        def _():
            carry[...] = jnp.zeros_like(carry)

        r = lax.broadcasted_iota(jnp.int32, (BLOCK, BLOCK), 0)
        c = lax.broadcasted_iota(jnp.int32, (BLOCK, BLOCK), 1)
        tri = (c <= r).astype(F32)
        cum = jnp.dot(tri, x_ref[...], precision=lax.Precision.HIGHEST, preferred_element_type=F32) + carry[0:1, :]
        o_ref[...] = cum
        carry[...] = jnp.broadcast_to(cum[BLOCK - 1:BLOCK, :], carry.shape)

    return pl.pallas_call(
        body, name="cumsum", out_shape=jax.ShapeDtypeStruct(v.shape, F32), grid=(B, S // BLOCK),
        in_specs=[pl.BlockSpec((None, BLOCK, LANES), lambda b, i: (b, i, 0))],
        out_specs=pl.BlockSpec((None, BLOCK, LANES), lambda b, i: (b, i, 0)),
        scratch_shapes=[pltpu.VMEM((8, LANES), F32)],
        compiler_params=_params(("arbitrary", "arbitrary")),
    )(v)


def _fox_scores(q, k, cq, ck, qpos, kpos):
    s = lax.dot_general(q, k, (((1,), (1,)), ((), ())), preferred_element_type=F32) * SCALE + cq - ck
    return jnp.where(kpos <= qpos, s, NEG_INF)


def _fox_positions(qi, kj, tq, tk):
    qpos = qi * tq + lax.broadcasted_iota(jnp.int32, (tq, tk), 0)
    kpos = kj * tk + lax.broadcasted_iota(jnp.int32, (tq, tk), 1)
    return qpos, kpos


def _fox_forward(pm3, cum, cumT, qcol, kcol, vcol, tq):
    B, S, _ = pm3.shape
    nq = S // tq

    def body(q_ref, k_ref, v_ref, cq_ref, ck_ref, o_ref, o32_ref, lse_ref, m_sc, l_sc, acc_sc):
        qi, kj = pl.program_id(1), pl.program_id(2)

        @pl.when(kj == 0)
        def _():
            m_sc[...] = jnp.full_like(m_sc, NEG_INF)
            l_sc[...] = jnp.zeros_like(l_sc)
            acc_sc[...] = jnp.zeros_like(acc_sc)

        @pl.when(kj <= qi)
        def _():
            qpos, kpos = _fox_positions(qi, kj, tq, tq)
            for h in range(FOX_HEADS):
                hs = slice(HEAD_DIM * h, HEAD_DIM * (h + 1))
                s = _fox_scores(q_ref[:, hs], k_ref[:, hs], cq_ref[:, h:h + 1], ck_ref[h:h + 1, :], qpos, kpos)
                m_prev = m_sc[h]
                m_new = jnp.maximum(m_prev, jnp.max(s, axis=-1, keepdims=True))
                alpha = jnp.exp(m_prev - m_new)
                p = jnp.exp(s - m_new)
                l_sc[h] = alpha * l_sc[h] + jnp.sum(p, axis=-1, keepdims=True)
                acc_sc[:, hs] = alpha * acc_sc[:, hs] + lax.dot_general(
                    p.astype(BF16), v_ref[:, hs], (((1,), (0,)), ((), ())), preferred_element_type=F32)
                m_sc[h] = m_new

        @pl.when(kj == nq - 1)
        def _():
            lse_ref[...] = jnp.zeros_like(lse_ref)
            for h in range(FOX_HEADS):
                hs = slice(HEAD_DIM * h, HEAD_DIM * (h + 1))
                oh = acc_sc[:, hs] / l_sc[h]
                o_ref[:, hs] = oh.astype(o_ref.dtype)
                o32_ref[:, hs] = oh
                lse_ref[:, h:h + 1] = m_sc[h] + jnp.log(l_sc[h])

    return pl.pallas_call(
        body, name="fox_forward",
        out_shape=[jax.ShapeDtypeStruct((B, S, FOX_W), BF16), jax.ShapeDtypeStruct((B, S, FOX_W), F32),
                   jax.ShapeDtypeStruct((B, S, LANES), F32)],
        grid=(B, nq, nq),
        in_specs=[pl.BlockSpec((None, tq, FOX_W), lambda b, i, j: (b, i, qcol)),
                  pl.BlockSpec((None, tq, FOX_W), lambda b, i, j: (b, jnp.minimum(i, j), kcol)),
                  pl.BlockSpec((None, tq, FOX_W), lambda b, i, j: (b, jnp.minimum(i, j), vcol)),
                  pl.BlockSpec((None, tq, LANES), lambda b, i, j: (b, i, 0)),
                  pl.BlockSpec((None, 8, tq), lambda b, i, j: (b, 0, jnp.minimum(i, j)))],
        out_specs=[pl.BlockSpec((None, tq, FOX_W), lambda b, i, j: (b, i, 0)),
                   pl.BlockSpec((None, tq, FOX_W), lambda b, i, j: (b, i, 0)),
                   pl.BlockSpec((None, tq, LANES), lambda b, i, j: (b, i, 0))],
        scratch_shapes=[pltpu.VMEM((FOX_HEADS, tq, 1), F32), pltpu.VMEM((FOX_HEADS, tq, 1), F32), pltpu.VMEM((tq, FOX_W), F32)],
        compiler_params=_params(("parallel", "parallel", "arbitrary")),
    )(pm3, pm3, pm3, cum, cumT)


def _fox_dq(pm3, do, delta, lse, cum, cumT, qcol, kcol, vcol, tq):
    B, S, _ = pm3.shape
    nq = S // tq

    def body(q_ref, k_ref, v_ref, do_ref, dl_ref, lse_ref, cq_ref, ck_ref, dq_ref, dc_ref, acc_sc, dc_sc):
        qi, kj = pl.program_id(1), pl.program_id(2)

        @pl.when(kj == 0)
        def _():
            acc_sc[...] = jnp.zeros_like(acc_sc)
            dc_sc[...] = jnp.zeros_like(dc_sc)

        @pl.when(kj <= qi)
        def _():
            qpos, kpos = _fox_positions(qi, kj, tq, tq)
            for h in range(FOX_HEADS):
                hs = slice(HEAD_DIM * h, HEAD_DIM * (h + 1))
                s = _fox_scores(q_ref[:, hs], k_ref[:, hs], cq_ref[:, h:h + 1], ck_ref[h:h + 1, :], qpos, kpos)
                p = jnp.exp(s - lse_ref[:, h:h + 1])
                doh = do_ref[:, hs]
                dp = lax.dot_general(doh, v_ref[:, hs], (((1,), (1,)), ((), ())), preferred_element_type=F32)
                ds = p * (dp - dl_ref[:, h:h + 1])
                dc_sc[h] += jnp.sum(ds, axis=-1, keepdims=True)
                acc_sc[:, hs] += lax.dot_general(ds.astype(BF16), k_ref[:, hs], (((1,), (0,)), ((), ())),
                                                 preferred_element_type=F32)

        @pl.when(kj == nq - 1)
        def _():
            dq_ref[...] = (acc_sc[...] * SCALE).astype(dq_ref.dtype)
            dc_ref[...] = jnp.zeros_like(dc_ref)
            for h in range(FOX_HEADS):
                dc_ref[:, h:h + 1] = dc_sc[h]

    qspec = pl.BlockSpec((None, tq, FOX_W), lambda b, i, j: (b, i, 0))
    lspec = pl.BlockSpec((None, tq, LANES), lambda b, i, j: (b, i, 0))
    return pl.pallas_call(
        body, name="fox_dq",
        out_shape=[jax.ShapeDtypeStruct((B, S, FOX_W), BF16), jax.ShapeDtypeStruct((B, S, LANES), F32)],
        grid=(B, nq, nq),
        in_specs=[pl.BlockSpec((None, tq, FOX_W), lambda b, i, j: (b, i, qcol)),
                  pl.BlockSpec((None, tq, FOX_W), lambda b, i, j: (b, jnp.minimum(i, j), kcol)),
                  pl.BlockSpec((None, tq, FOX_W), lambda b, i, j: (b, jnp.minimum(i, j), vcol)),
                  qspec, lspec, lspec, lspec,
                  pl.BlockSpec((None, 8, tq), lambda b, i, j: (b, 0, jnp.minimum(i, j)))],
        out_specs=[qspec, lspec],
        scratch_shapes=[pltpu.VMEM((tq, FOX_W), F32), pltpu.VMEM((FOX_HEADS, tq, 1), F32)],
        compiler_params=_params(("parallel", "parallel", "arbitrary")),
    )(pm3, pm3, pm3, do, delta, lse, cum, cumT)


def _fox_dkv(pm3, do, delta, lse, cum, cumT, qcol, kcol, vcol, tq):
    B, S, _ = pm3.shape
    nq = S // tq

    def body(q_ref, k_ref, v_ref, do_ref, dl_ref, lse_ref, cq_ref, ck_ref, dk_ref, dv_ref, dc_ref, dk_sc, dv_sc, dc_sc):
        kj, qi = pl.program_id(1), pl.program_id(2)

        @pl.when(qi == 0)
        def _():
            dk_sc[...] = jnp.zeros_like(dk_sc)
            dv_sc[...] = jnp.zeros_like(dv_sc)
            dc_sc[...] = jnp.zeros_like(dc_sc)

        @pl.when(qi >= kj)
        def _():
            qpos, kpos = _fox_positions(qi, kj, tq, tq)
            for h in range(FOX_HEADS):
                hs = slice(HEAD_DIM * h, HEAD_DIM * (h + 1))
                qh = q_ref[:, hs]
                s = _fox_scores(qh, k_ref[:, hs], cq_ref[:, h:h + 1], ck_ref[h:h + 1, :], qpos, kpos)
                p = jnp.exp(s - lse_ref[:, h:h + 1])
                doh = do_ref[:, hs]
                dp = lax.dot_general(doh, v_ref[:, hs], (((1,), (1,)), ((), ())), preferred_element_type=F32)
                ds = p * (dp - dl_ref[:, h:h + 1])
                dv_sc[:, hs] += lax.dot_general(p.astype(BF16), doh, (((0,), (0,)), ((), ())), preferred_element_type=F32)
                dk_sc[:, hs] += lax.dot_general(ds.astype(BF16), qh, (((0,), (0,)), ((), ())), preferred_element_type=F32)
                dc_sc[h:h + 1, :] -= jnp.sum(ds, axis=0, keepdims=True)

        @pl.when(qi == nq - 1)
        def _():
            dk_ref[...] = (dk_sc[...] * SCALE).astype(dk_ref.dtype)
            dv_ref[...] = dv_sc[...].astype(dv_ref.dtype)
            dc_ref[...] = dc_sc[...]

    def qside(width):
        return pl.BlockSpec((None, tq, width), lambda b, j, i: (b, jnp.maximum(i, j), 0))

    kspec = pl.BlockSpec((None, tq, FOX_W), lambda b, j, i: (b, j, 0))
    return pl.pallas_call(
        body, name="fox_dkv",
        out_shape=[jax.ShapeDtypeStruct((B, S, FOX_W), BF16), jax.ShapeDtypeStruct((B, S, FOX_W), BF16),
                   jax.ShapeDtypeStruct((B, 8, S), F32)],
        grid=(B, nq, nq),
        in_specs=[pl.BlockSpec((None, tq, FOX_W), lambda b, j, i: (b, jnp.maximum(i, j), qcol)),
                  pl.BlockSpec((None, tq, FOX_W), lambda b, j, i: (b, j, kcol)),
                  pl.BlockSpec((None, tq, FOX_W), lambda b, j, i: (b, j, vcol)),
                  qside(FOX_W), qside(LANES), qside(LANES), qside(LANES),
                  pl.BlockSpec((None, 8, tq), lambda b, j, i: (b, 0, j))],
        out_specs=[kspec, kspec, pl.BlockSpec((None, 8, tq), lambda b, j, i: (b, 0, j))],
        scratch_shapes=[pltpu.VMEM((tq, FOX_W), F32), pltpu.VMEM((tq, FOX_W), F32), pltpu.VMEM((8, tq), F32)],
        compiler_params=_params(("parallel", "parallel", "arbitrary")),
    )(pm3, pm3, pm3, do, delta, lse, cum, cumT)


def _fox_delta(do, o32, T, S):
    def fn(dov, ov):
        prod = dov.astype(F32) * ov
        lane = lax.broadcasted_iota(jnp.int32, (dov.shape[0], LANES), 1)
        delta = jnp.zeros((dov.shape[0], LANES), F32)
        for h in range(FOX_HEADS):
            hs = slice(HEAD_DIM * h, HEAD_DIM * (h + 1))
            delta = jnp.where(lane == h, jnp.sum(prod[:, hs], axis=-1, keepdims=True), delta)
        return [delta]

    return _rowwise("fox_delta", fn, T, _div(S, 512, 8), [(do, "row", None), (o32, "row", None)], [("row", LANES, F32)], S)[0]


def _alibi_slope(group, head):
    return 2.0 ** (-ALIBI_MAX_BIAS * (group * DIL_HPG + head + 1) / (N_DIL * DIL_HPG))


def _dil_tiles(q, k_cur, k_prev, slope, dilation, has_prev):
    qi = lax.broadcasted_iota(jnp.int32, (BLOCK, BLOCK), 0)
    ki = lax.broadcasted_iota(jnp.int32, (BLOCK, BLOCK), 1)
    rel = (qi - ki).astype(F32)
    nt = (((1,), (1,)), ((), ()))
    s_cur = lax.dot_general(q, k_cur, nt, preferred_element_type=F32) * SCALE - (slope * dilation) * rel
    s_cur = jnp.where(ki <= qi, s_cur, NEG_INF)
    s_prev = lax.dot_general(q, k_prev, nt, preferred_element_type=F32) * SCALE - (slope * dilation) * (rel + BLOCK)
    s_prev = jnp.where((ki >= qi) & has_prev, s_prev, NEG_INF)
    return s_cur, s_prev


def _dil_forward(group, pmv, nmb, qa_blk, B, S):
    _, dilation = DIL_GROUPS[group]
    sub = S // dilation
    nb = sub // BLOCK
    qb, kb, vb = qa_blk + group, qa_blk + N_DIL + group, qa_blk + 2 * N_DIL + group

    def body(q_ref, kc_ref, kp_ref, vc_ref, vp_ref, o_ref, lse_ref):
        has_prev = pl.program_id(2) > 0
        lse_ref[...] = jnp.zeros_like(lse_ref)
        for h in range(DIL_HPG):
            hs = slice(HEAD_DIM * h, HEAD_DIM * (h + 1))
            s_cur, s_prev = _dil_tiles(q_ref[:, hs], kc_ref[:, hs], kp_ref[:, hs], _alibi_slope(group, h), dilation, has_prev)
            m = jnp.maximum(jnp.max(s_cur, axis=-1, keepdims=True), jnp.max(s_prev, axis=-1, keepdims=True))
            p_cur, p_prev = jnp.exp(s_cur - m), jnp.exp(s_prev - m)
            l = jnp.sum(p_cur, axis=-1, keepdims=True) + jnp.sum(p_prev, axis=-1, keepdims=True)
            nn = (((1,), (0,)), ((), ()))
            o = (lax.dot_general(p_cur.astype(BF16), vc_ref[:, hs], nn, preferred_element_type=F32)
                 + lax.dot_general(p_prev.astype(BF16), vp_ref[:, hs], nn, preferred_element_type=F32))
            o_ref[:, hs] = o / l
            lse_ref[:, h:h + 1] = m + jnp.log(l)

    def cur(col):
        return pl.BlockSpec((None, BLOCK, DIL_GW), lambda b, r, n: (b, n, r * nmb + col))

    def prev(col):
        return pl.BlockSpec((None, BLOCK, DIL_GW), lambda b, r, n: (b, jnp.maximum(n - 1, 0), r * nmb + col))

    return pl.pallas_call(
        body, name=f"dil_forward_{group}",
        out_shape=[jax.ShapeDtypeStruct((B, sub, dilation * DIL_GW), F32), jax.ShapeDtypeStruct((B, sub, dilation * LANES), F32)],
        grid=(B, dilation, nb),
        in_specs=[cur(qb), cur(kb), prev(kb), cur(vb), prev(vb)],
        out_specs=[pl.BlockSpec((None, BLOCK, DIL_GW), lambda b, r, n: (b, n, r)),
                   pl.BlockSpec((None, BLOCK, LANES), lambda b, r, n: (b, n, r))],
        compiler_params=_params(("parallel", "parallel", "arbitrary")),
    )(pmv, pmv, pmv, pmv, pmv)


def _dil_combine(os_, lses, T, S):
    def fn(o0, o1, o2, l0, l1, l2):
        m = jnp.maximum(jnp.maximum(l0, l1), l2)
        e0, e1, e2 = jnp.exp(l0 - m), jnp.exp(l1 - m), jnp.exp(l2 - m)
        tot = e0 + e1 + e2
        w0, w1, w2 = e0 / tot, e1 / tot, e2 / tot
        parts = []
        for h in range(DIL_HPG):
            hs = slice(HEAD_DIM * h, HEAD_DIM * (h + 1))
            parts.append(w0[:, h:h + 1] * o0[:, hs] + w1[:, h:h + 1] * o1[:, hs] + w2[:, h:h + 1] * o2[:, hs])
        return [jnp.concatenate(parts, axis=1), m + jnp.log(tot)]

    ins = [(a, "row", None) for a in os_] + [(a, "row", None) for a in lses]
    return _rowwise("dil_combine", fn, T, _div(S, 512, 8), ins, [("row", DIL_GW, BF16), ("row", LANES, F32)], S)


def _dil_delta(dy, y, T, S):
    def fn(dyv, yv):
        prod = dyv * yv.astype(F32)
        lane = lax.broadcasted_iota(jnp.int32, (dyv.shape[0], LANES), 1)
        delta = jnp.zeros((dyv.shape[0], LANES), F32)
        for h in range(DIL_HPG):
            hs = slice(HEAD_DIM * h, HEAD_DIM * (h + 1))
            delta = jnp.where(lane == h, jnp.sum(prod[:, hs], axis=-1, keepdims=True), delta)
        return [delta, dyv]

    return _rowwise("dil_delta", fn, T, _div(S, 512, 8), [(dy, "row", None), (y, "row", None)],
                    [("row", LANES, F32), ("row", DIL_GW, BF16)], S)


def _dil_dq(group, pmv, nmb, qa_blk, Lv, dyv, deltav, B, S):
    _, dilation = DIL_GROUPS[group]
    sub = S // dilation
    nb = sub // BLOCK
    qb, kb, vb = qa_blk + group, qa_blk + N_DIL + group, qa_blk + 2 * N_DIL + group

    def body(q_ref, kc_ref, kp_ref, vc_ref, vp_ref, L_ref, dy_ref, dl_ref, dq_ref):
        has_prev = pl.program_id(2) > 0
        nt = (((1,), (1,)), ((), ()))
        nn = (((1,), (0,)), ((), ()))
        for h in range(DIL_HPG):
            hs = slice(HEAD_DIM * h, HEAD_DIM * (h + 1))
            s_cur, s_prev = _dil_tiles(q_ref[:, hs], kc_ref[:, hs], kp_ref[:, hs], _alibi_slope(group, h), dilation, has_prev)
            L, delta, dyh = L_ref[:, h:h + 1], dl_ref[:, h:h + 1], dy_ref[:, hs]
            ds_cur = jnp.exp(s_cur - L) * (lax.dot_general(dyh, vc_ref[:, hs], nt, preferred_element_type=F32) - delta)
            ds_prev = jnp.exp(s_prev - L) * (lax.dot_general(dyh, vp_ref[:, hs], nt, preferred_element_type=F32) - delta)
            dq = (lax.dot_general(ds_cur.astype(BF16), kc_ref[:, hs], nn, preferred_element_type=F32)
                  + lax.dot_general(ds_prev.astype(BF16), kp_ref[:, hs], nn, preferred_element_type=F32))
            dq_ref[:, hs] = (dq * SCALE).astype(dq_ref.dtype)

    def cur(col):
        return pl.BlockSpec((None, BLOCK, DIL_GW), lambda b, r, n: (b, n, r * nmb + col))

    def prev(col):
        return pl.BlockSpec((None, BLOCK, DIL_GW), lambda b, r, n: (b, jnp.maximum(n - 1, 0), r * nmb + col))

    own = pl.BlockSpec((None, BLOCK, DIL_GW), lambda b, r, n: (b, n, r))
    own128 = pl.BlockSpec((None, BLOCK, LANES), lambda b, r, n: (b, n, r))
    return pl.pallas_call(
        body, name=f"dil_dq_{group}",
        out_shape=jax.ShapeDtypeStruct((B, sub, dilation * DIL_GW), BF16),
        grid=(B, dilation, nb),
        in_specs=[cur(qb), cur(kb), prev(kb), cur(vb), prev(vb), own128, own, own128],
        out_specs=own,
        compiler_params=_params(("parallel", "parallel", "arbitrary")),
    )(pmv, pmv, pmv, pmv, pmv, Lv, dyv, deltav)


def _dil_dkv(group, pmv, nmb, qa_blk, Lv, dyv, deltav, B, S):
    _, dilation = DIL_GROUPS[group]
    sub = S // dilation
    nb = sub // BLOCK
    qb, kb, vb = qa_blk + group, qa_blk + N_DIL + group, qa_blk + 2 * N_DIL + group

    def body(k_ref, v_ref, q0_ref, q1_ref, L0_ref, L1_ref, dy0_ref, dy1_ref, dl0_ref, dl1_ref, dk_ref, dv_ref):
        has_next = pl.program_id(2) < nb - 1
        qi = lax.broadcasted_iota(jnp.int32, (BLOCK, BLOCK), 0)
        ki = lax.broadcasted_iota(jnp.int32, (BLOCK, BLOCK), 1)
        rel = (qi - ki).astype(F32)
        nt = (((1,), (1,)), ((), ()))
        tn = (((0,), (0,)), ((), ()))
        for h in range(DIL_HPG):
            hs = slice(HEAD_DIM * h, HEAD_DIM * (h + 1))
            bias = _alibi_slope(group, h) * dilation
            kh, vh, q0, q1 = k_ref[:, hs], v_ref[:, hs], q0_ref[:, hs], q1_ref[:, hs]
            s0 = lax.dot_general(q0, kh, nt, preferred_element_type=F32) * SCALE - bias * rel
            s0 = jnp.where(ki <= qi, s0, NEG_INF)
            s1 = lax.dot_general(q1, kh, nt, preferred_element_type=F32) * SCALE - bias * (rel + BLOCK)
            s1 = jnp.where((ki >= qi) & has_next, s1, NEG_INF)
            p0 = jnp.exp(s0 - L0_ref[:, h:h + 1])
            p1 = jnp.exp(s1 - L1_ref[:, h:h + 1])
            dy0, dy1 = dy0_ref[:, hs], dy1_ref[:, hs]
            ds0 = p0 * (lax.dot_general(dy0, vh, nt, preferred_element_type=F32) - dl0_ref[:, h:h + 1])
            ds1 = p1 * (lax.dot_general(dy1, vh, nt, preferred_element_type=F32) - dl1_ref[:, h:h + 1])
            dv = (lax.dot_general(p0.astype(BF16), dy0, tn, preferred_element_type=F32)
                  + lax.dot_general(p1.astype(BF16), dy1, tn, preferred_element_type=F32))
            dk = (lax.dot_general(ds0.astype(BF16), q0, tn, preferred_element_type=F32)
                  + lax.dot_general(ds1.astype(BF16), q1, tn, preferred_element_type=F32))
            dv_ref[:, hs] = dv.astype(dv_ref.dtype)
            dk_ref[:, hs] = (dk * SCALE).astype(dk_ref.dtype)

    def cur(col):
        return pl.BlockSpec((None, BLOCK, DIL_GW), lambda b, r, n: (b, n, r * nmb + col))

    def nxt(col):
        return pl.BlockSpec((None, BLOCK, DIL_GW), lambda b, r, n: (b, jnp.minimum(n + 1, nb - 1), r * nmb + col))

    own = pl.BlockSpec((None, BLOCK, DIL_GW), lambda b, r, n: (b, n, r))
    own_next = pl.BlockSpec((None, BLOCK, DIL_GW), lambda b, r, n: (b, jnp.minimum(n + 1, nb - 1), r))
    own128 = pl.BlockSpec((None, BLOCK, LANES), lambda b, r, n: (b, n, r))
    own128_next = pl.BlockSpec((None, BLOCK, LANES), lambda b, r, n: (b, jnp.minimum(n + 1, nb - 1), r))
    shape = jax.ShapeDtypeStruct((B, sub, dilation * DIL_GW), BF16)
    return pl.pallas_call(
        body, name=f"dil_dkv_{group}",
        out_shape=[shape, shape],
        grid=(B, dilation, nb),
        in_specs=[cur(kb), cur(vb), cur(qb), nxt(qb), own128, own128_next, own, own_next, own128, own128_next],
        out_specs=[own, own],
        compiler_params=_params(("parallel", "parallel", "arbitrary")),
    )(pmv, pmv, pmv, pmv, Lv, Lv, dyv, dyv, deltav, deltav)


def _ada_forward(c_all, w, b):
    n, D = c_all.shape
    cl = w.shape[1]

    def body(c_ref, w_ref, b_ref, o_ref, ca_ref):
        cv = c_ref[...]
        ca = (cv * _sigmoid(cv)).astype(BF16)
        ca_ref[...] = ca
        o_ref[...] = jnp.dot(ca, w_ref[...].astype(BF16), preferred_element_type=F32) + b_ref[...]

    return pl.pallas_call(
        body, name="ada_forward",
        out_shape=[jax.ShapeDtypeStruct((n, cl), F32), jax.ShapeDtypeStruct((n, D), BF16)],
        compiler_params=_params(),
    )(c_all, w, b)


def _ada_backward(ca, dmod_cols, dmod_all):
    n, D = ca.shape
    cl = dmod_cols.shape[1]

    def body(ca_ref, dc_ref, da_ref, gw_ref, gb_ref):
        gw_ref[...] = lax.dot_general(ca_ref[...], dc_ref[...].astype(BF16), (((0,), (0,)), ((), ())), preferred_element_type=F32)
        gb_ref[...] = _colsum(da_ref[...])

    return pl.pallas_call(
        body, name="ada_backward",
        out_shape=[jax.ShapeDtypeStruct((D, cl), F32), jax.ShapeDtypeStruct((1, dmod_all.shape[1]), F32)],
        compiler_params=_params(),
    )(ca, dmod_cols, dmod_all)


def _sum_devices(v):
    def body(v_ref, o_ref):
        tot = v_ref[0]
        for k in range(1, N_DEV):
            tot = tot + v_ref[k]
        o_ref[...] = tot

    return pl.pallas_call(body, name="sum_devices", out_shape=jax.ShapeDtypeStruct(v.shape[1:], F32))(v)


def _adamw(name, w, g, m, v):
    rows, cols = w.shape
    tr = _div(rows, 256, 8)

    def body(w_ref, g_ref, m_ref, v_ref, d_ref, nm_ref, nv_ref):
        gv = g_ref[...]
        nm = ADAM_B1 * m_ref[...] + (1.0 - ADAM_B1) * gv
        nv = ADAM_B2 * v_ref[...] + (1.0 - ADAM_B2) * (gv * gv)
        m_hat = nm / (1.0 - ADAM_B1 ** ADAM_STEP)
        v_hat = nv / (1.0 - ADAM_B2 ** ADAM_STEP)
        d_ref[...] = -ADAM_LR * (m_hat / (jnp.sqrt(v_hat) + ADAM_EPS) + ADAM_WD * w_ref[...])
        nm_ref[...] = nm
        nv_ref[...] = nv

    spec = pl.BlockSpec((tr, cols), lambda i: (i, 0))
    shape = jax.ShapeDtypeStruct((rows, cols), F32)
    return pl.pallas_call(
        body, name=name, out_shape=[shape, shape, shape], grid=(rows // tr,),
        in_specs=[spec, spec, spec, spec], out_specs=[spec, spec, spec],
        compiler_params=_params(("arbitrary",)),
    )(w, g, m, v)


def _pad_rows(a, rows):
    return a if a.shape[0] == rows else jnp.pad(a, ((0, rows - a.shape[0]), (0, 0)))


class _Packed:
    def __init__(self, kind, local_shape, D):
        self.kind, self.local_shape, self.D = kind, local_shape, D
        r, c = local_shape
        self.rows = {"T": c, "N": r, "F": r * c // D}[kind]
        self.rows_pad = -(-self.rows // ROW_ALIGN) * ROW_ALIGN

    def pack_local(self, w):
        if self.kind == "T":
            w = w.T
        elif self.kind == "F":
            w = w.reshape(self.rows, self.D)
        return _pad_rows(w, self.rows_pad)

    def full(self, gathered):
        g = gathered[:, :self.rows]
        if self.kind == "F":
            r, c = self.local_shape
            return g.reshape(N_DEV, r, c).transpose(1, 0, 2).reshape(r, N_DEV * c)
        return g.reshape(N_DEV * self.rows, self.D)

    def pack_grad(self, gfull):
        if self.kind == "F":
            r, c = self.local_shape
            g = gfull.reshape(r, N_DEV, c).transpose(1, 0, 2).reshape(N_DEV, self.rows, self.D)
        else:
            g = gfull.reshape(N_DEV, self.rows, self.D)
        if self.rows_pad != self.rows:
            g = jnp.pad(g, ((0, 0), (0, self.rows_pad - self.rows), (0, 0)))
        return g

    def unpack_local(self, g):
        g = g[:self.rows]
        if self.kind == "T":
            return g.T
        if self.kind == "F":
            return g.reshape(self.local_shape)
        return g


BIG = ["ffn1_w_gate", "ffn1_w_up", "ffn1_w_down", "w_in", "w_branch_a", "w_branch_b", "w_out",
       "ffn2_w_gate", "ffn2_w_up", "ffn2_w_down"]
BIG_KIND = {"ffn1_w_gate": "T", "ffn1_w_up": "T", "ffn1_w_down": "N", "w_in": "T", "w_branch_a": "F", "w_branch_b": "F",
            "w_out": "N", "ffn2_w_gate": "T", "ffn2_w_up": "T", "ffn2_w_down": "N"}
SMALL = ["ada_b", "norm_ffn1", "norm_mix", "forget_bias", "norm_ffn2", "norm_final"]


def kernel(x, c, ada_w, ada_b, norm_ffn1, ffn1_w_gate, ffn1_w_up, ffn1_w_down, norm_mix, w_in, forget_bias, w_branch_a, w_branch_b, w_out, norm_ffn2, ffn2_w_gate, ffn2_w_up, ffn2_w_down, norm_final, loss_target, m_ada_w, m_ada_b, m_norm_ffn1, m_ffn1_w_gate, m_ffn1_w_up, m_ffn1_w_down, m_norm_mix, m_w_in, m_forget_bias, m_w_branch_a, m_w_branch_b, m_w_out, m_norm_ffn2, m_ffn2_w_gate, m_ffn2_w_up, m_ffn2_w_down, m_norm_final, v_ada_w, v_ada_b, v_norm_ffn1, v_ffn1_w_gate, v_ffn1_w_up, v_ffn1_w_down, v_norm_mix, v_w_in, v_forget_bias, v_w_branch_a, v_w_branch_b, v_w_out, v_norm_ffn2, v_ffn2_w_gate, v_ffn2_w_up, v_ffn2_w_down, v_norm_final):
    args = dict(locals())
    B, S, D = x.shape
    T = B * S
    cl = ada_w.shape[2]
    n_in = w_in.shape[2] * N_DEV
    nm = 2 * D + 3 * FOX_W + 3 * DIL_W
    nmp = -(-nm // 512) * 512
    GA, GB, QB, QA = 0, D, 2 * D, 2 * D + 3 * FOX_W
    xpos, ypos, cpos = _position()
    me = 4 * xpos + 2 * ypos + cpos

    packs = {n: _Packed(BIG_KIND[n], args[n].shape[1:], D) for n in BIG}
    offs, r = {}, 0
    for n in BIG:
        offs[n] = r
        r += packs[n].rows_pad
    p_local = jnp.concatenate([packs[n].pack_local(args[n][0]).astype(BF16) for n in BIG], axis=0)
    gathered = _weight_allgather(p_local)
    W = {n: packs[n].full(gathered[:, offs[n]:offs[n] + packs[n].rows_pad]) for n in BIG}
    winT = W["w_in"]
    o_f = 3 * DIL_W + 3 * FOX_W
    wmT = jnp.concatenate([winT[o_f + 8:], winT[3 * DIL_W:o_f], winT[:3 * DIL_W], jnp.zeros((nmp - nm, D), BF16)], axis=0)
    wfT = jnp.concatenate([winT[o_f:o_f + 8], jnp.zeros((LANES - 8, D), BF16)], axis=0)

    c_all = _small_allgather(c, "gather_c").reshape(N_DEV * B, D)
    b_cols = lax.dynamic_slice(ada_b, (0, me * cl), (1, cl))
    mod_cols, c_act = _ada_forward(c_all, ada_w[0], b_cols)
    mod_all = _small_allgather(mod_cols, "gather_mod").transpose(1, 0, 2).reshape(N_DEV * B, N_MOD * D)
    mod = lax.dynamic_slice(mod_all, (me * B, 0), (B, N_MOD * D)).reshape(B, N_MOD, 1, D)
    sh1, sc1, gt1, sh2, sc2, gt2, sh3, sc3, gt3 = [mod[:, i] for i in range(N_MOD)]

    x0 = x.reshape(T, D)
    x1, saved1 = _ffn_forward("ffn1", x0, norm_ffn1, sh1, sc1, gt1, W["ffn1_w_gate"], W["ffn1_w_up"], W["ffn1_w_down"], S)

    h2 = _normmod("mix_normmod", x1, norm_mix, sc2, sh2, S)
    tm1k = _div(T, 1024, 8)
    pm = _matmul("mix_proj", "nt", [[(h2, wmT)]], T, nmp, D, tm1k, 512, D, [BF16])[0]
    fraw = _matmul("mix_proj_f", "nt", [[(h2, wfT)]], T, LANES, D, tm1k, LANES, D, [F32])[0]
    fb = jnp.pad(forget_bias, ((0, 0), (0, LANES - FOX_HEADS)))

    def forget_fn(fr, fbv):
        fl = fr + fbv
        lane = lax.broadcasted_iota(jnp.int32, fl.shape, 1)
        ls = jnp.minimum(fl, 0.0) - jnp.log(1.0 + jnp.exp(-jnp.abs(fl)))
        return [jnp.where(lane < FOX_HEADS, ls, 0.0), fl]

    tms = _div(S, 512, 8)
    logsig, flog = _rowwise("forget_gate", forget_fn, T, tms, [(fraw, "row", None), (fb, "vec", None)],
                            [("row", LANES, F32), ("row", LANES, F32)], S)
    cum = _cumsum(logsig.reshape(B, S, LANES))
    cumT = cum[:, :, :8].transpose(0, 2, 1)
    pm3 = pm.reshape(B, S, nmp)
    tq = _div(S, 512, LANES)
    qcol, kcol, vcol = QB // FOX_W, QB // FOX_W + 1, QB // FOX_W + 2
    o_b, o_b32, lse_b = _fox_forward(pm3, cum, cumT, qcol, kcol, vcol, tq)
    y_b = o_b.reshape(T, FOX_W)

    nmb, qa_blk = nmp // DIL_GW, QA // DIL_GW
    pmvs = [pm.reshape(B, S // d, d * nmp) for _, d in DIL_GROUPS]
    dil_o, dil_lse = [], []
    for g in range(N_DIL):
        o_g, lse_g = _dil_forward(g, pmvs[g], nmb, qa_blk, B, S)
        dil_o.append(o_g.reshape(T, DIL_GW))
        dil_lse.append(lse_g.reshape(T, LANES))
    y_a, L_a = _dil_combine(dil_o, dil_lse, T, S)

    wa, wb, wout = W["w_branch_a"], W["w_branch_b"], W["w_out"]
    tnd = _div(D, 512, LANES)
    yap = _matmul("mix_branch_a", "nn", [[(y_a, wa)]], T, D, DIL_GW, tm1k, tnd, DIL_GW, [BF16])[0]

    def merge(accs, ex):
        yapv, gav, gbv = ex
        ybp = accs[0]
        return [ybp, _sigmoid(gav.astype(F32)) * yapv.astype(F32) + _sigmoid(gbv.astype(F32)) * ybp]

    ybp, merged = _matmul("mix_branch_b", "nn", [[(y_b, wb)]], T, D, FOX_W, tm1k, tnd, FOX_W, [BF16, BF16],
                          extras=[(yap, "tile", 0), (pm, "tile", GA), (pm, "tile", GB)], epilogue=merge)

    def out_proj(accs, ex):
        xv, gtv = ex
        return [xv + gtv * accs[0], accs[0]]

    x2, ymix = _matmul("mix_out", "nn", [[(merged, wout)]], T, D, D, tms, tnd, D, [F32, BF16],
                       extras=[(x1, "tile", 0), (gt2, "brow", 0)], epilogue=out_proj, rows_per_example=S)

    x3, saved3 = _ffn_forward("ffn2", x2, norm_ffn2, sh3, sc3, gt3, W["ffn2_w_gate"], W["ffn2_w_up"], W["ffn2_w_down"], S)

    dx3, loss_b, dg_final = _loss_head(x3, loss_target.reshape(T, D), norm_final.reshape(1, D), S)
    dx2, (dsh3, dsc3, dgt3, dg3), (dwg2, dwu2, dwd2) = _ffn_backward(
        "ffn2", dx3, saved3, norm_ffn2, sc3, gt3, W["ffn2_w_gate"], W["ffn2_w_up"], W["ffn2_w_down"], S)

    dym, dgt2 = _gate_grad("mix_gate_grad", dx2, ymix, gt2, 1.0, S)
    tkw = _div(T, 1024, LANES)
    dwout = _matmul("mix_dw_out", "tn", [[(merged, dym)]], D, D, T, _div(D, 1024, LANES), D, tkw, [F32])[0]

    def merge_grad(accs, ex):
        gav, gbv, yapv, ybpv = [e.astype(F32) for e in ex]
        dm = accs[0]
        sga, sgb = _sigmoid(gav), _sigmoid(gbv)
        return [dm * sga, dm * sgb, dm * yapv * sga * (1.0 - sga), dm * ybpv * sgb * (1.0 - sgb)]

    dyap, dybp, dga, dgb = _matmul("mix_merge_grad", "nt", [[(dym, wout)]], T, D, D, tm1k, tnd, D, [BF16] * 4,
                                   extras=[(pm, "tile", GA), (pm, "tile", GB), (yap, "tile", 0), (ybp, "tile", 0)],
                                   epilogue=merge_grad)
    dwa = _matmul("mix_dw_a", "tn", [[(y_a, dyap)]], DIL_GW, D, T, DIL_GW, D, tkw, [F32])[0]
    dwb = _matmul("mix_dw_b", "tn", [[(y_b, dybp)]], FOX_W, D, T, FOX_W, D, tkw, [F32])[0]
    dy_a = _matmul("mix_dy_a", "nt", [[(dyap, wa)]], T, DIL_GW, D, tm1k, DIL_GW, D, [F32])[0]
    dy_b = _matmul("mix_dy_b", "nt", [[(dybp, wb)]], T, FOX_W, D, tm1k, FOX_W, D, [BF16])[0]

    do3 = dy_b.reshape(B, S, FOX_W)
    delta_b = _fox_delta(dy_b, o_b32.reshape(T, FOX_W), T, S).reshape(B, S, LANES)
    dq_b, dcum_q = _fox_dq(pm3, do3, delta_b, lse_b, cum, cumT, qcol, kcol, vcol, tq)
    dk_b, dv_b, dcumT = _fox_dkv(pm3, do3, delta_b, lse_b, cum, cumT, qcol, kcol, vcol, tq)

    delta_a, dy_a16 = _dil_delta(dy_a, y_a, T, S)
    dqs, dks, dvs = [], [], []
    for g, (_, d) in enumerate(DIL_GROUPS):
        Lv = L_a.reshape(B, S // d, d * LANES)
        dyv = dy_a16.reshape(B, S // d, d * DIL_GW)
        dlv = delta_a.reshape(B, S // d, d * LANES)
        dqs.append(_dil_dq(g, pmvs[g], nmb, qa_blk, Lv, dyv, dlv, B, S).reshape(T, DIL_GW))
        dk_g, dv_g = _dil_dkv(g, pmvs[g], nmb, qa_blk, Lv, dyv, dlv, B, S)
        dks.append(dk_g.reshape(T, DIL_GW))
        dvs.append(dv_g.reshape(T, DIL_GW))

    dcum = dcum_q + jnp.pad(dcumT.transpose(0, 2, 1), ((0, 0), (0, 0), (0, LANES - 8)))
    dcum_run = _cumsum(dcum)
    dcum_tot = dcum_run[:, S - 1:S, :]

    def forget_grad_fn(run, dcv, fl, tot):
        lane = lax.broadcasted_iota(jnp.int32, fl.shape, 1)
        df = jnp.where(lane < FOX_HEADS, (tot - run + dcv) * _sigmoid(-fl), 0.0)
        return [df, _colsum(df)]

    df16, dfb = _rowwise("forget_gate_grad", forget_grad_fn, T, tms,
                         [(dcum_run.reshape(T, LANES), "row", None), (dcum.reshape(T, LANES), "row", None), (flog, "row", None),
                          (dcum_tot, "bvec", None)],
                         [("row", LANES, BF16), ("bacc", LANES, F32)], S)

    dpm = jnp.concatenate([dga, dgb, dq_b.reshape(T, FOX_W), dk_b.reshape(T, FOX_W), dv_b.reshape(T, FOX_W)]
                          + dqs + dks + dvs + ([jnp.zeros((T, nmp - nm), BF16)] if nmp > nm else []), axis=1)
    tmn = _div(nmp, 512, LANES)
    dwmT = _matmul("mix_dw_in", "tn", [[(dpm, h2)]], nmp, D, T, tmn, D, tkw, [F32])[0]
    dwfT = _matmul("mix_dw_f", "tn", [[(df16, h2)]], LANES, D, T, LANES, D, tkw, [F32])[0]
    dh2f = _matmul("mix_dh_f", "nn", [[(df16, wfT)]], T, D, LANES, tm1k, tnd, LANES, [F32])[0]

    def add_tile(accs, ex):
        return [accs[0] + ex[0]]

    dh2 = _matmul("mix_dh", "nn", [[(dpm, wmT)]], T, D, nmp, _div(T, 512, 8), tnd, _div(nmp, 2048, LANES), [F32],
                  extras=[(dh2f, "tile", 0)], epilogue=add_tile)[0]
    dx1, dsh2, dsc2, dgmix = _normmod_bwd("mix_normmod_bwd", x1, norm_mix, sc2, dh2, dx2, S)

    dx0, (dsh1, dsc1, dgt1, dg1), (dwg1, dwu1, dwd1) = _ffn_backward(
        "ffn1", dx1, saved1, norm_ffn1, sc1, gt1, W["ffn1_w_gate"], W["ffn1_w_up"], W["ffn1_w_down"], S)
    grad_x = dx0.reshape(B, S, D)

    dmod = jnp.concatenate([dsh1, dsc1, dgt1, dsh2, dsc2, dgt2, dsh3, dsc3, dgt3], axis=1).reshape(B, N_MOD * D)
    dmod_all = _small_allgather(dmod, "gather_dmod").reshape(N_DEV * B, N_MOD * D)
    dmod_cols = lax.dynamic_slice(dmod_all, (0, me * cl), (N_DEV * B, cl))
    g_ada_w, g_ada_b = _ada_backward(c_act, dmod_cols, dmod_all)

    fbg = jnp.sum(dfb, axis=0)
    small = jnp.concatenate([jnp.sum(dg1, axis=0), jnp.sum(dgmix, axis=0), jnp.sum(dg3, axis=0), jnp.sum(dg_final, axis=0),
                             fbg, jnp.sum(loss_b, axis=0)], axis=1)
    small = _sum_devices(_small_allgather(small, "gather_small"))
    g_small = {"norm_ffn1": small[:, 0:D], "norm_mix": small[:, D:2 * D], "norm_ffn2": small[:, 2 * D:3 * D],
               "norm_final": small[:, 3 * D:4 * D], "forget_bias": small[:, 4 * D:4 * D + FOX_HEADS], "ada_b": g_ada_b}
    loss = small[0, 4 * D + LANES]

    dwinT = jnp.concatenate([dwmT[QA:QA + 3 * DIL_W], dwmT[QB:QB + 3 * FOX_W], dwfT[:8], dwmT[GA:2 * D]], axis=0)
    gfull = {"ffn1_w_gate": dwg1, "ffn1_w_up": dwu1, "ffn1_w_down": dwd1, "w_in": dwinT, "w_branch_a": dwa, "w_branch_b": dwb,
             "w_out": dwout, "ffn2_w_gate": dwg2, "ffn2_w_up": dwu2, "ffn2_w_down": dwd2}
    g_packed = jnp.concatenate([packs[n].pack_grad(gfull[n]) for n in BIG], axis=1)
    g_local = _reduce_scatter(g_packed)
    grads = {n: packs[n].unpack_local(g_local[offs[n]:offs[n] + packs[n].rows_pad])[None] for n in BIG}
    grads["ada_w"] = g_ada_w[None]

    delta, new_m, new_v = {}, {}, {}
    for n in ["ada_w"] + BIG:
        shp = args[n].shape
        d_, m_, v_ = _adamw(f"adamw_{n}", args[n][0], grads[n][0], args["m_" + n][0], args["v_" + n][0])
        delta[n], new_m[n], new_v[n] = d_.reshape(shp), m_.reshape(shp), v_.reshape(shp)
    sizes = [args[n].size for n in SMALL]
    tot = sum(sizes)
    padded = -(-tot // (8 * LANES)) * (8 * LANES)

    def flat(get):
        v = jnp.concatenate([get(n).reshape(-1) for n in SMALL])
        return jnp.pad(v, (0, padded - tot)).reshape(8, padded // 8)

    d_s, m_s, v_s = _adamw("adamw_small", flat(lambda n: args[n]), flat(lambda n: g_small[n]), flat(lambda n: args["m_" + n]),
                           flat(lambda n: args["v_" + n]))
    o = 0
    for n, sz in zip(SMALL, sizes):
        shp = args[n].shape
        grads[n] = g_small[n].reshape(shp)
        delta[n] = d_s.reshape(-1)[o:o + sz].reshape(shp)
        new_m[n] = m_s.reshape(-1)[o:o + sz].reshape(shp)
        new_v[n] = v_s.reshape(-1)[o:o + sz].reshape(shp)
        o += sz

    order = ["ada_w", "ada_b", "norm_ffn1", "ffn1_w_gate", "ffn1_w_up", "ffn1_w_down", "norm_mix", "w_in", "forget_bias",
             "w_branch_a", "w_branch_b", "w_out", "norm_ffn2", "ffn2_w_gate", "ffn2_w_up", "ffn2_w_down", "norm_final"]
    return (loss, grad_x, *[grads[n] for n in order], *[delta[n] for n in order], *[new_m[n] for n in order],
            *[new_v[n] for n in order])
```

```python
import functools
import math

import jax
import jax.numpy as jnp
from jax import lax
from jax.experimental import pallas as pl
from jax.experimental.pallas import tpu as pltpu

F32 = jnp.float32
BF16 = jnp.bfloat16
MESH = pl.DeviceIdType.MESH
ANY = pl.BlockSpec(memory_space=pl.ANY)
VMEM_SPEC = pl.BlockSpec(memory_space=pltpu.VMEM)

N_DEV = 8
HEAD_DIM = 64
BLOCK = 128
DIL_GROUPS = ((128, 1), (512, 4), (2048, 16))
N_DIL = len(DIL_GROUPS)
DIL_HPG = 4
DIL_GW = DIL_HPG * HEAD_DIM
DIL_W = N_DIL * DIL_GW
FOX_HEADS = 8
FOX_W = FOX_HEADS * HEAD_DIM
N_MOD = 9
RMS_EPS = 1e-6
ALIBI_MAX_BIAS = 8.0
NEG_INF = -1e30
ADAM_LR, ADAM_B1, ADAM_B2, ADAM_EPS, ADAM_WD, ADAM_STEP = 0.001, 0.9, 0.999, 1e-08, 0.01, 10
V7X_VMEM_LIMIT = 52 * 1024 * 1024
LANES = 128
ROW_ALIGN = 16
PACK_ROW_QUANTUM = 256
FOX_STRIP = 32
SCALE = 1.0 / math.sqrt(HEAD_DIM)


def _div(dim, target, quantum):
    best = None
    for t in range(quantum, min(dim, target) + 1, quantum):
        if dim % t == 0:
            best = t
    return best or dim


def _params(sem=None):
    return pltpu.CompilerParams(dimension_semantics=sem, vmem_limit_bytes=V7X_VMEM_LIMIT)


def _sigmoid(x):
    return 1.0 / (1.0 + jnp.exp(-x))


def _position():
    x, y, c = lax.axis_index("x"), lax.axis_index("y"), lax.axis_index("c")
    return x, y, c


def _small_allgather(v, name):
    rows, cols = v.shape

    def body(v_ref, out_ref, send_sems, recv_sems):
        x, y, c = _position()
        me = 4 * x + 2 * y + c
        out_ref[me] = v_ref[...]

        def peer(k):
            return (1 - x if k & 4 else x, 1 - y if k & 2 else y, 1 - c if k & 1 else c)

        def copy(k, slot):
            return pltpu.make_async_remote_copy(
                src_ref=v_ref, dst_ref=out_ref.at[slot], send_sem=send_sems.at[k - 1], recv_sem=recv_sems.at[k - 1],
                device_id=peer(k), device_id_type=MESH)

        sends = [copy(k, me) for k in range(1, N_DEV)]
        for cp in sends:
            cp.start()
        for k in range(1, N_DEV):
            px, py, pc = peer(k)
            copy(k, 4 * px + 2 * py + pc).wait_recv()
        for cp in sends:
            cp.wait_send()

    return pl.pallas_call(
        body, name=name,
        out_shape=jax.ShapeDtypeStruct((N_DEV, rows, cols), v.dtype),
        in_specs=[VMEM_SPEC], out_specs=VMEM_SPEC,
        scratch_shapes=[pltpu.SemaphoreType.DMA((N_DEV - 1,)), pltpu.SemaphoreType.DMA((N_DEV - 1,))],
    )(v)


def _weight_allgather(p):
    rows, cols = p.shape

    def body(p_ref, out_ref, send_sems, recv_sems, local_sem):
        x, y, c = _position()
        me, sibling = (x, y, c), (x, y, 1 - c)
        chips = [(1 - x, y), (x, 1 - y), (1 - x, 1 - y)]

        def slot(px, py, pc):
            return out_ref.at[4 * px + 2 * py + pc]

        def copy(k, block, to, src=None):
            return pltpu.make_async_remote_copy(
                src_ref=slot(*block) if src is None else src, dst_ref=slot(*block),
                send_sem=send_sems.at[k], recv_sem=recv_sems.at[k], device_id=to, device_id_type=MESH)

        mine = pltpu.make_async_copy(p_ref, slot(*me), local_sem)
        mine.start()
        first = [copy(0, me, sibling, src=p_ref)]
        first += [copy(1 + j, me, (*chip, c), src=p_ref) for j, chip in enumerate(chips)]
        for cp in first:
            cp.start()
        passed = [copy(4 + j, (*chip, c), sibling) for j, chip in enumerate(chips)]
        for j, chip in enumerate(chips):
            copy(1 + j, (*chip, c), me).wait_recv()
            passed[j].start()
        copy(0, sibling, me).wait_recv()
        for j, chip in enumerate(chips):
            copy(4 + j, (*chip, 1 - c), me).wait_recv()
        for cp in first + passed:
            cp.wait_send()
        mine.wait()

    return pl.pallas_call(
        body, name="weight_allgather",
        out_shape=jax.ShapeDtypeStruct((N_DEV, rows, cols), p.dtype),
        in_specs=[ANY], out_specs=ANY,
        scratch_shapes=[pltpu.SemaphoreType.DMA((7,)), pltpu.SemaphoreType.DMA((7,)), pltpu.SemaphoreType.DMA],
    )(p)


def _grad_exchange_sibling(g):
    _, rows, cols = g.shape

    def body(g_ref, out_ref, send_sems, recv_sems):
        x, y, c = _position()
        sibling = (x, y, 1 - c)

        def copy(q):
            px, py = q >> 1, q & 1
            return pltpu.make_async_remote_copy(
                src_ref=g_ref.at[4 * px + 2 * py + (1 - c)], dst_ref=out_ref.at[q],
                send_sem=send_sems.at[q], recv_sem=recv_sems.at[q], device_id=sibling, device_id_type=MESH)

        copies = [copy(q) for q in range(4)]
        for cp in copies:
            cp.start()
        for cp in copies:
            cp.wait_recv()
        for cp in copies:
            cp.wait_send()

    return pl.pallas_call(
        body, name="grad_exchange_sibling",
        out_shape=jax.ShapeDtypeStruct((4, rows, cols), g.dtype),
        in_specs=[ANY], out_specs=ANY,
        scratch_shapes=[pltpu.SemaphoreType.DMA((4,)), pltpu.SemaphoreType.DMA((4,))],
    )(g)


def _grad_exchange_chips(s):
    _, rows, cols = s.shape

    def body(s_ref, out_ref, send_sems, recv_sems):
        x, y, c = _position()
        chips = [(1 - x, y), (x, 1 - y), (1 - x, 1 - y)]

        def copy(k):
            return pltpu.make_async_remote_copy(
                src_ref=s_ref.at[k], dst_ref=out_ref.at[k], send_sem=send_sems.at[k], recv_sem=recv_sems.at[k],
                device_id=(*chips[k], c), device_id_type=MESH)

        copies = [copy(k) for k in range(3)]
        for cp in copies:
            cp.start()
        for cp in copies:
            cp.wait_recv()
        for cp in copies:
            cp.wait_send()

    return pl.pallas_call(
        body, name="grad_exchange_chips",
        out_shape=jax.ShapeDtypeStruct((3, rows, cols), s.dtype),
        in_specs=[ANY], out_specs=ANY,
        scratch_shapes=[pltpu.SemaphoreType.DMA((3,)), pltpu.SemaphoreType.DMA((3,))],
    )(s)


def _chip_partial_sums(g, recv_sib, jj, qq):
    _, rows, cols = g.shape
    tr = _div(rows, 512, ROW_ALIGN)

    def body(jj_ref, qq_ref, g_ref, r_ref, o_ref):
        o_ref[...] = (g_ref[...] + r_ref[...]).astype(o_ref.dtype)

    return pl.pallas_call(
        body, name="chip_partial_sums",
        out_shape=jax.ShapeDtypeStruct((3, rows, cols), BF16),
        grid_spec=pltpu.PrefetchScalarGridSpec(
            num_scalar_prefetch=2, grid=(3, rows // tr),
            in_specs=[pl.BlockSpec((None, tr, cols), lambda k, i, jj, qq: (jj[k], i, 0)),
                      pl.BlockSpec((None, tr, cols), lambda k, i, jj, qq: (qq[k], i, 0))],
            out_specs=pl.BlockSpec((None, tr, cols), lambda k, i, jj, qq: (k, i, 0))),
        compiler_params=_params(("arbitrary", "arbitrary")),
    )(jj, qq, g, recv_sib)


def _own_partial_sum(g, recv_sib, jj, qq):
    _, rows, cols = g.shape
    tr = _div(rows, 512, ROW_ALIGN)

    def body(jj_ref, qq_ref, g_ref, r_ref, o_ref):
        o_ref[...] = g_ref[...] + r_ref[...]

    return pl.pallas_call(
        body, name="own_partial_sum",
        out_shape=jax.ShapeDtypeStruct((rows, cols), F32),
        grid_spec=pltpu.PrefetchScalarGridSpec(
            num_scalar_prefetch=2, grid=(rows // tr,),
            in_specs=[pl.BlockSpec((None, tr, cols), lambda i, jj, qq: (jj[0], i, 0)),
                      pl.BlockSpec((None, tr, cols), lambda i, jj, qq: (qq[0], i, 0))],
            out_specs=pl.BlockSpec((tr, cols), lambda i, jj, qq: (i, 0))),
        compiler_params=_params(("arbitrary",)),
    )(jj, qq, g, recv_sib)


def _final_grad_sum(own, recv):
    rows, cols = own.shape
    tr = _div(rows, 512, ROW_ALIGN)

    def body(o_ref, r_ref, out_ref):
        out_ref[...] = ((o_ref[...] + r_ref[0].astype(F32)) + r_ref[1].astype(F32)) + r_ref[2].astype(F32)

    return pl.pallas_call(
        body, name="final_grad_sum",
        out_shape=jax.ShapeDtypeStruct((rows, cols), F32),
        grid=(rows // tr,),
        in_specs=[pl.BlockSpec((tr, cols), lambda i: (i, 0)), pl.BlockSpec((3, tr, cols), lambda i: (0, i, 0))],
        out_specs=pl.BlockSpec((tr, cols), lambda i: (i, 0)),
        compiler_params=_params(("arbitrary",)),
    )(own, recv)


def _reduce_scatter(g):
    x, y, c = _position()
    chips = [(1 - x, y), (x, 1 - y), (1 - x, 1 - y)]
    jj = jnp.stack([4 * px + 2 * py + c for px, py in chips]).astype(jnp.int32)
    qq = jnp.stack([2 * px + py for px, py in chips]).astype(jnp.int32)
    jme = jnp.reshape(4 * x + 2 * y + c, (1,)).astype(jnp.int32)
    qme = jnp.reshape(2 * x + y, (1,)).astype(jnp.int32)
    recv_sib = _grad_exchange_sibling(g)
    sums = _chip_partial_sums(g, recv_sib, jj, qq)
    own = _own_partial_sum(g, recv_sib, jme, qme)
    recv = _grad_exchange_chips(sums)
    return _final_grad_sum(own, recv)


def _matmul(name, form, prods, M, N, K, tm, tn, tk, out_dtypes, extras=(), epilogue=None, rows_per_example=None):
    nk = K // tk
    n_acc = len(prods)
    flat = [ab for group in prods for ab in group]
    dims = {"nn": (((1,), (0,)), ((), ())), "nt": (((1,), (1,)), ((), ())), "tn": (((0,), (0,)), ((), ()))}[form]
    if form == "tn":
        a_spec = pl.BlockSpec((tk, tm), lambda i, j, k: (k, i))
    else:
        a_spec = pl.BlockSpec((tm, tk), lambda i, j, k: (i, k))
    if form == "nt":
        b_spec = pl.BlockSpec((tn, tk), lambda i, j, k: (j, k))
    else:
        b_spec = pl.BlockSpec((tk, tn), lambda i, j, k: (k, j))
    in_specs, operands = [], []
    for a, b in flat:
        in_specs += [a_spec, b_spec]
        operands += [a, b]
    for arr, kind, off in extras:
        if kind == "tile":
            assert off % tn == 0
            in_specs.append(pl.BlockSpec((tm, tn), functools.partial(lambda i, j, k, o: (i, j + o), o=off // tn)))
        else:
            tiles = rows_per_example // tm
            in_specs.append(pl.BlockSpec((None, 1, tn), functools.partial(lambda i, j, k, t: (i // t, 0, j), t=tiles)))
        operands.append(arr)
    n_in, n_out = len(operands), len(out_dtypes)

    def body(*refs):
        in_refs, out_refs, acc_refs = refs[:n_in], refs[n_in:n_in + n_out], refs[n_in + n_out:]
        k = pl.program_id(2)
        partials, p = [], 0
        for group in prods:
            tot = None
            for _ in group:
                d = lax.dot_general(in_refs[2 * p][...], in_refs[2 * p + 1][...], dims, preferred_element_type=F32)
                tot = d if tot is None else tot + d
                p += 1
            partials.append(tot)

        def finish(accs):
            ex = [r[...] for r in in_refs[2 * len(flat):]]
            outs = epilogue(accs, ex) if epilogue is not None else accs
            for r, o in zip(out_refs, outs):
                r[...] = o.astype(r.dtype)

        if nk == 1:
            finish(partials)
        else:
            @pl.when(k == 0)
            def _():
                for r, v in zip(acc_refs, partials):
                    r[...] = v

            @pl.when(k > 0)
            def _():
                for r, v in zip(acc_refs, partials):
                    r[...] += v

            @pl.when(k == nk - 1)
            def _():
                finish([r[...] for r in acc_refs])

    outs = pl.pallas_call(
        body, name=name,
        out_shape=[jax.ShapeDtypeStruct((M, N), dt) for dt in out_dtypes],
        grid=(M // tm, N // tn, nk),
        in_specs=in_specs,
        out_specs=[pl.BlockSpec((tm, tn), lambda i, j, k: (i, j)) for _ in out_dtypes],
        scratch_shapes=[pltpu.VMEM((tm, tn), F32) for _ in range(n_acc)] if nk > 1 else [],
        compiler_params=_params(("parallel", "parallel", "arbitrary")),
    )(*operands)
    return outs


def _rowwise(name, fn, T, tm, ins, outs, rows_per_example):
    tiles = rows_per_example // tm
    n_ex = T // rows_per_example
    in_specs, operands = [], []
    for arr, kind, arg in ins:
        if kind == "row":
            if arg is None:
                in_specs.append(pl.BlockSpec((tm, arr.shape[1]), lambda i: (i, 0)))
            else:
                in_specs.append(pl.BlockSpec((tm, arg[0]), functools.partial(lambda i, cb: (i, cb), cb=arg[1])))
        elif kind == "bvec":
            in_specs.append(pl.BlockSpec((None, 1, arr.shape[2]), lambda i: (i // tiles, 0, 0)))
        else:
            in_specs.append(pl.BlockSpec((1, arr.shape[1]), lambda i: (0, 0)))
        operands.append(arr)
    out_shape, out_specs = [], []
    for kind, cols, dt in outs:
        if kind == "row":
            out_shape.append(jax.ShapeDtypeStruct((T, cols), dt))
            out_specs.append(pl.BlockSpec((tm, cols), lambda i: (i, 0)))
        else:
            out_shape.append(jax.ShapeDtypeStruct((n_ex, 1, cols), F32))
            out_specs.append(pl.BlockSpec((None, 1, cols), lambda i: (i // tiles, 0, 0)))
    n_in = len(operands)

    def body(*refs):
        i = pl.program_id(0)
        vals = fn(*[r[...] for r in refs[:n_in]])
        for (kind, _, _), r, v in zip(outs, refs[n_in:], vals):
            if kind == "row":
                r[...] = v.astype(r.dtype)
            else:
                @pl.when(i % tiles == 0)
                def _():
                    r[...] = jnp.zeros_like(r)

                r[...] += v

    return pl.pallas_call(
        body, name=name, out_shape=out_shape, grid=(T // tm,), in_specs=in_specs, out_specs=out_specs,
        compiler_params=_params(("arbitrary",)),
    )(*operands)


def _colsum(v):
    return jnp.sum(v, axis=0, keepdims=True)


def _rms_parts(x):
    rstd = lax.rsqrt(jnp.mean(x * x, axis=-1, keepdims=True) + RMS_EPS)
    return x * rstd, rstd


def _normmod(name, x, g, sc, sh, S):
    T, D = x.shape

    def fn(xv, gv, scv, shv):
        xhat, _ = _rms_parts(xv)
        return [(xhat * gv) * (1.0 + scv) + shv]

    return _rowwise(name, fn, T, _div(S, 512, 8), [(x, "row", None), (g, "vec", None), (sc, "bvec", None), (sh, "bvec", None)],
                    [("row", D, BF16)], S)[0]


def _normmod_bwd(name, x, g, sc, dh, dres, S):
    T, D = x.shape

    def fn(xv, gv, scv, dhv, drv):
        xhat, rstd = _rms_parts(xv)
        n = xhat * gv
        dn = dhv * (1.0 + scv)
        dxh = dn * gv
        dx = rstd * (dxh - xhat * jnp.mean(dxh * xhat, axis=-1, keepdims=True))
        return [drv + dx, _colsum(dhv), _colsum(dhv * n), _colsum(dn * xhat)]

    return _rowwise(name, fn, T, _div(S, 256, 8),
                    [(x, "row", None), (g, "vec", None), (sc, "bvec", None), (dh, "row", None), (dres, "row", None)],
                    [("row", D, F32), ("bacc", D, F32), ("bacc", D, F32), ("bacc", D, F32)], S)


def _gate_grad(name, dx, y, gt, coeff, S):
    T, D = dx.shape

    def fn(dxv, yv, gtv):
        return [coeff * gtv * dxv, _colsum(coeff * dxv * yv.astype(F32))]

    return _rowwise(name, fn, T, _div(S, 512, 8), [(dx, "row", None), (y, "row", None), (gt, "bvec", None)],
                    [("row", D, BF16), ("bacc", D, F32)], S)


def _ffn_forward(tag, x, g, sh, sc, gt, wgT, wuT, wd, S):
    T, D = x.shape
    F = wd.shape[0]
    h = _normmod(f"{tag}_normmod", x, g, sc, sh, S)

    def gateup(accs, ex):
        a, u = accs
        return [a, u, a * _sigmoid(a) * u]

    a, u, s = _matmul(f"{tag}_gateup", "nt", [[(h, wgT)], [(h, wuT)]], T, F, D, _div(T, 1024, 8), _div(F, 256, LANES), D,
                      [BF16, BF16, BF16], epilogue=gateup)

    def down(accs, ex):
        xv, gtv = ex
        return [xv + 0.5 * gtv * accs[0], accs[0]]

    tmd = _div(S, 512, 8)
    x_new, y = _matmul(f"{tag}_down", "nn", [[(s, wd)]], T, D, F, tmd, _div(D, 512, LANES), F, [F32, BF16],
                       extras=[(x, "tile", 0), (gt, "brow", 0)], epilogue=down, rows_per_example=S)
    return x_new, (x, h, a, u, s, y)


def _ffn_backward(tag, dx_out, saved, g, sc, gt, wgT, wuT, wd, S):
    x, h, a, u, s, y = saved
    T, D = x.shape
    F = wd.shape[0]
    dy, dgt = _gate_grad(f"{tag}_gate_grad", dx_out, y, gt, 0.5, S)

    def act_grad(accs, ex):
        ds = accs[0]
        av, uv = ex[0].astype(F32), ex[1].astype(F32)
        sg = _sigmoid(av)
        return [ds * uv * (sg * (1.0 + av * (1.0 - sg))), ds * (av * sg)]

    da, du = _matmul(f"{tag}_act_grad", "nt", [[(dy, wd)]], T, F, D, _div(T, 1024, 8), _div(F, 256, LANES), D, [BF16, BF16],
                     extras=[(a, "tile", 0), (u, "tile", 0)], epilogue=act_grad)
    tmw = _div(F, 1408, LANES)
    tkw = _div(T, 1024, LANES)
    dwd = _matmul(f"{tag}_dw_down", "tn", [[(s, dy)]], F, D, T, tmw, D, tkw, [F32])[0]
    dwgT = _matmul(f"{tag}_dw_gate", "tn", [[(da, h)]], F, D, T, tmw, D, tkw, [F32])[0]
    dwuT = _matmul(f"{tag}_dw_up", "tn", [[(du, h)]], F, D, T, tmw, D, tkw, [F32])[0]
    dh = _matmul(f"{tag}_dh", "nn", [[(da, wgT), (du, wuT)]], T, D, F, _div(T, 512, 8), _div(D, 512, LANES), F, [F32])[0]
    dx_in, dsh, dsc, dg = _normmod_bwd(f"{tag}_normmod_bwd", x, g, sc, dh, dx_out, S)
    return dx_in, (dsh, dsc, dgt, dg), (dwgT, dwuT, dwd)


def _loss_head(x, tgt, g, S):
    T, D = x.shape

    def fn(xv, tv, gv):
        xhat, rstd = _rms_parts(xv)
        e = xhat * gv - tv
        loss = jnp.broadcast_to(0.5 / D * jnp.sum(_colsum(e * e), axis=1, keepdims=True), (1, LANES))
        dy = e * (1.0 / D)
        dxh = dy * gv
        dx = rstd * (dxh - xhat * jnp.mean(dxh * xhat, axis=-1, keepdims=True))
        return [dx, loss, _colsum(dy * xhat)]

    return _rowwise("loss_head", fn, T, _div(S, 512, 8), [(x, "row", None), (tgt, "row", None), (g, "vec", None)],
                    [("row", D, F32), ("bacc", LANES, F32), ("bacc", D, F32)], S)


def _cumsum(v):
    B, S, _ = v.shape
    rows = _div(S, 1024, BLOCK)

    def body(x_ref, o_ref, carry):
        i = pl.program_id(1)

        @pl.when(i == 0)
        def _():
            carry[...] = jnp.zeros_like(carry)

        r = lax.broadcasted_iota(jnp.int32, (BLOCK, BLOCK), 0)
        c = lax.broadcasted_iota(jnp.int32, (BLOCK, BLOCK), 1)
        tri = (c <= r).astype(F32)
        last = carry[0:1, :]
        for j in range(0, rows, BLOCK):
            cum = jnp.dot(tri, x_ref[j:j + BLOCK, :], precision=lax.Precision.HIGHEST, preferred_element_type=F32) + last
            o_ref[j:j + BLOCK, :] = cum
            last = cum[BLOCK - 1:BLOCK, :]
        carry[...] = jnp.broadcast_to(last, carry.shape)

    return pl.pallas_call(
        body, name="cumsum", out_shape=jax.ShapeDtypeStruct(v.shape, F32), grid=(B, S // rows),
        in_specs=[pl.BlockSpec((None, rows, LANES), lambda b, i: (b, i, 0))],
        out_specs=pl.BlockSpec((None, rows, LANES), lambda b, i: (b, i, 0)),
        scratch_shapes=[pltpu.VMEM((8, LANES), F32)],
        compiler_params=_params(("arbitrary", "arbitrary")),
    )(v)


def _fox_scores(q, k, cq, ck, qpos, kpos):
    s = lax.dot_general(q, k, (((1,), (1,)), ((), ())), preferred_element_type=F32) * SCALE + cq - ck
    return jnp.where(kpos <= qpos, s, NEG_INF)


def _fox_positions(qi, kj, tq, tk):
    qpos = qi * tq + lax.broadcasted_iota(jnp.int32, (tq, tk), 0)
    kpos = kj * tk + lax.broadcasted_iota(jnp.int32, (tq, tk), 1)
    return qpos, kpos


def _fox_forward(pm3, cum, cumT, qcol, kcol, vcol, tq):
    B, S, _ = pm3.shape
    nq = S // tq

    def body(q_ref, k_ref, v_ref, cq_ref, ck_ref, o_ref, o32_ref, lse_ref, m_sc, l_sc, acc_sc):
        qi, kj = pl.program_id(1), pl.program_id(2)

        @pl.when(kj == 0)
        def _():
            m_sc[...] = jnp.full_like(m_sc, NEG_INF)
            l_sc[...] = jnp.zeros_like(l_sc)
            acc_sc[...] = jnp.zeros_like(acc_sc)

        @pl.when(kj <= qi)
        def _():
            qpos, kpos = _fox_positions(qi, kj, tq, tq)
            for h in range(FOX_HEADS):
                hs = slice(HEAD_DIM * h, HEAD_DIM * (h + 1))
                s = _fox_scores(q_ref[:, hs], k_ref[:, hs], cq_ref[:, h:h + 1], ck_ref[h:h + 1, :], qpos, kpos)
                m_prev = m_sc[h]
                m_new = jnp.maximum(m_prev, jnp.max(s, axis=-1, keepdims=True))
                alpha = jnp.exp(m_prev - m_new)
                p = jnp.exp(s - m_new)
                l_sc[h] = alpha * l_sc[h] + jnp.sum(p, axis=-1, keepdims=True)
                acc_sc[:, hs] = alpha * acc_sc[:, hs] + lax.dot_general(
                    p.astype(BF16), v_ref[:, hs], (((1,), (0,)), ((), ())), preferred_element_type=F32)
                m_sc[h] = m_new

        @pl.when(kj == nq - 1)
        def _():
            lse_ref[...] = jnp.zeros_like(lse_ref)
            for h in range(FOX_HEADS):
                hs = slice(HEAD_DIM * h, HEAD_DIM * (h + 1))
                oh = acc_sc[:, hs] / l_sc[h]
                o_ref[:, hs] = oh.astype(o_ref.dtype)
                o32_ref[:, hs] = oh
                lse_ref[:, h:h + 1] = m_sc[h] + jnp.log(l_sc[h])

    return pl.pallas_call(
        body, name="fox_forward",
        out_shape=[jax.ShapeDtypeStruct((B, S, FOX_W), BF16), jax.ShapeDtypeStruct((B, S, FOX_W), F32),
                   jax.ShapeDtypeStruct((B, S, LANES), F32)],
        grid=(B, nq, nq),
        in_specs=[pl.BlockSpec((None, tq, FOX_W), lambda b, i, j: (b, i, qcol)),
                  pl.BlockSpec((None, tq, FOX_W), lambda b, i, j: (b, jnp.minimum(i, j), kcol)),
                  pl.BlockSpec((None, tq, FOX_W), lambda b, i, j: (b, jnp.minimum(i, j), vcol)),
                  pl.BlockSpec((None, tq, LANES), lambda b, i, j: (b, i, 0)),
                  pl.BlockSpec((None, 8, tq), lambda b, i, j: (b, 0, jnp.minimum(i, j)))],
        out_specs=[pl.BlockSpec((None, tq, FOX_W), lambda b, i, j: (b, i, 0)),
                   pl.BlockSpec((None, tq, FOX_W), lambda b, i, j: (b, i, 0)),
                   pl.BlockSpec((None, tq, LANES), lambda b, i, j: (b, i, 0))],
        scratch_shapes=[pltpu.VMEM((FOX_HEADS, tq, 1), F32), pltpu.VMEM((FOX_HEADS, tq, 1), F32), pltpu.VMEM((tq, FOX_W), F32)],
        compiler_params=_params(("parallel", "parallel", "arbitrary")),
    )(pm3, pm3, pm3, cum, cumT)


def _fox_dq(pm3, do, delta, lse, cum, cumT, qcol, kcol, vcol, tq):
    B, S, _ = pm3.shape
    nq = S // tq

    def body(q_ref, k_ref, v_ref, do_ref, dl_ref, lse_ref, cq_ref, ck_ref, dq_ref, dc_ref, acc_sc, dc_sc):
        qi, kj = pl.program_id(1), pl.program_id(2)

        @pl.when(kj == 0)
        def _():
            acc_sc[...] = jnp.zeros_like(acc_sc)
            dc_sc[...] = jnp.zeros_like(dc_sc)

        @pl.when(kj <= qi)
        def _():
            qpos, kpos = _fox_positions(qi, kj, tq, tq)
            for h in range(FOX_HEADS):
                hs = slice(HEAD_DIM * h, HEAD_DIM * (h + 1))
                s = _fox_scores(q_ref[:, hs], k_ref[:, hs], cq_ref[:, h:h + 1], ck_ref[h:h + 1, :], qpos, kpos)
                p = jnp.exp(s - lse_ref[:, h:h + 1])
                doh = do_ref[:, hs]
                dp = lax.dot_general(doh, v_ref[:, hs], (((1,), (1,)), ((), ())), preferred_element_type=F32)
                ds = p * (dp - dl_ref[:, h:h + 1])
                dc_sc[h] += jnp.sum(ds, axis=-1, keepdims=True)
                acc_sc[:, hs] += lax.dot_general(ds.astype(BF16), k_ref[:, hs], (((1,), (0,)), ((), ())),
                                                 preferred_element_type=F32)

        @pl.when(kj == nq - 1)
        def _():
            dq_ref[...] = (acc_sc[...] * SCALE).astype(dq_ref.dtype)
            dc_ref[...] = jnp.zeros_like(dc_ref)
            for h in range(FOX_HEADS):
                dc_ref[:, h:h + 1] = dc_sc[h]

    qspec = pl.BlockSpec((None, tq, FOX_W), lambda b, i, j: (b, i, 0))
    lspec = pl.BlockSpec((None, tq, LANES), lambda b, i, j: (b, i, 0))
    return pl.pallas_call(
        body, name="fox_dq",
        out_shape=[jax.ShapeDtypeStruct((B, S, FOX_W), BF16), jax.ShapeDtypeStruct((B, S, LANES), F32)],
        grid=(B, nq, nq),
        in_specs=[pl.BlockSpec((None, tq, FOX_W), lambda b, i, j: (b, i, qcol)),
                  pl.BlockSpec((None, tq, FOX_W), lambda b, i, j: (b, jnp.minimum(i, j), kcol)),
                  pl.BlockSpec((None, tq, FOX_W), lambda b, i, j: (b, jnp.minimum(i, j), vcol)),
                  qspec, lspec, lspec, lspec,
                  pl.BlockSpec((None, 8, tq), lambda b, i, j: (b, 0, jnp.minimum(i, j)))],
        out_specs=[qspec, lspec],
        scratch_shapes=[pltpu.VMEM((tq, FOX_W), F32), pltpu.VMEM((FOX_HEADS, tq, 1), F32)],
        compiler_params=_params(("parallel", "parallel", "arbitrary")),
    )(pm3, pm3, pm3, do, delta, lse, cum, cumT)


def _fox_dkv(pm3, do, delta, lse, cum, cumT, qcol, kcol, vcol, tq):
    B, S, _ = pm3.shape
    nq = S // tq

    def body(q_ref, k_ref, v_ref, do_ref, dl_ref, lse_ref, cq_ref, ck_ref, dk_ref, dv_ref, dc_ref, dk_sc, dv_sc, dc_sc):
        kj, qi = pl.program_id(1), pl.program_id(2)

        @pl.when(qi == 0)
        def _():
            dk_sc[...] = jnp.zeros_like(dk_sc)
            dv_sc[...] = jnp.zeros_like(dv_sc)
            dc_sc[...] = jnp.zeros_like(dc_sc)

        @pl.when(qi >= kj)
        def _():
            qpos, kpos = _fox_positions(qi, kj, tq, tq)
            for h in range(FOX_HEADS):
                hs = slice(HEAD_DIM * h, HEAD_DIM * (h + 1))
                qh = q_ref[:, hs]
                s = _fox_scores(qh, k_ref[:, hs], cq_ref[:, h:h + 1], ck_ref[h:h + 1, :], qpos, kpos)
                p = jnp.exp(s - lse_ref[:, h:h + 1])
                doh = do_ref[:, hs]
                dp = lax.dot_general(doh, v_ref[:, hs], (((1,), (1,)), ((), ())), preferred_element_type=F32)
                ds = p * (dp - dl_ref[:, h:h + 1])
                dv_sc[:, hs] += lax.dot_general(p.astype(BF16), doh, (((0,), (0,)), ((), ())), preferred_element_type=F32)
                dk_sc[:, hs] += lax.dot_general(ds.astype(BF16), qh, (((0,), (0,)), ((), ())), preferred_element_type=F32)
                dc_sc[h:h + 1, :] -= jnp.sum(ds, axis=0, keepdims=True)

        @pl.when(qi == nq - 1)
        def _():
            dk_ref[...] = (dk_sc[...] * SCALE).astype(dk_ref.dtype)
            dv_ref[...] = dv_sc[...].astype(dv_ref.dtype)
            dc_ref[...] = dc_sc[...]

    def qside(width):
        return pl.BlockSpec((None, tq, width), lambda b, j, i: (b, jnp.maximum(i, j), 0))

    kspec = pl.BlockSpec((None, tq, FOX_W), lambda b, j, i: (b, j, 0))
    return pl.pallas_call(
        body, name="fox_dkv",
        out_shape=[jax.ShapeDtypeStruct((B, S, FOX_W), BF16), jax.ShapeDtypeStruct((B, S, FOX_W), BF16),
                   jax.ShapeDtypeStruct((B, 8, S), F32)],
        grid=(B, nq, nq),
        in_specs=[pl.BlockSpec((None, tq, FOX_W), lambda b, j, i: (b, jnp.maximum(i, j), qcol)),
                  pl.BlockSpec((None, tq, FOX_W), lambda b, j, i: (b, j, kcol)),
                  pl.BlockSpec((None, tq, FOX_W), lambda b, j, i: (b, j, vcol)),
                  qside(FOX_W), qside(LANES), qside(LANES), qside(LANES),
                  pl.BlockSpec((None, 8, tq), lambda b, j, i: (b, 0, j))],
        out_specs=[kspec, kspec, pl.BlockSpec((None, 8, tq), lambda b, j, i: (b, 0, j))],
        scratch_shapes=[pltpu.VMEM((tq, FOX_W), F32), pltpu.VMEM((tq, FOX_W), F32), pltpu.VMEM((8, tq), F32)],
        compiler_params=_params(("parallel", "parallel", "arbitrary")),
    )(pm3, pm3, pm3, do, delta, lse, cum, cumT)


def _with_ones(x):
    lane = lax.broadcasted_iota(jnp.int32, (x.shape[0], HEAD_DIM), 1)
    return jnp.concatenate([x, jnp.where(lane == 0, 1.0, 0.0).astype(x.dtype)], axis=1)


def _causal_strip(s, r):
    qpos = r + lax.broadcasted_iota(jnp.int32, s.shape, 0)
    kpos = lax.broadcasted_iota(jnp.int32, s.shape, 1)
    return jnp.where(kpos <= qpos, s, NEG_INF)


NT = (((1,), (1,)), ((), ()))
NN = (((1,), (0,)), ((), ()))
TN = (((0,), (0,)), ((), ()))


def _fox_fwd(pm3, cumT, qcol, kcol, vcol, tq):
    B, S, _ = pm3.shape
    nq = S // tq
    strips = range(0, tq, FOX_STRIP)

    def body(q_ref, k_ref, v_ref, ck_ref, o_ref, o32_ref, lse_ref, s_sc, p_sc, al_sc, m_sc, acc_sc):
        qi, kj = pl.program_id(1), pl.program_id(2)

        @pl.when(kj == 0)
        def _():
            m_sc[...] = jnp.full_like(m_sc, NEG_INF)
            acc_sc[...] = jnp.zeros_like(acc_sc)

        def tile(diagonal):
            for h in range(FOX_HEADS):
                hs = slice(HEAD_DIM * h, HEAD_DIM * (h + 1))
                s_sc[...] = lax.dot_general(q_ref[:, hs] * SCALE, k_ref[:, hs], NT, preferred_element_type=F32)
                ck = ck_ref[h:h + 1, :]
                for r in strips:
                    rows = slice(r, r + FOX_STRIP)
                    s = s_sc[rows, :] - ck
                    if diagonal:
                        s = _causal_strip(s, r)
                    m_prev = m_sc[h, rows, :]
                    m_new = jnp.maximum(m_prev, jnp.max(s, axis=-1, keepdims=True))
                    p_sc[rows, :] = jnp.exp(s - m_new).astype(BF16)
                    al_sc[rows, :] = jnp.exp(m_prev - m_new)
                    m_sc[h, rows, :] = m_new
                acc_sc[h] = al_sc[...] * acc_sc[h] + lax.dot_general(p_sc[...], _with_ones(v_ref[:, hs]), NN,
                                                                     preferred_element_type=F32)

        @pl.when(kj < qi)
        def _():
            tile(False)

        @pl.when(kj == qi)
        def _():
            tile(True)

        @pl.when(kj == nq - 1)
        def _():
            lse_ref[...] = jnp.zeros_like(lse_ref)
            for h in range(FOX_HEADS):
                hs = slice(HEAD_DIM * h, HEAD_DIM * (h + 1))
                acc = acc_sc[h]
                l = acc[:, HEAD_DIM:HEAD_DIM + 1]
                oh = acc[:, :HEAD_DIM] / l
                o_ref[:, hs] = oh.astype(o_ref.dtype)
                o32_ref[:, hs] = oh
                lse_ref[:, h:h + 1] = m_sc[h] + jnp.log(l)

    ospec = pl.BlockSpec((None, tq, FOX_W), lambda b, i, j: (b, i, 0))
    return pl.pallas_call(
        body, name="fox_forward",
        out_shape=[jax.ShapeDtypeStruct((B, S, FOX_W), BF16), jax.ShapeDtypeStruct((B, S, FOX_W), F32),
                   jax.ShapeDtypeStruct((B, S, LANES), F32)],
        grid=(B, nq, nq),
        in_specs=[pl.BlockSpec((None, tq, FOX_W), lambda b, i, j: (b, i, qcol)),
                  pl.BlockSpec((None, tq, FOX_W), lambda b, i, j: (b, jnp.minimum(i, j), kcol)),
                  pl.BlockSpec((None, tq, FOX_W), lambda b, i, j: (b, jnp.minimum(i, j), vcol)),
                  pl.BlockSpec((None, 8, tq), lambda b, i, j: (b, 0, jnp.minimum(i, j)))],
        out_specs=[ospec, ospec, pl.BlockSpec((None, tq, LANES), lambda b, i, j: (b, i, 0))],
        scratch_shapes=[pltpu.VMEM((tq, tq), F32), pltpu.VMEM((tq, tq), BF16), pltpu.VMEM((tq, 1), F32),
                        pltpu.VMEM((FOX_HEADS, tq, 1), F32), pltpu.VMEM((FOX_HEADS, tq, LANES), F32)],
        compiler_params=_params(("parallel", "parallel", "arbitrary")),
    )(pm3, pm3, pm3, cumT)


def _fox_bwd(pm3, do, delta, lse, cumT, qcol, kcol, vcol, tq):
    B, S, _ = pm3.shape
    nq = S // tq
    strips = range(0, tq, FOX_STRIP)

    def body(q_ref, k_ref, v_ref, do_ref, dl_ref, lse_ref, ck_ref, dq_ref, rs_ref, dk_ref, dv_ref, cs_ref,
             s_sc, dp_sc, p_sc, ds_sc, dq_sc, dk_sc, dv_sc):
        kj, qi = pl.program_id(1), pl.program_id(2)

        @pl.when((kj == 0) & (qi == 0))
        def _():
            dq_sc[...] = jnp.zeros_like(dq_sc)

        @pl.when(qi == 0)
        def _():
            dk_sc[...] = jnp.zeros_like(dk_sc)
            dv_sc[...] = jnp.zeros_like(dv_sc)

        def tile(diagonal):
            qrows = pl.ds(pl.multiple_of(qi * tq, tq), tq)
            for h in range(FOX_HEADS):
                hs = slice(HEAD_DIM * h, HEAD_DIM * (h + 1))
                qh, kh, doh = q_ref[:, hs] * SCALE, k_ref[:, hs], do_ref[:, hs]
                s_sc[...] = lax.dot_general(qh, kh, NT, preferred_element_type=F32)
                dp_sc[...] = lax.dot_general(doh, v_ref[:, hs], NT, preferred_element_type=F32)
                ck = ck_ref[h:h + 1, :]
                for r in strips:
                    rows = slice(r, r + FOX_STRIP)
                    s = s_sc[rows, :] - ck
                    if diagonal:
                        s = _causal_strip(s, r)
                    p = jnp.exp(s - lse_ref[rows, h:h + 1])
                    p_sc[rows, :] = p.astype(BF16)
                    ds_sc[rows, :] = (p * (dp_sc[rows, :] - dl_ref[rows, h:h + 1])).astype(BF16)
                dv_sc[:, hs] += lax.dot_general(p_sc[...], doh, TN, preferred_element_type=F32)
                dk_sc[h] += lax.dot_general(ds_sc[...], _with_ones(qh), TN, preferred_element_type=F32)
                dq_sc[h, qrows, :] += lax.dot_general(ds_sc[...], _with_ones(kh), NN, preferred_element_type=F32)

        @pl.when(qi > kj)
        def _():
            tile(False)

        @pl.when(qi == kj)
        def _():
            tile(True)

        @pl.when(qi == nq - 1)
        def _():
            dv_ref[...] = dv_sc[...].astype(dv_ref.dtype)
            cs_ref[...] = jnp.zeros_like(cs_ref)
            for h in range(FOX_HEADS):
                hs = slice(HEAD_DIM * h, HEAD_DIM * (h + 1))
                dk = dk_sc[h]
                dk_ref[:, hs] = dk[:, :HEAD_DIM].astype(dk_ref.dtype)
                cs_ref[:, h:h + 1] = dk[:, HEAD_DIM:HEAD_DIM + 1]

        @pl.when((kj == nq - 1) & (qi == nq - 1))
        def _():
            rs_ref[...] = jnp.zeros_like(rs_ref)
            for h in range(FOX_HEADS):
                hs = slice(HEAD_DIM * h, HEAD_DIM * (h + 1))
                dq_ref[:, hs] = (dq_sc[h, :, :HEAD_DIM] * SCALE).astype(dq_ref.dtype)
                rs_ref[:, h:h + 1] = dq_sc[h, :, HEAD_DIM:HEAD_DIM + 1]

    def qside(width, col=0):
        return pl.BlockSpec((None, tq, width), lambda b, j, i: (b, jnp.maximum(i, j), col))

    kspec = pl.BlockSpec((None, tq, FOX_W), lambda b, j, i: (b, j, 0))
    return pl.pallas_call(
        body, name="fox_backward",
        out_shape=[jax.ShapeDtypeStruct((B, S, FOX_W), BF16), jax.ShapeDtypeStruct((B, S, LANES), F32),
                   jax.ShapeDtypeStruct((B, S, FOX_W), BF16), jax.ShapeDtypeStruct((B, S, FOX_W), BF16),
                   jax.ShapeDtypeStruct((B, S, LANES), F32)],
        grid=(B, nq, nq),
        in_specs=[qside(FOX_W, qcol),
                  pl.BlockSpec((None, tq, FOX_W), lambda b, j, i: (b, j, kcol)),
                  pl.BlockSpec((None, tq, FOX_W), lambda b, j, i: (b, j, vcol)),
                  qside(FOX_W), qside(LANES), qside(LANES),
                  pl.BlockSpec((None, 8, tq), lambda b, j, i: (b, 0, j))],
        out_specs=[pl.BlockSpec((None, S, FOX_W), lambda b, j, i: (b, 0, 0)),
                   pl.BlockSpec((None, S, LANES), lambda b, j, i: (b, 0, 0)),
                   kspec, kspec, pl.BlockSpec((None, tq, LANES), lambda b, j, i: (b, j, 0))],
        scratch_shapes=[pltpu.VMEM((tq, tq), F32), pltpu.VMEM((tq, tq), F32), pltpu.VMEM((tq, tq), BF16),
                        pltpu.VMEM((tq, tq), BF16), pltpu.VMEM((FOX_HEADS, S, LANES), F32),
                        pltpu.VMEM((FOX_HEADS, tq, LANES), F32), pltpu.VMEM((tq, FOX_W), F32)],
        compiler_params=_params(("parallel", "arbitrary", "arbitrary")),
    )(pm3, pm3, pm3, do, delta, lse, cumT)


def _fox_delta(do, o32, T, S):
    def fn(dov, ov):
        prod = dov.astype(F32) * ov
        lane = lax.broadcasted_iota(jnp.int32, (dov.shape[0], LANES), 1)
        delta = jnp.zeros((dov.shape[0], LANES), F32)
        for h in range(FOX_HEADS):
            hs = slice(HEAD_DIM * h, HEAD_DIM * (h + 1))
            delta = jnp.where(lane == h, jnp.sum(prod[:, hs], axis=-1, keepdims=True), delta)
        return [delta]

    return _rowwise("fox_delta", fn, T, _div(S, 512, 8), [(do, "row", None), (o32, "row", None)], [("row", LANES, F32)], S)[0]


def _alibi_slope(group, head):
    return 2.0 ** (-ALIBI_MAX_BIAS * (group * DIL_HPG + head + 1) / (N_DIL * DIL_HPG))


def _dil_tiles(q, k_cur, k_prev, slope, dilation, has_prev):
    qi = lax.broadcasted_iota(jnp.int32, (BLOCK, BLOCK), 0)
    ki = lax.broadcasted_iota(jnp.int32, (BLOCK, BLOCK), 1)
    rel = (qi - ki).astype(F32)
    nt = (((1,), (1,)), ((), ()))
    s_cur = lax.dot_general(q, k_cur, nt, preferred_element_type=F32) * SCALE - (slope * dilation) * rel
    s_cur = jnp.where(ki <= qi, s_cur, NEG_INF)
    s_prev = lax.dot_general(q, k_prev, nt, preferred_element_type=F32) * SCALE - (slope * dilation) * (rel + BLOCK)
    s_prev = jnp.where((ki >= qi) & has_prev, s_prev, NEG_INF)
    return s_cur, s_prev


def _dil_forward(group, pmv, nmb, qa_blk, B, S):
    _, dilation = DIL_GROUPS[group]
    sub = S // dilation
    nb = sub // BLOCK
    qb, kb, vb = qa_blk + group, qa_blk + N_DIL + group, qa_blk + 2 * N_DIL + group

    def body(q_ref, kc_ref, kp_ref, vc_ref, vp_ref, o_ref, lse_ref):
        has_prev = pl.program_id(2) > 0
        lse_ref[...] = jnp.zeros_like(lse_ref)
        for h in range(DIL_HPG):
            hs = slice(HEAD_DIM * h, HEAD_DIM * (h + 1))
            s_cur, s_prev = _dil_tiles(q_ref[:, hs], kc_ref[:, hs], kp_ref[:, hs], _alibi_slope(group, h), dilation, has_prev)
            m = jnp.maximum(jnp.max(s_cur, axis=-1, keepdims=True), jnp.max(s_prev, axis=-1, keepdims=True))
            p_cur, p_prev = jnp.exp(s_cur - m), jnp.exp(s_prev - m)
            l = jnp.sum(p_cur, axis=-1, keepdims=True) + jnp.sum(p_prev, axis=-1, keepdims=True)
            nn = (((1,), (0,)), ((), ()))
            o = (lax.dot_general(p_cur.astype(BF16), vc_ref[:, hs], nn, preferred_element_type=F32)
                 + lax.dot_general(p_prev.astype(BF16), vp_ref[:, hs], nn, preferred_element_type=F32))
            o_ref[:, hs] = o / l
            lse_ref[:, h:h + 1] = m + jnp.log(l)

    def cur(col):
        return pl.BlockSpec((None, BLOCK, DIL_GW), lambda b, r, n: (b, n, r * nmb + col))

    def prev(col):
        return pl.BlockSpec((None, BLOCK, DIL_GW), lambda b, r, n: (b, jnp.maximum(n - 1, 0), r * nmb + col))

    return pl.pallas_call(
        body, name=f"dil_forward_{group}",
        out_shape=[jax.ShapeDtypeStruct((B, sub, dilation * DIL_GW), F32), jax.ShapeDtypeStruct((B, sub, dilation * LANES), F32)],
        grid=(B, dilation, nb),
        in_specs=[cur(qb), cur(kb), prev(kb), cur(vb), prev(vb)],
        out_specs=[pl.BlockSpec((None, BLOCK, DIL_GW), lambda b, r, n: (b, n, r)),
                   pl.BlockSpec((None, BLOCK, LANES), lambda b, r, n: (b, n, r))],
        compiler_params=_params(("parallel", "parallel", "arbitrary")),
    )(pmv, pmv, pmv, pmv, pmv)


def _dil_combine(os_, lses, T, S):
    def fn(o0, o1, o2, l0, l1, l2):
        m = jnp.maximum(jnp.maximum(l0, l1), l2)
        e0, e1, e2 = jnp.exp(l0 - m), jnp.exp(l1 - m), jnp.exp(l2 - m)
        tot = e0 + e1 + e2
        w0, w1, w2 = e0 / tot, e1 / tot, e2 / tot
        parts = []
        for h in range(DIL_HPG):
            hs = slice(HEAD_DIM * h, HEAD_DIM * (h + 1))
            parts.append(w0[:, h:h + 1] * o0[:, hs] + w1[:, h:h + 1] * o1[:, hs] + w2[:, h:h + 1] * o2[:, hs])
        return [jnp.concatenate(parts, axis=1), m + jnp.log(tot)]

    ins = [(a, "row", None) for a in os_] + [(a, "row", None) for a in lses]
    return _rowwise("dil_combine", fn, T, _div(S, 512, 8), ins, [("row", DIL_GW, BF16), ("row", LANES, F32)], S)


def _dil_delta(dy, y, T, S):
    def fn(dyv, yv):
        prod = dyv * yv.astype(F32)
        lane = lax.broadcasted_iota(jnp.int32, (dyv.shape[0], LANES), 1)
        delta = jnp.zeros((dyv.shape[0], LANES), F32)
        for h in range(DIL_HPG):
            hs = slice(HEAD_DIM * h, HEAD_DIM * (h + 1))
            delta = jnp.where(lane == h, jnp.sum(prod[:, hs], axis=-1, keepdims=True), delta)
        return [delta, dyv]

    return _rowwise("dil_delta", fn, T, _div(S, 512, 8), [(dy, "row", None), (y, "row", None)],
                    [("row", LANES, F32), ("row", DIL_GW, BF16)], S)


def _dil_dq(group, pmv, nmb, qa_blk, Lv, dyv, deltav, B, S):
    _, dilation = DIL_GROUPS[group]
    sub = S // dilation
    nb = sub // BLOCK
    qb, kb, vb = qa_blk + group, qa_blk + N_DIL + group, qa_blk + 2 * N_DIL + group

    def body(q_ref, kc_ref, kp_ref, vc_ref, vp_ref, L_ref, dy_ref, dl_ref, dq_ref):
        has_prev = pl.program_id(2) > 0
        nt = (((1,), (1,)), ((), ()))
        nn = (((1,), (0,)), ((), ()))
        for h in range(DIL_HPG):
            hs = slice(HEAD_DIM * h, HEAD_DIM * (h + 1))
            s_cur, s_prev = _dil_tiles(q_ref[:, hs], kc_ref[:, hs], kp_ref[:, hs], _alibi_slope(group, h), dilation, has_prev)
            L, delta, dyh = L_ref[:, h:h + 1], dl_ref[:, h:h + 1], dy_ref[:, hs]
            ds_cur = jnp.exp(s_cur - L) * (lax.dot_general(dyh, vc_ref[:, hs], nt, preferred_element_type=F32) - delta)
            ds_prev = jnp.exp(s_prev - L) * (lax.dot_general(dyh, vp_ref[:, hs], nt, preferred_element_type=F32) - delta)
            dq = (lax.dot_general(ds_cur.astype(BF16), kc_ref[:, hs], nn, preferred_element_type=F32)
                  + lax.dot_general(ds_prev.astype(BF16), kp_ref[:, hs], nn, preferred_element_type=F32))
            dq_ref[:, hs] = (dq * SCALE).astype(dq_ref.dtype)

    def cur(col):
        return pl.BlockSpec((None, BLOCK, DIL_GW), lambda b, r, n: (b, n, r * nmb + col))

    def prev(col):
        return pl.BlockSpec((None, BLOCK, DIL_GW), lambda b, r, n: (b, jnp.maximum(n - 1, 0), r * nmb + col))

    own = pl.BlockSpec((None, BLOCK, DIL_GW), lambda b, r, n: (b, n, r))
    own128 = pl.BlockSpec((None, BLOCK, LANES), lambda b, r, n: (b, n, r))
    return pl.pallas_call(
        body, name=f"dil_dq_{group}",
        out_shape=jax.ShapeDtypeStruct((B, sub, dilation * DIL_GW), BF16),
        grid=(B, dilation, nb),
        in_specs=[cur(qb), cur(kb), prev(kb), cur(vb), prev(vb), own128, own, own128],
        out_specs=own,
        compiler_params=_params(("parallel", "parallel", "arbitrary")),
    )(pmv, pmv, pmv, pmv, pmv, Lv, dyv, deltav)


def _dil_dkv(group, pmv, nmb, qa_blk, Lv, dyv, deltav, B, S):
    _, dilation = DIL_GROUPS[group]
    sub = S // dilation
    nb = sub // BLOCK
    qb, kb, vb = qa_blk + group, qa_blk + N_DIL + group, qa_blk + 2 * N_DIL + group

    def body(k_ref, v_ref, q0_ref, q1_ref, L0_ref, L1_ref, dy0_ref, dy1_ref, dl0_ref, dl1_ref, dk_ref, dv_ref):
        has_next = pl.program_id(2) < nb - 1
        qi = lax.broadcasted_iota(jnp.int32, (BLOCK, BLOCK), 0)
        ki = lax.broadcasted_iota(jnp.int32, (BLOCK, BLOCK), 1)
        rel = (qi - ki).astype(F32)
        nt = (((1,), (1,)), ((), ()))
        tn = (((0,), (0,)), ((), ()))
        for h in range(DIL_HPG):
            hs = slice(HEAD_DIM * h, HEAD_DIM * (h + 1))
            bias = _alibi_slope(group, h) * dilation
            kh, vh, q0, q1 = k_ref[:, hs], v_ref[:, hs], q0_ref[:, hs], q1_ref[:, hs]
            s0 = lax.dot_general(q0, kh, nt, preferred_element_type=F32) * SCALE - bias * rel
            s0 = jnp.where(ki <= qi, s0, NEG_INF)
            s1 = lax.dot_general(q1, kh, nt, preferred_element_type=F32) * SCALE - bias * (rel + BLOCK)
            s1 = jnp.where((ki >= qi) & has_next, s1, NEG_INF)
            p0 = jnp.exp(s0 - L0_ref[:, h:h + 1])
            p1 = jnp.exp(s1 - L1_ref[:, h:h + 1])
            dy0, dy1 = dy0_ref[:, hs], dy1_ref[:, hs]
            ds0 = p0 * (lax.dot_general(dy0, vh, nt, preferred_element_type=F32) - dl0_ref[:, h:h + 1])
            ds1 = p1 * (lax.dot_general(dy1, vh, nt, preferred_element_type=F32) - dl1_ref[:, h:h + 1])
            dv = (lax.dot_general(p0.astype(BF16), dy0, tn, preferred_element_type=F32)
                  + lax.dot_general(p1.astype(BF16), dy1, tn, preferred_element_type=F32))
            dk = (lax.dot_general(ds0.astype(BF16), q0, tn, preferred_element_type=F32)
                  + lax.dot_general(ds1.astype(BF16), q1, tn, preferred_element_type=F32))
            dv_ref[:, hs] = dv.astype(dv_ref.dtype)
            dk_ref[:, hs] = (dk * SCALE).astype(dk_ref.dtype)

    def cur(col):
        return pl.BlockSpec((None, BLOCK, DIL_GW), lambda b, r, n: (b, n, r * nmb + col))

    def nxt(col):
        return pl.BlockSpec((None, BLOCK, DIL_GW), lambda b, r, n: (b, jnp.minimum(n + 1, nb - 1), r * nmb + col))

    own = pl.BlockSpec((None, BLOCK, DIL_GW), lambda b, r, n: (b, n, r))
    own_next = pl.BlockSpec((None, BLOCK, DIL_GW), lambda b, r, n: (b, jnp.minimum(n + 1, nb - 1), r))
    own128 = pl.BlockSpec((None, BLOCK, LANES), lambda b, r, n: (b, n, r))
    own128_next = pl.BlockSpec((None, BLOCK, LANES), lambda b, r, n: (b, jnp.minimum(n + 1, nb - 1), r))
    shape = jax.ShapeDtypeStruct((B, sub, dilation * DIL_GW), BF16)
    return pl.pallas_call(
        body, name=f"dil_dkv_{group}",
        out_shape=[shape, shape],
        grid=(B, dilation, nb),
        in_specs=[cur(kb), cur(vb), cur(qb), nxt(qb), own128, own128_next, own, own_next, own128, own128_next],
        out_specs=[own, own],
        compiler_params=_params(("parallel", "parallel", "arbitrary")),
    )(pmv, pmv, pmv, pmv, Lv, Lv, dyv, dyv, deltav, deltav)


def _ada_forward(c_all, w, b):
    n, D = c_all.shape
    cl = w.shape[1]

    def body(c_ref, w_ref, b_ref, o_ref, ca_ref):
        cv = c_ref[...]
        ca = (cv * _sigmoid(cv)).astype(BF16)
        ca_ref[...] = ca
        o_ref[...] = jnp.dot(ca, w_ref[...].astype(BF16), preferred_element_type=F32) + b_ref[...]

    return pl.pallas_call(
        body, name="ada_forward",
        out_shape=[jax.ShapeDtypeStruct((n, cl), F32), jax.ShapeDtypeStruct((n, D), BF16)],
        compiler_params=_params(),
    )(c_all, w, b)


def _ada_backward(ca, dmod_cols, dmod_all):
    n, D = ca.shape
    cl = dmod_cols.shape[1]

    def body(ca_ref, dc_ref, da_ref, gw_ref, gb_ref):
        gw_ref[...] = lax.dot_general(ca_ref[...], dc_ref[...].astype(BF16), (((0,), (0,)), ((), ())), preferred_element_type=F32)
        gb_ref[...] = _colsum(da_ref[...])

    return pl.pallas_call(
        body, name="ada_backward",
        out_shape=[jax.ShapeDtypeStruct((D, cl), F32), jax.ShapeDtypeStruct((1, dmod_all.shape[1]), F32)],
        compiler_params=_params(),
    )(ca, dmod_cols, dmod_all)


def _sum_devices(v):
    def body(v_ref, o_ref):
        tot = v_ref[0]
        for k in range(1, N_DEV):
            tot = tot + v_ref[k]
        o_ref[...] = tot

    return pl.pallas_call(body, name="sum_devices", out_shape=jax.ShapeDtypeStruct(v.shape[1:], F32))(v)


def _adamw(name, w, g, m, v):
    rows, cols = w.shape
    tr = _div(rows, 256, 8)

    def body(w_ref, g_ref, m_ref, v_ref, d_ref, nm_ref, nv_ref):
        gv = g_ref[...]
        nm = ADAM_B1 * m_ref[...] + (1.0 - ADAM_B1) * gv
        nv = ADAM_B2 * v_ref[...] + (1.0 - ADAM_B2) * (gv * gv)
        m_hat = nm / (1.0 - ADAM_B1 ** ADAM_STEP)
        v_hat = nv / (1.0 - ADAM_B2 ** ADAM_STEP)
        d_ref[...] = -ADAM_LR * (m_hat / (jnp.sqrt(v_hat) + ADAM_EPS) + ADAM_WD * w_ref[...])
        nm_ref[...] = nm
        nv_ref[...] = nv

    spec = pl.BlockSpec((tr, cols), lambda i: (i, 0))
    shape = jax.ShapeDtypeStruct((rows, cols), F32)
    return pl.pallas_call(
        body, name=name, out_shape=[shape, shape, shape], grid=(rows // tr,),
        in_specs=[spec, spec, spec, spec], out_specs=[spec, spec, spec],
        compiler_params=_params(("arbitrary",)),
    )(w, g, m, v)


def _pad_rows(a, rows):
    return a if a.shape[0] == rows else jnp.pad(a, ((0, rows - a.shape[0]), (0, 0)))


class _Packed:
    def __init__(self, kind, local_shape, D):
        self.kind, self.local_shape, self.D = kind, local_shape, D
        r, c = local_shape
        self.rows = {"T": c, "N": r, "F": r * c // D}[kind]
        self.rows_pad = -(-self.rows // ROW_ALIGN) * ROW_ALIGN

    def pack_local(self, w):
        if self.kind == "T":
            w = w.T
        elif self.kind == "F":
            w = w.reshape(self.rows, self.D)
        return _pad_rows(w, self.rows_pad)

    def full(self, gathered):
        g = gathered[:, :self.rows]
        if self.kind == "F":
            r, c = self.local_shape
            return g.reshape(N_DEV, r, c).transpose(1, 0, 2).reshape(r, N_DEV * c)
        return g.reshape(N_DEV * self.rows, self.D)

    def pack_grad(self, gfull):
        if self.kind == "F":
            r, c = self.local_shape
            g = gfull.reshape(r, N_DEV, c).transpose(1, 0, 2).reshape(N_DEV, self.rows, self.D)
        else:
            g = gfull.reshape(N_DEV, self.rows, self.D)
        if self.rows_pad != self.rows:
            g = jnp.pad(g, ((0, 0), (0, self.rows_pad - self.rows), (0, 0)))
        return g

    def unpack_local(self, g):
        g = g[:self.rows]
        if self.kind == "T":
            return g.T
        if self.kind == "F":
            return g.reshape(self.local_shape)
        return g


BIG = ["ffn1_w_gate", "ffn1_w_up", "ffn1_w_down", "w_in", "w_branch_a", "w_branch_b", "w_out",
       "ffn2_w_gate", "ffn2_w_up", "ffn2_w_down"]
BIG_KIND = {"ffn1_w_gate": "T", "ffn1_w_up": "T", "ffn1_w_down": "N", "w_in": "T", "w_branch_a": "F", "w_branch_b": "F",
            "w_out": "N", "ffn2_w_gate": "T", "ffn2_w_up": "T", "ffn2_w_down": "N"}
SMALL = ["ada_b", "norm_ffn1", "norm_mix", "forget_bias", "norm_ffn2", "norm_final"]


def kernel(x, c, ada_w, ada_b, norm_ffn1, ffn1_w_gate, ffn1_w_up, ffn1_w_down, norm_mix, w_in, forget_bias, w_branch_a, w_branch_b, w_out, norm_ffn2, ffn2_w_gate, ffn2_w_up, ffn2_w_down, norm_final, loss_target, m_ada_w, m_ada_b, m_norm_ffn1, m_ffn1_w_gate, m_ffn1_w_up, m_ffn1_w_down, m_norm_mix, m_w_in, m_forget_bias, m_w_branch_a, m_w_branch_b, m_w_out, m_norm_ffn2, m_ffn2_w_gate, m_ffn2_w_up, m_ffn2_w_down, m_norm_final, v_ada_w, v_ada_b, v_norm_ffn1, v_ffn1_w_gate, v_ffn1_w_up, v_ffn1_w_down, v_norm_mix, v_w_in, v_forget_bias, v_w_branch_a, v_w_branch_b, v_w_out, v_norm_ffn2, v_ffn2_w_gate, v_ffn2_w_up, v_ffn2_w_down, v_norm_final):
    args = dict(locals())
    B, S, D = x.shape
    T = B * S
    cl = ada_w.shape[2]
    n_in = w_in.shape[2] * N_DEV
    nm = 2 * D + 3 * FOX_W + 3 * DIL_W
    nmp = -(-nm // 512) * 512
    GA, GB, QB, QA = 0, D, 2 * D, 2 * D + 3 * FOX_W
    xpos, ypos, cpos = _position()
    me = 4 * xpos + 2 * ypos + cpos

    packs = {n: _Packed(BIG_KIND[n], args[n].shape[1:], D) for n in BIG}
    offs, r = {}, 0
    for n in BIG:
        offs[n] = r
        r += packs[n].rows_pad
    pad_rows = -r % PACK_ROW_QUANTUM
    p_local = jnp.concatenate([packs[n].pack_local(args[n][0]).astype(BF16) for n in BIG]
                              + [jnp.zeros((pad_rows, D), BF16)], axis=0)
    gathered = _weight_allgather(p_local)
    W = {n: packs[n].full(gathered[:, offs[n]:offs[n] + packs[n].rows_pad]) for n in BIG}
    winT = W["w_in"]
    o_f = 3 * DIL_W + 3 * FOX_W
    wmT = jnp.concatenate([winT[o_f + 8:], winT[3 * DIL_W:o_f], winT[:3 * DIL_W], jnp.zeros((nmp - nm, D), BF16)], axis=0)
    wfT = jnp.concatenate([winT[o_f:o_f + 8], jnp.zeros((LANES - 8, D), BF16)], axis=0)

    c_all = _small_allgather(c, "gather_c").reshape(N_DEV * B, D)
    b_cols = lax.dynamic_slice(ada_b, (0, me * cl), (1, cl))
    mod_cols, c_act = _ada_forward(c_all, ada_w[0], b_cols)
    mod_all = _small_allgather(mod_cols, "gather_mod").transpose(1, 0, 2).reshape(N_DEV * B, N_MOD * D)
    mod = lax.dynamic_slice(mod_all, (me * B, 0), (B, N_MOD * D)).reshape(B, N_MOD, 1, D)
    sh1, sc1, gt1, sh2, sc2, gt2, sh3, sc3, gt3 = [mod[:, i] for i in range(N_MOD)]

    x0 = x.reshape(T, D)
    x1, saved1 = _ffn_forward("ffn1", x0, norm_ffn1, sh1, sc1, gt1, W["ffn1_w_gate"], W["ffn1_w_up"], W["ffn1_w_down"], S)

    h2 = _normmod("mix_normmod", x1, norm_mix, sc2, sh2, S)
    tm1k = _div(T, 1024, 8)
    pm = _matmul("mix_proj", "nt", [[(h2, wmT)]], T, nmp, D, tm1k, 512, D, [BF16])[0]
    fraw = _matmul("mix_proj_f", "nt", [[(h2, wfT)]], T, LANES, D, tm1k, LANES, D, [F32])[0]
    fb = jnp.pad(forget_bias, ((0, 0), (0, LANES - FOX_HEADS)))

    def forget_fn(fr, fbv):
        fl = fr + fbv
        lane = lax.broadcasted_iota(jnp.int32, fl.shape, 1)
        ls = jnp.minimum(fl, 0.0) - jnp.log(1.0 + jnp.exp(-jnp.abs(fl)))
        return [jnp.where(lane < FOX_HEADS, ls, 0.0), fl]

    tms = _div(S, 512, 8)
    logsig, flog = _rowwise("forget_gate", forget_fn, T, tms, [(fraw, "row", None), (fb, "vec", None)],
                            [("row", LANES, F32), ("row", LANES, F32)], S)
    cum = _cumsum(logsig.reshape(B, S, LANES))
    cumT = cum[:, :, :8].transpose(0, 2, 1)
    pm3 = pm.reshape(B, S, nmp)
    tq = _div(S, 512, LANES)
    qcol, kcol, vcol = QB // FOX_W, QB // FOX_W + 1, QB // FOX_W + 2
    o_b, o_b32, lse_b = _fox_fwd(pm3, cumT, qcol, kcol, vcol, tq)
    y_b = o_b.reshape(T, FOX_W)

    nmb, qa_blk = nmp // DIL_GW, QA // DIL_GW
    pmvs = [pm.reshape(B, S // d, d * nmp) for _, d in DIL_GROUPS]
    dil_o, dil_lse = [], []
    for g in range(N_DIL):
        o_g, lse_g = _dil_forward(g, pmvs[g], nmb, qa_blk, B, S)
        dil_o.append(o_g.reshape(T, DIL_GW))
        dil_lse.append(lse_g.reshape(T, LANES))
    y_a, L_a = _dil_combine(dil_o, dil_lse, T, S)

    wa, wb, wout = W["w_branch_a"], W["w_branch_b"], W["w_out"]
    tnd = _div(D, 512, LANES)
    yap = _matmul("mix_branch_a", "nn", [[(y_a, wa)]], T, D, DIL_GW, tm1k, tnd, DIL_GW, [BF16])[0]

    def merge(accs, ex):
        yapv, gav, gbv = ex
        ybp = accs[0]
        return [ybp, _sigmoid(gav.astype(F32)) * yapv.astype(F32) + _sigmoid(gbv.astype(F32)) * ybp]

    ybp, merged = _matmul("mix_branch_b", "nn", [[(y_b, wb)]], T, D, FOX_W, tm1k, tnd, FOX_W, [BF16, BF16],
                          extras=[(yap, "tile", 0), (pm, "tile", GA), (pm, "tile", GB)], epilogue=merge)

    def out_proj(accs, ex):
        xv, gtv = ex
        return [xv + gtv * accs[0], accs[0]]

    x2, ymix = _matmul("mix_out", "nn", [[(merged, wout)]], T, D, D, tms, tnd, D, [F32, BF16],
                       extras=[(x1, "tile", 0), (gt2, "brow", 0)], epilogue=out_proj, rows_per_example=S)

    x3, saved3 = _ffn_forward("ffn2", x2, norm_ffn2, sh3, sc3, gt3, W["ffn2_w_gate"], W["ffn2_w_up"], W["ffn2_w_down"], S)

    dx3, loss_b, dg_final = _loss_head(x3, loss_target.reshape(T, D), norm_final.reshape(1, D), S)
    dx2, (dsh3, dsc3, dgt3, dg3), (dwg2, dwu2, dwd2) = _ffn_backward(
        "ffn2", dx3, saved3, norm_ffn2, sc3, gt3, W["ffn2_w_gate"], W["ffn2_w_up"], W["ffn2_w_down"], S)

    dym, dgt2 = _gate_grad("mix_gate_grad", dx2, ymix, gt2, 1.0, S)
    tkw = _div(T, 1024, LANES)
    dwout = _matmul("mix_dw_out", "tn", [[(merged, dym)]], D, D, T, _div(D, 1024, LANES), D, tkw, [F32])[0]

    def merge_grad(accs, ex):
        gav, gbv, yapv, ybpv = [e.astype(F32) for e in ex]
        dm = accs[0]
        sga, sgb = _sigmoid(gav), _sigmoid(gbv)
        return [dm * sga, dm * sgb, dm * yapv * sga * (1.0 - sga), dm * ybpv * sgb * (1.0 - sgb)]

    dyap, dybp, dga, dgb = _matmul("mix_merge_grad", "nt", [[(dym, wout)]], T, D, D, tm1k, tnd, D, [BF16] * 4,
                                   extras=[(pm, "tile", GA), (pm, "tile", GB), (yap, "tile", 0), (ybp, "tile", 0)],
                                   epilogue=merge_grad)
    dwa = _matmul("mix_dw_a", "tn", [[(y_a, dyap)]], DIL_GW, D, T, DIL_GW, D, tkw, [F32])[0]
    dwb = _matmul("mix_dw_b", "tn", [[(y_b, dybp)]], FOX_W, D, T, FOX_W, D, tkw, [F32])[0]
    dy_a = _matmul("mix_dy_a", "nt", [[(dyap, wa)]], T, DIL_GW, D, tm1k, DIL_GW, D, [F32])[0]
    dy_b = _matmul("mix_dy_b", "nt", [[(dybp, wb)]], T, FOX_W, D, tm1k, FOX_W, D, [BF16])[0]

    do3 = dy_b.reshape(B, S, FOX_W)
    delta_b = _fox_delta(dy_b, o_b32.reshape(T, FOX_W), T, S).reshape(B, S, LANES)
    dq_b, ds_rows, dk_b, dv_b, ds_cols = _fox_bwd(pm3, do3, delta_b, lse_b, cumT, qcol, kcol, vcol, tq)

    delta_a, dy_a16 = _dil_delta(dy_a, y_a, T, S)
    dqs, dks, dvs = [], [], []
    for g, (_, d) in enumerate(DIL_GROUPS):
        Lv = L_a.reshape(B, S // d, d * LANES)
        dyv = dy_a16.reshape(B, S // d, d * DIL_GW)
        dlv = delta_a.reshape(B, S // d, d * LANES)
        dqs.append(_dil_dq(g, pmvs[g], nmb, qa_blk, Lv, dyv, dlv, B, S).reshape(T, DIL_GW))
        dk_g, dv_g = _dil_dkv(g, pmvs[g], nmb, qa_blk, Lv, dyv, dlv, B, S)
        dks.append(dk_g.reshape(T, DIL_GW))
        dvs.append(dv_g.reshape(T, DIL_GW))

    dcum = ds_rows - ds_cols
    dcum_run = _cumsum(dcum)
    dcum_tot = dcum_run[:, S - 1:S, :]

    def forget_grad_fn(run, dcv, fl, tot):
        lane = lax.broadcasted_iota(jnp.int32, fl.shape, 1)
        df = jnp.where(lane < FOX_HEADS, (tot - run + dcv) * _sigmoid(-fl), 0.0)
        return [df, _colsum(df)]

    df16, dfb = _rowwise("forget_gate_grad", forget_grad_fn, T, tms,
                         [(dcum_run.reshape(T, LANES), "row", None), (dcum.reshape(T, LANES), "row", None), (flog, "row", None),
                          (dcum_tot, "bvec", None)],
                         [("row", LANES, BF16), ("bacc", LANES, F32)], S)

    dpm = jnp.concatenate([dga, dgb, dq_b.reshape(T, FOX_W), dk_b.reshape(T, FOX_W), dv_b.reshape(T, FOX_W)]
                          + dqs + dks + dvs + ([jnp.zeros((T, nmp - nm), BF16)] if nmp > nm else []), axis=1)
    tmn = _div(nmp, 512, LANES)
    dwmT = _matmul("mix_dw_in", "tn", [[(dpm, h2)]], nmp, D, T, tmn, D, tkw, [F32])[0]
    dwfT = _matmul("mix_dw_f", "tn", [[(df16, h2)]], LANES, D, T, LANES, D, tkw, [F32])[0]
    dh2f = _matmul("mix_dh_f", "nn", [[(df16, wfT)]], T, D, LANES, tm1k, tnd, LANES, [F32])[0]

    def add_tile(accs, ex):
        return [accs[0] + ex[0]]

    dh2 = _matmul("mix_dh", "nn", [[(dpm, wmT)]], T, D, nmp, _div(T, 512, 8), tnd, _div(nmp, 2048, LANES), [F32],
                  extras=[(dh2f, "tile", 0)], epilogue=add_tile)[0]
    dx1, dsh2, dsc2, dgmix = _normmod_bwd("mix_normmod_bwd", x1, norm_mix, sc2, dh2, dx2, S)

    dx0, (dsh1, dsc1, dgt1, dg1), (dwg1, dwu1, dwd1) = _ffn_backward(
        "ffn1", dx1, saved1, norm_ffn1, sc1, gt1, W["ffn1_w_gate"], W["ffn1_w_up"], W["ffn1_w_down"], S)
    grad_x = dx0.reshape(B, S, D)

    dmod = jnp.concatenate([dsh1, dsc1, dgt1, dsh2, dsc2, dgt2, dsh3, dsc3, dgt3], axis=1).reshape(B, N_MOD * D)
    dmod_all = _small_allgather(dmod, "gather_dmod").reshape(N_DEV * B, N_MOD * D)
    dmod_cols = lax.dynamic_slice(dmod_all, (0, me * cl), (N_DEV * B, cl))
    g_ada_w, g_ada_b = _ada_backward(c_act, dmod_cols, dmod_all)

    fbg = jnp.sum(dfb, axis=0)
    small = jnp.concatenate([jnp.sum(dg1, axis=0), jnp.sum(dgmix, axis=0), jnp.sum(dg3, axis=0), jnp.sum(dg_final, axis=0),
                             fbg, jnp.sum(loss_b, axis=0)], axis=1)
    small = _sum_devices(_small_allgather(small, "gather_small"))
    g_small = {"norm_ffn1": small[:, 0:D], "norm_mix": small[:, D:2 * D], "norm_ffn2": small[:, 2 * D:3 * D],
               "norm_final": small[:, 3 * D:4 * D], "forget_bias": small[:, 4 * D:4 * D + FOX_HEADS], "ada_b": g_ada_b}
    loss = small[0, 4 * D + LANES]

    dwinT = jnp.concatenate([dwmT[QA:QA + 3 * DIL_W], dwmT[QB:QB + 3 * FOX_W], dwfT[:8], dwmT[GA:2 * D]], axis=0)
    gfull = {"ffn1_w_gate": dwg1, "ffn1_w_up": dwu1, "ffn1_w_down": dwd1, "w_in": dwinT, "w_branch_a": dwa, "w_branch_b": dwb,
             "w_out": dwout, "ffn2_w_gate": dwg2, "ffn2_w_up": dwu2, "ffn2_w_down": dwd2}
    g_packed = jnp.concatenate([packs[n].pack_grad(gfull[n]) for n in BIG] + [jnp.zeros((N_DEV, pad_rows, D), F32)], axis=1)
    g_local = _reduce_scatter(g_packed)
    grads = {n: packs[n].unpack_local(g_local[offs[n]:offs[n] + packs[n].rows_pad])[None] for n in BIG}
    grads["ada_w"] = g_ada_w[None]

    delta, new_m, new_v = {}, {}, {}
    for n in ["ada_w"] + BIG:
        shp = args[n].shape
        d_, m_, v_ = _adamw(f"adamw_{n}", args[n][0], grads[n][0], args["m_" + n][0], args["v_" + n][0])
        delta[n], new_m[n], new_v[n] = d_.reshape(shp), m_.reshape(shp), v_.reshape(shp)
    sizes = [args[n].size for n in SMALL]
    tot = sum(sizes)
    padded = -(-tot // (8 * LANES)) * (8 * LANES)

    def flat(get):
        v = jnp.concatenate([get(n).reshape(-1) for n in SMALL])
        return jnp.pad(v, (0, padded - tot)).reshape(8, padded // 8)

    d_s, m_s, v_s = _adamw("adamw_small", flat(lambda n: args[n]), flat(lambda n: g_small[n]), flat(lambda n: args["m_" + n]),
                           flat(lambda n: args["v_" + n]))
    o = 0
    for n, sz in zip(SMALL, sizes):
        shp = args[n].shape
        grads[n] = g_small[n].reshape(shp)
        delta[n] = d_s.reshape(-1)[o:o + sz].reshape(shp)
        new_m[n] = m_s.reshape(-1)[o:o + sz].reshape(shp)
        new_v[n] = v_s.reshape(-1)[o:o + sz].reshape(shp)
        o += sz

    order = ["ada_w", "ada_b", "norm_ffn1", "ffn1_w_gate", "ffn1_w_up", "ffn1_w_down", "norm_mix", "w_in", "forget_bias",
             "w_branch_a", "w_branch_b", "w_out", "norm_ffn2", "ffn2_w_gate", "ffn2_w_up", "ffn2_w_down", "norm_final"]
    return (loss, grad_x, *[grads[n] for n in order], *[delta[n] for n in order], *[new_m[n] for n in order],
            *[new_v[n] for n in order])
```

```python
import functools
import math

import jax
import jax.numpy as jnp
from jax import lax
from jax.experimental import pallas as pl
from jax.experimental.pallas import tpu as pltpu

F32 = jnp.float32
BF16 = jnp.bfloat16
MESH = pl.DeviceIdType.MESH
ANY = pl.BlockSpec(memory_space=pl.ANY)
VMEM_SPEC = pl.BlockSpec(memory_space=pltpu.VMEM)

N_DEV = 8
HEAD_DIM = 64
BLOCK = 128
DIL_GROUPS = ((128, 1), (512, 4), (2048, 16))
N_DIL = len(DIL_GROUPS)
DIL_HPG = 4
DIL_GW = DIL_HPG * HEAD_DIM
DIL_W = N_DIL * DIL_GW
FOX_HEADS = 8
FOX_W = FOX_HEADS * HEAD_DIM
N_MOD = 9
RMS_EPS = 1e-6
ALIBI_MAX_BIAS = 8.0
NEG_INF = -1e30
ADAM_LR, ADAM_B1, ADAM_B2, ADAM_EPS, ADAM_WD, ADAM_STEP = 0.001, 0.9, 0.999, 1e-08, 0.01, 10
V7X_VMEM_LIMIT = 52 * 1024 * 1024
LANES = 128
ROW_ALIGN = 16
PACK_ROW_QUANTUM = 256
FOX_STRIP = 32
SCALE = 1.0 / math.sqrt(HEAD_DIM)


def _div(dim, target, quantum):
    best = None
    for t in range(quantum, min(dim, target) + 1, quantum):
        if dim % t == 0:
            best = t
    return best or dim


def _params(sem=None):
    return pltpu.CompilerParams(dimension_semantics=sem, vmem_limit_bytes=V7X_VMEM_LIMIT)


def _sigmoid(x):
    return 1.0 / (1.0 + jnp.exp(-x))


def _position():
    x, y, c = lax.axis_index("x"), lax.axis_index("y"), lax.axis_index("c")
    return x, y, c


def _small_allgather(v, name):
    rows, cols = v.shape

    def body(v_ref, out_ref, send_sems, recv_sems):
        x, y, c = _position()
        me = 4 * x + 2 * y + c
        out_ref[me] = v_ref[...]

        def peer(k):
            return (1 - x if k & 4 else x, 1 - y if k & 2 else y, 1 - c if k & 1 else c)

        def copy(k, slot):
            return pltpu.make_async_remote_copy(
                src_ref=v_ref, dst_ref=out_ref.at[slot], send_sem=send_sems.at[k - 1], recv_sem=recv_sems.at[k - 1],
                device_id=peer(k), device_id_type=MESH)

        sends = [copy(k, me) for k in range(1, N_DEV)]
        for cp in sends:
            cp.start()
        for k in range(1, N_DEV):
            px, py, pc = peer(k)
            copy(k, 4 * px + 2 * py + pc).wait_recv()
        for cp in sends:
            cp.wait_send()

    return pl.pallas_call(
        body, name=name,
        out_shape=jax.ShapeDtypeStruct((N_DEV, rows, cols), v.dtype),
        in_specs=[VMEM_SPEC], out_specs=VMEM_SPEC,
        scratch_shapes=[pltpu.SemaphoreType.DMA((N_DEV - 1,)), pltpu.SemaphoreType.DMA((N_DEV - 1,))],
    )(v)


def _weight_allgather(p):
    rows, cols = p.shape

    def body(p_ref, out_ref, send_sems, recv_sems, local_sem):
        x, y, c = _position()
        me, sibling = (x, y, c), (x, y, 1 - c)
        chips = [(1 - x, y), (x, 1 - y), (1 - x, 1 - y)]

        def slot(px, py, pc):
            return out_ref.at[4 * px + 2 * py + pc]

        def copy(k, block, to, src=None):
            return pltpu.make_async_remote_copy(
                src_ref=slot(*block) if src is None else src, dst_ref=slot(*block),
                send_sem=send_sems.at[k], recv_sem=recv_sems.at[k], device_id=to, device_id_type=MESH)

        mine = pltpu.make_async_copy(p_ref, slot(*me), local_sem)
        mine.start()
        first = [copy(0, me, sibling, src=p_ref)]
        first += [copy(1 + j, me, (*chip, c), src=p_ref) for j, chip in enumerate(chips)]
        for cp in first:
            cp.start()
        passed = [copy(4 + j, (*chip, c), sibling) for j, chip in enumerate(chips)]
        for j, chip in enumerate(chips):
            copy(1 + j, (*chip, c), me).wait_recv()
            passed[j].start()
        copy(0, sibling, me).wait_recv()
        for j, chip in enumerate(chips):
            copy(4 + j, (*chip, 1 - c), me).wait_recv()
        for cp in first + passed:
            cp.wait_send()
        mine.wait()

    return pl.pallas_call(
        body, name="weight_allgather",
        out_shape=jax.ShapeDtypeStruct((N_DEV, rows, cols), p.dtype),
        in_specs=[ANY], out_specs=ANY,
        scratch_shapes=[pltpu.SemaphoreType.DMA((7,)), pltpu.SemaphoreType.DMA((7,)), pltpu.SemaphoreType.DMA],
    )(p)


def _grad_exchange_sibling(g):
    _, rows, cols = g.shape

    def body(g_ref, out_ref, send_sems, recv_sems):
        x, y, c = _position()
        sibling = (x, y, 1 - c)

        def copy(q):
            px, py = q >> 1, q & 1
            return pltpu.make_async_remote_copy(
                src_ref=g_ref.at[4 * px + 2 * py + (1 - c)], dst_ref=out_ref.at[q],
                send_sem=send_sems.at[q], recv_sem=recv_sems.at[q], device_id=sibling, device_id_type=MESH)

        copies = [copy(q) for q in range(4)]
        for cp in copies:
            cp.start()
        for cp in copies:
            cp.wait_recv()
        for cp in copies:
            cp.wait_send()

    return pl.pallas_call(
        body, name="grad_exchange_sibling",
        out_shape=jax.ShapeDtypeStruct((4, rows, cols), g.dtype),
        in_specs=[ANY], out_specs=ANY,
        scratch_shapes=[pltpu.SemaphoreType.DMA((4,)), pltpu.SemaphoreType.DMA((4,))],
    )(g)


def _grad_exchange_chips(s):
    _, rows, cols = s.shape

    def body(s_ref, out_ref, send_sems, recv_sems):
        x, y, c = _position()
        chips = [(1 - x, y), (x, 1 - y), (1 - x, 1 - y)]

        def copy(k):
            return pltpu.make_async_remote_copy(
                src_ref=s_ref.at[k], dst_ref=out_ref.at[k], send_sem=send_sems.at[k], recv_sem=recv_sems.at[k],
                device_id=(*chips[k], c), device_id_type=MESH)

        copies = [copy(k) for k in range(3)]
        for cp in copies:
            cp.start()
        for cp in copies:
            cp.wait_recv()
        for cp in copies:
            cp.wait_send()

    return pl.pallas_call(
        body, name="grad_exchange_chips",
        out_shape=jax.ShapeDtypeStruct((3, rows, cols), s.dtype),
        in_specs=[ANY], out_specs=ANY,
        scratch_shapes=[pltpu.SemaphoreType.DMA((3,)), pltpu.SemaphoreType.DMA((3,))],
    )(s)


def _chip_partial_sums(g, recv_sib, jj, qq):
    _, rows, cols = g.shape
    tr = _div(rows, 512, ROW_ALIGN)

    def body(jj_ref, qq_ref, g_ref, r_ref, o_ref):
        o_ref[...] = (g_ref[...] + r_ref[...]).astype(o_ref.dtype)

    return pl.pallas_call(
        body, name="chip_partial_sums",
        out_shape=jax.ShapeDtypeStruct((3, rows, cols), BF16),
        grid_spec=pltpu.PrefetchScalarGridSpec(
            num_scalar_prefetch=2, grid=(3, rows // tr),
            in_specs=[pl.BlockSpec((None, tr, cols), lambda k, i, jj, qq: (jj[k], i, 0)),
                      pl.BlockSpec((None, tr, cols), lambda k, i, jj, qq: (qq[k], i, 0))],
            out_specs=pl.BlockSpec((None, tr, cols), lambda k, i, jj, qq: (k, i, 0))),
        compiler_params=_params(("arbitrary", "arbitrary")),
    )(jj, qq, g, recv_sib)


def _own_partial_sum(g, recv_sib, jj, qq):
    _, rows, cols = g.shape
    tr = _div(rows, 512, ROW_ALIGN)

    def body(jj_ref, qq_ref, g_ref, r_ref, o_ref):
        o_ref[...] = g_ref[...] + r_ref[...]

    return pl.pallas_call(
        body, name="own_partial_sum",
        out_shape=jax.ShapeDtypeStruct((rows, cols), F32),
        grid_spec=pltpu.PrefetchScalarGridSpec(
            num_scalar_prefetch=2, grid=(rows // tr,),
            in_specs=[pl.BlockSpec((None, tr, cols), lambda i, jj, qq: (jj[0], i, 0)),
                      pl.BlockSpec((None, tr, cols), lambda i, jj, qq: (qq[0], i, 0))],
            out_specs=pl.BlockSpec((tr, cols), lambda i, jj, qq: (i, 0))),
        compiler_params=_params(("arbitrary",)),
    )(jj, qq, g, recv_sib)


def _final_grad_sum(own, recv):
    rows, cols = own.shape
    tr = _div(rows, 512, ROW_ALIGN)

    def body(o_ref, r_ref, out_ref):
        out_ref[...] = ((o_ref[...] + r_ref[0].astype(F32)) + r_ref[1].astype(F32)) + r_ref[2].astype(F32)

    return pl.pallas_call(
        body, name="final_grad_sum",
        out_shape=jax.ShapeDtypeStruct((rows, cols), F32),
        grid=(rows // tr,),
        in_specs=[pl.BlockSpec((tr, cols), lambda i: (i, 0)), pl.BlockSpec((3, tr, cols), lambda i: (0, i, 0))],
        out_specs=pl.BlockSpec((tr, cols), lambda i: (i, 0)),
        compiler_params=_params(("arbitrary",)),
    )(own, recv)


def _reduce_scatter(g):
    x, y, c = _position()
    chips = [(1 - x, y), (x, 1 - y), (1 - x, 1 - y)]
    jj = jnp.stack([4 * px + 2 * py + c for px, py in chips]).astype(jnp.int32)
    qq = jnp.stack([2 * px + py for px, py in chips]).astype(jnp.int32)
    jme = jnp.reshape(4 * x + 2 * y + c, (1,)).astype(jnp.int32)
    qme = jnp.reshape(2 * x + y, (1,)).astype(jnp.int32)
    recv_sib = _grad_exchange_sibling(g)
    sums = _chip_partial_sums(g, recv_sib, jj, qq)
    own = _own_partial_sum(g, recv_sib, jme, qme)
    recv = _grad_exchange_chips(sums)
    return _final_grad_sum(own, recv)


def _matmul(name, form, prods, M, N, K, tm, tn, tk, out_dtypes, extras=(), epilogue=None, rows_per_example=None):
    nk = K // tk
    n_acc = len(prods)
    flat = [ab for group in prods for ab in group]
    dims = {"nn": (((1,), (0,)), ((), ())), "nt": (((1,), (1,)), ((), ())), "tn": (((0,), (0,)), ((), ()))}[form]
    if form == "tn":
        a_spec = pl.BlockSpec((tk, tm), lambda i, j, k: (k, i))
    else:
        a_spec = pl.BlockSpec((tm, tk), lambda i, j, k: (i, k))
    if form == "nt":
        b_spec = pl.BlockSpec((tn, tk), lambda i, j, k: (j, k))
    else:
        b_spec = pl.BlockSpec((tk, tn), lambda i, j, k: (k, j))
    in_specs, operands = [], []
    for a, b in flat:
        in_specs += [a_spec, b_spec]
        operands += [a, b]
    for arr, kind, off in extras:
        if kind == "tile":
            assert off % tn == 0
            in_specs.append(pl.BlockSpec((tm, tn), functools.partial(lambda i, j, k, o: (i, j + o), o=off // tn)))
        else:
            tiles = rows_per_example // tm
            in_specs.append(pl.BlockSpec((None, 1, tn), functools.partial(lambda i, j, k, t: (i // t, 0, j), t=tiles)))
        operands.append(arr)
    n_in, n_out = len(operands), len(out_dtypes)

    def body(*refs):
        in_refs, out_refs, acc_refs = refs[:n_in], refs[n_in:n_in + n_out], refs[n_in + n_out:]
        k = pl.program_id(2)
        partials, p = [], 0
        for group in prods:
            tot = None
            for _ in group:
                d = lax.dot_general(in_refs[2 * p][...], in_refs[2 * p + 1][...], dims, preferred_element_type=F32)
                tot = d if tot is None else tot + d
                p += 1
            partials.append(tot)

        def finish(accs):
            ex = [r[...] for r in in_refs[2 * len(flat):]]
            outs = epilogue(accs, ex) if epilogue is not None else accs
            for r, o in zip(out_refs, outs):
                r[...] = o.astype(r.dtype)

        if nk == 1:
            finish(partials)
        else:
            @pl.when(k == 0)
            def _():
                for r, v in zip(acc_refs, partials):
                    r[...] = v

            @pl.when(k > 0)
            def _():
                for r, v in zip(acc_refs, partials):
                    r[...] += v

            @pl.when(k == nk - 1)
            def _():
                finish([r[...] for r in acc_refs])

    outs = pl.pallas_call(
        body, name=name,
        out_shape=[jax.ShapeDtypeStruct((M, N), dt) for dt in out_dtypes],
        grid=(M // tm, N // tn, nk),
        in_specs=in_specs,
        out_specs=[pl.BlockSpec((tm, tn), lambda i, j, k: (i, j)) for _ in out_dtypes],
        scratch_shapes=[pltpu.VMEM((tm, tn), F32) for _ in range(n_acc)] if nk > 1 else [],
        compiler_params=_params(("parallel", "parallel", "arbitrary")),
    )(*operands)
    return outs


def _rowwise(name, fn, T, tm, ins, outs, rows_per_example):
    tiles = rows_per_example // tm
    n_ex = T // rows_per_example
    in_specs, operands = [], []
    for arr, kind, arg in ins:
        if kind == "row":
            if arg is None:
                in_specs.append(pl.BlockSpec((tm, arr.shape[1]), lambda i: (i, 0)))
            else:
                in_specs.append(pl.BlockSpec((tm, arg[0]), functools.partial(lambda i, cb: (i, cb), cb=arg[1])))
        elif kind == "bvec":
            in_specs.append(pl.BlockSpec((None, 1, arr.shape[2]), lambda i: (i // tiles, 0, 0)))
        else:
            in_specs.append(pl.BlockSpec((1, arr.shape[1]), lambda i: (0, 0)))
        operands.append(arr)
    out_shape, out_specs = [], []
    for kind, cols, dt in outs:
        if kind == "row":
            out_shape.append(jax.ShapeDtypeStruct((T, cols), dt))
            out_specs.append(pl.BlockSpec((tm, cols), lambda i: (i, 0)))
        else:
            out_shape.append(jax.ShapeDtypeStruct((n_ex, 1, cols), F32))
            out_specs.append(pl.BlockSpec((None, 1, cols), lambda i: (i // tiles, 0, 0)))
    n_in = len(operands)

    def body(*refs):
        i = pl.program_id(0)
        vals = fn(*[r[...] for r in refs[:n_in]])
        for (kind, _, _), r, v in zip(outs, refs[n_in:], vals):
            if kind == "row":
                r[...] = v.astype(r.dtype)
            else:
                @pl.when(i % tiles == 0)
                def _():
                    r[...] = jnp.zeros_like(r)

                r[...] += v

    return pl.pallas_call(
        body, name=name, out_shape=out_shape, grid=(T // tm,), in_specs=in_specs, out_specs=out_specs,
        compiler_params=_params(("arbitrary",)),
    )(*operands)


def _colsum(v):
    return jnp.sum(v, axis=0, keepdims=True)


def _rms_parts(x):
    rstd = lax.rsqrt(jnp.mean(x * x, axis=-1, keepdims=True) + RMS_EPS)
    return x * rstd, rstd


def _normmod(name, x, g, sc, sh, S):
    T, D = x.shape

    def fn(xv, gv, scv, shv):
        xhat, _ = _rms_parts(xv)
        return [(xhat * gv) * (1.0 + scv) + shv]

    return _rowwise(name, fn, T, _div(S, 512, 8), [(x, "row", None), (g, "vec", None), (sc, "bvec", None), (sh, "bvec", None)],
                    [("row", D, BF16)], S)[0]


def _normmod_bwd(name, x, g, sc, dh, dres, S):
    T, D = x.shape

    def fn(xv, gv, scv, dhv, drv):
        xhat, rstd = _rms_parts(xv)
        n = xhat * gv
        dn = dhv * (1.0 + scv)
        dxh = dn * gv
        dx = rstd * (dxh - xhat * jnp.mean(dxh * xhat, axis=-1, keepdims=True))
        return [drv + dx, _colsum(dhv), _colsum(dhv * n), _colsum(dn * xhat)]

    return _rowwise(name, fn, T, _div(S, 256, 8),
                    [(x, "row", None), (g, "vec", None), (sc, "bvec", None), (dh, "row", None), (dres, "row", None)],
                    [("row", D, F32), ("bacc", D, F32), ("bacc", D, F32), ("bacc", D, F32)], S)


def _gate_grad(name, dx, y, gt, coeff, S):
    T, D = dx.shape

    def fn(dxv, yv, gtv):
        return [coeff * gtv * dxv, _colsum(coeff * dxv * yv.astype(F32))]

    return _rowwise(name, fn, T, _div(S, 512, 8), [(dx, "row", None), (y, "row", None), (gt, "bvec", None)],
                    [("row", D, BF16), ("bacc", D, F32)], S)


def _ffn_forward(tag, x, g, sh, sc, gt, wgT, wuT, wd, S):
    T, D = x.shape
    F = wd.shape[0]
    h = _normmod(f"{tag}_normmod", x, g, sc, sh, S)

    def gateup(accs, ex):
        a, u = accs
        return [a, u, a * _sigmoid(a) * u]

    a, u, s = _matmul(f"{tag}_gateup", "nt", [[(h, wgT)], [(h, wuT)]], T, F, D, _div(T, 1024, 8), _div(F, 256, LANES), D,
                      [BF16, BF16, BF16], epilogue=gateup)

    def down(accs, ex):
        xv, gtv = ex
        return [xv + 0.5 * gtv * accs[0], accs[0]]

    tmd = _div(S, 512, 8)
    x_new, y = _matmul(f"{tag}_down", "nn", [[(s, wd)]], T, D, F, tmd, _div(D, 512, LANES), F, [F32, BF16],
                       extras=[(x, "tile", 0), (gt, "brow", 0)], epilogue=down, rows_per_example=S)
    return x_new, (x, h, a, u, s, y)


def _ffn_backward(tag, dx_out, saved, g, sc, gt, wgT, wuT, wd, S):
    x, h, a, u, s, y = saved
    T, D = x.shape
    F = wd.shape[0]
    dy, dgt = _gate_grad(f"{tag}_gate_grad", dx_out, y, gt, 0.5, S)

    def act_grad(accs, ex):
        ds = accs[0]
        av, uv = ex[0].astype(F32), ex[1].astype(F32)
        sg = _sigmoid(av)
        return [ds * uv * (sg * (1.0 + av * (1.0 - sg))), ds * (av * sg)]

    da, du = _matmul(f"{tag}_act_grad", "nt", [[(dy, wd)]], T, F, D, _div(T, 1024, 8), _div(F, 256, LANES), D, [BF16, BF16],
                     extras=[(a, "tile", 0), (u, "tile", 0)], epilogue=act_grad)
    tmw = _div(F, 1408, LANES)
    tkw = _div(T, 1024, LANES)
    dwd = _matmul(f"{tag}_dw_down", "tn", [[(s, dy)]], F, D, T, tmw, D, tkw, [F32])[0]
    dwgT = _matmul(f"{tag}_dw_gate", "tn", [[(da, h)]], F, D, T, tmw, D, tkw, [F32])[0]
    dwuT = _matmul(f"{tag}_dw_up", "tn", [[(du, h)]], F, D, T, tmw, D, tkw, [F32])[0]
    dh = _matmul(f"{tag}_dh", "nn", [[(da, wgT), (du, wuT)]], T, D, F, _div(T, 512, 8), _div(D, 512, LANES), F, [F32])[0]
    dx_in, dsh, dsc, dg = _normmod_bwd(f"{tag}_normmod_bwd", x, g, sc, dh, dx_out, S)
    return dx_in, (dsh, dsc, dgt, dg), (dwgT, dwuT, dwd)


def _loss_head(x, tgt, g, S):
    T, D = x.shape

    def fn(xv, tv, gv):
        xhat, rstd = _rms_parts(xv)
        e = xhat * gv - tv
        loss = jnp.broadcast_to(0.5 / D * jnp.sum(_colsum(e * e), axis=1, keepdims=True), (1, LANES))
        dy = e * (1.0 / D)
        dxh = dy * gv
        dx = rstd * (dxh - xhat * jnp.mean(dxh * xhat, axis=-1, keepdims=True))
        return [dx, loss, _colsum(dy * xhat)]

    return _rowwise("loss_head", fn, T, _div(S, 512, 8), [(x, "row", None), (tgt, "row", None), (g, "vec", None)],
                    [("row", D, F32), ("bacc", LANES, F32), ("bacc", D, F32)], S)


def _cumsum(v):
    B, S, _ = v.shape
    rows = _div(S, 1024, BLOCK)

    def body(x_ref, o_ref, carry):
        i = pl.program_id(1)

        @pl.when(i == 0)
        def _():
            carry[...] = jnp.zeros_like(carry)

        r = lax.broadcasted_iota(jnp.int32, (BLOCK, BLOCK), 0)
        c = lax.broadcasted_iota(jnp.int32, (BLOCK, BLOCK), 1)
        tri = (c <= r).astype(F32)
        last = carry[0:1, :]
        for j in range(0, rows, BLOCK):
            cum = jnp.dot(tri, x_ref[j:j + BLOCK, :], precision=lax.Precision.HIGHEST, preferred_element_type=F32) + last
            o_ref[j:j + BLOCK, :] = cum
            last = cum[BLOCK - 1:BLOCK, :]
        carry[...] = jnp.broadcast_to(last, carry.shape)

    return pl.pallas_call(
        body, name="cumsum", out_shape=jax.ShapeDtypeStruct(v.shape, F32), grid=(B, S // rows),
        in_specs=[pl.BlockSpec((None, rows, LANES), lambda b, i: (b, i, 0))],
        out_specs=pl.BlockSpec((None, rows, LANES), lambda b, i: (b, i, 0)),
        scratch_shapes=[pltpu.VMEM((8, LANES), F32)],
        compiler_params=_params(("arbitrary", "arbitrary")),
    )(v)


def _fox_scores(q, k, cq, ck, qpos, kpos):
    s = lax.dot_general(q, k, (((1,), (1,)), ((), ())), preferred_element_type=F32) * SCALE + cq - ck
    return jnp.where(kpos <= qpos, s, NEG_INF)


def _fox_positions(qi, kj, tq, tk):
    qpos = qi * tq + lax.broadcasted_iota(jnp.int32, (tq, tk), 0)
    kpos = kj * tk + lax.broadcasted_iota(jnp.int32, (tq, tk), 1)
    return qpos, kpos


def _fox_forward(pm3, cum, cumT, qcol, kcol, vcol, tq):
    B, S, _ = pm3.shape
    nq = S // tq

    def body(q_ref, k_ref, v_ref, cq_ref, ck_ref, o_ref, o32_ref, lse_ref, m_sc, l_sc, acc_sc):
        qi, kj = pl.program_id(1), pl.program_id(2)

        @pl.when(kj == 0)
        def _():
            m_sc[...] = jnp.full_like(m_sc, NEG_INF)
            l_sc[...] = jnp.zeros_like(l_sc)
            acc_sc[...] = jnp.zeros_like(acc_sc)

        @pl.when(kj <= qi)
        def _():
            qpos, kpos = _fox_positions(qi, kj, tq, tq)
            for h in range(FOX_HEADS):
                hs = slice(HEAD_DIM * h, HEAD_DIM * (h + 1))
                s = _fox_scores(q_ref[:, hs], k_ref[:, hs], cq_ref[:, h:h + 1], ck_ref[h:h + 1, :], qpos, kpos)
                m_prev = m_sc[h]
                m_new = jnp.maximum(m_prev, jnp.max(s, axis=-1, keepdims=True))
                alpha = jnp.exp(m_prev - m_new)
                p = jnp.exp(s - m_new)
                l_sc[h] = alpha * l_sc[h] + jnp.sum(p, axis=-1, keepdims=True)
                acc_sc[:, hs] = alpha * acc_sc[:, hs] + lax.dot_general(
                    p.astype(BF16), v_ref[:, hs], (((1,), (0,)), ((), ())), preferred_element_type=F32)
                m_sc[h] = m_new

        @pl.when(kj == nq - 1)
        def _():
            lse_ref[...] = jnp.zeros_like(lse_ref)
            for h in range(FOX_HEADS):
                hs = slice(HEAD_DIM * h, HEAD_DIM * (h + 1))
                oh = acc_sc[:, hs] / l_sc[h]
                o_ref[:, hs] = oh.astype(o_ref.dtype)
                o32_ref[:, hs] = oh
                lse_ref[:, h:h + 1] = m_sc[h] + jnp.log(l_sc[h])

    return pl.pallas_call(
        body, name="fox_forward",
        out_shape=[jax.ShapeDtypeStruct((B, S, FOX_W), BF16), jax.ShapeDtypeStruct((B, S, FOX_W), F32),
                   jax.ShapeDtypeStruct((B, S, LANES), F32)],
        grid=(B, nq, nq),
        in_specs=[pl.BlockSpec((None, tq, FOX_W), lambda b, i, j: (b, i, qcol)),
                  pl.BlockSpec((None, tq, FOX_W), lambda b, i, j: (b, jnp.minimum(i, j), kcol)),
                  pl.BlockSpec((None, tq, FOX_W), lambda b, i, j: (b, jnp.minimum(i, j), vcol)),
                  pl.BlockSpec((None, tq, LANES), lambda b, i, j: (b, i, 0)),
                  pl.BlockSpec((None, 8, tq), lambda b, i, j: (b, 0, jnp.minimum(i, j)))],
        out_specs=[pl.BlockSpec((None, tq, FOX_W), lambda b, i, j: (b, i, 0)),
                   pl.BlockSpec((None, tq, FOX_W), lambda b, i, j: (b, i, 0)),
                   pl.BlockSpec((None, tq, LANES), lambda b, i, j: (b, i, 0))],
        scratch_shapes=[pltpu.VMEM((FOX_HEADS, tq, 1), F32), pltpu.VMEM((FOX_HEADS, tq, 1), F32), pltpu.VMEM((tq, FOX_W), F32)],
        compiler_params=_params(("parallel", "parallel", "arbitrary")),
    )(pm3, pm3, pm3, cum, cumT)


def _fox_dq(pm3, do, delta, lse, cum, cumT, qcol, kcol, vcol, tq):
    B, S, _ = pm3.shape
    nq = S // tq

    def body(q_ref, k_ref, v_ref, do_ref, dl_ref, lse_ref, cq_ref, ck_ref, dq_ref, dc_ref, acc_sc, dc_sc):
        qi, kj = pl.program_id(1), pl.program_id(2)

        @pl.when(kj == 0)
        def _():
            acc_sc[...] = jnp.zeros_like(acc_sc)
            dc_sc[...] = jnp.zeros_like(dc_sc)

        @pl.when(kj <= qi)
        def _():
            qpos, kpos = _fox_positions(qi, kj, tq, tq)
            for h in range(FOX_HEADS):
                hs = slice(HEAD_DIM * h, HEAD_DIM * (h + 1))
                s = _fox_scores(q_ref[:, hs], k_ref[:, hs], cq_ref[:, h:h + 1], ck_ref[h:h + 1, :], qpos, kpos)
                p = jnp.exp(s - lse_ref[:, h:h + 1])
                doh = do_ref[:, hs]
                dp = lax.dot_general(doh, v_ref[:, hs], (((1,), (1,)), ((), ())), preferred_element_type=F32)
                ds = p * (dp - dl_ref[:, h:h + 1])
                dc_sc[h] += jnp.sum(ds, axis=-1, keepdims=True)
                acc_sc[:, hs] += lax.dot_general(ds.astype(BF16), k_ref[:, hs], (((1,), (0,)), ((), ())),
                                                 preferred_element_type=F32)

        @pl.when(kj == nq - 1)
        def _():
            dq_ref[...] = (acc_sc[...] * SCALE).astype(dq_ref.dtype)
            dc_ref[...] = jnp.zeros_like(dc_ref)
            for h in range(FOX_HEADS):
                dc_ref[:, h:h + 1] = dc_sc[h]

    qspec = pl.BlockSpec((None, tq, FOX_W), lambda b, i, j: (b, i, 0))
    lspec = pl.BlockSpec((None, tq, LANES), lambda b, i, j: (b, i, 0))
    return pl.pallas_call(
        body, name="fox_dq",
        out_shape=[jax.ShapeDtypeStruct((B, S, FOX_W), BF16), jax.ShapeDtypeStruct((B, S, LANES), F32)],
        grid=(B, nq, nq),
        in_specs=[pl.BlockSpec((None, tq, FOX_W), lambda b, i, j: (b, i, qcol)),
                  pl.BlockSpec((None, tq, FOX_W), lambda b, i, j: (b, jnp.minimum(i, j), kcol)),
                  pl.BlockSpec((None, tq, FOX_W), lambda b, i, j: (b, jnp.minimum(i, j), vcol)),
                  qspec, lspec, lspec, lspec,
                  pl.BlockSpec((None, 8, tq), lambda b, i, j: (b, 0, jnp.minimum(i, j)))],
        out_specs=[qspec, lspec],
        scratch_shapes=[pltpu.VMEM((tq, FOX_W), F32), pltpu.VMEM((FOX_HEADS, tq, 1), F32)],
        compiler_params=_params(("parallel", "parallel", "arbitrary")),
    )(pm3, pm3, pm3, do, delta, lse, cum, cumT)


def _fox_dkv(pm3, do, delta, lse, cum, cumT, qcol, kcol, vcol, tq):
    B, S, _ = pm3.shape
    nq = S // tq

    def body(q_ref, k_ref, v_ref, do_ref, dl_ref, lse_ref, cq_ref, ck_ref, dk_ref, dv_ref, dc_ref, dk_sc, dv_sc, dc_sc):
        kj, qi = pl.program_id(1), pl.program_id(2)

        @pl.when(qi == 0)
        def _():
            dk_sc[...] = jnp.zeros_like(dk_sc)
            dv_sc[...] = jnp.zeros_like(dv_sc)
            dc_sc[...] = jnp.zeros_like(dc_sc)

        @pl.when(qi >= kj)
        def _():
            qpos, kpos = _fox_positions(qi, kj, tq, tq)
            for h in range(FOX_HEADS):
                hs = slice(HEAD_DIM * h, HEAD_DIM * (h + 1))
                qh = q_ref[:, hs]
                s = _fox_scores(qh, k_ref[:, hs], cq_ref[:, h:h + 1], ck_ref[h:h + 1, :], qpos, kpos)
                p = jnp.exp(s - lse_ref[:, h:h + 1])
                doh = do_ref[:, hs]
                dp = lax.dot_general(doh, v_ref[:, hs], (((1,), (1,)), ((), ())), preferred_element_type=F32)
                ds = p * (dp - dl_ref[:, h:h + 1])
                dv_sc[:, hs] += lax.dot_general(p.astype(BF16), doh, (((0,), (0,)), ((), ())), preferred_element_type=F32)
                dk_sc[:, hs] += lax.dot_general(ds.astype(BF16), qh, (((0,), (0,)), ((), ())), preferred_element_type=F32)
                dc_sc[h:h + 1, :] -= jnp.sum(ds, axis=0, keepdims=True)

        @pl.when(qi == nq - 1)
        def _():
            dk_ref[...] = (dk_sc[...] * SCALE).astype(dk_ref.dtype)
            dv_ref[...] = dv_sc[...].astype(dv_ref.dtype)
            dc_ref[...] = dc_sc[...]

    def qside(width):
        return pl.BlockSpec((None, tq, width), lambda b, j, i: (b, jnp.maximum(i, j), 0))

    kspec = pl.BlockSpec((None, tq, FOX_W), lambda b, j, i: (b, j, 0))
    return pl.pallas_call(
        body, name="fox_dkv",
        out_shape=[jax.ShapeDtypeStruct((B, S, FOX_W), BF16), jax.ShapeDtypeStruct((B, S, FOX_W), BF16),
                   jax.ShapeDtypeStruct((B, 8, S), F32)],
        grid=(B, nq, nq),
        in_specs=[pl.BlockSpec((None, tq, FOX_W), lambda b, j, i: (b, jnp.maximum(i, j), qcol)),
                  pl.BlockSpec((None, tq, FOX_W), lambda b, j, i: (b, j, kcol)),
                  pl.BlockSpec((None, tq, FOX_W), lambda b, j, i: (b, j, vcol)),
                  qside(FOX_W), qside(LANES), qside(LANES), qside(LANES),
                  pl.BlockSpec((None, 8, tq), lambda b, j, i: (b, 0, j))],
        out_specs=[kspec, kspec, pl.BlockSpec((None, 8, tq), lambda b, j, i: (b, 0, j))],
        scratch_shapes=[pltpu.VMEM((tq, FOX_W), F32), pltpu.VMEM((tq, FOX_W), F32), pltpu.VMEM((8, tq), F32)],
        compiler_params=_params(("parallel", "parallel", "arbitrary")),
    )(pm3, pm3, pm3, do, delta, lse, cum, cumT)


def _with_ones(x):
    lane = lax.broadcasted_iota(jnp.int32, (x.shape[0], HEAD_DIM), 1)
    return jnp.concatenate([x, jnp.where(lane == 0, 1.0, 0.0).astype(x.dtype)], axis=1)


def _causal_strip(s, r):
    qpos = r + lax.broadcasted_iota(jnp.int32, s.shape, 0)
    kpos = lax.broadcasted_iota(jnp.int32, s.shape, 1)
    return jnp.where(kpos <= qpos, s, NEG_INF)


NT = (((1,), (1,)), ((), ()))
NN = (((1,), (0,)), ((), ()))
TN = (((0,), (0,)), ((), ()))


def _fox_fwd(pm3, cumT, qcol, kcol, vcol, tq):
    B, S, _ = pm3.shape
    nq = S // tq
    strips = range(0, tq, FOX_STRIP)

    def body(q_ref, k_ref, v_ref, ck_ref, o_ref, o32_ref, lse_ref, s_sc, p_sc, al_sc, m_sc, acc_sc):
        qi, kj = pl.program_id(1), pl.program_id(2)

        @pl.when(kj == 0)
        def _():
            m_sc[...] = jnp.full_like(m_sc, NEG_INF)
            acc_sc[...] = jnp.zeros_like(acc_sc)

        def tile(diagonal):
            for h in range(FOX_HEADS):
                hs = slice(HEAD_DIM * h, HEAD_DIM * (h + 1))
                s_sc[...] = lax.dot_general(q_ref[:, hs] * SCALE, k_ref[:, hs], NT, preferred_element_type=F32)
                ck = ck_ref[h:h + 1, :]
                for r in strips:
                    rows = slice(r, r + FOX_STRIP)
                    s = s_sc[rows, :] - ck
                    if diagonal:
                        s = _causal_strip(s, r)
                    m_prev = m_sc[h, rows, :]
                    m_new = jnp.maximum(m_prev, jnp.max(s, axis=-1, keepdims=True))
                    p_sc[rows, :] = jnp.exp(s - m_new).astype(BF16)
                    al_sc[rows, :] = jnp.exp(m_prev - m_new)
                    m_sc[h, rows, :] = m_new
                acc_sc[h] = al_sc[...] * acc_sc[h] + lax.dot_general(p_sc[...], _with_ones(v_ref[:, hs]), NN,
                                                                     preferred_element_type=F32)

        @pl.when(kj < qi)
        def _():
            tile(False)

        @pl.when(kj == qi)
        def _():
            tile(True)

        @pl.when(kj == nq - 1)
        def _():
            lse_ref[...] = jnp.zeros_like(lse_ref)
            for h in range(FOX_HEADS):
                hs = slice(HEAD_DIM * h, HEAD_DIM * (h + 1))
                acc = acc_sc[h]
                l = acc[:, HEAD_DIM:HEAD_DIM + 1]
                oh = acc[:, :HEAD_DIM] / l
                o_ref[:, hs] = oh.astype(o_ref.dtype)
                o32_ref[:, hs] = oh
                lse_ref[:, h:h + 1] = m_sc[h] + jnp.log(l)

    ospec = pl.BlockSpec((None, tq, FOX_W), lambda b, i, j: (b, i, 0))
    return pl.pallas_call(
        body, name="fox_forward",
        out_shape=[jax.ShapeDtypeStruct((B, S, FOX_W), BF16), jax.ShapeDtypeStruct((B, S, FOX_W), F32),
                   jax.ShapeDtypeStruct((B, S, LANES), F32)],
        grid=(B, nq, nq),
        in_specs=[pl.BlockSpec((None, tq, FOX_W), lambda b, i, j: (b, i, qcol)),
                  pl.BlockSpec((None, tq, FOX_W), lambda b, i, j: (b, jnp.minimum(i, j), kcol)),
                  pl.BlockSpec((None, tq, FOX_W), lambda b, i, j: (b, jnp.minimum(i, j), vcol)),
                  pl.BlockSpec((None, 8, tq), lambda b, i, j: (b, 0, jnp.minimum(i, j)))],
        out_specs=[ospec, ospec, pl.BlockSpec((None, tq, LANES), lambda b, i, j: (b, i, 0))],
        scratch_shapes=[pltpu.VMEM((tq, tq), F32), pltpu.VMEM((tq, tq), BF16), pltpu.VMEM((tq, 1), F32),
                        pltpu.VMEM((FOX_HEADS, tq, 1), F32), pltpu.VMEM((FOX_HEADS, tq, LANES), F32)],
        compiler_params=_params(("parallel", "parallel", "arbitrary")),
    )(pm3, pm3, pm3, cumT)


def _fox_bwd(pm3, do, delta, lse, cumT, qcol, kcol, vcol, tq):
    B, S, _ = pm3.shape
    nq = S // tq
    strips = range(0, tq, FOX_STRIP)

    def body(q_ref, k_ref, v_ref, do_ref, dl_ref, lse_ref, ck_ref, dq_ref, rs_ref, dk_ref, dv_ref, cs_ref,
             s_sc, dp_sc, p_sc, ds_sc, dq_sc, dk_sc, dv_sc):
        kj, qi = pl.program_id(1), pl.program_id(2)

        @pl.when((kj == 0) & (qi == 0))
        def _():
            dq_sc[...] = jnp.zeros_like(dq_sc)

        @pl.when(qi == 0)
        def _():
            dk_sc[...] = jnp.zeros_like(dk_sc)
            dv_sc[...] = jnp.zeros_like(dv_sc)

        def tile(diagonal):
            qrows = pl.ds(pl.multiple_of(qi * tq, tq), tq)
            for h in range(FOX_HEADS):
                hs = slice(HEAD_DIM * h, HEAD_DIM * (h + 1))
                qh, kh, doh = q_ref[:, hs] * SCALE, k_ref[:, hs], do_ref[:, hs]
                s_sc[...] = lax.dot_general(qh, kh, NT, preferred_element_type=F32)
                dp_sc[...] = lax.dot_general(doh, v_ref[:, hs], NT, preferred_element_type=F32)
                ck = ck_ref[h:h + 1, :]
                for r in strips:
                    rows = slice(r, r + FOX_STRIP)
                    s = s_sc[rows, :] - ck
                    if diagonal:
                        s = _causal_strip(s, r)
                    p = jnp.exp(s - lse_ref[rows, h:h + 1])
                    p_sc[rows, :] = p.astype(BF16)
                    ds_sc[rows, :] = (p * (dp_sc[rows, :] - dl_ref[rows, h:h + 1])).astype(BF16)
                dv_sc[:, hs] += lax.dot_general(p_sc[...], doh, TN, preferred_element_type=F32)
                dk_sc[h] += lax.dot_general(ds_sc[...], _with_ones(qh), TN, preferred_element_type=F32)
                dq_sc[h, qrows, :] += lax.dot_general(ds_sc[...], _with_ones(kh), NN, preferred_element_type=F32)

        @pl.when(qi > kj)
        def _():
            tile(False)

        @pl.when(qi == kj)
        def _():
            tile(True)

        @pl.when(qi == nq - 1)
        def _():
            dv_ref[...] = dv_sc[...].astype(dv_ref.dtype)
            cs_ref[...] = jnp.zeros_like(cs_ref)
            for h in range(FOX_HEADS):
                hs = slice(HEAD_DIM * h, HEAD_DIM * (h + 1))
                dk = dk_sc[h]
                dk_ref[:, hs] = dk[:, :HEAD_DIM].astype(dk_ref.dtype)
                cs_ref[:, h:h + 1] = dk[:, HEAD_DIM:HEAD_DIM + 1]

        @pl.when((kj == nq - 1) & (qi == nq - 1))
        def _():
            rs_ref[...] = jnp.zeros_like(rs_ref)
            for h in range(FOX_HEADS):
                hs = slice(HEAD_DIM * h, HEAD_DIM * (h + 1))
                dq_ref[:, hs] = (dq_sc[h, :, :HEAD_DIM] * SCALE).astype(dq_ref.dtype)
                rs_ref[:, h:h + 1] = dq_sc[h, :, HEAD_DIM:HEAD_DIM + 1]

    def qside(width, col=0):
        return pl.BlockSpec((None, tq, width), lambda b, j, i: (b, jnp.maximum(i, j), col))

    kspec = pl.BlockSpec((None, tq, FOX_W), lambda b, j, i: (b, j, 0))
    return pl.pallas_call(
        body, name="fox_backward",
        out_shape=[jax.ShapeDtypeStruct((B, S, FOX_W), BF16), jax.ShapeDtypeStruct((B, S, LANES), F32),
                   jax.ShapeDtypeStruct((B, S, FOX_W), BF16), jax.ShapeDtypeStruct((B, S, FOX_W), BF16),
                   jax.ShapeDtypeStruct((B, S, LANES), F32)],
        grid=(B, nq, nq),
        in_specs=[qside(FOX_W, qcol),
                  pl.BlockSpec((None, tq, FOX_W), lambda b, j, i: (b, j, kcol)),
                  pl.BlockSpec((None, tq, FOX_W), lambda b, j, i: (b, j, vcol)),
                  qside(FOX_W), qside(LANES), qside(LANES),
                  pl.BlockSpec((None, 8, tq), lambda b, j, i: (b, 0, j))],
        out_specs=[pl.BlockSpec((None, S, FOX_W), lambda b, j, i: (b, 0, 0)),
                   pl.BlockSpec((None, S, LANES), lambda b, j, i: (b, 0, 0)),
                   kspec, kspec, pl.BlockSpec((None, tq, LANES), lambda b, j, i: (b, j, 0))],
        scratch_shapes=[pltpu.VMEM((tq, tq), F32), pltpu.VMEM((tq, tq), F32), pltpu.VMEM((tq, tq), BF16),
                        pltpu.VMEM((tq, tq), BF16), pltpu.VMEM((FOX_HEADS, S, LANES), F32),
                        pltpu.VMEM((FOX_HEADS, tq, LANES), F32), pltpu.VMEM((tq, FOX_W), F32)],
        compiler_params=_params(("parallel", "arbitrary", "arbitrary")),
    )(pm3, pm3, pm3, do, delta, lse, cumT)


def _fox_delta(do, o32, T, S):
    def fn(dov, ov):
        prod = dov.astype(F32) * ov
        lane = lax.broadcasted_iota(jnp.int32, (dov.shape[0], LANES), 1)
        delta = jnp.zeros((dov.shape[0], LANES), F32)
        for h in range(FOX_HEADS):
            hs = slice(HEAD_DIM * h, HEAD_DIM * (h + 1))
            delta = jnp.where(lane == h, jnp.sum(prod[:, hs], axis=-1, keepdims=True), delta)
        return [delta]

    return _rowwise("fox_delta", fn, T, _div(S, 512, 8), [(do, "row", None), (o32, "row", None)], [("row", LANES, F32)], S)[0]


def _alibi_slope(group, head):
    return 2.0 ** (-ALIBI_MAX_BIAS * (group * DIL_HPG + head + 1) / (N_DIL * DIL_HPG))


def _dil_tiles(q, k_cur, k_prev, slope, dilation, has_prev):
    qi = lax.broadcasted_iota(jnp.int32, (BLOCK, BLOCK), 0)
    ki = lax.broadcasted_iota(jnp.int32, (BLOCK, BLOCK), 1)
    rel = (qi - ki).astype(F32)
    nt = (((1,), (1,)), ((), ()))
    s_cur = lax.dot_general(q, k_cur, nt, preferred_element_type=F32) * SCALE - (slope * dilation) * rel
    s_cur = jnp.where(ki <= qi, s_cur, NEG_INF)
    s_prev = lax.dot_general(q, k_prev, nt, preferred_element_type=F32) * SCALE - (slope * dilation) * (rel + BLOCK)
    s_prev = jnp.where((ki >= qi) & has_prev, s_prev, NEG_INF)
    return s_cur, s_prev


def _dil_forward(group, pmv, nmb, qa_blk, B, S):
    _, dilation = DIL_GROUPS[group]
    sub = S // dilation
    nb = sub // BLOCK
    qb, kb, vb = qa_blk + group, qa_blk + N_DIL + group, qa_blk + 2 * N_DIL + group

    def body(q_ref, kc_ref, kp_ref, vc_ref, vp_ref, o_ref, lse_ref):
        has_prev = pl.program_id(2) > 0
        lse_ref[...] = jnp.zeros_like(lse_ref)
        for h in range(DIL_HPG):
            hs = slice(HEAD_DIM * h, HEAD_DIM * (h + 1))
            s_cur, s_prev = _dil_tiles(q_ref[:, hs], kc_ref[:, hs], kp_ref[:, hs], _alibi_slope(group, h), dilation, has_prev)
            m = jnp.maximum(jnp.max(s_cur, axis=-1, keepdims=True), jnp.max(s_prev, axis=-1, keepdims=True))
            p_cur, p_prev = jnp.exp(s_cur - m), jnp.exp(s_prev - m)
            l = jnp.sum(p_cur, axis=-1, keepdims=True) + jnp.sum(p_prev, axis=-1, keepdims=True)
            nn = (((1,), (0,)), ((), ()))
            o = (lax.dot_general(p_cur.astype(BF16), vc_ref[:, hs], nn, preferred_element_type=F32)
                 + lax.dot_general(p_prev.astype(BF16), vp_ref[:, hs], nn, preferred_element_type=F32))
            o_ref[:, hs] = o / l
            lse_ref[:, h:h + 1] = m + jnp.log(l)

    def cur(col):
        return pl.BlockSpec((None, BLOCK, DIL_GW), lambda b, r, n: (b, n, r * nmb + col))

    def prev(col):
        return pl.BlockSpec((None, BLOCK, DIL_GW), lambda b, r, n: (b, jnp.maximum(n - 1, 0), r * nmb + col))

    return pl.pallas_call(
        body, name=f"dil_forward_{group}",
        out_shape=[jax.ShapeDtypeStruct((B, sub, dilation * DIL_GW), F32), jax.ShapeDtypeStruct((B, sub, dilation * LANES), F32)],
        grid=(B, dilation, nb),
        in_specs=[cur(qb), cur(kb), prev(kb), cur(vb), prev(vb)],
        out_specs=[pl.BlockSpec((None, BLOCK, DIL_GW), lambda b, r, n: (b, n, r)),
                   pl.BlockSpec((None, BLOCK, LANES), lambda b, r, n: (b, n, r))],
        compiler_params=_params(("parallel", "parallel", "arbitrary")),
    )(pmv, pmv, pmv, pmv, pmv)


def _residue_order(a, B, S, d):
    C = a.shape[-1]
    if d == 1:
        return a.reshape(B, S, C)
    return a.reshape(B, S // d, d, C).transpose(0, 2, 1, 3).reshape(B * d, S // d, C)


def _token_order(a, B, S, d):
    C = a.shape[-1]
    if d == 1:
        return a.reshape(B * S, C)
    return a.reshape(B, d, S // d, C).transpose(0, 2, 1, 3).reshape(B * S, C)


def _band_scores(qh, kcat, slope_d, has_prev):
    qi = lax.broadcasted_iota(jnp.int32, (BLOCK, 2 * BLOCK), 0)
    c = lax.broadcasted_iota(jnp.int32, (BLOCK, 2 * BLOCK), 1)
    s = lax.dot_general(qh, kcat, NT, preferred_element_type=F32) - slope_d * (BLOCK + qi - c).astype(F32)
    valid = (c >= qi) & (c <= qi + BLOCK)
    if has_prev is not None:
        valid = valid & ((c >= BLOCK) | has_prev)
    return jnp.where(valid, s, NEG_INF)


def _band_operands(j, cur_ref, prev_ref, hs):
    if j == 0:
        return jnp.concatenate([prev_ref[:, hs], cur_ref[0:BLOCK, hs]], axis=0)
    return cur_ref[(j - 1) * BLOCK:(j + 1) * BLOCK, hs]


def _dil_specs(Ls, qb, cols):
    nsub = qb // BLOCK
    qcol, kcol, vcol = cols

    def cur(col):
        return pl.BlockSpec((None, qb, DIL_GW), lambda s, n: (s, n, col))

    def prev(col):
        return pl.BlockSpec((None, BLOCK, DIL_GW), lambda s, n: (s, jnp.maximum(n * nsub - 1, 0), col))

    return [cur(qcol), cur(kcol), prev(kcol), cur(vcol), prev(vcol)]


def _dil_fwd(group, src, cols):
    _, dilation = DIL_GROUPS[group]
    nseq, Ls, _ = src.shape
    qb = _div(Ls, 512, BLOCK)
    nsub = qb // BLOCK

    def body(q_ref, kc_ref, kp_ref, vc_ref, vp_ref, o_ref, lse_ref):
        has_prev = pl.program_id(1) > 0
        lse_ref[...] = jnp.zeros_like(lse_ref)
        for h in range(DIL_HPG):
            hs = slice(HEAD_DIM * h, HEAD_DIM * (h + 1))
            for j in range(nsub):
                rows = slice(j * BLOCK, (j + 1) * BLOCK)
                s = _band_scores(q_ref[rows, hs] * SCALE, _band_operands(j, kc_ref, kp_ref, hs),
                                 _alibi_slope(group, h) * dilation, has_prev if j == 0 else None)
                m = jnp.max(s, axis=-1, keepdims=True)
                p = jnp.exp(s - m).astype(BF16)
                acc = lax.dot_general(p, _with_ones(_band_operands(j, vc_ref, vp_ref, hs)), NN, preferred_element_type=F32)
                l = acc[:, HEAD_DIM:HEAD_DIM + 1]
                o_ref[rows, hs] = acc[:, :HEAD_DIM] / l
                lse_ref[rows, h:h + 1] = m + jnp.log(l)

    return pl.pallas_call(
        body, name=f"dil_forward_{group}",
        out_shape=[jax.ShapeDtypeStruct((nseq, Ls, DIL_GW), F32), jax.ShapeDtypeStruct((nseq, Ls, LANES), F32)],
        grid=(nseq, Ls // qb),
        in_specs=_dil_specs(Ls, qb, cols),
        out_specs=[pl.BlockSpec((None, qb, DIL_GW), lambda s, n: (s, n, 0)),
                   pl.BlockSpec((None, qb, LANES), lambda s, n: (s, n, 0))],
        compiler_params=_params(("parallel", "arbitrary")),
    )(src, src, src, src, src)


def _dil_bwd(group, src, cols, Lr, dyr, dlr):
    _, dilation = DIL_GROUPS[group]
    nseq, Ls, _ = src.shape
    qb = _div(Ls, 512, BLOCK)
    nsub, nb = qb // BLOCK, Ls // qb

    def body(q_ref, kc_ref, kp_ref, vc_ref, vp_ref, L_ref, dy_ref, dl_ref, dq_ref, dk_ref, dv_ref, dk_sc, dv_sc):
        n = pl.program_id(1)
        has_prev = n > 0

        @pl.when(n == 0)
        def _():
            dk_sc[...] = jnp.zeros_like(dk_sc)
            dv_sc[...] = jnp.zeros_like(dv_sc)

        base = pl.multiple_of(n * qb, BLOCK)
        for h in range(DIL_HPG):
            hs = slice(HEAD_DIM * h, HEAD_DIM * (h + 1))
            for j in range(nsub):
                rows = slice(j * BLOCK, (j + 1) * BLOCK)
                qh = q_ref[rows, hs] * SCALE
                kcat = _band_operands(j, kc_ref, kp_ref, hs)
                s = _band_scores(qh, kcat, _alibi_slope(group, h) * dilation, has_prev if j == 0 else None)
                p = jnp.exp(s - L_ref[rows, h:h + 1])
                dyh = dy_ref[rows, hs]
                dp = lax.dot_general(dyh, _band_operands(j, vc_ref, vp_ref, hs), NT, preferred_element_type=F32)
                ds = (p * (dp - dl_ref[rows, h:h + 1])).astype(BF16)
                dq_ref[rows, hs] = (lax.dot_general(ds, kcat, NN, preferred_element_type=F32) * SCALE).astype(dq_ref.dtype)
                win = pl.ds(base + j * BLOCK, 2 * BLOCK)
                dk_sc[win, hs] += lax.dot_general(ds, qh, TN, preferred_element_type=F32)
                dv_sc[win, hs] += lax.dot_general(p.astype(BF16), dyh, TN, preferred_element_type=F32)

        @pl.when(n == nb - 1)
        def _():
            dk_ref[...] = dk_sc[BLOCK:, :].astype(dk_ref.dtype)
            dv_ref[...] = dv_sc[BLOCK:, :].astype(dv_ref.dtype)

    own = pl.BlockSpec((None, qb, DIL_GW), lambda s, n: (s, n, 0))
    own128 = pl.BlockSpec((None, qb, LANES), lambda s, n: (s, n, 0))
    whole = pl.BlockSpec((None, Ls, DIL_GW), lambda s, n: (s, 0, 0))
    shape = jax.ShapeDtypeStruct((nseq, Ls, DIL_GW), BF16)
    return pl.pallas_call(
        body, name=f"dil_backward_{group}",
        out_shape=[shape, shape, shape],
        grid=(nseq, nb),
        in_specs=_dil_specs(Ls, qb, cols) + [own128, own, own128],
        out_specs=[own, whole, whole],
        scratch_shapes=[pltpu.VMEM((Ls + BLOCK, DIL_GW), F32), pltpu.VMEM((Ls + BLOCK, DIL_GW), F32)],
        compiler_params=_params(("parallel", "arbitrary")),
    )(src, src, src, src, src, Lr, dyr, dlr)


def _dil_combine(os_, lses, T, S):
    def fn(o0, o1, o2, l0, l1, l2):
        m = jnp.maximum(jnp.maximum(l0, l1), l2)
        e0, e1, e2 = jnp.exp(l0 - m), jnp.exp(l1 - m), jnp.exp(l2 - m)
        tot = e0 + e1 + e2
        w0, w1, w2 = e0 / tot, e1 / tot, e2 / tot
        parts = []
        for h in range(DIL_HPG):
            hs = slice(HEAD_DIM * h, HEAD_DIM * (h + 1))
            parts.append(w0[:, h:h + 1] * o0[:, hs] + w1[:, h:h + 1] * o1[:, hs] + w2[:, h:h + 1] * o2[:, hs])
        return [jnp.concatenate(parts, axis=1), m + jnp.log(tot)]

    ins = [(a, "row", None) for a in os_] + [(a, "row", None) for a in lses]
    return _rowwise("dil_combine", fn, T, _div(S, 512, 8), ins, [("row", DIL_GW, BF16), ("row", LANES, F32)], S)


def _dil_delta(dy, y, T, S):
    def fn(dyv, yv):
        prod = dyv * yv.astype(F32)
        lane = lax.broadcasted_iota(jnp.int32, (dyv.shape[0], LANES), 1)
        delta = jnp.zeros((dyv.shape[0], LANES), F32)
        for h in range(DIL_HPG):
            hs = slice(HEAD_DIM * h, HEAD_DIM * (h + 1))
            delta = jnp.where(lane == h, jnp.sum(prod[:, hs], axis=-1, keepdims=True), delta)
        return [delta, dyv]

    return _rowwise("dil_delta", fn, T, _div(S, 512, 8), [(dy, "row", None), (y, "row", None)],
                    [("row", LANES, F32), ("row", DIL_GW, BF16)], S)


def _dil_dq(group, pmv, nmb, qa_blk, Lv, dyv, deltav, B, S):
    _, dilation = DIL_GROUPS[group]
    sub = S // dilation
    nb = sub // BLOCK
    qb, kb, vb = qa_blk + group, qa_blk + N_DIL + group, qa_blk + 2 * N_DIL + group

    def body(q_ref, kc_ref, kp_ref, vc_ref, vp_ref, L_ref, dy_ref, dl_ref, dq_ref):
        has_prev = pl.program_id(2) > 0
        nt = (((1,), (1,)), ((), ()))
        nn = (((1,), (0,)), ((), ()))
        for h in range(DIL_HPG):
            hs = slice(HEAD_DIM * h, HEAD_DIM * (h + 1))
            s_cur, s_prev = _dil_tiles(q_ref[:, hs], kc_ref[:, hs], kp_ref[:, hs], _alibi_slope(group, h), dilation, has_prev)
            L, delta, dyh = L_ref[:, h:h + 1], dl_ref[:, h:h + 1], dy_ref[:, hs]
            ds_cur = jnp.exp(s_cur - L) * (lax.dot_general(dyh, vc_ref[:, hs], nt, preferred_element_type=F32) - delta)
            ds_prev = jnp.exp(s_prev - L) * (lax.dot_general(dyh, vp_ref[:, hs], nt, preferred_element_type=F32) - delta)
            dq = (lax.dot_general(ds_cur.astype(BF16), kc_ref[:, hs], nn, preferred_element_type=F32)
                  + lax.dot_general(ds_prev.astype(BF16), kp_ref[:, hs], nn, preferred_element_type=F32))
            dq_ref[:, hs] = (dq * SCALE).astype(dq_ref.dtype)

    def cur(col):
        return pl.BlockSpec((None, BLOCK, DIL_GW), lambda b, r, n: (b, n, r * nmb + col))

    def prev(col):
        return pl.BlockSpec((None, BLOCK, DIL_GW), lambda b, r, n: (b, jnp.maximum(n - 1, 0), r * nmb + col))

    own = pl.BlockSpec((None, BLOCK, DIL_GW), lambda b, r, n: (b, n, r))
    own128 = pl.BlockSpec((None, BLOCK, LANES), lambda b, r, n: (b, n, r))
    return pl.pallas_call(
        body, name=f"dil_dq_{group}",
        out_shape=jax.ShapeDtypeStruct((B, sub, dilation * DIL_GW), BF16),
        grid=(B, dilation, nb),
        in_specs=[cur(qb), cur(kb), prev(kb), cur(vb), prev(vb), own128, own, own128],
        out_specs=own,
        compiler_params=_params(("parallel", "parallel", "arbitrary")),
    )(pmv, pmv, pmv, pmv, pmv, Lv, dyv, deltav)


def _dil_dkv(group, pmv, nmb, qa_blk, Lv, dyv, deltav, B, S):
    _, dilation = DIL_GROUPS[group]
    sub = S // dilation
    nb = sub // BLOCK
    qb, kb, vb = qa_blk + group, qa_blk + N_DIL + group, qa_blk + 2 * N_DIL + group

    def body(k_ref, v_ref, q0_ref, q1_ref, L0_ref, L1_ref, dy0_ref, dy1_ref, dl0_ref, dl1_ref, dk_ref, dv_ref):
        has_next = pl.program_id(2) < nb - 1
        qi = lax.broadcasted_iota(jnp.int32, (BLOCK, BLOCK), 0)
        ki = lax.broadcasted_iota(jnp.int32, (BLOCK, BLOCK), 1)
        rel = (qi - ki).astype(F32)
        nt = (((1,), (1,)), ((), ()))
        tn = (((0,), (0,)), ((), ()))
        for h in range(DIL_HPG):
            hs = slice(HEAD_DIM * h, HEAD_DIM * (h + 1))
            bias = _alibi_slope(group, h) * dilation
            kh, vh, q0, q1 = k_ref[:, hs], v_ref[:, hs], q0_ref[:, hs], q1_ref[:, hs]
            s0 = lax.dot_general(q0, kh, nt, preferred_element_type=F32) * SCALE - bias * rel
            s0 = jnp.where(ki <= qi, s0, NEG_INF)
            s1 = lax.dot_general(q1, kh, nt, preferred_element_type=F32) * SCALE - bias * (rel + BLOCK)
            s1 = jnp.where((ki >= qi) & has_next, s1, NEG_INF)
            p0 = jnp.exp(s0 - L0_ref[:, h:h + 1])
            p1 = jnp.exp(s1 - L1_ref[:, h:h + 1])
            dy0, dy1 = dy0_ref[:, hs], dy1_ref[:, hs]
            ds0 = p0 * (lax.dot_general(dy0, vh, nt, preferred_element_type=F32) - dl0_ref[:, h:h + 1])
            ds1 = p1 * (lax.dot_general(dy1, vh, nt, preferred_element_type=F32) - dl1_ref[:, h:h + 1])
            dv = (lax.dot_general(p0.astype(BF16), dy0, tn, preferred_element_type=F32)
                  + lax.dot_general(p1.astype(BF16), dy1, tn, preferred_element_type=F32))
            dk = (lax.dot_general(ds0.astype(BF16), q0, tn, preferred_element_type=F32)
                  + lax.dot_general(ds1.astype(BF16), q1, tn, preferred_element_type=F32))
            dv_ref[:, hs] = dv.astype(dv_ref.dtype)
            dk_ref[:, hs] = (dk * SCALE).astype(dk_ref.dtype)

    def cur(col):
        return pl.BlockSpec((None, BLOCK, DIL_GW), lambda b, r, n: (b, n, r * nmb + col))

    def nxt(col):
        return pl.BlockSpec((None, BLOCK, DIL_GW), lambda b, r, n: (b, jnp.minimum(n + 1, nb - 1), r * nmb + col))

    own = pl.BlockSpec((None, BLOCK, DIL_GW), lambda b, r, n: (b, n, r))
    own_next = pl.BlockSpec((None, BLOCK, DIL_GW), lambda b, r, n: (b, jnp.minimum(n + 1, nb - 1), r))
    own128 = pl.BlockSpec((None, BLOCK, LANES), lambda b, r, n: (b, n, r))
    own128_next = pl.BlockSpec((None, BLOCK, LANES), lambda b, r, n: (b, jnp.minimum(n + 1, nb - 1), r))
    shape = jax.ShapeDtypeStruct((B, sub, dilation * DIL_GW), BF16)
    return pl.pallas_call(
        body, name=f"dil_dkv_{group}",
        out_shape=[shape, shape],
        grid=(B, dilation, nb),
        in_specs=[cur(kb), cur(vb), cur(qb), nxt(qb), own128, own128_next, own, own_next, own128, own128_next],
        out_specs=[own, own],
        compiler_params=_params(("parallel", "parallel", "arbitrary")),
    )(pmv, pmv, pmv, pmv, Lv, Lv, dyv, dyv, deltav, deltav)


def _ada_forward(c_all, w, b):
    n, D = c_all.shape
    cl = w.shape[1]

    def body(c_ref, w_ref, b_ref, o_ref, ca_ref):
        cv = c_ref[...]
        ca = (cv * _sigmoid(cv)).astype(BF16)
        ca_ref[...] = ca
        o_ref[...] = jnp.dot(ca, w_ref[...].astype(BF16), preferred_element_type=F32) + b_ref[...]

    return pl.pallas_call(
        body, name="ada_forward",
        out_shape=[jax.ShapeDtypeStruct((n, cl), F32), jax.ShapeDtypeStruct((n, D), BF16)],
        compiler_params=_params(),
    )(c_all, w, b)


def _ada_backward(ca, dmod_cols, dmod_all):
    n, D = ca.shape
    cl = dmod_cols.shape[1]

    def body(ca_ref, dc_ref, da_ref, gw_ref, gb_ref):
        gw_ref[...] = lax.dot_general(ca_ref[...], dc_ref[...].astype(BF16), (((0,), (0,)), ((), ())), preferred_element_type=F32)
        gb_ref[...] = _colsum(da_ref[...])

    return pl.pallas_call(
        body, name="ada_backward",
        out_shape=[jax.ShapeDtypeStruct((D, cl), F32), jax.ShapeDtypeStruct((1, dmod_all.shape[1]), F32)],
        compiler_params=_params(),
    )(ca, dmod_cols, dmod_all)


def _sum_devices(v):
    def body(v_ref, o_ref):
        tot = v_ref[0]
        for k in range(1, N_DEV):
            tot = tot + v_ref[k]
        o_ref[...] = tot

    return pl.pallas_call(body, name="sum_devices", out_shape=jax.ShapeDtypeStruct(v.shape[1:], F32))(v)


def _adamw(name, w, g, m, v):
    rows, cols = w.shape
    tr = _div(rows, 256, 8)

    def body(w_ref, g_ref, m_ref, v_ref, d_ref, nm_ref, nv_ref):
        gv = g_ref[...]
        nm = ADAM_B1 * m_ref[...] + (1.0 - ADAM_B1) * gv
        nv = ADAM_B2 * v_ref[...] + (1.0 - ADAM_B2) * (gv * gv)
        m_hat = nm / (1.0 - ADAM_B1 ** ADAM_STEP)
        v_hat = nv / (1.0 - ADAM_B2 ** ADAM_STEP)
        d_ref[...] = -ADAM_LR * (m_hat / (jnp.sqrt(v_hat) + ADAM_EPS) + ADAM_WD * w_ref[...])
        nm_ref[...] = nm
        nv_ref[...] = nv

    spec = pl.BlockSpec((tr, cols), lambda i: (i, 0))
    shape = jax.ShapeDtypeStruct((rows, cols), F32)
    return pl.pallas_call(
        body, name=name, out_shape=[shape, shape, shape], grid=(rows // tr,),
        in_specs=[spec, spec, spec, spec], out_specs=[spec, spec, spec],
        compiler_params=_params(("arbitrary",)),
    )(w, g, m, v)


def _pad_rows(a, rows):
    return a if a.shape[0] == rows else jnp.pad(a, ((0, rows - a.shape[0]), (0, 0)))


class _Packed:
    def __init__(self, kind, local_shape, D):
        self.kind, self.local_shape, self.D = kind, local_shape, D
        r, c = local_shape
        self.rows = {"T": c, "N": r, "F": r * c // D}[kind]
        self.rows_pad = -(-self.rows // ROW_ALIGN) * ROW_ALIGN

    def pack_local(self, w):
        if self.kind == "T":
            w = w.T
        elif self.kind == "F":
            w = w.reshape(self.rows, self.D)
        return _pad_rows(w, self.rows_pad)

    def full(self, gathered):
        g = gathered[:, :self.rows]
        if self.kind == "F":
            r, c = self.local_shape
            return g.reshape(N_DEV, r, c).transpose(1, 0, 2).reshape(r, N_DEV * c)
        return g.reshape(N_DEV * self.rows, self.D)

    def pack_grad(self, gfull):
        if self.kind == "F":
            r, c = self.local_shape
            g = gfull.reshape(r, N_DEV, c).transpose(1, 0, 2).reshape(N_DEV, self.rows, self.D)
        else:
            g = gfull.reshape(N_DEV, self.rows, self.D)
        if self.rows_pad != self.rows:
            g = jnp.pad(g, ((0, 0), (0, self.rows_pad - self.rows), (0, 0)))
        return g

    def unpack_local(self, g):
        g = g[:self.rows]
        if self.kind == "T":
            return g.T
        if self.kind == "F":
            return g.reshape(self.local_shape)
        return g


BIG = ["ffn1_w_gate", "ffn1_w_up", "ffn1_w_down", "w_in", "w_branch_a", "w_branch_b", "w_out",
       "ffn2_w_gate", "ffn2_w_up", "ffn2_w_down"]
BIG_KIND = {"ffn1_w_gate": "T", "ffn1_w_up": "T", "ffn1_w_down": "N", "w_in": "T", "w_branch_a": "F", "w_branch_b": "F",
            "w_out": "N", "ffn2_w_gate": "T", "ffn2_w_up": "T", "ffn2_w_down": "N"}
SMALL = ["ada_b", "norm_ffn1", "norm_mix", "forget_bias", "norm_ffn2", "norm_final"]


def kernel(x, c, ada_w, ada_b, norm_ffn1, ffn1_w_gate, ffn1_w_up, ffn1_w_down, norm_mix, w_in, forget_bias, w_branch_a, w_branch_b, w_out, norm_ffn2, ffn2_w_gate, ffn2_w_up, ffn2_w_down, norm_final, loss_target, m_ada_w, m_ada_b, m_norm_ffn1, m_ffn1_w_gate, m_ffn1_w_up, m_ffn1_w_down, m_norm_mix, m_w_in, m_forget_bias, m_w_branch_a, m_w_branch_b, m_w_out, m_norm_ffn2, m_ffn2_w_gate, m_ffn2_w_up, m_ffn2_w_down, m_norm_final, v_ada_w, v_ada_b, v_norm_ffn1, v_ffn1_w_gate, v_ffn1_w_up, v_ffn1_w_down, v_norm_mix, v_w_in, v_forget_bias, v_w_branch_a, v_w_branch_b, v_w_out, v_norm_ffn2, v_ffn2_w_gate, v_ffn2_w_up, v_ffn2_w_down, v_norm_final):
    args = dict(locals())
    B, S, D = x.shape
    T = B * S
    cl = ada_w.shape[2]
    n_in = w_in.shape[2] * N_DEV
    nm = 2 * D + 3 * FOX_W + 3 * DIL_W
    nmp = -(-nm // 512) * 512
    GA, GB, QB, QA = 0, D, 2 * D, 2 * D + 3 * FOX_W
    xpos, ypos, cpos = _position()
    me = 4 * xpos + 2 * ypos + cpos

    packs = {n: _Packed(BIG_KIND[n], args[n].shape[1:], D) for n in BIG}
    offs, r = {}, 0
    for n in BIG:
        offs[n] = r
        r += packs[n].rows_pad
    pad_rows = -r % PACK_ROW_QUANTUM
    p_local = jnp.concatenate([packs[n].pack_local(args[n][0]).astype(BF16) for n in BIG]
                              + [jnp.zeros((pad_rows, D), BF16)], axis=0)
    gathered = _weight_allgather(p_local)
    W = {n: packs[n].full(gathered[:, offs[n]:offs[n] + packs[n].rows_pad]) for n in BIG}
    winT = W["w_in"]
    o_f = 3 * DIL_W + 3 * FOX_W
    wmT = jnp.concatenate([winT[o_f + 8:], winT[3 * DIL_W:o_f], winT[:3 * DIL_W], jnp.zeros((nmp - nm, D), BF16)], axis=0)
    wfT = jnp.concatenate([winT[o_f:o_f + 8], jnp.zeros((LANES - 8, D), BF16)], axis=0)

    c_all = _small_allgather(c, "gather_c").reshape(N_DEV * B, D)
    b_cols = lax.dynamic_slice(ada_b, (0, me * cl), (1, cl))
    mod_cols, c_act = _ada_forward(c_all, ada_w[0], b_cols)
    mod_all = _small_allgather(mod_cols, "gather_mod").transpose(1, 0, 2).reshape(N_DEV * B, N_MOD * D)
    mod = lax.dynamic_slice(mod_all, (me * B, 0), (B, N_MOD * D)).reshape(B, N_MOD, 1, D)
    sh1, sc1, gt1, sh2, sc2, gt2, sh3, sc3, gt3 = [mod[:, i] for i in range(N_MOD)]

    x0 = x.reshape(T, D)
    x1, saved1 = _ffn_forward("ffn1", x0, norm_ffn1, sh1, sc1, gt1, W["ffn1_w_gate"], W["ffn1_w_up"], W["ffn1_w_down"], S)

    h2 = _normmod("mix_normmod", x1, norm_mix, sc2, sh2, S)
    tm1k = _div(T, 1024, 8)
    pm = _matmul("mix_proj", "nt", [[(h2, wmT)]], T, nmp, D, tm1k, 512, D, [BF16])[0]
    fraw = _matmul("mix_proj_f", "nt", [[(h2, wfT)]], T, LANES, D, tm1k, LANES, D, [F32])[0]
    fb = jnp.pad(forget_bias, ((0, 0), (0, LANES - FOX_HEADS)))

    def forget_fn(fr, fbv):
        fl = fr + fbv
        lane = lax.broadcasted_iota(jnp.int32, fl.shape, 1)
        ls = jnp.minimum(fl, 0.0) - jnp.log(1.0 + jnp.exp(-jnp.abs(fl)))
        return [jnp.where(lane < FOX_HEADS, ls, 0.0), fl]

    tms = _div(S, 512, 8)
    logsig, flog = _rowwise("forget_gate", forget_fn, T, tms, [(fraw, "row", None), (fb, "vec", None)],
                            [("row", LANES, F32), ("row", LANES, F32)], S)
    cum = _cumsum(logsig.reshape(B, S, LANES))
    cumT = cum[:, :, :8].transpose(0, 2, 1)
    pm3 = pm.reshape(B, S, nmp)
    tq = _div(S, 512, LANES)
    qcol, kcol, vcol = QB // FOX_W, QB // FOX_W + 1, QB // FOX_W + 2
    o_b, o_b32, lse_b = _fox_fwd(pm3, cumT, qcol, kcol, vcol, tq)
    y_b = o_b.reshape(T, FOX_W)

    qa_blk = QA // DIL_GW
    dil_src, dil_cols = [], []
    for g, (_, d) in enumerate(DIL_GROUPS):
        if d == 1:
            dil_src.append(pm3)
            dil_cols.append((qa_blk + g, qa_blk + N_DIL + g, qa_blk + 2 * N_DIL + g))
        else:
            starts = [QA + (i * N_DIL + g) * DIL_GW for i in range(3)]
            qkv = jnp.concatenate([pm[:, c:c + DIL_GW] for c in starts], axis=1)
            dil_src.append(_residue_order(qkv, B, S, d))
            dil_cols.append((0, 1, 2))
    dil_o, dil_lse = [], []
    for g, (_, d) in enumerate(DIL_GROUPS):
        o_g, lse_g = _dil_fwd(g, dil_src[g], dil_cols[g])
        dil_o.append(_token_order(o_g, B, S, d))
        dil_lse.append(_token_order(lse_g, B, S, d))
    y_a, L_a = _dil_combine(dil_o, dil_lse, T, S)

    wa, wb, wout = W["w_branch_a"], W["w_branch_b"], W["w_out"]
    tnd = _div(D, 512, LANES)
    yap = _matmul("mix_branch_a", "nn", [[(y_a, wa)]], T, D, DIL_GW, tm1k, tnd, DIL_GW, [BF16])[0]

    def merge(accs, ex):
        yapv, gav, gbv = ex
        ybp = accs[0]
        return [ybp, _sigmoid(gav.astype(F32)) * yapv.astype(F32) + _sigmoid(gbv.astype(F32)) * ybp]

    ybp, merged = _matmul("mix_branch_b", "nn", [[(y_b, wb)]], T, D, FOX_W, tm1k, tnd, FOX_W, [BF16, BF16],
                          extras=[(yap, "tile", 0), (pm, "tile", GA), (pm, "tile", GB)], epilogue=merge)

    def out_proj(accs, ex):
        xv, gtv = ex
        return [xv + gtv * accs[0], accs[0]]

    x2, ymix = _matmul("mix_out", "nn", [[(merged, wout)]], T, D, D, tms, tnd, D, [F32, BF16],
                       extras=[(x1, "tile", 0), (gt2, "brow", 0)], epilogue=out_proj, rows_per_example=S)

    x3, saved3 = _ffn_forward("ffn2", x2, norm_ffn2, sh3, sc3, gt3, W["ffn2_w_gate"], W["ffn2_w_up"], W["ffn2_w_down"], S)

    dx3, loss_b, dg_final = _loss_head(x3, loss_target.reshape(T, D), norm_final.reshape(1, D), S)
    dx2, (dsh3, dsc3, dgt3, dg3), (dwg2, dwu2, dwd2) = _ffn_backward(
        "ffn2", dx3, saved3, norm_ffn2, sc3, gt3, W["ffn2_w_gate"], W["ffn2_w_up"], W["ffn2_w_down"], S)

    dym, dgt2 = _gate_grad("mix_gate_grad", dx2, ymix, gt2, 1.0, S)
    tkw = _div(T, 1024, LANES)
    dwout = _matmul("mix_dw_out", "tn", [[(merged, dym)]], D, D, T, _div(D, 1024, LANES), D, tkw, [F32])[0]

    def merge_grad(accs, ex):
        gav, gbv, yapv, ybpv = [e.astype(F32) for e in ex]
        dm = accs[0]
        sga, sgb = _sigmoid(gav), _sigmoid(gbv)
        return [dm * sga, dm * sgb, dm * yapv * sga * (1.0 - sga), dm * ybpv * sgb * (1.0 - sgb)]

    dyap, dybp, dga, dgb = _matmul("mix_merge_grad", "nt", [[(dym, wout)]], T, D, D, tm1k, tnd, D, [BF16] * 4,
                                   extras=[(pm, "tile", GA), (pm, "tile", GB), (yap, "tile", 0), (ybp, "tile", 0)],
                                   epilogue=merge_grad)
    dwa = _matmul("mix_dw_a", "tn", [[(y_a, dyap)]], DIL_GW, D, T, DIL_GW, D, tkw, [F32])[0]
    dwb = _matmul("mix_dw_b", "tn", [[(y_b, dybp)]], FOX_W, D, T, FOX_W, D, tkw, [F32])[0]
    dy_a = _matmul("mix_dy_a", "nt", [[(dyap, wa)]], T, DIL_GW, D, tm1k, DIL_GW, D, [F32])[0]
    dy_b = _matmul("mix_dy_b", "nt", [[(dybp, wb)]], T, FOX_W, D, tm1k, FOX_W, D, [BF16])[0]

    do3 = dy_b.reshape(B, S, FOX_W)
    delta_b = _fox_delta(dy_b, o_b32.reshape(T, FOX_W), T, S).reshape(B, S, LANES)
    dq_b, ds_rows, dk_b, dv_b, ds_cols = _fox_bwd(pm3, do3, delta_b, lse_b, cumT, qcol, kcol, vcol, tq)

    delta_a, dy_a16 = _dil_delta(dy_a, y_a, T, S)
    dqs, dks, dvs = [], [], []
    for g, (_, d) in enumerate(DIL_GROUPS):
        dq_g, dk_g, dv_g = _dil_bwd(g, dil_src[g], dil_cols[g], _residue_order(L_a, B, S, d),
                                    _residue_order(dy_a16, B, S, d), _residue_order(delta_a, B, S, d))
        dqs.append(_token_order(dq_g, B, S, d))
        dks.append(_token_order(dk_g, B, S, d))
        dvs.append(_token_order(dv_g, B, S, d))

    dcum = ds_rows - ds_cols
    dcum_run = _cumsum(dcum)
    dcum_tot = dcum_run[:, S - 1:S, :]

    def forget_grad_fn(run, dcv, fl, tot):
        lane = lax.broadcasted_iota(jnp.int32, fl.shape, 1)
        df = jnp.where(lane < FOX_HEADS, (tot - run + dcv) * _sigmoid(-fl), 0.0)
        return [df, _colsum(df)]

    df16, dfb = _rowwise("forget_gate_grad", forget_grad_fn, T, tms,
                         [(dcum_run.reshape(T, LANES), "row", None), (dcum.reshape(T, LANES), "row", None), (flog, "row", None),
                          (dcum_tot, "bvec", None)],
                         [("row", LANES, BF16), ("bacc", LANES, F32)], S)

    dpm = jnp.concatenate([dga, dgb, dq_b.reshape(T, FOX_W), dk_b.reshape(T, FOX_W), dv_b.reshape(T, FOX_W)]
                          + dqs + dks + dvs + ([jnp.zeros((T, nmp - nm), BF16)] if nmp > nm else []), axis=1)
    tmn = _div(nmp, 512, LANES)
    dwmT = _matmul("mix_dw_in", "tn", [[(dpm, h2)]], nmp, D, T, tmn, D, tkw, [F32])[0]
    dwfT = _matmul("mix_dw_f", "tn", [[(df16, h2)]], LANES, D, T, LANES, D, tkw, [F32])[0]
    dh2f = _matmul("mix_dh_f", "nn", [[(df16, wfT)]], T, D, LANES, tm1k, tnd, LANES, [F32])[0]

    def add_tile(accs, ex):
        return [accs[0] + ex[0]]

    dh2 = _matmul("mix_dh", "nn", [[(dpm, wmT)]], T, D, nmp, _div(T, 512, 8), tnd, _div(nmp, 2048, LANES), [F32],
                  extras=[(dh2f, "tile", 0)], epilogue=add_tile)[0]
    dx1, dsh2, dsc2, dgmix = _normmod_bwd("mix_normmod_bwd", x1, norm_mix, sc2, dh2, dx2, S)

    dx0, (dsh1, dsc1, dgt1, dg1), (dwg1, dwu1, dwd1) = _ffn_backward(
        "ffn1", dx1, saved1, norm_ffn1, sc1, gt1, W["ffn1_w_gate"], W["ffn1_w_up"], W["ffn1_w_down"], S)
    grad_x = dx0.reshape(B, S, D)

    dmod = jnp.concatenate([dsh1, dsc1, dgt1, dsh2, dsc2, dgt2, dsh3, dsc3, dgt3], axis=1).reshape(B, N_MOD * D)
    dmod_all = _small_allgather(dmod, "gather_dmod").reshape(N_DEV * B, N_MOD * D)
    dmod_cols = lax.dynamic_slice(dmod_all, (0, me * cl), (N_DEV * B, cl))
    g_ada_w, g_ada_b = _ada_backward(c_act, dmod_cols, dmod_all)

    fbg = jnp.sum(dfb, axis=0)
    small = jnp.concatenate([jnp.sum(dg1, axis=0), jnp.sum(dgmix, axis=0), jnp.sum(dg3, axis=0), jnp.sum(dg_final, axis=0),
                             fbg, jnp.sum(loss_b, axis=0)], axis=1)
    small = _sum_devices(_small_allgather(small, "gather_small"))
    g_small = {"norm_ffn1": small[:, 0:D], "norm_mix": small[:, D:2 * D], "norm_ffn2": small[:, 2 * D:3 * D],
               "norm_final": small[:, 3 * D:4 * D], "forget_bias": small[:, 4 * D:4 * D + FOX_HEADS], "ada_b": g_ada_b}
    loss = small[0, 4 * D + LANES]

    dwinT = jnp.concatenate([dwmT[QA:QA + 3 * DIL_W], dwmT[QB:QB + 3 * FOX_W], dwfT[:8], dwmT[GA:2 * D]], axis=0)
    gfull = {"ffn1_w_gate": dwg1, "ffn1_w_up": dwu1, "ffn1_w_down": dwd1, "w_in": dwinT, "w_branch_a": dwa, "w_branch_b": dwb,
             "w_out": dwout, "ffn2_w_gate": dwg2, "ffn2_w_up": dwu2, "ffn2_w_down": dwd2}
    g_packed = jnp.concatenate([packs[n].pack_grad(gfull[n]) for n in BIG] + [jnp.zeros((N_DEV, pad_rows, D), F32)], axis=1)
    g_local = _reduce_scatter(g_packed)
    grads = {n: packs[n].unpack_local(g_local[offs[n]:offs[n] + packs[n].rows_pad])[None] for n in BIG}
    grads["ada_w"] = g_ada_w[None]

    delta, new_m, new_v = {}, {}, {}
    for n in ["ada_w"] + BIG:
        shp = args[n].shape
        d_, m_, v_ = _adamw(f"adamw_{n}", args[n][0], grads[n][0], args["m_" + n][0], args["v_" + n][0])
        delta[n], new_m[n], new_v[n] = d_.reshape(shp), m_.reshape(shp), v_.reshape(shp)
    sizes = [args[n].size for n in SMALL]
    tot = sum(sizes)
    padded = -(-tot // (8 * LANES)) * (8 * LANES)

    def flat(get):
        v = jnp.concatenate([get(n).reshape(-1) for n in SMALL])
        return jnp.pad(v, (0, padded - tot)).reshape(8, padded // 8)

    d_s, m_s, v_s = _adamw("adamw_small", flat(lambda n: args[n]), flat(lambda n: g_small[n]), flat(lambda n: args["m_" + n]),
                           flat(lambda n: args["v_" + n]))
    o = 0
    for n, sz in zip(SMALL, sizes):
        shp = args[n].shape
        grads[n] = g_small[n].reshape(shp)
        delta[n] = d_s.reshape(-1)[o:o + sz].reshape(shp)
        new_m[n] = m_s.reshape(-1)[o:o + sz].reshape(shp)
        new_v[n] = v_s.reshape(-1)[o:o + sz].reshape(shp)
        o += sz

    order = ["ada_w", "ada_b", "norm_ffn1", "ffn1_w_gate", "ffn1_w_up", "ffn1_w_down", "norm_mix", "w_in", "forget_bias",
             "w_branch_a", "w_branch_b", "w_out", "norm_ffn2", "ffn2_w_gate", "ffn2_w_up", "ffn2_w_down", "norm_final"]
    return (loss, grad_x, *[grads[n] for n in order], *[delta[n] for n in order], *[new_m[n] for n in order],
            *[new_v[n] for n in order])
```

```python
import functools
import math

import jax
import jax.numpy as jnp
from jax import lax
from jax.experimental import pallas as pl
from jax.experimental.pallas import tpu as pltpu

F32 = jnp.float32
BF16 = jnp.bfloat16
MESH = pl.DeviceIdType.MESH
ANY = pl.BlockSpec(memory_space=pl.ANY)
VMEM_SPEC = pl.BlockSpec(memory_space=pltpu.VMEM)

N_DEV = 8
HEAD_DIM = 64
BLOCK = 128
DIL_GROUPS = ((128, 1), (512, 4), (2048, 16))
N_DIL = len(DIL_GROUPS)
DIL_HPG = 4
DIL_GW = DIL_HPG * HEAD_DIM
DIL_W = N_DIL * DIL_GW
FOX_HEADS = 8
FOX_W = FOX_HEADS * HEAD_DIM
N_MOD = 9
RMS_EPS = 1e-6
ALIBI_MAX_BIAS = 8.0
NEG_INF = -1e30
ADAM_LR, ADAM_B1, ADAM_B2, ADAM_EPS, ADAM_WD, ADAM_STEP = 0.001, 0.9, 0.999, 1e-08, 0.01, 10
V7X_VMEM_LIMIT = 52 * 1024 * 1024
LANES = 128
ROW_ALIGN = 16
PACK_ROW_QUANTUM = 256
FOX_STRIP = 32
SCALE = 1.0 / math.sqrt(HEAD_DIM)


def _div(dim, target, quantum):
    best = None
    for t in range(quantum, min(dim, target) + 1, quantum):
        if dim % t == 0:
            best = t
    return best or dim


def _params(sem=None):
    return pltpu.CompilerParams(dimension_semantics=sem, vmem_limit_bytes=V7X_VMEM_LIMIT)


def _sigmoid(x):
    return 1.0 / (1.0 + jnp.exp(-x))


def _position():
    x, y, c = lax.axis_index("x"), lax.axis_index("y"), lax.axis_index("c")
    return x, y, c


def _small_allgather(v, name):
    rows, cols = v.shape

    def body(v_ref, out_ref, send_sems, recv_sems):
        x, y, c = _position()
        me = 4 * x + 2 * y + c
        out_ref[me] = v_ref[...]

        def peer(k):
            return (1 - x if k & 4 else x, 1 - y if k & 2 else y, 1 - c if k & 1 else c)

        def copy(k, slot):
            return pltpu.make_async_remote_copy(
                src_ref=v_ref, dst_ref=out_ref.at[slot], send_sem=send_sems.at[k - 1], recv_sem=recv_sems.at[k - 1],
                device_id=peer(k), device_id_type=MESH)

        sends = [copy(k, me) for k in range(1, N_DEV)]
        for cp in sends:
            cp.start()
        for k in range(1, N_DEV):
            px, py, pc = peer(k)
            copy(k, 4 * px + 2 * py + pc).wait_recv()
        for cp in sends:
            cp.wait_send()

    return pl.pallas_call(
        body, name=name,
        out_shape=jax.ShapeDtypeStruct((N_DEV, rows, cols), v.dtype),
        in_specs=[VMEM_SPEC], out_specs=VMEM_SPEC,
        scratch_shapes=[pltpu.SemaphoreType.DMA((N_DEV - 1,)), pltpu.SemaphoreType.DMA((N_DEV - 1,))],
    )(v)


def _weight_allgather(p):
    rows, cols = p.shape

    def body(p_ref, out_ref, send_sems, recv_sems, local_sem):
        x, y, c = _position()
        me, sibling = (x, y, c), (x, y, 1 - c)
        chips = [(1 - x, y), (x, 1 - y), (1 - x, 1 - y)]

        def slot(px, py, pc):
            return out_ref.at[4 * px + 2 * py + pc]

        def copy(k, block, to, src=None):
            return pltpu.make_async_remote_copy(
                src_ref=slot(*block) if src is None else src, dst_ref=slot(*block),
                send_sem=send_sems.at[k], recv_sem=recv_sems.at[k], device_id=to, device_id_type=MESH)

        mine = pltpu.make_async_copy(p_ref, slot(*me), local_sem)
        mine.start()
        first = [copy(0, me, sibling, src=p_ref)]
        first += [copy(1 + j, me, (*chip, c), src=p_ref) for j, chip in enumerate(chips)]
        for cp in first:
            cp.start()
        passed = [copy(4 + j, (*chip, c), sibling) for j, chip in enumerate(chips)]
        for j, chip in enumerate(chips):
            copy(1 + j, (*chip, c), me).wait_recv()
            passed[j].start()
        copy(0, sibling, me).wait_recv()
        for j, chip in enumerate(chips):
            copy(4 + j, (*chip, 1 - c), me).wait_recv()
        for cp in first + passed:
            cp.wait_send()
        mine.wait()

    return pl.pallas_call(
        body, name="weight_allgather",
        out_shape=jax.ShapeDtypeStruct((N_DEV, rows, cols), p.dtype),
        in_specs=[ANY], out_specs=ANY,
        scratch_shapes=[pltpu.SemaphoreType.DMA((7,)), pltpu.SemaphoreType.DMA((7,)), pltpu.SemaphoreType.DMA],
    )(p)


def _grad_exchange_sibling(g):
    _, rows, cols = g.shape

    def body(g_ref, out_ref, send_sems, recv_sems):
        x, y, c = _position()
        sibling = (x, y, 1 - c)

        def copy(q):
            px, py = q >> 1, q & 1
            return pltpu.make_async_remote_copy(
                src_ref=g_ref.at[4 * px + 2 * py + (1 - c)], dst_ref=out_ref.at[q],
                send_sem=send_sems.at[q], recv_sem=recv_sems.at[q], device_id=sibling, device_id_type=MESH)

        copies = [copy(q) for q in range(4)]
        for cp in copies:
            cp.start()
        for cp in copies:
            cp.wait_recv()
        for cp in copies:
            cp.wait_send()

    return pl.pallas_call(
        body, name="grad_exchange_sibling",
        out_shape=jax.ShapeDtypeStruct((4, rows, cols), g.dtype),
        in_specs=[ANY], out_specs=ANY,
        scratch_shapes=[pltpu.SemaphoreType.DMA((4,)), pltpu.SemaphoreType.DMA((4,))],
    )(g)


def _grad_exchange_chips(s):
    _, rows, cols = s.shape

    def body(s_ref, out_ref, send_sems, recv_sems):
        x, y, c = _position()
        chips = [(1 - x, y), (x, 1 - y), (1 - x, 1 - y)]

        def copy(k):
            return pltpu.make_async_remote_copy(
                src_ref=s_ref.at[k], dst_ref=out_ref.at[k], send_sem=send_sems.at[k], recv_sem=recv_sems.at[k],
                device_id=(*chips[k], c), device_id_type=MESH)

        copies = [copy(k) for k in range(3)]
        for cp in copies:
            cp.start()
        for cp in copies:
            cp.wait_recv()
        for cp in copies:
            cp.wait_send()

    return pl.pallas_call(
        body, name="grad_exchange_chips",
        out_shape=jax.ShapeDtypeStruct((3, rows, cols), s.dtype),
        in_specs=[ANY], out_specs=ANY,
        scratch_shapes=[pltpu.SemaphoreType.DMA((3,)), pltpu.SemaphoreType.DMA((3,))],
    )(s)


def _chip_partial_sums(g, recv_sib, jj, qq):
    _, rows, cols = g.shape
    tr = _div(rows, 512, ROW_ALIGN)

    def body(jj_ref, qq_ref, g_ref, r_ref, o_ref):
        o_ref[...] = (g_ref[...] + r_ref[...]).astype(o_ref.dtype)

    return pl.pallas_call(
        body, name="chip_partial_sums",
        out_shape=jax.ShapeDtypeStruct((3, rows, cols), BF16),
        grid_spec=pltpu.PrefetchScalarGridSpec(
            num_scalar_prefetch=2, grid=(3, rows // tr),
            in_specs=[pl.BlockSpec((None, tr, cols), lambda k, i, jj, qq: (jj[k], i, 0)),
                      pl.BlockSpec((None, tr, cols), lambda k, i, jj, qq: (qq[k], i, 0))],
            out_specs=pl.BlockSpec((None, tr, cols), lambda k, i, jj, qq: (k, i, 0))),
        compiler_params=_params(("arbitrary", "arbitrary")),
    )(jj, qq, g, recv_sib)


def _own_partial_sum(g, recv_sib, jj, qq):
    _, rows, cols = g.shape
    tr = _div(rows, 512, ROW_ALIGN)

    def body(jj_ref, qq_ref, g_ref, r_ref, o_ref):
        o_ref[...] = g_ref[...] + r_ref[...]

    return pl.pallas_call(
        body, name="own_partial_sum",
        out_shape=jax.ShapeDtypeStruct((rows, cols), F32),
        grid_spec=pltpu.PrefetchScalarGridSpec(
            num_scalar_prefetch=2, grid=(rows // tr,),
            in_specs=[pl.BlockSpec((None, tr, cols), lambda i, jj, qq: (jj[0], i, 0)),
                      pl.BlockSpec((None, tr, cols), lambda i, jj, qq: (qq[0], i, 0))],
            out_specs=pl.BlockSpec((tr, cols), lambda i, jj, qq: (i, 0))),
        compiler_params=_params(("arbitrary",)),
    )(jj, qq, g, recv_sib)


def _final_grad_sum(own, recv):
    rows, cols = own.shape
    tr = _div(rows, 512, ROW_ALIGN)

    def body(o_ref, r_ref, out_ref):
        out_ref[...] = ((o_ref[...] + r_ref[0].astype(F32)) + r_ref[1].astype(F32)) + r_ref[2].astype(F32)

    return pl.pallas_call(
        body, name="final_grad_sum",
        out_shape=jax.ShapeDtypeStruct((rows, cols), F32),
        grid=(rows // tr,),
        in_specs=[pl.BlockSpec((tr, cols), lambda i: (i, 0)), pl.BlockSpec((3, tr, cols), lambda i: (0, i, 0))],
        out_specs=pl.BlockSpec((tr, cols), lambda i: (i, 0)),
        compiler_params=_params(("arbitrary",)),
    )(own, recv)


def _reduce_scatter(g):
    x, y, c = _position()
    chips = [(1 - x, y), (x, 1 - y), (1 - x, 1 - y)]
    jj = jnp.stack([4 * px + 2 * py + c for px, py in chips]).astype(jnp.int32)
    qq = jnp.stack([2 * px + py for px, py in chips]).astype(jnp.int32)
    jme = jnp.reshape(4 * x + 2 * y + c, (1,)).astype(jnp.int32)
    qme = jnp.reshape(2 * x + y, (1,)).astype(jnp.int32)
    recv_sib = _grad_exchange_sibling(g)
    sums = _chip_partial_sums(g, recv_sib, jj, qq)
    own = _own_partial_sum(g, recv_sib, jme, qme)
    recv = _grad_exchange_chips(sums)
    return _final_grad_sum(own, recv)


def _matmul(name, form, prods, M, N, K, tm, tn, tk, out_dtypes, extras=(), epilogue=None, rows_per_example=None):
    nk = K // tk
    n_acc = len(prods)
    flat = [ab for group in prods for ab in group]
    dims = {"nn": (((1,), (0,)), ((), ())), "nt": (((1,), (1,)), ((), ())), "tn": (((0,), (0,)), ((), ()))}[form]
    direct = nk > 1 and epilogue is None and n_acc == 1 and list(out_dtypes) == [F32]

    def spec(shape, index_map, whole):
        if whole:
            return pl.BlockSpec(shape, index_map, pipeline_mode=pl.Buffered(1))
        return pl.BlockSpec(shape, index_map)

    if form == "tn":
        a_spec = spec((tk, tm), lambda i, j, k: (k, i), nk == 1 and M == tm)
    else:
        a_spec = spec((tm, tk), lambda i, j, k: (i, k), nk == 1 and M == tm)
    if form == "nt":
        b_spec = spec((tn, tk), lambda i, j, k: (j, k), nk == 1 and N == tn)
    else:
        b_spec = spec((tk, tn), lambda i, j, k: (k, j), nk == 1 and N == tn)
    in_specs, operands = [], []
    for a, b in flat:
        in_specs += [a_spec, b_spec]
        operands += [a, b]
    for arr, kind, off in extras:
        if kind == "tile":
            assert off % tn == 0
            in_specs.append(pl.BlockSpec((tm, tn), functools.partial(lambda i, j, k, o: (i, j + o), o=off // tn)))
        else:
            tiles = rows_per_example // tm
            in_specs.append(pl.BlockSpec((None, 1, tn), functools.partial(lambda i, j, k, t: (i // t, 0, j), t=tiles)))
        operands.append(arr)
    n_in, n_out = len(operands), len(out_dtypes)

    def body(*refs):
        in_refs, out_refs, acc_refs = refs[:n_in], refs[n_in:n_in + n_out], refs[n_in + n_out:]
        k = pl.program_id(2)
        partials, p = [], 0
        for group in prods:
            tot = None
            for _ in group:
                d = lax.dot_general(in_refs[2 * p][...], in_refs[2 * p + 1][...], dims, preferred_element_type=F32)
                tot = d if tot is None else tot + d
                p += 1
            partials.append(tot)

        def finish(accs):
            ex = [r[...] for r in in_refs[2 * len(flat):]]
            outs = epilogue(accs, ex) if epilogue is not None else accs
            for r, o in zip(out_refs, outs):
                r[...] = o.astype(r.dtype)

        if nk == 1:
            finish(partials)
        elif direct:
            @pl.when(k == 0)
            def _():
                out_refs[0][...] = partials[0]

            @pl.when(k > 0)
            def _():
                out_refs[0][...] += partials[0]
        else:
            @pl.when(k == 0)
            def _():
                for r, v in zip(acc_refs, partials):
                    r[...] = v

            @pl.when(k > 0)
            def _():
                for r, v in zip(acc_refs, partials):
                    r[...] += v

            @pl.when(k == nk - 1)
            def _():
                finish([r[...] for r in acc_refs])

    outs = pl.pallas_call(
        body, name=name,
        out_shape=[jax.ShapeDtypeStruct((M, N), dt) for dt in out_dtypes],
        grid=(M // tm, N // tn, nk),
        in_specs=in_specs,
        out_specs=[pl.BlockSpec((tm, tn), lambda i, j, k: (i, j)) for _ in out_dtypes],
        scratch_shapes=[pltpu.VMEM((tm, tn), F32) for _ in range(n_acc)] if nk > 1 and not direct else [],
        compiler_params=_params(("parallel", "parallel", "arbitrary")),
    )(*operands)
    return outs


def _rowwise(name, fn, T, tm, ins, outs, rows_per_example):
    tiles = rows_per_example // tm
    n_ex = T // rows_per_example
    in_specs, operands = [], []
    for arr, kind, arg in ins:
        if kind == "row":
            if arg is None:
                in_specs.append(pl.BlockSpec((tm, arr.shape[1]), lambda i: (i, 0)))
            else:
                in_specs.append(pl.BlockSpec((tm, arg[0]), functools.partial(lambda i, cb: (i, cb), cb=arg[1])))
        elif kind == "bvec":
            in_specs.append(pl.BlockSpec((None, 1, arr.shape[2]), lambda i: (i // tiles, 0, 0)))
        else:
            in_specs.append(pl.BlockSpec((1, arr.shape[1]), lambda i: (0, 0)))
        operands.append(arr)
    out_shape, out_specs = [], []
    for kind, cols, dt in outs:
        if kind == "row":
            out_shape.append(jax.ShapeDtypeStruct((T, cols), dt))
            out_specs.append(pl.BlockSpec((tm, cols), lambda i: (i, 0)))
        else:
            out_shape.append(jax.ShapeDtypeStruct((n_ex, 1, cols), F32))
            out_specs.append(pl.BlockSpec((None, 1, cols), lambda i: (i // tiles, 0, 0)))
    n_in = len(operands)

    def body(*refs):
        i = pl.program_id(0)
        vals = fn(*[r[...] for r in refs[:n_in]])
        for (kind, _, _), r, v in zip(outs, refs[n_in:], vals):
            if kind == "row":
                r[...] = v.astype(r.dtype)
            else:
                @pl.when(i % tiles == 0)
                def _():
                    r[...] = jnp.zeros_like(r)

                r[...] += v

    return pl.pallas_call(
        body, name=name, out_shape=out_shape, grid=(T // tm,), in_specs=in_specs, out_specs=out_specs,
        compiler_params=_params(("arbitrary",)),
    )(*operands)


def _colsum(v):
    return jnp.sum(v, axis=0, keepdims=True)


def _rms_parts(x):
    rstd = lax.rsqrt(jnp.mean(x * x, axis=-1, keepdims=True) + RMS_EPS)
    return x * rstd, rstd


def _normmod(name, x, g, sc, sh, S):
    T, D = x.shape

    def fn(xv, gv, scv, shv):
        xhat, _ = _rms_parts(xv)
        return [(xhat * gv) * (1.0 + scv) + shv]

    return _rowwise(name, fn, T, _div(S, 512, 8), [(x, "row", None), (g, "vec", None), (sc, "bvec", None), (sh, "bvec", None)],
                    [("row", D, BF16)], S)[0]


def _normmod_bwd(name, x, g, sc, dh, dres, S):
    T, D = x.shape

    def fn(xv, gv, scv, dhv, drv):
        xhat, rstd = _rms_parts(xv)
        n = xhat * gv
        dn = dhv * (1.0 + scv)
        dxh = dn * gv
        dx = rstd * (dxh - xhat * jnp.mean(dxh * xhat, axis=-1, keepdims=True))
        return [drv + dx, _colsum(dhv), _colsum(dhv * n), _colsum(dn * xhat)]

    return _rowwise(name, fn, T, _div(S, 256, 8),
                    [(x, "row", None), (g, "vec", None), (sc, "bvec", None), (dh, "row", None), (dres, "row", None)],
                    [("row", D, F32), ("bacc", D, F32), ("bacc", D, F32), ("bacc", D, F32)], S)


def _gate_grad(name, dx, y, gt, coeff, S):
    T, D = dx.shape

    def fn(dxv, yv, gtv):
        return [coeff * gtv * dxv, _colsum(coeff * dxv * yv.astype(F32))]

    return _rowwise(name, fn, T, _div(S, 512, 8), [(dx, "row", None), (y, "row", None), (gt, "bvec", None)],
                    [("row", D, BF16), ("bacc", D, F32)], S)


def _ffn_forward(tag, x, g, sh, sc, gt, wgT, wuT, wd, S):
    T, D = x.shape
    F = wd.shape[0]
    h = _normmod(f"{tag}_normmod", x, g, sc, sh, S)

    def gateup(accs, ex):
        a, u = accs
        return [a, u, a * _sigmoid(a) * u]

    a, u, s = _matmul(f"{tag}_gateup", "nt", [[(h, wgT)], [(h, wuT)]], T, F, D, _div(T, 256, 8), F, D,
                      [BF16, BF16, BF16], epilogue=gateup)

    def down(accs, ex):
        xv, gtv = ex
        return [xv + 0.5 * gtv * accs[0], accs[0]]

    tmd = _div(S, 512, 8)
    x_new, y = _matmul(f"{tag}_down", "nn", [[(s, wd)]], T, D, F, tmd, D, F, [F32, BF16],
                       extras=[(x, "tile", 0), (gt, "brow", 0)], epilogue=down, rows_per_example=S)
    return x_new, (x, h, a, u, s, y)


def _ffn_backward(tag, dx_out, saved, g, sc, gt, wgT, wuT, wd, S):
    x, h, a, u, s, y = saved
    T, D = x.shape
    F = wd.shape[0]
    dy, dgt = _gate_grad(f"{tag}_gate_grad", dx_out, y, gt, 0.5, S)

    def act_grad(accs, ex):
        ds = accs[0]
        av, uv = ex[0].astype(F32), ex[1].astype(F32)
        sg = _sigmoid(av)
        return [ds * uv * (sg * (1.0 + av * (1.0 - sg))), ds * (av * sg)]

    da, du = _matmul(f"{tag}_act_grad", "nt", [[(dy, wd)]], T, F, D, _div(T, 256, 8), F, D, [BF16, BF16],
                     extras=[(a, "tile", 0), (u, "tile", 0)], epilogue=act_grad)
    tkw = _div(T, 512, LANES)
    dwd = _matmul(f"{tag}_dw_down", "tn", [[(s, dy)]], F, D, T, F, D, tkw, [F32])[0]
    dwgT = _matmul(f"{tag}_dw_gate", "tn", [[(da, h)]], F, D, T, F, D, tkw, [F32])[0]
    dwuT = _matmul(f"{tag}_dw_up", "tn", [[(du, h)]], F, D, T, F, D, tkw, [F32])[0]
    dh = _matmul(f"{tag}_dh", "nn", [[(da, wgT), (du, wuT)]], T, D, F, _div(T, 512, 8), D, F, [F32])[0]
    dx_in, dsh, dsc, dg = _normmod_bwd(f"{tag}_normmod_bwd", x, g, sc, dh, dx_out, S)
    return dx_in, (dsh, dsc, dgt, dg), (dwgT, dwuT, dwd)


def _loss_head(x, tgt, g, S):
    T, D = x.shape

    def fn(xv, tv, gv):
        xhat, rstd = _rms_parts(xv)
        e = xhat * gv - tv
        loss = jnp.broadcast_to(0.5 / D * jnp.sum(_colsum(e * e), axis=1, keepdims=True), (1, LANES))
        dy = e * (1.0 / D)
        dxh = dy * gv
        dx = rstd * (dxh - xhat * jnp.mean(dxh * xhat, axis=-1, keepdims=True))
        return [dx, loss, _colsum(dy * xhat)]

    return _rowwise("loss_head", fn, T, _div(S, 512, 8), [(x, "row", None), (tgt, "row", None), (g, "vec", None)],
                    [("row", D, F32), ("bacc", LANES, F32), ("bacc", D, F32)], S)


def _cumsum(v):
    B, S, _ = v.shape
    rows = _div(S, 1024, BLOCK)

    def body(x_ref, o_ref, carry):
        i = pl.program_id(1)

        @pl.when(i == 0)
        def _():
            carry[...] = jnp.zeros_like(carry)

        r = lax.broadcasted_iota(jnp.int32, (BLOCK, BLOCK), 0)
        c = lax.broadcasted_iota(jnp.int32, (BLOCK, BLOCK), 1)
        tri = (c <= r).astype(F32)
        last = carry[0:1, :]
        for j in range(0, rows, BLOCK):
            cum = jnp.dot(tri, x_ref[j:j + BLOCK, :], precision=lax.Precision.HIGHEST, preferred_element_type=F32) + last
            o_ref[j:j + BLOCK, :] = cum
            last = cum[BLOCK - 1:BLOCK, :]
        carry[...] = jnp.broadcast_to(last, carry.shape)

    return pl.pallas_call(
        body, name="cumsum", out_shape=jax.ShapeDtypeStruct(v.shape, F32), grid=(B, S // rows),
        in_specs=[pl.BlockSpec((None, rows, LANES), lambda b, i: (b, i, 0))],
        out_specs=pl.BlockSpec((None, rows, LANES), lambda b, i: (b, i, 0)),
        scratch_shapes=[pltpu.VMEM((8, LANES), F32)],
        compiler_params=_params(("arbitrary", "arbitrary")),
    )(v)


def _fox_scores(q, k, cq, ck, qpos, kpos):
    s = lax.dot_general(q, k, (((1,), (1,)), ((), ())), preferred_element_type=F32) * SCALE + cq - ck
    return jnp.where(kpos <= qpos, s, NEG_INF)


def _fox_positions(qi, kj, tq, tk):
    qpos = qi * tq + lax.broadcasted_iota(jnp.int32, (tq, tk), 0)
    kpos = kj * tk + lax.broadcasted_iota(jnp.int32, (tq, tk), 1)
    return qpos, kpos


def _fox_forward(pm3, cum, cumT, qcol, kcol, vcol, tq):
    B, S, _ = pm3.shape
    nq = S // tq

    def body(q_ref, k_ref, v_ref, cq_ref, ck_ref, o_ref, o32_ref, lse_ref, m_sc, l_sc, acc_sc):
        qi, kj = pl.program_id(1), pl.program_id(2)

        @pl.when(kj == 0)
        def _():
            m_sc[...] = jnp.full_like(m_sc, NEG_INF)
            l_sc[...] = jnp.zeros_like(l_sc)
            acc_sc[...] = jnp.zeros_like(acc_sc)

        @pl.when(kj <= qi)
        def _():
            qpos, kpos = _fox_positions(qi, kj, tq, tq)
            for h in range(FOX_HEADS):
                hs = slice(HEAD_DIM * h, HEAD_DIM * (h + 1))
                s = _fox_scores(q_ref[:, hs], k_ref[:, hs], cq_ref[:, h:h + 1], ck_ref[h:h + 1, :], qpos, kpos)
                m_prev = m_sc[h]
                m_new = jnp.maximum(m_prev, jnp.max(s, axis=-1, keepdims=True))
                alpha = jnp.exp(m_prev - m_new)
                p = jnp.exp(s - m_new)
                l_sc[h] = alpha * l_sc[h] + jnp.sum(p, axis=-1, keepdims=True)
                acc_sc[:, hs] = alpha * acc_sc[:, hs] + lax.dot_general(
                    p.astype(BF16), v_ref[:, hs], (((1,), (0,)), ((), ())), preferred_element_type=F32)
                m_sc[h] = m_new

        @pl.when(kj == nq - 1)
        def _():
            lse_ref[...] = jnp.zeros_like(lse_ref)
            for h in range(FOX_HEADS):
                hs = slice(HEAD_DIM * h, HEAD_DIM * (h + 1))
                oh = acc_sc[:, hs] / l_sc[h]
                o_ref[:, hs] = oh.astype(o_ref.dtype)
                o32_ref[:, hs] = oh
                lse_ref[:, h:h + 1] = m_sc[h] + jnp.log(l_sc[h])

    return pl.pallas_call(
        body, name="fox_forward",
        out_shape=[jax.ShapeDtypeStruct((B, S, FOX_W), BF16), jax.ShapeDtypeStruct((B, S, FOX_W), F32),
                   jax.ShapeDtypeStruct((B, S, LANES), F32)],
        grid=(B, nq, nq),
        in_specs=[pl.BlockSpec((None, tq, FOX_W), lambda b, i, j: (b, i, qcol)),
                  pl.BlockSpec((None, tq, FOX_W), lambda b, i, j: (b, jnp.minimum(i, j), kcol)),
                  pl.BlockSpec((None, tq, FOX_W), lambda b, i, j: (b, jnp.minimum(i, j), vcol)),
                  pl.BlockSpec((None, tq, LANES), lambda b, i, j: (b, i, 0)),
                  pl.BlockSpec((None, 8, tq), lambda b, i, j: (b, 0, jnp.minimum(i, j)))],
        out_specs=[pl.BlockSpec((None, tq, FOX_W), lambda b, i, j: (b, i, 0)),
                   pl.BlockSpec((None, tq, FOX_W), lambda b, i, j: (b, i, 0)),
                   pl.BlockSpec((None, tq, LANES), lambda b, i, j: (b, i, 0))],
        scratch_shapes=[pltpu.VMEM((FOX_HEADS, tq, 1), F32), pltpu.VMEM((FOX_HEADS, tq, 1), F32), pltpu.VMEM((tq, FOX_W), F32)],
        compiler_params=_params(("parallel", "parallel", "arbitrary")),
    )(pm3, pm3, pm3, cum, cumT)


def _fox_dq(pm3, do, delta, lse, cum, cumT, qcol, kcol, vcol, tq):
    B, S, _ = pm3.shape
    nq = S // tq

    def body(q_ref, k_ref, v_ref, do_ref, dl_ref, lse_ref, cq_ref, ck_ref, dq_ref, dc_ref, acc_sc, dc_sc):
        qi, kj = pl.program_id(1), pl.program_id(2)

        @pl.when(kj == 0)
        def _():
            acc_sc[...] = jnp.zeros_like(acc_sc)
            dc_sc[...] = jnp.zeros_like(dc_sc)

        @pl.when(kj <= qi)
        def _():
            qpos, kpos = _fox_positions(qi, kj, tq, tq)
            for h in range(FOX_HEADS):
                hs = slice(HEAD_DIM * h, HEAD_DIM * (h + 1))
                s = _fox_scores(q_ref[:, hs], k_ref[:, hs], cq_ref[:, h:h + 1], ck_ref[h:h + 1, :], qpos, kpos)
                p = jnp.exp(s - lse_ref[:, h:h + 1])
                doh = do_ref[:, hs]
                dp = lax.dot_general(doh, v_ref[:, hs], (((1,), (1,)), ((), ())), preferred_element_type=F32)
                ds = p * (dp - dl_ref[:, h:h + 1])
                dc_sc[h] += jnp.sum(ds, axis=-1, keepdims=True)
                acc_sc[:, hs] += lax.dot_general(ds.astype(BF16), k_ref[:, hs], (((1,), (0,)), ((), ())),
                                                 preferred_element_type=F32)

        @pl.when(kj == nq - 1)
        def _():
            dq_ref[...] = (acc_sc[...] * SCALE).astype(dq_ref.dtype)
            dc_ref[...] = jnp.zeros_like(dc_ref)
            for h in range(FOX_HEADS):
                dc_ref[:, h:h + 1] = dc_sc[h]

    qspec = pl.BlockSpec((None, tq, FOX_W), lambda b, i, j: (b, i, 0))
    lspec = pl.BlockSpec((None, tq, LANES), lambda b, i, j: (b, i, 0))
    return pl.pallas_call(
        body, name="fox_dq",
        out_shape=[jax.ShapeDtypeStruct((B, S, FOX_W), BF16), jax.ShapeDtypeStruct((B, S, LANES), F32)],
        grid=(B, nq, nq),
        in_specs=[pl.BlockSpec((None, tq, FOX_W), lambda b, i, j: (b, i, qcol)),
                  pl.BlockSpec((None, tq, FOX_W), lambda b, i, j: (b, jnp.minimum(i, j), kcol)),
                  pl.BlockSpec((None, tq, FOX_W), lambda b, i, j: (b, jnp.minimum(i, j), vcol)),
                  qspec, lspec, lspec, lspec,
                  pl.BlockSpec((None, 8, tq), lambda b, i, j: (b, 0, jnp.minimum(i, j)))],
        out_specs=[qspec, lspec],
        scratch_shapes=[pltpu.VMEM((tq, FOX_W), F32), pltpu.VMEM((FOX_HEADS, tq, 1), F32)],
        compiler_params=_params(("parallel", "parallel", "arbitrary")),
    )(pm3, pm3, pm3, do, delta, lse, cum, cumT)


def _fox_dkv(pm3, do, delta, lse, cum, cumT, qcol, kcol, vcol, tq):
    B, S, _ = pm3.shape
    nq = S // tq

    def body(q_ref, k_ref, v_ref, do_ref, dl_ref, lse_ref, cq_ref, ck_ref, dk_ref, dv_ref, dc_ref, dk_sc, dv_sc, dc_sc):
        kj, qi = pl.program_id(1), pl.program_id(2)

        @pl.when(qi == 0)
        def _():
            dk_sc[...] = jnp.zeros_like(dk_sc)
            dv_sc[...] = jnp.zeros_like(dv_sc)
            dc_sc[...] = jnp.zeros_like(dc_sc)

        @pl.when(qi >= kj)
        def _():
            qpos, kpos = _fox_positions(qi, kj, tq, tq)
            for h in range(FOX_HEADS):
                hs = slice(HEAD_DIM * h, HEAD_DIM * (h + 1))
                qh = q_ref[:, hs]
                s = _fox_scores(qh, k_ref[:, hs], cq_ref[:, h:h + 1], ck_ref[h:h + 1, :], qpos, kpos)
                p = jnp.exp(s - lse_ref[:, h:h + 1])
                doh = do_ref[:, hs]
                dp = lax.dot_general(doh, v_ref[:, hs], (((1,), (1,)), ((), ())), preferred_element_type=F32)
                ds = p * (dp - dl_ref[:, h:h + 1])
                dv_sc[:, hs] += lax.dot_general(p.astype(BF16), doh, (((0,), (0,)), ((), ())), preferred_element_type=F32)
                dk_sc[:, hs] += lax.dot_general(ds.astype(BF16), qh, (((0,), (0,)), ((), ())), preferred_element_type=F32)
                dc_sc[h:h + 1, :] -= jnp.sum(ds, axis=0, keepdims=True)

        @pl.when(qi == nq - 1)
        def _():
            dk_ref[...] = (dk_sc[...] * SCALE).astype(dk_ref.dtype)
            dv_ref[...] = dv_sc[...].astype(dv_ref.dtype)
            dc_ref[...] = dc_sc[...]

    def qside(width):
        return pl.BlockSpec((None, tq, width), lambda b, j, i: (b, jnp.maximum(i, j), 0))

    kspec = pl.BlockSpec((None, tq, FOX_W), lambda b, j, i: (b, j, 0))
    return pl.pallas_call(
        body, name="fox_dkv",
        out_shape=[jax.ShapeDtypeStruct((B, S, FOX_W), BF16), jax.ShapeDtypeStruct((B, S, FOX_W), BF16),
                   jax.ShapeDtypeStruct((B, 8, S), F32)],
        grid=(B, nq, nq),
        in_specs=[pl.BlockSpec((None, tq, FOX_W), lambda b, j, i: (b, jnp.maximum(i, j), qcol)),
                  pl.BlockSpec((None, tq, FOX_W), lambda b, j, i: (b, j, kcol)),
                  pl.BlockSpec((None, tq, FOX_W), lambda b, j, i: (b, j, vcol)),
                  qside(FOX_W), qside(LANES), qside(LANES), qside(LANES),
                  pl.BlockSpec((None, 8, tq), lambda b, j, i: (b, 0, j))],
        out_specs=[kspec, kspec, pl.BlockSpec((None, 8, tq), lambda b, j, i: (b, 0, j))],
        scratch_shapes=[pltpu.VMEM((tq, FOX_W), F32), pltpu.VMEM((tq, FOX_W), F32), pltpu.VMEM((8, tq), F32)],
        compiler_params=_params(("parallel", "parallel", "arbitrary")),
    )(pm3, pm3, pm3, do, delta, lse, cum, cumT)


def _with_ones(x):
    lane = lax.broadcasted_iota(jnp.int32, (x.shape[0], HEAD_DIM), 1)
    return jnp.concatenate([x, jnp.where(lane == 0, 1.0, 0.0).astype(x.dtype)], axis=1)


def _causal_strip(s, r):
    qpos = r + lax.broadcasted_iota(jnp.int32, s.shape, 0)
    kpos = lax.broadcasted_iota(jnp.int32, s.shape, 1)
    return jnp.where(kpos <= qpos, s, NEG_INF)


NT = (((1,), (1,)), ((), ()))
NN = (((1,), (0,)), ((), ()))
TN = (((0,), (0,)), ((), ()))


def _fox_fwd(pm3, cumT, qcol, kcol, vcol, tq):
    B, S, _ = pm3.shape
    nq = S // tq
    strips = range(0, tq, FOX_STRIP)

    def body(q_ref, k_ref, v_ref, ck_ref, o_ref, o32_ref, lse_ref, s_sc, p_sc, al_sc, m_sc, acc_sc):
        qi, kj = pl.program_id(1), pl.program_id(2)

        @pl.when(kj == 0)
        def _():
            m_sc[...] = jnp.full_like(m_sc, NEG_INF)
            acc_sc[...] = jnp.zeros_like(acc_sc)

        def tile(diagonal):
            for h in range(FOX_HEADS):
                hs = slice(HEAD_DIM * h, HEAD_DIM * (h + 1))
                s_sc[...] = lax.dot_general(q_ref[:, hs] * SCALE, k_ref[:, hs], NT, preferred_element_type=F32)
                ck = ck_ref[h:h + 1, :]
                for r in strips:
                    rows = slice(r, r + FOX_STRIP)
                    s = s_sc[rows, :] - ck
                    if diagonal:
                        s = _causal_strip(s, r)
                    m_prev = m_sc[h, rows, :]
                    m_new = jnp.maximum(m_prev, jnp.max(s, axis=-1, keepdims=True))
                    p_sc[rows, :] = jnp.exp(s - m_new).astype(BF16)
                    al_sc[rows, :] = jnp.exp(m_prev - m_new)
                    m_sc[h, rows, :] = m_new
                acc_sc[h] = al_sc[...] * acc_sc[h] + lax.dot_general(p_sc[...], _with_ones(v_ref[:, hs]), NN,
                                                                     preferred_element_type=F32)

        @pl.when(kj < qi)
        def _():
            tile(False)

        @pl.when(kj == qi)
        def _():
            tile(True)

        @pl.when(kj == nq - 1)
        def _():
            lse_ref[...] = jnp.zeros_like(lse_ref)
            for h in range(FOX_HEADS):
                hs = slice(HEAD_DIM * h, HEAD_DIM * (h + 1))
                acc = acc_sc[h]
                l = acc[:, HEAD_DIM:HEAD_DIM + 1]
                oh = acc[:, :HEAD_DIM] / l
                o_ref[:, hs] = oh.astype(o_ref.dtype)
                o32_ref[:, hs] = oh
                lse_ref[:, h:h + 1] = m_sc[h] + jnp.log(l)

    ospec = pl.BlockSpec((None, tq, FOX_W), lambda b, i, j: (b, i, 0))
    return pl.pallas_call(
        body, name="fox_forward",
        out_shape=[jax.ShapeDtypeStruct((B, S, FOX_W), BF16), jax.ShapeDtypeStruct((B, S, FOX_W), F32),
                   jax.ShapeDtypeStruct((B, S, LANES), F32)],
        grid=(B, nq, nq),
        in_specs=[pl.BlockSpec((None, tq, FOX_W), lambda b, i, j: (b, i, qcol)),
                  pl.BlockSpec((None, tq, FOX_W), lambda b, i, j: (b, jnp.minimum(i, j), kcol)),
                  pl.BlockSpec((None, tq, FOX_W), lambda b, i, j: (b, jnp.minimum(i, j), vcol)),
                  pl.BlockSpec((None, 8, tq), lambda b, i, j: (b, 0, jnp.minimum(i, j)))],
        out_specs=[ospec, ospec, pl.BlockSpec((None, tq, LANES), lambda b, i, j: (b, i, 0))],
        scratch_shapes=[pltpu.VMEM((tq, tq), F32), pltpu.VMEM((tq, tq), BF16), pltpu.VMEM((tq, 1), F32),
                        pltpu.VMEM((FOX_HEADS, tq, 1), F32), pltpu.VMEM((FOX_HEADS, tq, LANES), F32)],
        compiler_params=_params(("parallel", "parallel", "arbitrary")),
    )(pm3, pm3, pm3, cumT)


def _fox_bwd(pm3, do, delta, lse, cumT, qcol, kcol, vcol, tq):
    B, S, _ = pm3.shape
    nq = S // tq
    strips = range(0, tq, FOX_STRIP)

    def body(q_ref, k_ref, v_ref, do_ref, dl_ref, lse_ref, ck_ref, dq_ref, rs_ref, dk_ref, dv_ref, cs_ref,
             s_sc, dp_sc, p_sc, ds_sc, dq_sc, dk_sc, dv_sc):
        kj, qi = pl.program_id(1), pl.program_id(2)

        @pl.when((kj == 0) & (qi == 0))
        def _():
            dq_sc[...] = jnp.zeros_like(dq_sc)

        @pl.when(qi == 0)
        def _():
            dk_sc[...] = jnp.zeros_like(dk_sc)
            dv_sc[...] = jnp.zeros_like(dv_sc)

        def tile(diagonal):
            qrows = pl.ds(pl.multiple_of(qi * tq, tq), tq)
            for h in range(FOX_HEADS):
                hs = slice(HEAD_DIM * h, HEAD_DIM * (h + 1))
                qh, kh, doh = q_ref[:, hs] * SCALE, k_ref[:, hs], do_ref[:, hs]
                s_sc[...] = lax.dot_general(qh, kh, NT, preferred_element_type=F32)
                dp_sc[...] = lax.dot_general(doh, v_ref[:, hs], NT, preferred_element_type=F32)
                ck = ck_ref[h:h + 1, :]
                for r in strips:
                    rows = slice(r, r + FOX_STRIP)
                    s = s_sc[rows, :] - ck
                    if diagonal:
                        s = _causal_strip(s, r)
                    p = jnp.exp(s - lse_ref[rows, h:h + 1])
                    p_sc[rows, :] = p.astype(BF16)
                    ds_sc[rows, :] = (p * (dp_sc[rows, :] - dl_ref[rows, h:h + 1])).astype(BF16)
                dv_sc[:, hs] += lax.dot_general(p_sc[...], doh, TN, preferred_element_type=F32)
                dk_sc[h] += lax.dot_general(ds_sc[...], _with_ones(qh), TN, preferred_element_type=F32)
                dq_sc[h, qrows, :] += lax.dot_general(ds_sc[...], _with_ones(kh), NN, preferred_element_type=F32)

        @pl.when(qi > kj)
        def _():
            tile(False)

        @pl.when(qi == kj)
        def _():
            tile(True)

        @pl.when(qi == nq - 1)
        def _():
            dv_ref[...] = dv_sc[...].astype(dv_ref.dtype)
            cs_ref[...] = jnp.zeros_like(cs_ref)
            for h in range(FOX_HEADS):
                hs = slice(HEAD_DIM * h, HEAD_DIM * (h + 1))
                dk = dk_sc[h]
                dk_ref[:, hs] = dk[:, :HEAD_DIM].astype(dk_ref.dtype)
                cs_ref[:, h:h + 1] = dk[:, HEAD_DIM:HEAD_DIM + 1]

        @pl.when((kj == nq - 1) & (qi == nq - 1))
        def _():
            rs_ref[...] = jnp.zeros_like(rs_ref)
            for h in range(FOX_HEADS):
                hs = slice(HEAD_DIM * h, HEAD_DIM * (h + 1))
                dq_ref[:, hs] = (dq_sc[h, :, :HEAD_DIM] * SCALE).astype(dq_ref.dtype)
                rs_ref[:, h:h + 1] = dq_sc[h, :, HEAD_DIM:HEAD_DIM + 1]

    def qside(width, col=0):
        return pl.BlockSpec((None, tq, width), lambda b, j, i: (b, jnp.maximum(i, j), col))

    kspec = pl.BlockSpec((None, tq, FOX_W), lambda b, j, i: (b, j, 0))
    return pl.pallas_call(
        body, name="fox_backward",
        out_shape=[jax.ShapeDtypeStruct((B, S, FOX_W), BF16), jax.ShapeDtypeStruct((B, S, LANES), F32),
                   jax.ShapeDtypeStruct((B, S, FOX_W), BF16), jax.ShapeDtypeStruct((B, S, FOX_W), BF16),
                   jax.ShapeDtypeStruct((B, S, LANES), F32)],
        grid=(B, nq, nq),
        in_specs=[qside(FOX_W, qcol),
                  pl.BlockSpec((None, tq, FOX_W), lambda b, j, i: (b, j, kcol)),
                  pl.BlockSpec((None, tq, FOX_W), lambda b, j, i: (b, j, vcol)),
                  qside(FOX_W), qside(LANES), qside(LANES),
                  pl.BlockSpec((None, 8, tq), lambda b, j, i: (b, 0, j))],
        out_specs=[pl.BlockSpec((None, S, FOX_W), lambda b, j, i: (b, 0, 0)),
                   pl.BlockSpec((None, S, LANES), lambda b, j, i: (b, 0, 0)),
                   kspec, kspec, pl.BlockSpec((None, tq, LANES), lambda b, j, i: (b, j, 0))],
        scratch_shapes=[pltpu.VMEM((tq, tq), F32), pltpu.VMEM((tq, tq), F32), pltpu.VMEM((tq, tq), BF16),
                        pltpu.VMEM((tq, tq), BF16), pltpu.VMEM((FOX_HEADS, S, LANES), F32),
                        pltpu.VMEM((FOX_HEADS, tq, LANES), F32), pltpu.VMEM((tq, FOX_W), F32)],
        compiler_params=_params(("parallel", "arbitrary", "arbitrary")),
    )(pm3, pm3, pm3, do, delta, lse, cumT)


def _fox_delta(do, o32, T, S):
    def fn(dov, ov):
        prod = dov.astype(F32) * ov
        lane = lax.broadcasted_iota(jnp.int32, (dov.shape[0], LANES), 1)
        delta = jnp.zeros((dov.shape[0], LANES), F32)
        for h in range(FOX_HEADS):
            hs = slice(HEAD_DIM * h, HEAD_DIM * (h + 1))
            delta = jnp.where(lane == h, jnp.sum(prod[:, hs], axis=-1, keepdims=True), delta)
        return [delta]

    return _rowwise("fox_delta", fn, T, _div(S, 512, 8), [(do, "row", None), (o32, "row", None)], [("row", LANES, F32)], S)[0]


def _alibi_slope(group, head):
    return 2.0 ** (-ALIBI_MAX_BIAS * (group * DIL_HPG + head + 1) / (N_DIL * DIL_HPG))


def _dil_tiles(q, k_cur, k_prev, slope, dilation, has_prev):
    qi = lax.broadcasted_iota(jnp.int32, (BLOCK, BLOCK), 0)
    ki = lax.broadcasted_iota(jnp.int32, (BLOCK, BLOCK), 1)
    rel = (qi - ki).astype(F32)
    nt = (((1,), (1,)), ((), ()))
    s_cur = lax.dot_general(q, k_cur, nt, preferred_element_type=F32) * SCALE - (slope * dilation) * rel
    s_cur = jnp.where(ki <= qi, s_cur, NEG_INF)
    s_prev = lax.dot_general(q, k_prev, nt, preferred_element_type=F32) * SCALE - (slope * dilation) * (rel + BLOCK)
    s_prev = jnp.where((ki >= qi) & has_prev, s_prev, NEG_INF)
    return s_cur, s_prev


def _dil_forward(group, pmv, nmb, qa_blk, B, S):
    _, dilation = DIL_GROUPS[group]
    sub = S // dilation
    nb = sub // BLOCK
    qb, kb, vb = qa_blk + group, qa_blk + N_DIL + group, qa_blk + 2 * N_DIL + group

    def body(q_ref, kc_ref, kp_ref, vc_ref, vp_ref, o_ref, lse_ref):
        has_prev = pl.program_id(2) > 0
        lse_ref[...] = jnp.zeros_like(lse_ref)
        for h in range(DIL_HPG):
            hs = slice(HEAD_DIM * h, HEAD_DIM * (h + 1))
            s_cur, s_prev = _dil_tiles(q_ref[:, hs], kc_ref[:, hs], kp_ref[:, hs], _alibi_slope(group, h), dilation, has_prev)
            m = jnp.maximum(jnp.max(s_cur, axis=-1, keepdims=True), jnp.max(s_prev, axis=-1, keepdims=True))
            p_cur, p_prev = jnp.exp(s_cur - m), jnp.exp(s_prev - m)
            l = jnp.sum(p_cur, axis=-1, keepdims=True) + jnp.sum(p_prev, axis=-1, keepdims=True)
            nn = (((1,), (0,)), ((), ()))
            o = (lax.dot_general(p_cur.astype(BF16), vc_ref[:, hs], nn, preferred_element_type=F32)
                 + lax.dot_general(p_prev.astype(BF16), vp_ref[:, hs], nn, preferred_element_type=F32))
            o_ref[:, hs] = o / l
            lse_ref[:, h:h + 1] = m + jnp.log(l)

    def cur(col):
        return pl.BlockSpec((None, BLOCK, DIL_GW), lambda b, r, n: (b, n, r * nmb + col))

    def prev(col):
        return pl.BlockSpec((None, BLOCK, DIL_GW), lambda b, r, n: (b, jnp.maximum(n - 1, 0), r * nmb + col))

    return pl.pallas_call(
        body, name=f"dil_forward_{group}",
        out_shape=[jax.ShapeDtypeStruct((B, sub, dilation * DIL_GW), F32), jax.ShapeDtypeStruct((B, sub, dilation * LANES), F32)],
        grid=(B, dilation, nb),
        in_specs=[cur(qb), cur(kb), prev(kb), cur(vb), prev(vb)],
        out_specs=[pl.BlockSpec((None, BLOCK, DIL_GW), lambda b, r, n: (b, n, r)),
                   pl.BlockSpec((None, BLOCK, LANES), lambda b, r, n: (b, n, r))],
        compiler_params=_params(("parallel", "parallel", "arbitrary")),
    )(pmv, pmv, pmv, pmv, pmv)


def _residue_order(a, B, S, d):
    C = a.shape[-1]
    if d == 1:
        return a.reshape(B, S, C)
    return a.reshape(B, S // d, d, C).transpose(0, 2, 1, 3).reshape(B * d, S // d, C)


def _token_order(a, B, S, d):
    C = a.shape[-1]
    if d == 1:
        return a.reshape(B * S, C)
    return a.reshape(B, d, S // d, C).transpose(0, 2, 1, 3).reshape(B * S, C)


def _band_scores(qh, kcat, slope_d, has_prev):
    qi = lax.broadcasted_iota(jnp.int32, (BLOCK, 2 * BLOCK), 0)
    c = lax.broadcasted_iota(jnp.int32, (BLOCK, 2 * BLOCK), 1)
    s = lax.dot_general(qh, kcat, NT, preferred_element_type=F32) - slope_d * (BLOCK + qi - c).astype(F32)
    valid = (c >= qi) & (c <= qi + BLOCK)
    if has_prev is not None:
        valid = valid & ((c >= BLOCK) | has_prev)
    return jnp.where(valid, s, NEG_INF)


def _band_operands(j, cur_ref, prev_ref, hs):
    if j == 0:
        return jnp.concatenate([prev_ref[:, hs], cur_ref[0:BLOCK, hs]], axis=0)
    return cur_ref[(j - 1) * BLOCK:(j + 1) * BLOCK, hs]


def _dil_specs(Ls, qb, cols):
    nsub = qb // BLOCK
    qcol, kcol, vcol = cols

    def cur(col):
        return pl.BlockSpec((None, qb, DIL_GW), lambda s, n: (s, n, col))

    def prev(col):
        return pl.BlockSpec((None, BLOCK, DIL_GW), lambda s, n: (s, jnp.maximum(n * nsub - 1, 0), col))

    return [cur(qcol), cur(kcol), prev(kcol), cur(vcol), prev(vcol)]


def _dil_fwd(group, src, cols):
    _, dilation = DIL_GROUPS[group]
    nseq, Ls, _ = src.shape
    qb = _div(Ls, 512, BLOCK)
    nsub = qb // BLOCK

    def body(q_ref, kc_ref, kp_ref, vc_ref, vp_ref, o_ref, lse_ref):
        has_prev = pl.program_id(1) > 0
        lse_ref[...] = jnp.zeros_like(lse_ref)
        for h in range(DIL_HPG):
            hs = slice(HEAD_DIM * h, HEAD_DIM * (h + 1))
            for j in range(nsub):
                rows = slice(j * BLOCK, (j + 1) * BLOCK)
                s = _band_scores(q_ref[rows, hs] * SCALE, _band_operands(j, kc_ref, kp_ref, hs),
                                 _alibi_slope(group, h) * dilation, has_prev if j == 0 else None)
                m = jnp.max(s, axis=-1, keepdims=True)
                p = jnp.exp(s - m).astype(BF16)
                acc = lax.dot_general(p, _with_ones(_band_operands(j, vc_ref, vp_ref, hs)), NN, preferred_element_type=F32)
                l = acc[:, HEAD_DIM:HEAD_DIM + 1]
                o_ref[rows, hs] = acc[:, :HEAD_DIM] / l
                lse_ref[rows, h:h + 1] = m + jnp.log(l)

    return pl.pallas_call(
        body, name=f"dil_forward_{group}",
        out_shape=[jax.ShapeDtypeStruct((nseq, Ls, DIL_GW), F32), jax.ShapeDtypeStruct((nseq, Ls, LANES), F32)],
        grid=(nseq, Ls // qb),
        in_specs=_dil_specs(Ls, qb, cols),
        out_specs=[pl.BlockSpec((None, qb, DIL_GW), lambda s, n: (s, n, 0)),
                   pl.BlockSpec((None, qb, LANES), lambda s, n: (s, n, 0))],
        compiler_params=_params(("parallel", "arbitrary")),
    )(src, src, src, src, src)


def _dil_bwd(group, src, cols, Lr, dyr, dlr):
    _, dilation = DIL_GROUPS[group]
    nseq, Ls, _ = src.shape
    qb = _div(Ls, 512, BLOCK)
    nsub, nb = qb // BLOCK, Ls // qb

    def body(q_ref, kc_ref, kp_ref, vc_ref, vp_ref, L_ref, dy_ref, dl_ref, dq_ref, dk_ref, dv_ref, dk_sc, dv_sc):
        n = pl.program_id(1)
        has_prev = n > 0

        @pl.when(n == 0)
        def _():
            dk_sc[...] = jnp.zeros_like(dk_sc)
            dv_sc[...] = jnp.zeros_like(dv_sc)

        base = pl.multiple_of(n * qb, BLOCK)
        for h in range(DIL_HPG):
            hs = slice(HEAD_DIM * h, HEAD_DIM * (h + 1))
            for j in range(nsub):
                rows = slice(j * BLOCK, (j + 1) * BLOCK)
                qh = q_ref[rows, hs] * SCALE
                kcat = _band_operands(j, kc_ref, kp_ref, hs)
                s = _band_scores(qh, kcat, _alibi_slope(group, h) * dilation, has_prev if j == 0 else None)
                p = jnp.exp(s - L_ref[rows, h:h + 1])
                dyh = dy_ref[rows, hs]
                dp = lax.dot_general(dyh, _band_operands(j, vc_ref, vp_ref, hs), NT, preferred_element_type=F32)
                ds = (p * (dp - dl_ref[rows, h:h + 1])).astype(BF16)
                dq_ref[rows, hs] = (lax.dot_general(ds, kcat, NN, preferred_element_type=F32) * SCALE).astype(dq_ref.dtype)
                win = pl.ds(base + j * BLOCK, 2 * BLOCK)
                dk_sc[win, hs] += lax.dot_general(ds, qh, TN, preferred_element_type=F32)
                dv_sc[win, hs] += lax.dot_general(p.astype(BF16), dyh, TN, preferred_element_type=F32)

        @pl.when(n == nb - 1)
        def _():
            dk_ref[...] = dk_sc[BLOCK:, :].astype(dk_ref.dtype)
            dv_ref[...] = dv_sc[BLOCK:, :].astype(dv_ref.dtype)

    own = pl.BlockSpec((None, qb, DIL_GW), lambda s, n: (s, n, 0))
    own128 = pl.BlockSpec((None, qb, LANES), lambda s, n: (s, n, 0))
    whole = pl.BlockSpec((None, Ls, DIL_GW), lambda s, n: (s, 0, 0))
    shape = jax.ShapeDtypeStruct((nseq, Ls, DIL_GW), BF16)
    return pl.pallas_call(
        body, name=f"dil_backward_{group}",
        out_shape=[shape, shape, shape],
        grid=(nseq, nb),
        in_specs=_dil_specs(Ls, qb, cols) + [own128, own, own128],
        out_specs=[own, whole, whole],
        scratch_shapes=[pltpu.VMEM((Ls + BLOCK, DIL_GW), F32), pltpu.VMEM((Ls + BLOCK, DIL_GW), F32)],
        compiler_params=_params(("parallel", "arbitrary")),
    )(src, src, src, src, src, Lr, dyr, dlr)


def _dil_combine(os_, lses, T, S):
    def fn(o0, o1, o2, l0, l1, l2):
        m = jnp.maximum(jnp.maximum(l0, l1), l2)
        e0, e1, e2 = jnp.exp(l0 - m), jnp.exp(l1 - m), jnp.exp(l2 - m)
        tot = e0 + e1 + e2
        w0, w1, w2 = e0 / tot, e1 / tot, e2 / tot
        parts = []
        for h in range(DIL_HPG):
            hs = slice(HEAD_DIM * h, HEAD_DIM * (h + 1))
            parts.append(w0[:, h:h + 1] * o0[:, hs] + w1[:, h:h + 1] * o1[:, hs] + w2[:, h:h + 1] * o2[:, hs])
        return [jnp.concatenate(parts, axis=1), m + jnp.log(tot)]

    ins = [(a, "row", None) for a in os_] + [(a, "row", None) for a in lses]
    return _rowwise("dil_combine", fn, T, _div(S, 512, 8), ins, [("row", DIL_GW, BF16), ("row", LANES, F32)], S)


def _dil_delta(dy, y, T, S):
    def fn(dyv, yv):
        prod = dyv * yv.astype(F32)
        lane = lax.broadcasted_iota(jnp.int32, (dyv.shape[0], LANES), 1)
        delta = jnp.zeros((dyv.shape[0], LANES), F32)
        for h in range(DIL_HPG):
            hs = slice(HEAD_DIM * h, HEAD_DIM * (h + 1))
            delta = jnp.where(lane == h, jnp.sum(prod[:, hs], axis=-1, keepdims=True), delta)
        return [delta, dyv]

    return _rowwise("dil_delta", fn, T, _div(S, 512, 8), [(dy, "row", None), (y, "row", None)],
                    [("row", LANES, F32), ("row", DIL_GW, BF16)], S)


def _dil_dq(group, pmv, nmb, qa_blk, Lv, dyv, deltav, B, S):
    _, dilation = DIL_GROUPS[group]
    sub = S // dilation
    nb = sub // BLOCK
    qb, kb, vb = qa_blk + group, qa_blk + N_DIL + group, qa_blk + 2 * N_DIL + group

    def body(q_ref, kc_ref, kp_ref, vc_ref, vp_ref, L_ref, dy_ref, dl_ref, dq_ref):
        has_prev = pl.program_id(2) > 0
        nt = (((1,), (1,)), ((), ()))
        nn = (((1,), (0,)), ((), ()))
        for h in range(DIL_HPG):
            hs = slice(HEAD_DIM * h, HEAD_DIM * (h + 1))
            s_cur, s_prev = _dil_tiles(q_ref[:, hs], kc_ref[:, hs], kp_ref[:, hs], _alibi_slope(group, h), dilation, has_prev)
            L, delta, dyh = L_ref[:, h:h + 1], dl_ref[:, h:h + 1], dy_ref[:, hs]
            ds_cur = jnp.exp(s_cur - L) * (lax.dot_general(dyh, vc_ref[:, hs], nt, preferred_element_type=F32) - delta)
            ds_prev = jnp.exp(s_prev - L) * (lax.dot_general(dyh, vp_ref[:, hs], nt, preferred_element_type=F32) - delta)
            dq = (lax.dot_general(ds_cur.astype(BF16), kc_ref[:, hs], nn, preferred_element_type=F32)
                  + lax.dot_general(ds_prev.astype(BF16), kp_ref[:, hs], nn, preferred_element_type=F32))
            dq_ref[:, hs] = (dq * SCALE).astype(dq_ref.dtype)

    def cur(col):
        return pl.BlockSpec((None, BLOCK, DIL_GW), lambda b, r, n: (b, n, r * nmb + col))

    def prev(col):
        return pl.BlockSpec((None, BLOCK, DIL_GW), lambda b, r, n: (b, jnp.maximum(n - 1, 0), r * nmb + col))

    own = pl.BlockSpec((None, BLOCK, DIL_GW), lambda b, r, n: (b, n, r))
    own128 = pl.BlockSpec((None, BLOCK, LANES), lambda b, r, n: (b, n, r))
    return pl.pallas_call(
        body, name=f"dil_dq_{group}",
        out_shape=jax.ShapeDtypeStruct((B, sub, dilation * DIL_GW), BF16),
        grid=(B, dilation, nb),
        in_specs=[cur(qb), cur(kb), prev(kb), cur(vb), prev(vb), own128, own, own128],
        out_specs=own,
        compiler_params=_params(("parallel", "parallel", "arbitrary")),
    )(pmv, pmv, pmv, pmv, pmv, Lv, dyv, deltav)


def _dil_dkv(group, pmv, nmb, qa_blk, Lv, dyv, deltav, B, S):
    _, dilation = DIL_GROUPS[group]
    sub = S // dilation
    nb = sub // BLOCK
    qb, kb, vb = qa_blk + group, qa_blk + N_DIL + group, qa_blk + 2 * N_DIL + group

    def body(k_ref, v_ref, q0_ref, q1_ref, L0_ref, L1_ref, dy0_ref, dy1_ref, dl0_ref, dl1_ref, dk_ref, dv_ref):
        has_next = pl.program_id(2) < nb - 1
        qi = lax.broadcasted_iota(jnp.int32, (BLOCK, BLOCK), 0)
        ki = lax.broadcasted_iota(jnp.int32, (BLOCK, BLOCK), 1)
        rel = (qi - ki).astype(F32)
        nt = (((1,), (1,)), ((), ()))
        tn = (((0,), (0,)), ((), ()))
        for h in range(DIL_HPG):
            hs = slice(HEAD_DIM * h, HEAD_DIM * (h + 1))
            bias = _alibi_slope(group, h) * dilation
            kh, vh, q0, q1 = k_ref[:, hs], v_ref[:, hs], q0_ref[:, hs], q1_ref[:, hs]
            s0 = lax.dot_general(q0, kh, nt, preferred_element_type=F32) * SCALE - bias * rel
            s0 = jnp.where(ki <= qi, s0, NEG_INF)
            s1 = lax.dot_general(q1, kh, nt, preferred_element_type=F32) * SCALE - bias * (rel + BLOCK)
            s1 = jnp.where((ki >= qi) & has_next, s1, NEG_INF)
            p0 = jnp.exp(s0 - L0_ref[:, h:h + 1])
            p1 = jnp.exp(s1 - L1_ref[:, h:h + 1])
            dy0, dy1 = dy0_ref[:, hs], dy1_ref[:, hs]
            ds0 = p0 * (lax.dot_general(dy0, vh, nt, preferred_element_type=F32) - dl0_ref[:, h:h + 1])
            ds1 = p1 * (lax.dot_general(dy1, vh, nt, preferred_element_type=F32) - dl1_ref[:, h:h + 1])
            dv = (lax.dot_general(p0.astype(BF16), dy0, tn, preferred_element_type=F32)
                  + lax.dot_general(p1.astype(BF16), dy1, tn, preferred_element_type=F32))
            dk = (lax.dot_general(ds0.astype(BF16), q0, tn, preferred_element_type=F32)
                  + lax.dot_general(ds1.astype(BF16), q1, tn, preferred_element_type=F32))
            dv_ref[:, hs] = dv.astype(dv_ref.dtype)
            dk_ref[:, hs] = (dk * SCALE).astype(dk_ref.dtype)

    def cur(col):
        return pl.BlockSpec((None, BLOCK, DIL_GW), lambda b, r, n: (b, n, r * nmb + col))

    def nxt(col):
        return pl.BlockSpec((None, BLOCK, DIL_GW), lambda b, r, n: (b, jnp.minimum(n + 1, nb - 1), r * nmb + col))

    own = pl.BlockSpec((None, BLOCK, DIL_GW), lambda b, r, n: (b, n, r))
    own_next = pl.BlockSpec((None, BLOCK, DIL_GW), lambda b, r, n: (b, jnp.minimum(n + 1, nb - 1), r))
    own128 = pl.BlockSpec((None, BLOCK, LANES), lambda b, r, n: (b, n, r))
    own128_next = pl.BlockSpec((None, BLOCK, LANES), lambda b, r, n: (b, jnp.minimum(n + 1, nb - 1), r))
    shape = jax.ShapeDtypeStruct((B, sub, dilation * DIL_GW), BF16)
    return pl.pallas_call(
        body, name=f"dil_dkv_{group}",
        out_shape=[shape, shape],
        grid=(B, dilation, nb),
        in_specs=[cur(kb), cur(vb), cur(qb), nxt(qb), own128, own128_next, own, own_next, own128, own128_next],
        out_specs=[own, own],
        compiler_params=_params(("parallel", "parallel", "arbitrary")),
    )(pmv, pmv, pmv, pmv, Lv, Lv, dyv, dyv, deltav, deltav)


def _ada_forward(c_all, w, b):
    n, D = c_all.shape
    cl = w.shape[1]

    def body(c_ref, w_ref, b_ref, o_ref, ca_ref):
        cv = c_ref[...]
        ca = (cv * _sigmoid(cv)).astype(BF16)
        ca_ref[...] = ca
        o_ref[...] = jnp.dot(ca, w_ref[...].astype(BF16), preferred_element_type=F32) + b_ref[...]

    return pl.pallas_call(
        body, name="ada_forward",
        out_shape=[jax.ShapeDtypeStruct((n, cl), F32), jax.ShapeDtypeStruct((n, D), BF16)],
        compiler_params=_params(),
    )(c_all, w, b)


def _ada_backward(ca, dmod_cols, dmod_all):
    n, D = ca.shape
    cl = dmod_cols.shape[1]

    def body(ca_ref, dc_ref, da_ref, gw_ref, gb_ref):
        gw_ref[...] = lax.dot_general(ca_ref[...], dc_ref[...].astype(BF16), (((0,), (0,)), ((), ())), preferred_element_type=F32)
        gb_ref[...] = _colsum(da_ref[...])

    return pl.pallas_call(
        body, name="ada_backward",
        out_shape=[jax.ShapeDtypeStruct((D, cl), F32), jax.ShapeDtypeStruct((1, dmod_all.shape[1]), F32)],
        compiler_params=_params(),
    )(ca, dmod_cols, dmod_all)


def _sum_devices(v):
    def body(v_ref, o_ref):
        tot = v_ref[0]
        for k in range(1, N_DEV):
            tot = tot + v_ref[k]
        o_ref[...] = tot

    return pl.pallas_call(body, name="sum_devices", out_shape=jax.ShapeDtypeStruct(v.shape[1:], F32))(v)


def _adamw(name, w, g, m, v):
    rows, cols = w.shape
    tr = _div(rows, 256, 8)

    def body(w_ref, g_ref, m_ref, v_ref, d_ref, nm_ref, nv_ref):
        gv = g_ref[...]
        nm = ADAM_B1 * m_ref[...] + (1.0 - ADAM_B1) * gv
        nv = ADAM_B2 * v_ref[...] + (1.0 - ADAM_B2) * (gv * gv)
        m_hat = nm / (1.0 - ADAM_B1 ** ADAM_STEP)
        v_hat = nv / (1.0 - ADAM_B2 ** ADAM_STEP)
        d_ref[...] = -ADAM_LR * (m_hat / (jnp.sqrt(v_hat) + ADAM_EPS) + ADAM_WD * w_ref[...])
        nm_ref[...] = nm
        nv_ref[...] = nv

    spec = pl.BlockSpec((tr, cols), lambda i: (i, 0))
    shape = jax.ShapeDtypeStruct((rows, cols), F32)
    return pl.pallas_call(
        body, name=name, out_shape=[shape, shape, shape], grid=(rows // tr,),
        in_specs=[spec, spec, spec, spec], out_specs=[spec, spec, spec],
        compiler_params=_params(("arbitrary",)),
    )(w, g, m, v)


def _pad_rows(a, rows):
    return a if a.shape[0] == rows else jnp.pad(a, ((0, rows - a.shape[0]), (0, 0)))


class _Packed:
    def __init__(self, kind, local_shape, D):
        self.kind, self.local_shape, self.D = kind, local_shape, D
        r, c = local_shape
        self.rows = {"T": c, "N": r, "F": r * c // D}[kind]
        self.rows_pad = -(-self.rows // ROW_ALIGN) * ROW_ALIGN

    def pack_local(self, w):
        if self.kind == "T":
            w = w.T
        elif self.kind == "F":
            w = w.reshape(self.rows, self.D)
        return _pad_rows(w, self.rows_pad)

    def full(self, gathered):
        g = gathered[:, :self.rows]
        if self.kind == "F":
            r, c = self.local_shape
            return g.reshape(N_DEV, r, c).transpose(1, 0, 2).reshape(r, N_DEV * c)
        return g.reshape(N_DEV * self.rows, self.D)

    def pack_grad(self, gfull):
        if self.kind == "F":
            r, c = self.local_shape
            g = gfull.reshape(r, N_DEV, c).transpose(1, 0, 2).reshape(N_DEV, self.rows, self.D)
        else:
            g = gfull.reshape(N_DEV, self.rows, self.D)
        if self.rows_pad != self.rows:
            g = jnp.pad(g, ((0, 0), (0, self.rows_pad - self.rows), (0, 0)))
        return g

    def unpack_local(self, g):
        g = g[:self.rows]
        if self.kind == "T":
            return g.T
        if self.kind == "F":
            return g.reshape(self.local_shape)
        return g


BIG = ["ffn1_w_gate", "ffn1_w_up", "ffn1_w_down", "w_in", "w_branch_a", "w_branch_b", "w_out",
       "ffn2_w_gate", "ffn2_w_up", "ffn2_w_down"]
BIG_KIND = {"ffn1_w_gate": "T", "ffn1_w_up": "T", "ffn1_w_down": "N", "w_in": "T", "w_branch_a": "F", "w_branch_b": "F",
            "w_out": "N", "ffn2_w_gate": "T", "ffn2_w_up": "T", "ffn2_w_down": "N"}
SMALL = ["ada_b", "norm_ffn1", "norm_mix", "forget_bias", "norm_ffn2", "norm_final"]


def kernel(x, c, ada_w, ada_b, norm_ffn1, ffn1_w_gate, ffn1_w_up, ffn1_w_down, norm_mix, w_in, forget_bias, w_branch_a, w_branch_b, w_out, norm_ffn2, ffn2_w_gate, ffn2_w_up, ffn2_w_down, norm_final, loss_target, m_ada_w, m_ada_b, m_norm_ffn1, m_ffn1_w_gate, m_ffn1_w_up, m_ffn1_w_down, m_norm_mix, m_w_in, m_forget_bias, m_w_branch_a, m_w_branch_b, m_w_out, m_norm_ffn2, m_ffn2_w_gate, m_ffn2_w_up, m_ffn2_w_down, m_norm_final, v_ada_w, v_ada_b, v_norm_ffn1, v_ffn1_w_gate, v_ffn1_w_up, v_ffn1_w_down, v_norm_mix, v_w_in, v_forget_bias, v_w_branch_a, v_w_branch_b, v_w_out, v_norm_ffn2, v_ffn2_w_gate, v_ffn2_w_up, v_ffn2_w_down, v_norm_final):
    args = dict(locals())
    B, S, D = x.shape
    T = B * S
    cl = ada_w.shape[2]
    n_in = w_in.shape[2] * N_DEV
    nm = 2 * D + 3 * FOX_W + 3 * DIL_W
    nmp = -(-nm // 512) * 512
    GA, GB, QB, QA = 0, D, 2 * D, 2 * D + 3 * FOX_W
    xpos, ypos, cpos = _position()
    me = 4 * xpos + 2 * ypos + cpos

    packs = {n: _Packed(BIG_KIND[n], args[n].shape[1:], D) for n in BIG}
    offs, r = {}, 0
    for n in BIG:
        offs[n] = r
        r += packs[n].rows_pad
    pad_rows = -r % PACK_ROW_QUANTUM
    p_local = jnp.concatenate([packs[n].pack_local(args[n][0]).astype(BF16) for n in BIG]
                              + [jnp.zeros((pad_rows, D), BF16)], axis=0)
    gathered = _weight_allgather(p_local)
    W = {n: packs[n].full(gathered[:, offs[n]:offs[n] + packs[n].rows_pad]) for n in BIG}
    winT = W["w_in"]
    o_f = 3 * DIL_W + 3 * FOX_W
    wmT = jnp.concatenate([winT[o_f + 8:], winT[3 * DIL_W:o_f], winT[:3 * DIL_W], jnp.zeros((nmp - nm, D), BF16)], axis=0)
    wfT = jnp.concatenate([winT[o_f:o_f + 8], jnp.zeros((LANES - 8, D), BF16)], axis=0)

    c_all = _small_allgather(c, "gather_c").reshape(N_DEV * B, D)
    b_cols = lax.dynamic_slice(ada_b, (0, me * cl), (1, cl))
    mod_cols, c_act = _ada_forward(c_all, ada_w[0], b_cols)
    mod_all = _small_allgather(mod_cols, "gather_mod").transpose(1, 0, 2).reshape(N_DEV * B, N_MOD * D)
    mod = lax.dynamic_slice(mod_all, (me * B, 0), (B, N_MOD * D)).reshape(B, N_MOD, 1, D)
    sh1, sc1, gt1, sh2, sc2, gt2, sh3, sc3, gt3 = [mod[:, i] for i in range(N_MOD)]

    x0 = x.reshape(T, D)
    x1, saved1 = _ffn_forward("ffn1", x0, norm_ffn1, sh1, sc1, gt1, W["ffn1_w_gate"], W["ffn1_w_up"], W["ffn1_w_down"], S)

    h2 = _normmod("mix_normmod", x1, norm_mix, sc2, sh2, S)
    tm1k = _div(T, 1024, 8)
    pm = _matmul("mix_proj", "nt", [[(h2, wmT)]], T, nmp, D, _div(T, 256, 8), nmp, D, [BF16])[0]
    fraw = _matmul("mix_proj_f", "nt", [[(h2, wfT)]], T, LANES, D, tm1k, LANES, D, [F32])[0]
    fb = jnp.pad(forget_bias, ((0, 0), (0, LANES - FOX_HEADS)))

    def forget_fn(fr, fbv):
        fl = fr + fbv
        lane = lax.broadcasted_iota(jnp.int32, fl.shape, 1)
        ls = jnp.minimum(fl, 0.0) - jnp.log(1.0 + jnp.exp(-jnp.abs(fl)))
        return [jnp.where(lane < FOX_HEADS, ls, 0.0), fl]

    tms = _div(S, 512, 8)
    logsig, flog = _rowwise("forget_gate", forget_fn, T, tms, [(fraw, "row", None), (fb, "vec", None)],
                            [("row", LANES, F32), ("row", LANES, F32)], S)
    cum = _cumsum(logsig.reshape(B, S, LANES))
    cumT = cum[:, :, :8].transpose(0, 2, 1)
    pm3 = pm.reshape(B, S, nmp)
    tq = _div(S, 512, LANES)
    qcol, kcol, vcol = QB // FOX_W, QB // FOX_W + 1, QB // FOX_W + 2
    o_b, o_b32, lse_b = _fox_fwd(pm3, cumT, qcol, kcol, vcol, tq)
    y_b = o_b.reshape(T, FOX_W)

    qa_blk = QA // DIL_GW
    dil_src, dil_cols = [], []
    for g, (_, d) in enumerate(DIL_GROUPS):
        if d == 1:
            dil_src.append(pm3)
            dil_cols.append((qa_blk + g, qa_blk + N_DIL + g, qa_blk + 2 * N_DIL + g))
        else:
            starts = [QA + (i * N_DIL + g) * DIL_GW for i in range(3)]
            qkv = jnp.concatenate([pm[:, c:c + DIL_GW] for c in starts], axis=1)
            dil_src.append(_residue_order(qkv, B, S, d))
            dil_cols.append((0, 1, 2))
    dil_o, dil_lse = [], []
    for g, (_, d) in enumerate(DIL_GROUPS):
        o_g, lse_g = _dil_fwd(g, dil_src[g], dil_cols[g])
        dil_o.append(_token_order(o_g, B, S, d))
        dil_lse.append(_token_order(lse_g, B, S, d))
    y_a, L_a = _dil_combine(dil_o, dil_lse, T, S)

    wa, wb, wout = W["w_branch_a"], W["w_branch_b"], W["w_out"]
    tnd = D
    tm5 = _div(T, 512, 8)
    yap = _matmul("mix_branch_a", "nn", [[(y_a, wa)]], T, D, DIL_GW, tm5, tnd, DIL_GW, [BF16])[0]

    def merge(accs, ex):
        yapv, gav, gbv = ex
        ybp = accs[0]
        return [ybp, _sigmoid(gav.astype(F32)) * yapv.astype(F32) + _sigmoid(gbv.astype(F32)) * ybp]

    ybp, merged = _matmul("mix_branch_b", "nn", [[(y_b, wb)]], T, D, FOX_W, tm5, tnd, FOX_W, [BF16, BF16],
                          extras=[(yap, "tile", 0), (pm, "tile", GA), (pm, "tile", GB)], epilogue=merge)

    def out_proj(accs, ex):
        xv, gtv = ex
        return [xv + gtv * accs[0], accs[0]]

    x2, ymix = _matmul("mix_out", "nn", [[(merged, wout)]], T, D, D, tms, tnd, D, [F32, BF16],
                       extras=[(x1, "tile", 0), (gt2, "brow", 0)], epilogue=out_proj, rows_per_example=S)

    x3, saved3 = _ffn_forward("ffn2", x2, norm_ffn2, sh3, sc3, gt3, W["ffn2_w_gate"], W["ffn2_w_up"], W["ffn2_w_down"], S)

    dx3, loss_b, dg_final = _loss_head(x3, loss_target.reshape(T, D), norm_final.reshape(1, D), S)
    dx2, (dsh3, dsc3, dgt3, dg3), (dwg2, dwu2, dwd2) = _ffn_backward(
        "ffn2", dx3, saved3, norm_ffn2, sc3, gt3, W["ffn2_w_gate"], W["ffn2_w_up"], W["ffn2_w_down"], S)

    dym, dgt2 = _gate_grad("mix_gate_grad", dx2, ymix, gt2, 1.0, S)
    tkw = _div(T, 512, LANES)
    dwout = _matmul("mix_dw_out", "tn", [[(merged, dym)]], D, D, T, D, D, tkw, [F32])[0]

    def merge_grad(accs, ex):
        gav, gbv, yapv, ybpv = [e.astype(F32) for e in ex]
        dm = accs[0]
        sga, sgb = _sigmoid(gav), _sigmoid(gbv)
        return [dm * sga, dm * sgb, dm * yapv * sga * (1.0 - sga), dm * ybpv * sgb * (1.0 - sgb)]

    dyap, dybp, dga, dgb = _matmul("mix_merge_grad", "nt", [[(dym, wout)]], T, D, D, tm5, tnd, D, [BF16] * 4,
                                   extras=[(pm, "tile", GA), (pm, "tile", GB), (yap, "tile", 0), (ybp, "tile", 0)],
                                   epilogue=merge_grad)
    dwa = _matmul("mix_dw_a", "tn", [[(y_a, dyap)]], DIL_GW, D, T, DIL_GW, D, tkw, [F32])[0]
    dwb = _matmul("mix_dw_b", "tn", [[(y_b, dybp)]], FOX_W, D, T, FOX_W, D, tkw, [F32])[0]
    dy_a = _matmul("mix_dy_a", "nt", [[(dyap, wa)]], T, DIL_GW, D, tm1k, DIL_GW, D, [F32])[0]
    dy_b = _matmul("mix_dy_b", "nt", [[(dybp, wb)]], T, FOX_W, D, tm1k, FOX_W, D, [BF16])[0]

    do3 = dy_b.reshape(B, S, FOX_W)
    delta_b = _fox_delta(dy_b, o_b32.reshape(T, FOX_W), T, S).reshape(B, S, LANES)
    dq_b, ds_rows, dk_b, dv_b, ds_cols = _fox_bwd(pm3, do3, delta_b, lse_b, cumT, qcol, kcol, vcol, tq)

    delta_a, dy_a16 = _dil_delta(dy_a, y_a, T, S)
    dqs, dks, dvs = [], [], []
    for g, (_, d) in enumerate(DIL_GROUPS):
        dq_g, dk_g, dv_g = _dil_bwd(g, dil_src[g], dil_cols[g], _residue_order(L_a, B, S, d),
                                    _residue_order(dy_a16, B, S, d), _residue_order(delta_a, B, S, d))
        dqs.append(_token_order(dq_g, B, S, d))
        dks.append(_token_order(dk_g, B, S, d))
        dvs.append(_token_order(dv_g, B, S, d))

    dcum = ds_rows - ds_cols
    dcum_run = _cumsum(dcum)
    dcum_tot = dcum_run[:, S - 1:S, :]

    def forget_grad_fn(run, dcv, fl, tot):
        lane = lax.broadcasted_iota(jnp.int32, fl.shape, 1)
        df = jnp.where(lane < FOX_HEADS, (tot - run + dcv) * _sigmoid(-fl), 0.0)
        return [df, _colsum(df)]

    df16, dfb = _rowwise("forget_gate_grad", forget_grad_fn, T, tms,
                         [(dcum_run.reshape(T, LANES), "row", None), (dcum.reshape(T, LANES), "row", None), (flog, "row", None),
                          (dcum_tot, "bvec", None)],
                         [("row", LANES, BF16), ("bacc", LANES, F32)], S)

    dpm = jnp.concatenate([dga, dgb, dq_b.reshape(T, FOX_W), dk_b.reshape(T, FOX_W), dv_b.reshape(T, FOX_W)]
                          + dqs + dks + dvs + ([jnp.zeros((T, nmp - nm), BF16)] if nmp > nm else []), axis=1)
    tmn = _div(nmp, 2048, LANES)
    dwmT = _matmul("mix_dw_in", "tn", [[(dpm, h2)]], nmp, D, T, tmn, D, tkw, [F32])[0]
    dwfT = _matmul("mix_dw_f", "tn", [[(df16, h2)]], LANES, D, T, LANES, D, tkw, [F32])[0]
    dh2f = _matmul("mix_dh_f", "nn", [[(df16, wfT)]], T, D, LANES, tm5, tnd, LANES, [F32])[0]

    def add_tile(accs, ex):
        return [accs[0] + ex[0]]

    dh2 = _matmul("mix_dh", "nn", [[(dpm, wmT)]], T, D, nmp, _div(T, 256, 8), tnd, nmp, [F32],
                  extras=[(dh2f, "tile", 0)], epilogue=add_tile)[0]
    dx1, dsh2, dsc2, dgmix = _normmod_bwd("mix_normmod_bwd", x1, norm_mix, sc2, dh2, dx2, S)

    dx0, (dsh1, dsc1, dgt1, dg1), (dwg1, dwu1, dwd1) = _ffn_backward(
        "ffn1", dx1, saved1, norm_ffn1, sc1, gt1, W["ffn1_w_gate"], W["ffn1_w_up"], W["ffn1_w_down"], S)
    grad_x = dx0.reshape(B, S, D)

    dmod = jnp.concatenate([dsh1, dsc1, dgt1, dsh2, dsc2, dgt2, dsh3, dsc3, dgt3], axis=1).reshape(B, N_MOD * D)
    dmod_all = _small_allgather(dmod, "gather_dmod").reshape(N_DEV * B, N_MOD * D)
    dmod_cols = lax.dynamic_slice(dmod_all, (0, me * cl), (N_DEV * B, cl))
    g_ada_w, g_ada_b = _ada_backward(c_act, dmod_cols, dmod_all)

    fbg = jnp.sum(dfb, axis=0)
    small = jnp.concatenate([jnp.sum(dg1, axis=0), jnp.sum(dgmix, axis=0), jnp.sum(dg3, axis=0), jnp.sum(dg_final, axis=0),
                             fbg, jnp.sum(loss_b, axis=0)], axis=1)
    small = _sum_devices(_small_allgather(small, "gather_small"))
    g_small = {"norm_ffn1": small[:, 0:D], "norm_mix": small[:, D:2 * D], "norm_ffn2": small[:, 2 * D:3 * D],
               "norm_final": small[:, 3 * D:4 * D], "forget_bias": small[:, 4 * D:4 * D + FOX_HEADS], "ada_b": g_ada_b}
    loss = small[0, 4 * D + LANES]

    dwinT = jnp.concatenate([dwmT[QA:QA + 3 * DIL_W], dwmT[QB:QB + 3 * FOX_W], dwfT[:8], dwmT[GA:2 * D]], axis=0)
    gfull = {"ffn1_w_gate": dwg1, "ffn1_w_up": dwu1, "ffn1_w_down": dwd1, "w_in": dwinT, "w_branch_a": dwa, "w_branch_b": dwb,
             "w_out": dwout, "ffn2_w_gate": dwg2, "ffn2_w_up": dwu2, "ffn2_w_down": dwd2}
    g_packed = jnp.concatenate([packs[n].pack_grad(gfull[n]) for n in BIG] + [jnp.zeros((N_DEV, pad_rows, D), F32)], axis=1)
    g_local = _reduce_scatter(g_packed)
    grads = {n: packs[n].unpack_local(g_local[offs[n]:offs[n] + packs[n].rows_pad])[None] for n in BIG}
    grads["ada_w"] = g_ada_w[None]

    delta, new_m, new_v = {}, {}, {}
    for n in ["ada_w"] + BIG:
        shp = args[n].shape
        d_, m_, v_ = _adamw(f"adamw_{n}", args[n][0], grads[n][0], args["m_" + n][0], args["v_" + n][0])
        delta[n], new_m[n], new_v[n] = d_.reshape(shp), m_.reshape(shp), v_.reshape(shp)
    sizes = [args[n].size for n in SMALL]
    tot = sum(sizes)
    padded = -(-tot // (8 * LANES)) * (8 * LANES)

    def flat(get):
        v = jnp.concatenate([get(n).reshape(-1) for n in SMALL])
        return jnp.pad(v, (0, padded - tot)).reshape(8, padded // 8)

    d_s, m_s, v_s = _adamw("adamw_small", flat(lambda n: args[n]), flat(lambda n: g_small[n]), flat(lambda n: args["m_" + n]),
                           flat(lambda n: args["v_" + n]))
    o = 0
    for n, sz in zip(SMALL, sizes):
        shp = args[n].shape
        grads[n] = g_small[n].reshape(shp)
        delta[n] = d_s.reshape(-1)[o:o + sz].reshape(shp)
        new_m[n] = m_s.reshape(-1)[o:o + sz].reshape(shp)
        new_v[n] = v_s.reshape(-1)[o:o + sz].reshape(shp)
        o += sz

    order = ["ada_w", "ada_b", "norm_ffn1", "ffn1_w_gate", "ffn1_w_up", "ffn1_w_down", "norm_mix", "w_in", "forget_bias",
             "w_branch_a", "w_branch_b", "w_out", "norm_ffn2", "ffn2_w_gate", "ffn2_w_up", "ffn2_w_down", "norm_final"]
    return (loss, grad_x, *[grads[n] for n in order], *[delta[n] for n in order], *[new_m[n] for n in order],
            *[new_v[n] for n in order])
```

```python
import functools
import math

import jax
import jax.numpy as jnp
from jax import lax
from jax.experimental import pallas as pl
from jax.experimental.pallas import tpu as pltpu

F32 = jnp.float32
BF16 = jnp.bfloat16
MESH = pl.DeviceIdType.MESH
ANY = pl.BlockSpec(memory_space=pl.ANY)
VMEM_SPEC = pl.BlockSpec(memory_space=pltpu.VMEM)

N_DEV = 8
HEAD_DIM = 64
BLOCK = 128
DIL_GROUPS = ((128, 1), (512, 4), (2048, 16))
N_DIL = len(DIL_GROUPS)
DIL_HPG = 4
DIL_GW = DIL_HPG * HEAD_DIM
DIL_W = N_DIL * DIL_GW
FOX_HEADS = 8
FOX_W = FOX_HEADS * HEAD_DIM
N_MOD = 9
RMS_EPS = 1e-6
ALIBI_MAX_BIAS = 8.0
NEG_INF = -1e30
ADAM_LR, ADAM_B1, ADAM_B2, ADAM_EPS, ADAM_WD, ADAM_STEP = 0.001, 0.9, 0.999, 1e-08, 0.01, 10
V7X_VMEM_LIMIT = 52 * 1024 * 1024
LANES = 128
ROW_ALIGN = 16
PACK_ROW_QUANTUM = 256
FOX_STRIP = 32
SCALE = 1.0 / math.sqrt(HEAD_DIM)


def _div(dim, target, quantum):
    best = None
    for t in range(quantum, min(dim, target) + 1, quantum):
        if dim % t == 0:
            best = t
    return best or dim


def _params(sem=None):
    return pltpu.CompilerParams(dimension_semantics=sem, vmem_limit_bytes=V7X_VMEM_LIMIT)


def _sigmoid(x):
    return 1.0 / (1.0 + jnp.exp(-x))


def _position():
    x, y, c = lax.axis_index("x"), lax.axis_index("y"), lax.axis_index("c")
    return x, y, c


def _small_allgather(v, name):
    rows, cols = v.shape

    def body(v_ref, out_ref, send_sems, recv_sems):
        x, y, c = _position()
        me = 4 * x + 2 * y + c
        out_ref[me] = v_ref[...]

        def peer(k):
            return (1 - x if k & 4 else x, 1 - y if k & 2 else y, 1 - c if k & 1 else c)

        def copy(k, slot):
            return pltpu.make_async_remote_copy(
                src_ref=v_ref, dst_ref=out_ref.at[slot], send_sem=send_sems.at[k - 1], recv_sem=recv_sems.at[k - 1],
                device_id=peer(k), device_id_type=MESH)

        sends = [copy(k, me) for k in range(1, N_DEV)]
        for cp in sends:
            cp.start()
        for k in range(1, N_DEV):
            px, py, pc = peer(k)
            copy(k, 4 * px + 2 * py + pc).wait_recv()
        for cp in sends:
            cp.wait_send()

    return pl.pallas_call(
        body, name=name,
        out_shape=jax.ShapeDtypeStruct((N_DEV, rows, cols), v.dtype),
        in_specs=[VMEM_SPEC], out_specs=VMEM_SPEC,
        scratch_shapes=[pltpu.SemaphoreType.DMA((N_DEV - 1,)), pltpu.SemaphoreType.DMA((N_DEV - 1,))],
    )(v)


def _weight_allgather(p):
    rows, cols = p.shape

    def body(p_ref, out_ref, send_sems, recv_sems, local_sem):
        x, y, c = _position()
        me, sibling = (x, y, c), (x, y, 1 - c)
        chips = [(1 - x, y), (x, 1 - y), (1 - x, 1 - y)]

        def slot(px, py, pc):
            return out_ref.at[4 * px + 2 * py + pc]

        def copy(k, block, to, src=None):
            return pltpu.make_async_remote_copy(
                src_ref=slot(*block) if src is None else src, dst_ref=slot(*block),
                send_sem=send_sems.at[k], recv_sem=recv_sems.at[k], device_id=to, device_id_type=MESH)

        mine = pltpu.make_async_copy(p_ref, slot(*me), local_sem)
        mine.start()
        first = [copy(0, me, sibling, src=p_ref)]
        first += [copy(1 + j, me, (*chip, c), src=p_ref) for j, chip in enumerate(chips)]
        for cp in first:
            cp.start()
        passed = [copy(4 + j, (*chip, c), sibling) for j, chip in enumerate(chips)]
        for j, chip in enumerate(chips):
            copy(1 + j, (*chip, c), me).wait_recv()
            passed[j].start()
        copy(0, sibling, me).wait_recv()
        for j, chip in enumerate(chips):
            copy(4 + j, (*chip, 1 - c), me).wait_recv()
        for cp in first + passed:
            cp.wait_send()
        mine.wait()

    return pl.pallas_call(
        body, name="weight_allgather",
        out_shape=jax.ShapeDtypeStruct((N_DEV, rows, cols), p.dtype),
        in_specs=[ANY], out_specs=ANY,
        scratch_shapes=[pltpu.SemaphoreType.DMA((7,)), pltpu.SemaphoreType.DMA((7,)), pltpu.SemaphoreType.DMA],
    )(p)


def _grad_exchange_sibling(g):
    _, rows, cols = g.shape

    def body(g_ref, out_ref, send_sems, recv_sems):
        x, y, c = _position()
        sibling = (x, y, 1 - c)

        def copy(q):
            px, py = q >> 1, q & 1
            return pltpu.make_async_remote_copy(
                src_ref=g_ref.at[4 * px + 2 * py + (1 - c)], dst_ref=out_ref.at[q],
                send_sem=send_sems.at[q], recv_sem=recv_sems.at[q], device_id=sibling, device_id_type=MESH)

        copies = [copy(q) for q in range(4)]
        for cp in copies:
            cp.start()
        for cp in copies:
            cp.wait_recv()
        for cp in copies:
            cp.wait_send()

    return pl.pallas_call(
        body, name="grad_exchange_sibling",
        out_shape=jax.ShapeDtypeStruct((4, rows, cols), g.dtype),
        in_specs=[ANY], out_specs=ANY,
        scratch_shapes=[pltpu.SemaphoreType.DMA((4,)), pltpu.SemaphoreType.DMA((4,))],
    )(g)


def _grad_exchange_chips(s):
    _, rows, cols = s.shape

    def body(s_ref, out_ref, send_sems, recv_sems):
        x, y, c = _position()
        chips = [(1 - x, y), (x, 1 - y), (1 - x, 1 - y)]

        def copy(k):
            return pltpu.make_async_remote_copy(
                src_ref=s_ref.at[k], dst_ref=out_ref.at[k], send_sem=send_sems.at[k], recv_sem=recv_sems.at[k],
                device_id=(*chips[k], c), device_id_type=MESH)

        copies = [copy(k) for k in range(3)]
        for cp in copies:
            cp.start()
        for cp in copies:
            cp.wait_recv()
        for cp in copies:
            cp.wait_send()

    return pl.pallas_call(
        body, name="grad_exchange_chips",
        out_shape=jax.ShapeDtypeStruct((3, rows, cols), s.dtype),
        in_specs=[ANY], out_specs=ANY,
        scratch_shapes=[pltpu.SemaphoreType.DMA((3,)), pltpu.SemaphoreType.DMA((3,))],
    )(s)


def _chip_partial_sums(g, recv_sib, jj, qq):
    _, rows, cols = g.shape
    tr = _div(rows, 512, ROW_ALIGN)

    def body(jj_ref, qq_ref, g_ref, r_ref, o_ref):
        o_ref[...] = (g_ref[...] + r_ref[...]).astype(o_ref.dtype)

    return pl.pallas_call(
        body, name="chip_partial_sums",
        out_shape=jax.ShapeDtypeStruct((3, rows, cols), BF16),
        grid_spec=pltpu.PrefetchScalarGridSpec(
            num_scalar_prefetch=2, grid=(3, rows // tr),
            in_specs=[pl.BlockSpec((None, tr, cols), lambda k, i, jj, qq: (jj[k], i, 0)),
                      pl.BlockSpec((None, tr, cols), lambda k, i, jj, qq: (qq[k], i, 0))],
            out_specs=pl.BlockSpec((None, tr, cols), lambda k, i, jj, qq: (k, i, 0))),
        compiler_params=_params(("arbitrary", "arbitrary")),
    )(jj, qq, g, recv_sib)


def _own_partial_sum(g, recv_sib, jj, qq):
    _, rows, cols = g.shape
    tr = _div(rows, 512, ROW_ALIGN)

    def body(jj_ref, qq_ref, g_ref, r_ref, o_ref):
        o_ref[...] = g_ref[...] + r_ref[...]

    return pl.pallas_call(
        body, name="own_partial_sum",
        out_shape=jax.ShapeDtypeStruct((rows, cols), F32),
        grid_spec=pltpu.PrefetchScalarGridSpec(
            num_scalar_prefetch=2, grid=(rows // tr,),
            in_specs=[pl.BlockSpec((None, tr, cols), lambda i, jj, qq: (jj[0], i, 0)),
                      pl.BlockSpec((None, tr, cols), lambda i, jj, qq: (qq[0], i, 0))],
            out_specs=pl.BlockSpec((tr, cols), lambda i, jj, qq: (i, 0))),
        compiler_params=_params(("arbitrary",)),
    )(jj, qq, g, recv_sib)


def _final_grad_sum(own, recv):
    rows, cols = own.shape
    tr = _div(rows, 512, ROW_ALIGN)

    def body(o_ref, r_ref, out_ref):
        out_ref[...] = ((o_ref[...] + r_ref[0].astype(F32)) + r_ref[1].astype(F32)) + r_ref[2].astype(F32)

    return pl.pallas_call(
        body, name="final_grad_sum",
        out_shape=jax.ShapeDtypeStruct((rows, cols), F32),
        grid=(rows // tr,),
        in_specs=[pl.BlockSpec((tr, cols), lambda i: (i, 0)), pl.BlockSpec((3, tr, cols), lambda i: (0, i, 0))],
        out_specs=pl.BlockSpec((tr, cols), lambda i: (i, 0)),
        compiler_params=_params(("arbitrary",)),
    )(own, recv)


def _reduce_scatter(g):
    x, y, c = _position()
    chips = [(1 - x, y), (x, 1 - y), (1 - x, 1 - y)]
    jj = jnp.stack([4 * px + 2 * py + c for px, py in chips]).astype(jnp.int32)
    qq = jnp.stack([2 * px + py for px, py in chips]).astype(jnp.int32)
    jme = jnp.reshape(4 * x + 2 * y + c, (1,)).astype(jnp.int32)
    qme = jnp.reshape(2 * x + y, (1,)).astype(jnp.int32)
    recv_sib = _grad_exchange_sibling(g)
    sums = _chip_partial_sums(g, recv_sib, jj, qq)
    own = _own_partial_sum(g, recv_sib, jme, qme)
    recv = _grad_exchange_chips(sums)
    return _final_grad_sum(own, recv)


def _matmul(name, form, prods, M, N, K, tm, tn, tk, out_dtypes, extras=(), epilogue=None, rows_per_example=None):
    nk = K // tk
    n_acc = len(prods)
    flat = [ab for group in prods for ab in group]
    dims = {"nn": (((1,), (0,)), ((), ())), "nt": (((1,), (1,)), ((), ())), "tn": (((0,), (0,)), ((), ()))}[form]
    direct = nk > 1 and epilogue is None and n_acc == 1 and list(out_dtypes) == [F32]

    def spec(shape, index_map, whole):
        if whole:
            return pl.BlockSpec(shape, index_map, pipeline_mode=pl.Buffered(1))
        return pl.BlockSpec(shape, index_map)

    if form == "tn":
        a_spec = spec((tk, tm), lambda i, j, k: (k, i), nk == 1 and M == tm)
    else:
        a_spec = spec((tm, tk), lambda i, j, k: (i, k), nk == 1 and M == tm)
    if form == "nt":
        b_spec = spec((tn, tk), lambda i, j, k: (j, k), nk == 1 and N == tn)
    else:
        b_spec = spec((tk, tn), lambda i, j, k: (k, j), nk == 1 and N == tn)
    in_specs, operands = [], []
    for a, b in flat:
        in_specs += [a_spec, b_spec]
        operands += [a, b]
    for arr, kind, off in extras:
        if kind == "tile":
            assert off % tn == 0
            in_specs.append(pl.BlockSpec((tm, tn), functools.partial(lambda i, j, k, o: (i, j + o), o=off // tn)))
        else:
            tiles = rows_per_example // tm
            in_specs.append(pl.BlockSpec((None, 1, tn), functools.partial(lambda i, j, k, t: (i // t, 0, j), t=tiles)))
        operands.append(arr)
    n_in, n_out = len(operands), len(out_dtypes)

    def body(*refs):
        in_refs, out_refs, acc_refs = refs[:n_in], refs[n_in:n_in + n_out], refs[n_in + n_out:]
        k = pl.program_id(2)
        partials, p = [], 0
        for group in prods:
            tot = None
            for _ in group:
                d = lax.dot_general(in_refs[2 * p][...], in_refs[2 * p + 1][...], dims, preferred_element_type=F32)
                tot = d if tot is None else tot + d
                p += 1
            partials.append(tot)

        def finish(accs):
            ex = [r[...] for r in in_refs[2 * len(flat):]]
            outs = epilogue(accs, ex) if epilogue is not None else accs
            for r, o in zip(out_refs, outs):
                r[...] = o.astype(r.dtype)

        if nk == 1:
            finish(partials)
        elif direct:
            @pl.when(k == 0)
            def _():
                out_refs[0][...] = partials[0]

            @pl.when(k > 0)
            def _():
                out_refs[0][...] += partials[0]
        else:
            @pl.when(k == 0)
            def _():
                for r, v in zip(acc_refs, partials):
                    r[...] = v

            @pl.when(k > 0)
            def _():
                for r, v in zip(acc_refs, partials):
                    r[...] += v

            @pl.when(k == nk - 1)
            def _():
                finish([r[...] for r in acc_refs])

    outs = pl.pallas_call(
        body, name=name,
        out_shape=[jax.ShapeDtypeStruct((M, N), dt) for dt in out_dtypes],
        grid=(M // tm, N // tn, nk),
        in_specs=in_specs,
        out_specs=[pl.BlockSpec((tm, tn), lambda i, j, k: (i, j)) for _ in out_dtypes],
        scratch_shapes=[pltpu.VMEM((tm, tn), F32) for _ in range(n_acc)] if nk > 1 and not direct else [],
        compiler_params=_params(("parallel", "parallel", "arbitrary")),
    )(*operands)
    return outs


def _rowwise(name, fn, T, tm, ins, outs, rows_per_example):
    tiles = rows_per_example // tm
    n_ex = T // rows_per_example
    in_specs, operands = [], []
    for arr, kind, arg in ins:
        if kind == "row":
            if arg is None:
                in_specs.append(pl.BlockSpec((tm, arr.shape[1]), lambda i: (i, 0)))
            else:
                in_specs.append(pl.BlockSpec((tm, arg[0]), functools.partial(lambda i, cb: (i, cb), cb=arg[1])))
        elif kind == "bvec":
            in_specs.append(pl.BlockSpec((None, 1, arr.shape[2]), lambda i: (i // tiles, 0, 0)))
        else:
            in_specs.append(pl.BlockSpec((1, arr.shape[1]), lambda i: (0, 0)))
        operands.append(arr)
    out_shape, out_specs = [], []
    for kind, cols, dt in outs:
        if kind == "row":
            out_shape.append(jax.ShapeDtypeStruct((T, cols), dt))
            out_specs.append(pl.BlockSpec((tm, cols), lambda i: (i, 0)))
        else:
            out_shape.append(jax.ShapeDtypeStruct((n_ex, 1, cols), F32))
            out_specs.append(pl.BlockSpec((None, 1, cols), lambda i: (i // tiles, 0, 0)))
    n_in = len(operands)

    def body(*refs):
        i = pl.program_id(0)
        vals = fn(*[r[...] for r in refs[:n_in]])
        for (kind, _, _), r, v in zip(outs, refs[n_in:], vals):
            if kind == "row":
                r[...] = v.astype(r.dtype)
            else:
                @pl.when(i % tiles == 0)
                def _():
                    r[...] = jnp.zeros_like(r)

                r[...] += v

    return pl.pallas_call(
        body, name=name, out_shape=out_shape, grid=(T // tm,), in_specs=in_specs, out_specs=out_specs,
        compiler_params=_params(("arbitrary",)),
    )(*operands)


def _colsum(v):
    return jnp.sum(v, axis=0, keepdims=True)


def _rms_parts(x):
    rstd = lax.rsqrt(jnp.mean(x * x, axis=-1, keepdims=True) + RMS_EPS)
    return x * rstd, rstd


def _normmod(name, x, g, sc, sh, S):
    T, D = x.shape

    def fn(xv, gv, scv, shv):
        xhat, _ = _rms_parts(xv)
        return [(xhat * gv) * (1.0 + scv) + shv]

    return _rowwise(name, fn, T, _div(S, 512, 8), [(x, "row", None), (g, "vec", None), (sc, "bvec", None), (sh, "bvec", None)],
                    [("row", D, BF16)], S)[0]


def _normmod_bwd(name, x, g, sc, dh, dres, S):
    T, D = x.shape

    def fn(xv, gv, scv, dhv, drv):
        xhat, rstd = _rms_parts(xv)
        n = xhat * gv
        dn = dhv * (1.0 + scv)
        dxh = dn * gv
        dx = rstd * (dxh - xhat * jnp.mean(dxh * xhat, axis=-1, keepdims=True))
        return [drv + dx, _colsum(dhv), _colsum(dhv * n), _colsum(dn * xhat)]

    return _rowwise(name, fn, T, _div(S, 256, 8),
                    [(x, "row", None), (g, "vec", None), (sc, "bvec", None), (dh, "row", None), (dres, "row", None)],
                    [("row", D, F32), ("bacc", D, F32), ("bacc", D, F32), ("bacc", D, F32)], S)


def _gate_grad(name, dx, y, gt, coeff, S):
    T, D = dx.shape

    def fn(dxv, yv, gtv):
        return [coeff * gtv * dxv, _colsum(coeff * dxv * yv.astype(F32))]

    return _rowwise(name, fn, T, _div(S, 512, 8), [(dx, "row", None), (y, "row", None), (gt, "bvec", None)],
                    [("row", D, BF16), ("bacc", D, F32)], S)


def _ffn_forward(tag, x, g, sh, sc, gt, wgT, wuT, wd, S):
    T, D = x.shape
    F = wd.shape[0]
    h = _normmod(f"{tag}_normmod", x, g, sc, sh, S)

    def gateup(accs, ex):
        a, u = accs
        return [a, u, a * _sigmoid(a) * u]

    a, u, s = _matmul(f"{tag}_gateup", "nt", [[(h, wgT)], [(h, wuT)]], T, F, D, _div(T, 256, 8), F, D,
                      [BF16, BF16, BF16], epilogue=gateup)

    def down(accs, ex):
        xv, gtv = ex
        return [xv + 0.5 * gtv * accs[0], accs[0]]

    tmd = _div(S, 512, 8)
    x_new, y = _matmul(f"{tag}_down", "nn", [[(s, wd)]], T, D, F, tmd, D, F, [F32, BF16],
                       extras=[(x, "tile", 0), (gt, "brow", 0)], epilogue=down, rows_per_example=S)
    return x_new, (x, h, a, u, s, y)


def _ffn_backward(tag, dx_out, saved, g, sc, gt, wgT, wuT, wd, S):
    x, h, a, u, s, y = saved
    T, D = x.shape
    F = wd.shape[0]
    dy, dgt = _gate_grad(f"{tag}_gate_grad", dx_out, y, gt, 0.5, S)

    def act_grad(accs, ex):
        ds = accs[0]
        av, uv = ex[0].astype(F32), ex[1].astype(F32)
        sg = _sigmoid(av)
        return [ds * uv * (sg * (1.0 + av * (1.0 - sg))), ds * (av * sg)]

    da, du = _matmul(f"{tag}_act_grad", "nt", [[(dy, wd)]], T, F, D, _div(T, 256, 8), F, D, [BF16, BF16],
                     extras=[(a, "tile", 0), (u, "tile", 0)], epilogue=act_grad)
    tkw = _div(T, 1024, LANES)
    dwd = _matmul(f"{tag}_dw_down", "tn", [[(dy, s)]], D, F, T, D, F, tkw, [F32])[0]
    dwgT = _matmul(f"{tag}_dw_gate", "tn", [[(h, da)]], D, F, T, D, F, tkw, [F32])[0]
    dwuT = _matmul(f"{tag}_dw_up", "tn", [[(h, du)]], D, F, T, D, F, tkw, [F32])[0]
    dh = _matmul(f"{tag}_dh", "nn", [[(da, wgT), (du, wuT)]], T, D, F, _div(T, 512, 8), D, F, [F32])[0]
    dx_in, dsh, dsc, dg = _normmod_bwd(f"{tag}_normmod_bwd", x, g, sc, dh, dx_out, S)
    return dx_in, (dsh, dsc, dgt, dg), (dwgT, dwuT, dwd)


def _loss_head(x, tgt, g, S):
    T, D = x.shape

    def fn(xv, tv, gv):
        xhat, rstd = _rms_parts(xv)
        e = xhat * gv - tv
        loss = jnp.broadcast_to(0.5 / D * jnp.sum(_colsum(e * e), axis=1, keepdims=True), (1, LANES))
        dy = e * (1.0 / D)
        dxh = dy * gv
        dx = rstd * (dxh - xhat * jnp.mean(dxh * xhat, axis=-1, keepdims=True))
        return [dx, loss, _colsum(dy * xhat)]

    return _rowwise("loss_head", fn, T, _div(S, 512, 8), [(x, "row", None), (tgt, "row", None), (g, "vec", None)],
                    [("row", D, F32), ("bacc", LANES, F32), ("bacc", D, F32)], S)


def _cumsum(v):
    B, S, _ = v.shape
    rows = _div(S, 1024, BLOCK)

    def body(x_ref, o_ref, carry):
        i = pl.program_id(1)

        @pl.when(i == 0)
        def _():
            carry[...] = jnp.zeros_like(carry)

        r = lax.broadcasted_iota(jnp.int32, (BLOCK, BLOCK), 0)
        c = lax.broadcasted_iota(jnp.int32, (BLOCK, BLOCK), 1)
        tri = (c <= r).astype(F32)
        last = carry[0:1, :]
        for j in range(0, rows, BLOCK):
            cum = jnp.dot(tri, x_ref[j:j + BLOCK, :], precision=lax.Precision.HIGHEST, preferred_element_type=F32) + last
            o_ref[j:j + BLOCK, :] = cum
            last = cum[BLOCK - 1:BLOCK, :]
        carry[...] = jnp.broadcast_to(last, carry.shape)

    return pl.pallas_call(
        body, name="cumsum", out_shape=jax.ShapeDtypeStruct(v.shape, F32), grid=(B, S // rows),
        in_specs=[pl.BlockSpec((None, rows, LANES), lambda b, i: (b, i, 0))],
        out_specs=pl.BlockSpec((None, rows, LANES), lambda b, i: (b, i, 0)),
        scratch_shapes=[pltpu.VMEM((8, LANES), F32)],
        compiler_params=_params(("arbitrary", "arbitrary")),
    )(v)


def _fox_scores(q, k, cq, ck, qpos, kpos):
    s = lax.dot_general(q, k, (((1,), (1,)), ((), ())), preferred_element_type=F32) * SCALE + cq - ck
    return jnp.where(kpos <= qpos, s, NEG_INF)


def _fox_positions(qi, kj, tq, tk):
    qpos = qi * tq + lax.broadcasted_iota(jnp.int32, (tq, tk), 0)
    kpos = kj * tk + lax.broadcasted_iota(jnp.int32, (tq, tk), 1)
    return qpos, kpos


def _fox_forward(pm3, cum, cumT, qcol, kcol, vcol, tq):
    B, S, _ = pm3.shape
    nq = S // tq

    def body(q_ref, k_ref, v_ref, cq_ref, ck_ref, o_ref, o32_ref, lse_ref, m_sc, l_sc, acc_sc):
        qi, kj = pl.program_id(1), pl.program_id(2)

        @pl.when(kj == 0)
        def _():
            m_sc[...] = jnp.full_like(m_sc, NEG_INF)
            l_sc[...] = jnp.zeros_like(l_sc)
            acc_sc[...] = jnp.zeros_like(acc_sc)

        @pl.when(kj <= qi)
        def _():
            qpos, kpos = _fox_positions(qi, kj, tq, tq)
            for h in range(FOX_HEADS):
                hs = slice(HEAD_DIM * h, HEAD_DIM * (h + 1))
                s = _fox_scores(q_ref[:, hs], k_ref[:, hs], cq_ref[:, h:h + 1], ck_ref[h:h + 1, :], qpos, kpos)
                m_prev = m_sc[h]
                m_new = jnp.maximum(m_prev, jnp.max(s, axis=-1, keepdims=True))
                alpha = jnp.exp(m_prev - m_new)
                p = jnp.exp(s - m_new)
                l_sc[h] = alpha * l_sc[h] + jnp.sum(p, axis=-1, keepdims=True)
                acc_sc[:, hs] = alpha * acc_sc[:, hs] + lax.dot_general(
                    p.astype(BF16), v_ref[:, hs], (((1,), (0,)), ((), ())), preferred_element_type=F32)
                m_sc[h] = m_new

        @pl.when(kj == nq - 1)
        def _():
            lse_ref[...] = jnp.zeros_like(lse_ref)
            for h in range(FOX_HEADS):
                hs = slice(HEAD_DIM * h, HEAD_DIM * (h + 1))
                oh = acc_sc[:, hs] / l_sc[h]
                o_ref[:, hs] = oh.astype(o_ref.dtype)
                o32_ref[:, hs] = oh
                lse_ref[:, h:h + 1] = m_sc[h] + jnp.log(l_sc[h])

    return pl.pallas_call(
        body, name="fox_forward",
        out_shape=[jax.ShapeDtypeStruct((B, S, FOX_W), BF16), jax.ShapeDtypeStruct((B, S, FOX_W), F32),
                   jax.ShapeDtypeStruct((B, S, LANES), F32)],
        grid=(B, nq, nq),
        in_specs=[pl.BlockSpec((None, tq, FOX_W), lambda b, i, j: (b, i, qcol)),
                  pl.BlockSpec((None, tq, FOX_W), lambda b, i, j: (b, jnp.minimum(i, j), kcol)),
                  pl.BlockSpec((None, tq, FOX_W), lambda b, i, j: (b, jnp.minimum(i, j), vcol)),
                  pl.BlockSpec((None, tq, LANES), lambda b, i, j: (b, i, 0)),
                  pl.BlockSpec((None, 8, tq), lambda b, i, j: (b, 0, jnp.minimum(i, j)))],
        out_specs=[pl.BlockSpec((None, tq, FOX_W), lambda b, i, j: (b, i, 0)),
                   pl.BlockSpec((None, tq, FOX_W), lambda b, i, j: (b, i, 0)),
                   pl.BlockSpec((None, tq, LANES), lambda b, i, j: (b, i, 0))],
        scratch_shapes=[pltpu.VMEM((FOX_HEADS, tq, 1), F32), pltpu.VMEM((FOX_HEADS, tq, 1), F32), pltpu.VMEM((tq, FOX_W), F32)],
        compiler_params=_params(("parallel", "parallel", "arbitrary")),
    )(pm3, pm3, pm3, cum, cumT)


def _fox_dq(pm3, do, delta, lse, cum, cumT, qcol, kcol, vcol, tq):
    B, S, _ = pm3.shape
    nq = S // tq

    def body(q_ref, k_ref, v_ref, do_ref, dl_ref, lse_ref, cq_ref, ck_ref, dq_ref, dc_ref, acc_sc, dc_sc):
        qi, kj = pl.program_id(1), pl.program_id(2)

        @pl.when(kj == 0)
        def _():
            acc_sc[...] = jnp.zeros_like(acc_sc)
            dc_sc[...] = jnp.zeros_like(dc_sc)

        @pl.when(kj <= qi)
        def _():
            qpos, kpos = _fox_positions(qi, kj, tq, tq)
            for h in range(FOX_HEADS):
                hs = slice(HEAD_DIM * h, HEAD_DIM * (h + 1))
                s = _fox_scores(q_ref[:, hs], k_ref[:, hs], cq_ref[:, h:h + 1], ck_ref[h:h + 1, :], qpos, kpos)
                p = jnp.exp(s - lse_ref[:, h:h + 1])
                doh = do_ref[:, hs]
                dp = lax.dot_general(doh, v_ref[:, hs], (((1,), (1,)), ((), ())), preferred_element_type=F32)
                ds = p * (dp - dl_ref[:, h:h + 1])
                dc_sc[h] += jnp.sum(ds, axis=-1, keepdims=True)
                acc_sc[:, hs] += lax.dot_general(ds.astype(BF16), k_ref[:, hs], (((1,), (0,)), ((), ())),
                                                 preferred_element_type=F32)

        @pl.when(kj == nq - 1)
        def _():
            dq_ref[...] = (acc_sc[...] * SCALE).astype(dq_ref.dtype)
            dc_ref[...] = jnp.zeros_like(dc_ref)
            for h in range(FOX_HEADS):
                dc_ref[:, h:h + 1] = dc_sc[h]

    qspec = pl.BlockSpec((None, tq, FOX_W), lambda b, i, j: (b, i, 0))
    lspec = pl.BlockSpec((None, tq, LANES), lambda b, i, j: (b, i, 0))
    return pl.pallas_call(
        body, name="fox_dq",
        out_shape=[jax.ShapeDtypeStruct((B, S, FOX_W), BF16), jax.ShapeDtypeStruct((B, S, LANES), F32)],
        grid=(B, nq, nq),
        in_specs=[pl.BlockSpec((None, tq, FOX_W), lambda b, i, j: (b, i, qcol)),
                  pl.BlockSpec((None, tq, FOX_W), lambda b, i, j: (b, jnp.minimum(i, j), kcol)),
                  pl.BlockSpec((None, tq, FOX_W), lambda b, i, j: (b, jnp.minimum(i, j), vcol)),
                  qspec, lspec, lspec, lspec,
                  pl.BlockSpec((None, 8, tq), lambda b, i, j: (b, 0, jnp.minimum(i, j)))],
        out_specs=[qspec, lspec],
        scratch_shapes=[pltpu.VMEM((tq, FOX_W), F32), pltpu.VMEM((FOX_HEADS, tq, 1), F32)],
        compiler_params=_params(("parallel", "parallel", "arbitrary")),
    )(pm3, pm3, pm3, do, delta, lse, cum, cumT)


def _fox_dkv(pm3, do, delta, lse, cum, cumT, qcol, kcol, vcol, tq):
    B, S, _ = pm3.shape
    nq = S // tq

    def body(q_ref, k_ref, v_ref, do_ref, dl_ref, lse_ref, cq_ref, ck_ref, dk_ref, dv_ref, dc_ref, dk_sc, dv_sc, dc_sc):
        kj, qi = pl.program_id(1), pl.program_id(2)

        @pl.when(qi == 0)
        def _():
            dk_sc[...] = jnp.zeros_like(dk_sc)
            dv_sc[...] = jnp.zeros_like(dv_sc)
            dc_sc[...] = jnp.zeros_like(dc_sc)

        @pl.when(qi >= kj)
        def _():
            qpos, kpos = _fox_positions(qi, kj, tq, tq)
            for h in range(FOX_HEADS):
                hs = slice(HEAD_DIM * h, HEAD_DIM * (h + 1))
                qh = q_ref[:, hs]
                s = _fox_scores(qh, k_ref[:, hs], cq_ref[:, h:h + 1], ck_ref[h:h + 1, :], qpos, kpos)
                p = jnp.exp(s - lse_ref[:, h:h + 1])
                doh = do_ref[:, hs]
                dp = lax.dot_general(doh, v_ref[:, hs], (((1,), (1,)), ((), ())), preferred_element_type=F32)
                ds = p * (dp - dl_ref[:, h:h + 1])
                dv_sc[:, hs] += lax.dot_general(p.astype(BF16), doh, (((0,), (0,)), ((), ())), preferred_element_type=F32)
                dk_sc[:, hs] += lax.dot_general(ds.astype(BF16), qh, (((0,), (0,)), ((), ())), preferred_element_type=F32)
                dc_sc[h:h + 1, :] -= jnp.sum(ds, axis=0, keepdims=True)

        @pl.when(qi == nq - 1)
        def _():
            dk_ref[...] = (dk_sc[...] * SCALE).astype(dk_ref.dtype)
            dv_ref[...] = dv_sc[...].astype(dv_ref.dtype)
            dc_ref[...] = dc_sc[...]

    def qside(width):
        return pl.BlockSpec((None, tq, width), lambda b, j, i: (b, jnp.maximum(i, j), 0))

    kspec = pl.BlockSpec((None, tq, FOX_W), lambda b, j, i: (b, j, 0))
    return pl.pallas_call(
        body, name="fox_dkv",
        out_shape=[jax.ShapeDtypeStruct((B, S, FOX_W), BF16), jax.ShapeDtypeStruct((B, S, FOX_W), BF16),
                   jax.ShapeDtypeStruct((B, 8, S), F32)],
        grid=(B, nq, nq),
        in_specs=[pl.BlockSpec((None, tq, FOX_W), lambda b, j, i: (b, jnp.maximum(i, j), qcol)),
                  pl.BlockSpec((None, tq, FOX_W), lambda b, j, i: (b, j, kcol)),
                  pl.BlockSpec((None, tq, FOX_W), lambda b, j, i: (b, j, vcol)),
                  qside(FOX_W), qside(LANES), qside(LANES), qside(LANES),
                  pl.BlockSpec((None, 8, tq), lambda b, j, i: (b, 0, j))],
        out_specs=[kspec, kspec, pl.BlockSpec((None, 8, tq), lambda b, j, i: (b, 0, j))],
        scratch_shapes=[pltpu.VMEM((tq, FOX_W), F32), pltpu.VMEM((tq, FOX_W), F32), pltpu.VMEM((8, tq), F32)],
        compiler_params=_params(("parallel", "parallel", "arbitrary")),
    )(pm3, pm3, pm3, do, delta, lse, cum, cumT)


def _with_ones(x):
    lane = lax.broadcasted_iota(jnp.int32, (x.shape[0], HEAD_DIM), 1)
    return jnp.concatenate([x, jnp.where(lane == 0, 1.0, 0.0).astype(x.dtype)], axis=1)


def _causal_strip(s, r):
    qpos = r + lax.broadcasted_iota(jnp.int32, s.shape, 0)
    kpos = lax.broadcasted_iota(jnp.int32, s.shape, 1)
    return jnp.where(kpos <= qpos, s, NEG_INF)


NT = (((1,), (1,)), ((), ()))
NN = (((1,), (0,)), ((), ()))
TN = (((0,), (0,)), ((), ()))


def _fox_fwd(pm3, cumT, qcol, kcol, vcol, tq):
    B, S, _ = pm3.shape
    nq = S // tq
    strips = range(0, tq, FOX_STRIP)

    def body(q_ref, k_ref, v_ref, ck_ref, o_ref, o32_ref, lse_ref, s_sc, p_sc, al_sc, m_sc, acc_sc):
        qi, kj = pl.program_id(1), pl.program_id(2)

        @pl.when(kj == 0)
        def _():
            m_sc[...] = jnp.full_like(m_sc, NEG_INF)
            acc_sc[...] = jnp.zeros_like(acc_sc)

        def tile(diagonal):
            def scores(h):
                hs = slice(HEAD_DIM * h, HEAD_DIM * (h + 1))
                s_sc[h % 2] = lax.dot_general(q_ref[:, hs] * SCALE, k_ref[:, hs], NT, preferred_element_type=F32)

            def accumulate(h):
                hs = slice(HEAD_DIM * h, HEAD_DIM * (h + 1))
                acc_sc[h] = al_sc[h % 2] * acc_sc[h] + lax.dot_general(p_sc[h % 2], _with_ones(v_ref[:, hs]), NN,
                                                                       preferred_element_type=F32)

            scores(0)
            for h in range(FOX_HEADS):
                b = h % 2
                if h + 1 < FOX_HEADS:
                    scores(h + 1)
                if h >= 1:
                    accumulate(h - 1)
                ck = ck_ref[h:h + 1, :]
                for r in strips:
                    rows = slice(r, r + FOX_STRIP)
                    s = s_sc[b, rows, :] - ck
                    if diagonal:
                        s = _causal_strip(s, r)
                    m_prev = m_sc[h, rows, :]
                    m_new = jnp.maximum(m_prev, jnp.max(s, axis=-1, keepdims=True))
                    p_sc[b, rows, :] = jnp.exp(s - m_new).astype(BF16)
                    al_sc[b, rows, :] = jnp.exp(m_prev - m_new)
                    m_sc[h, rows, :] = m_new
            accumulate(FOX_HEADS - 1)

        @pl.when(kj < qi)
        def _():
            tile(False)

        @pl.when(kj == qi)
        def _():
            tile(True)

        @pl.when(kj == nq - 1)
        def _():
            lse_ref[...] = jnp.zeros_like(lse_ref)
            for h in range(FOX_HEADS):
                hs = slice(HEAD_DIM * h, HEAD_DIM * (h + 1))
                acc = acc_sc[h]
                l = acc[:, HEAD_DIM:HEAD_DIM + 1]
                oh = acc[:, :HEAD_DIM] / l
                o_ref[:, hs] = oh.astype(o_ref.dtype)
                o32_ref[:, hs] = oh
                lse_ref[:, h:h + 1] = m_sc[h] + jnp.log(l)

    ospec = pl.BlockSpec((None, tq, FOX_W), lambda b, i, j: (b, i, 0))
    return pl.pallas_call(
        body, name="fox_forward",
        out_shape=[jax.ShapeDtypeStruct((B, S, FOX_W), BF16), jax.ShapeDtypeStruct((B, S, FOX_W), F32),
                   jax.ShapeDtypeStruct((B, S, LANES), F32)],
        grid=(B, nq, nq),
        in_specs=[pl.BlockSpec((None, tq, FOX_W), lambda b, i, j: (b, i, qcol)),
                  pl.BlockSpec((None, tq, FOX_W), lambda b, i, j: (b, jnp.minimum(i, j), kcol)),
                  pl.BlockSpec((None, tq, FOX_W), lambda b, i, j: (b, jnp.minimum(i, j), vcol)),
                  pl.BlockSpec((None, 8, tq), lambda b, i, j: (b, 0, jnp.minimum(i, j)))],
        out_specs=[ospec, ospec, pl.BlockSpec((None, tq, LANES), lambda b, i, j: (b, i, 0))],
        scratch_shapes=[pltpu.VMEM((2, tq, tq), F32), pltpu.VMEM((2, tq, tq), BF16), pltpu.VMEM((2, tq, 1), F32),
                        pltpu.VMEM((FOX_HEADS, tq, 1), F32), pltpu.VMEM((FOX_HEADS, tq, LANES), F32)],
        compiler_params=_params(("parallel", "parallel", "arbitrary")),
    )(pm3, pm3, pm3, cumT)


def _fox_bwd(pm3, do, delta, lse, cumT, qcol, kcol, vcol, tq):
    B, S, _ = pm3.shape
    nq = S // tq
    strips = range(0, tq, FOX_STRIP)

    def body(q_ref, k_ref, v_ref, do_ref, dl_ref, lse_ref, ck_ref, dq_ref, rs_ref, dk_ref, dv_ref, cs_ref,
             s_sc, dp_sc, p_sc, ds_sc, dq_sc, dk_sc, dv_sc):
        kj, qi = pl.program_id(1), pl.program_id(2)

        @pl.when((kj == 0) & (qi == 0))
        def _():
            dq_sc[...] = jnp.zeros_like(dq_sc)

        @pl.when(qi == 0)
        def _():
            dk_sc[...] = jnp.zeros_like(dk_sc)
            dv_sc[...] = jnp.zeros_like(dv_sc)

        def tile(diagonal):
            qrows = pl.ds(pl.multiple_of(qi * tq, tq), tq)
            def scores(h):
                hs = slice(HEAD_DIM * h, HEAD_DIM * (h + 1))
                s_sc[h % 2] = lax.dot_general(q_ref[:, hs] * SCALE, k_ref[:, hs], NT, preferred_element_type=F32)
                dp_sc[h % 2] = lax.dot_general(do_ref[:, hs], v_ref[:, hs], NT, preferred_element_type=F32)

            def accumulate(h):
                hs = slice(HEAD_DIM * h, HEAD_DIM * (h + 1))
                b = h % 2
                dv_sc[:, hs] += lax.dot_general(p_sc[b], do_ref[:, hs], TN, preferred_element_type=F32)
                dk_sc[h] += lax.dot_general(ds_sc[b], _with_ones(q_ref[:, hs] * SCALE), TN, preferred_element_type=F32)
                dq_sc[h, qrows, :] += lax.dot_general(ds_sc[b], _with_ones(k_ref[:, hs]), NN, preferred_element_type=F32)

            for h in range(FOX_HEADS):
                b = h % 2
                scores(h)
                ck = ck_ref[h:h + 1, :]
                for r in strips:
                    rows = slice(r, r + FOX_STRIP)
                    s = s_sc[b, rows, :] - ck
                    if diagonal:
                        s = _causal_strip(s, r)
                    p = jnp.exp(s - lse_ref[rows, h:h + 1])
                    p_sc[b, rows, :] = p.astype(BF16)
                    ds_sc[b, rows, :] = (p * (dp_sc[b, rows, :] - dl_ref[rows, h:h + 1])).astype(BF16)
                accumulate(h)

        @pl.when(qi > kj)
        def _():
            tile(False)

        @pl.when(qi == kj)
        def _():
            tile(True)

        @pl.when(qi == nq - 1)
        def _():
            dv_ref[...] = dv_sc[...].astype(dv_ref.dtype)
            cs_ref[...] = jnp.zeros_like(cs_ref)
            for h in range(FOX_HEADS):
                hs = slice(HEAD_DIM * h, HEAD_DIM * (h + 1))
                dk = dk_sc[h]
                dk_ref[:, hs] = dk[:, :HEAD_DIM].astype(dk_ref.dtype)
                cs_ref[:, h:h + 1] = dk[:, HEAD_DIM:HEAD_DIM + 1]

        @pl.when((kj == nq - 1) & (qi == nq - 1))
        def _():
            rs_ref[...] = jnp.zeros_like(rs_ref)
            for h in range(FOX_HEADS):
                hs = slice(HEAD_DIM * h, HEAD_DIM * (h + 1))
                dq_ref[:, hs] = (dq_sc[h, :, :HEAD_DIM] * SCALE).astype(dq_ref.dtype)
                rs_ref[:, h:h + 1] = dq_sc[h, :, HEAD_DIM:HEAD_DIM + 1]

    def qside(width, col=0):
        return pl.BlockSpec((None, tq, width), lambda b, j, i: (b, jnp.maximum(i, j), col))

    kspec = pl.BlockSpec((None, tq, FOX_W), lambda b, j, i: (b, j, 0))
    return pl.pallas_call(
        body, name="fox_backward",
        out_shape=[jax.ShapeDtypeStruct((B, S, FOX_W), BF16), jax.ShapeDtypeStruct((B, S, LANES), F32),
                   jax.ShapeDtypeStruct((B, S, FOX_W), BF16), jax.ShapeDtypeStruct((B, S, FOX_W), BF16),
                   jax.ShapeDtypeStruct((B, S, LANES), F32)],
        grid=(B, nq, nq),
        in_specs=[qside(FOX_W, qcol),
                  pl.BlockSpec((None, tq, FOX_W), lambda b, j, i: (b, j, kcol)),
                  pl.BlockSpec((None, tq, FOX_W), lambda b, j, i: (b, j, vcol)),
                  qside(FOX_W), qside(LANES), qside(LANES),
                  pl.BlockSpec((None, 8, tq), lambda b, j, i: (b, 0, j))],
        out_specs=[pl.BlockSpec((None, S, FOX_W), lambda b, j, i: (b, 0, 0)),
                   pl.BlockSpec((None, S, LANES), lambda b, j, i: (b, 0, 0)),
                   kspec, kspec, pl.BlockSpec((None, tq, LANES), lambda b, j, i: (b, j, 0))],
        scratch_shapes=[pltpu.VMEM((2, tq, tq), F32), pltpu.VMEM((2, tq, tq), F32), pltpu.VMEM((2, tq, tq), BF16),
                        pltpu.VMEM((2, tq, tq), BF16), pltpu.VMEM((FOX_HEADS, S, LANES), F32),
                        pltpu.VMEM((FOX_HEADS, tq, LANES), F32), pltpu.VMEM((tq, FOX_W), F32)],
        compiler_params=_params(("parallel", "arbitrary", "arbitrary")),
    )(pm3, pm3, pm3, do, delta, lse, cumT)


def _fox_delta(do, o32, T, S):
    def fn(dov, ov):
        prod = dov.astype(F32) * ov
        lane = lax.broadcasted_iota(jnp.int32, (dov.shape[0], LANES), 1)
        delta = jnp.zeros((dov.shape[0], LANES), F32)
        for h in range(FOX_HEADS):
            hs = slice(HEAD_DIM * h, HEAD_DIM * (h + 1))
            delta = jnp.where(lane == h, jnp.sum(prod[:, hs], axis=-1, keepdims=True), delta)
        return [delta]

    return _rowwise("fox_delta", fn, T, _div(S, 512, 8), [(do, "row", None), (o32, "row", None)], [("row", LANES, F32)], S)[0]


def _alibi_slope(group, head):
    return 2.0 ** (-ALIBI_MAX_BIAS * (group * DIL_HPG + head + 1) / (N_DIL * DIL_HPG))


def _dil_tiles(q, k_cur, k_prev, slope, dilation, has_prev):
    qi = lax.broadcasted_iota(jnp.int32, (BLOCK, BLOCK), 0)
    ki = lax.broadcasted_iota(jnp.int32, (BLOCK, BLOCK), 1)
    rel = (qi - ki).astype(F32)
    nt = (((1,), (1,)), ((), ()))
    s_cur = lax.dot_general(q, k_cur, nt, preferred_element_type=F32) * SCALE - (slope * dilation) * rel
    s_cur = jnp.where(ki <= qi, s_cur, NEG_INF)
    s_prev = lax.dot_general(q, k_prev, nt, preferred_element_type=F32) * SCALE - (slope * dilation) * (rel + BLOCK)
    s_prev = jnp.where((ki >= qi) & has_prev, s_prev, NEG_INF)
    return s_cur, s_prev


def _dil_forward(group, pmv, nmb, qa_blk, B, S):
    _, dilation = DIL_GROUPS[group]
    sub = S // dilation
    nb = sub // BLOCK
    qb, kb, vb = qa_blk + group, qa_blk + N_DIL + group, qa_blk + 2 * N_DIL + group

    def body(q_ref, kc_ref, kp_ref, vc_ref, vp_ref, o_ref, lse_ref):
        has_prev = pl.program_id(2) > 0
        lse_ref[...] = jnp.zeros_like(lse_ref)
        for h in range(DIL_HPG):
            hs = slice(HEAD_DIM * h, HEAD_DIM * (h + 1))
            s_cur, s_prev = _dil_tiles(q_ref[:, hs], kc_ref[:, hs], kp_ref[:, hs], _alibi_slope(group, h), dilation, has_prev)
            m = jnp.maximum(jnp.max(s_cur, axis=-1, keepdims=True), jnp.max(s_prev, axis=-1, keepdims=True))
            p_cur, p_prev = jnp.exp(s_cur - m), jnp.exp(s_prev - m)
            l = jnp.sum(p_cur, axis=-1, keepdims=True) + jnp.sum(p_prev, axis=-1, keepdims=True)
            nn = (((1,), (0,)), ((), ()))
            o = (lax.dot_general(p_cur.astype(BF16), vc_ref[:, hs], nn, preferred_element_type=F32)
                 + lax.dot_general(p_prev.astype(BF16), vp_ref[:, hs], nn, preferred_element_type=F32))
            o_ref[:, hs] = o / l
            lse_ref[:, h:h + 1] = m + jnp.log(l)

    def cur(col):
        return pl.BlockSpec((None, BLOCK, DIL_GW), lambda b, r, n: (b, n, r * nmb + col))

    def prev(col):
        return pl.BlockSpec((None, BLOCK, DIL_GW), lambda b, r, n: (b, jnp.maximum(n - 1, 0), r * nmb + col))

    return pl.pallas_call(
        body, name=f"dil_forward_{group}",
        out_shape=[jax.ShapeDtypeStruct((B, sub, dilation * DIL_GW), F32), jax.ShapeDtypeStruct((B, sub, dilation * LANES), F32)],
        grid=(B, dilation, nb),
        in_specs=[cur(qb), cur(kb), prev(kb), cur(vb), prev(vb)],
        out_specs=[pl.BlockSpec((None, BLOCK, DIL_GW), lambda b, r, n: (b, n, r)),
                   pl.BlockSpec((None, BLOCK, LANES), lambda b, r, n: (b, n, r))],
        compiler_params=_params(("parallel", "parallel", "arbitrary")),
    )(pmv, pmv, pmv, pmv, pmv)


def _residue_order(a, B, S, d):
    C = a.shape[-1]
    if d == 1:
        return a.reshape(B, S, C)
    return a.reshape(B, S // d, d, C).transpose(0, 2, 1, 3).reshape(B * d, S // d, C)


def _token_order(a, B, S, d):
    C = a.shape[-1]
    if d == 1:
        return a.reshape(B * S, C)
    return a.reshape(B, d, S // d, C).transpose(0, 2, 1, 3).reshape(B * S, C)


def _band_scores(qh, kcat, slope_d, has_prev):
    qi = lax.broadcasted_iota(jnp.int32, (BLOCK, 2 * BLOCK), 0)
    c = lax.broadcasted_iota(jnp.int32, (BLOCK, 2 * BLOCK), 1)
    s = lax.dot_general(qh, kcat, NT, preferred_element_type=F32) - slope_d * (BLOCK + qi - c).astype(F32)
    valid = (c >= qi) & (c <= qi + BLOCK)
    if has_prev is not None:
        valid = valid & ((c >= BLOCK) | has_prev)
    return jnp.where(valid, s, NEG_INF)


def _band_operands(j, cur_ref, prev_ref, hs):
    if j == 0:
        return jnp.concatenate([prev_ref[:, hs], cur_ref[0:BLOCK, hs]], axis=0)
    return cur_ref[(j - 1) * BLOCK:(j + 1) * BLOCK, hs]


def _dil_specs(Ls, qb, cols):
    nsub = qb // BLOCK
    qcol, kcol, vcol = cols

    def cur(col):
        return pl.BlockSpec((None, qb, DIL_GW), lambda s, n: (s, n, col))

    def prev(col):
        return pl.BlockSpec((None, BLOCK, DIL_GW), lambda s, n: (s, jnp.maximum(n * nsub - 1, 0), col))

    return [cur(qcol), cur(kcol), prev(kcol), cur(vcol), prev(vcol)]


def _dil_fwd(group, src, cols):
    _, dilation = DIL_GROUPS[group]
    nseq, Ls, _ = src.shape
    qb = _div(Ls, 512, BLOCK)
    nsub = qb // BLOCK

    def body(q_ref, kc_ref, kp_ref, vc_ref, vp_ref, o_ref, lse_ref):
        has_prev = pl.program_id(1) > 0
        lse_ref[...] = jnp.zeros_like(lse_ref)
        for h in range(DIL_HPG):
            hs = slice(HEAD_DIM * h, HEAD_DIM * (h + 1))
            for j in range(nsub):
                rows = slice(j * BLOCK, (j + 1) * BLOCK)
                s = _band_scores(q_ref[rows, hs] * SCALE, _band_operands(j, kc_ref, kp_ref, hs),
                                 _alibi_slope(group, h) * dilation, has_prev if j == 0 else None)
                m = jnp.max(s, axis=-1, keepdims=True)
                p = jnp.exp(s - m).astype(BF16)
                acc = lax.dot_general(p, _with_ones(_band_operands(j, vc_ref, vp_ref, hs)), NN, preferred_element_type=F32)
                l = acc[:, HEAD_DIM:HEAD_DIM + 1]
                o_ref[rows, hs] = acc[:, :HEAD_DIM] / l
                lse_ref[rows, h:h + 1] = m + jnp.log(l)

    return pl.pallas_call(
        body, name=f"dil_forward_{group}",
        out_shape=[jax.ShapeDtypeStruct((nseq, Ls, DIL_GW), F32), jax.ShapeDtypeStruct((nseq, Ls, LANES), F32)],
        grid=(nseq, Ls // qb),
        in_specs=_dil_specs(Ls, qb, cols),
        out_specs=[pl.BlockSpec((None, qb, DIL_GW), lambda s, n: (s, n, 0)),
                   pl.BlockSpec((None, qb, LANES), lambda s, n: (s, n, 0))],
        compiler_params=_params(("parallel", "arbitrary")),
    )(src, src, src, src, src)


def _dil_bwd(group, src, cols, Lr, dyr, dlr):
    _, dilation = DIL_GROUPS[group]
    nseq, Ls, _ = src.shape
    qb = _div(Ls, 512, BLOCK)
    nsub, nb = qb // BLOCK, Ls // qb

    def body(q_ref, kc_ref, kp_ref, vc_ref, vp_ref, L_ref, dy_ref, dl_ref, dq_ref, dk_ref, dv_ref, dk_sc, dv_sc):
        n = pl.program_id(1)
        has_prev = n > 0

        @pl.when(n == 0)
        def _():
            dk_sc[...] = jnp.zeros_like(dk_sc)
            dv_sc[...] = jnp.zeros_like(dv_sc)

        base = pl.multiple_of(n * qb, BLOCK)
        for h in range(DIL_HPG):
            hs = slice(HEAD_DIM * h, HEAD_DIM * (h + 1))
            for j in range(nsub):
                rows = slice(j * BLOCK, (j + 1) * BLOCK)
                qh = q_ref[rows, hs] * SCALE
                kcat = _band_operands(j, kc_ref, kp_ref, hs)
                s = _band_scores(qh, kcat, _alibi_slope(group, h) * dilation, has_prev if j == 0 else None)
                p = jnp.exp(s - L_ref[rows, h:h + 1])
                dyh = dy_ref[rows, hs]
                dp = lax.dot_general(dyh, _band_operands(j, vc_ref, vp_ref, hs), NT, preferred_element_type=F32)
                ds = (p * (dp - dl_ref[rows, h:h + 1])).astype(BF16)
                dq_ref[rows, hs] = (lax.dot_general(ds, kcat, NN, preferred_element_type=F32) * SCALE).astype(dq_ref.dtype)
                win = pl.ds(base + j * BLOCK, 2 * BLOCK)
                dk_sc[win, hs] += lax.dot_general(ds, qh, TN, preferred_element_type=F32)
                dv_sc[win, hs] += lax.dot_general(p.astype(BF16), dyh, TN, preferred_element_type=F32)

        @pl.when(n == nb - 1)
        def _():
            dk_ref[...] = dk_sc[BLOCK:, :].astype(dk_ref.dtype)
            dv_ref[...] = dv_sc[BLOCK:, :].astype(dv_ref.dtype)

    own = pl.BlockSpec((None, qb, DIL_GW), lambda s, n: (s, n, 0))
    own128 = pl.BlockSpec((None, qb, LANES), lambda s, n: (s, n, 0))
    whole = pl.BlockSpec((None, Ls, DIL_GW), lambda s, n: (s, 0, 0))
    shape = jax.ShapeDtypeStruct((nseq, Ls, DIL_GW), BF16)
    return pl.pallas_call(
        body, name=f"dil_backward_{group}",
        out_shape=[shape, shape, shape],
        grid=(nseq, nb),
        in_specs=_dil_specs(Ls, qb, cols) + [own128, own, own128],
        out_specs=[own, whole, whole],
        scratch_shapes=[pltpu.VMEM((Ls + BLOCK, DIL_GW), F32), pltpu.VMEM((Ls + BLOCK, DIL_GW), F32)],
        compiler_params=_params(("parallel", "arbitrary")),
    )(src, src, src, src, src, Lr, dyr, dlr)


def _dil_combine(os_, lses, T, S):
    def fn(o0, o1, o2, l0, l1, l2):
        m = jnp.maximum(jnp.maximum(l0, l1), l2)
        e0, e1, e2 = jnp.exp(l0 - m), jnp.exp(l1 - m), jnp.exp(l2 - m)
        tot = e0 + e1 + e2
        w0, w1, w2 = e0 / tot, e1 / tot, e2 / tot
        parts = []
        for h in range(DIL_HPG):
            hs = slice(HEAD_DIM * h, HEAD_DIM * (h + 1))
            parts.append(w0[:, h:h + 1] * o0[:, hs] + w1[:, h:h + 1] * o1[:, hs] + w2[:, h:h + 1] * o2[:, hs])
        return [jnp.concatenate(parts, axis=1), m + jnp.log(tot)]

    ins = [(a, "row", None) for a in os_] + [(a, "row", None) for a in lses]
    return _rowwise("dil_combine", fn, T, _div(S, 512, 8), ins, [("row", DIL_GW, BF16), ("row", LANES, F32)], S)


def _dil_delta(dy, y, T, S):
    def fn(dyv, yv):
        prod = dyv * yv.astype(F32)
        lane = lax.broadcasted_iota(jnp.int32, (dyv.shape[0], LANES), 1)
        delta = jnp.zeros((dyv.shape[0], LANES), F32)
        for h in range(DIL_HPG):
            hs = slice(HEAD_DIM * h, HEAD_DIM * (h + 1))
            delta = jnp.where(lane == h, jnp.sum(prod[:, hs], axis=-1, keepdims=True), delta)
        return [delta, dyv]

    return _rowwise("dil_delta", fn, T, _div(S, 512, 8), [(dy, "row", None), (y, "row", None)],
                    [("row", LANES, F32), ("row", DIL_GW, BF16)], S)


def _dil_dq(group, pmv, nmb, qa_blk, Lv, dyv, deltav, B, S):
    _, dilation = DIL_GROUPS[group]
    sub = S // dilation
    nb = sub // BLOCK
    qb, kb, vb = qa_blk + group, qa_blk + N_DIL + group, qa_blk + 2 * N_DIL + group

    def body(q_ref, kc_ref, kp_ref, vc_ref, vp_ref, L_ref, dy_ref, dl_ref, dq_ref):
        has_prev = pl.program_id(2) > 0
        nt = (((1,), (1,)), ((), ()))
        nn = (((1,), (0,)), ((), ()))
        for h in range(DIL_HPG):
            hs = slice(HEAD_DIM * h, HEAD_DIM * (h + 1))
            s_cur, s_prev = _dil_tiles(q_ref[:, hs], kc_ref[:, hs], kp_ref[:, hs], _alibi_slope(group, h), dilation, has_prev)
            L, delta, dyh = L_ref[:, h:h + 1], dl_ref[:, h:h + 1], dy_ref[:, hs]
            ds_cur = jnp.exp(s_cur - L) * (lax.dot_general(dyh, vc_ref[:, hs], nt, preferred_element_type=F32) - delta)
            ds_prev = jnp.exp(s_prev - L) * (lax.dot_general(dyh, vp_ref[:, hs], nt, preferred_element_type=F32) - delta)
            dq = (lax.dot_general(ds_cur.astype(BF16), kc_ref[:, hs], nn, preferred_element_type=F32)
                  + lax.dot_general(ds_prev.astype(BF16), kp_ref[:, hs], nn, preferred_element_type=F32))
            dq_ref[:, hs] = (dq * SCALE).astype(dq_ref.dtype)

    def cur(col):
        return pl.BlockSpec((None, BLOCK, DIL_GW), lambda b, r, n: (b, n, r * nmb + col))

    def prev(col):
        return pl.BlockSpec((None, BLOCK, DIL_GW), lambda b, r, n: (b, jnp.maximum(n - 1, 0), r * nmb + col))

    own = pl.BlockSpec((None, BLOCK, DIL_GW), lambda b, r, n: (b, n, r))
    own128 = pl.BlockSpec((None, BLOCK, LANES), lambda b, r, n: (b, n, r))
    return pl.pallas_call(
        body, name=f"dil_dq_{group}",
        out_shape=jax.ShapeDtypeStruct((B, sub, dilation * DIL_GW), BF16),
        grid=(B, dilation, nb),
        in_specs=[cur(qb), cur(kb), prev(kb), cur(vb), prev(vb), own128, own, own128],
        out_specs=own,
        compiler_params=_params(("parallel", "parallel", "arbitrary")),
    )(pmv, pmv, pmv, pmv, pmv, Lv, dyv, deltav)


def _dil_dkv(group, pmv, nmb, qa_blk, Lv, dyv, deltav, B, S):
    _, dilation = DIL_GROUPS[group]
    sub = S // dilation
    nb = sub // BLOCK
    qb, kb, vb = qa_blk + group, qa_blk + N_DIL + group, qa_blk + 2 * N_DIL + group

    def body(k_ref, v_ref, q0_ref, q1_ref, L0_ref, L1_ref, dy0_ref, dy1_ref, dl0_ref, dl1_ref, dk_ref, dv_ref):
        has_next = pl.program_id(2) < nb - 1
        qi = lax.broadcasted_iota(jnp.int32, (BLOCK, BLOCK), 0)
        ki = lax.broadcasted_iota(jnp.int32, (BLOCK, BLOCK), 1)
        rel = (qi - ki).astype(F32)
        nt = (((1,), (1,)), ((), ()))
        tn = (((0,), (0,)), ((), ()))
        for h in range(DIL_HPG):
            hs = slice(HEAD_DIM * h, HEAD_DIM * (h + 1))
            bias = _alibi_slope(group, h) * dilation
            kh, vh, q0, q1 = k_ref[:, hs], v_ref[:, hs], q0_ref[:, hs], q1_ref[:, hs]
            s0 = lax.dot_general(q0, kh, nt, preferred_element_type=F32) * SCALE - bias * rel
            s0 = jnp.where(ki <= qi, s0, NEG_INF)
            s1 = lax.dot_general(q1, kh, nt, preferred_element_type=F32) * SCALE - bias * (rel + BLOCK)
            s1 = jnp.where((ki >= qi) & has_next, s1, NEG_INF)
            p0 = jnp.exp(s0 - L0_ref[:, h:h + 1])
            p1 = jnp.exp(s1 - L1_ref[:, h:h + 1])
            dy0, dy1 = dy0_ref[:, hs], dy1_ref[:, hs]
            ds0 = p0 * (lax.dot_general(dy0, vh, nt, preferred_element_type=F32) - dl0_ref[:, h:h + 1])
            ds1 = p1 * (lax.dot_general(dy1, vh, nt, preferred_element_type=F32) - dl1_ref[:, h:h + 1])
            dv = (lax.dot_general(p0.astype(BF16), dy0, tn, preferred_element_type=F32)
                  + lax.dot_general(p1.astype(BF16), dy1, tn, preferred_element_type=F32))
            dk = (lax.dot_general(ds0.astype(BF16), q0, tn, preferred_element_type=F32)
                  + lax.dot_general(ds1.astype(BF16), q1, tn, preferred_element_type=F32))
            dv_ref[:, hs] = dv.astype(dv_ref.dtype)
            dk_ref[:, hs] = (dk * SCALE).astype(dk_ref.dtype)

    def cur(col):
        return pl.BlockSpec((None, BLOCK, DIL_GW), lambda b, r, n: (b, n, r * nmb + col))

    def nxt(col):
        return pl.BlockSpec((None, BLOCK, DIL_GW), lambda b, r, n: (b, jnp.minimum(n + 1, nb - 1), r * nmb + col))

    own = pl.BlockSpec((None, BLOCK, DIL_GW), lambda b, r, n: (b, n, r))
    own_next = pl.BlockSpec((None, BLOCK, DIL_GW), lambda b, r, n: (b, jnp.minimum(n + 1, nb - 1), r))
    own128 = pl.BlockSpec((None, BLOCK, LANES), lambda b, r, n: (b, n, r))
    own128_next = pl.BlockSpec((None, BLOCK, LANES), lambda b, r, n: (b, jnp.minimum(n + 1, nb - 1), r))
    shape = jax.ShapeDtypeStruct((B, sub, dilation * DIL_GW), BF16)
    return pl.pallas_call(
        body, name=f"dil_dkv_{group}",
        out_shape=[shape, shape],
        grid=(B, dilation, nb),
        in_specs=[cur(kb), cur(vb), cur(qb), nxt(qb), own128, own128_next, own, own_next, own128, own128_next],
        out_specs=[own, own],
        compiler_params=_params(("parallel", "parallel", "arbitrary")),
    )(pmv, pmv, pmv, pmv, Lv, Lv, dyv, dyv, deltav, deltav)


def _ada_forward(c_all, w, b):
    n, D = c_all.shape
    cl = w.shape[1]

    def body(c_ref, w_ref, b_ref, o_ref, ca_ref):
        cv = c_ref[...]
        ca = (cv * _sigmoid(cv)).astype(BF16)
        ca_ref[...] = ca
        o_ref[...] = jnp.dot(ca, w_ref[...].astype(BF16), preferred_element_type=F32) + b_ref[...]

    return pl.pallas_call(
        body, name="ada_forward",
        out_shape=[jax.ShapeDtypeStruct((n, cl), F32), jax.ShapeDtypeStruct((n, D), BF16)],
        compiler_params=_params(),
    )(c_all, w, b)


def _ada_backward(ca, dmod_cols, dmod_all):
    n, D = ca.shape
    cl = dmod_cols.shape[1]

    def body(ca_ref, dc_ref, da_ref, gw_ref, gb_ref):
        gw_ref[...] = lax.dot_general(ca_ref[...], dc_ref[...].astype(BF16), (((0,), (0,)), ((), ())), preferred_element_type=F32)
        gb_ref[...] = _colsum(da_ref[...])

    return pl.pallas_call(
        body, name="ada_backward",
        out_shape=[jax.ShapeDtypeStruct((D, cl), F32), jax.ShapeDtypeStruct((1, dmod_all.shape[1]), F32)],
        compiler_params=_params(),
    )(ca, dmod_cols, dmod_all)


def _sum_devices(v):
    def body(v_ref, o_ref):
        tot = v_ref[0]
        for k in range(1, N_DEV):
            tot = tot + v_ref[k]
        o_ref[...] = tot

    return pl.pallas_call(body, name="sum_devices", out_shape=jax.ShapeDtypeStruct(v.shape[1:], F32))(v)


def _adamw(name, w, g, m, v):
    rows, cols = w.shape
    tr = _div(rows, 256, 8)

    def body(w_ref, g_ref, m_ref, v_ref, d_ref, nm_ref, nv_ref):
        gv = g_ref[...]
        nm = ADAM_B1 * m_ref[...] + (1.0 - ADAM_B1) * gv
        nv = ADAM_B2 * v_ref[...] + (1.0 - ADAM_B2) * (gv * gv)
        m_hat = nm / (1.0 - ADAM_B1 ** ADAM_STEP)
        v_hat = nv / (1.0 - ADAM_B2 ** ADAM_STEP)
        d_ref[...] = -ADAM_LR * (m_hat / (jnp.sqrt(v_hat) + ADAM_EPS) + ADAM_WD * w_ref[...])
        nm_ref[...] = nm
        nv_ref[...] = nv

    spec = pl.BlockSpec((tr, cols), lambda i: (i, 0))
    shape = jax.ShapeDtypeStruct((rows, cols), F32)
    return pl.pallas_call(
        body, name=name, out_shape=[shape, shape, shape], grid=(rows // tr,),
        in_specs=[spec, spec, spec, spec], out_specs=[spec, spec, spec],
        compiler_params=_params(("arbitrary",)),
    )(w, g, m, v)


def _pad_rows(a, rows):
    return a if a.shape[0] == rows else jnp.pad(a, ((0, rows - a.shape[0]), (0, 0)))


class _Packed:
    def __init__(self, kind, local_shape, D, grad_by_columns=False):
        self.kind, self.local_shape, self.D = kind, local_shape, D
        self.grad_by_columns = grad_by_columns or kind == "T"
        r, c = local_shape
        self.rows = {"T": c, "N": r, "F": r * c // D}[kind]
        self.rows_pad = -(-self.rows // ROW_ALIGN) * ROW_ALIGN

    def pack_local(self, w):
        if self.kind == "T":
            w = w.T
        elif self.kind == "F":
            w = w.reshape(self.rows, self.D)
        return _pad_rows(w, self.rows_pad)

    def full(self, gathered):
        g = gathered[:, :self.rows]
        if self.kind == "F":
            r, c = self.local_shape
            return g.reshape(N_DEV, r, c).transpose(1, 0, 2).reshape(r, N_DEV * c)
        return g.reshape(N_DEV * self.rows, self.D)

    def pack_grad(self, gfull):
        if self.grad_by_columns:
            g = gfull.reshape(self.D, N_DEV, self.rows).transpose(1, 0, 2).reshape(N_DEV, self.rows, self.D)
        elif self.kind == "F":
            r, c = self.local_shape
            g = gfull.reshape(r, N_DEV, c).transpose(1, 0, 2).reshape(N_DEV, self.rows, self.D)
        else:
            g = gfull.reshape(N_DEV, self.rows, self.D)
        if self.rows_pad != self.rows:
            g = jnp.pad(g, ((0, 0), (0, self.rows_pad - self.rows), (0, 0)))
        return g

    def unpack_local(self, g):
        g = g[:self.rows]
        if self.grad_by_columns:
            g = g.reshape(self.D, self.rows)
            return g if self.kind == "T" else g.T
        if self.kind == "F":
            return g.reshape(self.local_shape)
        return g


BIG = ["ffn1_w_gate", "ffn1_w_up", "ffn1_w_down", "w_in", "w_branch_a", "w_branch_b", "w_out",
       "ffn2_w_gate", "ffn2_w_up", "ffn2_w_down"]
BIG_KIND = {"ffn1_w_gate": "T", "ffn1_w_up": "T", "ffn1_w_down": "N", "w_in": "T", "w_branch_a": "F", "w_branch_b": "F",
            "w_out": "N", "ffn2_w_gate": "T", "ffn2_w_up": "T", "ffn2_w_down": "N"}
SMALL = ["ada_b", "norm_ffn1", "norm_mix", "forget_bias", "norm_ffn2", "norm_final"]


def kernel(x, c, ada_w, ada_b, norm_ffn1, ffn1_w_gate, ffn1_w_up, ffn1_w_down, norm_mix, w_in, forget_bias, w_branch_a, w_branch_b, w_out, norm_ffn2, ffn2_w_gate, ffn2_w_up, ffn2_w_down, norm_final, loss_target, m_ada_w, m_ada_b, m_norm_ffn1, m_ffn1_w_gate, m_ffn1_w_up, m_ffn1_w_down, m_norm_mix, m_w_in, m_forget_bias, m_w_branch_a, m_w_branch_b, m_w_out, m_norm_ffn2, m_ffn2_w_gate, m_ffn2_w_up, m_ffn2_w_down, m_norm_final, v_ada_w, v_ada_b, v_norm_ffn1, v_ffn1_w_gate, v_ffn1_w_up, v_ffn1_w_down, v_norm_mix, v_w_in, v_forget_bias, v_w_branch_a, v_w_branch_b, v_w_out, v_norm_ffn2, v_ffn2_w_gate, v_ffn2_w_up, v_ffn2_w_down, v_norm_final):
    args = dict(locals())
    B, S, D = x.shape
    T = B * S
    cl = ada_w.shape[2]
    n_in = w_in.shape[2] * N_DEV
    nm = 2 * D + 3 * FOX_W + 3 * DIL_W
    nmp = -(-nm // 512) * 512
    GA, GB, QB, QA = 0, D, 2 * D, 2 * D + 3 * FOX_W
    xpos, ypos, cpos = _position()
    me = 4 * xpos + 2 * ypos + cpos

    packs = {n: _Packed(BIG_KIND[n], args[n].shape[1:], D, grad_by_columns=n.endswith("w_down")) for n in BIG}
    offs, r = {}, 0
    for n in BIG:
        offs[n] = r
        r += packs[n].rows_pad
    pad_rows = -r % PACK_ROW_QUANTUM
    p_local = jnp.concatenate([packs[n].pack_local(args[n][0]).astype(BF16) for n in BIG]
                              + [jnp.zeros((pad_rows, D), BF16)], axis=0)
    gathered = _weight_allgather(p_local)
    W = {n: packs[n].full(gathered[:, offs[n]:offs[n] + packs[n].rows_pad]) for n in BIG}
    winT = W["w_in"]
    o_f = 3 * DIL_W + 3 * FOX_W
    wmT = jnp.concatenate([winT[o_f + 8:], winT[3 * DIL_W:o_f], winT[:3 * DIL_W], jnp.zeros((nmp - nm, D), BF16)], axis=0)
    wfT = jnp.concatenate([winT[o_f:o_f + 8], jnp.zeros((LANES - 8, D), BF16)], axis=0)

    c_all = _small_allgather(c, "gather_c").reshape(N_DEV * B, D)
    b_cols = lax.dynamic_slice(ada_b, (0, me * cl), (1, cl))
    mod_cols, c_act = _ada_forward(c_all, ada_w[0], b_cols)
    mod_all = _small_allgather(mod_cols, "gather_mod").transpose(1, 0, 2).reshape(N_DEV * B, N_MOD * D)
    mod = lax.dynamic_slice(mod_all, (me * B, 0), (B, N_MOD * D)).reshape(B, N_MOD, 1, D)
    sh1, sc1, gt1, sh2, sc2, gt2, sh3, sc3, gt3 = [mod[:, i] for i in range(N_MOD)]

    x0 = x.reshape(T, D)
    x1, saved1 = _ffn_forward("ffn1", x0, norm_ffn1, sh1, sc1, gt1, W["ffn1_w_gate"], W["ffn1_w_up"], W["ffn1_w_down"], S)

    h2 = _normmod("mix_normmod", x1, norm_mix, sc2, sh2, S)
    tm1k = _div(T, 1024, 8)
    pm = _matmul("mix_proj", "nt", [[(h2, wmT)]], T, nmp, D, _div(T, 256, 8), nmp, D, [BF16])[0]
    fraw = _matmul("mix_proj_f", "nt", [[(h2, wfT)]], T, LANES, D, tm1k, LANES, D, [F32])[0]
    fb = jnp.pad(forget_bias, ((0, 0), (0, LANES - FOX_HEADS)))

    def forget_fn(fr, fbv):
        fl = fr + fbv
        lane = lax.broadcasted_iota(jnp.int32, fl.shape, 1)
        ls = jnp.minimum(fl, 0.0) - jnp.log(1.0 + jnp.exp(-jnp.abs(fl)))
        return [jnp.where(lane < FOX_HEADS, ls, 0.0), fl]

    tms = _div(S, 512, 8)
    logsig, flog = _rowwise("forget_gate", forget_fn, T, tms, [(fraw, "row", None), (fb, "vec", None)],
                            [("row", LANES, F32), ("row", LANES, F32)], S)
    cum = _cumsum(logsig.reshape(B, S, LANES))
    cumT = cum[:, :, :8].transpose(0, 2, 1)
    pm3 = pm.reshape(B, S, nmp)
    tq = _div(S, 512, LANES)
    qcol, kcol, vcol = QB // FOX_W, QB // FOX_W + 1, QB // FOX_W + 2
    o_b, o_b32, lse_b = _fox_fwd(pm3, cumT, qcol, kcol, vcol, tq)
    y_b = o_b.reshape(T, FOX_W)

    qa_blk = QA // DIL_GW
    dil_src, dil_cols = [], []
    for g, (_, d) in enumerate(DIL_GROUPS):
        if d == 1:
            dil_src.append(pm3)
            dil_cols.append((qa_blk + g, qa_blk + N_DIL + g, qa_blk + 2 * N_DIL + g))
        else:
            starts = [QA + (i * N_DIL + g) * DIL_GW for i in range(3)]
            qkv = jnp.concatenate([pm[:, c:c + DIL_GW] for c in starts], axis=1)
            dil_src.append(_residue_order(qkv, B, S, d))
            dil_cols.append((0, 1, 2))
    dil_o, dil_lse = [], []
    for g, (_, d) in enumerate(DIL_GROUPS):
        o_g, lse_g = _dil_fwd(g, dil_src[g], dil_cols[g])
        dil_o.append(_token_order(o_g, B, S, d))
        dil_lse.append(_token_order(lse_g, B, S, d))
    y_a, L_a = _dil_combine(dil_o, dil_lse, T, S)

    wa, wb, wout = W["w_branch_a"], W["w_branch_b"], W["w_out"]
    tnd = D
    tm5 = _div(T, 512, 8)
    yap = _matmul("mix_branch_a", "nn", [[(y_a, wa)]], T, D, DIL_GW, tm5, tnd, DIL_GW, [BF16])[0]

    def merge(accs, ex):
        yapv, gav, gbv = ex
        ybp = accs[0]
        return [ybp, _sigmoid(gav.astype(F32)) * yapv.astype(F32) + _sigmoid(gbv.astype(F32)) * ybp]

    ybp, merged = _matmul("mix_branch_b", "nn", [[(y_b, wb)]], T, D, FOX_W, tm5, tnd, FOX_W, [BF16, BF16],
                          extras=[(yap, "tile", 0), (pm, "tile", GA), (pm, "tile", GB)], epilogue=merge)

    def out_proj(accs, ex):
        xv, gtv = ex
        return [xv + gtv * accs[0], accs[0]]

    x2, ymix = _matmul("mix_out", "nn", [[(merged, wout)]], T, D, D, tms, tnd, D, [F32, BF16],
                       extras=[(x1, "tile", 0), (gt2, "brow", 0)], epilogue=out_proj, rows_per_example=S)

    x3, saved3 = _ffn_forward("ffn2", x2, norm_ffn2, sh3, sc3, gt3, W["ffn2_w_gate"], W["ffn2_w_up"], W["ffn2_w_down"], S)

    dx3, loss_b, dg_final = _loss_head(x3, loss_target.reshape(T, D), norm_final.reshape(1, D), S)
    dx2, (dsh3, dsc3, dgt3, dg3), (dwg2, dwu2, dwd2) = _ffn_backward(
        "ffn2", dx3, saved3, norm_ffn2, sc3, gt3, W["ffn2_w_gate"], W["ffn2_w_up"], W["ffn2_w_down"], S)

    dym, dgt2 = _gate_grad("mix_gate_grad", dx2, ymix, gt2, 1.0, S)
    tkw = _div(T, 512, LANES)
    dwout = _matmul("mix_dw_out", "tn", [[(merged, dym)]], D, D, T, D, D, tkw, [F32])[0]

    def merge_grad(accs, ex):
        gav, gbv, yapv, ybpv = [e.astype(F32) for e in ex]
        dm = accs[0]
        sga, sgb = _sigmoid(gav), _sigmoid(gbv)
        return [dm * sga, dm * sgb, dm * yapv * sga * (1.0 - sga), dm * ybpv * sgb * (1.0 - sgb)]

    dyap, dybp, dga, dgb = _matmul("mix_merge_grad", "nt", [[(dym, wout)]], T, D, D, tm5, tnd, D, [BF16] * 4,
                                   extras=[(pm, "tile", GA), (pm, "tile", GB), (yap, "tile", 0), (ybp, "tile", 0)],
                                   epilogue=merge_grad)
    dwa = _matmul("mix_dw_a", "tn", [[(y_a, dyap)]], DIL_GW, D, T, DIL_GW, D, tkw, [F32])[0]
    dwb = _matmul("mix_dw_b", "tn", [[(y_b, dybp)]], FOX_W, D, T, FOX_W, D, tkw, [F32])[0]
    dy_a = _matmul("mix_dy_a", "nt", [[(dyap, wa)]], T, DIL_GW, D, tm1k, DIL_GW, D, [F32])[0]
    dy_b = _matmul("mix_dy_b", "nt", [[(dybp, wb)]], T, FOX_W, D, tm1k, FOX_W, D, [BF16])[0]

    do3 = dy_b.reshape(B, S, FOX_W)
    delta_b = _fox_delta(dy_b, o_b32.reshape(T, FOX_W), T, S).reshape(B, S, LANES)
    dq_b, ds_rows, dk_b, dv_b, ds_cols = _fox_bwd(pm3, do3, delta_b, lse_b, cumT, qcol, kcol, vcol, tq)

    delta_a, dy_a16 = _dil_delta(dy_a, y_a, T, S)
    dqs, dks, dvs = [], [], []
    for g, (_, d) in enumerate(DIL_GROUPS):
        dq_g, dk_g, dv_g = _dil_bwd(g, dil_src[g], dil_cols[g], _residue_order(L_a, B, S, d),
                                    _residue_order(dy_a16, B, S, d), _residue_order(delta_a, B, S, d))
        dqs.append(_token_order(dq_g, B, S, d))
        dks.append(_token_order(dk_g, B, S, d))
        dvs.append(_token_order(dv_g, B, S, d))

    dcum = ds_rows - ds_cols
    dcum_run = _cumsum(dcum)
    dcum_tot = dcum_run[:, S - 1:S, :]

    def forget_grad_fn(run, dcv, fl, tot):
        lane = lax.broadcasted_iota(jnp.int32, fl.shape, 1)
        df = jnp.where(lane < FOX_HEADS, (tot - run + dcv) * _sigmoid(-fl), 0.0)
        return [df, _colsum(df)]

    df16, dfb = _rowwise("forget_gate_grad", forget_grad_fn, T, tms,
                         [(dcum_run.reshape(T, LANES), "row", None), (dcum.reshape(T, LANES), "row", None), (flog, "row", None),
                          (dcum_tot, "bvec", None)],
                         [("row", LANES, BF16), ("bacc", LANES, F32)], S)

    dpm = jnp.concatenate([dga, dgb, dq_b.reshape(T, FOX_W), dk_b.reshape(T, FOX_W), dv_b.reshape(T, FOX_W)]
                          + dqs + dks + dvs + ([jnp.zeros((T, nmp - nm), BF16)] if nmp > nm else []), axis=1)
    dwm = _matmul("mix_dw_in", "tn", [[(h2, dpm)]], D, nmp, T, D, _div(nmp, 2048, LANES), tkw, [F32])[0]
    dwf = _matmul("mix_dw_f", "tn", [[(h2, df16)]], D, LANES, T, D, LANES, tkw, [F32])[0]
    dh2f = _matmul("mix_dh_f", "nn", [[(df16, wfT)]], T, D, LANES, tm5, tnd, LANES, [F32])[0]

    def add_tile(accs, ex):
        return [accs[0] + ex[0]]

    dh2 = _matmul("mix_dh", "nn", [[(dpm, wmT)]], T, D, nmp, _div(T, 256, 8), tnd, nmp, [F32],
                  extras=[(dh2f, "tile", 0)], epilogue=add_tile)[0]
    dx1, dsh2, dsc2, dgmix = _normmod_bwd("mix_normmod_bwd", x1, norm_mix, sc2, dh2, dx2, S)

    dx0, (dsh1, dsc1, dgt1, dg1), (dwg1, dwu1, dwd1) = _ffn_backward(
        "ffn1", dx1, saved1, norm_ffn1, sc1, gt1, W["ffn1_w_gate"], W["ffn1_w_up"], W["ffn1_w_down"], S)
    grad_x = dx0.reshape(B, S, D)

    dmod = jnp.concatenate([dsh1, dsc1, dgt1, dsh2, dsc2, dgt2, dsh3, dsc3, dgt3], axis=1).reshape(B, N_MOD * D)
    dmod_all = _small_allgather(dmod, "gather_dmod").reshape(N_DEV * B, N_MOD * D)
    dmod_cols = lax.dynamic_slice(dmod_all, (0, me * cl), (N_DEV * B, cl))
    g_ada_w, g_ada_b = _ada_backward(c_act, dmod_cols, dmod_all)

    fbg = jnp.sum(dfb, axis=0)
    small = jnp.concatenate([jnp.sum(dg1, axis=0), jnp.sum(dgmix, axis=0), jnp.sum(dg3, axis=0), jnp.sum(dg_final, axis=0),
                             fbg, jnp.sum(loss_b, axis=0)], axis=1)
    small = _sum_devices(_small_allgather(small, "gather_small"))
    g_small = {"norm_ffn1": small[:, 0:D], "norm_mix": small[:, D:2 * D], "norm_ffn2": small[:, 2 * D:3 * D],
               "norm_final": small[:, 3 * D:4 * D], "forget_bias": small[:, 4 * D:4 * D + FOX_HEADS], "ada_b": g_ada_b}
    loss = small[0, 4 * D + LANES]

    dwin = jnp.concatenate([dwm[:, QA:QA + 3 * DIL_W], dwm[:, QB:QB + 3 * FOX_W], dwf[:, :8], dwm[:, GA:2 * D]], axis=1)
    gfull = {"ffn1_w_gate": dwg1, "ffn1_w_up": dwu1, "ffn1_w_down": dwd1, "w_in": dwin, "w_branch_a": dwa, "w_branch_b": dwb,
             "w_out": dwout, "ffn2_w_gate": dwg2, "ffn2_w_up": dwu2, "ffn2_w_down": dwd2}
    g_packed = jnp.concatenate([packs[n].pack_grad(gfull[n]) for n in BIG] + [jnp.zeros((N_DEV, pad_rows, D), F32)], axis=1)
    g_local = _reduce_scatter(g_packed)
    grads = {n: packs[n].unpack_local(g_local[offs[n]:offs[n] + packs[n].rows_pad])[None] for n in BIG}
    grads["ada_w"] = g_ada_w[None]

    delta, new_m, new_v = {}, {}, {}
    for n in ["ada_w"] + BIG:
        shp = args[n].shape
        d_, m_, v_ = _adamw(f"adamw_{n}", args[n][0], grads[n][0], args["m_" + n][0], args["v_" + n][0])
        delta[n], new_m[n], new_v[n] = d_.reshape(shp), m_.reshape(shp), v_.reshape(shp)
    sizes = [args[n].size for n in SMALL]
    tot = sum(sizes)
    padded = -(-tot // (8 * LANES)) * (8 * LANES)

    def flat(get):
        v = jnp.concatenate([get(n).reshape(-1) for n in SMALL])
        return jnp.pad(v, (0, padded - tot)).reshape(8, padded // 8)

    d_s, m_s, v_s = _adamw("adamw_small", flat(lambda n: args[n]), flat(lambda n: g_small[n]), flat(lambda n: args["m_" + n]),
                           flat(lambda n: args["v_" + n]))
    o = 0
    for n, sz in zip(SMALL, sizes):
        shp = args[n].shape
        grads[n] = g_small[n].reshape(shp)
        delta[n] = d_s.reshape(-1)[o:o + sz].reshape(shp)
        new_m[n] = m_s.reshape(-1)[o:o + sz].reshape(shp)
        new_v[n] = v_s.reshape(-1)[o:o + sz].reshape(shp)
        o += sz

    order = ["ada_w", "ada_b", "norm_ffn1", "ffn1_w_gate", "ffn1_w_up", "ffn1_w_down", "norm_mix", "w_in", "forget_bias",
             "w_branch_a", "w_branch_b", "w_out", "norm_ffn2", "ffn2_w_gate", "ffn2_w_up", "ffn2_w_down", "norm_final"]
    return (loss, grad_x, *[grads[n] for n in order], *[delta[n] for n in order], *[new_m[n] for n in order],
            *[new_v[n] for n in order])
```

```python
import functools
import math

import jax
import jax.numpy as jnp
from jax import lax
from jax.experimental import pallas as pl
from jax.experimental.pallas import tpu as pltpu

F32 = jnp.float32
BF16 = jnp.bfloat16
MESH = pl.DeviceIdType.MESH
ANY = pl.BlockSpec(memory_space=pl.ANY)
VMEM_SPEC = pl.BlockSpec(memory_space=pltpu.VMEM)

N_DEV = 8
HEAD_DIM = 64
BLOCK = 128
DIL_GROUPS = ((128, 1), (512, 4), (2048, 16))
N_DIL = len(DIL_GROUPS)
DIL_HPG = 4
DIL_GW = DIL_HPG * HEAD_DIM
DIL_W = N_DIL * DIL_GW
FOX_HEADS = 8
FOX_W = FOX_HEADS * HEAD_DIM
N_MOD = 9
RMS_EPS = 1e-6
ALIBI_MAX_BIAS = 8.0
NEG_INF = -1e30
ADAM_LR, ADAM_B1, ADAM_B2, ADAM_EPS, ADAM_WD, ADAM_STEP = 0.001, 0.9, 0.999, 1e-08, 0.01, 10
V7X_VMEM_LIMIT = 52 * 1024 * 1024
LANES = 128
ROW_ALIGN = 16
PACK_ROW_QUANTUM = 256
FOX_STRIP = 32
SCALE = 1.0 / math.sqrt(HEAD_DIM)


def _div(dim, target, quantum):
    best = None
    for t in range(quantum, min(dim, target) + 1, quantum):
        if dim % t == 0:
            best = t
    return best or dim


def _params(sem=None):
    return pltpu.CompilerParams(dimension_semantics=sem, vmem_limit_bytes=V7X_VMEM_LIMIT)


def _sigmoid(x):
    return 1.0 / (1.0 + jnp.exp(-x))


def _position():
    x, y, c = lax.axis_index("x"), lax.axis_index("y"), lax.axis_index("c")
    return x, y, c


def _small_allgather(v, name):
    rows, cols = v.shape

    def body(v_ref, out_ref, send_sems, recv_sems):
        x, y, c = _position()
        me = 4 * x + 2 * y + c
        out_ref[me] = v_ref[...]

        def peer(k):
            return (1 - x if k & 4 else x, 1 - y if k & 2 else y, 1 - c if k & 1 else c)

        def copy(k, slot):
            return pltpu.make_async_remote_copy(
                src_ref=v_ref, dst_ref=out_ref.at[slot], send_sem=send_sems.at[k - 1], recv_sem=recv_sems.at[k - 1],
                device_id=peer(k), device_id_type=MESH)

        sends = [copy(k, me) for k in range(1, N_DEV)]
        for cp in sends:
            cp.start()
        for k in range(1, N_DEV):
            px, py, pc = peer(k)
            copy(k, 4 * px + 2 * py + pc).wait_recv()
        for cp in sends:
            cp.wait_send()

    return pl.pallas_call(
        body, name=name,
        out_shape=jax.ShapeDtypeStruct((N_DEV, rows, cols), v.dtype),
        in_specs=[VMEM_SPEC], out_specs=VMEM_SPEC,
        scratch_shapes=[pltpu.SemaphoreType.DMA((N_DEV - 1,)), pltpu.SemaphoreType.DMA((N_DEV - 1,))],
    )(v)


def _weight_allgather(p):
    rows, cols = p.shape

    def body(p_ref, out_ref, send_sems, recv_sems, local_sem):
        x, y, c = _position()
        me, sibling = (x, y, c), (x, y, 1 - c)
        chips = [(1 - x, y), (x, 1 - y), (1 - x, 1 - y)]

        def slot(px, py, pc):
            return out_ref.at[4 * px + 2 * py + pc]

        def copy(k, block, to, src=None):
            return pltpu.make_async_remote_copy(
                src_ref=slot(*block) if src is None else src, dst_ref=slot(*block),
                send_sem=send_sems.at[k], recv_sem=recv_sems.at[k], device_id=to, device_id_type=MESH)

        mine = pltpu.make_async_copy(p_ref, slot(*me), local_sem)
        mine.start()
        first = [copy(0, me, sibling, src=p_ref)]
        first += [copy(1 + j, me, (*chip, c), src=p_ref) for j, chip in enumerate(chips)]
        for cp in first:
            cp.start()
        passed = [copy(4 + j, (*chip, c), sibling) for j, chip in enumerate(chips)]
        for j, chip in enumerate(chips):
            copy(1 + j, (*chip, c), me).wait_recv()
            passed[j].start()
        copy(0, sibling, me).wait_recv()
        for j, chip in enumerate(chips):
            copy(4 + j, (*chip, 1 - c), me).wait_recv()
        for cp in first + passed:
            cp.wait_send()
        mine.wait()

    return pl.pallas_call(
        body, name="weight_allgather",
        out_shape=jax.ShapeDtypeStruct((N_DEV, rows, cols), p.dtype),
        in_specs=[ANY], out_specs=ANY,
        scratch_shapes=[pltpu.SemaphoreType.DMA((7,)), pltpu.SemaphoreType.DMA((7,)), pltpu.SemaphoreType.DMA],
    )(p)


def _grad_exchange_sibling(g):
    _, rows, cols = g.shape

    def body(g_ref, out_ref, send_sems, recv_sems):
        x, y, c = _position()
        sibling = (x, y, 1 - c)

        def copy(q):
            px, py = q >> 1, q & 1
            return pltpu.make_async_remote_copy(
                src_ref=g_ref.at[4 * px + 2 * py + (1 - c)], dst_ref=out_ref.at[q],
                send_sem=send_sems.at[q], recv_sem=recv_sems.at[q], device_id=sibling, device_id_type=MESH)

        copies = [copy(q) for q in range(4)]
        for cp in copies:
            cp.start()
        for cp in copies:
            cp.wait_recv()
        for cp in copies:
            cp.wait_send()

    return pl.pallas_call(
        body, name="grad_exchange_sibling",
        out_shape=jax.ShapeDtypeStruct((4, rows, cols), g.dtype),
        in_specs=[ANY], out_specs=ANY,
        scratch_shapes=[pltpu.SemaphoreType.DMA((4,)), pltpu.SemaphoreType.DMA((4,))],
    )(g)


def _grad_exchange_chips(s):
    _, rows, cols = s.shape

    def body(s_ref, out_ref, send_sems, recv_sems):
        x, y, c = _position()
        chips = [(1 - x, y), (x, 1 - y), (1 - x, 1 - y)]

        def copy(k):
            return pltpu.make_async_remote_copy(
                src_ref=s_ref.at[k], dst_ref=out_ref.at[k], send_sem=send_sems.at[k], recv_sem=recv_sems.at[k],
                device_id=(*chips[k], c), device_id_type=MESH)

        copies = [copy(k) for k in range(3)]
        for cp in copies:
            cp.start()
        for cp in copies:
            cp.wait_recv()
        for cp in copies:
            cp.wait_send()

    return pl.pallas_call(
        body, name="grad_exchange_chips",
        out_shape=jax.ShapeDtypeStruct((3, rows, cols), s.dtype),
        in_specs=[ANY], out_specs=ANY,
        scratch_shapes=[pltpu.SemaphoreType.DMA((3,)), pltpu.SemaphoreType.DMA((3,))],
    )(s)


def _chip_partial_sums(g, recv_sib, jj, qq):
    _, rows, cols = g.shape
    tr = _div(rows, 512, ROW_ALIGN)

    def body(jj_ref, qq_ref, g_ref, r_ref, o_ref):
        o_ref[...] = (g_ref[...] + r_ref[...]).astype(o_ref.dtype)

    return pl.pallas_call(
        body, name="chip_partial_sums",
        out_shape=jax.ShapeDtypeStruct((3, rows, cols), BF16),
        grid_spec=pltpu.PrefetchScalarGridSpec(
            num_scalar_prefetch=2, grid=(3, rows // tr),
            in_specs=[pl.BlockSpec((None, tr, cols), lambda k, i, jj, qq: (jj[k], i, 0)),
                      pl.BlockSpec((None, tr, cols), lambda k, i, jj, qq: (qq[k], i, 0))],
            out_specs=pl.BlockSpec((None, tr, cols), lambda k, i, jj, qq: (k, i, 0))),
        compiler_params=_params(("arbitrary", "arbitrary")),
    )(jj, qq, g, recv_sib)


def _own_partial_sum(g, recv_sib, jj, qq):
    _, rows, cols = g.shape
    tr = _div(rows, 512, ROW_ALIGN)

    def body(jj_ref, qq_ref, g_ref, r_ref, o_ref):
        o_ref[...] = g_ref[...] + r_ref[...]

    return pl.pallas_call(
        body, name="own_partial_sum",
        out_shape=jax.ShapeDtypeStruct((rows, cols), F32),
        grid_spec=pltpu.PrefetchScalarGridSpec(
            num_scalar_prefetch=2, grid=(rows // tr,),
            in_specs=[pl.BlockSpec((None, tr, cols), lambda i, jj, qq: (jj[0], i, 0)),
                      pl.BlockSpec((None, tr, cols), lambda i, jj, qq: (qq[0], i, 0))],
            out_specs=pl.BlockSpec((tr, cols), lambda i, jj, qq: (i, 0))),
        compiler_params=_params(("arbitrary",)),
    )(jj, qq, g, recv_sib)


def _final_grad_sum(own, recv):
    rows, cols = own.shape
    tr = _div(rows, 512, ROW_ALIGN)

    def body(o_ref, r_ref, out_ref):
        out_ref[...] = ((o_ref[...] + r_ref[0].astype(F32)) + r_ref[1].astype(F32)) + r_ref[2].astype(F32)

    return pl.pallas_call(
        body, name="final_grad_sum",
        out_shape=jax.ShapeDtypeStruct((rows, cols), F32),
        grid=(rows // tr,),
        in_specs=[pl.BlockSpec((tr, cols), lambda i: (i, 0)), pl.BlockSpec((3, tr, cols), lambda i: (0, i, 0))],
        out_specs=pl.BlockSpec((tr, cols), lambda i: (i, 0)),
        compiler_params=_params(("arbitrary",)),
    )(own, recv)


def _reduce_scatter(g):
    x, y, c = _position()
    chips = [(1 - x, y), (x, 1 - y), (1 - x, 1 - y)]
    jj = jnp.stack([4 * px + 2 * py + c for px, py in chips]).astype(jnp.int32)
    qq = jnp.stack([2 * px + py for px, py in chips]).astype(jnp.int32)
    jme = jnp.reshape(4 * x + 2 * y + c, (1,)).astype(jnp.int32)
    qme = jnp.reshape(2 * x + y, (1,)).astype(jnp.int32)
    recv_sib = _grad_exchange_sibling(g)
    sums = _chip_partial_sums(g, recv_sib, jj, qq)
    own = _own_partial_sum(g, recv_sib, jme, qme)
    recv = _grad_exchange_chips(sums)
    return _final_grad_sum(own, recv)


def _matmul(name, form, prods, M, N, K, tm, tn, tk, out_dtypes, extras=(), epilogue=None, rows_per_example=None):
    nk = K // tk
    n_acc = len(prods)
    flat = [ab for group in prods for ab in group]
    dims = {"nn": (((1,), (0,)), ((), ())), "nt": (((1,), (1,)), ((), ())), "tn": (((0,), (0,)), ((), ()))}[form]
    direct = nk > 1 and epilogue is None and n_acc == 1 and list(out_dtypes) == [F32]

    def spec(shape, index_map, whole):
        if whole:
            return pl.BlockSpec(shape, index_map, pipeline_mode=pl.Buffered(1))
        return pl.BlockSpec(shape, index_map)

    if form == "tn":
        a_spec = spec((tk, tm), lambda i, j, k: (k, i), nk == 1 and M == tm)
    else:
        a_spec = spec((tm, tk), lambda i, j, k: (i, k), nk == 1 and M == tm)
    if form == "nt":
        b_spec = spec((tn, tk), lambda i, j, k: (j, k), nk == 1 and N == tn)
    else:
        b_spec = spec((tk, tn), lambda i, j, k: (k, j), nk == 1 and N == tn)
    in_specs, operands = [], []
    for a, b in flat:
        in_specs += [a_spec, b_spec]
        operands += [a, b]
    for arr, kind, off in extras:
        if kind == "tile":
            assert off % tn == 0
            in_specs.append(pl.BlockSpec((tm, tn), functools.partial(lambda i, j, k, o: (i, j + o), o=off // tn)))
        else:
            tiles = rows_per_example // tm
            in_specs.append(pl.BlockSpec((None, 1, tn), functools.partial(lambda i, j, k, t: (i // t, 0, j), t=tiles)))
        operands.append(arr)
    n_in, n_out = len(operands), len(out_dtypes)

    def body(*refs):
        in_refs, out_refs, acc_refs = refs[:n_in], refs[n_in:n_in + n_out], refs[n_in + n_out:]
        k = pl.program_id(2)
        partials, p = [], 0
        for group in prods:
            tot = None
            for _ in group:
                d = lax.dot_general(in_refs[2 * p][...], in_refs[2 * p + 1][...], dims, preferred_element_type=F32)
                tot = d if tot is None else tot + d
                p += 1
            partials.append(tot)

        def finish(accs):
            ex = [r[...] for r in in_refs[2 * len(flat):]]
            outs = epilogue(accs, ex) if epilogue is not None else accs
            for r, o in zip(out_refs, outs):
                r[...] = o.astype(r.dtype)

        if nk == 1:
            finish(partials)
        elif direct:
            @pl.when(k == 0)
            def _():
                out_refs[0][...] = partials[0]

            @pl.when(k > 0)
            def _():
                out_refs[0][...] += partials[0]
        else:
            @pl.when(k == 0)
            def _():
                for r, v in zip(acc_refs, partials):
                    r[...] = v

            @pl.when(k > 0)
            def _():
                for r, v in zip(acc_refs, partials):
                    r[...] += v

            @pl.when(k == nk - 1)
            def _():
                finish([r[...] for r in acc_refs])

    outs = pl.pallas_call(
        body, name=name,
        out_shape=[jax.ShapeDtypeStruct((M, N), dt) for dt in out_dtypes],
        grid=(M // tm, N // tn, nk),
        in_specs=in_specs,
        out_specs=[pl.BlockSpec((tm, tn), lambda i, j, k: (i, j)) for _ in out_dtypes],
        scratch_shapes=[pltpu.VMEM((tm, tn), F32) for _ in range(n_acc)] if nk > 1 and not direct else [],
        compiler_params=_params(("parallel", "parallel", "arbitrary")),
    )(*operands)
    return outs


def _rowwise(name, fn, T, tm, ins, outs, rows_per_example):
    tiles = rows_per_example // tm
    n_ex = T // rows_per_example
    in_specs, operands = [], []
    for arr, kind, arg in ins:
        if kind == "row":
            if arg is None:
                in_specs.append(pl.BlockSpec((tm, arr.shape[1]), lambda i: (i, 0)))
            else:
                in_specs.append(pl.BlockSpec((tm, arg[0]), functools.partial(lambda i, cb: (i, cb), cb=arg[1])))
        elif kind == "bvec":
            in_specs.append(pl.BlockSpec((None, 1, arr.shape[2]), lambda i: (i // tiles, 0, 0)))
        else:
            in_specs.append(pl.BlockSpec((1, arr.shape[1]), lambda i: (0, 0)))
        operands.append(arr)
    out_shape, out_specs = [], []
    for kind, cols, dt in outs:
        if kind == "row":
            out_shape.append(jax.ShapeDtypeStruct((T, cols), dt))
            out_specs.append(pl.BlockSpec((tm, cols), lambda i: (i, 0)))
        else:
            out_shape.append(jax.ShapeDtypeStruct((n_ex, 1, cols), F32))
            out_specs.append(pl.BlockSpec((None, 1, cols), lambda i: (i // tiles, 0, 0)))
    n_in = len(operands)

    def body(*refs):
        i = pl.program_id(0)
        vals = fn(*[r[...] for r in refs[:n_in]])
        for (kind, _, _), r, v in zip(outs, refs[n_in:], vals):
            if kind == "row":
                r[...] = v.astype(r.dtype)
            else:
                @pl.when(i % tiles == 0)
                def _():
                    r[...] = jnp.zeros_like(r)

                r[...] += v

    return pl.pallas_call(
        body, name=name, out_shape=out_shape, grid=(T // tm,), in_specs=in_specs, out_specs=out_specs,
        compiler_params=_params(("arbitrary",)),
    )(*operands)


def _colsum(v):
    return jnp.sum(v, axis=0, keepdims=True)


def _rms_parts(x):
    rstd = lax.rsqrt(jnp.mean(x * x, axis=-1, keepdims=True) + RMS_EPS)
    return x * rstd, rstd


def _normmod(name, x, g, sc, sh, S):
    T, D = x.shape

    def fn(xv, gv, scv, shv):
        xhat, _ = _rms_parts(xv)
        return [(xhat * gv) * (1.0 + scv) + shv]

    return _rowwise(name, fn, T, _div(S, 512, 8), [(x, "row", None), (g, "vec", None), (sc, "bvec", None), (sh, "bvec", None)],
                    [("row", D, BF16)], S)[0]


def _normmod_bwd(name, x, g, sc, dh, dres, S):
    T, D = x.shape

    def fn(xv, gv, scv, dhv, drv):
        xhat, rstd = _rms_parts(xv)
        n = xhat * gv
        dn = dhv * (1.0 + scv)
        dxh = dn * gv
        dx = rstd * (dxh - xhat * jnp.mean(dxh * xhat, axis=-1, keepdims=True))
        return [drv + dx, _colsum(dhv), _colsum(dhv * n), _colsum(dn * xhat)]

    return _rowwise(name, fn, T, _div(S, 256, 8),
                    [(x, "row", None), (g, "vec", None), (sc, "bvec", None), (dh, "row", None), (dres, "row", None)],
                    [("row", D, F32), ("bacc", D, F32), ("bacc", D, F32), ("bacc", D, F32)], S)


def _gate_grad(name, dx, y, gt, coeff, S):
    T, D = dx.shape

    def fn(dxv, yv, gtv):
        return [coeff * gtv * dxv, _colsum(coeff * dxv * yv.astype(F32))]

    return _rowwise(name, fn, T, _div(S, 512, 8), [(dx, "row", None), (y, "row", None), (gt, "bvec", None)],
                    [("row", D, BF16), ("bacc", D, F32)], S)


def _resident(shape):
    return pl.BlockSpec(shape, lambda i: (0, 0), pipeline_mode=pl.Buffered(1))


def _example_acc(r, i, tiles, v):
    @pl.when(i % tiles == 0)
    def _():
        r[...] = jnp.zeros_like(r)

    r[...] += v


def _norm_matmul(name, x, g, sc, sh, weights, epilogue, outs, S, vecs=()):
    T, D = x.shape
    tm = _div(S, 256, 8)
    tiles = S // tm
    nw, nv = len(weights), len(vecs)

    def body(*refs):
        x_ref, g_ref, sc_ref, sh_ref = refs[:4]
        w_refs, v_refs = refs[4:4 + nw], refs[4 + nw:4 + nw + nv]
        h_ref, out_refs = refs[4 + nw + nv], refs[5 + nw + nv:]
        xhat, _ = _rms_parts(x_ref[...])
        h = ((xhat * g_ref[...]) * (1.0 + sc_ref[...]) + sh_ref[...]).astype(BF16)
        h_ref[...] = h
        accs = [lax.dot_general(h, w[...], NT, preferred_element_type=F32) for w in w_refs]
        for r, o in zip(out_refs, epilogue(accs, *[v[...] for v in v_refs])):
            r[...] = o.astype(r.dtype)

    bvec = pl.BlockSpec((None, 1, D), lambda i: (i // tiles, 0, 0))
    return pl.pallas_call(
        body, name=name,
        out_shape=[jax.ShapeDtypeStruct((T, D), BF16)] + [jax.ShapeDtypeStruct((T, w), dt) for w, dt in outs],
        grid=(T // tm,),
        in_specs=[pl.BlockSpec((tm, D), lambda i: (i, 0)), pl.BlockSpec((1, D), lambda i: (0, 0)), bvec, bvec]
        + [_resident(w.shape) for w in weights] + [pl.BlockSpec(v.shape, lambda i: (0, 0)) for v in vecs],
        out_specs=[pl.BlockSpec((tm, D), lambda i: (i, 0))] + [pl.BlockSpec((tm, w), lambda i: (i, 0)) for w, _ in outs],
        compiler_params=_params(("arbitrary",)),
    )(x, g, sc, sh, *weights, *vecs)


def _gated_grad_matmul(name, dx, y, gt, coeff, w, tiles_in, epilogue, outs, S):
    T, D = dx.shape
    N = w.shape[0]
    tm = _div(S, 256, 8)
    tiles = S // tm
    nt = len(tiles_in)

    def body(*refs):
        dx_ref, y_ref, gt_ref, w_ref = refs[:4]
        t_refs, dy_ref, dgt_ref, out_refs = refs[4:4 + nt], refs[4 + nt], refs[5 + nt], refs[6 + nt:]
        i = pl.program_id(0)
        dxv = dx_ref[...]
        dy = (coeff * gt_ref[...] * dxv).astype(BF16)
        dy_ref[...] = dy
        _example_acc(dgt_ref, i, tiles, _colsum(coeff * dxv * y_ref[...].astype(F32)))
        acc = lax.dot_general(dy, w_ref[...], NT, preferred_element_type=F32)
        for r, o in zip(out_refs, epilogue(acc, [t[...] for t in t_refs])):
            r[...] = o.astype(r.dtype)

    row = pl.BlockSpec((tm, D), lambda i: (i, 0))
    bvec = pl.BlockSpec((None, 1, D), lambda i: (i // tiles, 0, 0))
    return pl.pallas_call(
        body, name=name,
        out_shape=[jax.ShapeDtypeStruct((T, D), BF16), jax.ShapeDtypeStruct((T // S, 1, D), F32)]
        + [jax.ShapeDtypeStruct((T, N), dt) for dt in outs],
        grid=(T // tm,),
        in_specs=[row, row, bvec, _resident(w.shape)]
        + [pl.BlockSpec((tm, N), functools.partial(lambda i, cb: (i, cb), cb=cb)) for _, cb in tiles_in],
        out_specs=[row, bvec] + [pl.BlockSpec((tm, N), lambda i: (i, 0)) for _ in outs],
        compiler_params=_params(("arbitrary",)),
    )(dx, y, gt, w, *[t for t, _ in tiles_in])


def _matmul_normmod_bwd(name, prods, x, g, sc, dres, S):
    T, D = x.shape
    tm = _div(S, 256, 8)
    tiles = S // tm
    npr = len(prods)

    def body(*refs):
        ab = refs[:2 * npr]
        x_ref, g_ref, sc_ref, dr_ref = refs[2 * npr:2 * npr + 4]
        dx_ref, dsh_ref, dsc_ref, dg_ref = refs[2 * npr + 4:]
        i = pl.program_id(0)
        dh = None
        for p in range(npr):
            d = lax.dot_general(ab[2 * p][...], ab[2 * p + 1][...], NN, preferred_element_type=F32)
            dh = d if dh is None else dh + d
        xhat, rstd = _rms_parts(x_ref[...])
        gv = g_ref[...]
        dn = dh * (1.0 + sc_ref[...])
        dxh = dn * gv
        dx_ref[...] = dr_ref[...] + rstd * (dxh - xhat * jnp.mean(dxh * xhat, axis=-1, keepdims=True))
        _example_acc(dsh_ref, i, tiles, _colsum(dh))
        _example_acc(dsc_ref, i, tiles, _colsum(dh * (xhat * gv)))
        _example_acc(dg_ref, i, tiles, _colsum(dn * xhat))

    row = pl.BlockSpec((tm, D), lambda i: (i, 0))
    bvec = pl.BlockSpec((None, 1, D), lambda i: (i // tiles, 0, 0))
    in_specs, operands = [], []
    for a, b in prods:
        in_specs += [pl.BlockSpec((tm, a.shape[1]), lambda i: (i, 0)), _resident(b.shape)]
        operands += [a, b]
    acc_shape = jax.ShapeDtypeStruct((T // S, 1, D), F32)
    return pl.pallas_call(
        body, name=name,
        out_shape=[jax.ShapeDtypeStruct((T, D), F32), acc_shape, acc_shape, acc_shape],
        grid=(T // tm,),
        in_specs=in_specs + [row, pl.BlockSpec((1, D), lambda i: (0, 0)), bvec, row],
        out_specs=[row, bvec, bvec, bvec],
        compiler_params=_params(("arbitrary",)),
    )(*operands, x, g, sc, dres)


def _ffn_forward(tag, x, g, sh, sc, gt, wgT, wuT, wd, S):
    T, D = x.shape
    F = wd.shape[0]

    def gateup(accs):
        a, u = accs
        return [a, u, a * _sigmoid(a) * u]

    h, a, u, s = _norm_matmul(f"{tag}_gateup", x, g, sc, sh, [wgT, wuT], gateup, [(F, BF16)] * 3, S)

    def down(accs, ex):
        xv, gtv = ex
        return [xv + 0.5 * gtv * accs[0], accs[0]]

    tmd = _div(S, 512, 8)
    x_new, y = _matmul(f"{tag}_down", "nn", [[(s, wd)]], T, D, F, tmd, D, F, [F32, BF16],
                       extras=[(x, "tile", 0), (gt, "brow", 0)], epilogue=down, rows_per_example=S)
    return x_new, (x, h, a, u, s, y)


def _ffn_backward(tag, dx_out, saved, g, sc, gt, wgT, wuT, wd, S):
    x, h, a, u, s, y = saved
    T, D = x.shape
    F = wd.shape[0]
    def act_grad(ds, ex):
        av, uv = ex[0].astype(F32), ex[1].astype(F32)
        sg = _sigmoid(av)
        return [ds * uv * (sg * (1.0 + av * (1.0 - sg))), ds * (av * sg)]

    dy, dgt, da, du = _gated_grad_matmul(f"{tag}_act_grad", dx_out, y, gt, 0.5, wd, [(a, 0), (u, 0)], act_grad, [BF16, BF16], S)
    tkw = _div(T, 1024, LANES)
    dwd = _matmul(f"{tag}_dw_down", "tn", [[(s, dy)]], F, D, T, F, D, tkw, [F32])[0]
    dwgT = _matmul(f"{tag}_dw_gate", "tn", [[(da, h)]], F, D, T, F, D, tkw, [F32])[0]
    dwuT = _matmul(f"{tag}_dw_up", "tn", [[(du, h)]], F, D, T, F, D, tkw, [F32])[0]
    dx_in, dsh, dsc, dg = _matmul_normmod_bwd(f"{tag}_dh", [(da, wgT), (du, wuT)], x, g, sc, dx_out, S)
    return dx_in, (dsh, dsc, dgt, dg), (dwgT, dwuT, dwd)


def _loss_head(x, tgt, g, S):
    T, D = x.shape

    def fn(xv, tv, gv):
        xhat, rstd = _rms_parts(xv)
        e = xhat * gv - tv
        loss = jnp.broadcast_to(0.5 / D * jnp.sum(_colsum(e * e), axis=1, keepdims=True), (1, LANES))
        dy = e * (1.0 / D)
        dxh = dy * gv
        dx = rstd * (dxh - xhat * jnp.mean(dxh * xhat, axis=-1, keepdims=True))
        return [dx, loss, _colsum(dy * xhat)]

    return _rowwise("loss_head", fn, T, _div(S, 512, 8), [(x, "row", None), (tgt, "row", None), (g, "vec", None)],
                    [("row", D, F32), ("bacc", LANES, F32), ("bacc", D, F32)], S)


def _cumsum(v):
    B, S, _ = v.shape
    rows = _div(S, 1024, BLOCK)

    def body(x_ref, o_ref, carry):
        i = pl.program_id(1)

        @pl.when(i == 0)
        def _():
            carry[...] = jnp.zeros_like(carry)

        r = lax.broadcasted_iota(jnp.int32, (BLOCK, BLOCK), 0)
        c = lax.broadcasted_iota(jnp.int32, (BLOCK, BLOCK), 1)
        tri = (c <= r).astype(F32)
        last = carry[0:1, :]
        for j in range(0, rows, BLOCK):
            cum = jnp.dot(tri, x_ref[j:j + BLOCK, :], precision=lax.Precision.HIGHEST, preferred_element_type=F32) + last
            o_ref[j:j + BLOCK, :] = cum
            last = cum[BLOCK - 1:BLOCK, :]
        carry[...] = jnp.broadcast_to(last, carry.shape)

    return pl.pallas_call(
        body, name="cumsum", out_shape=jax.ShapeDtypeStruct(v.shape, F32), grid=(B, S // rows),
        in_specs=[pl.BlockSpec((None, rows, LANES), lambda b, i: (b, i, 0))],
        out_specs=pl.BlockSpec((None, rows, LANES), lambda b, i: (b, i, 0)),
        scratch_shapes=[pltpu.VMEM((8, LANES), F32)],
        compiler_params=_params(("arbitrary", "arbitrary")),
    )(v)


def _fox_scores(q, k, cq, ck, qpos, kpos):
    s = lax.dot_general(q, k, (((1,), (1,)), ((), ())), preferred_element_type=F32) * SCALE + cq - ck
    return jnp.where(kpos <= qpos, s, NEG_INF)


def _fox_positions(qi, kj, tq, tk):
    qpos = qi * tq + lax.broadcasted_iota(jnp.int32, (tq, tk), 0)
    kpos = kj * tk + lax.broadcasted_iota(jnp.int32, (tq, tk), 1)
    return qpos, kpos


def _fox_forward(pm3, cum, cumT, qcol, kcol, vcol, tq):
    B, S, _ = pm3.shape
    nq = S // tq

    def body(q_ref, k_ref, v_ref, cq_ref, ck_ref, o_ref, o32_ref, lse_ref, m_sc, l_sc, acc_sc):
        qi, kj = pl.program_id(1), pl.program_id(2)

        @pl.when(kj == 0)
        def _():
            m_sc[...] = jnp.full_like(m_sc, NEG_INF)
            l_sc[...] = jnp.zeros_like(l_sc)
            acc_sc[...] = jnp.zeros_like(acc_sc)

        @pl.when(kj <= qi)
        def _():
            qpos, kpos = _fox_positions(qi, kj, tq, tq)
            for h in range(FOX_HEADS):
                hs = slice(HEAD_DIM * h, HEAD_DIM * (h + 1))
                s = _fox_scores(q_ref[:, hs], k_ref[:, hs], cq_ref[:, h:h + 1], ck_ref[h:h + 1, :], qpos, kpos)
                m_prev = m_sc[h]
                m_new = jnp.maximum(m_prev, jnp.max(s, axis=-1, keepdims=True))
                alpha = jnp.exp(m_prev - m_new)
                p = jnp.exp(s - m_new)
                l_sc[h] = alpha * l_sc[h] + jnp.sum(p, axis=-1, keepdims=True)
                acc_sc[:, hs] = alpha * acc_sc[:, hs] + lax.dot_general(
                    p.astype(BF16), v_ref[:, hs], (((1,), (0,)), ((), ())), preferred_element_type=F32)
                m_sc[h] = m_new

        @pl.when(kj == nq - 1)
        def _():
            lse_ref[...] = jnp.zeros_like(lse_ref)
            for h in range(FOX_HEADS):
                hs = slice(HEAD_DIM * h, HEAD_DIM * (h + 1))
                oh = acc_sc[:, hs] / l_sc[h]
                o_ref[:, hs] = oh.astype(o_ref.dtype)
                o32_ref[:, hs] = oh
                lse_ref[:, h:h + 1] = m_sc[h] + jnp.log(l_sc[h])

    return pl.pallas_call(
        body, name="fox_forward",
        out_shape=[jax.ShapeDtypeStruct((B, S, FOX_W), BF16), jax.ShapeDtypeStruct((B, S, FOX_W), F32),
                   jax.ShapeDtypeStruct((B, S, LANES), F32)],
        grid=(B, nq, nq),
        in_specs=[pl.BlockSpec((None, tq, FOX_W), lambda b, i, j: (b, i, qcol)),
                  pl.BlockSpec((None, tq, FOX_W), lambda b, i, j: (b, jnp.minimum(i, j), kcol)),
                  pl.BlockSpec((None, tq, FOX_W), lambda b, i, j: (b, jnp.minimum(i, j), vcol)),
                  pl.BlockSpec((None, tq, LANES), lambda b, i, j: (b, i, 0)),
                  pl.BlockSpec((None, 8, tq), lambda b, i, j: (b, 0, jnp.minimum(i, j)))],
        out_specs=[pl.BlockSpec((None, tq, FOX_W), lambda b, i, j: (b, i, 0)),
                   pl.BlockSpec((None, tq, FOX_W), lambda b, i, j: (b, i, 0)),
                   pl.BlockSpec((None, tq, LANES), lambda b, i, j: (b, i, 0))],
        scratch_shapes=[pltpu.VMEM((FOX_HEADS, tq, 1), F32), pltpu.VMEM((FOX_HEADS, tq, 1), F32), pltpu.VMEM((tq, FOX_W), F32)],
        compiler_params=_params(("parallel", "parallel", "arbitrary")),
    )(pm3, pm3, pm3, cum, cumT)


def _fox_dq(pm3, do, delta, lse, cum, cumT, qcol, kcol, vcol, tq):
    B, S, _ = pm3.shape
    nq = S // tq

    def body(q_ref, k_ref, v_ref, do_ref, dl_ref, lse_ref, cq_ref, ck_ref, dq_ref, dc_ref, acc_sc, dc_sc):
        qi, kj = pl.program_id(1), pl.program_id(2)

        @pl.when(kj == 0)
        def _():
            acc_sc[...] = jnp.zeros_like(acc_sc)
            dc_sc[...] = jnp.zeros_like(dc_sc)

        @pl.when(kj <= qi)
        def _():
            qpos, kpos = _fox_positions(qi, kj, tq, tq)
            for h in range(FOX_HEADS):
                hs = slice(HEAD_DIM * h, HEAD_DIM * (h + 1))
                s = _fox_scores(q_ref[:, hs], k_ref[:, hs], cq_ref[:, h:h + 1], ck_ref[h:h + 1, :], qpos, kpos)
                p = jnp.exp(s - lse_ref[:, h:h + 1])
                doh = do_ref[:, hs]
                dp = lax.dot_general(doh, v_ref[:, hs], (((1,), (1,)), ((), ())), preferred_element_type=F32)
                ds = p * (dp - dl_ref[:, h:h + 1])
                dc_sc[h] += jnp.sum(ds, axis=-1, keepdims=True)
                acc_sc[:, hs] += lax.dot_general(ds.astype(BF16), k_ref[:, hs], (((1,), (0,)), ((), ())),
                                                 preferred_element_type=F32)

        @pl.when(kj == nq - 1)
        def _():
            dq_ref[...] = (acc_sc[...] * SCALE).astype(dq_ref.dtype)
            dc_ref[...] = jnp.zeros_like(dc_ref)
            for h in range(FOX_HEADS):
                dc_ref[:, h:h + 1] = dc_sc[h]

    qspec = pl.BlockSpec((None, tq, FOX_W), lambda b, i, j: (b, i, 0))
    lspec = pl.BlockSpec((None, tq, LANES), lambda b, i, j: (b, i, 0))
    return pl.pallas_call(
        body, name="fox_dq",
        out_shape=[jax.ShapeDtypeStruct((B, S, FOX_W), BF16), jax.ShapeDtypeStruct((B, S, LANES), F32)],
        grid=(B, nq, nq),
        in_specs=[pl.BlockSpec((None, tq, FOX_W), lambda b, i, j: (b, i, qcol)),
                  pl.BlockSpec((None, tq, FOX_W), lambda b, i, j: (b, jnp.minimum(i, j), kcol)),
                  pl.BlockSpec((None, tq, FOX_W), lambda b, i, j: (b, jnp.minimum(i, j), vcol)),
                  qspec, lspec, lspec, lspec,
                  pl.BlockSpec((None, 8, tq), lambda b, i, j: (b, 0, jnp.minimum(i, j)))],
        out_specs=[qspec, lspec],
        scratch_shapes=[pltpu.VMEM((tq, FOX_W), F32), pltpu.VMEM((FOX_HEADS, tq, 1), F32)],
        compiler_params=_params(("parallel", "parallel", "arbitrary")),
    )(pm3, pm3, pm3, do, delta, lse, cum, cumT)


def _fox_dkv(pm3, do, delta, lse, cum, cumT, qcol, kcol, vcol, tq):
    B, S, _ = pm3.shape
    nq = S // tq

    def body(q_ref, k_ref, v_ref, do_ref, dl_ref, lse_ref, cq_ref, ck_ref, dk_ref, dv_ref, dc_ref, dk_sc, dv_sc, dc_sc):
        kj, qi = pl.program_id(1), pl.program_id(2)

        @pl.when(qi == 0)
        def _():
            dk_sc[...] = jnp.zeros_like(dk_sc)
            dv_sc[...] = jnp.zeros_like(dv_sc)
            dc_sc[...] = jnp.zeros_like(dc_sc)

        @pl.when(qi >= kj)
        def _():
            qpos, kpos = _fox_positions(qi, kj, tq, tq)
            for h in range(FOX_HEADS):
                hs = slice(HEAD_DIM * h, HEAD_DIM * (h + 1))
                qh = q_ref[:, hs]
                s = _fox_scores(qh, k_ref[:, hs], cq_ref[:, h:h + 1], ck_ref[h:h + 1, :], qpos, kpos)
                p = jnp.exp(s - lse_ref[:, h:h + 1])
                doh = do_ref[:, hs]
                dp = lax.dot_general(doh, v_ref[:, hs], (((1,), (1,)), ((), ())), preferred_element_type=F32)
                ds = p * (dp - dl_ref[:, h:h + 1])
                dv_sc[:, hs] += lax.dot_general(p.astype(BF16), doh, (((0,), (0,)), ((), ())), preferred_element_type=F32)
                dk_sc[:, hs] += lax.dot_general(ds.astype(BF16), qh, (((0,), (0,)), ((), ())), preferred_element_type=F32)
                dc_sc[h:h + 1, :] -= jnp.sum(ds, axis=0, keepdims=True)

        @pl.when(qi == nq - 1)
        def _():
            dk_ref[...] = (dk_sc[...] * SCALE).astype(dk_ref.dtype)
            dv_ref[...] = dv_sc[...].astype(dv_ref.dtype)
            dc_ref[...] = dc_sc[...]

    def qside(width):
        return pl.BlockSpec((None, tq, width), lambda b, j, i: (b, jnp.maximum(i, j), 0))

    kspec = pl.BlockSpec((None, tq, FOX_W), lambda b, j, i: (b, j, 0))
    return pl.pallas_call(
        body, name="fox_dkv",
        out_shape=[jax.ShapeDtypeStruct((B, S, FOX_W), BF16), jax.ShapeDtypeStruct((B, S, FOX_W), BF16),
                   jax.ShapeDtypeStruct((B, 8, S), F32)],
        grid=(B, nq, nq),
        in_specs=[pl.BlockSpec((None, tq, FOX_W), lambda b, j, i: (b, jnp.maximum(i, j), qcol)),
                  pl.BlockSpec((None, tq, FOX_W), lambda b, j, i: (b, j, kcol)),
                  pl.BlockSpec((None, tq, FOX_W), lambda b, j, i: (b, j, vcol)),
                  qside(FOX_W), qside(LANES), qside(LANES), qside(LANES),
                  pl.BlockSpec((None, 8, tq), lambda b, j, i: (b, 0, j))],
        out_specs=[kspec, kspec, pl.BlockSpec((None, 8, tq), lambda b, j, i: (b, 0, j))],
        scratch_shapes=[pltpu.VMEM((tq, FOX_W), F32), pltpu.VMEM((tq, FOX_W), F32), pltpu.VMEM((8, tq), F32)],
        compiler_params=_params(("parallel", "parallel", "arbitrary")),
    )(pm3, pm3, pm3, do, delta, lse, cum, cumT)


def _with_ones(x):
    lane = lax.broadcasted_iota(jnp.int32, (x.shape[0], HEAD_DIM), 1)
    return jnp.concatenate([x, jnp.where(lane == 0, 1.0, 0.0).astype(x.dtype)], axis=1)


def _causal_strip(s, r):
    qpos = r + lax.broadcasted_iota(jnp.int32, s.shape, 0)
    kpos = lax.broadcasted_iota(jnp.int32, s.shape, 1)
    return jnp.where(kpos <= qpos, s, NEG_INF)


NT = (((1,), (1,)), ((), ()))
NN = (((1,), (0,)), ((), ()))
TN = (((0,), (0,)), ((), ()))


def _fox_fwd(pm3, cumT, qcol, kcol, vcol, tq):
    B, S, _ = pm3.shape
    nq = S // tq
    strips = range(0, tq, FOX_STRIP)

    def body(q_ref, k_ref, v_ref, ck_ref, o_ref, o32_ref, lse_ref, s_sc, p_sc, al_sc, m_sc, acc_sc):
        qi, kj = pl.program_id(1), pl.program_id(2)

        @pl.when(kj == 0)
        def _():
            m_sc[...] = jnp.full_like(m_sc, NEG_INF)
            acc_sc[...] = jnp.zeros_like(acc_sc)

        def tile(diagonal):
            def scores(h):
                hs = slice(HEAD_DIM * h, HEAD_DIM * (h + 1))
                s_sc[h % 2] = lax.dot_general(q_ref[:, hs] * SCALE, k_ref[:, hs], NT, preferred_element_type=F32)

            def accumulate(h):
                hs = slice(HEAD_DIM * h, HEAD_DIM * (h + 1))
                acc_sc[h] = al_sc[h % 2] * acc_sc[h] + lax.dot_general(p_sc[h % 2], _with_ones(v_ref[:, hs]), NN,
                                                                       preferred_element_type=F32)

            scores(0)
            for h in range(FOX_HEADS):
                b = h % 2
                if h + 1 < FOX_HEADS:
                    scores(h + 1)
                if h >= 1:
                    accumulate(h - 1)
                ck = ck_ref[h:h + 1, :]
                for r in strips:
                    rows = slice(r, r + FOX_STRIP)
                    s = s_sc[b, rows, :] - ck
                    if diagonal:
                        s = _causal_strip(s, r)
                    m_prev = m_sc[h, rows, :]
                    m_new = jnp.maximum(m_prev, jnp.max(s, axis=-1, keepdims=True))
                    p_sc[b, rows, :] = jnp.exp(s - m_new).astype(BF16)
                    al_sc[b, rows, :] = jnp.exp(m_prev - m_new)
                    m_sc[h, rows, :] = m_new
            accumulate(FOX_HEADS - 1)

        @pl.when(kj < qi)
        def _():
            tile(False)

        @pl.when(kj == qi)
        def _():
            tile(True)

        @pl.when(kj == nq - 1)
        def _():
            lse_ref[...] = jnp.zeros_like(lse_ref)
            for h in range(FOX_HEADS):
                hs = slice(HEAD_DIM * h, HEAD_DIM * (h + 1))
                acc = acc_sc[h]
                l = acc[:, HEAD_DIM:HEAD_DIM + 1]
                oh = acc[:, :HEAD_DIM] / l
                o_ref[:, hs] = oh.astype(o_ref.dtype)
                o32_ref[:, hs] = oh
                lse_ref[:, h:h + 1] = m_sc[h] + jnp.log(l)

    ospec = pl.BlockSpec((None, tq, FOX_W), lambda b, i, j: (b, i, 0))
    return pl.pallas_call(
        body, name="fox_forward",
        out_shape=[jax.ShapeDtypeStruct((B, S, FOX_W), BF16), jax.ShapeDtypeStruct((B, S, FOX_W), F32),
                   jax.ShapeDtypeStruct((B, S, LANES), F32)],
        grid=(B, nq, nq),
        in_specs=[pl.BlockSpec((None, tq, FOX_W), lambda b, i, j: (b, i, qcol)),
                  pl.BlockSpec((None, tq, FOX_W), lambda b, i, j: (b, jnp.minimum(i, j), kcol)),
                  pl.BlockSpec((None, tq, FOX_W), lambda b, i, j: (b, jnp.minimum(i, j), vcol)),
                  pl.BlockSpec((None, 8, tq), lambda b, i, j: (b, 0, jnp.minimum(i, j)))],
        out_specs=[ospec, ospec, pl.BlockSpec((None, tq, LANES), lambda b, i, j: (b, i, 0))],
        scratch_shapes=[pltpu.VMEM((2, tq, tq), F32), pltpu.VMEM((2, tq, tq), BF16), pltpu.VMEM((2, tq, 1), F32),
                        pltpu.VMEM((FOX_HEADS, tq, 1), F32), pltpu.VMEM((FOX_HEADS, tq, LANES), F32)],
        compiler_params=_params(("parallel", "parallel", "arbitrary")),
    )(pm3, pm3, pm3, cumT)


def _fox_bwd(pm3, do, delta, lse, cumT, qcol, kcol, vcol, tq):
    B, S, _ = pm3.shape
    nq = S // tq
    strips = range(0, tq, FOX_STRIP)

    def body(q_ref, k_ref, v_ref, do_ref, dl_ref, lse_ref, ck_ref, dq_ref, rs_ref, dk_ref, dv_ref, cs_ref,
             s_sc, dp_sc, p_sc, ds_sc, dq_sc, dk_sc, dv_sc):
        kj, qi = pl.program_id(1), pl.program_id(2)

        @pl.when((kj == 0) & (qi == 0))
        def _():
            dq_sc[...] = jnp.zeros_like(dq_sc)

        @pl.when(qi == 0)
        def _():
            dk_sc[...] = jnp.zeros_like(dk_sc)
            dv_sc[...] = jnp.zeros_like(dv_sc)

        def tile(diagonal):
            qrows = pl.ds(pl.multiple_of(qi * tq, tq), tq)
            for h in range(FOX_HEADS):
                hs = slice(HEAD_DIM * h, HEAD_DIM * (h + 1))
                qh, kh, doh = q_ref[:, hs] * SCALE, k_ref[:, hs], do_ref[:, hs]
                b = h % 2
                s_sc[b] = lax.dot_general(qh, kh, NT, preferred_element_type=F32)
                dp_sc[b] = lax.dot_general(doh, v_ref[:, hs], NT, preferred_element_type=F32)
                ck = ck_ref[h:h + 1, :]
                for r in strips:
                    rows = slice(r, r + FOX_STRIP)
                    s = s_sc[b, rows, :] - ck
                    if diagonal:
                        s = _causal_strip(s, r)
                    p = jnp.exp(s - lse_ref[rows, h:h + 1])
                    p_sc[b, rows, :] = p.astype(BF16)
                    ds_sc[b, rows, :] = (p * (dp_sc[b, rows, :] - dl_ref[rows, h:h + 1])).astype(BF16)
                dv_sc[:, hs] += lax.dot_general(p_sc[b], doh, TN, preferred_element_type=F32)
                dk_sc[h] += lax.dot_general(ds_sc[b], _with_ones(qh), TN, preferred_element_type=F32)
                dq_sc[h, qrows, :] += lax.dot_general(ds_sc[b], _with_ones(kh), NN, preferred_element_type=F32)

        @pl.when(qi > kj)
        def _():
            tile(False)

        @pl.when(qi == kj)
        def _():
            tile(True)

        @pl.when(qi == nq - 1)
        def _():
            dv_ref[...] = dv_sc[...].astype(dv_ref.dtype)
            cs_ref[...] = jnp.zeros_like(cs_ref)
            for h in range(FOX_HEADS):
                hs = slice(HEAD_DIM * h, HEAD_DIM * (h + 1))
                dk = dk_sc[h]
                dk_ref[:, hs] = dk[:, :HEAD_DIM].astype(dk_ref.dtype)
                cs_ref[:, h:h + 1] = dk[:, HEAD_DIM:HEAD_DIM + 1]

        @pl.when((kj == nq - 1) & (qi == nq - 1))
        def _():
            rs_ref[...] = jnp.zeros_like(rs_ref)
            for h in range(FOX_HEADS):
                hs = slice(HEAD_DIM * h, HEAD_DIM * (h + 1))
                dq_ref[:, hs] = (dq_sc[h, :, :HEAD_DIM] * SCALE).astype(dq_ref.dtype)
                rs_ref[:, h:h + 1] = dq_sc[h, :, HEAD_DIM:HEAD_DIM + 1]

    def qside(width, col=0):
        return pl.BlockSpec((None, tq, width), lambda b, j, i: (b, jnp.maximum(i, j), col))

    kspec = pl.BlockSpec((None, tq, FOX_W), lambda b, j, i: (b, j, 0))
    return pl.pallas_call(
        body, name="fox_backward",
        out_shape=[jax.ShapeDtypeStruct((B, S, FOX_W), BF16), jax.ShapeDtypeStruct((B, S, LANES), F32),
                   jax.ShapeDtypeStruct((B, S, FOX_W), BF16), jax.ShapeDtypeStruct((B, S, FOX_W), BF16),
                   jax.ShapeDtypeStruct((B, S, LANES), F32)],
        grid=(B, nq, nq),
        in_specs=[qside(FOX_W, qcol),
                  pl.BlockSpec((None, tq, FOX_W), lambda b, j, i: (b, j, kcol)),
                  pl.BlockSpec((None, tq, FOX_W), lambda b, j, i: (b, j, vcol)),
                  qside(FOX_W), qside(LANES), qside(LANES),
                  pl.BlockSpec((None, 8, tq), lambda b, j, i: (b, 0, j))],
        out_specs=[pl.BlockSpec((None, S, FOX_W), lambda b, j, i: (b, 0, 0)),
                   pl.BlockSpec((None, S, LANES), lambda b, j, i: (b, 0, 0)),
                   kspec, kspec, pl.BlockSpec((None, tq, LANES), lambda b, j, i: (b, j, 0))],
        scratch_shapes=[pltpu.VMEM((2, tq, tq), F32), pltpu.VMEM((2, tq, tq), F32), pltpu.VMEM((2, tq, tq), BF16),
                        pltpu.VMEM((2, tq, tq), BF16), pltpu.VMEM((FOX_HEADS, S, LANES), F32),
                        pltpu.VMEM((FOX_HEADS, tq, LANES), F32), pltpu.VMEM((tq, FOX_W), F32)],
        compiler_params=_params(("parallel", "arbitrary", "arbitrary")),
    )(pm3, pm3, pm3, do, delta, lse, cumT)


def _fox_delta(do, o32, T, S):
    def fn(dov, ov):
        prod = dov.astype(F32) * ov
        lane = lax.broadcasted_iota(jnp.int32, (dov.shape[0], LANES), 1)
        delta = jnp.zeros((dov.shape[0], LANES), F32)
        for h in range(FOX_HEADS):
            hs = slice(HEAD_DIM * h, HEAD_DIM * (h + 1))
            delta = jnp.where(lane == h, jnp.sum(prod[:, hs], axis=-1, keepdims=True), delta)
        return [delta]

    return _rowwise("fox_delta", fn, T, _div(S, 512, 8), [(do, "row", None), (o32, "row", None)], [("row", LANES, F32)], S)[0]


def _alibi_slope(group, head):
    return 2.0 ** (-ALIBI_MAX_BIAS * (group * DIL_HPG + head + 1) / (N_DIL * DIL_HPG))


def _dil_tiles(q, k_cur, k_prev, slope, dilation, has_prev):
    qi = lax.broadcasted_iota(jnp.int32, (BLOCK, BLOCK), 0)
    ki = lax.broadcasted_iota(jnp.int32, (BLOCK, BLOCK), 1)
    rel = (qi - ki).astype(F32)
    nt = (((1,), (1,)), ((), ()))
    s_cur = lax.dot_general(q, k_cur, nt, preferred_element_type=F32) * SCALE - (slope * dilation) * rel
    s_cur = jnp.where(ki <= qi, s_cur, NEG_INF)
    s_prev = lax.dot_general(q, k_prev, nt, preferred_element_type=F32) * SCALE - (slope * dilation) * (rel + BLOCK)
    s_prev = jnp.where((ki >= qi) & has_prev, s_prev, NEG_INF)
    return s_cur, s_prev


def _dil_forward(group, pmv, nmb, qa_blk, B, S):
    _, dilation = DIL_GROUPS[group]
    sub = S // dilation
    nb = sub // BLOCK
    qb, kb, vb = qa_blk + group, qa_blk + N_DIL + group, qa_blk + 2 * N_DIL + group

    def body(q_ref, kc_ref, kp_ref, vc_ref, vp_ref, o_ref, lse_ref):
        has_prev = pl.program_id(2) > 0
        lse_ref[...] = jnp.zeros_like(lse_ref)
        for h in range(DIL_HPG):
            hs = slice(HEAD_DIM * h, HEAD_DIM * (h + 1))
            s_cur, s_prev = _dil_tiles(q_ref[:, hs], kc_ref[:, hs], kp_ref[:, hs], _alibi_slope(group, h), dilation, has_prev)
            m = jnp.maximum(jnp.max(s_cur, axis=-1, keepdims=True), jnp.max(s_prev, axis=-1, keepdims=True))
            p_cur, p_prev = jnp.exp(s_cur - m), jnp.exp(s_prev - m)
            l = jnp.sum(p_cur, axis=-1, keepdims=True) + jnp.sum(p_prev, axis=-1, keepdims=True)
            nn = (((1,), (0,)), ((), ()))
            o = (lax.dot_general(p_cur.astype(BF16), vc_ref[:, hs], nn, preferred_element_type=F32)
                 + lax.dot_general(p_prev.astype(BF16), vp_ref[:, hs], nn, preferred_element_type=F32))
            o_ref[:, hs] = o / l
            lse_ref[:, h:h + 1] = m + jnp.log(l)

    def cur(col):
        return pl.BlockSpec((None, BLOCK, DIL_GW), lambda b, r, n: (b, n, r * nmb + col))

    def prev(col):
        return pl.BlockSpec((None, BLOCK, DIL_GW), lambda b, r, n: (b, jnp.maximum(n - 1, 0), r * nmb + col))

    return pl.pallas_call(
        body, name=f"dil_forward_{group}",
        out_shape=[jax.ShapeDtypeStruct((B, sub, dilation * DIL_GW), F32), jax.ShapeDtypeStruct((B, sub, dilation * LANES), F32)],
        grid=(B, dilation, nb),
        in_specs=[cur(qb), cur(kb), prev(kb), cur(vb), prev(vb)],
        out_specs=[pl.BlockSpec((None, BLOCK, DIL_GW), lambda b, r, n: (b, n, r)),
                   pl.BlockSpec((None, BLOCK, LANES), lambda b, r, n: (b, n, r))],
        compiler_params=_params(("parallel", "parallel", "arbitrary")),
    )(pmv, pmv, pmv, pmv, pmv)


def _residue_order(a, B, S, d):
    C = a.shape[-1]
    if d == 1:
        return a.reshape(B, S, C)
    return a.reshape(B, S // d, d, C).transpose(0, 2, 1, 3).reshape(B * d, S // d, C)


def _token_order(a, B, S, d):
    C = a.shape[-1]
    if d == 1:
        return a.reshape(B * S, C)
    return a.reshape(B, d, S // d, C).transpose(0, 2, 1, 3).reshape(B * S, C)


def _band_scores(qh, kcat, slope_d, has_prev):
    qi = lax.broadcasted_iota(jnp.int32, (BLOCK, 2 * BLOCK), 0)
    c = lax.broadcasted_iota(jnp.int32, (BLOCK, 2 * BLOCK), 1)
    s = lax.dot_general(qh, kcat, NT, preferred_element_type=F32) - slope_d * (BLOCK + qi - c).astype(F32)
    valid = (c >= qi) & (c <= qi + BLOCK)
    if has_prev is not None:
        valid = valid & ((c >= BLOCK) | has_prev)
    return jnp.where(valid, s, NEG_INF)


def _band_operands(j, cur_ref, prev_ref, hs):
    if j == 0:
        return jnp.concatenate([prev_ref[:, hs], cur_ref[0:BLOCK, hs]], axis=0)
    return cur_ref[(j - 1) * BLOCK:(j + 1) * BLOCK, hs]


def _dil_specs(Ls, qb, cols):
    nsub = qb // BLOCK
    qcol, kcol, vcol = cols

    def cur(col):
        return pl.BlockSpec((None, qb, DIL_GW), lambda s, n: (s, n, col))

    def prev(col):
        return pl.BlockSpec((None, BLOCK, DIL_GW), lambda s, n: (s, jnp.maximum(n * nsub - 1, 0), col))

    return [cur(qcol), cur(kcol), prev(kcol), cur(vcol), prev(vcol)]


def _dil_fwd(group, src, cols):
    _, dilation = DIL_GROUPS[group]
    nseq, Ls, _ = src.shape
    qb = _div(Ls, 512, BLOCK)
    nsub = qb // BLOCK

    def body(q_ref, kc_ref, kp_ref, vc_ref, vp_ref, o_ref, lse_ref):
        has_prev = pl.program_id(1) > 0
        lse_ref[...] = jnp.zeros_like(lse_ref)
        for h in range(DIL_HPG):
            hs = slice(HEAD_DIM * h, HEAD_DIM * (h + 1))
            for j in range(nsub):
                rows = slice(j * BLOCK, (j + 1) * BLOCK)
                s = _band_scores(q_ref[rows, hs] * SCALE, _band_operands(j, kc_ref, kp_ref, hs),
                                 _alibi_slope(group, h) * dilation, has_prev if j == 0 else None)
                m = jnp.max(s, axis=-1, keepdims=True)
                p = jnp.exp(s - m).astype(BF16)
                acc = lax.dot_general(p, _with_ones(_band_operands(j, vc_ref, vp_ref, hs)), NN, preferred_element_type=F32)
                l = acc[:, HEAD_DIM:HEAD_DIM + 1]
                o_ref[rows, hs] = acc[:, :HEAD_DIM] / l
                lse_ref[rows, h:h + 1] = m + jnp.log(l)

    return pl.pallas_call(
        body, name=f"dil_forward_{group}",
        out_shape=[jax.ShapeDtypeStruct((nseq, Ls, DIL_GW), F32), jax.ShapeDtypeStruct((nseq, Ls, LANES), F32)],
        grid=(nseq, Ls // qb),
        in_specs=_dil_specs(Ls, qb, cols),
        out_specs=[pl.BlockSpec((None, qb, DIL_GW), lambda s, n: (s, n, 0)),
                   pl.BlockSpec((None, qb, LANES), lambda s, n: (s, n, 0))],
        compiler_params=_params(("parallel", "arbitrary")),
    )(src, src, src, src, src)


def _dil_bwd(group, src, cols, Lr, dyr, dlr):
    _, dilation = DIL_GROUPS[group]
    nseq, Ls, _ = src.shape
    qb = _div(Ls, 512, BLOCK)
    nsub, nb = qb // BLOCK, Ls // qb

    def body(q_ref, kc_ref, kp_ref, vc_ref, vp_ref, L_ref, dy_ref, dl_ref, dq_ref, dk_ref, dv_ref, dk_sc, dv_sc):
        n = pl.program_id(1)
        has_prev = n > 0

        @pl.when(n == 0)
        def _():
            dk_sc[...] = jnp.zeros_like(dk_sc)
            dv_sc[...] = jnp.zeros_like(dv_sc)

        base = pl.multiple_of(n * qb, BLOCK)
        for h in range(DIL_HPG):
            hs = slice(HEAD_DIM * h, HEAD_DIM * (h + 1))
            for j in range(nsub):
                rows = slice(j * BLOCK, (j + 1) * BLOCK)
                qh = q_ref[rows, hs] * SCALE
                kcat = _band_operands(j, kc_ref, kp_ref, hs)
                s = _band_scores(qh, kcat, _alibi_slope(group, h) * dilation, has_prev if j == 0 else None)
                p = jnp.exp(s - L_ref[rows, h:h + 1])
                dyh = dy_ref[rows, hs]
                dp = lax.dot_general(dyh, _band_operands(j, vc_ref, vp_ref, hs), NT, preferred_element_type=F32)
                ds = (p * (dp - dl_ref[rows, h:h + 1])).astype(BF16)
                dq_ref[rows, hs] = (lax.dot_general(ds, kcat, NN, preferred_element_type=F32) * SCALE).astype(dq_ref.dtype)
                win = pl.ds(base + j * BLOCK, 2 * BLOCK)
                dk_sc[win, hs] += lax.dot_general(ds, qh, TN, preferred_element_type=F32)
                dv_sc[win, hs] += lax.dot_general(p.astype(BF16), dyh, TN, preferred_element_type=F32)

        @pl.when(n == nb - 1)
        def _():
            dk_ref[...] = dk_sc[BLOCK:, :].astype(dk_ref.dtype)
            dv_ref[...] = dv_sc[BLOCK:, :].astype(dv_ref.dtype)

    own = pl.BlockSpec((None, qb, DIL_GW), lambda s, n: (s, n, 0))
    own128 = pl.BlockSpec((None, qb, LANES), lambda s, n: (s, n, 0))
    whole = pl.BlockSpec((None, Ls, DIL_GW), lambda s, n: (s, 0, 0))
    shape = jax.ShapeDtypeStruct((nseq, Ls, DIL_GW), BF16)
    return pl.pallas_call(
        body, name=f"dil_backward_{group}",
        out_shape=[shape, shape, shape],
        grid=(nseq, nb),
        in_specs=_dil_specs(Ls, qb, cols) + [own128, own, own128],
        out_specs=[own, whole, whole],
        scratch_shapes=[pltpu.VMEM((Ls + BLOCK, DIL_GW), F32), pltpu.VMEM((Ls + BLOCK, DIL_GW), F32)],
        compiler_params=_params(("parallel", "arbitrary")),
    )(src, src, src, src, src, Lr, dyr, dlr)


def _dil_combine(os_, lses, T, S):
    def fn(o0, o1, o2, l0, l1, l2):
        m = jnp.maximum(jnp.maximum(l0, l1), l2)
        e0, e1, e2 = jnp.exp(l0 - m), jnp.exp(l1 - m), jnp.exp(l2 - m)
        tot = e0 + e1 + e2
        w0, w1, w2 = e0 / tot, e1 / tot, e2 / tot
        parts = []
        for h in range(DIL_HPG):
            hs = slice(HEAD_DIM * h, HEAD_DIM * (h + 1))
            parts.append(w0[:, h:h + 1] * o0[:, hs] + w1[:, h:h + 1] * o1[:, hs] + w2[:, h:h + 1] * o2[:, hs])
        return [jnp.concatenate(parts, axis=1), m + jnp.log(tot)]

    ins = [(a, "row", None) for a in os_] + [(a, "row", None) for a in lses]
    return _rowwise("dil_combine", fn, T, _div(S, 512, 8), ins, [("row", DIL_GW, BF16), ("row", LANES, F32)], S)


def _dil_delta(dy, y, T, S):
    def fn(dyv, yv):
        prod = dyv * yv.astype(F32)
        lane = lax.broadcasted_iota(jnp.int32, (dyv.shape[0], LANES), 1)
        delta = jnp.zeros((dyv.shape[0], LANES), F32)
        for h in range(DIL_HPG):
            hs = slice(HEAD_DIM * h, HEAD_DIM * (h + 1))
            delta = jnp.where(lane == h, jnp.sum(prod[:, hs], axis=-1, keepdims=True), delta)
        return [delta, dyv]

    return _rowwise("dil_delta", fn, T, _div(S, 512, 8), [(dy, "row", None), (y, "row", None)],
                    [("row", LANES, F32), ("row", DIL_GW, BF16)], S)


def _dil_dq(group, pmv, nmb, qa_blk, Lv, dyv, deltav, B, S):
    _, dilation = DIL_GROUPS[group]
    sub = S // dilation
    nb = sub // BLOCK
    qb, kb, vb = qa_blk + group, qa_blk + N_DIL + group, qa_blk + 2 * N_DIL + group

    def body(q_ref, kc_ref, kp_ref, vc_ref, vp_ref, L_ref, dy_ref, dl_ref, dq_ref):
        has_prev = pl.program_id(2) > 0
        nt = (((1,), (1,)), ((), ()))
        nn = (((1,), (0,)), ((), ()))
        for h in range(DIL_HPG):
            hs = slice(HEAD_DIM * h, HEAD_DIM * (h + 1))
            s_cur, s_prev = _dil_tiles(q_ref[:, hs], kc_ref[:, hs], kp_ref[:, hs], _alibi_slope(group, h), dilation, has_prev)
            L, delta, dyh = L_ref[:, h:h + 1], dl_ref[:, h:h + 1], dy_ref[:, hs]
            ds_cur = jnp.exp(s_cur - L) * (lax.dot_general(dyh, vc_ref[:, hs], nt, preferred_element_type=F32) - delta)
            ds_prev = jnp.exp(s_prev - L) * (lax.dot_general(dyh, vp_ref[:, hs], nt, preferred_element_type=F32) - delta)
            dq = (lax.dot_general(ds_cur.astype(BF16), kc_ref[:, hs], nn, preferred_element_type=F32)
                  + lax.dot_general(ds_prev.astype(BF16), kp_ref[:, hs], nn, preferred_element_type=F32))
            dq_ref[:, hs] = (dq * SCALE).astype(dq_ref.dtype)

    def cur(col):
        return pl.BlockSpec((None, BLOCK, DIL_GW), lambda b, r, n: (b, n, r * nmb + col))

    def prev(col):
        return pl.BlockSpec((None, BLOCK, DIL_GW), lambda b, r, n: (b, jnp.maximum(n - 1, 0), r * nmb + col))

    own = pl.BlockSpec((None, BLOCK, DIL_GW), lambda b, r, n: (b, n, r))
    own128 = pl.BlockSpec((None, BLOCK, LANES), lambda b, r, n: (b, n, r))
    return pl.pallas_call(
        body, name=f"dil_dq_{group}",
        out_shape=jax.ShapeDtypeStruct((B, sub, dilation * DIL_GW), BF16),
        grid=(B, dilation, nb),
        in_specs=[cur(qb), cur(kb), prev(kb), cur(vb), prev(vb), own128, own, own128],
        out_specs=own,
        compiler_params=_params(("parallel", "parallel", "arbitrary")),
    )(pmv, pmv, pmv, pmv, pmv, Lv, dyv, deltav)


def _dil_dkv(group, pmv, nmb, qa_blk, Lv, dyv, deltav, B, S):
    _, dilation = DIL_GROUPS[group]
    sub = S // dilation
    nb = sub // BLOCK
    qb, kb, vb = qa_blk + group, qa_blk + N_DIL + group, qa_blk + 2 * N_DIL + group

    def body(k_ref, v_ref, q0_ref, q1_ref, L0_ref, L1_ref, dy0_ref, dy1_ref, dl0_ref, dl1_ref, dk_ref, dv_ref):
        has_next = pl.program_id(2) < nb - 1
        qi = lax.broadcasted_iota(jnp.int32, (BLOCK, BLOCK), 0)
        ki = lax.broadcasted_iota(jnp.int32, (BLOCK, BLOCK), 1)
        rel = (qi - ki).astype(F32)
        nt = (((1,), (1,)), ((), ()))
        tn = (((0,), (0,)), ((), ()))
        for h in range(DIL_HPG):
            hs = slice(HEAD_DIM * h, HEAD_DIM * (h + 1))
            bias = _alibi_slope(group, h) * dilation
            kh, vh, q0, q1 = k_ref[:, hs], v_ref[:, hs], q0_ref[:, hs], q1_ref[:, hs]
            s0 = lax.dot_general(q0, kh, nt, preferred_element_type=F32) * SCALE - bias * rel
            s0 = jnp.where(ki <= qi, s0, NEG_INF)
            s1 = lax.dot_general(q1, kh, nt, preferred_element_type=F32) * SCALE - bias * (rel + BLOCK)
            s1 = jnp.where((ki >= qi) & has_next, s1, NEG_INF)
            p0 = jnp.exp(s0 - L0_ref[:, h:h + 1])
            p1 = jnp.exp(s1 - L1_ref[:, h:h + 1])
            dy0, dy1 = dy0_ref[:, hs], dy1_ref[:, hs]
            ds0 = p0 * (lax.dot_general(dy0, vh, nt, preferred_element_type=F32) - dl0_ref[:, h:h + 1])
            ds1 = p1 * (lax.dot_general(dy1, vh, nt, preferred_element_type=F32) - dl1_ref[:, h:h + 1])
            dv = (lax.dot_general(p0.astype(BF16), dy0, tn, preferred_element_type=F32)
                  + lax.dot_general(p1.astype(BF16), dy1, tn, preferred_element_type=F32))
            dk = (lax.dot_general(ds0.astype(BF16), q0, tn, preferred_element_type=F32)
                  + lax.dot_general(ds1.astype(BF16), q1, tn, preferred_element_type=F32))
            dv_ref[:, hs] = dv.astype(dv_ref.dtype)
            dk_ref[:, hs] = (dk * SCALE).astype(dk_ref.dtype)

    def cur(col):
        return pl.BlockSpec((None, BLOCK, DIL_GW), lambda b, r, n: (b, n, r * nmb + col))

    def nxt(col):
        return pl.BlockSpec((None, BLOCK, DIL_GW), lambda b, r, n: (b, jnp.minimum(n + 1, nb - 1), r * nmb + col))

    own = pl.BlockSpec((None, BLOCK, DIL_GW), lambda b, r, n: (b, n, r))
    own_next = pl.BlockSpec((None, BLOCK, DIL_GW), lambda b, r, n: (b, jnp.minimum(n + 1, nb - 1), r))
    own128 = pl.BlockSpec((None, BLOCK, LANES), lambda b, r, n: (b, n, r))
    own128_next = pl.BlockSpec((None, BLOCK, LANES), lambda b, r, n: (b, jnp.minimum(n + 1, nb - 1), r))
    shape = jax.ShapeDtypeStruct((B, sub, dilation * DIL_GW), BF16)
    return pl.pallas_call(
        body, name=f"dil_dkv_{group}",
        out_shape=[shape, shape],
        grid=(B, dilation, nb),
        in_specs=[cur(kb), cur(vb), cur(qb), nxt(qb), own128, own128_next, own, own_next, own128, own128_next],
        out_specs=[own, own],
        compiler_params=_params(("parallel", "parallel", "arbitrary")),
    )(pmv, pmv, pmv, pmv, Lv, Lv, dyv, dyv, deltav, deltav)


def _ada_forward(c_all, w, b):
    n, D = c_all.shape
    cl = w.shape[1]

    def body(c_ref, w_ref, b_ref, o_ref, ca_ref):
        cv = c_ref[...]
        ca = (cv * _sigmoid(cv)).astype(BF16)
        ca_ref[...] = ca
        o_ref[...] = jnp.dot(ca, w_ref[...].astype(BF16), preferred_element_type=F32) + b_ref[...]

    return pl.pallas_call(
        body, name="ada_forward",
        out_shape=[jax.ShapeDtypeStruct((n, cl), F32), jax.ShapeDtypeStruct((n, D), BF16)],
        compiler_params=_params(),
    )(c_all, w, b)


def _ada_backward(ca, dmod_cols, dmod_all):
    n, D = ca.shape
    cl = dmod_cols.shape[1]

    def body(ca_ref, dc_ref, da_ref, gw_ref, gb_ref):
        gw_ref[...] = lax.dot_general(ca_ref[...], dc_ref[...].astype(BF16), (((0,), (0,)), ((), ())), preferred_element_type=F32)
        gb_ref[...] = _colsum(da_ref[...])

    return pl.pallas_call(
        body, name="ada_backward",
        out_shape=[jax.ShapeDtypeStruct((D, cl), F32), jax.ShapeDtypeStruct((1, dmod_all.shape[1]), F32)],
        compiler_params=_params(),
    )(ca, dmod_cols, dmod_all)


def _sum_devices(v):
    def body(v_ref, o_ref):
        tot = v_ref[0]
        for k in range(1, N_DEV):
            tot = tot + v_ref[k]
        o_ref[...] = tot

    return pl.pallas_call(body, name="sum_devices", out_shape=jax.ShapeDtypeStruct(v.shape[1:], F32))(v)


def _adamw(name, w, g, m, v):
    rows, cols = w.shape
    tr = _div(rows, 256, 8)

    def body(w_ref, g_ref, m_ref, v_ref, d_ref, nm_ref, nv_ref):
        gv = g_ref[...]
        nm = ADAM_B1 * m_ref[...] + (1.0 - ADAM_B1) * gv
        nv = ADAM_B2 * v_ref[...] + (1.0 - ADAM_B2) * (gv * gv)
        m_hat = nm / (1.0 - ADAM_B1 ** ADAM_STEP)
        v_hat = nv / (1.0 - ADAM_B2 ** ADAM_STEP)
        d_ref[...] = -ADAM_LR * (m_hat / (jnp.sqrt(v_hat) + ADAM_EPS) + ADAM_WD * w_ref[...])
        nm_ref[...] = nm
        nv_ref[...] = nv

    spec = pl.BlockSpec((tr, cols), lambda i: (i, 0))
    shape = jax.ShapeDtypeStruct((rows, cols), F32)
    return pl.pallas_call(
        body, name=name, out_shape=[shape, shape, shape], grid=(rows // tr,),
        in_specs=[spec, spec, spec, spec], out_specs=[spec, spec, spec],
        compiler_params=_params(("arbitrary",)),
    )(w, g, m, v)


def _pad_rows(a, rows):
    return a if a.shape[0] == rows else jnp.pad(a, ((0, rows - a.shape[0]), (0, 0)))


class _Packed:
    def __init__(self, kind, local_shape, D):
        self.kind, self.local_shape, self.D = kind, local_shape, D
        r, c = local_shape
        self.rows = {"T": c, "N": r, "F": r * c // D}[kind]
        self.rows_pad = -(-self.rows // ROW_ALIGN) * ROW_ALIGN

    def pack_local(self, w):
        if self.kind == "T":
            w = w.T
        elif self.kind == "F":
            w = w.reshape(self.rows, self.D)
        return _pad_rows(w, self.rows_pad)

    def full(self, gathered):
        g = gathered[:, :self.rows]
        if self.kind == "F":
            r, c = self.local_shape
            return g.reshape(N_DEV, r, c).transpose(1, 0, 2).reshape(r, N_DEV * c)
        return g.reshape(N_DEV * self.rows, self.D)

    def pack_grad(self, gfull):
        if self.kind == "F":
            r, c = self.local_shape
            g = gfull.reshape(r, N_DEV, c).transpose(1, 0, 2).reshape(N_DEV, self.rows, self.D)
        else:
            g = gfull.reshape(N_DEV, self.rows, self.D)
        if self.rows_pad != self.rows:
            g = jnp.pad(g, ((0, 0), (0, self.rows_pad - self.rows), (0, 0)))
        return g

    def unpack_local(self, g):
        g = g[:self.rows]
        if self.kind == "T":
            return g.T
        if self.kind == "F":
            return g.reshape(self.local_shape)
        return g


BIG = ["ffn1_w_gate", "ffn1_w_up", "ffn1_w_down", "w_in", "w_branch_a", "w_branch_b", "w_out",
       "ffn2_w_gate", "ffn2_w_up", "ffn2_w_down"]
BIG_KIND = {"ffn1_w_gate": "T", "ffn1_w_up": "T", "ffn1_w_down": "N", "w_in": "T", "w_branch_a": "F", "w_branch_b": "F",
            "w_out": "N", "ffn2_w_gate": "T", "ffn2_w_up": "T", "ffn2_w_down": "N"}
SMALL = ["ada_b", "norm_ffn1", "norm_mix", "forget_bias", "norm_ffn2", "norm_final"]


def kernel(x, c, ada_w, ada_b, norm_ffn1, ffn1_w_gate, ffn1_w_up, ffn1_w_down, norm_mix, w_in, forget_bias, w_branch_a, w_branch_b, w_out, norm_ffn2, ffn2_w_gate, ffn2_w_up, ffn2_w_down, norm_final, loss_target, m_ada_w, m_ada_b, m_norm_ffn1, m_ffn1_w_gate, m_ffn1_w_up, m_ffn1_w_down, m_norm_mix, m_w_in, m_forget_bias, m_w_branch_a, m_w_branch_b, m_w_out, m_norm_ffn2, m_ffn2_w_gate, m_ffn2_w_up, m_ffn2_w_down, m_norm_final, v_ada_w, v_ada_b, v_norm_ffn1, v_ffn1_w_gate, v_ffn1_w_up, v_ffn1_w_down, v_norm_mix, v_w_in, v_forget_bias, v_w_branch_a, v_w_branch_b, v_w_out, v_norm_ffn2, v_ffn2_w_gate, v_ffn2_w_up, v_ffn2_w_down, v_norm_final):
    args = dict(locals())
    B, S, D = x.shape
    T = B * S
    cl = ada_w.shape[2]
    n_in = w_in.shape[2] * N_DEV
    nm = 2 * D + 3 * FOX_W + 3 * DIL_W
    nmp = -(-nm // 512) * 512
    GA, GB, QB, QA = 0, D, 2 * D, 2 * D + 3 * FOX_W
    xpos, ypos, cpos = _position()
    me = 4 * xpos + 2 * ypos + cpos

    packs = {n: _Packed(BIG_KIND[n], args[n].shape[1:], D) for n in BIG}
    offs, r = {}, 0
    for n in BIG:
        offs[n] = r
        r += packs[n].rows_pad
    pad_rows = -r % PACK_ROW_QUANTUM
    p_local = jnp.concatenate([packs[n].pack_local(args[n][0]).astype(BF16) for n in BIG]
                              + [jnp.zeros((pad_rows, D), BF16)], axis=0)
    gathered = _weight_allgather(p_local)
    W = {n: packs[n].full(gathered[:, offs[n]:offs[n] + packs[n].rows_pad]) for n in BIG}
    winT = W["w_in"]
    o_f = 3 * DIL_W + 3 * FOX_W
    wmT = jnp.concatenate([winT[o_f + 8:], winT[3 * DIL_W:o_f], winT[:3 * DIL_W], jnp.zeros((nmp - nm, D), BF16)], axis=0)
    wfT = jnp.concatenate([winT[o_f:o_f + 8], jnp.zeros((LANES - 8, D), BF16)], axis=0)

    c_all = _small_allgather(c, "gather_c").reshape(N_DEV * B, D)
    b_cols = lax.dynamic_slice(ada_b, (0, me * cl), (1, cl))
    mod_cols, c_act = _ada_forward(c_all, ada_w[0], b_cols)
    mod_all = _small_allgather(mod_cols, "gather_mod").transpose(1, 0, 2).reshape(N_DEV * B, N_MOD * D)
    mod = lax.dynamic_slice(mod_all, (me * B, 0), (B, N_MOD * D)).reshape(B, N_MOD, 1, D)
    sh1, sc1, gt1, sh2, sc2, gt2, sh3, sc3, gt3 = [mod[:, i] for i in range(N_MOD)]

    x0 = x.reshape(T, D)
    x1, saved1 = _ffn_forward("ffn1", x0, norm_ffn1, sh1, sc1, gt1, W["ffn1_w_gate"], W["ffn1_w_up"], W["ffn1_w_down"], S)

    tm1k = _div(T, 1024, 8)
    fb = jnp.pad(forget_bias, ((0, 0), (0, LANES - FOX_HEADS)))

    def proj(accs, fbv):
        fl = accs[1] + fbv
        lane = lax.broadcasted_iota(jnp.int32, fl.shape, 1)
        ls = jnp.minimum(fl, 0.0) - jnp.log(1.0 + jnp.exp(-jnp.abs(fl)))
        return [accs[0], jnp.where(lane < FOX_HEADS, ls, 0.0), fl]

    tms = _div(S, 512, 8)
    h2, pm, logsig, flog = _norm_matmul("mix_proj", x1, norm_mix, sc2, sh2, [wmT, wfT], proj,
                                        [(nmp, BF16), (LANES, F32), (LANES, F32)], S, vecs=[fb])
    cum = _cumsum(logsig.reshape(B, S, LANES))
    cumT = cum[:, :, :8].transpose(0, 2, 1)
    pm3 = pm.reshape(B, S, nmp)
    tq = _div(S, 512, LANES)
    qcol, kcol, vcol = QB // FOX_W, QB // FOX_W + 1, QB // FOX_W + 2
    o_b, o_b32, lse_b = _fox_fwd(pm3, cumT, qcol, kcol, vcol, tq)
    y_b = o_b.reshape(T, FOX_W)

    qa_blk = QA // DIL_GW
    dil_src, dil_cols = [], []
    for g, (_, d) in enumerate(DIL_GROUPS):
        if d == 1:
            dil_src.append(pm3)
            dil_cols.append((qa_blk + g, qa_blk + N_DIL + g, qa_blk + 2 * N_DIL + g))
        else:
            starts = [QA + (i * N_DIL + g) * DIL_GW for i in range(3)]
            qkv = jnp.concatenate([pm[:, c:c + DIL_GW] for c in starts], axis=1)
            dil_src.append(_residue_order(qkv, B, S, d))
            dil_cols.append((0, 1, 2))
    dil_o, dil_lse = [], []
    for g, (_, d) in enumerate(DIL_GROUPS):
        o_g, lse_g = _dil_fwd(g, dil_src[g], dil_cols[g])
        dil_o.append(_token_order(o_g, B, S, d))
        dil_lse.append(_token_order(lse_g, B, S, d))
    y_a, L_a = _dil_combine(dil_o, dil_lse, T, S)

    wa, wb, wout = W["w_branch_a"], W["w_branch_b"], W["w_out"]
    tnd = D
    tm5 = _div(T, 512, 8)
    yap = _matmul("mix_branch_a", "nn", [[(y_a, wa)]], T, D, DIL_GW, tm5, tnd, DIL_GW, [BF16])[0]

    def merge(accs, ex):
        yapv, gav, gbv = ex
        ybp = accs[0]
        return [ybp, _sigmoid(gav.astype(F32)) * yapv.astype(F32) + _sigmoid(gbv.astype(F32)) * ybp]

    ybp, merged = _matmul("mix_branch_b", "nn", [[(y_b, wb)]], T, D, FOX_W, tm5, tnd, FOX_W, [BF16, BF16],
                          extras=[(yap, "tile", 0), (pm, "tile", GA), (pm, "tile", GB)], epilogue=merge)

    def out_proj(accs, ex):
        xv, gtv = ex
        return [xv + gtv * accs[0], accs[0]]

    x2, ymix = _matmul("mix_out", "nn", [[(merged, wout)]], T, D, D, tms, tnd, D, [F32, BF16],
                       extras=[(x1, "tile", 0), (gt2, "brow", 0)], epilogue=out_proj, rows_per_example=S)

    x3, saved3 = _ffn_forward("ffn2", x2, norm_ffn2, sh3, sc3, gt3, W["ffn2_w_gate"], W["ffn2_w_up"], W["ffn2_w_down"], S)

    dx3, loss_b, dg_final = _loss_head(x3, loss_target.reshape(T, D), norm_final.reshape(1, D), S)
    dx2, (dsh3, dsc3, dgt3, dg3), (dwg2, dwu2, dwd2) = _ffn_backward(
        "ffn2", dx3, saved3, norm_ffn2, sc3, gt3, W["ffn2_w_gate"], W["ffn2_w_up"], W["ffn2_w_down"], S)

    def merge_grad(dm, ex):
        gav, gbv, yapv, ybpv = [e.astype(F32) for e in ex]
        sga, sgb = _sigmoid(gav), _sigmoid(gbv)
        return [dm * sga, dm * sgb, dm * yapv * sga * (1.0 - sga), dm * ybpv * sgb * (1.0 - sgb)]

    dym, dgt2, dyap, dybp, dga, dgb = _gated_grad_matmul(
        "mix_merge_grad", dx2, ymix, gt2, 1.0, wout, [(pm, GA // D), (pm, GB // D), (yap, 0), (ybp, 0)], merge_grad, [BF16] * 4, S)
    tkw = _div(T, 512, LANES)
    dwout = _matmul("mix_dw_out", "tn", [[(merged, dym)]], D, D, T, D, D, tkw, [F32])[0]
    dwa = _matmul("mix_dw_a", "tn", [[(y_a, dyap)]], DIL_GW, D, T, DIL_GW, D, tkw, [F32])[0]
    dwb = _matmul("mix_dw_b", "tn", [[(y_b, dybp)]], FOX_W, D, T, FOX_W, D, tkw, [F32])[0]
    dy_a = _matmul("mix_dy_a", "nt", [[(dyap, wa)]], T, DIL_GW, D, tm1k, DIL_GW, D, [F32])[0]
    dy_b = _matmul("mix_dy_b", "nt", [[(dybp, wb)]], T, FOX_W, D, tm1k, FOX_W, D, [BF16])[0]

    do3 = dy_b.reshape(B, S, FOX_W)
    delta_b = _fox_delta(dy_b, o_b32.reshape(T, FOX_W), T, S).reshape(B, S, LANES)
    dq_b, ds_rows, dk_b, dv_b, ds_cols = _fox_bwd(pm3, do3, delta_b, lse_b, cumT, qcol, kcol, vcol, tq)

    delta_a, dy_a16 = _dil_delta(dy_a, y_a, T, S)
    dqs, dks, dvs = [], [], []
    for g, (_, d) in enumerate(DIL_GROUPS):
        dq_g, dk_g, dv_g = _dil_bwd(g, dil_src[g], dil_cols[g], _residue_order(L_a, B, S, d),
                                    _residue_order(dy_a16, B, S, d), _residue_order(delta_a, B, S, d))
        dqs.append(_token_order(dq_g, B, S, d))
        dks.append(_token_order(dk_g, B, S, d))
        dvs.append(_token_order(dv_g, B, S, d))

    dcum = ds_rows - ds_cols
    dcum_run = _cumsum(dcum)
    dcum_tot = dcum_run[:, S - 1:S, :]

    def forget_grad_fn(run, dcv, fl, tot):
        lane = lax.broadcasted_iota(jnp.int32, fl.shape, 1)
        df = jnp.where(lane < FOX_HEADS, (tot - run + dcv) * _sigmoid(-fl), 0.0)
        return [df, _colsum(df)]

    df16, dfb = _rowwise("forget_gate_grad", forget_grad_fn, T, tms,
                         [(dcum_run.reshape(T, LANES), "row", None), (dcum.reshape(T, LANES), "row", None), (flog, "row", None),
                          (dcum_tot, "bvec", None)],
                         [("row", LANES, BF16), ("bacc", LANES, F32)], S)

    dpm = jnp.concatenate([dga, dgb, dq_b.reshape(T, FOX_W), dk_b.reshape(T, FOX_W), dv_b.reshape(T, FOX_W)]
                          + dqs + dks + dvs + ([jnp.zeros((T, nmp - nm), BF16)] if nmp > nm else []), axis=1)
    tmn = _div(nmp, 2048, LANES)
    dwmT = _matmul("mix_dw_in", "tn", [[(dpm, h2)]], nmp, D, T, tmn, D, tkw, [F32])[0]
    dwfT = _matmul("mix_dw_f", "tn", [[(df16, h2)]], LANES, D, T, LANES, D, tkw, [F32])[0]
    dx1, dsh2, dsc2, dgmix = _matmul_normmod_bwd("mix_dh", [(dpm, wmT), (df16, wfT)], x1, norm_mix, sc2, dx2, S)

    dx0, (dsh1, dsc1, dgt1, dg1), (dwg1, dwu1, dwd1) = _ffn_backward(
        "ffn1", dx1, saved1, norm_ffn1, sc1, gt1, W["ffn1_w_gate"], W["ffn1_w_up"], W["ffn1_w_down"], S)
    grad_x = dx0.reshape(B, S, D)

    dmod = jnp.concatenate([dsh1, dsc1, dgt1, dsh2, dsc2, dgt2, dsh3, dsc3, dgt3], axis=1).reshape(B, N_MOD * D)
    dmod_all = _small_allgather(dmod, "gather_dmod").reshape(N_DEV * B, N_MOD * D)
    dmod_cols = lax.dynamic_slice(dmod_all, (0, me * cl), (N_DEV * B, cl))
    g_ada_w, g_ada_b = _ada_backward(c_act, dmod_cols, dmod_all)

    fbg = jnp.sum(dfb, axis=0)
    small = jnp.concatenate([jnp.sum(dg1, axis=0), jnp.sum(dgmix, axis=0), jnp.sum(dg3, axis=0), jnp.sum(dg_final, axis=0),
                             fbg, jnp.sum(loss_b, axis=0)], axis=1)
    small = _sum_devices(_small_allgather(small, "gather_small"))
    g_small = {"norm_ffn1": small[:, 0:D], "norm_mix": small[:, D:2 * D], "norm_ffn2": small[:, 2 * D:3 * D],
               "norm_final": small[:, 3 * D:4 * D], "forget_bias": small[:, 4 * D:4 * D + FOX_HEADS], "ada_b": g_ada_b}
    loss = small[0, 4 * D + LANES]

    dwinT = jnp.concatenate([dwmT[QA:QA + 3 * DIL_W], dwmT[QB:QB + 3 * FOX_W], dwfT[:8], dwmT[GA:2 * D]], axis=0)
    gfull = {"ffn1_w_gate": dwg1, "ffn1_w_up": dwu1, "ffn1_w_down": dwd1, "w_in": dwinT, "w_branch_a": dwa, "w_branch_b": dwb,
             "w_out": dwout, "ffn2_w_gate": dwg2, "ffn2_w_up": dwu2, "ffn2_w_down": dwd2}
    g_packed = jnp.concatenate([packs[n].pack_grad(gfull[n]) for n in BIG] + [jnp.zeros((N_DEV, pad_rows, D), F32)], axis=1)
    g_local = _reduce_scatter(g_packed)
    grads = {n: packs[n].unpack_local(g_local[offs[n]:offs[n] + packs[n].rows_pad])[None] for n in BIG}
    grads["ada_w"] = g_ada_w[None]

    delta, new_m, new_v = {}, {}, {}
    for n in ["ada_w"] + BIG:
        shp = args[n].shape
        d_, m_, v_ = _adamw(f"adamw_{n}", args[n][0], grads[n][0], args["m_" + n][0], args["v_" + n][0])
        delta[n], new_m[n], new_v[n] = d_.reshape(shp), m_.reshape(shp), v_.reshape(shp)
    sizes = [args[n].size for n in SMALL]
    tot = sum(sizes)
    padded = -(-tot // (8 * LANES)) * (8 * LANES)

    def flat(get):
        v = jnp.concatenate([get(n).reshape(-1) for n in SMALL])
        return jnp.pad(v, (0, padded - tot)).reshape(8, padded // 8)

    d_s, m_s, v_s = _adamw("adamw_small", flat(lambda n: args[n]), flat(lambda n: g_small[n]), flat(lambda n: args["m_" + n]),
                           flat(lambda n: args["v_" + n]))
    o = 0
    for n, sz in zip(SMALL, sizes):
        shp = args[n].shape
        grads[n] = g_small[n].reshape(shp)
        delta[n] = d_s.reshape(-1)[o:o + sz].reshape(shp)
        new_m[n] = m_s.reshape(-1)[o:o + sz].reshape(shp)
        new_v[n] = v_s.reshape(-1)[o:o + sz].reshape(shp)
        o += sz

    order = ["ada_w", "ada_b", "norm_ffn1", "ffn1_w_gate", "ffn1_w_up", "ffn1_w_down", "norm_mix", "w_in", "forget_bias",
             "w_branch_a", "w_branch_b", "w_out", "norm_ffn2", "ffn2_w_gate", "ffn2_w_up", "ffn2_w_down", "norm_final"]
    return (loss, grad_x, *[grads[n] for n in order], *[delta[n] for n in order], *[new_m[n] for n in order],
            *[new_v[n] for n in order])
```

```python
import functools
import math

import jax
import jax.numpy as jnp
from jax import lax
from jax.experimental import pallas as pl
from jax.experimental.pallas import tpu as pltpu

F32 = jnp.float32
BF16 = jnp.bfloat16
MESH = pl.DeviceIdType.MESH
ANY = pl.BlockSpec(memory_space=pl.ANY)
VMEM_SPEC = pl.BlockSpec(memory_space=pltpu.VMEM)

N_DEV = 8
HEAD_DIM = 64
BLOCK = 128
DIL_GROUPS = ((128, 1), (512, 4), (2048, 16))
N_DIL = len(DIL_GROUPS)
DIL_HPG = 4
DIL_GW = DIL_HPG * HEAD_DIM
DIL_W = N_DIL * DIL_GW
FOX_HEADS = 8
FOX_W = FOX_HEADS * HEAD_DIM
N_MOD = 9
RMS_EPS = 1e-6
ALIBI_MAX_BIAS = 8.0
NEG_INF = -1e30
ADAM_LR, ADAM_B1, ADAM_B2, ADAM_EPS, ADAM_WD, ADAM_STEP = 0.001, 0.9, 0.999, 1e-08, 0.01, 10
V7X_VMEM_LIMIT = 52 * 1024 * 1024
LANES = 128
ROW_ALIGN = 16
PACK_ROW_QUANTUM = 32
FOX_STRIP = 32
SCALE = 1.0 / math.sqrt(HEAD_DIM)


def _div(dim, target, quantum):
    best = None
    for t in range(quantum, min(dim, target) + 1, quantum):
        if dim % t == 0:
            best = t
    return best or dim


def _params(sem=None):
    return pltpu.CompilerParams(dimension_semantics=sem, vmem_limit_bytes=V7X_VMEM_LIMIT)


def _sigmoid(x):
    return 1.0 / (1.0 + jnp.exp(-x))


def _position():
    x, y, c = lax.axis_index("x"), lax.axis_index("y"), lax.axis_index("c")
    return x, y, c


def _small_allgather(v, name):
    rows, cols = v.shape

    def body(v_ref, out_ref, send_sems, recv_sems):
        x, y, c = _position()
        me = 4 * x + 2 * y + c
        out_ref[me] = v_ref[...]

        def peer(k):
            return (1 - x if k & 4 else x, 1 - y if k & 2 else y, 1 - c if k & 1 else c)

        def copy(k, slot):
            return pltpu.make_async_remote_copy(
                src_ref=v_ref, dst_ref=out_ref.at[slot], send_sem=send_sems.at[k - 1], recv_sem=recv_sems.at[k - 1],
                device_id=peer(k), device_id_type=MESH)

        sends = [copy(k, me) for k in range(1, N_DEV)]
        for cp in sends:
            cp.start()
        for k in range(1, N_DEV):
            px, py, pc = peer(k)
            copy(k, 4 * px + 2 * py + pc).wait_recv()
        for cp in sends:
            cp.wait_send()

    return pl.pallas_call(
        body, name=name,
        out_shape=jax.ShapeDtypeStruct((N_DEV, rows, cols), v.dtype),
        in_specs=[VMEM_SPEC], out_specs=VMEM_SPEC,
        scratch_shapes=[pltpu.SemaphoreType.DMA((N_DEV - 1,)), pltpu.SemaphoreType.DMA((N_DEV - 1,))],
    )(v)


def _weight_allgather(p):
    rows, cols = p.shape

    def body(p_ref, out_ref, send_sems, recv_sems, local_sem):
        x, y, c = _position()
        me, sibling = (x, y, c), (x, y, 1 - c)
        chips = [(1 - x, y), (x, 1 - y), (1 - x, 1 - y)]

        def slot(px, py, pc):
            return out_ref.at[4 * px + 2 * py + pc]

        def copy(k, block, to, src=None):
            return pltpu.make_async_remote_copy(
                src_ref=slot(*block) if src is None else src, dst_ref=slot(*block),
                send_sem=send_sems.at[k], recv_sem=recv_sems.at[k], device_id=to, device_id_type=MESH)

        mine = pltpu.make_async_copy(p_ref, slot(*me), local_sem)
        mine.start()
        first = [copy(0, me, sibling, src=p_ref)]
        first += [copy(1 + j, me, (*chip, c), src=p_ref) for j, chip in enumerate(chips)]
        for cp in first:
            cp.start()
        passed = [copy(4 + j, (*chip, c), sibling) for j, chip in enumerate(chips)]
        for j, chip in enumerate(chips):
            copy(1 + j, (*chip, c), me).wait_recv()
            passed[j].start()
        copy(0, sibling, me).wait_recv()
        for j, chip in enumerate(chips):
            copy(4 + j, (*chip, 1 - c), me).wait_recv()
        for cp in first + passed:
            cp.wait_send()
        mine.wait()

    return pl.pallas_call(
        body, name="weight_allgather",
        out_shape=jax.ShapeDtypeStruct((N_DEV, rows, cols), p.dtype),
        in_specs=[ANY], out_specs=ANY,
        scratch_shapes=[pltpu.SemaphoreType.DMA((7,)), pltpu.SemaphoreType.DMA((7,)), pltpu.SemaphoreType.DMA],
    )(p)


def _grad_exchange_sibling(g):
    _, rows, cols = g.shape

    def body(g_ref, out_ref, send_sems, recv_sems):
        x, y, c = _position()
        sibling = (x, y, 1 - c)

        def copy(q):
            px, py = q >> 1, q & 1
            return pltpu.make_async_remote_copy(
                src_ref=g_ref.at[4 * px + 2 * py + (1 - c)], dst_ref=out_ref.at[q],
                send_sem=send_sems.at[q], recv_sem=recv_sems.at[q], device_id=sibling, device_id_type=MESH)

        copies = [copy(q) for q in range(4)]
        for cp in copies:
            cp.start()
        for cp in copies:
            cp.wait_recv()
        for cp in copies:
            cp.wait_send()

    return pl.pallas_call(
        body, name="grad_exchange_sibling",
        out_shape=jax.ShapeDtypeStruct((4, rows, cols), g.dtype),
        in_specs=[ANY], out_specs=ANY,
        scratch_shapes=[pltpu.SemaphoreType.DMA((4,)), pltpu.SemaphoreType.DMA((4,))],
    )(g)


def _grad_exchange_chips(s):
    _, rows, cols = s.shape

    def body(s_ref, out_ref, send_sems, recv_sems):
        x, y, c = _position()
        chips = [(1 - x, y), (x, 1 - y), (1 - x, 1 - y)]

        def copy(k):
            return pltpu.make_async_remote_copy(
                src_ref=s_ref.at[k], dst_ref=out_ref.at[k], send_sem=send_sems.at[k], recv_sem=recv_sems.at[k],
                device_id=(*chips[k], c), device_id_type=MESH)

        copies = [copy(k) for k in range(3)]
        for cp in copies:
            cp.start()
        for cp in copies:
            cp.wait_recv()
        for cp in copies:
            cp.wait_send()

    return pl.pallas_call(
        body, name="grad_exchange_chips",
        out_shape=jax.ShapeDtypeStruct((3, rows, cols), s.dtype),
        in_specs=[ANY], out_specs=ANY,
        scratch_shapes=[pltpu.SemaphoreType.DMA((3,)), pltpu.SemaphoreType.DMA((3,))],
    )(s)


class _Rider:
    def __init__(self, operands, out_shapes, n_send, n_recv, start, finish, aliases=None):
        self.operands, self.out_shapes = list(operands), list(out_shapes)
        self.n_send, self.n_recv, self.start, self.finish = n_send, n_recv, start, finish
        self.aliases = aliases or {}


def _pcall(body, *, name, grid, in_specs, operands, out_shape, out_specs, scratch_shapes, params, rider=None):
    if rider is None:
        return pl.pallas_call(body, name=name, out_shape=out_shape, grid=grid, in_specs=in_specs, out_specs=out_specs,
                              scratch_shapes=scratch_shapes, compiler_params=params)(*operands)
    n_in, n_out, n_sc = len(operands), len(out_shape), len(scratch_shapes)
    r_in, r_out = len(rider.operands), len(rider.out_shapes)

    def wrapped(*refs):
        ins, rins = refs[:n_in], refs[n_in:n_in + r_in]
        outs, routs = refs[n_in + r_in:n_in + r_in + n_out], refs[n_in + r_in + n_out:n_in + r_in + n_out + r_out]
        rest = refs[n_in + r_in + n_out + r_out:]
        scratch, sems = rest[:n_sc], rest[n_sc:]
        ids = [pl.program_id(a) for a in range(len(grid))]
        first, last = ids[0] == 0, ids[0] == grid[0] - 1
        for a in range(1, len(grid)):
            first, last = first & (ids[a] == 0), last & (ids[a] == grid[a] - 1)

        @pl.when(first)
        def _():
            rider.start(rins, routs, *sems)

        body(*ins, *outs, *scratch)

        @pl.when(last)
        def _():
            rider.finish(rins, routs, *sems)

    return pl.pallas_call(
        wrapped, name=name, out_shape=list(out_shape) + rider.out_shapes, grid=grid,
        in_specs=list(in_specs) + [ANY] * r_in, out_specs=list(out_specs) + [ANY] * r_out,
        scratch_shapes=list(scratch_shapes) + [pltpu.SemaphoreType.DMA((rider.n_send,)), pltpu.SemaphoreType.DMA((rider.n_recv,))],
        input_output_aliases={n_in + i: n_out + o for i, o in rider.aliases.items()},
        compiler_params=params,
    )(*operands, *rider.operands)


def _flips(x, y, c):
    return [(x, y, 1 - c), (1 - x, y, c), (x, 1 - y, c), (1 - x, 1 - y, c)]


def _gather_direct_rider(p):
    rows, cols = p.shape

    def copies(p_ref, land, send_sems, recv_sems):
        x, y, c = _position()
        me = 4 * x + 2 * y + c
        peers = _flips(x, y, c)
        sends = [pltpu.make_async_remote_copy(src_ref=p_ref, dst_ref=land.at[me], send_sem=send_sems.at[k], recv_sem=recv_sems.at[k],
                                              device_id=to, device_id_type=MESH) for k, to in enumerate(peers)]
        recvs = [pltpu.make_async_remote_copy(src_ref=p_ref, dst_ref=land.at[4 * px + 2 * py + pc], send_sem=send_sems.at[k],
                                              recv_sem=recv_sems.at[k], device_id=(px, py, pc), device_id_type=MESH)
                 for k, (px, py, pc) in enumerate(peers)]
        mine = pltpu.make_async_copy(p_ref, land.at[me], send_sems.at[len(peers)])
        return sends, recvs, mine

    def start(rins, routs, send_sems, recv_sems):
        sends, _, mine = copies(rins[0], routs[0], send_sems, recv_sems)
        mine.start()
        for cp in sends:
            cp.start()

    def finish(rins, routs, send_sems, recv_sems):
        sends, recvs, mine = copies(rins[0], routs[0], send_sems, recv_sems)
        for cp in recvs:
            cp.wait_recv()
        for cp in sends:
            cp.wait_send()
        mine.wait()

    return _Rider([p], [jax.ShapeDtypeStruct((N_DEV, rows, cols), p.dtype)], 5, 4, start, finish)


def _gather_forward_rider(land):
    def copies(buf, send_sems, recv_sems):
        x, y, c = _position()
        chips = [(1 - x, y), (x, 1 - y), (1 - x, 1 - y)]
        sends = [pltpu.make_async_remote_copy(src_ref=buf.at[4 * px + 2 * py + c], dst_ref=buf.at[4 * px + 2 * py + c],
                                              send_sem=send_sems.at[k], recv_sem=recv_sems.at[k], device_id=(x, y, 1 - c),
                                              device_id_type=MESH) for k, (px, py) in enumerate(chips)]
        recvs = [pltpu.make_async_remote_copy(src_ref=buf.at[4 * px + 2 * py + 1 - c], dst_ref=buf.at[4 * px + 2 * py + 1 - c],
                                              send_sem=send_sems.at[k], recv_sem=recv_sems.at[k], device_id=(x, y, 1 - c),
                                              device_id_type=MESH) for k, (px, py) in enumerate(chips)]
        return sends, recvs

    def start(rins, routs, send_sems, recv_sems):
        for cp in copies(routs[0], send_sems, recv_sems)[0]:
            cp.start()

    def finish(rins, routs, send_sems, recv_sems):
        sends, recvs = copies(routs[0], send_sems, recv_sems)
        for cp in recvs:
            cp.wait_recv()
        for cp in sends:
            cp.wait_send()

    return _Rider([land], [jax.ShapeDtypeStruct(land.shape, land.dtype)], 3, 3, start, finish, aliases={0: 0})


def _chip_exchange_rider(s):
    def copies(s_ref, out_ref, send_sems, recv_sems):
        x, y, c = _position()
        chips = [(1 - x, y), (x, 1 - y), (1 - x, 1 - y)]
        return [pltpu.make_async_remote_copy(src_ref=s_ref.at[k], dst_ref=out_ref.at[k], send_sem=send_sems.at[k],
                                             recv_sem=recv_sems.at[k], device_id=(*chips[k], c), device_id_type=MESH)
                for k in range(3)]

    def start(rins, routs, send_sems, recv_sems):
        for cp in copies(rins[0], routs[0], send_sems, recv_sems):
            cp.start()

    def finish(rins, routs, send_sems, recv_sems):
        cps = copies(rins[0], routs[0], send_sems, recv_sems)
        for cp in cps:
            cp.wait_recv()
        for cp in cps:
            cp.wait_send()

    return _Rider([s], [jax.ShapeDtypeStruct(s.shape, s.dtype)], 3, 3, start, finish)


def _chip_partial_sums(g, recv_sib, jj, qq):
    _, rows, cols = g.shape
    tr = _div(rows, 512, ROW_ALIGN)

    def body(jj_ref, qq_ref, g_ref, r_ref, o_ref):
        o_ref[...] = (g_ref[...] + r_ref[...]).astype(o_ref.dtype)

    return pl.pallas_call(
        body, name="chip_partial_sums",
        out_shape=jax.ShapeDtypeStruct((3, rows, cols), BF16),
        grid_spec=pltpu.PrefetchScalarGridSpec(
            num_scalar_prefetch=2, grid=(3, rows // tr),
            in_specs=[pl.BlockSpec((None, tr, cols), lambda k, i, jj, qq: (jj[k], i, 0)),
                      pl.BlockSpec((None, tr, cols), lambda k, i, jj, qq: (qq[k], i, 0))],
            out_specs=pl.BlockSpec((None, tr, cols), lambda k, i, jj, qq: (k, i, 0))),
        compiler_params=_params(("arbitrary", "arbitrary")),
    )(jj, qq, g, recv_sib)


def _own_partial_sum(g, recv_sib, jj, qq):
    _, rows, cols = g.shape
    tr = _div(rows, 512, ROW_ALIGN)

    def body(jj_ref, qq_ref, g_ref, r_ref, o_ref):
        o_ref[...] = g_ref[...] + r_ref[...]

    return pl.pallas_call(
        body, name="own_partial_sum",
        out_shape=jax.ShapeDtypeStruct((rows, cols), F32),
        grid_spec=pltpu.PrefetchScalarGridSpec(
            num_scalar_prefetch=2, grid=(rows // tr,),
            in_specs=[pl.BlockSpec((None, tr, cols), lambda i, jj, qq: (jj[0], i, 0)),
                      pl.BlockSpec((None, tr, cols), lambda i, jj, qq: (qq[0], i, 0))],
            out_specs=pl.BlockSpec((tr, cols), lambda i, jj, qq: (i, 0))),
        compiler_params=_params(("arbitrary",)),
    )(jj, qq, g, recv_sib)


def _final_grad_sum(own, recv):
    rows, cols = own.shape
    tr = _div(rows, 512, ROW_ALIGN)

    def body(o_ref, r_ref, out_ref):
        out_ref[...] = ((o_ref[...] + r_ref[0].astype(F32)) + r_ref[1].astype(F32)) + r_ref[2].astype(F32)

    return pl.pallas_call(
        body, name="final_grad_sum",
        out_shape=jax.ShapeDtypeStruct((rows, cols), F32),
        grid=(rows // tr,),
        in_specs=[pl.BlockSpec((tr, cols), lambda i: (i, 0)), pl.BlockSpec((3, tr, cols), lambda i: (0, i, 0))],
        out_specs=pl.BlockSpec((tr, cols), lambda i: (i, 0)),
        compiler_params=_params(("arbitrary",)),
    )(own, recv)


def _chip_sums(g):
    x, y, c = _position()
    chips = [(1 - x, y), (x, 1 - y), (1 - x, 1 - y)]
    jj = jnp.stack([4 * px + 2 * py + c for px, py in chips]).astype(jnp.int32)
    qq = jnp.stack([2 * px + py for px, py in chips]).astype(jnp.int32)
    jme = jnp.reshape(4 * x + 2 * y + c, (1,)).astype(jnp.int32)
    qme = jnp.reshape(2 * x + y, (1,)).astype(jnp.int32)
    recv_sib = _grad_exchange_sibling(g)
    return _chip_partial_sums(g, recv_sib, jj, qq), _own_partial_sum(g, recv_sib, jme, qme)


def _matmul(name, form, prods, M, N, K, tm, tn, tk, out_dtypes, extras=(), epilogue=None, rows_per_example=None, rider=None):
    nk = K // tk
    n_acc = len(prods)
    flat = [ab for group in prods for ab in group]
    dims = {"nn": (((1,), (0,)), ((), ())), "nt": (((1,), (1,)), ((), ())), "tn": (((0,), (0,)), ((), ()))}[form]
    direct = nk > 1 and epilogue is None and n_acc == 1 and list(out_dtypes) == [F32]

    def spec(shape, index_map, whole):
        if whole:
            return pl.BlockSpec(shape, index_map, pipeline_mode=pl.Buffered(1))
        return pl.BlockSpec(shape, index_map)

    if form == "tn":
        a_spec = spec((tk, tm), lambda i, j, k: (k, i), nk == 1 and M == tm)
    else:
        a_spec = spec((tm, tk), lambda i, j, k: (i, k), nk == 1 and M == tm)
    if form == "nt":
        b_spec = spec((tn, tk), lambda i, j, k: (j, k), nk == 1 and N == tn)
    else:
        b_spec = spec((tk, tn), lambda i, j, k: (k, j), nk == 1 and N == tn)
    in_specs, operands = [], []
    for a, b in flat:
        in_specs += [a_spec, b_spec]
        operands += [a, b]
    for arr, kind, off in extras:
        if kind == "tile":
            assert off % tn == 0
            in_specs.append(pl.BlockSpec((tm, tn), functools.partial(lambda i, j, k, o: (i, j + o), o=off // tn)))
        else:
            tiles = rows_per_example // tm
            in_specs.append(pl.BlockSpec((None, 1, tn), functools.partial(lambda i, j, k, t: (i // t, 0, j), t=tiles)))
        operands.append(arr)
    n_in, n_out = len(operands), len(out_dtypes)

    def body(*refs):
        in_refs, out_refs, acc_refs = refs[:n_in], refs[n_in:n_in + n_out], refs[n_in + n_out:]
        k = pl.program_id(2)
        partials, p = [], 0
        for group in prods:
            tot = None
            for _ in group:
                d = lax.dot_general(in_refs[2 * p][...], in_refs[2 * p + 1][...], dims, preferred_element_type=F32)
                tot = d if tot is None else tot + d
                p += 1
            partials.append(tot)

        def finish(accs):
            ex = [r[...] for r in in_refs[2 * len(flat):]]
            outs = epilogue(accs, ex) if epilogue is not None else accs
            for r, o in zip(out_refs, outs):
                r[...] = o.astype(r.dtype)

        if nk == 1:
            finish(partials)
        elif direct:
            @pl.when(k == 0)
            def _():
                out_refs[0][...] = partials[0]

            @pl.when(k > 0)
            def _():
                out_refs[0][...] += partials[0]
        else:
            @pl.when(k == 0)
            def _():
                for r, v in zip(acc_refs, partials):
                    r[...] = v

            @pl.when(k > 0)
            def _():
                for r, v in zip(acc_refs, partials):
                    r[...] += v

            @pl.when(k == nk - 1)
            def _():
                finish([r[...] for r in acc_refs])

    return _pcall(
        body, name=name,
        out_shape=[jax.ShapeDtypeStruct((M, N), dt) for dt in out_dtypes],
        grid=(M // tm, N // tn, nk),
        in_specs=in_specs, operands=operands,
        out_specs=[pl.BlockSpec((tm, tn), lambda i, j, k: (i, j)) for _ in out_dtypes],
        scratch_shapes=[pltpu.VMEM((tm, tn), F32) for _ in range(n_acc)] if nk > 1 and not direct else [],
        params=_params(("parallel", "parallel", "arbitrary")), rider=rider)


def _rowwise(name, fn, T, tm, ins, outs, rows_per_example):
    tiles = rows_per_example // tm
    n_ex = T // rows_per_example
    in_specs, operands = [], []
    for arr, kind, arg in ins:
        if kind == "row":
            if arg is None:
                in_specs.append(pl.BlockSpec((tm, arr.shape[1]), lambda i: (i, 0)))
            else:
                in_specs.append(pl.BlockSpec((tm, arg[0]), functools.partial(lambda i, cb: (i, cb), cb=arg[1])))
        elif kind == "bvec":
            in_specs.append(pl.BlockSpec((None, 1, arr.shape[2]), lambda i: (i // tiles, 0, 0)))
        else:
            in_specs.append(pl.BlockSpec((1, arr.shape[1]), lambda i: (0, 0)))
        operands.append(arr)
    out_shape, out_specs = [], []
    for kind, cols, dt in outs:
        if kind == "row":
            out_shape.append(jax.ShapeDtypeStruct((T, cols), dt))
            out_specs.append(pl.BlockSpec((tm, cols), lambda i: (i, 0)))
        else:
            out_shape.append(jax.ShapeDtypeStruct((n_ex, 1, cols), F32))
            out_specs.append(pl.BlockSpec((None, 1, cols), lambda i: (i // tiles, 0, 0)))
    n_in = len(operands)

    def body(*refs):
        i = pl.program_id(0)
        vals = fn(*[r[...] for r in refs[:n_in]])
        for (kind, _, _), r, v in zip(outs, refs[n_in:], vals):
            if kind == "row":
                r[...] = v.astype(r.dtype)
            else:
                @pl.when(i % tiles == 0)
                def _():
                    r[...] = jnp.zeros_like(r)

                r[...] += v

    return pl.pallas_call(
        body, name=name, out_shape=out_shape, grid=(T // tm,), in_specs=in_specs, out_specs=out_specs,
        compiler_params=_params(("arbitrary",)),
    )(*operands)


def _colsum(v):
    return jnp.sum(v, axis=0, keepdims=True)


def _rms_parts(x):
    rstd = lax.rsqrt(jnp.mean(x * x, axis=-1, keepdims=True) + RMS_EPS)
    return x * rstd, rstd


def _normmod(name, x, g, sc, sh, S):
    T, D = x.shape

    def fn(xv, gv, scv, shv):
        xhat, _ = _rms_parts(xv)
        return [(xhat * gv) * (1.0 + scv) + shv]

    return _rowwise(name, fn, T, _div(S, 512, 8), [(x, "row", None), (g, "vec", None), (sc, "bvec", None), (sh, "bvec", None)],
                    [("row", D, BF16)], S)[0]


def _normmod_bwd(name, x, g, sc, dh, dres, S):
    T, D = x.shape

    def fn(xv, gv, scv, dhv, drv):
        xhat, rstd = _rms_parts(xv)
        n = xhat * gv
        dn = dhv * (1.0 + scv)
        dxh = dn * gv
        dx = rstd * (dxh - xhat * jnp.mean(dxh * xhat, axis=-1, keepdims=True))
        return [drv + dx, _colsum(dhv), _colsum(dhv * n), _colsum(dn * xhat)]

    return _rowwise(name, fn, T, _div(S, 256, 8),
                    [(x, "row", None), (g, "vec", None), (sc, "bvec", None), (dh, "row", None), (dres, "row", None)],
                    [("row", D, F32), ("bacc", D, F32), ("bacc", D, F32), ("bacc", D, F32)], S)


def _gate_grad(name, dx, y, gt, coeff, S):
    T, D = dx.shape

    def fn(dxv, yv, gtv):
        return [coeff * gtv * dxv, _colsum(coeff * dxv * yv.astype(F32))]

    return _rowwise(name, fn, T, _div(S, 512, 8), [(dx, "row", None), (y, "row", None), (gt, "bvec", None)],
                    [("row", D, BF16), ("bacc", D, F32)], S)


def _resident(shape):
    return pl.BlockSpec(shape, lambda i: (0, 0), pipeline_mode=pl.Buffered(1))


def _example_acc(r, i, tiles, v):
    @pl.when(i % tiles == 0)
    def _():
        r[...] = jnp.zeros_like(r)

    r[...] += v


def _norm_matmul(name, x, g, sc, sh, weights, epilogue, outs, S, vecs=(), rider=None):
    T, D = x.shape
    tm = _div(S, 256, 8)
    tiles = S // tm
    nw, nv = len(weights), len(vecs)

    def body(*refs):
        x_ref, g_ref, sc_ref, sh_ref = refs[:4]
        w_refs, v_refs = refs[4:4 + nw], refs[4 + nw:4 + nw + nv]
        h_ref, out_refs = refs[4 + nw + nv], refs[5 + nw + nv:]
        xhat, _ = _rms_parts(x_ref[...])
        h = ((xhat * g_ref[...]) * (1.0 + sc_ref[...]) + sh_ref[...]).astype(BF16)
        h_ref[...] = h
        accs = [lax.dot_general(h, w[...], NT, preferred_element_type=F32) for w in w_refs]
        for r, o in zip(out_refs, epilogue(accs, *[v[...] for v in v_refs])):
            r[...] = o.astype(r.dtype)

    bvec = pl.BlockSpec((None, 1, D), lambda i: (i // tiles, 0, 0))
    return _pcall(
        body, name=name,
        out_shape=[jax.ShapeDtypeStruct((T, D), BF16)] + [jax.ShapeDtypeStruct((T, w), dt) for w, dt in outs],
        grid=(T // tm,),
        in_specs=[pl.BlockSpec((tm, D), lambda i: (i, 0)), pl.BlockSpec((1, D), lambda i: (0, 0)), bvec, bvec]
        + [_resident(w.shape) for w in weights] + [pl.BlockSpec(v.shape, lambda i: (0, 0)) for v in vecs],
        operands=[x, g, sc, sh, *weights, *vecs],
        out_specs=[pl.BlockSpec((tm, D), lambda i: (i, 0))] + [pl.BlockSpec((tm, w), lambda i: (i, 0)) for w, _ in outs],
        scratch_shapes=[], params=_params(("arbitrary",)), rider=rider)


def _gated_grad_matmul(name, dx, y, gt, coeff, w, tiles_in, epilogue, outs, S, rider=None):
    T, D = dx.shape
    N = w.shape[0]
    tm = _div(S, 256, 8)
    tiles = S // tm
    nt = len(tiles_in)

    def body(*refs):
        dx_ref, y_ref, gt_ref, w_ref = refs[:4]
        t_refs, dy_ref, dgt_ref, out_refs = refs[4:4 + nt], refs[4 + nt], refs[5 + nt], refs[6 + nt:]
        i = pl.program_id(0)
        dxv = dx_ref[...]
        dy = (coeff * gt_ref[...] * dxv).astype(BF16)
        dy_ref[...] = dy
        _example_acc(dgt_ref, i, tiles, _colsum(coeff * dxv * y_ref[...].astype(F32)))
        acc = lax.dot_general(dy, w_ref[...], NT, preferred_element_type=F32)
        for r, o in zip(out_refs, epilogue(acc, [t[...] for t in t_refs])):
            r[...] = o.astype(r.dtype)

    row = pl.BlockSpec((tm, D), lambda i: (i, 0))
    bvec = pl.BlockSpec((None, 1, D), lambda i: (i // tiles, 0, 0))
    return _pcall(
        body, name=name,
        out_shape=[jax.ShapeDtypeStruct((T, D), BF16), jax.ShapeDtypeStruct((T // S, 1, D), F32)]
        + [jax.ShapeDtypeStruct((T, N), dt) for dt in outs],
        grid=(T // tm,),
        in_specs=[row, row, bvec, _resident(w.shape)]
        + [pl.BlockSpec((tm, N), functools.partial(lambda i, cb: (i, cb), cb=cb)) for _, cb in tiles_in],
        operands=[dx, y, gt, w, *[t for t, _ in tiles_in]],
        out_specs=[row, bvec] + [pl.BlockSpec((tm, N), lambda i: (i, 0)) for _ in outs],
        scratch_shapes=[], params=_params(("arbitrary",)), rider=rider)


def _matmul_normmod_bwd(name, prods, x, g, sc, dres, S):
    T, D = x.shape
    tm = _div(S, 256, 8)
    tiles = S // tm
    npr = len(prods)

    def body(*refs):
        ab = refs[:2 * npr]
        x_ref, g_ref, sc_ref, dr_ref = refs[2 * npr:2 * npr + 4]
        dx_ref, dsh_ref, dsc_ref, dg_ref = refs[2 * npr + 4:]
        i = pl.program_id(0)
        dh = None
        for p in range(npr):
            d = lax.dot_general(ab[2 * p][...], ab[2 * p + 1][...], NN, preferred_element_type=F32)
            dh = d if dh is None else dh + d
        xhat, rstd = _rms_parts(x_ref[...])
        gv = g_ref[...]
        dn = dh * (1.0 + sc_ref[...])
        dxh = dn * gv
        dx_ref[...] = dr_ref[...] + rstd * (dxh - xhat * jnp.mean(dxh * xhat, axis=-1, keepdims=True))
        _example_acc(dsh_ref, i, tiles, _colsum(dh))
        _example_acc(dsc_ref, i, tiles, _colsum(dh * (xhat * gv)))
        _example_acc(dg_ref, i, tiles, _colsum(dn * xhat))

    row = pl.BlockSpec((tm, D), lambda i: (i, 0))
    bvec = pl.BlockSpec((None, 1, D), lambda i: (i // tiles, 0, 0))
    in_specs, operands = [], []
    for a, b in prods:
        in_specs += [pl.BlockSpec((tm, a.shape[1]), lambda i: (i, 0)), _resident(b.shape)]
        operands += [a, b]
    acc_shape = jax.ShapeDtypeStruct((T // S, 1, D), F32)
    return pl.pallas_call(
        body, name=name,
        out_shape=[jax.ShapeDtypeStruct((T, D), F32), acc_shape, acc_shape, acc_shape],
        grid=(T // tm,),
        in_specs=in_specs + [row, pl.BlockSpec((1, D), lambda i: (0, 0)), bvec, row],
        out_specs=[row, bvec, bvec, bvec],
        compiler_params=_params(("arbitrary",)),
    )(*operands, x, g, sc, dres)


def _ffn_forward(tag, x, g, sh, sc, gt, wgT, wuT, wd, S, gather=None):
    T, D = x.shape
    F = wd.shape[0]

    def gateup(accs):
        a, u = accs
        return [a, u, a * _sigmoid(a) * u]

    h, a, u, s, *land = _norm_matmul(f"{tag}_gateup", x, g, sc, sh, [wgT, wuT], gateup, [(F, BF16)] * 3, S,
                                     rider=None if gather is None else _gather_direct_rider(gather))

    def down(accs, ex):
        xv, gtv = ex
        return [xv + 0.5 * gtv * accs[0], accs[0]]

    tmd = _div(S, 512, 8)
    x_new, y, *land = _matmul(f"{tag}_down", "nn", [[(s, wd)]], T, D, F, tmd, D, F, [F32, BF16],
                              extras=[(x, "tile", 0), (gt, "brow", 0)], epilogue=down, rows_per_example=S,
                              rider=None if gather is None else _gather_forward_rider(land[0]))
    return x_new, (x, h, a, u, s, y), (land[0] if land else None)


def _ffn_backward(tag, dx_out, saved, g, sc, gt, wgT, wuT, wd, S, rider=None):
    x, h, a, u, s, y = saved
    T, D = x.shape
    F = wd.shape[0]

    def act_grad(ds, ex):
        av, uv = ex[0].astype(F32), ex[1].astype(F32)
        sg = _sigmoid(av)
        return [ds * uv * (sg * (1.0 + av * (1.0 - sg))), ds * (av * sg)]

    dy, dgt, da, du, *rode = _gated_grad_matmul(f"{tag}_act_grad", dx_out, y, gt, 0.5, wd, [(a, 0), (u, 0)], act_grad,
                                                [BF16, BF16], S, rider=rider)
    tkw = _div(T, 1024, LANES)
    dwd = _matmul(f"{tag}_dw_down", "tn", [[(s, dy)]], F, D, T, F, D, tkw, [F32])[0]
    dwgT = _matmul(f"{tag}_dw_gate", "tn", [[(da, h)]], F, D, T, F, D, tkw, [F32])[0]
    dwuT = _matmul(f"{tag}_dw_up", "tn", [[(du, h)]], F, D, T, F, D, tkw, [F32])[0]
    dx_in, dsh, dsc, dg = _matmul_normmod_bwd(f"{tag}_dh", [(da, wgT), (du, wuT)], x, g, sc, dx_out, S)
    return dx_in, (dsh, dsc, dgt, dg), (dwgT, dwuT, dwd), rode


def _loss_head(x, tgt, g, S):
    T, D = x.shape

    def fn(xv, tv, gv):
        xhat, rstd = _rms_parts(xv)
        e = xhat * gv - tv
        loss = jnp.broadcast_to(0.5 / D * jnp.sum(_colsum(e * e), axis=1, keepdims=True), (1, LANES))
        dy = e * (1.0 / D)
        dxh = dy * gv
        dx = rstd * (dxh - xhat * jnp.mean(dxh * xhat, axis=-1, keepdims=True))
        return [dx, loss, _colsum(dy * xhat)]

    return _rowwise("loss_head", fn, T, _div(S, 512, 8), [(x, "row", None), (tgt, "row", None), (g, "vec", None)],
                    [("row", D, F32), ("bacc", LANES, F32), ("bacc", D, F32)], S)


def _cumsum(v):
    B, S, _ = v.shape
    rows = _div(S, 1024, BLOCK)

    def body(x_ref, o_ref, carry):
        i = pl.program_id(1)

        @pl.when(i == 0)
        def _():
            carry[...] = jnp.zeros_like(carry)

        r = lax.broadcasted_iota(jnp.int32, (BLOCK, BLOCK), 0)
        c = lax.broadcasted_iota(jnp.int32, (BLOCK, BLOCK), 1)
        tri = (c <= r).astype(F32)
        last = carry[0:1, :]
        for j in range(0, rows, BLOCK):
            cum = jnp.dot(tri, x_ref[j:j + BLOCK, :], precision=lax.Precision.HIGHEST, preferred_element_type=F32) + last
            o_ref[j:j + BLOCK, :] = cum
            last = cum[BLOCK - 1:BLOCK, :]
        carry[...] = jnp.broadcast_to(last, carry.shape)

    return pl.pallas_call(
        body, name="cumsum", out_shape=jax.ShapeDtypeStruct(v.shape, F32), grid=(B, S // rows),
        in_specs=[pl.BlockSpec((None, rows, LANES), lambda b, i: (b, i, 0))],
        out_specs=pl.BlockSpec((None, rows, LANES), lambda b, i: (b, i, 0)),
        scratch_shapes=[pltpu.VMEM((8, LANES), F32)],
        compiler_params=_params(("arbitrary", "arbitrary")),
    )(v)


def _fox_scores(q, k, cq, ck, qpos, kpos):
    s = lax.dot_general(q, k, (((1,), (1,)), ((), ())), preferred_element_type=F32) * SCALE + cq - ck
    return jnp.where(kpos <= qpos, s, NEG_INF)


def _fox_positions(qi, kj, tq, tk):
    qpos = qi * tq + lax.broadcasted_iota(jnp.int32, (tq, tk), 0)
    kpos = kj * tk + lax.broadcasted_iota(jnp.int32, (tq, tk), 1)
    return qpos, kpos


def _fox_forward(pm3, cum, cumT, qcol, kcol, vcol, tq):
    B, S, _ = pm3.shape
    nq = S // tq

    def body(q_ref, k_ref, v_ref, cq_ref, ck_ref, o_ref, o32_ref, lse_ref, m_sc, l_sc, acc_sc):
        qi, kj = pl.program_id(1), pl.program_id(2)

        @pl.when(kj == 0)
        def _():
            m_sc[...] = jnp.full_like(m_sc, NEG_INF)
            l_sc[...] = jnp.zeros_like(l_sc)
            acc_sc[...] = jnp.zeros_like(acc_sc)

        @pl.when(kj <= qi)
        def _():
            qpos, kpos = _fox_positions(qi, kj, tq, tq)
            for h in range(FOX_HEADS):
                hs = slice(HEAD_DIM * h, HEAD_DIM * (h + 1))
                s = _fox_scores(q_ref[:, hs], k_ref[:, hs], cq_ref[:, h:h + 1], ck_ref[h:h + 1, :], qpos, kpos)
                m_prev = m_sc[h]
                m_new = jnp.maximum(m_prev, jnp.max(s, axis=-1, keepdims=True))
                alpha = jnp.exp(m_prev - m_new)
                p = jnp.exp(s - m_new)
                l_sc[h] = alpha * l_sc[h] + jnp.sum(p, axis=-1, keepdims=True)
                acc_sc[:, hs] = alpha * acc_sc[:, hs] + lax.dot_general(
                    p.astype(BF16), v_ref[:, hs], (((1,), (0,)), ((), ())), preferred_element_type=F32)
                m_sc[h] = m_new

        @pl.when(kj == nq - 1)
        def _():
            lse_ref[...] = jnp.zeros_like(lse_ref)
            for h in range(FOX_HEADS):
                hs = slice(HEAD_DIM * h, HEAD_DIM * (h + 1))
                oh = acc_sc[:, hs] / l_sc[h]
                o_ref[:, hs] = oh.astype(o_ref.dtype)
                o32_ref[:, hs] = oh
                lse_ref[:, h:h + 1] = m_sc[h] + jnp.log(l_sc[h])

    return pl.pallas_call(
        body, name="fox_forward",
        out_shape=[jax.ShapeDtypeStruct((B, S, FOX_W), BF16), jax.ShapeDtypeStruct((B, S, FOX_W), F32),
                   jax.ShapeDtypeStruct((B, S, LANES), F32)],
        grid=(B, nq, nq),
        in_specs=[pl.BlockSpec((None, tq, FOX_W), lambda b, i, j: (b, i, qcol)),
                  pl.BlockSpec((None, tq, FOX_W), lambda b, i, j: (b, jnp.minimum(i, j), kcol)),
                  pl.BlockSpec((None, tq, FOX_W), lambda b, i, j: (b, jnp.minimum(i, j), vcol)),
                  pl.BlockSpec((None, tq, LANES), lambda b, i, j: (b, i, 0)),
                  pl.BlockSpec((None, 8, tq), lambda b, i, j: (b, 0, jnp.minimum(i, j)))],
        out_specs=[pl.BlockSpec((None, tq, FOX_W), lambda b, i, j: (b, i, 0)),
                   pl.BlockSpec((None, tq, FOX_W), lambda b, i, j: (b, i, 0)),
                   pl.BlockSpec((None, tq, LANES), lambda b, i, j: (b, i, 0))],
        scratch_shapes=[pltpu.VMEM((FOX_HEADS, tq, 1), F32), pltpu.VMEM((FOX_HEADS, tq, 1), F32), pltpu.VMEM((tq, FOX_W), F32)],
        compiler_params=_params(("parallel", "parallel", "arbitrary")),
    )(pm3, pm3, pm3, cum, cumT)


def _fox_dq(pm3, do, delta, lse, cum, cumT, qcol, kcol, vcol, tq):
    B, S, _ = pm3.shape
    nq = S // tq

    def body(q_ref, k_ref, v_ref, do_ref, dl_ref, lse_ref, cq_ref, ck_ref, dq_ref, dc_ref, acc_sc, dc_sc):
        qi, kj = pl.program_id(1), pl.program_id(2)

        @pl.when(kj == 0)
        def _():
            acc_sc[...] = jnp.zeros_like(acc_sc)
            dc_sc[...] = jnp.zeros_like(dc_sc)

        @pl.when(kj <= qi)
        def _():
            qpos, kpos = _fox_positions(qi, kj, tq, tq)
            for h in range(FOX_HEADS):
                hs = slice(HEAD_DIM * h, HEAD_DIM * (h + 1))
                s = _fox_scores(q_ref[:, hs], k_ref[:, hs], cq_ref[:, h:h + 1], ck_ref[h:h + 1, :], qpos, kpos)
                p = jnp.exp(s - lse_ref[:, h:h + 1])
                doh = do_ref[:, hs]
                dp = lax.dot_general(doh, v_ref[:, hs], (((1,), (1,)), ((), ())), preferred_element_type=F32)
                ds = p * (dp - dl_ref[:, h:h + 1])
                dc_sc[h] += jnp.sum(ds, axis=-1, keepdims=True)
                acc_sc[:, hs] += lax.dot_general(ds.astype(BF16), k_ref[:, hs], (((1,), (0,)), ((), ())),
                                                 preferred_element_type=F32)

        @pl.when(kj == nq - 1)
        def _():
            dq_ref[...] = (acc_sc[...] * SCALE).astype(dq_ref.dtype)
            dc_ref[...] = jnp.zeros_like(dc_ref)
            for h in range(FOX_HEADS):
                dc_ref[:, h:h + 1] = dc_sc[h]

    qspec = pl.BlockSpec((None, tq, FOX_W), lambda b, i, j: (b, i, 0))
    lspec = pl.BlockSpec((None, tq, LANES), lambda b, i, j: (b, i, 0))
    return pl.pallas_call(
        body, name="fox_dq",
        out_shape=[jax.ShapeDtypeStruct((B, S, FOX_W), BF16), jax.ShapeDtypeStruct((B, S, LANES), F32)],
        grid=(B, nq, nq),
        in_specs=[pl.BlockSpec((None, tq, FOX_W), lambda b, i, j: (b, i, qcol)),
                  pl.BlockSpec((None, tq, FOX_W), lambda b, i, j: (b, jnp.minimum(i, j), kcol)),
                  pl.BlockSpec((None, tq, FOX_W), lambda b, i, j: (b, jnp.minimum(i, j), vcol)),
                  qspec, lspec, lspec, lspec,
                  pl.BlockSpec((None, 8, tq), lambda b, i, j: (b, 0, jnp.minimum(i, j)))],
        out_specs=[qspec, lspec],
        scratch_shapes=[pltpu.VMEM((tq, FOX_W), F32), pltpu.VMEM((FOX_HEADS, tq, 1), F32)],
        compiler_params=_params(("parallel", "parallel", "arbitrary")),
    )(pm3, pm3, pm3, do, delta, lse, cum, cumT)


def _fox_dkv(pm3, do, delta, lse, cum, cumT, qcol, kcol, vcol, tq):
    B, S, _ = pm3.shape
    nq = S // tq

    def body(q_ref, k_ref, v_ref, do_ref, dl_ref, lse_ref, cq_ref, ck_ref, dk_ref, dv_ref, dc_ref, dk_sc, dv_sc, dc_sc):
        kj, qi = pl.program_id(1), pl.program_id(2)

        @pl.when(qi == 0)
        def _():
            dk_sc[...] = jnp.zeros_like(dk_sc)
            dv_sc[...] = jnp.zeros_like(dv_sc)
            dc_sc[...] = jnp.zeros_like(dc_sc)

        @pl.when(qi >= kj)
        def _():
            qpos, kpos = _fox_positions(qi, kj, tq, tq)
            for h in range(FOX_HEADS):
                hs = slice(HEAD_DIM * h, HEAD_DIM * (h + 1))
                qh = q_ref[:, hs]
                s = _fox_scores(qh, k_ref[:, hs], cq_ref[:, h:h + 1], ck_ref[h:h + 1, :], qpos, kpos)
                p = jnp.exp(s - lse_ref[:, h:h + 1])
                doh = do_ref[:, hs]
                dp = lax.dot_general(doh, v_ref[:, hs], (((1,), (1,)), ((), ())), preferred_element_type=F32)
                ds = p * (dp - dl_ref[:, h:h + 1])
                dv_sc[:, hs] += lax.dot_general(p.astype(BF16), doh, (((0,), (0,)), ((), ())), preferred_element_type=F32)
                dk_sc[:, hs] += lax.dot_general(ds.astype(BF16), qh, (((0,), (0,)), ((), ())), preferred_element_type=F32)
                dc_sc[h:h + 1, :] -= jnp.sum(ds, axis=0, keepdims=True)

        @pl.when(qi == nq - 1)
        def _():
            dk_ref[...] = (dk_sc[...] * SCALE).astype(dk_ref.dtype)
            dv_ref[...] = dv_sc[...].astype(dv_ref.dtype)
            dc_ref[...] = dc_sc[...]

    def qside(width):
        return pl.BlockSpec((None, tq, width), lambda b, j, i: (b, jnp.maximum(i, j), 0))

    kspec = pl.BlockSpec((None, tq, FOX_W), lambda b, j, i: (b, j, 0))
    return pl.pallas_call(
        body, name="fox_dkv",
        out_shape=[jax.ShapeDtypeStruct((B, S, FOX_W), BF16), jax.ShapeDtypeStruct((B, S, FOX_W), BF16),
                   jax.ShapeDtypeStruct((B, 8, S), F32)],
        grid=(B, nq, nq),
        in_specs=[pl.BlockSpec((None, tq, FOX_W), lambda b, j, i: (b, jnp.maximum(i, j), qcol)),
                  pl.BlockSpec((None, tq, FOX_W), lambda b, j, i: (b, j, kcol)),
                  pl.BlockSpec((None, tq, FOX_W), lambda b, j, i: (b, j, vcol)),
                  qside(FOX_W), qside(LANES), qside(LANES), qside(LANES),
                  pl.BlockSpec((None, 8, tq), lambda b, j, i: (b, 0, j))],
        out_specs=[kspec, kspec, pl.BlockSpec((None, 8, tq), lambda b, j, i: (b, 0, j))],
        scratch_shapes=[pltpu.VMEM((tq, FOX_W), F32), pltpu.VMEM((tq, FOX_W), F32), pltpu.VMEM((8, tq), F32)],
        compiler_params=_params(("parallel", "parallel", "arbitrary")),
    )(pm3, pm3, pm3, do, delta, lse, cum, cumT)


def _with_ones(x):
    lane = lax.broadcasted_iota(jnp.int32, (x.shape[0], HEAD_DIM), 1)
    return jnp.concatenate([x, jnp.where(lane == 0, 1.0, 0.0).astype(x.dtype)], axis=1)


def _causal_strip(s, r):
    qpos = r + lax.broadcasted_iota(jnp.int32, s.shape, 0)
    kpos = lax.broadcasted_iota(jnp.int32, s.shape, 1)
    return jnp.where(kpos <= qpos, s, NEG_INF)


NT = (((1,), (1,)), ((), ()))
NN = (((1,), (0,)), ((), ()))
TN = (((0,), (0,)), ((), ()))


def _fox_fwd(pm3, cumT, qcol, kcol, vcol, tq, rider=None):
    B, S, _ = pm3.shape
    nq = S // tq
    strips = range(0, tq, FOX_STRIP)

    def body(q_ref, k_ref, v_ref, ck_ref, o_ref, o32_ref, lse_ref, s_sc, p_sc, al_sc, m_sc, acc_sc):
        qi, kj = pl.program_id(1), pl.program_id(2)

        @pl.when(kj == 0)
        def _():
            m_sc[...] = jnp.full_like(m_sc, NEG_INF)
            acc_sc[...] = jnp.zeros_like(acc_sc)

        def tile(diagonal):
            def scores(h):
                hs = slice(HEAD_DIM * h, HEAD_DIM * (h + 1))
                s_sc[h % 2] = lax.dot_general(q_ref[:, hs] * SCALE, k_ref[:, hs], NT, preferred_element_type=F32)

            def accumulate(h):
                hs = slice(HEAD_DIM * h, HEAD_DIM * (h + 1))
                acc_sc[h] = al_sc[h % 2] * acc_sc[h] + lax.dot_general(p_sc[h % 2], _with_ones(v_ref[:, hs]), NN,
                                                                       preferred_element_type=F32)

            scores(0)
            for h in range(FOX_HEADS):
                b = h % 2
                if h + 1 < FOX_HEADS:
                    scores(h + 1)
                if h >= 1:
                    accumulate(h - 1)
                ck = ck_ref[h:h + 1, :]
                for r in strips:
                    rows = slice(r, r + FOX_STRIP)
                    s = s_sc[b, rows, :] - ck
                    if diagonal:
                        s = _causal_strip(s, r)
                    m_prev = m_sc[h, rows, :]
                    m_new = jnp.maximum(m_prev, jnp.max(s, axis=-1, keepdims=True))
                    p_sc[b, rows, :] = jnp.exp(s - m_new).astype(BF16)
                    al_sc[b, rows, :] = jnp.exp(m_prev - m_new)
                    m_sc[h, rows, :] = m_new
            accumulate(FOX_HEADS - 1)

        @pl.when(kj < qi)
        def _():
            tile(False)

        @pl.when(kj == qi)
        def _():
            tile(True)

        @pl.when(kj == nq - 1)
        def _():
            lse_ref[...] = jnp.zeros_like(lse_ref)
            for h in range(FOX_HEADS):
                hs = slice(HEAD_DIM * h, HEAD_DIM * (h + 1))
                acc = acc_sc[h]
                l = acc[:, HEAD_DIM:HEAD_DIM + 1]
                oh = acc[:, :HEAD_DIM] / l
                o_ref[:, hs] = oh.astype(o_ref.dtype)
                o32_ref[:, hs] = oh
                lse_ref[:, h:h + 1] = m_sc[h] + jnp.log(l)

    ospec = pl.BlockSpec((None, tq, FOX_W), lambda b, i, j: (b, i, 0))
    return _pcall(
        body, name="fox_forward",
        out_shape=[jax.ShapeDtypeStruct((B, S, FOX_W), BF16), jax.ShapeDtypeStruct((B, S, FOX_W), F32),
                   jax.ShapeDtypeStruct((B, S, LANES), F32)],
        grid=(B, nq, nq),
        in_specs=[pl.BlockSpec((None, tq, FOX_W), lambda b, i, j: (b, i, qcol)),
                  pl.BlockSpec((None, tq, FOX_W), lambda b, i, j: (b, jnp.minimum(i, j), kcol)),
                  pl.BlockSpec((None, tq, FOX_W), lambda b, i, j: (b, jnp.minimum(i, j), vcol)),
                  pl.BlockSpec((None, 8, tq), lambda b, i, j: (b, 0, jnp.minimum(i, j)))],
        operands=[pm3, pm3, pm3, cumT],
        out_specs=[ospec, ospec, pl.BlockSpec((None, tq, LANES), lambda b, i, j: (b, i, 0))],
        scratch_shapes=[pltpu.VMEM((2, tq, tq), F32), pltpu.VMEM((2, tq, tq), BF16), pltpu.VMEM((2, tq, 1), F32),
                        pltpu.VMEM((FOX_HEADS, tq, 1), F32), pltpu.VMEM((FOX_HEADS, tq, LANES), F32)],
        params=_params(("parallel", "parallel", "arbitrary")), rider=rider)


def _fox_bwd(pm3, do, delta, lse, cumT, qcol, kcol, vcol, tq, rider=None):
    B, S, _ = pm3.shape
    nq = S // tq
    strips = range(0, tq, FOX_STRIP)

    def body(q_ref, k_ref, v_ref, do_ref, dl_ref, lse_ref, ck_ref, dq_ref, rs_ref, dk_ref, dv_ref, cs_ref,
             s_sc, dp_sc, p_sc, ds_sc, dq_sc, dk_sc, dv_sc):
        kj, qi = pl.program_id(1), pl.program_id(2)

        @pl.when((kj == 0) & (qi == 0))
        def _():
            dq_sc[...] = jnp.zeros_like(dq_sc)

        @pl.when(qi == 0)
        def _():
            dk_sc[...] = jnp.zeros_like(dk_sc)
            dv_sc[...] = jnp.zeros_like(dv_sc)

        def tile(diagonal):
            qrows = pl.ds(pl.multiple_of(qi * tq, tq), tq)
            for h in range(FOX_HEADS):
                hs = slice(HEAD_DIM * h, HEAD_DIM * (h + 1))
                qh, kh, doh = q_ref[:, hs] * SCALE, k_ref[:, hs], do_ref[:, hs]
                b = h % 2
                s_sc[b] = lax.dot_general(qh, kh, NT, preferred_element_type=F32)
                dp_sc[b] = lax.dot_general(doh, v_ref[:, hs], NT, preferred_element_type=F32)
                ck = ck_ref[h:h + 1, :]
                for r in strips:
                    rows = slice(r, r + FOX_STRIP)
                    s = s_sc[b, rows, :] - ck
                    if diagonal:
                        s = _causal_strip(s, r)
                    p = jnp.exp(s - lse_ref[rows, h:h + 1])
                    p_sc[b, rows, :] = p.astype(BF16)
                    ds_sc[b, rows, :] = (p * (dp_sc[b, rows, :] - dl_ref[rows, h:h + 1])).astype(BF16)
                dv_sc[:, hs] += lax.dot_general(p_sc[b], doh, TN, preferred_element_type=F32)
                dk_sc[h] += lax.dot_general(ds_sc[b], _with_ones(qh), TN, preferred_element_type=F32)
                dq_sc[h, qrows, :] += lax.dot_general(ds_sc[b], _with_ones(kh), NN, preferred_element_type=F32)

        @pl.when(qi > kj)
        def _():
            tile(False)

        @pl.when(qi == kj)
        def _():
            tile(True)

        @pl.when(qi == nq - 1)
        def _():
            dv_ref[...] = dv_sc[...].astype(dv_ref.dtype)
            cs_ref[...] = jnp.zeros_like(cs_ref)
            for h in range(FOX_HEADS):
                hs = slice(HEAD_DIM * h, HEAD_DIM * (h + 1))
                dk = dk_sc[h]
                dk_ref[:, hs] = dk[:, :HEAD_DIM].astype(dk_ref.dtype)
                cs_ref[:, h:h + 1] = dk[:, HEAD_DIM:HEAD_DIM + 1]

        @pl.when((kj == nq - 1) & (qi == nq - 1))
        def _():
            rs_ref[...] = jnp.zeros_like(rs_ref)
            for h in range(FOX_HEADS):
                hs = slice(HEAD_DIM * h, HEAD_DIM * (h + 1))
                dq_ref[:, hs] = (dq_sc[h, :, :HEAD_DIM] * SCALE).astype(dq_ref.dtype)
                rs_ref[:, h:h + 1] = dq_sc[h, :, HEAD_DIM:HEAD_DIM + 1]

    def qside(width, col=0):
        return pl.BlockSpec((None, tq, width), lambda b, j, i: (b, jnp.maximum(i, j), col))

    kspec = pl.BlockSpec((None, tq, FOX_W), lambda b, j, i: (b, j, 0))
    return _pcall(
        body, name="fox_backward",
        out_shape=[jax.ShapeDtypeStruct((B, S, FOX_W), BF16), jax.ShapeDtypeStruct((B, S, LANES), F32),
                   jax.ShapeDtypeStruct((B, S, FOX_W), BF16), jax.ShapeDtypeStruct((B, S, FOX_W), BF16),
                   jax.ShapeDtypeStruct((B, S, LANES), F32)],
        grid=(B, nq, nq),
        in_specs=[qside(FOX_W, qcol),
                  pl.BlockSpec((None, tq, FOX_W), lambda b, j, i: (b, j, kcol)),
                  pl.BlockSpec((None, tq, FOX_W), lambda b, j, i: (b, j, vcol)),
                  qside(FOX_W), qside(LANES), qside(LANES),
                  pl.BlockSpec((None, 8, tq), lambda b, j, i: (b, 0, j))],
        operands=[pm3, pm3, pm3, do, delta, lse, cumT],
        out_specs=[pl.BlockSpec((None, S, FOX_W), lambda b, j, i: (b, 0, 0)),
                   pl.BlockSpec((None, S, LANES), lambda b, j, i: (b, 0, 0)),
                   kspec, kspec, pl.BlockSpec((None, tq, LANES), lambda b, j, i: (b, j, 0))],
        scratch_shapes=[pltpu.VMEM((2, tq, tq), F32), pltpu.VMEM((2, tq, tq), F32), pltpu.VMEM((2, tq, tq), BF16),
                        pltpu.VMEM((2, tq, tq), BF16), pltpu.VMEM((FOX_HEADS, S, LANES), F32),
                        pltpu.VMEM((FOX_HEADS, tq, LANES), F32), pltpu.VMEM((tq, FOX_W), F32)],
        params=_params(("parallel", "arbitrary", "arbitrary")), rider=rider)


def _fox_delta(do, o32, T, S):
    def fn(dov, ov):
        prod = dov.astype(F32) * ov
        lane = lax.broadcasted_iota(jnp.int32, (dov.shape[0], LANES), 1)
        delta = jnp.zeros((dov.shape[0], LANES), F32)
        for h in range(FOX_HEADS):
            hs = slice(HEAD_DIM * h, HEAD_DIM * (h + 1))
            delta = jnp.where(lane == h, jnp.sum(prod[:, hs], axis=-1, keepdims=True), delta)
        return [delta]

    return _rowwise("fox_delta", fn, T, _div(S, 512, 8), [(do, "row", None), (o32, "row", None)], [("row", LANES, F32)], S)[0]


def _alibi_slope(group, head):
    return 2.0 ** (-ALIBI_MAX_BIAS * (group * DIL_HPG + head + 1) / (N_DIL * DIL_HPG))


def _dil_tiles(q, k_cur, k_prev, slope, dilation, has_prev):
    qi = lax.broadcasted_iota(jnp.int32, (BLOCK, BLOCK), 0)
    ki = lax.broadcasted_iota(jnp.int32, (BLOCK, BLOCK), 1)
    rel = (qi - ki).astype(F32)
    nt = (((1,), (1,)), ((), ()))
    s_cur = lax.dot_general(q, k_cur, nt, preferred_element_type=F32) * SCALE - (slope * dilation) * rel
    s_cur = jnp.where(ki <= qi, s_cur, NEG_INF)
    s_prev = lax.dot_general(q, k_prev, nt, preferred_element_type=F32) * SCALE - (slope * dilation) * (rel + BLOCK)
    s_prev = jnp.where((ki >= qi) & has_prev, s_prev, NEG_INF)
    return s_cur, s_prev


def _dil_forward(group, pmv, nmb, qa_blk, B, S):
    _, dilation = DIL_GROUPS[group]
    sub = S // dilation
    nb = sub // BLOCK
    qb, kb, vb = qa_blk + group, qa_blk + N_DIL + group, qa_blk + 2 * N_DIL + group

    def body(q_ref, kc_ref, kp_ref, vc_ref, vp_ref, o_ref, lse_ref):
        has_prev = pl.program_id(2) > 0
        lse_ref[...] = jnp.zeros_like(lse_ref)
        for h in range(DIL_HPG):
            hs = slice(HEAD_DIM * h, HEAD_DIM * (h + 1))
            s_cur, s_prev = _dil_tiles(q_ref[:, hs], kc_ref[:, hs], kp_ref[:, hs], _alibi_slope(group, h), dilation, has_prev)
            m = jnp.maximum(jnp.max(s_cur, axis=-1, keepdims=True), jnp.max(s_prev, axis=-1, keepdims=True))
            p_cur, p_prev = jnp.exp(s_cur - m), jnp.exp(s_prev - m)
            l = jnp.sum(p_cur, axis=-1, keepdims=True) + jnp.sum(p_prev, axis=-1, keepdims=True)
            nn = (((1,), (0,)), ((), ()))
            o = (lax.dot_general(p_cur.astype(BF16), vc_ref[:, hs], nn, preferred_element_type=F32)
                 + lax.dot_general(p_prev.astype(BF16), vp_ref[:, hs], nn, preferred_element_type=F32))
            o_ref[:, hs] = o / l
            lse_ref[:, h:h + 1] = m + jnp.log(l)

    def cur(col):
        return pl.BlockSpec((None, BLOCK, DIL_GW), lambda b, r, n: (b, n, r * nmb + col))

    def prev(col):
        return pl.BlockSpec((None, BLOCK, DIL_GW), lambda b, r, n: (b, jnp.maximum(n - 1, 0), r * nmb + col))

    return pl.pallas_call(
        body, name=f"dil_forward_{group}",
        out_shape=[jax.ShapeDtypeStruct((B, sub, dilation * DIL_GW), F32), jax.ShapeDtypeStruct((B, sub, dilation * LANES), F32)],
        grid=(B, dilation, nb),
        in_specs=[cur(qb), cur(kb), prev(kb), cur(vb), prev(vb)],
        out_specs=[pl.BlockSpec((None, BLOCK, DIL_GW), lambda b, r, n: (b, n, r)),
                   pl.BlockSpec((None, BLOCK, LANES), lambda b, r, n: (b, n, r))],
        compiler_params=_params(("parallel", "parallel", "arbitrary")),
    )(pmv, pmv, pmv, pmv, pmv)


def _residue_order(a, B, S, d):
    C = a.shape[-1]
    if d == 1:
        return a.reshape(B, S, C)
    return a.reshape(B, S // d, d, C).transpose(0, 2, 1, 3).reshape(B * d, S // d, C)


def _token_order(a, B, S, d):
    C = a.shape[-1]
    if d == 1:
        return a.reshape(B * S, C)
    return a.reshape(B, d, S // d, C).transpose(0, 2, 1, 3).reshape(B * S, C)


def _band_scores(qh, kcat, slope_d, has_prev):
    qi = lax.broadcasted_iota(jnp.int32, (BLOCK, 2 * BLOCK), 0)
    c = lax.broadcasted_iota(jnp.int32, (BLOCK, 2 * BLOCK), 1)
    s = lax.dot_general(qh, kcat, NT, preferred_element_type=F32) - slope_d * (BLOCK + qi - c).astype(F32)
    valid = (c >= qi) & (c <= qi + BLOCK)
    if has_prev is not None:
        valid = valid & ((c >= BLOCK) | has_prev)
    return jnp.where(valid, s, NEG_INF)


def _band_operands(j, cur_ref, prev_ref, hs):
    if j == 0:
        return jnp.concatenate([prev_ref[:, hs], cur_ref[0:BLOCK, hs]], axis=0)
    return cur_ref[(j - 1) * BLOCK:(j + 1) * BLOCK, hs]


def _dil_specs(Ls, qb, cols):
    nsub = qb // BLOCK
    qcol, kcol, vcol = cols

    def cur(col):
        return pl.BlockSpec((None, qb, DIL_GW), lambda s, n: (s, n, col))

    def prev(col):
        return pl.BlockSpec((None, BLOCK, DIL_GW), lambda s, n: (s, jnp.maximum(n * nsub - 1, 0), col))

    return [cur(qcol), cur(kcol), prev(kcol), cur(vcol), prev(vcol)]


def _dil_fwd(group, src, cols):
    _, dilation = DIL_GROUPS[group]
    nseq, Ls, _ = src.shape
    qb = _div(Ls, 512, BLOCK)
    nsub = qb // BLOCK

    def body(q_ref, kc_ref, kp_ref, vc_ref, vp_ref, o_ref, lse_ref):
        has_prev = pl.program_id(1) > 0
        lse_ref[...] = jnp.zeros_like(lse_ref)
        for h in range(DIL_HPG):
            hs = slice(HEAD_DIM * h, HEAD_DIM * (h + 1))
            for j in range(nsub):
                rows = slice(j * BLOCK, (j + 1) * BLOCK)
                s = _band_scores(q_ref[rows, hs] * SCALE, _band_operands(j, kc_ref, kp_ref, hs),
                                 _alibi_slope(group, h) * dilation, has_prev if j == 0 else None)
                m = jnp.max(s, axis=-1, keepdims=True)
                p = jnp.exp(s - m).astype(BF16)
                acc = lax.dot_general(p, _with_ones(_band_operands(j, vc_ref, vp_ref, hs)), NN, preferred_element_type=F32)
                l = acc[:, HEAD_DIM:HEAD_DIM + 1]
                o_ref[rows, hs] = acc[:, :HEAD_DIM] / l
                lse_ref[rows, h:h + 1] = m + jnp.log(l)

    return pl.pallas_call(
        body, name=f"dil_forward_{group}",
        out_shape=[jax.ShapeDtypeStruct((nseq, Ls, DIL_GW), F32), jax.ShapeDtypeStruct((nseq, Ls, LANES), F32)],
        grid=(nseq, Ls // qb),
        in_specs=_dil_specs(Ls, qb, cols),
        out_specs=[pl.BlockSpec((None, qb, DIL_GW), lambda s, n: (s, n, 0)),
                   pl.BlockSpec((None, qb, LANES), lambda s, n: (s, n, 0))],
        compiler_params=_params(("parallel", "arbitrary")),
    )(src, src, src, src, src)


def _dil_bwd(group, src, cols, Lr, dyr, dlr):
    _, dilation = DIL_GROUPS[group]
    nseq, Ls, _ = src.shape
    qb = _div(Ls, 512, BLOCK)
    nsub, nb = qb // BLOCK, Ls // qb

    def body(q_ref, kc_ref, kp_ref, vc_ref, vp_ref, L_ref, dy_ref, dl_ref, dq_ref, dk_ref, dv_ref, dk_sc, dv_sc):
        n = pl.program_id(1)
        has_prev = n > 0

        @pl.when(n == 0)
        def _():
            dk_sc[...] = jnp.zeros_like(dk_sc)
            dv_sc[...] = jnp.zeros_like(dv_sc)

        base = pl.multiple_of(n * qb, BLOCK)
        for h in range(DIL_HPG):
            hs = slice(HEAD_DIM * h, HEAD_DIM * (h + 1))
            for j in range(nsub):
                rows = slice(j * BLOCK, (j + 1) * BLOCK)
                qh = q_ref[rows, hs] * SCALE
                kcat = _band_operands(j, kc_ref, kp_ref, hs)
                s = _band_scores(qh, kcat, _alibi_slope(group, h) * dilation, has_prev if j == 0 else None)
                p = jnp.exp(s - L_ref[rows, h:h + 1])
                dyh = dy_ref[rows, hs]
                dp = lax.dot_general(dyh, _band_operands(j, vc_ref, vp_ref, hs), NT, preferred_element_type=F32)
                ds = (p * (dp - dl_ref[rows, h:h + 1])).astype(BF16)
                dq_ref[rows, hs] = (lax.dot_general(ds, kcat, NN, preferred_element_type=F32) * SCALE).astype(dq_ref.dtype)
                win = pl.ds(base + j * BLOCK, 2 * BLOCK)
                dk_sc[win, hs] += lax.dot_general(ds, qh, TN, preferred_element_type=F32)
                dv_sc[win, hs] += lax.dot_general(p.astype(BF16), dyh, TN, preferred_element_type=F32)

        @pl.when(n == nb - 1)
        def _():
            dk_ref[...] = dk_sc[BLOCK:, :].astype(dk_ref.dtype)
            dv_ref[...] = dv_sc[BLOCK:, :].astype(dv_ref.dtype)

    own = pl.BlockSpec((None, qb, DIL_GW), lambda s, n: (s, n, 0))
    own128 = pl.BlockSpec((None, qb, LANES), lambda s, n: (s, n, 0))
    whole = pl.BlockSpec((None, Ls, DIL_GW), lambda s, n: (s, 0, 0))
    shape = jax.ShapeDtypeStruct((nseq, Ls, DIL_GW), BF16)
    return pl.pallas_call(
        body, name=f"dil_backward_{group}",
        out_shape=[shape, shape, shape],
        grid=(nseq, nb),
        in_specs=_dil_specs(Ls, qb, cols) + [own128, own, own128],
        out_specs=[own, whole, whole],
        scratch_shapes=[pltpu.VMEM((Ls + BLOCK, DIL_GW), F32), pltpu.VMEM((Ls + BLOCK, DIL_GW), F32)],
        compiler_params=_params(("parallel", "arbitrary")),
    )(src, src, src, src, src, Lr, dyr, dlr)


def _dil_combine(os_, lses, T, S):
    def fn(o0, o1, o2, l0, l1, l2):
        m = jnp.maximum(jnp.maximum(l0, l1), l2)
        e0, e1, e2 = jnp.exp(l0 - m), jnp.exp(l1 - m), jnp.exp(l2 - m)
        tot = e0 + e1 + e2
        w0, w1, w2 = e0 / tot, e1 / tot, e2 / tot
        parts = []
        for h in range(DIL_HPG):
            hs = slice(HEAD_DIM * h, HEAD_DIM * (h + 1))
            parts.append(w0[:, h:h + 1] * o0[:, hs] + w1[:, h:h + 1] * o1[:, hs] + w2[:, h:h + 1] * o2[:, hs])
        return [jnp.concatenate(parts, axis=1), m + jnp.log(tot)]

    ins = [(a, "row", None) for a in os_] + [(a, "row", None) for a in lses]
    return _rowwise("dil_combine", fn, T, _div(S, 512, 8), ins, [("row", DIL_GW, BF16), ("row", LANES, F32)], S)


def _dil_delta(dy, y, T, S):
    def fn(dyv, yv):
        prod = dyv * yv.astype(F32)
        lane = lax.broadcasted_iota(jnp.int32, (dyv.shape[0], LANES), 1)
        delta = jnp.zeros((dyv.shape[0], LANES), F32)
        for h in range(DIL_HPG):
            hs = slice(HEAD_DIM * h, HEAD_DIM * (h + 1))
            delta = jnp.where(lane == h, jnp.sum(prod[:, hs], axis=-1, keepdims=True), delta)
        return [delta, dyv]

    return _rowwise("dil_delta", fn, T, _div(S, 512, 8), [(dy, "row", None), (y, "row", None)],
                    [("row", LANES, F32), ("row", DIL_GW, BF16)], S)


def _dil_dq(group, pmv, nmb, qa_blk, Lv, dyv, deltav, B, S):
    _, dilation = DIL_GROUPS[group]
    sub = S // dilation
    nb = sub // BLOCK
    qb, kb, vb = qa_blk + group, qa_blk + N_DIL + group, qa_blk + 2 * N_DIL + group

    def body(q_ref, kc_ref, kp_ref, vc_ref, vp_ref, L_ref, dy_ref, dl_ref, dq_ref):
        has_prev = pl.program_id(2) > 0
        nt = (((1,), (1,)), ((), ()))
        nn = (((1,), (0,)), ((), ()))
        for h in range(DIL_HPG):
            hs = slice(HEAD_DIM * h, HEAD_DIM * (h + 1))
            s_cur, s_prev = _dil_tiles(q_ref[:, hs], kc_ref[:, hs], kp_ref[:, hs], _alibi_slope(group, h), dilation, has_prev)
            L, delta, dyh = L_ref[:, h:h + 1], dl_ref[:, h:h + 1], dy_ref[:, hs]
            ds_cur = jnp.exp(s_cur - L) * (lax.dot_general(dyh, vc_ref[:, hs], nt, preferred_element_type=F32) - delta)
            ds_prev = jnp.exp(s_prev - L) * (lax.dot_general(dyh, vp_ref[:, hs], nt, preferred_element_type=F32) - delta)
            dq = (lax.dot_general(ds_cur.astype(BF16), kc_ref[:, hs], nn, preferred_element_type=F32)
                  + lax.dot_general(ds_prev.astype(BF16), kp_ref[:, hs], nn, preferred_element_type=F32))
            dq_ref[:, hs] = (dq * SCALE).astype(dq_ref.dtype)

    def cur(col):
        return pl.BlockSpec((None, BLOCK, DIL_GW), lambda b, r, n: (b, n, r * nmb + col))

    def prev(col):
        return pl.BlockSpec((None, BLOCK, DIL_GW), lambda b, r, n: (b, jnp.maximum(n - 1, 0), r * nmb + col))

    own = pl.BlockSpec((None, BLOCK, DIL_GW), lambda b, r, n: (b, n, r))
    own128 = pl.BlockSpec((None, BLOCK, LANES), lambda b, r, n: (b, n, r))
    return pl.pallas_call(
        body, name=f"dil_dq_{group}",
        out_shape=jax.ShapeDtypeStruct((B, sub, dilation * DIL_GW), BF16),
        grid=(B, dilation, nb),
        in_specs=[cur(qb), cur(kb), prev(kb), cur(vb), prev(vb), own128, own, own128],
        out_specs=own,
        compiler_params=_params(("parallel", "parallel", "arbitrary")),
    )(pmv, pmv, pmv, pmv, pmv, Lv, dyv, deltav)


def _dil_dkv(group, pmv, nmb, qa_blk, Lv, dyv, deltav, B, S):
    _, dilation = DIL_GROUPS[group]
    sub = S // dilation
    nb = sub // BLOCK
    qb, kb, vb = qa_blk + group, qa_blk + N_DIL + group, qa_blk + 2 * N_DIL + group

    def body(k_ref, v_ref, q0_ref, q1_ref, L0_ref, L1_ref, dy0_ref, dy1_ref, dl0_ref, dl1_ref, dk_ref, dv_ref):
        has_next = pl.program_id(2) < nb - 1
        qi = lax.broadcasted_iota(jnp.int32, (BLOCK, BLOCK), 0)
        ki = lax.broadcasted_iota(jnp.int32, (BLOCK, BLOCK), 1)
        rel = (qi - ki).astype(F32)
        nt = (((1,), (1,)), ((), ()))
        tn = (((0,), (0,)), ((), ()))
        for h in range(DIL_HPG):
            hs = slice(HEAD_DIM * h, HEAD_DIM * (h + 1))
            bias = _alibi_slope(group, h) * dilation
            kh, vh, q0, q1 = k_ref[:, hs], v_ref[:, hs], q0_ref[:, hs], q1_ref[:, hs]
            s0 = lax.dot_general(q0, kh, nt, preferred_element_type=F32) * SCALE - bias * rel
            s0 = jnp.where(ki <= qi, s0, NEG_INF)
            s1 = lax.dot_general(q1, kh, nt, preferred_element_type=F32) * SCALE - bias * (rel + BLOCK)
            s1 = jnp.where((ki >= qi) & has_next, s1, NEG_INF)
            p0 = jnp.exp(s0 - L0_ref[:, h:h + 1])
            p1 = jnp.exp(s1 - L1_ref[:, h:h + 1])
            dy0, dy1 = dy0_ref[:, hs], dy1_ref[:, hs]
            ds0 = p0 * (lax.dot_general(dy0, vh, nt, preferred_element_type=F32) - dl0_ref[:, h:h + 1])
            ds1 = p1 * (lax.dot_general(dy1, vh, nt, preferred_element_type=F32) - dl1_ref[:, h:h + 1])
            dv = (lax.dot_general(p0.astype(BF16), dy0, tn, preferred_element_type=F32)
                  + lax.dot_general(p1.astype(BF16), dy1, tn, preferred_element_type=F32))
            dk = (lax.dot_general(ds0.astype(BF16), q0, tn, preferred_element_type=F32)
                  + lax.dot_general(ds1.astype(BF16), q1, tn, preferred_element_type=F32))
            dv_ref[:, hs] = dv.astype(dv_ref.dtype)
            dk_ref[:, hs] = (dk * SCALE).astype(dk_ref.dtype)

    def cur(col):
        return pl.BlockSpec((None, BLOCK, DIL_GW), lambda b, r, n: (b, n, r * nmb + col))

    def nxt(col):
        return pl.BlockSpec((None, BLOCK, DIL_GW), lambda b, r, n: (b, jnp.minimum(n + 1, nb - 1), r * nmb + col))

    own = pl.BlockSpec((None, BLOCK, DIL_GW), lambda b, r, n: (b, n, r))
    own_next = pl.BlockSpec((None, BLOCK, DIL_GW), lambda b, r, n: (b, jnp.minimum(n + 1, nb - 1), r))
    own128 = pl.BlockSpec((None, BLOCK, LANES), lambda b, r, n: (b, n, r))
    own128_next = pl.BlockSpec((None, BLOCK, LANES), lambda b, r, n: (b, jnp.minimum(n + 1, nb - 1), r))
    shape = jax.ShapeDtypeStruct((B, sub, dilation * DIL_GW), BF16)
    return pl.pallas_call(
        body, name=f"dil_dkv_{group}",
        out_shape=[shape, shape],
        grid=(B, dilation, nb),
        in_specs=[cur(kb), cur(vb), cur(qb), nxt(qb), own128, own128_next, own, own_next, own128, own128_next],
        out_specs=[own, own],
        compiler_params=_params(("parallel", "parallel", "arbitrary")),
    )(pmv, pmv, pmv, pmv, Lv, Lv, dyv, dyv, deltav, deltav)


def _ada_forward(c_all, w, b):
    n, D = c_all.shape
    cl = w.shape[1]

    def body(c_ref, w_ref, b_ref, o_ref, ca_ref):
        cv = c_ref[...]
        ca = (cv * _sigmoid(cv)).astype(BF16)
        ca_ref[...] = ca
        o_ref[...] = jnp.dot(ca, w_ref[...].astype(BF16), preferred_element_type=F32) + b_ref[...]

    return pl.pallas_call(
        body, name="ada_forward",
        out_shape=[jax.ShapeDtypeStruct((n, cl), F32), jax.ShapeDtypeStruct((n, D), BF16)],
        compiler_params=_params(),
    )(c_all, w, b)


def _ada_backward(ca, dmod_cols, dmod_all):
    n, D = ca.shape
    cl = dmod_cols.shape[1]

    def body(ca_ref, dc_ref, da_ref, gw_ref, gb_ref):
        gw_ref[...] = lax.dot_general(ca_ref[...], dc_ref[...].astype(BF16), (((0,), (0,)), ((), ())), preferred_element_type=F32)
        gb_ref[...] = _colsum(da_ref[...])

    return pl.pallas_call(
        body, name="ada_backward",
        out_shape=[jax.ShapeDtypeStruct((D, cl), F32), jax.ShapeDtypeStruct((1, dmod_all.shape[1]), F32)],
        compiler_params=_params(),
    )(ca, dmod_cols, dmod_all)


def _sum_devices(v):
    def body(v_ref, o_ref):
        tot = v_ref[0]
        for k in range(1, N_DEV):
            tot = tot + v_ref[k]
        o_ref[...] = tot

    return pl.pallas_call(body, name="sum_devices", out_shape=jax.ShapeDtypeStruct(v.shape[1:], F32))(v)


def _adamw(name, w, g, m, v):
    rows, cols = w.shape
    tr = _div(rows, 256, 8)

    def body(w_ref, g_ref, m_ref, v_ref, d_ref, nm_ref, nv_ref):
        gv = g_ref[...]
        nm = ADAM_B1 * m_ref[...] + (1.0 - ADAM_B1) * gv
        nv = ADAM_B2 * v_ref[...] + (1.0 - ADAM_B2) * (gv * gv)
        m_hat = nm / (1.0 - ADAM_B1 ** ADAM_STEP)
        v_hat = nv / (1.0 - ADAM_B2 ** ADAM_STEP)
        d_ref[...] = -ADAM_LR * (m_hat / (jnp.sqrt(v_hat) + ADAM_EPS) + ADAM_WD * w_ref[...])
        nm_ref[...] = nm
        nv_ref[...] = nv

    spec = pl.BlockSpec((tr, cols), lambda i: (i, 0))
    shape = jax.ShapeDtypeStruct((rows, cols), F32)
    return pl.pallas_call(
        body, name=name, out_shape=[shape, shape, shape], grid=(rows // tr,),
        in_specs=[spec, spec, spec, spec], out_specs=[spec, spec, spec],
        compiler_params=_params(("arbitrary",)),
    )(w, g, m, v)


def _pad_rows(a, rows):
    return a if a.shape[0] == rows else jnp.pad(a, ((0, rows - a.shape[0]), (0, 0)))


class _Packed:
    def __init__(self, kind, local_shape, D):
        self.kind, self.local_shape, self.D = kind, local_shape, D
        r, c = local_shape
        self.rows = {"T": c, "N": r, "F": r * c // D}[kind]
        self.rows_pad = -(-self.rows // ROW_ALIGN) * ROW_ALIGN

    def pack_local(self, w):
        if self.kind == "T":
            w = w.T
        elif self.kind == "F":
            w = w.reshape(self.rows, self.D)
        return _pad_rows(w, self.rows_pad)

    def full(self, gathered):
        g = gathered[:, :self.rows]
        if self.kind == "F":
            r, c = self.local_shape
            return g.reshape(N_DEV, r, c).transpose(1, 0, 2).reshape(r, N_DEV * c)
        return g.reshape(N_DEV * self.rows, self.D)

    def pack_grad(self, gfull):
        if self.kind == "F":
            r, c = self.local_shape
            g = gfull.reshape(r, N_DEV, c).transpose(1, 0, 2).reshape(N_DEV, self.rows, self.D)
        else:
            g = gfull.reshape(N_DEV, self.rows, self.D)
        if self.rows_pad != self.rows:
            g = jnp.pad(g, ((0, 0), (0, self.rows_pad - self.rows), (0, 0)))
        return g

    def unpack_local(self, g):
        g = g[:self.rows]
        if self.kind == "T":
            return g.T
        if self.kind == "F":
            return g.reshape(self.local_shape)
        return g


BIG = ["ffn1_w_gate", "ffn1_w_up", "ffn1_w_down", "w_in", "w_branch_a", "w_branch_b", "w_out",
       "ffn2_w_gate", "ffn2_w_up", "ffn2_w_down"]
BIG_KIND = {"ffn1_w_gate": "T", "ffn1_w_up": "T", "ffn1_w_down": "N", "w_in": "T", "w_branch_a": "F", "w_branch_b": "F",
            "w_out": "N", "ffn2_w_gate": "T", "ffn2_w_up": "T", "ffn2_w_down": "N"}
GROUPS = (("ffn1_w_gate", "ffn1_w_up", "ffn1_w_down"), ("w_in", "w_branch_a", "w_branch_b", "w_out"),
          ("ffn2_w_gate", "ffn2_w_up", "ffn2_w_down"))
SMALL = ["ada_b", "norm_ffn1", "norm_mix", "forget_bias", "norm_ffn2", "norm_final"]


def kernel(x, c, ada_w, ada_b, norm_ffn1, ffn1_w_gate, ffn1_w_up, ffn1_w_down, norm_mix, w_in, forget_bias, w_branch_a, w_branch_b, w_out, norm_ffn2, ffn2_w_gate, ffn2_w_up, ffn2_w_down, norm_final, loss_target, m_ada_w, m_ada_b, m_norm_ffn1, m_ffn1_w_gate, m_ffn1_w_up, m_ffn1_w_down, m_norm_mix, m_w_in, m_forget_bias, m_w_branch_a, m_w_branch_b, m_w_out, m_norm_ffn2, m_ffn2_w_gate, m_ffn2_w_up, m_ffn2_w_down, m_norm_final, v_ada_w, v_ada_b, v_norm_ffn1, v_ffn1_w_gate, v_ffn1_w_up, v_ffn1_w_down, v_norm_mix, v_w_in, v_forget_bias, v_w_branch_a, v_w_branch_b, v_w_out, v_norm_ffn2, v_ffn2_w_gate, v_ffn2_w_up, v_ffn2_w_down, v_norm_final):
    args = dict(locals())
    B, S, D = x.shape
    T = B * S
    cl = ada_w.shape[2]
    n_in = w_in.shape[2] * N_DEV
    nm = 2 * D + 3 * FOX_W + 3 * DIL_W
    nmp = -(-nm // 512) * 512
    GA, GB, QB, QA = 0, D, 2 * D, 2 * D + 3 * FOX_W
    xpos, ypos, cpos = _position()
    me = 4 * xpos + 2 * ypos + cpos

    packs = {n: _Packed(BIG_KIND[n], args[n].shape[1:], D) for n in BIG}
    offs, pads = {}, {}
    for names in GROUPS:
        r = 0
        for n in names:
            offs[n] = r
            r += packs[n].rows_pad
        pads[names] = -r % PACK_ROW_QUANTUM

    def pack_weights(names):
        return jnp.concatenate([packs[n].pack_local(args[n][0]).astype(BF16) for n in names]
                               + [jnp.zeros((pads[names], D), BF16)], axis=0)

    def unpack_weights(names, land):
        return {n: packs[n].full(land[:, offs[n]:offs[n] + packs[n].rows_pad]) for n in names}

    def pack_grads(names, gfull):
        return jnp.concatenate([packs[n].pack_grad(gfull[n]) for n in names] + [jnp.zeros((N_DEV, pads[names], D), F32)], axis=1)

    def unpack_grads(names, g_local):
        return {n: packs[n].unpack_local(g_local[offs[n]:offs[n] + packs[n].rows_pad])[None] for n in names}

    W = unpack_weights(GROUPS[0], _weight_allgather(pack_weights(GROUPS[0])))

    c_all = _small_allgather(c, "gather_c").reshape(N_DEV * B, D)
    b_cols = lax.dynamic_slice(ada_b, (0, me * cl), (1, cl))
    mod_cols, c_act = _ada_forward(c_all, ada_w[0], b_cols)
    mod_all = _small_allgather(mod_cols, "gather_mod").transpose(1, 0, 2).reshape(N_DEV * B, N_MOD * D)
    mod = lax.dynamic_slice(mod_all, (me * B, 0), (B, N_MOD * D)).reshape(B, N_MOD, 1, D)
    sh1, sc1, gt1, sh2, sc2, gt2, sh3, sc3, gt3 = [mod[:, i] for i in range(N_MOD)]

    x0 = x.reshape(T, D)
    x1, saved1, land = _ffn_forward("ffn1", x0, norm_ffn1, sh1, sc1, gt1, W["ffn1_w_gate"], W["ffn1_w_up"], W["ffn1_w_down"], S,
                                    gather=pack_weights(GROUPS[1]))
    W.update(unpack_weights(GROUPS[1], land))
    winT = W["w_in"]
    o_f = 3 * DIL_W + 3 * FOX_W
    wmT = jnp.concatenate([winT[o_f + 8:], winT[3 * DIL_W:o_f], winT[:3 * DIL_W], jnp.zeros((nmp - nm, D), BF16)], axis=0)
    wfT = jnp.concatenate([winT[o_f:o_f + 8], jnp.zeros((LANES - 8, D), BF16)], axis=0)

    tm1k = _div(T, 1024, 8)
    fb = jnp.pad(forget_bias, ((0, 0), (0, LANES - FOX_HEADS)))

    def proj(accs, fbv):
        fl = accs[1] + fbv
        lane = lax.broadcasted_iota(jnp.int32, fl.shape, 1)
        ls = jnp.minimum(fl, 0.0) - jnp.log(1.0 + jnp.exp(-jnp.abs(fl)))
        return [accs[0], jnp.where(lane < FOX_HEADS, ls, 0.0), fl]

    tms = _div(S, 512, 8)
    h2, pm, logsig, flog, land = _norm_matmul("mix_proj", x1, norm_mix, sc2, sh2, [wmT, wfT], proj,
                                              [(nmp, BF16), (LANES, F32), (LANES, F32)], S, vecs=[fb],
                                              rider=_gather_direct_rider(pack_weights(GROUPS[2])))
    cum = _cumsum(logsig.reshape(B, S, LANES))
    cumT = cum[:, :, :8].transpose(0, 2, 1)
    pm3 = pm.reshape(B, S, nmp)
    tq = _div(S, 512, LANES)
    qcol, kcol, vcol = QB // FOX_W, QB // FOX_W + 1, QB // FOX_W + 2
    o_b, o_b32, lse_b, land = _fox_fwd(pm3, cumT, qcol, kcol, vcol, tq, rider=_gather_forward_rider(land))
    W.update(unpack_weights(GROUPS[2], land))
    y_b = o_b.reshape(T, FOX_W)

    qa_blk = QA // DIL_GW
    dil_src, dil_cols = [], []
    for g, (_, d) in enumerate(DIL_GROUPS):
        if d == 1:
            dil_src.append(pm3)
            dil_cols.append((qa_blk + g, qa_blk + N_DIL + g, qa_blk + 2 * N_DIL + g))
        else:
            starts = [QA + (i * N_DIL + g) * DIL_GW for i in range(3)]
            qkv = jnp.concatenate([pm[:, c:c + DIL_GW] for c in starts], axis=1)
            dil_src.append(_residue_order(qkv, B, S, d))
            dil_cols.append((0, 1, 2))
    dil_o, dil_lse = [], []
    for g, (_, d) in enumerate(DIL_GROUPS):
        o_g, lse_g = _dil_fwd(g, dil_src[g], dil_cols[g])
        dil_o.append(_token_order(o_g, B, S, d))
        dil_lse.append(_token_order(lse_g, B, S, d))
    y_a, L_a = _dil_combine(dil_o, dil_lse, T, S)

    wa, wb, wout = W["w_branch_a"], W["w_branch_b"], W["w_out"]
    tnd = D
    tm5 = _div(T, 512, 8)
    yap = _matmul("mix_branch_a", "nn", [[(y_a, wa)]], T, D, DIL_GW, tm5, tnd, DIL_GW, [BF16])[0]

    def merge(accs, ex):
        yapv, gav, gbv = ex
        ybp = accs[0]
        return [ybp, _sigmoid(gav.astype(F32)) * yapv.astype(F32) + _sigmoid(gbv.astype(F32)) * ybp]

    ybp, merged = _matmul("mix_branch_b", "nn", [[(y_b, wb)]], T, D, FOX_W, tm5, tnd, FOX_W, [BF16, BF16],
                          extras=[(yap, "tile", 0), (pm, "tile", GA), (pm, "tile", GB)], epilogue=merge)

    def out_proj(accs, ex):
        xv, gtv = ex
        return [xv + gtv * accs[0], accs[0]]

    x2, ymix = _matmul("mix_out", "nn", [[(merged, wout)]], T, D, D, tms, tnd, D, [F32, BF16],
                       extras=[(x1, "tile", 0), (gt2, "brow", 0)], epilogue=out_proj, rows_per_example=S)

    x3, saved3, _ = _ffn_forward("ffn2", x2, norm_ffn2, sh3, sc3, gt3, W["ffn2_w_gate"], W["ffn2_w_up"], W["ffn2_w_down"], S)

    dx3, loss_b, dg_final = _loss_head(x3, loss_target.reshape(T, D), norm_final.reshape(1, D), S)
    dx2, (dsh3, dsc3, dgt3, dg3), (dwg2, dwu2, dwd2), _ = _ffn_backward(
        "ffn2", dx3, saved3, norm_ffn2, sc3, gt3, W["ffn2_w_gate"], W["ffn2_w_up"], W["ffn2_w_down"], S)
    sums3, own3 = _chip_sums(pack_grads(GROUPS[2], {"ffn2_w_gate": dwg2, "ffn2_w_up": dwu2, "ffn2_w_down": dwd2}))

    def merge_grad(dm, ex):
        gav, gbv, yapv, ybpv = [e.astype(F32) for e in ex]
        sga, sgb = _sigmoid(gav), _sigmoid(gbv)
        return [dm * sga, dm * sgb, dm * yapv * sga * (1.0 - sga), dm * ybpv * sgb * (1.0 - sgb)]

    dym, dgt2, dyap, dybp, dga, dgb = _gated_grad_matmul(
        "mix_merge_grad", dx2, ymix, gt2, 1.0, wout, [(pm, GA // D), (pm, GB // D), (yap, 0), (ybp, 0)], merge_grad, [BF16] * 4, S)
    tkw = _div(T, 512, LANES)
    dwout = _matmul("mix_dw_out", "tn", [[(merged, dym)]], D, D, T, D, D, tkw, [F32])[0]
    dwa = _matmul("mix_dw_a", "tn", [[(y_a, dyap)]], DIL_GW, D, T, DIL_GW, D, tkw, [F32])[0]
    dwb = _matmul("mix_dw_b", "tn", [[(y_b, dybp)]], FOX_W, D, T, FOX_W, D, tkw, [F32])[0]
    dy_a = _matmul("mix_dy_a", "nt", [[(dyap, wa)]], T, DIL_GW, D, tm1k, DIL_GW, D, [F32])[0]
    dy_b = _matmul("mix_dy_b", "nt", [[(dybp, wb)]], T, FOX_W, D, tm1k, FOX_W, D, [BF16])[0]

    do3 = dy_b.reshape(B, S, FOX_W)
    delta_b = _fox_delta(dy_b, o_b32.reshape(T, FOX_W), T, S).reshape(B, S, LANES)
    dq_b, ds_rows, dk_b, dv_b, ds_cols, recv3 = _fox_bwd(pm3, do3, delta_b, lse_b, cumT, qcol, kcol, vcol, tq,
                                                         rider=_chip_exchange_rider(sums3))

    delta_a, dy_a16 = _dil_delta(dy_a, y_a, T, S)
    dqs, dks, dvs = [], [], []
    for g, (_, d) in enumerate(DIL_GROUPS):
        dq_g, dk_g, dv_g = _dil_bwd(g, dil_src[g], dil_cols[g], _residue_order(L_a, B, S, d),
                                    _residue_order(dy_a16, B, S, d), _residue_order(delta_a, B, S, d))
        dqs.append(_token_order(dq_g, B, S, d))
        dks.append(_token_order(dk_g, B, S, d))
        dvs.append(_token_order(dv_g, B, S, d))

    dcum = ds_rows - ds_cols
    dcum_run = _cumsum(dcum)
    dcum_tot = dcum_run[:, S - 1:S, :]

    def forget_grad_fn(run, dcv, fl, tot):
        lane = lax.broadcasted_iota(jnp.int32, fl.shape, 1)
        df = jnp.where(lane < FOX_HEADS, (tot - run + dcv) * _sigmoid(-fl), 0.0)
        return [df, _colsum(df)]

    df16, dfb = _rowwise("forget_gate_grad", forget_grad_fn, T, tms,
                         [(dcum_run.reshape(T, LANES), "row", None), (dcum.reshape(T, LANES), "row", None), (flog, "row", None),
                          (dcum_tot, "bvec", None)],
                         [("row", LANES, BF16), ("bacc", LANES, F32)], S)

    dpm = jnp.concatenate([dga, dgb, dq_b.reshape(T, FOX_W), dk_b.reshape(T, FOX_W), dv_b.reshape(T, FOX_W)]
                          + dqs + dks + dvs + ([jnp.zeros((T, nmp - nm), BF16)] if nmp > nm else []), axis=1)
    tmn = _div(nmp, 2048, LANES)
    dwmT = _matmul("mix_dw_in", "tn", [[(dpm, h2)]], nmp, D, T, tmn, D, tkw, [F32])[0]
    dwfT = _matmul("mix_dw_f", "tn", [[(df16, h2)]], LANES, D, T, LANES, D, tkw, [F32])[0]
    dx1, dsh2, dsc2, dgmix = _matmul_normmod_bwd("mix_dh", [(dpm, wmT), (df16, wfT)], x1, norm_mix, sc2, dx2, S)

    dwinT = jnp.concatenate([dwmT[QA:QA + 3 * DIL_W], dwmT[QB:QB + 3 * FOX_W], dwfT[:8], dwmT[GA:2 * D]], axis=0)
    sums2, own2 = _chip_sums(pack_grads(GROUPS[1], {"w_in": dwinT, "w_branch_a": dwa, "w_branch_b": dwb, "w_out": dwout}))

    dx0, (dsh1, dsc1, dgt1, dg1), (dwg1, dwu1, dwd1), (recv2,) = _ffn_backward(
        "ffn1", dx1, saved1, norm_ffn1, sc1, gt1, W["ffn1_w_gate"], W["ffn1_w_up"], W["ffn1_w_down"], S,
        rider=_chip_exchange_rider(sums2))
    grad_x = dx0.reshape(B, S, D)
    sums1, own1 = _chip_sums(pack_grads(GROUPS[0], {"ffn1_w_gate": dwg1, "ffn1_w_up": dwu1, "ffn1_w_down": dwd1}))
    recv1 = _grad_exchange_chips(sums1)

    dmod = jnp.concatenate([dsh1, dsc1, dgt1, dsh2, dsc2, dgt2, dsh3, dsc3, dgt3], axis=1).reshape(B, N_MOD * D)
    dmod_all = _small_allgather(dmod, "gather_dmod").reshape(N_DEV * B, N_MOD * D)
    dmod_cols = lax.dynamic_slice(dmod_all, (0, me * cl), (N_DEV * B, cl))
    g_ada_w, g_ada_b = _ada_backward(c_act, dmod_cols, dmod_all)

    fbg = jnp.sum(dfb, axis=0)
    small = jnp.concatenate([jnp.sum(dg1, axis=0), jnp.sum(dgmix, axis=0), jnp.sum(dg3, axis=0), jnp.sum(dg_final, axis=0),
                             fbg, jnp.sum(loss_b, axis=0)], axis=1)
    small = _sum_devices(_small_allgather(small, "gather_small"))
    g_small = {"norm_ffn1": small[:, 0:D], "norm_mix": small[:, D:2 * D], "norm_ffn2": small[:, 2 * D:3 * D],
               "norm_final": small[:, 3 * D:4 * D], "forget_bias": small[:, 4 * D:4 * D + FOX_HEADS], "ada_b": g_ada_b}
    loss = small[0, 4 * D + LANES]

    grads = {"ada_w": g_ada_w[None]}
    for names, own, recv in ((GROUPS[0], own1, recv1), (GROUPS[1], own2, recv2), (GROUPS[2], own3, recv3)):
        grads.update(unpack_grads(names, _final_grad_sum(own, recv)))

    delta, new_m, new_v = {}, {}, {}
    for n in ["ada_w"] + BIG:
        shp = args[n].shape
        d_, m_, v_ = _adamw(f"adamw_{n}", args[n][0], grads[n][0], args["m_" + n][0], args["v_" + n][0])
        delta[n], new_m[n], new_v[n] = d_.reshape(shp), m_.reshape(shp), v_.reshape(shp)
    sizes = [args[n].size for n in SMALL]
    tot = sum(sizes)
    padded = -(-tot // (8 * LANES)) * (8 * LANES)

    def flat(get):
        v = jnp.concatenate([get(n).reshape(-1) for n in SMALL])
        return jnp.pad(v, (0, padded - tot)).reshape(8, padded // 8)

    d_s, m_s, v_s = _adamw("adamw_small", flat(lambda n: args[n]), flat(lambda n: g_small[n]), flat(lambda n: args["m_" + n]),
                           flat(lambda n: args["v_" + n]))
    o = 0
    for n, sz in zip(SMALL, sizes):
        shp = args[n].shape
        grads[n] = g_small[n].reshape(shp)
        delta[n] = d_s.reshape(-1)[o:o + sz].reshape(shp)
        new_m[n] = m_s.reshape(-1)[o:o + sz].reshape(shp)
        new_v[n] = v_s.reshape(-1)[o:o + sz].reshape(shp)
        o += sz

    order = ["ada_w", "ada_b", "norm_ffn1", "ffn1_w_gate", "ffn1_w_up", "ffn1_w_down", "norm_mix", "w_in", "forget_bias",
             "w_branch_a", "w_branch_b", "w_out", "norm_ffn2", "ffn2_w_gate", "ffn2_w_up", "ffn2_w_down", "norm_final"]
    return (loss, grad_x, *[grads[n] for n in order], *[delta[n] for n in order], *[new_m[n] for n in order],
            *[new_v[n] for n in order])
```

```python
import functools
import math

import jax
import jax.numpy as jnp
from jax import lax
from jax.experimental import pallas as pl
from jax.experimental.pallas import tpu as pltpu

F32 = jnp.float32
BF16 = jnp.bfloat16
MESH = pl.DeviceIdType.MESH
ANY = pl.BlockSpec(memory_space=pl.ANY)
VMEM_SPEC = pl.BlockSpec(memory_space=pltpu.VMEM)

N_DEV = 8
HEAD_DIM = 64
BLOCK = 128
DIL_GROUPS = ((128, 1), (512, 4), (2048, 16))
N_DIL = len(DIL_GROUPS)
DIL_HPG = 4
DIL_GW = DIL_HPG * HEAD_DIM
DIL_W = N_DIL * DIL_GW
FOX_HEADS = 8
FOX_W = FOX_HEADS * HEAD_DIM
N_MOD = 9
RMS_EPS = 1e-6
ALIBI_MAX_BIAS = 8.0
NEG_INF = -1e30
ADAM_LR, ADAM_B1, ADAM_B2, ADAM_EPS, ADAM_WD, ADAM_STEP = 0.001, 0.9, 0.999, 1e-08, 0.01, 10
V7X_VMEM_LIMIT = 52 * 1024 * 1024
LANES = 128
ROW_ALIGN = 16
PACK_ROW_QUANTUM = 32
FOX_STRIP = 32
SCALE = 1.0 / math.sqrt(HEAD_DIM)


def _div(dim, target, quantum):
    best = None
    for t in range(quantum, min(dim, target) + 1, quantum):
        if dim % t == 0:
            best = t
    return best or dim


def _params(sem=None):
    return pltpu.CompilerParams(dimension_semantics=sem, vmem_limit_bytes=V7X_VMEM_LIMIT)


def _sigmoid(x):
    return 1.0 / (1.0 + jnp.exp(-x))


def _position():
    x, y, c = lax.axis_index("x"), lax.axis_index("y"), lax.axis_index("c")
    return x, y, c


def _small_allgather(v, name):
    rows, cols = v.shape

    def body(v_ref, out_ref, send_sems, recv_sems):
        x, y, c = _position()
        me = 4 * x + 2 * y + c
        out_ref[me] = v_ref[...]

        def peer(k):
            return (1 - x if k & 4 else x, 1 - y if k & 2 else y, 1 - c if k & 1 else c)

        def copy(k, slot):
            return pltpu.make_async_remote_copy(
                src_ref=v_ref, dst_ref=out_ref.at[slot], send_sem=send_sems.at[k - 1], recv_sem=recv_sems.at[k - 1],
                device_id=peer(k), device_id_type=MESH)

        sends = [copy(k, me) for k in range(1, N_DEV)]
        for cp in sends:
            cp.start()
        for k in range(1, N_DEV):
            px, py, pc = peer(k)
            copy(k, 4 * px + 2 * py + pc).wait_recv()
        for cp in sends:
            cp.wait_send()

    return pl.pallas_call(
        body, name=name,
        out_shape=jax.ShapeDtypeStruct((N_DEV, rows, cols), v.dtype),
        in_specs=[VMEM_SPEC], out_specs=VMEM_SPEC,
        scratch_shapes=[pltpu.SemaphoreType.DMA((N_DEV - 1,)), pltpu.SemaphoreType.DMA((N_DEV - 1,))],
    )(v)


def _weight_allgather(p):
    rows, cols = p.shape

    def body(p_ref, out_ref, send_sems, recv_sems, local_sem):
        x, y, c = _position()
        me, sibling = (x, y, c), (x, y, 1 - c)
        chips = [(1 - x, y), (x, 1 - y), (1 - x, 1 - y)]

        def slot(px, py, pc):
            return out_ref.at[4 * px + 2 * py + pc]

        def copy(k, block, to, src=None):
            return pltpu.make_async_remote_copy(
                src_ref=slot(*block) if src is None else src, dst_ref=slot(*block),
                send_sem=send_sems.at[k], recv_sem=recv_sems.at[k], device_id=to, device_id_type=MESH)

        mine = pltpu.make_async_copy(p_ref, slot(*me), local_sem)
        mine.start()
        first = [copy(0, me, sibling, src=p_ref)]
        first += [copy(1 + j, me, (*chip, c), src=p_ref) for j, chip in enumerate(chips)]
        for cp in first:
            cp.start()
        passed = [copy(4 + j, (*chip, c), sibling) for j, chip in enumerate(chips)]
        for j, chip in enumerate(chips):
            copy(1 + j, (*chip, c), me).wait_recv()
            passed[j].start()
        copy(0, sibling, me).wait_recv()
        for j, chip in enumerate(chips):
            copy(4 + j, (*chip, 1 - c), me).wait_recv()
        for cp in first + passed:
            cp.wait_send()
        mine.wait()

    return pl.pallas_call(
        body, name="weight_allgather",
        out_shape=jax.ShapeDtypeStruct((N_DEV, rows, cols), p.dtype),
        in_specs=[ANY], out_specs=ANY,
        scratch_shapes=[pltpu.SemaphoreType.DMA((7,)), pltpu.SemaphoreType.DMA((7,)), pltpu.SemaphoreType.DMA],
    )(p)


def _grad_exchange_sibling(g):
    _, rows, cols = g.shape

    def body(g_ref, out_ref, send_sems, recv_sems):
        x, y, c = _position()
        sibling = (x, y, 1 - c)

        def copy(q):
            px, py = q >> 1, q & 1
            return pltpu.make_async_remote_copy(
                src_ref=g_ref.at[4 * px + 2 * py + (1 - c)], dst_ref=out_ref.at[q],
                send_sem=send_sems.at[q], recv_sem=recv_sems.at[q], device_id=sibling, device_id_type=MESH)

        copies = [copy(q) for q in range(4)]
        for cp in copies:
            cp.start()
        for cp in copies:
            cp.wait_recv()
        for cp in copies:
            cp.wait_send()

    return pl.pallas_call(
        body, name="grad_exchange_sibling",
        out_shape=jax.ShapeDtypeStruct((4, rows, cols), g.dtype),
        in_specs=[ANY], out_specs=ANY,
        scratch_shapes=[pltpu.SemaphoreType.DMA((4,)), pltpu.SemaphoreType.DMA((4,))],
    )(g)


def _grad_exchange_chips(s):
    _, rows, cols = s.shape

    def body(s_ref, out_ref, send_sems, recv_sems):
        x, y, c = _position()
        chips = [(1 - x, y), (x, 1 - y), (1 - x, 1 - y)]

        def copy(k):
            return pltpu.make_async_remote_copy(
                src_ref=s_ref.at[k], dst_ref=out_ref.at[k], send_sem=send_sems.at[k], recv_sem=recv_sems.at[k],
                device_id=(*chips[k], c), device_id_type=MESH)

        copies = [copy(k) for k in range(3)]
        for cp in copies:
            cp.start()
        for cp in copies:
            cp.wait_recv()
        for cp in copies:
            cp.wait_send()

    return pl.pallas_call(
        body, name="grad_exchange_chips",
        out_shape=jax.ShapeDtypeStruct((3, rows, cols), s.dtype),
        in_specs=[ANY], out_specs=ANY,
        scratch_shapes=[pltpu.SemaphoreType.DMA((3,)), pltpu.SemaphoreType.DMA((3,))],
    )(s)


class _Rider:
    def __init__(self, operands, out_shapes, n_send, n_recv, start, finish, aliases=None):
        self.operands, self.out_shapes = list(operands), list(out_shapes)
        self.n_send, self.n_recv, self.start, self.finish = n_send, n_recv, start, finish
        self.aliases = aliases or {}


def _pcall(body, *, name, grid, in_specs, operands, out_shape, out_specs, scratch_shapes, params, rider=None):
    if rider is None:
        return pl.pallas_call(body, name=name, out_shape=out_shape, grid=grid, in_specs=in_specs, out_specs=out_specs,
                              scratch_shapes=scratch_shapes, compiler_params=params)(*operands)
    n_in, n_out, n_sc = len(operands), len(out_shape), len(scratch_shapes)
    r_in, r_out = len(rider.operands), len(rider.out_shapes)

    def wrapped(*refs):
        ins, rins = refs[:n_in], refs[n_in:n_in + r_in]
        outs, routs = refs[n_in + r_in:n_in + r_in + n_out], refs[n_in + r_in + n_out:n_in + r_in + n_out + r_out]
        rest = refs[n_in + r_in + n_out + r_out:]
        scratch, sems = rest[:n_sc], rest[n_sc:]
        ids = [pl.program_id(a) for a in range(len(grid))]
        first, last = ids[0] == 0, ids[0] == grid[0] - 1
        for a in range(1, len(grid)):
            first, last = first & (ids[a] == 0), last & (ids[a] == grid[a] - 1)

        @pl.when(first)
        def _():
            rider.start(rins, routs, *sems)

        body(*ins, *outs, *scratch)

        @pl.when(last)
        def _():
            rider.finish(rins, routs, *sems)

    return pl.pallas_call(
        wrapped, name=name, out_shape=list(out_shape) + rider.out_shapes, grid=grid,
        in_specs=list(in_specs) + [ANY] * r_in, out_specs=list(out_specs) + [ANY] * r_out,
        scratch_shapes=list(scratch_shapes) + [pltpu.SemaphoreType.DMA((rider.n_send,)), pltpu.SemaphoreType.DMA((rider.n_recv,))],
        input_output_aliases={n_in + i: n_out + o for i, o in rider.aliases.items()},
        compiler_params=params,
    )(*operands, *rider.operands)


def _flips(x, y, c):
    return [(x, y, 1 - c), (1 - x, y, c), (x, 1 - y, c), (1 - x, 1 - y, c)]


def _gather_direct_rider(p):
    rows, cols = p.shape

    def copies(p_ref, land, send_sems, recv_sems):
        x, y, c = _position()
        me = 4 * x + 2 * y + c
        peers = _flips(x, y, c)
        sends = [pltpu.make_async_remote_copy(src_ref=p_ref, dst_ref=land.at[me], send_sem=send_sems.at[k], recv_sem=recv_sems.at[k],
                                              device_id=to, device_id_type=MESH) for k, to in enumerate(peers)]
        recvs = [pltpu.make_async_remote_copy(src_ref=p_ref, dst_ref=land.at[4 * px + 2 * py + pc], send_sem=send_sems.at[k],
                                              recv_sem=recv_sems.at[k], device_id=(px, py, pc), device_id_type=MESH)
                 for k, (px, py, pc) in enumerate(peers)]
        mine = pltpu.make_async_copy(p_ref, land.at[me], send_sems.at[len(peers)])
        return sends, recvs, mine

    def start(rins, routs, send_sems, recv_sems):
        sends, _, mine = copies(rins[0], routs[0], send_sems, recv_sems)
        mine.start()
        for cp in sends:
            cp.start()

    def finish(rins, routs, send_sems, recv_sems):
        sends, recvs, mine = copies(rins[0], routs[0], send_sems, recv_sems)
        for cp in recvs:
            cp.wait_recv()
        for cp in sends:
            cp.wait_send()
        mine.wait()

    return _Rider([p], [jax.ShapeDtypeStruct((N_DEV, rows, cols), p.dtype)], 5, 4, start, finish)


def _gather_forward_rider(land):
    def copies(buf, send_sems, recv_sems):
        x, y, c = _position()
        chips = [(1 - x, y), (x, 1 - y), (1 - x, 1 - y)]
        sends = [pltpu.make_async_remote_copy(src_ref=buf.at[4 * px + 2 * py + c], dst_ref=buf.at[4 * px + 2 * py + c],
                                              send_sem=send_sems.at[k], recv_sem=recv_sems.at[k], device_id=(x, y, 1 - c),
                                              device_id_type=MESH) for k, (px, py) in enumerate(chips)]
        recvs = [pltpu.make_async_remote_copy(src_ref=buf.at[4 * px + 2 * py + 1 - c], dst_ref=buf.at[4 * px + 2 * py + 1 - c],
                                              send_sem=send_sems.at[k], recv_sem=recv_sems.at[k], device_id=(x, y, 1 - c),
                                              device_id_type=MESH) for k, (px, py) in enumerate(chips)]
        return sends, recvs

    def start(rins, routs, send_sems, recv_sems):
        for cp in copies(routs[0], send_sems, recv_sems)[0]:
            cp.start()

    def finish(rins, routs, send_sems, recv_sems):
        sends, recvs = copies(routs[0], send_sems, recv_sems)
        for cp in recvs:
            cp.wait_recv()
        for cp in sends:
            cp.wait_send()

    return _Rider([land], [jax.ShapeDtypeStruct(land.shape, land.dtype)], 3, 3, start, finish, aliases={0: 0})


def _chip_exchange_rider(s):
    def copies(s_ref, out_ref, send_sems, recv_sems):
        x, y, c = _position()
        chips = [(1 - x, y), (x, 1 - y), (1 - x, 1 - y)]
        return [pltpu.make_async_remote_copy(src_ref=s_ref.at[k], dst_ref=out_ref.at[k], send_sem=send_sems.at[k],
                                             recv_sem=recv_sems.at[k], device_id=(*chips[k], c), device_id_type=MESH)
                for k in range(3)]

    def start(rins, routs, send_sems, recv_sems):
        for cp in copies(rins[0], routs[0], send_sems, recv_sems):
            cp.start()

    def finish(rins, routs, send_sems, recv_sems):
        cps = copies(rins[0], routs[0], send_sems, recv_sems)
        for cp in cps:
            cp.wait_recv()
        for cp in cps:
            cp.wait_send()

    return _Rider([s], [jax.ShapeDtypeStruct(s.shape, s.dtype)], 3, 3, start, finish)


def _chip_partial_sums(g, recv_sib, jj, qq):
    _, rows, cols = g.shape
    tr = _div(rows, 512, ROW_ALIGN)

    def body(jj_ref, qq_ref, g_ref, r_ref, o_ref):
        o_ref[...] = (g_ref[...] + r_ref[...]).astype(o_ref.dtype)

    return pl.pallas_call(
        body, name="chip_partial_sums",
        out_shape=jax.ShapeDtypeStruct((3, rows, cols), BF16),
        grid_spec=pltpu.PrefetchScalarGridSpec(
            num_scalar_prefetch=2, grid=(3, rows // tr),
            in_specs=[pl.BlockSpec((None, tr, cols), lambda k, i, jj, qq: (jj[k], i, 0)),
                      pl.BlockSpec((None, tr, cols), lambda k, i, jj, qq: (qq[k], i, 0))],
            out_specs=pl.BlockSpec((None, tr, cols), lambda k, i, jj, qq: (k, i, 0))),
        compiler_params=_params(("arbitrary", "arbitrary")),
    )(jj, qq, g, recv_sib)


def _own_partial_sum(g, recv_sib, jj, qq):
    _, rows, cols = g.shape
    tr = _div(rows, 512, ROW_ALIGN)

    def body(jj_ref, qq_ref, g_ref, r_ref, o_ref):
        o_ref[...] = g_ref[...] + r_ref[...]

    return pl.pallas_call(
        body, name="own_partial_sum",
        out_shape=jax.ShapeDtypeStruct((rows, cols), F32),
        grid_spec=pltpu.PrefetchScalarGridSpec(
            num_scalar_prefetch=2, grid=(rows // tr,),
            in_specs=[pl.BlockSpec((None, tr, cols), lambda i, jj, qq: (jj[0], i, 0)),
                      pl.BlockSpec((None, tr, cols), lambda i, jj, qq: (qq[0], i, 0))],
            out_specs=pl.BlockSpec((tr, cols), lambda i, jj, qq: (i, 0))),
        compiler_params=_params(("arbitrary",)),
    )(jj, qq, g, recv_sib)


def _final_grad_sum(own, recv):
    rows, cols = own.shape
    tr = _div(rows, 512, ROW_ALIGN)

    def body(o_ref, r_ref, out_ref):
        out_ref[...] = ((o_ref[...] + r_ref[0].astype(F32)) + r_ref[1].astype(F32)) + r_ref[2].astype(F32)

    return pl.pallas_call(
        body, name="final_grad_sum",
        out_shape=jax.ShapeDtypeStruct((rows, cols), F32),
        grid=(rows // tr,),
        in_specs=[pl.BlockSpec((tr, cols), lambda i: (i, 0)), pl.BlockSpec((3, tr, cols), lambda i: (0, i, 0))],
        out_specs=pl.BlockSpec((tr, cols), lambda i: (i, 0)),
        compiler_params=_params(("arbitrary",)),
    )(own, recv)


def _chip_sums(g):
    x, y, c = _position()
    chips = [(1 - x, y), (x, 1 - y), (1 - x, 1 - y)]
    jj = jnp.stack([4 * px + 2 * py + c for px, py in chips]).astype(jnp.int32)
    qq = jnp.stack([2 * px + py for px, py in chips]).astype(jnp.int32)
    jme = jnp.reshape(4 * x + 2 * y + c, (1,)).astype(jnp.int32)
    qme = jnp.reshape(2 * x + y, (1,)).astype(jnp.int32)
    recv_sib = _grad_exchange_sibling(g)
    return _chip_partial_sums(g, recv_sib, jj, qq), _own_partial_sum(g, recv_sib, jme, qme)


def _matmul(name, form, prods, M, N, K, tm, tn, tk, out_dtypes, extras=(), epilogue=None, rows_per_example=None, rider=None):
    nk = K // tk
    n_acc = len(prods)
    flat = [ab for group in prods for ab in group]
    dims = {"nn": (((1,), (0,)), ((), ())), "nt": (((1,), (1,)), ((), ())), "tn": (((0,), (0,)), ((), ()))}[form]
    direct = nk > 1 and epilogue is None and n_acc == 1 and list(out_dtypes) == [F32]

    def spec(shape, index_map, whole):
        if whole:
            return pl.BlockSpec(shape, index_map, pipeline_mode=pl.Buffered(1))
        return pl.BlockSpec(shape, index_map)

    if form == "tn":
        a_spec = spec((tk, tm), lambda i, j, k: (k, i), nk == 1 and M == tm)
    else:
        a_spec = spec((tm, tk), lambda i, j, k: (i, k), nk == 1 and M == tm)
    if form == "nt":
        b_spec = spec((tn, tk), lambda i, j, k: (j, k), nk == 1 and N == tn)
    else:
        b_spec = spec((tk, tn), lambda i, j, k: (k, j), nk == 1 and N == tn)
    in_specs, operands = [], []
    for a, b in flat:
        in_specs += [a_spec, b_spec]
        operands += [a, b]
    for arr, kind, off in extras:
        if kind == "tile":
            assert off % tn == 0
            in_specs.append(pl.BlockSpec((tm, tn), functools.partial(lambda i, j, k, o: (i, j + o), o=off // tn)))
        else:
            tiles = rows_per_example // tm
            in_specs.append(pl.BlockSpec((None, 1, tn), functools.partial(lambda i, j, k, t: (i // t, 0, j), t=tiles)))
        operands.append(arr)
    n_in, n_out = len(operands), len(out_dtypes)

    def body(*refs):
        in_refs, out_refs, acc_refs = refs[:n_in], refs[n_in:n_in + n_out], refs[n_in + n_out:]
        k = pl.program_id(2)
        partials, p = [], 0
        for group in prods:
            tot = None
            for _ in group:
                d = lax.dot_general(in_refs[2 * p][...], in_refs[2 * p + 1][...], dims, preferred_element_type=F32)
                tot = d if tot is None else tot + d
                p += 1
            partials.append(tot)

        def finish(accs):
            ex = [r[...] for r in in_refs[2 * len(flat):]]
            outs = epilogue(accs, ex) if epilogue is not None else accs
            for r, o in zip(out_refs, outs):
                r[...] = o.astype(r.dtype)

        if nk == 1:
            finish(partials)
        elif direct:
            @pl.when(k == 0)
            def _():
                out_refs[0][...] = partials[0]

            @pl.when(k > 0)
            def _():
                out_refs[0][...] += partials[0]
        else:
            @pl.when(k == 0)
            def _():
                for r, v in zip(acc_refs, partials):
                    r[...] = v

            @pl.when(k > 0)
            def _():
                for r, v in zip(acc_refs, partials):
                    r[...] += v

            @pl.when(k == nk - 1)
            def _():
                finish([r[...] for r in acc_refs])

    return _pcall(
        body, name=name,
        out_shape=[jax.ShapeDtypeStruct((M, N), dt) for dt in out_dtypes],
        grid=(M // tm, N // tn, nk),
        in_specs=in_specs, operands=operands,
        out_specs=[pl.BlockSpec((tm, tn), lambda i, j, k: (i, j)) for _ in out_dtypes],
        scratch_shapes=[pltpu.VMEM((tm, tn), F32) for _ in range(n_acc)] if nk > 1 and not direct else [],
        params=_params(("parallel", "parallel", "arbitrary")), rider=rider)


def _rowwise(name, fn, T, tm, ins, outs, rows_per_example):
    tiles = rows_per_example // tm
    n_ex = T // rows_per_example
    in_specs, operands = [], []
    for arr, kind, arg in ins:
        if kind == "row":
            if arg is None:
                in_specs.append(pl.BlockSpec((tm, arr.shape[1]), lambda i: (i, 0)))
            else:
                in_specs.append(pl.BlockSpec((tm, arg[0]), functools.partial(lambda i, cb: (i, cb), cb=arg[1])))
        elif kind == "bvec":
            in_specs.append(pl.BlockSpec((None, 1, arr.shape[2]), lambda i: (i // tiles, 0, 0)))
        else:
            in_specs.append(pl.BlockSpec((1, arr.shape[1]), lambda i: (0, 0)))
        operands.append(arr)
    out_shape, out_specs = [], []
    for kind, cols, dt in outs:
        if kind == "row":
            out_shape.append(jax.ShapeDtypeStruct((T, cols), dt))
            out_specs.append(pl.BlockSpec((tm, cols), lambda i: (i, 0)))
        else:
            out_shape.append(jax.ShapeDtypeStruct((n_ex, 1, cols), F32))
            out_specs.append(pl.BlockSpec((None, 1, cols), lambda i: (i // tiles, 0, 0)))
    n_in = len(operands)

    def body(*refs):
        i = pl.program_id(0)
        vals = fn(*[r[...] for r in refs[:n_in]])
        for (kind, _, _), r, v in zip(outs, refs[n_in:], vals):
            if kind == "row":
                r[...] = v.astype(r.dtype)
            else:
                @pl.when(i % tiles == 0)
                def _():
                    r[...] = jnp.zeros_like(r)

                r[...] += v

    return pl.pallas_call(
        body, name=name, out_shape=out_shape, grid=(T // tm,), in_specs=in_specs, out_specs=out_specs,
        compiler_params=_params(("arbitrary",)),
    )(*operands)


def _colsum(v):
    return jnp.sum(v, axis=0, keepdims=True)


def _rms_parts(x):
    rstd = lax.rsqrt(jnp.mean(x * x, axis=-1, keepdims=True) + RMS_EPS)
    return x * rstd, rstd


def _normmod(name, x, g, sc, sh, S):
    T, D = x.shape

    def fn(xv, gv, scv, shv):
        xhat, _ = _rms_parts(xv)
        return [(xhat * gv) * (1.0 + scv) + shv]

    return _rowwise(name, fn, T, _div(S, 512, 8), [(x, "row", None), (g, "vec", None), (sc, "bvec", None), (sh, "bvec", None)],
                    [("row", D, BF16)], S)[0]


def _normmod_bwd(name, x, g, sc, dh, dres, S):
    T, D = x.shape

    def fn(xv, gv, scv, dhv, drv):
        xhat, rstd = _rms_parts(xv)
        n = xhat * gv
        dn = dhv * (1.0 + scv)
        dxh = dn * gv
        dx = rstd * (dxh - xhat * jnp.mean(dxh * xhat, axis=-1, keepdims=True))
        return [drv + dx, _colsum(dhv), _colsum(dhv * n), _colsum(dn * xhat)]

    return _rowwise(name, fn, T, _div(S, 256, 8),
                    [(x, "row", None), (g, "vec", None), (sc, "bvec", None), (dh, "row", None), (dres, "row", None)],
                    [("row", D, F32), ("bacc", D, F32), ("bacc", D, F32), ("bacc", D, F32)], S)


def _gate_grad(name, dx, y, gt, coeff, S):
    T, D = dx.shape

    def fn(dxv, yv, gtv):
        return [coeff * gtv * dxv, _colsum(coeff * dxv * yv.astype(F32))]

    return _rowwise(name, fn, T, _div(S, 512, 8), [(dx, "row", None), (y, "row", None), (gt, "bvec", None)],
                    [("row", D, BF16), ("bacc", D, F32)], S)


def _resident(shape):
    return pl.BlockSpec(shape, lambda i: (0, 0), pipeline_mode=pl.Buffered(1))


def _example_acc(r, i, tiles, v):
    @pl.when(i % tiles == 0)
    def _():
        r[...] = jnp.zeros_like(r)

    r[...] += v


def _norm_matmul(name, x, g, sc, sh, weights, epilogue, outs, S, vecs=(), rider=None):
    T, D = x.shape
    tm = _div(S, 256, 8)
    tiles = S // tm
    nw, nv = len(weights), len(vecs)

    def body(*refs):
        x_ref, g_ref, sc_ref, sh_ref = refs[:4]
        w_refs, v_refs = refs[4:4 + nw], refs[4 + nw:4 + nw + nv]
        h_ref, out_refs = refs[4 + nw + nv], refs[5 + nw + nv:]
        xhat, _ = _rms_parts(x_ref[...])
        h = ((xhat * g_ref[...]) * (1.0 + sc_ref[...]) + sh_ref[...]).astype(BF16)
        h_ref[...] = h
        accs = [lax.dot_general(h, w[...], NT, preferred_element_type=F32) for w in w_refs]
        for r, o in zip(out_refs, epilogue(accs, *[v[...] for v in v_refs])):
            r[...] = o.astype(r.dtype)

    bvec = pl.BlockSpec((None, 1, D), lambda i: (i // tiles, 0, 0))
    return _pcall(
        body, name=name,
        out_shape=[jax.ShapeDtypeStruct((T, D), BF16)] + [jax.ShapeDtypeStruct((T, w), dt) for w, dt in outs],
        grid=(T // tm,),
        in_specs=[pl.BlockSpec((tm, D), lambda i: (i, 0)), pl.BlockSpec((1, D), lambda i: (0, 0)), bvec, bvec]
        + [_resident(w.shape) for w in weights] + [pl.BlockSpec(v.shape, lambda i: (0, 0)) for v in vecs],
        operands=[x, g, sc, sh, *weights, *vecs],
        out_specs=[pl.BlockSpec((tm, D), lambda i: (i, 0))] + [pl.BlockSpec((tm, w), lambda i: (i, 0)) for w, _ in outs],
        scratch_shapes=[], params=_params(("arbitrary",)), rider=rider)


def _gated_grad_matmul(name, dx, y, gt, coeff, w, tiles_in, epilogue, outs, S, rider=None):
    T, D = dx.shape
    N = w.shape[0]
    tm = _div(S, 256, 8)
    tiles = S // tm
    nt = len(tiles_in)

    def body(*refs):
        dx_ref, y_ref, gt_ref, w_ref = refs[:4]
        t_refs, dy_ref, dgt_ref, out_refs = refs[4:4 + nt], refs[4 + nt], refs[5 + nt], refs[6 + nt:]
        i = pl.program_id(0)
        dxv = dx_ref[...]
        dy = (coeff * gt_ref[...] * dxv).astype(BF16)
        dy_ref[...] = dy
        _example_acc(dgt_ref, i, tiles, _colsum(coeff * dxv * y_ref[...].astype(F32)))
        acc = lax.dot_general(dy, w_ref[...], NT, preferred_element_type=F32)
        for r, o in zip(out_refs, epilogue(acc, [t[...] for t in t_refs])):
            r[...] = o.astype(r.dtype)

    row = pl.BlockSpec((tm, D), lambda i: (i, 0))
    bvec = pl.BlockSpec((None, 1, D), lambda i: (i // tiles, 0, 0))
    return _pcall(
        body, name=name,
        out_shape=[jax.ShapeDtypeStruct((T, D), BF16), jax.ShapeDtypeStruct((T // S, 1, D), F32)]
        + [jax.ShapeDtypeStruct((T, N), dt) for dt in outs],
        grid=(T // tm,),
        in_specs=[row, row, bvec, _resident(w.shape)]
        + [pl.BlockSpec((tm, N), functools.partial(lambda i, cb: (i, cb), cb=cb)) for _, cb in tiles_in],
        operands=[dx, y, gt, w, *[t for t, _ in tiles_in]],
        out_specs=[row, bvec] + [pl.BlockSpec((tm, N), lambda i: (i, 0)) for _ in outs],
        scratch_shapes=[], params=_params(("arbitrary",)), rider=rider)


def _matmul_normmod_bwd(name, prods, x, g, sc, dres, S, rider=None):
    T, D = x.shape
    tm = _div(S, 256, 8)
    tiles = S // tm
    npr = len(prods)

    def body(*refs):
        ab = refs[:2 * npr]
        x_ref, g_ref, sc_ref, dr_ref = refs[2 * npr:2 * npr + 4]
        dx_ref, dsh_ref, dsc_ref, dg_ref = refs[2 * npr + 4:]
        i = pl.program_id(0)
        dh = None
        for p in range(npr):
            d = lax.dot_general(ab[2 * p][...], ab[2 * p + 1][...], NN, preferred_element_type=F32)
            dh = d if dh is None else dh + d
        xhat, rstd = _rms_parts(x_ref[...])
        gv = g_ref[...]
        dn = dh * (1.0 + sc_ref[...])
        dxh = dn * gv
        dx_ref[...] = dr_ref[...] + rstd * (dxh - xhat * jnp.mean(dxh * xhat, axis=-1, keepdims=True))
        _example_acc(dsh_ref, i, tiles, _colsum(dh))
        _example_acc(dsc_ref, i, tiles, _colsum(dh * (xhat * gv)))
        _example_acc(dg_ref, i, tiles, _colsum(dn * xhat))

    row = pl.BlockSpec((tm, D), lambda i: (i, 0))
    bvec = pl.BlockSpec((None, 1, D), lambda i: (i // tiles, 0, 0))
    in_specs, operands = [], []
    for a, b in prods:
        in_specs += [pl.BlockSpec((tm, a.shape[1]), lambda i: (i, 0)), _resident(b.shape)]
        operands += [a, b]
    acc_shape = jax.ShapeDtypeStruct((T // S, 1, D), F32)
    return _pcall(
        body, name=name,
        out_shape=[jax.ShapeDtypeStruct((T, D), F32), acc_shape, acc_shape, acc_shape],
        grid=(T // tm,),
        in_specs=in_specs + [row, pl.BlockSpec((1, D), lambda i: (0, 0)), bvec, row],
        operands=[*operands, x, g, sc, dres],
        out_specs=[row, bvec, bvec, bvec],
        scratch_shapes=[], params=_params(("arbitrary",)), rider=rider)


def _ffn_forward(tag, x, g, sh, sc, gt, wgT, wuT, wd, S, gather=None):
    T, D = x.shape
    F = wd.shape[0]

    def gateup(accs):
        a, u = accs
        return [a, u, a * _sigmoid(a) * u]

    h, a, u, s, *land = _norm_matmul(f"{tag}_gateup", x, g, sc, sh, [wgT, wuT], gateup, [(F, BF16)] * 3, S,
                                     rider=None if gather is None else _gather_direct_rider(gather))

    def down(accs, ex):
        xv, gtv = ex
        return [xv + 0.5 * gtv * accs[0], accs[0]]

    tmd = _div(S, 512, 8)
    x_new, y, *land = _matmul(f"{tag}_down", "nn", [[(s, wd)]], T, D, F, tmd, D, F, [F32, BF16],
                              extras=[(x, "tile", 0), (gt, "brow", 0)], epilogue=down, rows_per_example=S,
                              rider=None if gather is None else _gather_forward_rider(land[0]))
    return x_new, (x, h, a, u, s, y), (land[0] if land else None)


def _ffn_backward(tag, dx_out, saved, g, sc, gt, wgT, wuT, wd, S, rider=None, dh_rider=None):
    x, h, a, u, s, y = saved
    T, D = x.shape
    F = wd.shape[0]

    def act_grad(ds, ex):
        av, uv = ex[0].astype(F32), ex[1].astype(F32)
        sg = _sigmoid(av)
        return [ds * uv * (sg * (1.0 + av * (1.0 - sg))), ds * (av * sg)]

    dy, dgt, da, du, *rode = _gated_grad_matmul(f"{tag}_act_grad", dx_out, y, gt, 0.5, wd, [(a, 0), (u, 0)], act_grad,
                                                [BF16, BF16], S, rider=rider)
    tkw = _div(T, 1024, LANES)
    dwd = _matmul(f"{tag}_dw_down", "tn", [[(s, dy)]], F, D, T, F, D, tkw, [F32])[0]
    dwgT = _matmul(f"{tag}_dw_gate", "tn", [[(da, h)]], F, D, T, F, D, tkw, [F32])[0]
    dwuT = _matmul(f"{tag}_dw_up", "tn", [[(du, h)]], F, D, T, F, D, tkw, [F32])[0]
    dx_in, dsh, dsc, dg, *rode_dh = _matmul_normmod_bwd(
        f"{tag}_dh", [(da, wgT), (du, wuT)], x, g, sc, dx_out, S,
        rider=None if dh_rider is None else dh_rider(dwgT, dwuT, dwd))
    return dx_in, (dsh, dsc, dgt, dg), (dwgT, dwuT, dwd), rode + rode_dh


def _loss_head(x, tgt, g, S):
    T, D = x.shape

    def fn(xv, tv, gv):
        xhat, rstd = _rms_parts(xv)
        e = xhat * gv - tv
        loss = jnp.broadcast_to(0.5 / D * jnp.sum(_colsum(e * e), axis=1, keepdims=True), (1, LANES))
        dy = e * (1.0 / D)
        dxh = dy * gv
        dx = rstd * (dxh - xhat * jnp.mean(dxh * xhat, axis=-1, keepdims=True))
        return [dx, loss, _colsum(dy * xhat)]

    return _rowwise("loss_head", fn, T, _div(S, 512, 8), [(x, "row", None), (tgt, "row", None), (g, "vec", None)],
                    [("row", D, F32), ("bacc", LANES, F32), ("bacc", D, F32)], S)


def _cumsum(v):
    B, S, _ = v.shape
    rows = _div(S, 1024, BLOCK)

    def body(x_ref, o_ref, carry):
        i = pl.program_id(1)

        @pl.when(i == 0)
        def _():
            carry[...] = jnp.zeros_like(carry)

        r = lax.broadcasted_iota(jnp.int32, (BLOCK, BLOCK), 0)
        c = lax.broadcasted_iota(jnp.int32, (BLOCK, BLOCK), 1)
        tri = (c <= r).astype(F32)
        last = carry[0:1, :]
        for j in range(0, rows, BLOCK):
            cum = jnp.dot(tri, x_ref[j:j + BLOCK, :], precision=lax.Precision.HIGHEST, preferred_element_type=F32) + last
            o_ref[j:j + BLOCK, :] = cum
            last = cum[BLOCK - 1:BLOCK, :]
        carry[...] = jnp.broadcast_to(last, carry.shape)

    return pl.pallas_call(
        body, name="cumsum", out_shape=jax.ShapeDtypeStruct(v.shape, F32), grid=(B, S // rows),
        in_specs=[pl.BlockSpec((None, rows, LANES), lambda b, i: (b, i, 0))],
        out_specs=pl.BlockSpec((None, rows, LANES), lambda b, i: (b, i, 0)),
        scratch_shapes=[pltpu.VMEM((8, LANES), F32)],
        compiler_params=_params(("arbitrary", "arbitrary")),
    )(v)


def _fox_scores(q, k, cq, ck, qpos, kpos):
    s = lax.dot_general(q, k, (((1,), (1,)), ((), ())), preferred_element_type=F32) * SCALE + cq - ck
    return jnp.where(kpos <= qpos, s, NEG_INF)


def _fox_positions(qi, kj, tq, tk):
    qpos = qi * tq + lax.broadcasted_iota(jnp.int32, (tq, tk), 0)
    kpos = kj * tk + lax.broadcasted_iota(jnp.int32, (tq, tk), 1)
    return qpos, kpos


def _fox_forward(pm3, cum, cumT, qcol, kcol, vcol, tq):
    B, S, _ = pm3.shape
    nq = S // tq

    def body(q_ref, k_ref, v_ref, cq_ref, ck_ref, o_ref, o32_ref, lse_ref, m_sc, l_sc, acc_sc):
        qi, kj = pl.program_id(1), pl.program_id(2)

        @pl.when(kj == 0)
        def _():
            m_sc[...] = jnp.full_like(m_sc, NEG_INF)
            l_sc[...] = jnp.zeros_like(l_sc)
            acc_sc[...] = jnp.zeros_like(acc_sc)

        @pl.when(kj <= qi)
        def _():
            qpos, kpos = _fox_positions(qi, kj, tq, tq)
            for h in range(FOX_HEADS):
                hs = slice(HEAD_DIM * h, HEAD_DIM * (h + 1))
                s = _fox_scores(q_ref[:, hs], k_ref[:, hs], cq_ref[:, h:h + 1], ck_ref[h:h + 1, :], qpos, kpos)
                m_prev = m_sc[h]
                m_new = jnp.maximum(m_prev, jnp.max(s, axis=-1, keepdims=True))
                alpha = jnp.exp(m_prev - m_new)
                p = jnp.exp(s - m_new)
                l_sc[h] = alpha * l_sc[h] + jnp.sum(p, axis=-1, keepdims=True)
                acc_sc[:, hs] = alpha * acc_sc[:, hs] + lax.dot_general(
                    p.astype(BF16), v_ref[:, hs], (((1,), (0,)), ((), ())), preferred_element_type=F32)
                m_sc[h] = m_new

        @pl.when(kj == nq - 1)
        def _():
            lse_ref[...] = jnp.zeros_like(lse_ref)
            for h in range(FOX_HEADS):
                hs = slice(HEAD_DIM * h, HEAD_DIM * (h + 1))
                oh = acc_sc[:, hs] / l_sc[h]
                o_ref[:, hs] = oh.astype(o_ref.dtype)
                o32_ref[:, hs] = oh
                lse_ref[:, h:h + 1] = m_sc[h] + jnp.log(l_sc[h])

    return pl.pallas_call(
        body, name="fox_forward",
        out_shape=[jax.ShapeDtypeStruct((B, S, FOX_W), BF16), jax.ShapeDtypeStruct((B, S, FOX_W), F32),
                   jax.ShapeDtypeStruct((B, S, LANES), F32)],
        grid=(B, nq, nq),
        in_specs=[pl.BlockSpec((None, tq, FOX_W), lambda b, i, j: (b, i, qcol)),
                  pl.BlockSpec((None, tq, FOX_W), lambda b, i, j: (b, jnp.minimum(i, j), kcol)),
                  pl.BlockSpec((None, tq, FOX_W), lambda b, i, j: (b, jnp.minimum(i, j), vcol)),
                  pl.BlockSpec((None, tq, LANES), lambda b, i, j: (b, i, 0)),
                  pl.BlockSpec((None, 8, tq), lambda b, i, j: (b, 0, jnp.minimum(i, j)))],
        out_specs=[pl.BlockSpec((None, tq, FOX_W), lambda b, i, j: (b, i, 0)),
                   pl.BlockSpec((None, tq, FOX_W), lambda b, i, j: (b, i, 0)),
                   pl.BlockSpec((None, tq, LANES), lambda b, i, j: (b, i, 0))],
        scratch_shapes=[pltpu.VMEM((FOX_HEADS, tq, 1), F32), pltpu.VMEM((FOX_HEADS, tq, 1), F32), pltpu.VMEM((tq, FOX_W), F32)],
        compiler_params=_params(("parallel", "parallel", "arbitrary")),
    )(pm3, pm3, pm3, cum, cumT)


def _fox_dq(pm3, do, delta, lse, cum, cumT, qcol, kcol, vcol, tq):
    B, S, _ = pm3.shape
    nq = S // tq

    def body(q_ref, k_ref, v_ref, do_ref, dl_ref, lse_ref, cq_ref, ck_ref, dq_ref, dc_ref, acc_sc, dc_sc):
        qi, kj = pl.program_id(1), pl.program_id(2)

        @pl.when(kj == 0)
        def _():
            acc_sc[...] = jnp.zeros_like(acc_sc)
            dc_sc[...] = jnp.zeros_like(dc_sc)

        @pl.when(kj <= qi)
        def _():
            qpos, kpos = _fox_positions(qi, kj, tq, tq)
            for h in range(FOX_HEADS):
                hs = slice(HEAD_DIM * h, HEAD_DIM * (h + 1))
                s = _fox_scores(q_ref[:, hs], k_ref[:, hs], cq_ref[:, h:h + 1], ck_ref[h:h + 1, :], qpos, kpos)
                p = jnp.exp(s - lse_ref[:, h:h + 1])
                doh = do_ref[:, hs]
                dp = lax.dot_general(doh, v_ref[:, hs], (((1,), (1,)), ((), ())), preferred_element_type=F32)
                ds = p * (dp - dl_ref[:, h:h + 1])
                dc_sc[h] += jnp.sum(ds, axis=-1, keepdims=True)
                acc_sc[:, hs] += lax.dot_general(ds.astype(BF16), k_ref[:, hs], (((1,), (0,)), ((), ())),
                                                 preferred_element_type=F32)

        @pl.when(kj == nq - 1)
        def _():
            dq_ref[...] = (acc_sc[...] * SCALE).astype(dq_ref.dtype)
            dc_ref[...] = jnp.zeros_like(dc_ref)
            for h in range(FOX_HEADS):
                dc_ref[:, h:h + 1] = dc_sc[h]

    qspec = pl.BlockSpec((None, tq, FOX_W), lambda b, i, j: (b, i, 0))
    lspec = pl.BlockSpec((None, tq, LANES), lambda b, i, j: (b, i, 0))
    return pl.pallas_call(
        body, name="fox_dq",
        out_shape=[jax.ShapeDtypeStruct((B, S, FOX_W), BF16), jax.ShapeDtypeStruct((B, S, LANES), F32)],
        grid=(B, nq, nq),
        in_specs=[pl.BlockSpec((None, tq, FOX_W), lambda b, i, j: (b, i, qcol)),
                  pl.BlockSpec((None, tq, FOX_W), lambda b, i, j: (b, jnp.minimum(i, j), kcol)),
                  pl.BlockSpec((None, tq, FOX_W), lambda b, i, j: (b, jnp.minimum(i, j), vcol)),
                  qspec, lspec, lspec, lspec,
                  pl.BlockSpec((None, 8, tq), lambda b, i, j: (b, 0, jnp.minimum(i, j)))],
        out_specs=[qspec, lspec],
        scratch_shapes=[pltpu.VMEM((tq, FOX_W), F32), pltpu.VMEM((FOX_HEADS, tq, 1), F32)],
        compiler_params=_params(("parallel", "parallel", "arbitrary")),
    )(pm3, pm3, pm3, do, delta, lse, cum, cumT)


def _fox_dkv(pm3, do, delta, lse, cum, cumT, qcol, kcol, vcol, tq):
    B, S, _ = pm3.shape
    nq = S // tq

    def body(q_ref, k_ref, v_ref, do_ref, dl_ref, lse_ref, cq_ref, ck_ref, dk_ref, dv_ref, dc_ref, dk_sc, dv_sc, dc_sc):
        kj, qi = pl.program_id(1), pl.program_id(2)

        @pl.when(qi == 0)
        def _():
            dk_sc[...] = jnp.zeros_like(dk_sc)
            dv_sc[...] = jnp.zeros_like(dv_sc)
            dc_sc[...] = jnp.zeros_like(dc_sc)

        @pl.when(qi >= kj)
        def _():
            qpos, kpos = _fox_positions(qi, kj, tq, tq)
            for h in range(FOX_HEADS):
                hs = slice(HEAD_DIM * h, HEAD_DIM * (h + 1))
                qh = q_ref[:, hs]
                s = _fox_scores(qh, k_ref[:, hs], cq_ref[:, h:h + 1], ck_ref[h:h + 1, :], qpos, kpos)
                p = jnp.exp(s - lse_ref[:, h:h + 1])
                doh = do_ref[:, hs]
                dp = lax.dot_general(doh, v_ref[:, hs], (((1,), (1,)), ((), ())), preferred_element_type=F32)
                ds = p * (dp - dl_ref[:, h:h + 1])
                dv_sc[:, hs] += lax.dot_general(p.astype(BF16), doh, (((0,), (0,)), ((), ())), preferred_element_type=F32)
                dk_sc[:, hs] += lax.dot_general(ds.astype(BF16), qh, (((0,), (0,)), ((), ())), preferred_element_type=F32)
                dc_sc[h:h + 1, :] -= jnp.sum(ds, axis=0, keepdims=True)

        @pl.when(qi == nq - 1)
        def _():
            dk_ref[...] = (dk_sc[...] * SCALE).astype(dk_ref.dtype)
            dv_ref[...] = dv_sc[...].astype(dv_ref.dtype)
            dc_ref[...] = dc_sc[...]

    def qside(width):
        return pl.BlockSpec((None, tq, width), lambda b, j, i: (b, jnp.maximum(i, j), 0))

    kspec = pl.BlockSpec((None, tq, FOX_W), lambda b, j, i: (b, j, 0))
    return pl.pallas_call(
        body, name="fox_dkv",
        out_shape=[jax.ShapeDtypeStruct((B, S, FOX_W), BF16), jax.ShapeDtypeStruct((B, S, FOX_W), BF16),
                   jax.ShapeDtypeStruct((B, 8, S), F32)],
        grid=(B, nq, nq),
        in_specs=[pl.BlockSpec((None, tq, FOX_W), lambda b, j, i: (b, jnp.maximum(i, j), qcol)),
                  pl.BlockSpec((None, tq, FOX_W), lambda b, j, i: (b, j, kcol)),
                  pl.BlockSpec((None, tq, FOX_W), lambda b, j, i: (b, j, vcol)),
                  qside(FOX_W), qside(LANES), qside(LANES), qside(LANES),
                  pl.BlockSpec((None, 8, tq), lambda b, j, i: (b, 0, j))],
        out_specs=[kspec, kspec, pl.BlockSpec((None, 8, tq), lambda b, j, i: (b, 0, j))],
        scratch_shapes=[pltpu.VMEM((tq, FOX_W), F32), pltpu.VMEM((tq, FOX_W), F32), pltpu.VMEM((8, tq), F32)],
        compiler_params=_params(("parallel", "parallel", "arbitrary")),
    )(pm3, pm3, pm3, do, delta, lse, cum, cumT)


def _with_ones(x):
    lane = lax.broadcasted_iota(jnp.int32, (x.shape[0], HEAD_DIM), 1)
    return jnp.concatenate([x, jnp.where(lane == 0, 1.0, 0.0).astype(x.dtype)], axis=1)


def _causal_strip(s, r):
    qpos = r + lax.broadcasted_iota(jnp.int32, s.shape, 0)
    kpos = lax.broadcasted_iota(jnp.int32, s.shape, 1)
    return jnp.where(kpos <= qpos, s, NEG_INF)


NT = (((1,), (1,)), ((), ()))
NN = (((1,), (0,)), ((), ()))
TN = (((0,), (0,)), ((), ()))


def _fox_fwd(pm3, cumT, qcol, kcol, vcol, tq, rider=None):
    B, S, _ = pm3.shape
    nq = S // tq
    strips = range(0, tq, FOX_STRIP)

    def body(q_ref, k_ref, v_ref, ck_ref, o_ref, o32_ref, lse_ref, s_sc, p_sc, al_sc, m_sc, acc_sc):
        qi, kj = pl.program_id(1), pl.program_id(2)

        @pl.when(kj == 0)
        def _():
            m_sc[...] = jnp.full_like(m_sc, NEG_INF)
            acc_sc[...] = jnp.zeros_like(acc_sc)

        def tile(diagonal):
            def scores(h):
                hs = slice(HEAD_DIM * h, HEAD_DIM * (h + 1))
                s_sc[h % 2] = lax.dot_general(q_ref[:, hs] * SCALE, k_ref[:, hs], NT, preferred_element_type=F32)

            def accumulate(h):
                hs = slice(HEAD_DIM * h, HEAD_DIM * (h + 1))
                acc_sc[h] = al_sc[h % 2] * acc_sc[h] + lax.dot_general(p_sc[h % 2], _with_ones(v_ref[:, hs]), NN,
                                                                       preferred_element_type=F32)

            scores(0)
            for h in range(FOX_HEADS):
                b = h % 2
                if h + 1 < FOX_HEADS:
                    scores(h + 1)
                if h >= 1:
                    accumulate(h - 1)
                ck = ck_ref[h:h + 1, :]
                for r in strips:
                    rows = slice(r, r + FOX_STRIP)
                    s = s_sc[b, rows, :] - ck
                    if diagonal:
                        s = _causal_strip(s, r)
                    m_prev = m_sc[h, rows, :]
                    m_new = jnp.maximum(m_prev, jnp.max(s, axis=-1, keepdims=True))
                    p_sc[b, rows, :] = jnp.exp(s - m_new).astype(BF16)
                    al_sc[b, rows, :] = jnp.exp(m_prev - m_new)
                    m_sc[h, rows, :] = m_new
            accumulate(FOX_HEADS - 1)

        @pl.when(kj < qi)
        def _():
            tile(False)

        @pl.when(kj == qi)
        def _():
            tile(True)

        @pl.when(kj == nq - 1)
        def _():
            lse_ref[...] = jnp.zeros_like(lse_ref)
            for h in range(FOX_HEADS):
                hs = slice(HEAD_DIM * h, HEAD_DIM * (h + 1))
                acc = acc_sc[h]
                l = acc[:, HEAD_DIM:HEAD_DIM + 1]
                oh = acc[:, :HEAD_DIM] / l
                o_ref[:, hs] = oh.astype(o_ref.dtype)
                o32_ref[:, hs] = oh
                lse_ref[:, h:h + 1] = m_sc[h] + jnp.log(l)

    ospec = pl.BlockSpec((None, tq, FOX_W), lambda b, i, j: (b, i, 0))
    return _pcall(
        body, name="fox_forward",
        out_shape=[jax.ShapeDtypeStruct((B, S, FOX_W), BF16), jax.ShapeDtypeStruct((B, S, FOX_W), F32),
                   jax.ShapeDtypeStruct((B, S, LANES), F32)],
        grid=(B, nq, nq),
        in_specs=[pl.BlockSpec((None, tq, FOX_W), lambda b, i, j: (b, i, qcol)),
                  pl.BlockSpec((None, tq, FOX_W), lambda b, i, j: (b, jnp.minimum(i, j), kcol)),
                  pl.BlockSpec((None, tq, FOX_W), lambda b, i, j: (b, jnp.minimum(i, j), vcol)),
                  pl.BlockSpec((None, 8, tq), lambda b, i, j: (b, 0, jnp.minimum(i, j)))],
        operands=[pm3, pm3, pm3, cumT],
        out_specs=[ospec, ospec, pl.BlockSpec((None, tq, LANES), lambda b, i, j: (b, i, 0))],
        scratch_shapes=[pltpu.VMEM((2, tq, tq), F32), pltpu.VMEM((2, tq, tq), BF16), pltpu.VMEM((2, tq, 1), F32),
                        pltpu.VMEM((FOX_HEADS, tq, 1), F32), pltpu.VMEM((FOX_HEADS, tq, LANES), F32)],
        params=_params(("parallel", "parallel", "arbitrary")), rider=rider)


def _fox_bwd(pm3, do, delta, lse, cumT, qcol, kcol, vcol, tq, rider=None):
    B, S, _ = pm3.shape
    nq = S // tq
    strips = range(0, tq, FOX_STRIP)

    def body(q_ref, k_ref, v_ref, do_ref, dl_ref, lse_ref, ck_ref, dq_ref, rs_ref, dk_ref, dv_ref, cs_ref,
             s_sc, dp_sc, p_sc, ds_sc, dq_sc, dk_sc, dv_sc):
        kj, qi = pl.program_id(1), pl.program_id(2)

        @pl.when((kj == 0) & (qi == 0))
        def _():
            dq_sc[...] = jnp.zeros_like(dq_sc)

        @pl.when(qi == 0)
        def _():
            dk_sc[...] = jnp.zeros_like(dk_sc)
            dv_sc[...] = jnp.zeros_like(dv_sc)

        def tile(diagonal):
            qrows = pl.ds(pl.multiple_of(qi * tq, tq), tq)
            for h in range(FOX_HEADS):
                hs = slice(HEAD_DIM * h, HEAD_DIM * (h + 1))
                qh, kh, doh = q_ref[:, hs] * SCALE, k_ref[:, hs], do_ref[:, hs]
                b = h % 2
                s_sc[b] = lax.dot_general(qh, kh, NT, preferred_element_type=F32)
                dp_sc[b] = lax.dot_general(doh, v_ref[:, hs], NT, preferred_element_type=F32)
                ck = ck_ref[h:h + 1, :]
                for r in strips:
                    rows = slice(r, r + FOX_STRIP)
                    s = s_sc[b, rows, :] - ck
                    if diagonal:
                        s = _causal_strip(s, r)
                    p = jnp.exp(s - lse_ref[rows, h:h + 1])
                    p_sc[b, rows, :] = p.astype(BF16)
                    ds_sc[b, rows, :] = (p * (dp_sc[b, rows, :] - dl_ref[rows, h:h + 1])).astype(BF16)
                dv_sc[:, hs] += lax.dot_general(p_sc[b], doh, TN, preferred_element_type=F32)
                dk_sc[h] += lax.dot_general(ds_sc[b], _with_ones(qh), TN, preferred_element_type=F32)
                dq_sc[h, qrows, :] += lax.dot_general(ds_sc[b], _with_ones(kh), NN, preferred_element_type=F32)

        @pl.when(qi > kj)
        def _():
            tile(False)

        @pl.when(qi == kj)
        def _():
            tile(True)

        @pl.when(qi == nq - 1)
        def _():
            dv_ref[...] = dv_sc[...].astype(dv_ref.dtype)
            cs_ref[...] = jnp.zeros_like(cs_ref)
            for h in range(FOX_HEADS):
                hs = slice(HEAD_DIM * h, HEAD_DIM * (h + 1))
                dk = dk_sc[h]
                dk_ref[:, hs] = dk[:, :HEAD_DIM].astype(dk_ref.dtype)
                cs_ref[:, h:h + 1] = dk[:, HEAD_DIM:HEAD_DIM + 1]

        @pl.when((kj == nq - 1) & (qi == nq - 1))
        def _():
            rs_ref[...] = jnp.zeros_like(rs_ref)
            for h in range(FOX_HEADS):
                hs = slice(HEAD_DIM * h, HEAD_DIM * (h + 1))
                dq_ref[:, hs] = (dq_sc[h, :, :HEAD_DIM] * SCALE).astype(dq_ref.dtype)
                rs_ref[:, h:h + 1] = dq_sc[h, :, HEAD_DIM:HEAD_DIM + 1]

    def qside(width, col=0):
        return pl.BlockSpec((None, tq, width), lambda b, j, i: (b, jnp.maximum(i, j), col))

    kspec = pl.BlockSpec((None, tq, FOX_W), lambda b, j, i: (b, j, 0))
    return _pcall(
        body, name="fox_backward",
        out_shape=[jax.ShapeDtypeStruct((B, S, FOX_W), BF16), jax.ShapeDtypeStruct((B, S, LANES), F32),
                   jax.ShapeDtypeStruct((B, S, FOX_W), BF16), jax.ShapeDtypeStruct((B, S, FOX_W), BF16),
                   jax.ShapeDtypeStruct((B, S, LANES), F32)],
        grid=(B, nq, nq),
        in_specs=[qside(FOX_W, qcol),
                  pl.BlockSpec((None, tq, FOX_W), lambda b, j, i: (b, j, kcol)),
                  pl.BlockSpec((None, tq, FOX_W), lambda b, j, i: (b, j, vcol)),
                  qside(FOX_W), qside(LANES), qside(LANES),
                  pl.BlockSpec((None, 8, tq), lambda b, j, i: (b, 0, j))],
        operands=[pm3, pm3, pm3, do, delta, lse, cumT],
        out_specs=[pl.BlockSpec((None, S, FOX_W), lambda b, j, i: (b, 0, 0)),
                   pl.BlockSpec((None, S, LANES), lambda b, j, i: (b, 0, 0)),
                   kspec, kspec, pl.BlockSpec((None, tq, LANES), lambda b, j, i: (b, j, 0))],
        scratch_shapes=[pltpu.VMEM((2, tq, tq), F32), pltpu.VMEM((2, tq, tq), F32), pltpu.VMEM((2, tq, tq), BF16),
                        pltpu.VMEM((2, tq, tq), BF16), pltpu.VMEM((FOX_HEADS, S, LANES), F32),
                        pltpu.VMEM((FOX_HEADS, tq, LANES), F32), pltpu.VMEM((tq, FOX_W), F32)],
        params=_params(("parallel", "arbitrary", "arbitrary")), rider=rider)


def _fox_delta(do, o32, T, S):
    def fn(dov, ov):
        prod = dov.astype(F32) * ov
        lane = lax.broadcasted_iota(jnp.int32, (dov.shape[0], LANES), 1)
        delta = jnp.zeros((dov.shape[0], LANES), F32)
        for h in range(FOX_HEADS):
            hs = slice(HEAD_DIM * h, HEAD_DIM * (h + 1))
            delta = jnp.where(lane == h, jnp.sum(prod[:, hs], axis=-1, keepdims=True), delta)
        return [delta]

    return _rowwise("fox_delta", fn, T, _div(S, 512, 8), [(do, "row", None), (o32, "row", None)], [("row", LANES, F32)], S)[0]


def _alibi_slope(group, head):
    return 2.0 ** (-ALIBI_MAX_BIAS * (group * DIL_HPG + head + 1) / (N_DIL * DIL_HPG))


def _dil_tiles(q, k_cur, k_prev, slope, dilation, has_prev):
    qi = lax.broadcasted_iota(jnp.int32, (BLOCK, BLOCK), 0)
    ki = lax.broadcasted_iota(jnp.int32, (BLOCK, BLOCK), 1)
    rel = (qi - ki).astype(F32)
    nt = (((1,), (1,)), ((), ()))
    s_cur = lax.dot_general(q, k_cur, nt, preferred_element_type=F32) * SCALE - (slope * dilation) * rel
    s_cur = jnp.where(ki <= qi, s_cur, NEG_INF)
    s_prev = lax.dot_general(q, k_prev, nt, preferred_element_type=F32) * SCALE - (slope * dilation) * (rel + BLOCK)
    s_prev = jnp.where((ki >= qi) & has_prev, s_prev, NEG_INF)
    return s_cur, s_prev


def _dil_forward(group, pmv, nmb, qa_blk, B, S):
    _, dilation = DIL_GROUPS[group]
    sub = S // dilation
    nb = sub // BLOCK
    qb, kb, vb = qa_blk + group, qa_blk + N_DIL + group, qa_blk + 2 * N_DIL + group

    def body(q_ref, kc_ref, kp_ref, vc_ref, vp_ref, o_ref, lse_ref):
        has_prev = pl.program_id(2) > 0
        lse_ref[...] = jnp.zeros_like(lse_ref)
        for h in range(DIL_HPG):
            hs = slice(HEAD_DIM * h, HEAD_DIM * (h + 1))
            s_cur, s_prev = _dil_tiles(q_ref[:, hs], kc_ref[:, hs], kp_ref[:, hs], _alibi_slope(group, h), dilation, has_prev)
            m = jnp.maximum(jnp.max(s_cur, axis=-1, keepdims=True), jnp.max(s_prev, axis=-1, keepdims=True))
            p_cur, p_prev = jnp.exp(s_cur - m), jnp.exp(s_prev - m)
            l = jnp.sum(p_cur, axis=-1, keepdims=True) + jnp.sum(p_prev, axis=-1, keepdims=True)
            nn = (((1,), (0,)), ((), ()))
            o = (lax.dot_general(p_cur.astype(BF16), vc_ref[:, hs], nn, preferred_element_type=F32)
                 + lax.dot_general(p_prev.astype(BF16), vp_ref[:, hs], nn, preferred_element_type=F32))
            o_ref[:, hs] = o / l
            lse_ref[:, h:h + 1] = m + jnp.log(l)

    def cur(col):
        return pl.BlockSpec((None, BLOCK, DIL_GW), lambda b, r, n: (b, n, r * nmb + col))

    def prev(col):
        return pl.BlockSpec((None, BLOCK, DIL_GW), lambda b, r, n: (b, jnp.maximum(n - 1, 0), r * nmb + col))

    return pl.pallas_call(
        body, name=f"dil_forward_{group}",
        out_shape=[jax.ShapeDtypeStruct((B, sub, dilation * DIL_GW), F32), jax.ShapeDtypeStruct((B, sub, dilation * LANES), F32)],
        grid=(B, dilation, nb),
        in_specs=[cur(qb), cur(kb), prev(kb), cur(vb), prev(vb)],
        out_specs=[pl.BlockSpec((None, BLOCK, DIL_GW), lambda b, r, n: (b, n, r)),
                   pl.BlockSpec((None, BLOCK, LANES), lambda b, r, n: (b, n, r))],
        compiler_params=_params(("parallel", "parallel", "arbitrary")),
    )(pmv, pmv, pmv, pmv, pmv)


def _residue_order(a, B, S, d):
    C = a.shape[-1]
    if d == 1:
        return a.reshape(B, S, C)
    return a.reshape(B, S // d, d, C).transpose(0, 2, 1, 3).reshape(B * d, S // d, C)


def _token_order(a, B, S, d):
    C = a.shape[-1]
    if d == 1:
        return a.reshape(B * S, C)
    return a.reshape(B, d, S // d, C).transpose(0, 2, 1, 3).reshape(B * S, C)


def _band_scores(qh, kcat, slope_d, has_prev):
    qi = lax.broadcasted_iota(jnp.int32, (BLOCK, 2 * BLOCK), 0)
    c = lax.broadcasted_iota(jnp.int32, (BLOCK, 2 * BLOCK), 1)
    s = lax.dot_general(qh, kcat, NT, preferred_element_type=F32) - slope_d * (BLOCK + qi - c).astype(F32)
    valid = (c >= qi) & (c <= qi + BLOCK)
    if has_prev is not None:
        valid = valid & ((c >= BLOCK) | has_prev)
    return jnp.where(valid, s, NEG_INF)


def _band_operands(j, cur_ref, prev_ref, hs):
    if j == 0:
        return jnp.concatenate([prev_ref[:, hs], cur_ref[0:BLOCK, hs]], axis=0)
    return cur_ref[(j - 1) * BLOCK:(j + 1) * BLOCK, hs]


def _dil_specs(Ls, qb, cols):
    nsub = qb // BLOCK
    qcol, kcol, vcol = cols

    def cur(col):
        return pl.BlockSpec((None, qb, DIL_GW), lambda s, n: (s, n, col))

    def prev(col):
        return pl.BlockSpec((None, BLOCK, DIL_GW), lambda s, n: (s, jnp.maximum(n * nsub - 1, 0), col))

    return [cur(qcol), cur(kcol), prev(kcol), cur(vcol), prev(vcol)]


def _dil_fwd(group, src, cols):
    _, dilation = DIL_GROUPS[group]
    nseq, Ls, _ = src.shape
    qb = _div(Ls, 512, BLOCK)
    nsub = qb // BLOCK

    def body(q_ref, kc_ref, kp_ref, vc_ref, vp_ref, o_ref, lse_ref):
        has_prev = pl.program_id(1) > 0
        lse_ref[...] = jnp.zeros_like(lse_ref)
        for h in range(DIL_HPG):
            hs = slice(HEAD_DIM * h, HEAD_DIM * (h + 1))
            for j in range(nsub):
                rows = slice(j * BLOCK, (j + 1) * BLOCK)
                s = _band_scores(q_ref[rows, hs] * SCALE, _band_operands(j, kc_ref, kp_ref, hs),
                                 _alibi_slope(group, h) * dilation, has_prev if j == 0 else None)
                m = jnp.max(s, axis=-1, keepdims=True)
                p = jnp.exp(s - m).astype(BF16)
                acc = lax.dot_general(p, _with_ones(_band_operands(j, vc_ref, vp_ref, hs)), NN, preferred_element_type=F32)
                l = acc[:, HEAD_DIM:HEAD_DIM + 1]
                o_ref[rows, hs] = acc[:, :HEAD_DIM] / l
                lse_ref[rows, h:h + 1] = m + jnp.log(l)

    return pl.pallas_call(
        body, name=f"dil_forward_{group}",
        out_shape=[jax.ShapeDtypeStruct((nseq, Ls, DIL_GW), F32), jax.ShapeDtypeStruct((nseq, Ls, LANES), F32)],
        grid=(nseq, Ls // qb),
        in_specs=_dil_specs(Ls, qb, cols),
        out_specs=[pl.BlockSpec((None, qb, DIL_GW), lambda s, n: (s, n, 0)),
                   pl.BlockSpec((None, qb, LANES), lambda s, n: (s, n, 0))],
        compiler_params=_params(("parallel", "arbitrary")),
    )(src, src, src, src, src)


def _dil_bwd(group, src, cols, Lr, dyr, dlr):
    _, dilation = DIL_GROUPS[group]
    nseq, Ls, _ = src.shape
    qb = _div(Ls, 512, BLOCK)
    nsub, nb = qb // BLOCK, Ls // qb

    def body(q_ref, kc_ref, kp_ref, vc_ref, vp_ref, L_ref, dy_ref, dl_ref, dq_ref, dk_ref, dv_ref, dk_sc, dv_sc):
        n = pl.program_id(1)
        has_prev = n > 0

        @pl.when(n == 0)
        def _():
            dk_sc[...] = jnp.zeros_like(dk_sc)
            dv_sc[...] = jnp.zeros_like(dv_sc)

        base = pl.multiple_of(n * qb, BLOCK)
        for h in range(DIL_HPG):
            hs = slice(HEAD_DIM * h, HEAD_DIM * (h + 1))
            for j in range(nsub):
                rows = slice(j * BLOCK, (j + 1) * BLOCK)
                qh = q_ref[rows, hs] * SCALE
                kcat = _band_operands(j, kc_ref, kp_ref, hs)
                s = _band_scores(qh, kcat, _alibi_slope(group, h) * dilation, has_prev if j == 0 else None)
                p = jnp.exp(s - L_ref[rows, h:h + 1])
                dyh = dy_ref[rows, hs]
                dp = lax.dot_general(dyh, _band_operands(j, vc_ref, vp_ref, hs), NT, preferred_element_type=F32)
                ds = (p * (dp - dl_ref[rows, h:h + 1])).astype(BF16)
                dq_ref[rows, hs] = (lax.dot_general(ds, kcat, NN, preferred_element_type=F32) * SCALE).astype(dq_ref.dtype)
                win = pl.ds(base + j * BLOCK, 2 * BLOCK)
                dk_sc[win, hs] += lax.dot_general(ds, qh, TN, preferred_element_type=F32)
                dv_sc[win, hs] += lax.dot_general(p.astype(BF16), dyh, TN, preferred_element_type=F32)

        @pl.when(n == nb - 1)
        def _():
            dk_ref[...] = dk_sc[BLOCK:, :].astype(dk_ref.dtype)
            dv_ref[...] = dv_sc[BLOCK:, :].astype(dv_ref.dtype)

    own = pl.BlockSpec((None, qb, DIL_GW), lambda s, n: (s, n, 0))
    own128 = pl.BlockSpec((None, qb, LANES), lambda s, n: (s, n, 0))
    whole = pl.BlockSpec((None, Ls, DIL_GW), lambda s, n: (s, 0, 0))
    shape = jax.ShapeDtypeStruct((nseq, Ls, DIL_GW), BF16)
    return pl.pallas_call(
        body, name=f"dil_backward_{group}",
        out_shape=[shape, shape, shape],
        grid=(nseq, nb),
        in_specs=_dil_specs(Ls, qb, cols) + [own128, own, own128],
        out_specs=[own, whole, whole],
        scratch_shapes=[pltpu.VMEM((Ls + BLOCK, DIL_GW), F32), pltpu.VMEM((Ls + BLOCK, DIL_GW), F32)],
        compiler_params=_params(("parallel", "arbitrary")),
    )(src, src, src, src, src, Lr, dyr, dlr)


def _dil_combine(os_, lses, T, S):
    def fn(o0, o1, o2, l0, l1, l2):
        m = jnp.maximum(jnp.maximum(l0, l1), l2)
        e0, e1, e2 = jnp.exp(l0 - m), jnp.exp(l1 - m), jnp.exp(l2 - m)
        tot = e0 + e1 + e2
        w0, w1, w2 = e0 / tot, e1 / tot, e2 / tot
        parts = []
        for h in range(DIL_HPG):
            hs = slice(HEAD_DIM * h, HEAD_DIM * (h + 1))
            parts.append(w0[:, h:h + 1] * o0[:, hs] + w1[:, h:h + 1] * o1[:, hs] + w2[:, h:h + 1] * o2[:, hs])
        return [jnp.concatenate(parts, axis=1), m + jnp.log(tot)]

    ins = [(a, "row", None) for a in os_] + [(a, "row", None) for a in lses]
    return _rowwise("dil_combine", fn, T, _div(S, 512, 8), ins, [("row", DIL_GW, BF16), ("row", LANES, F32)], S)


def _dil_delta(dy, y, T, S):
    def fn(dyv, yv):
        prod = dyv * yv.astype(F32)
        lane = lax.broadcasted_iota(jnp.int32, (dyv.shape[0], LANES), 1)
        delta = jnp.zeros((dyv.shape[0], LANES), F32)
        for h in range(DIL_HPG):
            hs = slice(HEAD_DIM * h, HEAD_DIM * (h + 1))
            delta = jnp.where(lane == h, jnp.sum(prod[:, hs], axis=-1, keepdims=True), delta)
        return [delta, dyv]

    return _rowwise("dil_delta", fn, T, _div(S, 512, 8), [(dy, "row", None), (y, "row", None)],
                    [("row", LANES, F32), ("row", DIL_GW, BF16)], S)


def _dil_dq(group, pmv, nmb, qa_blk, Lv, dyv, deltav, B, S):
    _, dilation = DIL_GROUPS[group]
    sub = S // dilation
    nb = sub // BLOCK
    qb, kb, vb = qa_blk + group, qa_blk + N_DIL + group, qa_blk + 2 * N_DIL + group

    def body(q_ref, kc_ref, kp_ref, vc_ref, vp_ref, L_ref, dy_ref, dl_ref, dq_ref):
        has_prev = pl.program_id(2) > 0
        nt = (((1,), (1,)), ((), ()))
        nn = (((1,), (0,)), ((), ()))
        for h in range(DIL_HPG):
            hs = slice(HEAD_DIM * h, HEAD_DIM * (h + 1))
            s_cur, s_prev = _dil_tiles(q_ref[:, hs], kc_ref[:, hs], kp_ref[:, hs], _alibi_slope(group, h), dilation, has_prev)
            L, delta, dyh = L_ref[:, h:h + 1], dl_ref[:, h:h + 1], dy_ref[:, hs]
            ds_cur = jnp.exp(s_cur - L) * (lax.dot_general(dyh, vc_ref[:, hs], nt, preferred_element_type=F32) - delta)
            ds_prev = jnp.exp(s_prev - L) * (lax.dot_general(dyh, vp_ref[:, hs], nt, preferred_element_type=F32) - delta)
            dq = (lax.dot_general(ds_cur.astype(BF16), kc_ref[:, hs], nn, preferred_element_type=F32)
                  + lax.dot_general(ds_prev.astype(BF16), kp_ref[:, hs], nn, preferred_element_type=F32))
            dq_ref[:, hs] = (dq * SCALE).astype(dq_ref.dtype)

    def cur(col):
        return pl.BlockSpec((None, BLOCK, DIL_GW), lambda b, r, n: (b, n, r * nmb + col))

    def prev(col):
        return pl.BlockSpec((None, BLOCK, DIL_GW), lambda b, r, n: (b, jnp.maximum(n - 1, 0), r * nmb + col))

    own = pl.BlockSpec((None, BLOCK, DIL_GW), lambda b, r, n: (b, n, r))
    own128 = pl.BlockSpec((None, BLOCK, LANES), lambda b, r, n: (b, n, r))
    return pl.pallas_call(
        body, name=f"dil_dq_{group}",
        out_shape=jax.ShapeDtypeStruct((B, sub, dilation * DIL_GW), BF16),
        grid=(B, dilation, nb),
        in_specs=[cur(qb), cur(kb), prev(kb), cur(vb), prev(vb), own128, own, own128],
        out_specs=own,
        compiler_params=_params(("parallel", "parallel", "arbitrary")),
    )(pmv, pmv, pmv, pmv, pmv, Lv, dyv, deltav)


def _dil_dkv(group, pmv, nmb, qa_blk, Lv, dyv, deltav, B, S):
    _, dilation = DIL_GROUPS[group]
    sub = S // dilation
    nb = sub // BLOCK
    qb, kb, vb = qa_blk + group, qa_blk + N_DIL + group, qa_blk + 2 * N_DIL + group

    def body(k_ref, v_ref, q0_ref, q1_ref, L0_ref, L1_ref, dy0_ref, dy1_ref, dl0_ref, dl1_ref, dk_ref, dv_ref):
        has_next = pl.program_id(2) < nb - 1
        qi = lax.broadcasted_iota(jnp.int32, (BLOCK, BLOCK), 0)
        ki = lax.broadcasted_iota(jnp.int32, (BLOCK, BLOCK), 1)
        rel = (qi - ki).astype(F32)
        nt = (((1,), (1,)), ((), ()))
        tn = (((0,), (0,)), ((), ()))
        for h in range(DIL_HPG):
            hs = slice(HEAD_DIM * h, HEAD_DIM * (h + 1))
            bias = _alibi_slope(group, h) * dilation
            kh, vh, q0, q1 = k_ref[:, hs], v_ref[:, hs], q0_ref[:, hs], q1_ref[:, hs]
            s0 = lax.dot_general(q0, kh, nt, preferred_element_type=F32) * SCALE - bias * rel
            s0 = jnp.where(ki <= qi, s0, NEG_INF)
            s1 = lax.dot_general(q1, kh, nt, preferred_element_type=F32) * SCALE - bias * (rel + BLOCK)
            s1 = jnp.where((ki >= qi) & has_next, s1, NEG_INF)
            p0 = jnp.exp(s0 - L0_ref[:, h:h + 1])
            p1 = jnp.exp(s1 - L1_ref[:, h:h + 1])
            dy0, dy1 = dy0_ref[:, hs], dy1_ref[:, hs]
            ds0 = p0 * (lax.dot_general(dy0, vh, nt, preferred_element_type=F32) - dl0_ref[:, h:h + 1])
            ds1 = p1 * (lax.dot_general(dy1, vh, nt, preferred_element_type=F32) - dl1_ref[:, h:h + 1])
            dv = (lax.dot_general(p0.astype(BF16), dy0, tn, preferred_element_type=F32)
                  + lax.dot_general(p1.astype(BF16), dy1, tn, preferred_element_type=F32))
            dk = (lax.dot_general(ds0.astype(BF16), q0, tn, preferred_element_type=F32)
                  + lax.dot_general(ds1.astype(BF16), q1, tn, preferred_element_type=F32))
            dv_ref[:, hs] = dv.astype(dv_ref.dtype)
            dk_ref[:, hs] = (dk * SCALE).astype(dk_ref.dtype)

    def cur(col):
        return pl.BlockSpec((None, BLOCK, DIL_GW), lambda b, r, n: (b, n, r * nmb + col))

    def nxt(col):
        return pl.BlockSpec((None, BLOCK, DIL_GW), lambda b, r, n: (b, jnp.minimum(n + 1, nb - 1), r * nmb + col))

    own = pl.BlockSpec((None, BLOCK, DIL_GW), lambda b, r, n: (b, n, r))
    own_next = pl.BlockSpec((None, BLOCK, DIL_GW), lambda b, r, n: (b, jnp.minimum(n + 1, nb - 1), r))
    own128 = pl.BlockSpec((None, BLOCK, LANES), lambda b, r, n: (b, n, r))
    own128_next = pl.BlockSpec((None, BLOCK, LANES), lambda b, r, n: (b, jnp.minimum(n + 1, nb - 1), r))
    shape = jax.ShapeDtypeStruct((B, sub, dilation * DIL_GW), BF16)
    return pl.pallas_call(
        body, name=f"dil_dkv_{group}",
        out_shape=[shape, shape],
        grid=(B, dilation, nb),
        in_specs=[cur(kb), cur(vb), cur(qb), nxt(qb), own128, own128_next, own, own_next, own128, own128_next],
        out_specs=[own, own],
        compiler_params=_params(("parallel", "parallel", "arbitrary")),
    )(pmv, pmv, pmv, pmv, Lv, Lv, dyv, dyv, deltav, deltav)


def _ada_forward(c_all, w, b):
    n, D = c_all.shape
    cl = w.shape[1]

    def body(c_ref, w_ref, b_ref, o_ref, ca_ref):
        cv = c_ref[...]
        ca = (cv * _sigmoid(cv)).astype(BF16)
        ca_ref[...] = ca
        o_ref[...] = jnp.dot(ca, w_ref[...].astype(BF16), preferred_element_type=F32) + b_ref[...]

    return pl.pallas_call(
        body, name="ada_forward",
        out_shape=[jax.ShapeDtypeStruct((n, cl), F32), jax.ShapeDtypeStruct((n, D), BF16)],
        compiler_params=_params(),
    )(c_all, w, b)


def _ada_backward(ca, dmod_cols, dmod_all):
    n, D = ca.shape
    cl = dmod_cols.shape[1]

    def body(ca_ref, dc_ref, da_ref, gw_ref, gb_ref):
        gw_ref[...] = lax.dot_general(ca_ref[...], dc_ref[...].astype(BF16), (((0,), (0,)), ((), ())), preferred_element_type=F32)
        gb_ref[...] = _colsum(da_ref[...])

    return pl.pallas_call(
        body, name="ada_backward",
        out_shape=[jax.ShapeDtypeStruct((D, cl), F32), jax.ShapeDtypeStruct((1, dmod_all.shape[1]), F32)],
        compiler_params=_params(),
    )(ca, dmod_cols, dmod_all)


def _sum_devices(v):
    def body(v_ref, o_ref):
        tot = v_ref[0]
        for k in range(1, N_DEV):
            tot = tot + v_ref[k]
        o_ref[...] = tot

    return pl.pallas_call(body, name="sum_devices", out_shape=jax.ShapeDtypeStruct(v.shape[1:], F32))(v)


def _adamw(name, w, g, m, v):
    rows, cols = w.shape
    tr = _div(rows, 256, 8)

    def body(w_ref, g_ref, m_ref, v_ref, d_ref, nm_ref, nv_ref):
        gv = g_ref[...]
        nm = ADAM_B1 * m_ref[...] + (1.0 - ADAM_B1) * gv
        nv = ADAM_B2 * v_ref[...] + (1.0 - ADAM_B2) * (gv * gv)
        m_hat = nm / (1.0 - ADAM_B1 ** ADAM_STEP)
        v_hat = nv / (1.0 - ADAM_B2 ** ADAM_STEP)
        d_ref[...] = -ADAM_LR * (m_hat / (jnp.sqrt(v_hat) + ADAM_EPS) + ADAM_WD * w_ref[...])
        nm_ref[...] = nm
        nv_ref[...] = nv

    spec = pl.BlockSpec((tr, cols), lambda i: (i, 0))
    shape = jax.ShapeDtypeStruct((rows, cols), F32)
    return pl.pallas_call(
        body, name=name, out_shape=[shape, shape, shape], grid=(rows // tr,),
        in_specs=[spec, spec, spec, spec], out_specs=[spec, spec, spec],
        compiler_params=_params(("arbitrary",)),
    )(w, g, m, v)


def _pad_rows(a, rows):
    return a if a.shape[0] == rows else jnp.pad(a, ((0, rows - a.shape[0]), (0, 0)))


class _Packed:
    def __init__(self, kind, local_shape, D):
        self.kind, self.local_shape, self.D = kind, local_shape, D
        r, c = local_shape
        self.rows = {"T": c, "N": r, "F": r * c // D}[kind]
        self.rows_pad = -(-self.rows // ROW_ALIGN) * ROW_ALIGN

    def pack_local(self, w):
        if self.kind == "T":
            w = w.T
        elif self.kind == "F":
            w = w.reshape(self.rows, self.D)
        return _pad_rows(w, self.rows_pad)

    def full(self, gathered):
        g = gathered[:, :self.rows]
        if self.kind == "F":
            r, c = self.local_shape
            return g.reshape(N_DEV, r, c).transpose(1, 0, 2).reshape(r, N_DEV * c)
        return g.reshape(N_DEV * self.rows, self.D)

    def pack_grad(self, gfull):
        if self.kind == "F":
            r, c = self.local_shape
            g = gfull.reshape(r, N_DEV, c).transpose(1, 0, 2).reshape(N_DEV, self.rows, self.D)
        else:
            g = gfull.reshape(N_DEV, self.rows, self.D)
        if self.rows_pad != self.rows:
            g = jnp.pad(g, ((0, 0), (0, self.rows_pad - self.rows), (0, 0)))
        return g

    def unpack_local(self, g):
        g = g[:self.rows]
        if self.kind == "T":
            return g.T
        if self.kind == "F":
            return g.reshape(self.local_shape)
        return g


BIG = ["ffn1_w_gate", "ffn1_w_up", "ffn1_w_down", "w_in", "w_branch_a", "w_branch_b", "w_out",
       "ffn2_w_gate", "ffn2_w_up", "ffn2_w_down"]
BIG_KIND = {"ffn1_w_gate": "T", "ffn1_w_up": "T", "ffn1_w_down": "N", "w_in": "T", "w_branch_a": "F", "w_branch_b": "F",
            "w_out": "N", "ffn2_w_gate": "T", "ffn2_w_up": "T", "ffn2_w_down": "N"}
GROUPS = (("ffn1_w_gate", "ffn1_w_up", "ffn1_w_down"), ("w_in", "w_branch_a", "w_branch_b", "w_out"),
          ("ffn2_w_gate", "ffn2_w_up", "ffn2_w_down"))
SMALL = ["ada_b", "norm_ffn1", "norm_mix", "forget_bias", "norm_ffn2", "norm_final"]


def kernel(x, c, ada_w, ada_b, norm_ffn1, ffn1_w_gate, ffn1_w_up, ffn1_w_down, norm_mix, w_in, forget_bias, w_branch_a, w_branch_b, w_out, norm_ffn2, ffn2_w_gate, ffn2_w_up, ffn2_w_down, norm_final, loss_target, m_ada_w, m_ada_b, m_norm_ffn1, m_ffn1_w_gate, m_ffn1_w_up, m_ffn1_w_down, m_norm_mix, m_w_in, m_forget_bias, m_w_branch_a, m_w_branch_b, m_w_out, m_norm_ffn2, m_ffn2_w_gate, m_ffn2_w_up, m_ffn2_w_down, m_norm_final, v_ada_w, v_ada_b, v_norm_ffn1, v_ffn1_w_gate, v_ffn1_w_up, v_ffn1_w_down, v_norm_mix, v_w_in, v_forget_bias, v_w_branch_a, v_w_branch_b, v_w_out, v_norm_ffn2, v_ffn2_w_gate, v_ffn2_w_up, v_ffn2_w_down, v_norm_final):
    args = dict(locals())
    B, S, D = x.shape
    T = B * S
    cl = ada_w.shape[2]
    n_in = w_in.shape[2] * N_DEV
    nm = 2 * D + 3 * FOX_W + 3 * DIL_W
    nmp = -(-nm // 512) * 512
    GA, GB, QB, QA = 0, D, 2 * D, 2 * D + 3 * FOX_W
    xpos, ypos, cpos = _position()
    me = 4 * xpos + 2 * ypos + cpos

    packs = {n: _Packed(BIG_KIND[n], args[n].shape[1:], D) for n in BIG}
    offs, pads = {}, {}
    for names in GROUPS:
        r = 0
        for n in names:
            offs[n] = r
            r += packs[n].rows_pad
        pads[names] = -r % PACK_ROW_QUANTUM

    def pack_weights(names):
        return jnp.concatenate([packs[n].pack_local(args[n][0]).astype(BF16) for n in names]
                               + [jnp.zeros((pads[names], D), BF16)], axis=0)

    def unpack_weights(names, land):
        return {n: packs[n].full(land[:, offs[n]:offs[n] + packs[n].rows_pad]) for n in names}

    def pack_grads(names, gfull):
        return jnp.concatenate([packs[n].pack_grad(gfull[n]) for n in names] + [jnp.zeros((N_DEV, pads[names], D), F32)], axis=1)

    def unpack_grads(names, g_local):
        return {n: packs[n].unpack_local(g_local[offs[n]:offs[n] + packs[n].rows_pad])[None] for n in names}

    W = unpack_weights(GROUPS[0], _weight_allgather(pack_weights(GROUPS[0])))

    c_all = _small_allgather(c, "gather_c").reshape(N_DEV * B, D)
    b_cols = lax.dynamic_slice(ada_b, (0, me * cl), (1, cl))
    mod_cols, c_act = _ada_forward(c_all, ada_w[0], b_cols)
    mod_all = _small_allgather(mod_cols, "gather_mod").transpose(1, 0, 2).reshape(N_DEV * B, N_MOD * D)
    mod = lax.dynamic_slice(mod_all, (me * B, 0), (B, N_MOD * D)).reshape(B, N_MOD, 1, D)
    sh1, sc1, gt1, sh2, sc2, gt2, sh3, sc3, gt3 = [mod[:, i] for i in range(N_MOD)]

    x0 = x.reshape(T, D)
    x1, saved1, land = _ffn_forward("ffn1", x0, norm_ffn1, sh1, sc1, gt1, W["ffn1_w_gate"], W["ffn1_w_up"], W["ffn1_w_down"], S,
                                    gather=pack_weights(GROUPS[1]))
    W.update(unpack_weights(GROUPS[1], land))
    winT = W["w_in"]
    o_f = 3 * DIL_W + 3 * FOX_W
    wmT = jnp.concatenate([winT[o_f + 8:], winT[3 * DIL_W:o_f], winT[:3 * DIL_W], jnp.zeros((nmp - nm, D), BF16)], axis=0)
    wfT = jnp.concatenate([winT[o_f:o_f + 8], jnp.zeros((LANES - 8, D), BF16)], axis=0)

    tm1k = _div(T, 1024, 8)
    fb = jnp.pad(forget_bias, ((0, 0), (0, LANES - FOX_HEADS)))

    def proj(accs, fbv):
        fl = accs[1] + fbv
        lane = lax.broadcasted_iota(jnp.int32, fl.shape, 1)
        ls = jnp.minimum(fl, 0.0) - jnp.log(1.0 + jnp.exp(-jnp.abs(fl)))
        return [accs[0], jnp.where(lane < FOX_HEADS, ls, 0.0), fl]

    tms = _div(S, 512, 8)
    h2, pm, logsig, flog, land = _norm_matmul("mix_proj", x1, norm_mix, sc2, sh2, [wmT, wfT], proj,
                                              [(nmp, BF16), (LANES, F32), (LANES, F32)], S, vecs=[fb],
                                              rider=_gather_direct_rider(pack_weights(GROUPS[2])))
    cum = _cumsum(logsig.reshape(B, S, LANES))
    cumT = cum[:, :, :8].transpose(0, 2, 1)
    pm3 = pm.reshape(B, S, nmp)
    tq = _div(S, 512, LANES)
    qcol, kcol, vcol = QB // FOX_W, QB // FOX_W + 1, QB // FOX_W + 2
    o_b, o_b32, lse_b, land = _fox_fwd(pm3, cumT, qcol, kcol, vcol, tq, rider=_gather_forward_rider(land))
    W.update(unpack_weights(GROUPS[2], land))
    y_b = o_b.reshape(T, FOX_W)

    qa_blk = QA // DIL_GW
    dil_src, dil_cols = [], []
    for g, (_, d) in enumerate(DIL_GROUPS):
        if d == 1:
            dil_src.append(pm3)
            dil_cols.append((qa_blk + g, qa_blk + N_DIL + g, qa_blk + 2 * N_DIL + g))
        else:
            starts = [QA + (i * N_DIL + g) * DIL_GW for i in range(3)]
            qkv = jnp.concatenate([pm[:, c:c + DIL_GW] for c in starts], axis=1)
            dil_src.append(_residue_order(qkv, B, S, d))
            dil_cols.append((0, 1, 2))
    dil_o, dil_lse = [], []
    for g, (_, d) in enumerate(DIL_GROUPS):
        o_g, lse_g = _dil_fwd(g, dil_src[g], dil_cols[g])
        dil_o.append(_token_order(o_g, B, S, d))
        dil_lse.append(_token_order(lse_g, B, S, d))
    y_a, L_a = _dil_combine(dil_o, dil_lse, T, S)

    wa, wb, wout = W["w_branch_a"], W["w_branch_b"], W["w_out"]
    tnd = D
    tm5 = _div(T, 512, 8)
    yap = _matmul("mix_branch_a", "nn", [[(y_a, wa)]], T, D, DIL_GW, tm5, tnd, DIL_GW, [BF16])[0]

    def merge(accs, ex):
        yapv, gav, gbv = ex
        ybp = accs[0]
        return [ybp, _sigmoid(gav.astype(F32)) * yapv.astype(F32) + _sigmoid(gbv.astype(F32)) * ybp]

    ybp, merged = _matmul("mix_branch_b", "nn", [[(y_b, wb)]], T, D, FOX_W, tm5, tnd, FOX_W, [BF16, BF16],
                          extras=[(yap, "tile", 0), (pm, "tile", GA), (pm, "tile", GB)], epilogue=merge)

    def out_proj(accs, ex):
        xv, gtv = ex
        return [xv + gtv * accs[0], accs[0]]

    x2, ymix = _matmul("mix_out", "nn", [[(merged, wout)]], T, D, D, tms, tnd, D, [F32, BF16],
                       extras=[(x1, "tile", 0), (gt2, "brow", 0)], epilogue=out_proj, rows_per_example=S)

    x3, saved3, _ = _ffn_forward("ffn2", x2, norm_ffn2, sh3, sc3, gt3, W["ffn2_w_gate"], W["ffn2_w_up"], W["ffn2_w_down"], S)

    dx3, loss_b, dg_final = _loss_head(x3, loss_target.reshape(T, D), norm_final.reshape(1, D), S)
    dx2, (dsh3, dsc3, dgt3, dg3), (dwg2, dwu2, dwd2), _ = _ffn_backward(
        "ffn2", dx3, saved3, norm_ffn2, sc3, gt3, W["ffn2_w_gate"], W["ffn2_w_up"], W["ffn2_w_down"], S)
    sums3, own3 = _chip_sums(pack_grads(GROUPS[2], {"ffn2_w_gate": dwg2, "ffn2_w_up": dwu2, "ffn2_w_down": dwd2}))

    def merge_grad(dm, ex):
        gav, gbv, yapv, ybpv = [e.astype(F32) for e in ex]
        sga, sgb = _sigmoid(gav), _sigmoid(gbv)
        return [dm * sga, dm * sgb, dm * yapv * sga * (1.0 - sga), dm * ybpv * sgb * (1.0 - sgb)]

    dym, dgt2, dyap, dybp, dga, dgb = _gated_grad_matmul(
        "mix_merge_grad", dx2, ymix, gt2, 1.0, wout, [(pm, GA // D), (pm, GB // D), (yap, 0), (ybp, 0)], merge_grad, [BF16] * 4, S)
    tkw = _div(T, 512, LANES)
    dwout = _matmul("mix_dw_out", "tn", [[(merged, dym)]], D, D, T, D, D, tkw, [F32])[0]
    dwa = _matmul("mix_dw_a", "tn", [[(y_a, dyap)]], DIL_GW, D, T, DIL_GW, D, tkw, [F32])[0]
    dwb = _matmul("mix_dw_b", "tn", [[(y_b, dybp)]], FOX_W, D, T, FOX_W, D, tkw, [F32])[0]
    dy_a = _matmul("mix_dy_a", "nt", [[(dyap, wa)]], T, DIL_GW, D, tm1k, DIL_GW, D, [F32])[0]
    dy_b = _matmul("mix_dy_b", "nt", [[(dybp, wb)]], T, FOX_W, D, tm1k, FOX_W, D, [BF16])[0]

    do3 = dy_b.reshape(B, S, FOX_W)
    delta_b = _fox_delta(dy_b, o_b32.reshape(T, FOX_W), T, S).reshape(B, S, LANES)
    dq_b, ds_rows, dk_b, dv_b, ds_cols, recv3 = _fox_bwd(pm3, do3, delta_b, lse_b, cumT, qcol, kcol, vcol, tq,
                                                         rider=_chip_exchange_rider(sums3))

    delta_a, dy_a16 = _dil_delta(dy_a, y_a, T, S)
    dqs, dks, dvs = [], [], []
    for g, (_, d) in enumerate(DIL_GROUPS):
        dq_g, dk_g, dv_g = _dil_bwd(g, dil_src[g], dil_cols[g], _residue_order(L_a, B, S, d),
                                    _residue_order(dy_a16, B, S, d), _residue_order(delta_a, B, S, d))
        dqs.append(_token_order(dq_g, B, S, d))
        dks.append(_token_order(dk_g, B, S, d))
        dvs.append(_token_order(dv_g, B, S, d))

    dcum = ds_rows - ds_cols
    dcum_run = _cumsum(dcum)
    dcum_tot = dcum_run[:, S - 1:S, :]

    def forget_grad_fn(run, dcv, fl, tot):
        lane = lax.broadcasted_iota(jnp.int32, fl.shape, 1)
        df = jnp.where(lane < FOX_HEADS, (tot - run + dcv) * _sigmoid(-fl), 0.0)
        return [df, _colsum(df)]

    df16, dfb = _rowwise("forget_gate_grad", forget_grad_fn, T, tms,
                         [(dcum_run.reshape(T, LANES), "row", None), (dcum.reshape(T, LANES), "row", None), (flog, "row", None),
                          (dcum_tot, "bvec", None)],
                         [("row", LANES, BF16), ("bacc", LANES, F32)], S)

    dpm = jnp.concatenate([dga, dgb, dq_b.reshape(T, FOX_W), dk_b.reshape(T, FOX_W), dv_b.reshape(T, FOX_W)]
                          + dqs + dks + dvs + ([jnp.zeros((T, nmp - nm), BF16)] if nmp > nm else []), axis=1)
    tmn = _div(nmp, 2048, LANES)
    dwmT = _matmul("mix_dw_in", "tn", [[(dpm, h2)]], nmp, D, T, tmn, D, tkw, [F32])[0]
    dwfT = _matmul("mix_dw_f", "tn", [[(df16, h2)]], LANES, D, T, LANES, D, tkw, [F32])[0]
    dx1, dsh2, dsc2, dgmix = _matmul_normmod_bwd("mix_dh", [(dpm, wmT), (df16, wfT)], x1, norm_mix, sc2, dx2, S)

    dwinT = jnp.concatenate([dwmT[QA:QA + 3 * DIL_W], dwmT[QB:QB + 3 * FOX_W], dwfT[:8], dwmT[GA:2 * D]], axis=0)
    sums2, own2 = _chip_sums(pack_grads(GROUPS[1], {"w_in": dwinT, "w_branch_a": dwa, "w_branch_b": dwb, "w_out": dwout}))

    own1 = []

    def ffn1_exchange(dwg, dwu, dwd):
        sums1, own = _chip_sums(pack_grads(GROUPS[0], {"ffn1_w_gate": dwg, "ffn1_w_up": dwu, "ffn1_w_down": dwd}))
        own1.append(own)
        return _chip_exchange_rider(sums1)

    dx0, (dsh1, dsc1, dgt1, dg1), _, (recv2, recv1) = _ffn_backward(
        "ffn1", dx1, saved1, norm_ffn1, sc1, gt1, W["ffn1_w_gate"], W["ffn1_w_up"], W["ffn1_w_down"], S,
        rider=_chip_exchange_rider(sums2), dh_rider=ffn1_exchange)
    own1 = own1[0]
    grad_x = dx0.reshape(B, S, D)

    dmod = jnp.concatenate([dsh1, dsc1, dgt1, dsh2, dsc2, dgt2, dsh3, dsc3, dgt3], axis=1).reshape(B, N_MOD * D)
    dmod_all = _small_allgather(dmod, "gather_dmod").reshape(N_DEV * B, N_MOD * D)
    dmod_cols = lax.dynamic_slice(dmod_all, (0, me * cl), (N_DEV * B, cl))
    g_ada_w, g_ada_b = _ada_backward(c_act, dmod_cols, dmod_all)

    fbg = jnp.sum(dfb, axis=0)
    small = jnp.concatenate([jnp.sum(dg1, axis=0), jnp.sum(dgmix, axis=0), jnp.sum(dg3, axis=0), jnp.sum(dg_final, axis=0),
                             fbg, jnp.sum(loss_b, axis=0)], axis=1)
    small = _sum_devices(_small_allgather(small, "gather_small"))
    g_small = {"norm_ffn1": small[:, 0:D], "norm_mix": small[:, D:2 * D], "norm_ffn2": small[:, 2 * D:3 * D],
               "norm_final": small[:, 3 * D:4 * D], "forget_bias": small[:, 4 * D:4 * D + FOX_HEADS], "ada_b": g_ada_b}
    loss = small[0, 4 * D + LANES]

    grads = {"ada_w": g_ada_w[None]}
    for names, own, recv in ((GROUPS[0], own1, recv1), (GROUPS[1], own2, recv2), (GROUPS[2], own3, recv3)):
        grads.update(unpack_grads(names, _final_grad_sum(own, recv)))

    delta, new_m, new_v = {}, {}, {}
    for n in ["ada_w"] + BIG:
        shp = args[n].shape
        d_, m_, v_ = _adamw(f"adamw_{n}", args[n][0], grads[n][0], args["m_" + n][0], args["v_" + n][0])
        delta[n], new_m[n], new_v[n] = d_.reshape(shp), m_.reshape(shp), v_.reshape(shp)
    sizes = [args[n].size for n in SMALL]
    tot = sum(sizes)
    padded = -(-tot // (8 * LANES)) * (8 * LANES)

    def flat(get):
        v = jnp.concatenate([get(n).reshape(-1) for n in SMALL])
        return jnp.pad(v, (0, padded - tot)).reshape(8, padded // 8)

    d_s, m_s, v_s = _adamw("adamw_small", flat(lambda n: args[n]), flat(lambda n: g_small[n]), flat(lambda n: args["m_" + n]),
                           flat(lambda n: args["v_" + n]))
    o = 0
    for n, sz in zip(SMALL, sizes):
        shp = args[n].shape
        grads[n] = g_small[n].reshape(shp)
        delta[n] = d_s.reshape(-1)[o:o + sz].reshape(shp)
        new_m[n] = m_s.reshape(-1)[o:o + sz].reshape(shp)
        new_v[n] = v_s.reshape(-1)[o:o + sz].reshape(shp)
        o += sz

    order = ["ada_w", "ada_b", "norm_ffn1", "ffn1_w_gate", "ffn1_w_up", "ffn1_w_down", "norm_mix", "w_in", "forget_bias",
             "w_branch_a", "w_branch_b", "w_out", "norm_ffn2", "ffn2_w_gate", "ffn2_w_up", "ffn2_w_down", "norm_final"]
    return (loss, grad_x, *[grads[n] for n in order], *[delta[n] for n in order], *[new_m[n] for n in order],
            *[new_v[n] for n in order])
```

```python
import functools
import math

import jax
import jax.numpy as jnp
from jax import lax
from jax.experimental import pallas as pl
from jax.experimental.pallas import tpu as pltpu

F32 = jnp.float32
BF16 = jnp.bfloat16
MESH = pl.DeviceIdType.MESH
ANY = pl.BlockSpec(memory_space=pl.ANY)
VMEM_SPEC = pl.BlockSpec(memory_space=pltpu.VMEM)

N_DEV = 8
HEAD_DIM = 64
BLOCK = 128
DIL_GROUPS = ((128, 1), (512, 4), (2048, 16))
N_DIL = len(DIL_GROUPS)
DIL_HPG = 4
DIL_GW = DIL_HPG * HEAD_DIM
DIL_W = N_DIL * DIL_GW
FOX_HEADS = 8
FOX_W = FOX_HEADS * HEAD_DIM
N_MOD = 9
RMS_EPS = 1e-6
ALIBI_MAX_BIAS = 8.0
NEG_INF = -1e30
ADAM_LR, ADAM_B1, ADAM_B2, ADAM_EPS, ADAM_WD, ADAM_STEP = 0.001, 0.9, 0.999, 1e-08, 0.01, 10
V7X_VMEM_LIMIT = 52 * 1024 * 1024
LANES = 128
ROW_ALIGN = 16
PACK_ROW_QUANTUM = 32
FOX_STRIP = 32
SCALE = 1.0 / math.sqrt(HEAD_DIM)


def _div(dim, target, quantum):
    best = None
    for t in range(quantum, min(dim, target) + 1, quantum):
        if dim % t == 0:
            best = t
    return best or dim


def _params(sem=None):
    return pltpu.CompilerParams(dimension_semantics=sem, vmem_limit_bytes=V7X_VMEM_LIMIT)


def _sigmoid(x):
    return 1.0 / (1.0 + jnp.exp(-x))


def _position():
    x, y, c = lax.axis_index("x"), lax.axis_index("y"), lax.axis_index("c")
    return x, y, c


def _small_allgather(v, name):
    rows, cols = v.shape

    def body(v_ref, out_ref, send_sems, recv_sems):
        x, y, c = _position()
        me = 4 * x + 2 * y + c
        out_ref[me] = v_ref[...]

        def peer(k):
            return (1 - x if k & 4 else x, 1 - y if k & 2 else y, 1 - c if k & 1 else c)

        def copy(k, slot):
            return pltpu.make_async_remote_copy(
                src_ref=v_ref, dst_ref=out_ref.at[slot], send_sem=send_sems.at[k - 1], recv_sem=recv_sems.at[k - 1],
                device_id=peer(k), device_id_type=MESH)

        sends = [copy(k, me) for k in range(1, N_DEV)]
        for cp in sends:
            cp.start()
        for k in range(1, N_DEV):
            px, py, pc = peer(k)
            copy(k, 4 * px + 2 * py + pc).wait_recv()
        for cp in sends:
            cp.wait_send()

    return pl.pallas_call(
        body, name=name,
        out_shape=jax.ShapeDtypeStruct((N_DEV, rows, cols), v.dtype),
        in_specs=[VMEM_SPEC], out_specs=VMEM_SPEC,
        scratch_shapes=[pltpu.SemaphoreType.DMA((N_DEV - 1,)), pltpu.SemaphoreType.DMA((N_DEV - 1,))],
    )(v)


def _weight_allgather(p):
    rows, cols = p.shape

    def body(p_ref, out_ref, send_sems, recv_sems, local_sem):
        x, y, c = _position()
        me, sibling = (x, y, c), (x, y, 1 - c)
        chips = [(1 - x, y), (x, 1 - y), (1 - x, 1 - y)]

        def slot(px, py, pc):
            return out_ref.at[4 * px + 2 * py + pc]

        def copy(k, block, to, src=None):
            return pltpu.make_async_remote_copy(
                src_ref=slot(*block) if src is None else src, dst_ref=slot(*block),
                send_sem=send_sems.at[k], recv_sem=recv_sems.at[k], device_id=to, device_id_type=MESH)

        mine = pltpu.make_async_copy(p_ref, slot(*me), local_sem)
        mine.start()
        first = [copy(0, me, sibling, src=p_ref)]
        first += [copy(1 + j, me, (*chip, c), src=p_ref) for j, chip in enumerate(chips)]
        for cp in first:
            cp.start()
        passed = [copy(4 + j, (*chip, c), sibling) for j, chip in enumerate(chips)]
        for j, chip in enumerate(chips):
            copy(1 + j, (*chip, c), me).wait_recv()
            passed[j].start()
        copy(0, sibling, me).wait_recv()
        for j, chip in enumerate(chips):
            copy(4 + j, (*chip, 1 - c), me).wait_recv()
        for cp in first + passed:
            cp.wait_send()
        mine.wait()

    return pl.pallas_call(
        body, name="weight_allgather",
        out_shape=jax.ShapeDtypeStruct((N_DEV, rows, cols), p.dtype),
        in_specs=[ANY], out_specs=ANY,
        scratch_shapes=[pltpu.SemaphoreType.DMA((7,)), pltpu.SemaphoreType.DMA((7,)), pltpu.SemaphoreType.DMA],
    )(p)


def _grad_exchange_sibling(g):
    _, rows, cols = g.shape

    def body(g_ref, out_ref, send_sems, recv_sems):
        x, y, c = _position()
        sibling = (x, y, 1 - c)

        def copy(q):
            px, py = q >> 1, q & 1
            return pltpu.make_async_remote_copy(
                src_ref=g_ref.at[4 * px + 2 * py + (1 - c)], dst_ref=out_ref.at[q],
                send_sem=send_sems.at[q], recv_sem=recv_sems.at[q], device_id=sibling, device_id_type=MESH)

        copies = [copy(q) for q in range(4)]
        for cp in copies:
            cp.start()
        for cp in copies:
            cp.wait_recv()
        for cp in copies:
            cp.wait_send()

    return pl.pallas_call(
        body, name="grad_exchange_sibling",
        out_shape=jax.ShapeDtypeStruct((4, rows, cols), g.dtype),
        in_specs=[ANY], out_specs=ANY,
        scratch_shapes=[pltpu.SemaphoreType.DMA((4,)), pltpu.SemaphoreType.DMA((4,))],
    )(g)


def _grad_exchange_chips(s):
    _, rows, cols = s.shape

    def body(s_ref, out_ref, send_sems, recv_sems):
        x, y, c = _position()
        chips = [(1 - x, y), (x, 1 - y), (1 - x, 1 - y)]

        def copy(k):
            return pltpu.make_async_remote_copy(
                src_ref=s_ref.at[k], dst_ref=out_ref.at[k], send_sem=send_sems.at[k], recv_sem=recv_sems.at[k],
                device_id=(*chips[k], c), device_id_type=MESH)

        copies = [copy(k) for k in range(3)]
        for cp in copies:
            cp.start()
        for cp in copies:
            cp.wait_recv()
        for cp in copies:
            cp.wait_send()

    return pl.pallas_call(
        body, name="grad_exchange_chips",
        out_shape=jax.ShapeDtypeStruct((3, rows, cols), s.dtype),
        in_specs=[ANY], out_specs=ANY,
        scratch_shapes=[pltpu.SemaphoreType.DMA((3,)), pltpu.SemaphoreType.DMA((3,))],
    )(s)


class _Rider:
    def __init__(self, operands, out_shapes, n_send, n_recv, start, finish, aliases=None):
        self.operands, self.out_shapes = list(operands), list(out_shapes)
        self.n_send, self.n_recv, self.start, self.finish = n_send, n_recv, start, finish
        self.aliases = aliases or {}


def _pcall(body, *, name, grid, in_specs, operands, out_shape, out_specs, scratch_shapes, params, rider=None):
    if rider is None:
        return pl.pallas_call(body, name=name, out_shape=out_shape, grid=grid, in_specs=in_specs, out_specs=out_specs,
                              scratch_shapes=scratch_shapes, compiler_params=params)(*operands)
    n_in, n_out, n_sc = len(operands), len(out_shape), len(scratch_shapes)
    r_in, r_out = len(rider.operands), len(rider.out_shapes)

    def wrapped(*refs):
        ins, rins = refs[:n_in], refs[n_in:n_in + r_in]
        outs, routs = refs[n_in + r_in:n_in + r_in + n_out], refs[n_in + r_in + n_out:n_in + r_in + n_out + r_out]
        rest = refs[n_in + r_in + n_out + r_out:]
        scratch, sems = rest[:n_sc], rest[n_sc:]
        ids = [pl.program_id(a) for a in range(len(grid))]
        first, last = ids[0] == 0, ids[0] == grid[0] - 1
        for a in range(1, len(grid)):
            first, last = first & (ids[a] == 0), last & (ids[a] == grid[a] - 1)

        @pl.when(first)
        def _():
            rider.start(rins, routs, *sems)

        body(*ins, *outs, *scratch)

        @pl.when(last)
        def _():
            rider.finish(rins, routs, *sems)

    return pl.pallas_call(
        wrapped, name=name, out_shape=list(out_shape) + rider.out_shapes, grid=grid,
        in_specs=list(in_specs) + [ANY] * r_in, out_specs=list(out_specs) + [ANY] * r_out,
        scratch_shapes=list(scratch_shapes) + [pltpu.SemaphoreType.DMA((rider.n_send,)), pltpu.SemaphoreType.DMA((rider.n_recv,))],
        input_output_aliases={n_in + i: n_out + o for i, o in rider.aliases.items()},
        compiler_params=params,
    )(*operands, *rider.operands)


def _flips(x, y, c):
    return [(x, y, 1 - c), (1 - x, y, c), (x, 1 - y, c), (1 - x, 1 - y, c)]


def _gather_direct_rider(p):
    rows, cols = p.shape

    def copies(p_ref, land, send_sems, recv_sems):
        x, y, c = _position()
        me = 4 * x + 2 * y + c
        peers = _flips(x, y, c)
        sends = [pltpu.make_async_remote_copy(src_ref=p_ref, dst_ref=land.at[me], send_sem=send_sems.at[k], recv_sem=recv_sems.at[k],
                                              device_id=to, device_id_type=MESH) for k, to in enumerate(peers)]
        recvs = [pltpu.make_async_remote_copy(src_ref=p_ref, dst_ref=land.at[4 * px + 2 * py + pc], send_sem=send_sems.at[k],
                                              recv_sem=recv_sems.at[k], device_id=(px, py, pc), device_id_type=MESH)
                 for k, (px, py, pc) in enumerate(peers)]
        mine = pltpu.make_async_copy(p_ref, land.at[me], send_sems.at[len(peers)])
        return sends, recvs, mine

    def start(rins, routs, send_sems, recv_sems):
        sends, _, mine = copies(rins[0], routs[0], send_sems, recv_sems)
        mine.start()
        for cp in sends:
            cp.start()

    def finish(rins, routs, send_sems, recv_sems):
        sends, recvs, mine = copies(rins[0], routs[0], send_sems, recv_sems)
        for cp in recvs:
            cp.wait_recv()
        for cp in sends:
            cp.wait_send()
        mine.wait()

    return _Rider([p], [jax.ShapeDtypeStruct((N_DEV, rows, cols), p.dtype)], 5, 4, start, finish)


def _gather_forward_rider(land):
    def copies(buf, send_sems, recv_sems):
        x, y, c = _position()
        chips = [(1 - x, y), (x, 1 - y), (1 - x, 1 - y)]
        sends = [pltpu.make_async_remote_copy(src_ref=buf.at[4 * px + 2 * py + c], dst_ref=buf.at[4 * px + 2 * py + c],
                                              send_sem=send_sems.at[k], recv_sem=recv_sems.at[k], device_id=(x, y, 1 - c),
                                              device_id_type=MESH) for k, (px, py) in enumerate(chips)]
        recvs = [pltpu.make_async_remote_copy(src_ref=buf.at[4 * px + 2 * py + 1 - c], dst_ref=buf.at[4 * px + 2 * py + 1 - c],
                                              send_sem=send_sems.at[k], recv_sem=recv_sems.at[k], device_id=(x, y, 1 - c),
                                              device_id_type=MESH) for k, (px, py) in enumerate(chips)]
        return sends, recvs

    def start(rins, routs, send_sems, recv_sems):
        for cp in copies(routs[0], send_sems, recv_sems)[0]:
            cp.start()

    def finish(rins, routs, send_sems, recv_sems):
        sends, recvs = copies(routs[0], send_sems, recv_sems)
        for cp in recvs:
            cp.wait_recv()
        for cp in sends:
            cp.wait_send()

    return _Rider([land], [jax.ShapeDtypeStruct(land.shape, land.dtype)], 3, 3, start, finish, aliases={0: 0})


def _chip_exchange_rider(s):
    def copies(s_ref, out_ref, send_sems, recv_sems):
        x, y, c = _position()
        chips = [(1 - x, y), (x, 1 - y), (1 - x, 1 - y)]
        return [pltpu.make_async_remote_copy(src_ref=s_ref.at[k], dst_ref=out_ref.at[k], send_sem=send_sems.at[k],
                                             recv_sem=recv_sems.at[k], device_id=(*chips[k], c), device_id_type=MESH)
                for k in range(3)]

    def start(rins, routs, send_sems, recv_sems):
        for cp in copies(rins[0], routs[0], send_sems, recv_sems):
            cp.start()

    def finish(rins, routs, send_sems, recv_sems):
        cps = copies(rins[0], routs[0], send_sems, recv_sems)
        for cp in cps:
            cp.wait_recv()
        for cp in cps:
            cp.wait_send()

    return _Rider([s], [jax.ShapeDtypeStruct(s.shape, s.dtype)], 3, 3, start, finish)


def _chip_partial_sums(g, recv_sib, jj, qq):
    _, rows, cols = g.shape
    tr = _div(rows, 512, ROW_ALIGN)

    def body(jj_ref, qq_ref, g_ref, r_ref, o_ref):
        o_ref[...] = (g_ref[...] + r_ref[...]).astype(o_ref.dtype)

    return pl.pallas_call(
        body, name="chip_partial_sums",
        out_shape=jax.ShapeDtypeStruct((3, rows, cols), BF16),
        grid_spec=pltpu.PrefetchScalarGridSpec(
            num_scalar_prefetch=2, grid=(3, rows // tr),
            in_specs=[pl.BlockSpec((None, tr, cols), lambda k, i, jj, qq: (jj[k], i, 0)),
                      pl.BlockSpec((None, tr, cols), lambda k, i, jj, qq: (qq[k], i, 0))],
            out_specs=pl.BlockSpec((None, tr, cols), lambda k, i, jj, qq: (k, i, 0))),
        compiler_params=_params(("arbitrary", "arbitrary")),
    )(jj, qq, g, recv_sib)


def _own_partial_sum(g, recv_sib, jj, qq):
    _, rows, cols = g.shape
    tr = _div(rows, 512, ROW_ALIGN)

    def body(jj_ref, qq_ref, g_ref, r_ref, o_ref):
        o_ref[...] = g_ref[...] + r_ref[...]

    return pl.pallas_call(
        body, name="own_partial_sum",
        out_shape=jax.ShapeDtypeStruct((rows, cols), F32),
        grid_spec=pltpu.PrefetchScalarGridSpec(
            num_scalar_prefetch=2, grid=(rows // tr,),
            in_specs=[pl.BlockSpec((None, tr, cols), lambda i, jj, qq: (jj[0], i, 0)),
                      pl.BlockSpec((None, tr, cols), lambda i, jj, qq: (qq[0], i, 0))],
            out_specs=pl.BlockSpec((tr, cols), lambda i, jj, qq: (i, 0))),
        compiler_params=_params(("arbitrary",)),
    )(jj, qq, g, recv_sib)


def _final_grad_sum(own, recv):
    rows, cols = own.shape
    tr = _div(rows, 512, ROW_ALIGN)

    def body(o_ref, r_ref, out_ref):
        out_ref[...] = ((o_ref[...] + r_ref[0].astype(F32)) + r_ref[1].astype(F32)) + r_ref[2].astype(F32)

    return pl.pallas_call(
        body, name="final_grad_sum",
        out_shape=jax.ShapeDtypeStruct((rows, cols), F32),
        grid=(rows // tr,),
        in_specs=[pl.BlockSpec((tr, cols), lambda i: (i, 0)), pl.BlockSpec((3, tr, cols), lambda i: (0, i, 0))],
        out_specs=pl.BlockSpec((tr, cols), lambda i: (i, 0)),
        compiler_params=_params(("arbitrary",)),
    )(own, recv)


def _sibling_exchange_rider(g):
    _, rows, cols = g.shape

    def copies(g_ref, out_ref, send_sems, recv_sems):
        x, y, c = _position()
        return [pltpu.make_async_remote_copy(
            src_ref=g_ref.at[4 * (q >> 1) + 2 * (q & 1) + (1 - c)], dst_ref=out_ref.at[q], send_sem=send_sems.at[q],
            recv_sem=recv_sems.at[q], device_id=(x, y, 1 - c), device_id_type=MESH) for q in range(4)]

    def start(rins, routs, send_sems, recv_sems):
        for cp in copies(rins[0], routs[0], send_sems, recv_sems):
            cp.start()

    def finish(rins, routs, send_sems, recv_sems):
        cps = copies(rins[0], routs[0], send_sems, recv_sems)
        for cp in cps:
            cp.wait_recv()
        for cp in cps:
            cp.wait_send()

    return _Rider([g], [jax.ShapeDtypeStruct((4, rows, cols), g.dtype)], 4, 4, start, finish)


def _chip_sums(g, recv_sib=None):
    x, y, c = _position()
    chips = [(1 - x, y), (x, 1 - y), (1 - x, 1 - y)]
    jj = jnp.stack([4 * px + 2 * py + c for px, py in chips]).astype(jnp.int32)
    qq = jnp.stack([2 * px + py for px, py in chips]).astype(jnp.int32)
    jme = jnp.reshape(4 * x + 2 * y + c, (1,)).astype(jnp.int32)
    qme = jnp.reshape(2 * x + y, (1,)).astype(jnp.int32)
    if recv_sib is None:
        recv_sib = _grad_exchange_sibling(g)
    return _chip_partial_sums(g, recv_sib, jj, qq), _own_partial_sum(g, recv_sib, jme, qme)


def _matmul(name, form, prods, M, N, K, tm, tn, tk, out_dtypes, extras=(), epilogue=None, rows_per_example=None, rider=None):
    nk = K // tk
    n_acc = len(prods)
    flat = [ab for group in prods for ab in group]
    dims = {"nn": (((1,), (0,)), ((), ())), "nt": (((1,), (1,)), ((), ())), "tn": (((0,), (0,)), ((), ()))}[form]
    direct = nk > 1 and epilogue is None and n_acc == 1 and list(out_dtypes) == [F32]

    def spec(shape, index_map, whole):
        if whole:
            return pl.BlockSpec(shape, index_map, pipeline_mode=pl.Buffered(1))
        return pl.BlockSpec(shape, index_map)

    if form == "tn":
        a_spec = spec((tk, tm), lambda i, j, k: (k, i), nk == 1 and M == tm)
    else:
        a_spec = spec((tm, tk), lambda i, j, k: (i, k), nk == 1 and M == tm)
    if form == "nt":
        b_spec = spec((tn, tk), lambda i, j, k: (j, k), nk == 1 and N == tn)
    else:
        b_spec = spec((tk, tn), lambda i, j, k: (k, j), nk == 1 and N == tn)
    in_specs, operands = [], []
    for a, b in flat:
        in_specs += [a_spec, b_spec]
        operands += [a, b]
    for arr, kind, off in extras:
        if kind == "tile":
            assert off % tn == 0
            in_specs.append(pl.BlockSpec((tm, tn), functools.partial(lambda i, j, k, o: (i, j + o), o=off // tn)))
        else:
            tiles = rows_per_example // tm
            in_specs.append(pl.BlockSpec((None, 1, tn), functools.partial(lambda i, j, k, t: (i // t, 0, j), t=tiles)))
        operands.append(arr)
    n_in, n_out = len(operands), len(out_dtypes)

    def body(*refs):
        in_refs, out_refs, acc_refs = refs[:n_in], refs[n_in:n_in + n_out], refs[n_in + n_out:]
        k = pl.program_id(2)
        partials, p = [], 0
        for group in prods:
            tot = None
            for _ in group:
                d = lax.dot_general(in_refs[2 * p][...], in_refs[2 * p + 1][...], dims, preferred_element_type=F32)
                tot = d if tot is None else tot + d
                p += 1
            partials.append(tot)

        def finish(accs):
            ex = [r[...] for r in in_refs[2 * len(flat):]]
            outs = epilogue(accs, ex) if epilogue is not None else accs
            for r, o in zip(out_refs, outs):
                r[...] = o.astype(r.dtype)

        if nk == 1:
            finish(partials)
        elif direct:
            @pl.when(k == 0)
            def _():
                out_refs[0][...] = partials[0]

            @pl.when(k > 0)
            def _():
                out_refs[0][...] += partials[0]
        else:
            @pl.when(k == 0)
            def _():
                for r, v in zip(acc_refs, partials):
                    r[...] = v

            @pl.when(k > 0)
            def _():
                for r, v in zip(acc_refs, partials):
                    r[...] += v

            @pl.when(k == nk - 1)
            def _():
                finish([r[...] for r in acc_refs])

    return _pcall(
        body, name=name,
        out_shape=[jax.ShapeDtypeStruct((M, N), dt) for dt in out_dtypes],
        grid=(M // tm, N // tn, nk),
        in_specs=in_specs, operands=operands,
        out_specs=[pl.BlockSpec((tm, tn), lambda i, j, k: (i, j)) for _ in out_dtypes],
        scratch_shapes=[pltpu.VMEM((tm, tn), F32) for _ in range(n_acc)] if nk > 1 and not direct else [],
        params=_params(("parallel", "parallel", "arbitrary")), rider=rider)


def _rowwise(name, fn, T, tm, ins, outs, rows_per_example):
    tiles = rows_per_example // tm
    n_ex = T // rows_per_example
    in_specs, operands = [], []
    for arr, kind, arg in ins:
        if kind == "row":
            if arg is None:
                in_specs.append(pl.BlockSpec((tm, arr.shape[1]), lambda i: (i, 0)))
            else:
                in_specs.append(pl.BlockSpec((tm, arg[0]), functools.partial(lambda i, cb: (i, cb), cb=arg[1])))
        elif kind == "bvec":
            in_specs.append(pl.BlockSpec((None, 1, arr.shape[2]), lambda i: (i // tiles, 0, 0)))
        else:
            in_specs.append(pl.BlockSpec((1, arr.shape[1]), lambda i: (0, 0)))
        operands.append(arr)
    out_shape, out_specs = [], []
    for kind, cols, dt in outs:
        if kind == "row":
            out_shape.append(jax.ShapeDtypeStruct((T, cols), dt))
            out_specs.append(pl.BlockSpec((tm, cols), lambda i: (i, 0)))
        else:
            out_shape.append(jax.ShapeDtypeStruct((n_ex, 1, cols), F32))
            out_specs.append(pl.BlockSpec((None, 1, cols), lambda i: (i // tiles, 0, 0)))
    n_in = len(operands)

    def body(*refs):
        i = pl.program_id(0)
        vals = fn(*[r[...] for r in refs[:n_in]])
        for (kind, _, _), r, v in zip(outs, refs[n_in:], vals):
            if kind == "row":
                r[...] = v.astype(r.dtype)
            else:
                @pl.when(i % tiles == 0)
                def _():
                    r[...] = jnp.zeros_like(r)

                r[...] += v

    return pl.pallas_call(
        body, name=name, out_shape=out_shape, grid=(T // tm,), in_specs=in_specs, out_specs=out_specs,
        compiler_params=_params(("arbitrary",)),
    )(*operands)


def _colsum(v):
    return jnp.sum(v, axis=0, keepdims=True)


def _rms_parts(x):
    rstd = lax.rsqrt(jnp.mean(x * x, axis=-1, keepdims=True) + RMS_EPS)
    return x * rstd, rstd


def _normmod(name, x, g, sc, sh, S):
    T, D = x.shape

    def fn(xv, gv, scv, shv):
        xhat, _ = _rms_parts(xv)
        return [(xhat * gv) * (1.0 + scv) + shv]

    return _rowwise(name, fn, T, _div(S, 512, 8), [(x, "row", None), (g, "vec", None), (sc, "bvec", None), (sh, "bvec", None)],
                    [("row", D, BF16)], S)[0]


def _normmod_bwd(name, x, g, sc, dh, dres, S):
    T, D = x.shape

    def fn(xv, gv, scv, dhv, drv):
        xhat, rstd = _rms_parts(xv)
        n = xhat * gv
        dn = dhv * (1.0 + scv)
        dxh = dn * gv
        dx = rstd * (dxh - xhat * jnp.mean(dxh * xhat, axis=-1, keepdims=True))
        return [drv + dx, _colsum(dhv), _colsum(dhv * n), _colsum(dn * xhat)]

    return _rowwise(name, fn, T, _div(S, 256, 8),
                    [(x, "row", None), (g, "vec", None), (sc, "bvec", None), (dh, "row", None), (dres, "row", None)],
                    [("row", D, F32), ("bacc", D, F32), ("bacc", D, F32), ("bacc", D, F32)], S)


def _gate_grad(name, dx, y, gt, coeff, S):
    T, D = dx.shape

    def fn(dxv, yv, gtv):
        return [coeff * gtv * dxv, _colsum(coeff * dxv * yv.astype(F32))]

    return _rowwise(name, fn, T, _div(S, 512, 8), [(dx, "row", None), (y, "row", None), (gt, "bvec", None)],
                    [("row", D, BF16), ("bacc", D, F32)], S)


def _resident(shape):
    return pl.BlockSpec(shape, lambda i: (0, 0), pipeline_mode=pl.Buffered(1))


def _example_acc(r, i, tiles, v):
    @pl.when(i % tiles == 0)
    def _():
        r[...] = jnp.zeros_like(r)

    r[...] += v


def _norm_matmul(name, x, g, sc, sh, weights, epilogue, outs, S, vecs=(), rider=None):
    T, D = x.shape
    tm = _div(S, 256, 8)
    tiles = S // tm
    nw, nv = len(weights), len(vecs)

    def body(*refs):
        x_ref, g_ref, sc_ref, sh_ref = refs[:4]
        w_refs, v_refs = refs[4:4 + nw], refs[4 + nw:4 + nw + nv]
        h_ref, out_refs = refs[4 + nw + nv], refs[5 + nw + nv:]
        xhat, _ = _rms_parts(x_ref[...])
        h = ((xhat * g_ref[...]) * (1.0 + sc_ref[...]) + sh_ref[...]).astype(BF16)
        h_ref[...] = h
        accs = [lax.dot_general(h, w[...], NT, preferred_element_type=F32) for w in w_refs]
        for r, o in zip(out_refs, epilogue(accs, *[v[...] for v in v_refs])):
            r[...] = o.astype(r.dtype)

    bvec = pl.BlockSpec((None, 1, D), lambda i: (i // tiles, 0, 0))
    return _pcall(
        body, name=name,
        out_shape=[jax.ShapeDtypeStruct((T, D), BF16)] + [jax.ShapeDtypeStruct((T, w), dt) for w, dt in outs],
        grid=(T // tm,),
        in_specs=[pl.BlockSpec((tm, D), lambda i: (i, 0)), pl.BlockSpec((1, D), lambda i: (0, 0)), bvec, bvec]
        + [_resident(w.shape) for w in weights] + [pl.BlockSpec(v.shape, lambda i: (0, 0)) for v in vecs],
        operands=[x, g, sc, sh, *weights, *vecs],
        out_specs=[pl.BlockSpec((tm, D), lambda i: (i, 0))] + [pl.BlockSpec((tm, w), lambda i: (i, 0)) for w, _ in outs],
        scratch_shapes=[], params=_params(("arbitrary",)), rider=rider)


def _gated_grad_matmul(name, dx, y, gt, coeff, w, tiles_in, epilogue, outs, S, rider=None):
    T, D = dx.shape
    N = w.shape[0]
    tm = _div(S, 256, 8)
    tiles = S // tm
    nt = len(tiles_in)

    def body(*refs):
        dx_ref, y_ref, gt_ref, w_ref = refs[:4]
        t_refs, dy_ref, dgt_ref, out_refs = refs[4:4 + nt], refs[4 + nt], refs[5 + nt], refs[6 + nt:]
        i = pl.program_id(0)
        dxv = dx_ref[...]
        dy = (coeff * gt_ref[...] * dxv).astype(BF16)
        dy_ref[...] = dy
        _example_acc(dgt_ref, i, tiles, _colsum(coeff * dxv * y_ref[...].astype(F32)))
        acc = lax.dot_general(dy, w_ref[...], NT, preferred_element_type=F32)
        for r, o in zip(out_refs, epilogue(acc, [t[...] for t in t_refs])):
            r[...] = o.astype(r.dtype)

    row = pl.BlockSpec((tm, D), lambda i: (i, 0))
    bvec = pl.BlockSpec((None, 1, D), lambda i: (i // tiles, 0, 0))
    return _pcall(
        body, name=name,
        out_shape=[jax.ShapeDtypeStruct((T, D), BF16), jax.ShapeDtypeStruct((T // S, 1, D), F32)]
        + [jax.ShapeDtypeStruct((T, N), dt) for dt in outs],
        grid=(T // tm,),
        in_specs=[row, row, bvec, _resident(w.shape)]
        + [pl.BlockSpec((tm, N), functools.partial(lambda i, cb: (i, cb), cb=cb)) for _, cb in tiles_in],
        operands=[dx, y, gt, w, *[t for t, _ in tiles_in]],
        out_specs=[row, bvec] + [pl.BlockSpec((tm, N), lambda i: (i, 0)) for _ in outs],
        scratch_shapes=[], params=_params(("arbitrary",)), rider=rider)


def _matmul_normmod_bwd(name, prods, x, g, sc, dres, S, rider=None):
    T, D = x.shape
    tm = _div(S, 256, 8)
    tiles = S // tm
    npr = len(prods)

    def body(*refs):
        ab = refs[:2 * npr]
        x_ref, g_ref, sc_ref, dr_ref = refs[2 * npr:2 * npr + 4]
        dx_ref, dsh_ref, dsc_ref, dg_ref = refs[2 * npr + 4:]
        i = pl.program_id(0)
        dh = None
        for p in range(npr):
            d = lax.dot_general(ab[2 * p][...], ab[2 * p + 1][...], NN, preferred_element_type=F32)
            dh = d if dh is None else dh + d
        xhat, rstd = _rms_parts(x_ref[...])
        gv = g_ref[...]
        dn = dh * (1.0 + sc_ref[...])
        dxh = dn * gv
        dx_ref[...] = dr_ref[...] + rstd * (dxh - xhat * jnp.mean(dxh * xhat, axis=-1, keepdims=True))
        _example_acc(dsh_ref, i, tiles, _colsum(dh))
        _example_acc(dsc_ref, i, tiles, _colsum(dh * (xhat * gv)))
        _example_acc(dg_ref, i, tiles, _colsum(dn * xhat))

    row = pl.BlockSpec((tm, D), lambda i: (i, 0))
    bvec = pl.BlockSpec((None, 1, D), lambda i: (i // tiles, 0, 0))
    in_specs, operands = [], []
    for a, b in prods:
        in_specs += [pl.BlockSpec((tm, a.shape[1]), lambda i: (i, 0)), _resident(b.shape)]
        operands += [a, b]
    acc_shape = jax.ShapeDtypeStruct((T // S, 1, D), F32)
    return _pcall(
        body, name=name,
        out_shape=[jax.ShapeDtypeStruct((T, D), F32), acc_shape, acc_shape, acc_shape],
        grid=(T // tm,),
        in_specs=in_specs + [row, pl.BlockSpec((1, D), lambda i: (0, 0)), bvec, row],
        operands=[*operands, x, g, sc, dres],
        out_specs=[row, bvec, bvec, bvec],
        scratch_shapes=[], params=_params(("arbitrary",)), rider=rider)


def _ffn_forward(tag, x, g, sh, sc, gt, wgT, wuT, wd, S, gather=None):
    T, D = x.shape
    F = wd.shape[0]

    def gateup(accs):
        a, u = accs
        return [a, u, a * _sigmoid(a) * u]

    h, a, u, s, *land = _norm_matmul(f"{tag}_gateup", x, g, sc, sh, [wgT, wuT], gateup, [(F, BF16)] * 3, S,
                                     rider=None if gather is None else _gather_direct_rider(gather))

    def down(accs, ex):
        xv, gtv = ex
        return [xv + 0.5 * gtv * accs[0], accs[0]]

    tmd = _div(S, 512, 8)
    x_new, y, *land = _matmul(f"{tag}_down", "nn", [[(s, wd)]], T, D, F, tmd, D, F, [F32, BF16],
                              extras=[(x, "tile", 0), (gt, "brow", 0)], epilogue=down, rows_per_example=S,
                              rider=None if gather is None else _gather_forward_rider(land[0]))
    return x_new, (x, h, a, u, s, y), (land[0] if land else None)


def _ffn_backward(tag, dx_out, saved, g, sc, gt, wgT, wuT, wd, S, rider=None, dh_rider=None):
    x, h, a, u, s, y = saved
    T, D = x.shape
    F = wd.shape[0]

    def act_grad(ds, ex):
        av, uv = ex[0].astype(F32), ex[1].astype(F32)
        sg = _sigmoid(av)
        return [ds * uv * (sg * (1.0 + av * (1.0 - sg))), ds * (av * sg)]

    dy, dgt, da, du, *rode = _gated_grad_matmul(f"{tag}_act_grad", dx_out, y, gt, 0.5, wd, [(a, 0), (u, 0)], act_grad,
                                                [BF16, BF16], S, rider=rider)
    tkw = _div(T, 1024, LANES)
    dwd = _matmul(f"{tag}_dw_down", "tn", [[(s, dy)]], F, D, T, F, D, tkw, [F32])[0]
    dwgT = _matmul(f"{tag}_dw_gate", "tn", [[(da, h)]], F, D, T, F, D, tkw, [F32])[0]
    dwuT = _matmul(f"{tag}_dw_up", "tn", [[(du, h)]], F, D, T, F, D, tkw, [F32])[0]
    dx_in, dsh, dsc, dg, *rode_dh = _matmul_normmod_bwd(
        f"{tag}_dh", [(da, wgT), (du, wuT)], x, g, sc, dx_out, S,
        rider=None if dh_rider is None else dh_rider(dwgT, dwuT, dwd))
    return dx_in, (dsh, dsc, dgt, dg), (dwgT, dwuT, dwd), rode + rode_dh


def _loss_head(x, tgt, g, S):
    T, D = x.shape

    def fn(xv, tv, gv):
        xhat, rstd = _rms_parts(xv)
        e = xhat * gv - tv
        loss = jnp.broadcast_to(0.5 / D * jnp.sum(_colsum(e * e), axis=1, keepdims=True), (1, LANES))
        dy = e * (1.0 / D)
        dxh = dy * gv
        dx = rstd * (dxh - xhat * jnp.mean(dxh * xhat, axis=-1, keepdims=True))
        return [dx, loss, _colsum(dy * xhat)]

    return _rowwise("loss_head", fn, T, _div(S, 512, 8), [(x, "row", None), (tgt, "row", None), (g, "vec", None)],
                    [("row", D, F32), ("bacc", LANES, F32), ("bacc", D, F32)], S)


def _cumsum(v):
    B, S, _ = v.shape
    rows = _div(S, 1024, BLOCK)

    def body(x_ref, o_ref, carry):
        i = pl.program_id(1)

        @pl.when(i == 0)
        def _():
            carry[...] = jnp.zeros_like(carry)

        r = lax.broadcasted_iota(jnp.int32, (BLOCK, BLOCK), 0)
        c = lax.broadcasted_iota(jnp.int32, (BLOCK, BLOCK), 1)
        tri = (c <= r).astype(F32)
        last = carry[0:1, :]
        for j in range(0, rows, BLOCK):
            cum = jnp.dot(tri, x_ref[j:j + BLOCK, :], precision=lax.Precision.HIGHEST, preferred_element_type=F32) + last
            o_ref[j:j + BLOCK, :] = cum
            last = cum[BLOCK - 1:BLOCK, :]
        carry[...] = jnp.broadcast_to(last, carry.shape)

    return pl.pallas_call(
        body, name="cumsum", out_shape=jax.ShapeDtypeStruct(v.shape, F32), grid=(B, S // rows),
        in_specs=[pl.BlockSpec((None, rows, LANES), lambda b, i: (b, i, 0))],
        out_specs=pl.BlockSpec((None, rows, LANES), lambda b, i: (b, i, 0)),
        scratch_shapes=[pltpu.VMEM((8, LANES), F32)],
        compiler_params=_params(("arbitrary", "arbitrary")),
    )(v)


def _fox_scores(q, k, cq, ck, qpos, kpos):
    s = lax.dot_general(q, k, (((1,), (1,)), ((), ())), preferred_element_type=F32) * SCALE + cq - ck
    return jnp.where(kpos <= qpos, s, NEG_INF)


def _fox_positions(qi, kj, tq, tk):
    qpos = qi * tq + lax.broadcasted_iota(jnp.int32, (tq, tk), 0)
    kpos = kj * tk + lax.broadcasted_iota(jnp.int32, (tq, tk), 1)
    return qpos, kpos


def _fox_forward(pm3, cum, cumT, qcol, kcol, vcol, tq):
    B, S, _ = pm3.shape
    nq = S // tq

    def body(q_ref, k_ref, v_ref, cq_ref, ck_ref, o_ref, o32_ref, lse_ref, m_sc, l_sc, acc_sc):
        qi, kj = pl.program_id(1), pl.program_id(2)

        @pl.when(kj == 0)
        def _():
            m_sc[...] = jnp.full_like(m_sc, NEG_INF)
            l_sc[...] = jnp.zeros_like(l_sc)
            acc_sc[...] = jnp.zeros_like(acc_sc)

        @pl.when(kj <= qi)
        def _():
            qpos, kpos = _fox_positions(qi, kj, tq, tq)
            for h in range(FOX_HEADS):
                hs = slice(HEAD_DIM * h, HEAD_DIM * (h + 1))
                s = _fox_scores(q_ref[:, hs], k_ref[:, hs], cq_ref[:, h:h + 1], ck_ref[h:h + 1, :], qpos, kpos)
                m_prev = m_sc[h]
                m_new = jnp.maximum(m_prev, jnp.max(s, axis=-1, keepdims=True))
                alpha = jnp.exp(m_prev - m_new)
                p = jnp.exp(s - m_new)
                l_sc[h] = alpha * l_sc[h] + jnp.sum(p, axis=-1, keepdims=True)
                acc_sc[:, hs] = alpha * acc_sc[:, hs] + lax.dot_general(
                    p.astype(BF16), v_ref[:, hs], (((1,), (0,)), ((), ())), preferred_element_type=F32)
                m_sc[h] = m_new

        @pl.when(kj == nq - 1)
        def _():
            lse_ref[...] = jnp.zeros_like(lse_ref)
            for h in range(FOX_HEADS):
                hs = slice(HEAD_DIM * h, HEAD_DIM * (h + 1))
                oh = acc_sc[:, hs] / l_sc[h]
                o_ref[:, hs] = oh.astype(o_ref.dtype)
                o32_ref[:, hs] = oh
                lse_ref[:, h:h + 1] = m_sc[h] + jnp.log(l_sc[h])

    return pl.pallas_call(
        body, name="fox_forward",
        out_shape=[jax.ShapeDtypeStruct((B, S, FOX_W), BF16), jax.ShapeDtypeStruct((B, S, FOX_W), F32),
                   jax.ShapeDtypeStruct((B, S, LANES), F32)],
        grid=(B, nq, nq),
        in_specs=[pl.BlockSpec((None, tq, FOX_W), lambda b, i, j: (b, i, qcol)),
                  pl.BlockSpec((None, tq, FOX_W), lambda b, i, j: (b, jnp.minimum(i, j), kcol)),
                  pl.BlockSpec((None, tq, FOX_W), lambda b, i, j: (b, jnp.minimum(i, j), vcol)),
                  pl.BlockSpec((None, tq, LANES), lambda b, i, j: (b, i, 0)),
                  pl.BlockSpec((None, 8, tq), lambda b, i, j: (b, 0, jnp.minimum(i, j)))],
        out_specs=[pl.BlockSpec((None, tq, FOX_W), lambda b, i, j: (b, i, 0)),
                   pl.BlockSpec((None, tq, FOX_W), lambda b, i, j: (b, i, 0)),
                   pl.BlockSpec((None, tq, LANES), lambda b, i, j: (b, i, 0))],
        scratch_shapes=[pltpu.VMEM((FOX_HEADS, tq, 1), F32), pltpu.VMEM((FOX_HEADS, tq, 1), F32), pltpu.VMEM((tq, FOX_W), F32)],
        compiler_params=_params(("parallel", "parallel", "arbitrary")),
    )(pm3, pm3, pm3, cum, cumT)


def _fox_dq(pm3, do, delta, lse, cum, cumT, qcol, kcol, vcol, tq):
    B, S, _ = pm3.shape
    nq = S // tq

    def body(q_ref, k_ref, v_ref, do_ref, dl_ref, lse_ref, cq_ref, ck_ref, dq_ref, dc_ref, acc_sc, dc_sc):
        qi, kj = pl.program_id(1), pl.program_id(2)

        @pl.when(kj == 0)
        def _():
            acc_sc[...] = jnp.zeros_like(acc_sc)
            dc_sc[...] = jnp.zeros_like(dc_sc)

        @pl.when(kj <= qi)
        def _():
            qpos, kpos = _fox_positions(qi, kj, tq, tq)
            for h in range(FOX_HEADS):
                hs = slice(HEAD_DIM * h, HEAD_DIM * (h + 1))
                s = _fox_scores(q_ref[:, hs], k_ref[:, hs], cq_ref[:, h:h + 1], ck_ref[h:h + 1, :], qpos, kpos)
                p = jnp.exp(s - lse_ref[:, h:h + 1])
                doh = do_ref[:, hs]
                dp = lax.dot_general(doh, v_ref[:, hs], (((1,), (1,)), ((), ())), preferred_element_type=F32)
                ds = p * (dp - dl_ref[:, h:h + 1])
                dc_sc[h] += jnp.sum(ds, axis=-1, keepdims=True)
                acc_sc[:, hs] += lax.dot_general(ds.astype(BF16), k_ref[:, hs], (((1,), (0,)), ((), ())),
                                                 preferred_element_type=F32)

        @pl.when(kj == nq - 1)
        def _():
            dq_ref[...] = (acc_sc[...] * SCALE).astype(dq_ref.dtype)
            dc_ref[...] = jnp.zeros_like(dc_ref)
            for h in range(FOX_HEADS):
                dc_ref[:, h:h + 1] = dc_sc[h]

    qspec = pl.BlockSpec((None, tq, FOX_W), lambda b, i, j: (b, i, 0))
    lspec = pl.BlockSpec((None, tq, LANES), lambda b, i, j: (b, i, 0))
    return pl.pallas_call(
        body, name="fox_dq",
        out_shape=[jax.ShapeDtypeStruct((B, S, FOX_W), BF16), jax.ShapeDtypeStruct((B, S, LANES), F32)],
        grid=(B, nq, nq),
        in_specs=[pl.BlockSpec((None, tq, FOX_W), lambda b, i, j: (b, i, qcol)),
                  pl.BlockSpec((None, tq, FOX_W), lambda b, i, j: (b, jnp.minimum(i, j), kcol)),
                  pl.BlockSpec((None, tq, FOX_W), lambda b, i, j: (b, jnp.minimum(i, j), vcol)),
                  qspec, lspec, lspec, lspec,
                  pl.BlockSpec((None, 8, tq), lambda b, i, j: (b, 0, jnp.minimum(i, j)))],
        out_specs=[qspec, lspec],
        scratch_shapes=[pltpu.VMEM((tq, FOX_W), F32), pltpu.VMEM((FOX_HEADS, tq, 1), F32)],
        compiler_params=_params(("parallel", "parallel", "arbitrary")),
    )(pm3, pm3, pm3, do, delta, lse, cum, cumT)


def _fox_dkv(pm3, do, delta, lse, cum, cumT, qcol, kcol, vcol, tq):
    B, S, _ = pm3.shape
    nq = S // tq

    def body(q_ref, k_ref, v_ref, do_ref, dl_ref, lse_ref, cq_ref, ck_ref, dk_ref, dv_ref, dc_ref, dk_sc, dv_sc, dc_sc):
        kj, qi = pl.program_id(1), pl.program_id(2)

        @pl.when(qi == 0)
        def _():
            dk_sc[...] = jnp.zeros_like(dk_sc)
            dv_sc[...] = jnp.zeros_like(dv_sc)
            dc_sc[...] = jnp.zeros_like(dc_sc)

        @pl.when(qi >= kj)
        def _():
            qpos, kpos = _fox_positions(qi, kj, tq, tq)
            for h in range(FOX_HEADS):
                hs = slice(HEAD_DIM * h, HEAD_DIM * (h + 1))
                qh = q_ref[:, hs]
                s = _fox_scores(qh, k_ref[:, hs], cq_ref[:, h:h + 1], ck_ref[h:h + 1, :], qpos, kpos)
                p = jnp.exp(s - lse_ref[:, h:h + 1])
                doh = do_ref[:, hs]
                dp = lax.dot_general(doh, v_ref[:, hs], (((1,), (1,)), ((), ())), preferred_element_type=F32)
                ds = p * (dp - dl_ref[:, h:h + 1])
                dv_sc[:, hs] += lax.dot_general(p.astype(BF16), doh, (((0,), (0,)), ((), ())), preferred_element_type=F32)
                dk_sc[:, hs] += lax.dot_general(ds.astype(BF16), qh, (((0,), (0,)), ((), ())), preferred_element_type=F32)
                dc_sc[h:h + 1, :] -= jnp.sum(ds, axis=0, keepdims=True)

        @pl.when(qi == nq - 1)
        def _():
            dk_ref[...] = (dk_sc[...] * SCALE).astype(dk_ref.dtype)
            dv_ref[...] = dv_sc[...].astype(dv_ref.dtype)
            dc_ref[...] = dc_sc[...]

    def qside(width):
        return pl.BlockSpec((None, tq, width), lambda b, j, i: (b, jnp.maximum(i, j), 0))

    kspec = pl.BlockSpec((None, tq, FOX_W), lambda b, j, i: (b, j, 0))
    return pl.pallas_call(
        body, name="fox_dkv",
        out_shape=[jax.ShapeDtypeStruct((B, S, FOX_W), BF16), jax.ShapeDtypeStruct((B, S, FOX_W), BF16),
                   jax.ShapeDtypeStruct((B, 8, S), F32)],
        grid=(B, nq, nq),
        in_specs=[pl.BlockSpec((None, tq, FOX_W), lambda b, j, i: (b, jnp.maximum(i, j), qcol)),
                  pl.BlockSpec((None, tq, FOX_W), lambda b, j, i: (b, j, kcol)),
                  pl.BlockSpec((None, tq, FOX_W), lambda b, j, i: (b, j, vcol)),
                  qside(FOX_W), qside(LANES), qside(LANES), qside(LANES),
                  pl.BlockSpec((None, 8, tq), lambda b, j, i: (b, 0, j))],
        out_specs=[kspec, kspec, pl.BlockSpec((None, 8, tq), lambda b, j, i: (b, 0, j))],
        scratch_shapes=[pltpu.VMEM((tq, FOX_W), F32), pltpu.VMEM((tq, FOX_W), F32), pltpu.VMEM((8, tq), F32)],
        compiler_params=_params(("parallel", "parallel", "arbitrary")),
    )(pm3, pm3, pm3, do, delta, lse, cum, cumT)


def _with_ones(x):
    lane = lax.broadcasted_iota(jnp.int32, (x.shape[0], HEAD_DIM), 1)
    return jnp.concatenate([x, jnp.where(lane == 0, 1.0, 0.0).astype(x.dtype)], axis=1)


def _causal_strip(s, r):
    qpos = r + lax.broadcasted_iota(jnp.int32, s.shape, 0)
    kpos = lax.broadcasted_iota(jnp.int32, s.shape, 1)
    return jnp.where(kpos <= qpos, s, NEG_INF)


NT = (((1,), (1,)), ((), ()))
NN = (((1,), (0,)), ((), ()))
TN = (((0,), (0,)), ((), ()))


def _fox_fwd(pm3, cumT, qcol, kcol, vcol, tq, rider=None):
    B, S, _ = pm3.shape
    nq = S // tq
    strips = range(0, tq, FOX_STRIP)

    def body(q_ref, k_ref, v_ref, ck_ref, o_ref, o32_ref, lse_ref, s_sc, p_sc, al_sc, m_sc, acc_sc):
        qi, kj = pl.program_id(1), pl.program_id(2)

        @pl.when(kj == 0)
        def _():
            m_sc[...] = jnp.full_like(m_sc, NEG_INF)
            acc_sc[...] = jnp.zeros_like(acc_sc)

        def tile(diagonal):
            def scores(h):
                hs = slice(HEAD_DIM * h, HEAD_DIM * (h + 1))
                s_sc[h % 2] = lax.dot_general(q_ref[:, hs] * SCALE, k_ref[:, hs], NT, preferred_element_type=F32)

            def accumulate(h):
                hs = slice(HEAD_DIM * h, HEAD_DIM * (h + 1))
                acc_sc[h] = al_sc[h % 2] * acc_sc[h] + lax.dot_general(p_sc[h % 2], _with_ones(v_ref[:, hs]), NN,
                                                                       preferred_element_type=F32)

            scores(0)
            for h in range(FOX_HEADS):
                b = h % 2
                if h + 1 < FOX_HEADS:
                    scores(h + 1)
                if h >= 1:
                    accumulate(h - 1)
                ck = ck_ref[h:h + 1, :]
                for r in strips:
                    rows = slice(r, r + FOX_STRIP)
                    s = s_sc[b, rows, :] - ck
                    if diagonal:
                        s = _causal_strip(s, r)
                    m_prev = m_sc[h, rows, :]
                    m_new = jnp.maximum(m_prev, jnp.max(s, axis=-1, keepdims=True))
                    p_sc[b, rows, :] = jnp.exp(s - m_new).astype(BF16)
                    al_sc[b, rows, :] = jnp.exp(m_prev - m_new)
                    m_sc[h, rows, :] = m_new
            accumulate(FOX_HEADS - 1)

        @pl.when(kj < qi)
        def _():
            tile(False)

        @pl.when(kj == qi)
        def _():
            tile(True)

        @pl.when(kj == nq - 1)
        def _():
            lse_ref[...] = jnp.zeros_like(lse_ref)
            for h in range(FOX_HEADS):
                hs = slice(HEAD_DIM * h, HEAD_DIM * (h + 1))
                acc = acc_sc[h]
                l = acc[:, HEAD_DIM:HEAD_DIM + 1]
                oh = acc[:, :HEAD_DIM] / l
                o_ref[:, hs] = oh.astype(o_ref.dtype)
                o32_ref[:, hs] = oh
                lse_ref[:, h:h + 1] = m_sc[h] + jnp.log(l)

    ospec = pl.BlockSpec((None, tq, FOX_W), lambda b, i, j: (b, i, 0))
    return _pcall(
        body, name="fox_forward",
        out_shape=[jax.ShapeDtypeStruct((B, S, FOX_W), BF16), jax.ShapeDtypeStruct((B, S, FOX_W), F32),
                   jax.ShapeDtypeStruct((B, S, LANES), F32)],
        grid=(B, nq, nq),
        in_specs=[pl.BlockSpec((None, tq, FOX_W), lambda b, i, j: (b, i, qcol)),
                  pl.BlockSpec((None, tq, FOX_W), lambda b, i, j: (b, jnp.minimum(i, j), kcol)),
                  pl.BlockSpec((None, tq, FOX_W), lambda b, i, j: (b, jnp.minimum(i, j), vcol)),
                  pl.BlockSpec((None, 8, tq), lambda b, i, j: (b, 0, jnp.minimum(i, j)))],
        operands=[pm3, pm3, pm3, cumT],
        out_specs=[ospec, ospec, pl.BlockSpec((None, tq, LANES), lambda b, i, j: (b, i, 0))],
        scratch_shapes=[pltpu.VMEM((2, tq, tq), F32), pltpu.VMEM((2, tq, tq), BF16), pltpu.VMEM((2, tq, 1), F32),
                        pltpu.VMEM((FOX_HEADS, tq, 1), F32), pltpu.VMEM((FOX_HEADS, tq, LANES), F32)],
        params=_params(("parallel", "parallel", "arbitrary")), rider=rider)


def _fox_bwd(pm3, do, delta, lse, cumT, qcol, kcol, vcol, tq, rider=None):
    B, S, _ = pm3.shape
    nq = S // tq
    strips = range(0, tq, FOX_STRIP)

    def body(q_ref, k_ref, v_ref, do_ref, dl_ref, lse_ref, ck_ref, dq_ref, rs_ref, dk_ref, dv_ref, cs_ref,
             s_sc, dp_sc, p_sc, ds_sc, dq_sc, dk_sc, dv_sc):
        kj, qi = pl.program_id(1), pl.program_id(2)

        @pl.when((kj == 0) & (qi == 0))
        def _():
            dq_sc[...] = jnp.zeros_like(dq_sc)

        @pl.when(qi == 0)
        def _():
            dk_sc[...] = jnp.zeros_like(dk_sc)
            dv_sc[...] = jnp.zeros_like(dv_sc)

        def tile(diagonal):
            qrows = pl.ds(pl.multiple_of(qi * tq, tq), tq)
            for h in range(FOX_HEADS):
                hs = slice(HEAD_DIM * h, HEAD_DIM * (h + 1))
                qh, kh, doh = q_ref[:, hs] * SCALE, k_ref[:, hs], do_ref[:, hs]
                b = h % 2
                s_sc[b] = lax.dot_general(qh, kh, NT, preferred_element_type=F32)
                dp_sc[b] = lax.dot_general(doh, v_ref[:, hs], NT, preferred_element_type=F32)
                ck = ck_ref[h:h + 1, :]
                for r in strips:
                    rows = slice(r, r + FOX_STRIP)
                    s = s_sc[b, rows, :] - ck
                    if diagonal:
                        s = _causal_strip(s, r)
                    p = jnp.exp(s - lse_ref[rows, h:h + 1])
                    p_sc[b, rows, :] = p.astype(BF16)
                    ds_sc[b, rows, :] = (p * (dp_sc[b, rows, :] - dl_ref[rows, h:h + 1])).astype(BF16)
                dv_sc[:, hs] += lax.dot_general(p_sc[b], doh, TN, preferred_element_type=F32)
                dk_sc[h] += lax.dot_general(ds_sc[b], _with_ones(qh), TN, preferred_element_type=F32)
                dq_sc[h, qrows, :] += lax.dot_general(ds_sc[b], _with_ones(kh), NN, preferred_element_type=F32)

        @pl.when(qi > kj)
        def _():
            tile(False)

        @pl.when(qi == kj)
        def _():
            tile(True)

        @pl.when(qi == nq - 1)
        def _():
            dv_ref[...] = dv_sc[...].astype(dv_ref.dtype)
            cs_ref[...] = jnp.zeros_like(cs_ref)
            for h in range(FOX_HEADS):
                hs = slice(HEAD_DIM * h, HEAD_DIM * (h + 1))
                dk = dk_sc[h]
                dk_ref[:, hs] = dk[:, :HEAD_DIM].astype(dk_ref.dtype)
                cs_ref[:, h:h + 1] = dk[:, HEAD_DIM:HEAD_DIM + 1]

        @pl.when((kj == nq - 1) & (qi == nq - 1))
        def _():
            rs_ref[...] = jnp.zeros_like(rs_ref)
            for h in range(FOX_HEADS):
                hs = slice(HEAD_DIM * h, HEAD_DIM * (h + 1))
                dq_ref[:, hs] = (dq_sc[h, :, :HEAD_DIM] * SCALE).astype(dq_ref.dtype)
                rs_ref[:, h:h + 1] = dq_sc[h, :, HEAD_DIM:HEAD_DIM + 1]

    def qside(width, col=0):
        return pl.BlockSpec((None, tq, width), lambda b, j, i: (b, jnp.maximum(i, j), col))

    kspec = pl.BlockSpec((None, tq, FOX_W), lambda b, j, i: (b, j, 0))
    return _pcall(
        body, name="fox_backward",
        out_shape=[jax.ShapeDtypeStruct((B, S, FOX_W), BF16), jax.ShapeDtypeStruct((B, S, LANES), F32),
                   jax.ShapeDtypeStruct((B, S, FOX_W), BF16), jax.ShapeDtypeStruct((B, S, FOX_W), BF16),
                   jax.ShapeDtypeStruct((B, S, LANES), F32)],
        grid=(B, nq, nq),
        in_specs=[qside(FOX_W, qcol),
                  pl.BlockSpec((None, tq, FOX_W), lambda b, j, i: (b, j, kcol)),
                  pl.BlockSpec((None, tq, FOX_W), lambda b, j, i: (b, j, vcol)),
                  qside(FOX_W), qside(LANES), qside(LANES),
                  pl.BlockSpec((None, 8, tq), lambda b, j, i: (b, 0, j))],
        operands=[pm3, pm3, pm3, do, delta, lse, cumT],
        out_specs=[pl.BlockSpec((None, S, FOX_W), lambda b, j, i: (b, 0, 0)),
                   pl.BlockSpec((None, S, LANES), lambda b, j, i: (b, 0, 0)),
                   kspec, kspec, pl.BlockSpec((None, tq, LANES), lambda b, j, i: (b, j, 0))],
        scratch_shapes=[pltpu.VMEM((2, tq, tq), F32), pltpu.VMEM((2, tq, tq), F32), pltpu.VMEM((2, tq, tq), BF16),
                        pltpu.VMEM((2, tq, tq), BF16), pltpu.VMEM((FOX_HEADS, S, LANES), F32),
                        pltpu.VMEM((FOX_HEADS, tq, LANES), F32), pltpu.VMEM((tq, FOX_W), F32)],
        params=_params(("parallel", "arbitrary", "arbitrary")), rider=rider)


def _fox_delta(do, o32, T, S):
    def fn(dov, ov):
        prod = dov.astype(F32) * ov
        lane = lax.broadcasted_iota(jnp.int32, (dov.shape[0], LANES), 1)
        delta = jnp.zeros((dov.shape[0], LANES), F32)
        for h in range(FOX_HEADS):
            hs = slice(HEAD_DIM * h, HEAD_DIM * (h + 1))
            delta = jnp.where(lane == h, jnp.sum(prod[:, hs], axis=-1, keepdims=True), delta)
        return [delta]

    return _rowwise("fox_delta", fn, T, _div(S, 512, 8), [(do, "row", None), (o32, "row", None)], [("row", LANES, F32)], S)[0]


def _alibi_slope(group, head):
    return 2.0 ** (-ALIBI_MAX_BIAS * (group * DIL_HPG + head + 1) / (N_DIL * DIL_HPG))


def _dil_tiles(q, k_cur, k_prev, slope, dilation, has_prev):
    qi = lax.broadcasted_iota(jnp.int32, (BLOCK, BLOCK), 0)
    ki = lax.broadcasted_iota(jnp.int32, (BLOCK, BLOCK), 1)
    rel = (qi - ki).astype(F32)
    nt = (((1,), (1,)), ((), ()))
    s_cur = lax.dot_general(q, k_cur, nt, preferred_element_type=F32) * SCALE - (slope * dilation) * rel
    s_cur = jnp.where(ki <= qi, s_cur, NEG_INF)
    s_prev = lax.dot_general(q, k_prev, nt, preferred_element_type=F32) * SCALE - (slope * dilation) * (rel + BLOCK)
    s_prev = jnp.where((ki >= qi) & has_prev, s_prev, NEG_INF)
    return s_cur, s_prev


def _dil_forward(group, pmv, nmb, qa_blk, B, S):
    _, dilation = DIL_GROUPS[group]
    sub = S // dilation
    nb = sub // BLOCK
    qb, kb, vb = qa_blk + group, qa_blk + N_DIL + group, qa_blk + 2 * N_DIL + group

    def body(q_ref, kc_ref, kp_ref, vc_ref, vp_ref, o_ref, lse_ref):
        has_prev = pl.program_id(2) > 0
        lse_ref[...] = jnp.zeros_like(lse_ref)
        for h in range(DIL_HPG):
            hs = slice(HEAD_DIM * h, HEAD_DIM * (h + 1))
            s_cur, s_prev = _dil_tiles(q_ref[:, hs], kc_ref[:, hs], kp_ref[:, hs], _alibi_slope(group, h), dilation, has_prev)
            m = jnp.maximum(jnp.max(s_cur, axis=-1, keepdims=True), jnp.max(s_prev, axis=-1, keepdims=True))
            p_cur, p_prev = jnp.exp(s_cur - m), jnp.exp(s_prev - m)
            l = jnp.sum(p_cur, axis=-1, keepdims=True) + jnp.sum(p_prev, axis=-1, keepdims=True)
            nn = (((1,), (0,)), ((), ()))
            o = (lax.dot_general(p_cur.astype(BF16), vc_ref[:, hs], nn, preferred_element_type=F32)
                 + lax.dot_general(p_prev.astype(BF16), vp_ref[:, hs], nn, preferred_element_type=F32))
            o_ref[:, hs] = o / l
            lse_ref[:, h:h + 1] = m + jnp.log(l)

    def cur(col):
        return pl.BlockSpec((None, BLOCK, DIL_GW), lambda b, r, n: (b, n, r * nmb + col))

    def prev(col):
        return pl.BlockSpec((None, BLOCK, DIL_GW), lambda b, r, n: (b, jnp.maximum(n - 1, 0), r * nmb + col))

    return pl.pallas_call(
        body, name=f"dil_forward_{group}",
        out_shape=[jax.ShapeDtypeStruct((B, sub, dilation * DIL_GW), F32), jax.ShapeDtypeStruct((B, sub, dilation * LANES), F32)],
        grid=(B, dilation, nb),
        in_specs=[cur(qb), cur(kb), prev(kb), cur(vb), prev(vb)],
        out_specs=[pl.BlockSpec((None, BLOCK, DIL_GW), lambda b, r, n: (b, n, r)),
                   pl.BlockSpec((None, BLOCK, LANES), lambda b, r, n: (b, n, r))],
        compiler_params=_params(("parallel", "parallel", "arbitrary")),
    )(pmv, pmv, pmv, pmv, pmv)


def _residue_order(a, B, S, d):
    C = a.shape[-1]
    if d == 1:
        return a.reshape(B, S, C)
    return a.reshape(B, S // d, d, C).transpose(0, 2, 1, 3).reshape(B * d, S // d, C)


def _token_order(a, B, S, d):
    C = a.shape[-1]
    if d == 1:
        return a.reshape(B * S, C)
    return a.reshape(B, d, S // d, C).transpose(0, 2, 1, 3).reshape(B * S, C)


def _band_scores(qh, kcat, slope_d, has_prev):
    qi = lax.broadcasted_iota(jnp.int32, (BLOCK, 2 * BLOCK), 0)
    c = lax.broadcasted_iota(jnp.int32, (BLOCK, 2 * BLOCK), 1)
    s = lax.dot_general(qh, kcat, NT, preferred_element_type=F32) - slope_d * (BLOCK + qi - c).astype(F32)
    valid = (c >= qi) & (c <= qi + BLOCK)
    if has_prev is not None:
        valid = valid & ((c >= BLOCK) | has_prev)
    return jnp.where(valid, s, NEG_INF)


def _band_operands(j, cur_ref, prev_ref, hs):
    if j == 0:
        return jnp.concatenate([prev_ref[:, hs], cur_ref[0:BLOCK, hs]], axis=0)
    return cur_ref[(j - 1) * BLOCK:(j + 1) * BLOCK, hs]


def _dil_specs(Ls, qb, cols):
    nsub = qb // BLOCK
    qcol, kcol, vcol = cols

    def cur(col):
        return pl.BlockSpec((None, qb, DIL_GW), lambda s, n: (s, n, col))

    def prev(col):
        return pl.BlockSpec((None, BLOCK, DIL_GW), lambda s, n: (s, jnp.maximum(n * nsub - 1, 0), col))

    return [cur(qcol), cur(kcol), prev(kcol), cur(vcol), prev(vcol)]


def _dil_fwd(group, src, cols):
    _, dilation = DIL_GROUPS[group]
    nseq, Ls, _ = src.shape
    qb = _div(Ls, 512, BLOCK)
    nsub = qb // BLOCK

    def body(q_ref, kc_ref, kp_ref, vc_ref, vp_ref, o_ref, lse_ref):
        has_prev = pl.program_id(1) > 0
        lse_ref[...] = jnp.zeros_like(lse_ref)
        for h in range(DIL_HPG):
            hs = slice(HEAD_DIM * h, HEAD_DIM * (h + 1))
            scores = [_band_scores(q_ref[j * BLOCK:(j + 1) * BLOCK, hs] * SCALE, _band_operands(j, kc_ref, kp_ref, hs),
                                   _alibi_slope(group, h) * dilation, has_prev if j == 0 else None) for j in range(nsub)]
            pending = None

            def write(j, m, acc):
                rows = slice(j * BLOCK, (j + 1) * BLOCK)
                l = acc[:, HEAD_DIM:HEAD_DIM + 1]
                o_ref[rows, hs] = acc[:, :HEAD_DIM] / l
                lse_ref[rows, h:h + 1] = m + jnp.log(l)

            for j in range(nsub):
                m = jnp.max(scores[j], axis=-1, keepdims=True)
                p = jnp.exp(scores[j] - m).astype(BF16)
                acc = lax.dot_general(p, _with_ones(_band_operands(j, vc_ref, vp_ref, hs)), NN, preferred_element_type=F32)
                if pending is not None:
                    write(*pending)
                pending = (j, m, acc)
            write(*pending)

    return pl.pallas_call(
        body, name=f"dil_forward_{group}",
        out_shape=[jax.ShapeDtypeStruct((nseq, Ls, DIL_GW), F32), jax.ShapeDtypeStruct((nseq, Ls, LANES), F32)],
        grid=(nseq, Ls // qb),
        in_specs=_dil_specs(Ls, qb, cols),
        out_specs=[pl.BlockSpec((None, qb, DIL_GW), lambda s, n: (s, n, 0)),
                   pl.BlockSpec((None, qb, LANES), lambda s, n: (s, n, 0))],
        compiler_params=_params(("parallel", "arbitrary")),
    )(src, src, src, src, src)


def _dil_bwd(group, src, cols, Lr, dyr, dlr):
    _, dilation = DIL_GROUPS[group]
    nseq, Ls, _ = src.shape
    qb = _div(Ls, 512, BLOCK)
    nsub, nb = qb // BLOCK, Ls // qb

    def body(q_ref, kc_ref, kp_ref, vc_ref, vp_ref, L_ref, dy_ref, dl_ref, dq_ref, dk_ref, dv_ref, dk_sc, dv_sc):
        n = pl.program_id(1)
        has_prev = n > 0

        @pl.when(n == 0)
        def _():
            dk_sc[...] = jnp.zeros_like(dk_sc)
            dv_sc[...] = jnp.zeros_like(dv_sc)

        base = pl.multiple_of(n * qb, BLOCK)
        for h in range(DIL_HPG):
            hs = slice(HEAD_DIM * h, HEAD_DIM * (h + 1))
            blocks = [slice(j * BLOCK, (j + 1) * BLOCK) for j in range(nsub)]
            qhs = [q_ref[rows, hs] * SCALE for rows in blocks]
            kcats = [_band_operands(j, kc_ref, kp_ref, hs) for j in range(nsub)]
            dyhs = [dy_ref[rows, hs] for rows in blocks]
            scores = [_band_scores(qhs[j], kcats[j], _alibi_slope(group, h) * dilation, has_prev if j == 0 else None)
                      for j in range(nsub)]
            dps = [lax.dot_general(dyhs[j], _band_operands(j, vc_ref, vp_ref, hs), NT, preferred_element_type=F32)
                   for j in range(nsub)]
            pending = None

            def write(j, dq, dk, dv):
                dq_ref[blocks[j], hs] = (dq * SCALE).astype(dq_ref.dtype)
                win = pl.ds(base + j * BLOCK, 2 * BLOCK)
                dk_sc[win, hs] += dk
                dv_sc[win, hs] += dv

            for j in range(nsub):
                p = jnp.exp(scores[j] - L_ref[blocks[j], h:h + 1])
                ds = (p * (dps[j] - dl_ref[blocks[j], h:h + 1])).astype(BF16)
                dq = lax.dot_general(ds, kcats[j], NN, preferred_element_type=F32)
                dk = lax.dot_general(ds, qhs[j], TN, preferred_element_type=F32)
                dv = lax.dot_general(p.astype(BF16), dyhs[j], TN, preferred_element_type=F32)
                if pending is not None:
                    write(*pending)
                pending = (j, dq, dk, dv)
            write(*pending)

        @pl.when(n == nb - 1)
        def _():
            dk_ref[...] = dk_sc[BLOCK:, :].astype(dk_ref.dtype)
            dv_ref[...] = dv_sc[BLOCK:, :].astype(dv_ref.dtype)

    own = pl.BlockSpec((None, qb, DIL_GW), lambda s, n: (s, n, 0))
    own128 = pl.BlockSpec((None, qb, LANES), lambda s, n: (s, n, 0))
    whole = pl.BlockSpec((None, Ls, DIL_GW), lambda s, n: (s, 0, 0))
    shape = jax.ShapeDtypeStruct((nseq, Ls, DIL_GW), BF16)
    return pl.pallas_call(
        body, name=f"dil_backward_{group}",
        out_shape=[shape, shape, shape],
        grid=(nseq, nb),
        in_specs=_dil_specs(Ls, qb, cols) + [own128, own, own128],
        out_specs=[own, whole, whole],
        scratch_shapes=[pltpu.VMEM((Ls + BLOCK, DIL_GW), F32), pltpu.VMEM((Ls + BLOCK, DIL_GW), F32)],
        compiler_params=_params(("parallel", "arbitrary")),
    )(src, src, src, src, src, Lr, dyr, dlr)


def _dil_combine(os_, lses, T, S):
    def fn(o0, o1, o2, l0, l1, l2):
        m = jnp.maximum(jnp.maximum(l0, l1), l2)
        e0, e1, e2 = jnp.exp(l0 - m), jnp.exp(l1 - m), jnp.exp(l2 - m)
        tot = e0 + e1 + e2
        w0, w1, w2 = e0 / tot, e1 / tot, e2 / tot
        parts = []
        for h in range(DIL_HPG):
            hs = slice(HEAD_DIM * h, HEAD_DIM * (h + 1))
            parts.append(w0[:, h:h + 1] * o0[:, hs] + w1[:, h:h + 1] * o1[:, hs] + w2[:, h:h + 1] * o2[:, hs])
        return [jnp.concatenate(parts, axis=1), m + jnp.log(tot)]

    ins = [(a, "row", None) for a in os_] + [(a, "row", None) for a in lses]
    return _rowwise("dil_combine", fn, T, _div(S, 512, 8), ins, [("row", DIL_GW, BF16), ("row", LANES, F32)], S)


def _dil_delta(dy, y, T, S):
    def fn(dyv, yv):
        prod = dyv * yv.astype(F32)
        lane = lax.broadcasted_iota(jnp.int32, (dyv.shape[0], LANES), 1)
        delta = jnp.zeros((dyv.shape[0], LANES), F32)
        for h in range(DIL_HPG):
            hs = slice(HEAD_DIM * h, HEAD_DIM * (h + 1))
            delta = jnp.where(lane == h, jnp.sum(prod[:, hs], axis=-1, keepdims=True), delta)
        return [delta, dyv]

    return _rowwise("dil_delta", fn, T, _div(S, 512, 8), [(dy, "row", None), (y, "row", None)],
                    [("row", LANES, F32), ("row", DIL_GW, BF16)], S)


def _dil_dq(group, pmv, nmb, qa_blk, Lv, dyv, deltav, B, S):
    _, dilation = DIL_GROUPS[group]
    sub = S // dilation
    nb = sub // BLOCK
    qb, kb, vb = qa_blk + group, qa_blk + N_DIL + group, qa_blk + 2 * N_DIL + group

    def body(q_ref, kc_ref, kp_ref, vc_ref, vp_ref, L_ref, dy_ref, dl_ref, dq_ref):
        has_prev = pl.program_id(2) > 0
        nt = (((1,), (1,)), ((), ()))
        nn = (((1,), (0,)), ((), ()))
        for h in range(DIL_HPG):
            hs = slice(HEAD_DIM * h, HEAD_DIM * (h + 1))
            s_cur, s_prev = _dil_tiles(q_ref[:, hs], kc_ref[:, hs], kp_ref[:, hs], _alibi_slope(group, h), dilation, has_prev)
            L, delta, dyh = L_ref[:, h:h + 1], dl_ref[:, h:h + 1], dy_ref[:, hs]
            ds_cur = jnp.exp(s_cur - L) * (lax.dot_general(dyh, vc_ref[:, hs], nt, preferred_element_type=F32) - delta)
            ds_prev = jnp.exp(s_prev - L) * (lax.dot_general(dyh, vp_ref[:, hs], nt, preferred_element_type=F32) - delta)
            dq = (lax.dot_general(ds_cur.astype(BF16), kc_ref[:, hs], nn, preferred_element_type=F32)
                  + lax.dot_general(ds_prev.astype(BF16), kp_ref[:, hs], nn, preferred_element_type=F32))
            dq_ref[:, hs] = (dq * SCALE).astype(dq_ref.dtype)

    def cur(col):
        return pl.BlockSpec((None, BLOCK, DIL_GW), lambda b, r, n: (b, n, r * nmb + col))

    def prev(col):
        return pl.BlockSpec((None, BLOCK, DIL_GW), lambda b, r, n: (b, jnp.maximum(n - 1, 0), r * nmb + col))

    own = pl.BlockSpec((None, BLOCK, DIL_GW), lambda b, r, n: (b, n, r))
    own128 = pl.BlockSpec((None, BLOCK, LANES), lambda b, r, n: (b, n, r))
    return pl.pallas_call(
        body, name=f"dil_dq_{group}",
        out_shape=jax.ShapeDtypeStruct((B, sub, dilation * DIL_GW), BF16),
        grid=(B, dilation, nb),
        in_specs=[cur(qb), cur(kb), prev(kb), cur(vb), prev(vb), own128, own, own128],
        out_specs=own,
        compiler_params=_params(("parallel", "parallel", "arbitrary")),
    )(pmv, pmv, pmv, pmv, pmv, Lv, dyv, deltav)


def _dil_dkv(group, pmv, nmb, qa_blk, Lv, dyv, deltav, B, S):
    _, dilation = DIL_GROUPS[group]
    sub = S // dilation
    nb = sub // BLOCK
    qb, kb, vb = qa_blk + group, qa_blk + N_DIL + group, qa_blk + 2 * N_DIL + group

    def body(k_ref, v_ref, q0_ref, q1_ref, L0_ref, L1_ref, dy0_ref, dy1_ref, dl0_ref, dl1_ref, dk_ref, dv_ref):
        has_next = pl.program_id(2) < nb - 1
        qi = lax.broadcasted_iota(jnp.int32, (BLOCK, BLOCK), 0)
        ki = lax.broadcasted_iota(jnp.int32, (BLOCK, BLOCK), 1)
        rel = (qi - ki).astype(F32)
        nt = (((1,), (1,)), ((), ()))
        tn = (((0,), (0,)), ((), ()))
        for h in range(DIL_HPG):
            hs = slice(HEAD_DIM * h, HEAD_DIM * (h + 1))
            bias = _alibi_slope(group, h) * dilation
            kh, vh, q0, q1 = k_ref[:, hs], v_ref[:, hs], q0_ref[:, hs], q1_ref[:, hs]
            s0 = lax.dot_general(q0, kh, nt, preferred_element_type=F32) * SCALE - bias * rel
            s0 = jnp.where(ki <= qi, s0, NEG_INF)
            s1 = lax.dot_general(q1, kh, nt, preferred_element_type=F32) * SCALE - bias * (rel + BLOCK)
            s1 = jnp.where((ki >= qi) & has_next, s1, NEG_INF)
            p0 = jnp.exp(s0 - L0_ref[:, h:h + 1])
            p1 = jnp.exp(s1 - L1_ref[:, h:h + 1])
            dy0, dy1 = dy0_ref[:, hs], dy1_ref[:, hs]
            ds0 = p0 * (lax.dot_general(dy0, vh, nt, preferred_element_type=F32) - dl0_ref[:, h:h + 1])
            ds1 = p1 * (lax.dot_general(dy1, vh, nt, preferred_element_type=F32) - dl1_ref[:, h:h + 1])
            dv = (lax.dot_general(p0.astype(BF16), dy0, tn, preferred_element_type=F32)
                  + lax.dot_general(p1.astype(BF16), dy1, tn, preferred_element_type=F32))
            dk = (lax.dot_general(ds0.astype(BF16), q0, tn, preferred_element_type=F32)
                  + lax.dot_general(ds1.astype(BF16), q1, tn, preferred_element_type=F32))
            dv_ref[:, hs] = dv.astype(dv_ref.dtype)
            dk_ref[:, hs] = (dk * SCALE).astype(dk_ref.dtype)

    def cur(col):
        return pl.BlockSpec((None, BLOCK, DIL_GW), lambda b, r, n: (b, n, r * nmb + col))

    def nxt(col):
        return pl.BlockSpec((None, BLOCK, DIL_GW), lambda b, r, n: (b, jnp.minimum(n + 1, nb - 1), r * nmb + col))

    own = pl.BlockSpec((None, BLOCK, DIL_GW), lambda b, r, n: (b, n, r))
    own_next = pl.BlockSpec((None, BLOCK, DIL_GW), lambda b, r, n: (b, jnp.minimum(n + 1, nb - 1), r))
    own128 = pl.BlockSpec((None, BLOCK, LANES), lambda b, r, n: (b, n, r))
    own128_next = pl.BlockSpec((None, BLOCK, LANES), lambda b, r, n: (b, jnp.minimum(n + 1, nb - 1), r))
    shape = jax.ShapeDtypeStruct((B, sub, dilation * DIL_GW), BF16)
    return pl.pallas_call(
        body, name=f"dil_dkv_{group}",
        out_shape=[shape, shape],
        grid=(B, dilation, nb),
        in_specs=[cur(kb), cur(vb), cur(qb), nxt(qb), own128, own128_next, own, own_next, own128, own128_next],
        out_specs=[own, own],
        compiler_params=_params(("parallel", "parallel", "arbitrary")),
    )(pmv, pmv, pmv, pmv, Lv, Lv, dyv, dyv, deltav, deltav)


def _ada_forward(c_all, w, b):
    n, D = c_all.shape
    cl = w.shape[1]

    def body(c_ref, w_ref, b_ref, o_ref, ca_ref):
        cv = c_ref[...]
        ca = (cv * _sigmoid(cv)).astype(BF16)
        ca_ref[...] = ca
        o_ref[...] = jnp.dot(ca, w_ref[...].astype(BF16), preferred_element_type=F32) + b_ref[...]

    return pl.pallas_call(
        body, name="ada_forward",
        out_shape=[jax.ShapeDtypeStruct((n, cl), F32), jax.ShapeDtypeStruct((n, D), BF16)],
        compiler_params=_params(),
    )(c_all, w, b)


def _ada_backward(ca, dmod_cols, dmod_all):
    n, D = ca.shape
    cl = dmod_cols.shape[1]

    def body(ca_ref, dc_ref, da_ref, gw_ref, gb_ref):
        gw_ref[...] = lax.dot_general(ca_ref[...], dc_ref[...].astype(BF16), (((0,), (0,)), ((), ())), preferred_element_type=F32)
        gb_ref[...] = _colsum(da_ref[...])

    return pl.pallas_call(
        body, name="ada_backward",
        out_shape=[jax.ShapeDtypeStruct((D, cl), F32), jax.ShapeDtypeStruct((1, dmod_all.shape[1]), F32)],
        compiler_params=_params(),
    )(ca, dmod_cols, dmod_all)


def _sum_devices(v):
    def body(v_ref, o_ref):
        tot = v_ref[0]
        for k in range(1, N_DEV):
            tot = tot + v_ref[k]
        o_ref[...] = tot

    return pl.pallas_call(body, name="sum_devices", out_shape=jax.ShapeDtypeStruct(v.shape[1:], F32))(v)


def _adamw(name, w, g, m, v):
    rows, cols = w.shape
    tr = _div(rows, 256, 8)

    def body(w_ref, g_ref, m_ref, v_ref, d_ref, nm_ref, nv_ref):
        gv = g_ref[...]
        nm = ADAM_B1 * m_ref[...] + (1.0 - ADAM_B1) * gv
        nv = ADAM_B2 * v_ref[...] + (1.0 - ADAM_B2) * (gv * gv)
        m_hat = nm / (1.0 - ADAM_B1 ** ADAM_STEP)
        v_hat = nv / (1.0 - ADAM_B2 ** ADAM_STEP)
        d_ref[...] = -ADAM_LR * (m_hat / (jnp.sqrt(v_hat) + ADAM_EPS) + ADAM_WD * w_ref[...])
        nm_ref[...] = nm
        nv_ref[...] = nv

    spec = pl.BlockSpec((tr, cols), lambda i: (i, 0))
    shape = jax.ShapeDtypeStruct((rows, cols), F32)
    return pl.pallas_call(
        body, name=name, out_shape=[shape, shape, shape], grid=(rows // tr,),
        in_specs=[spec, spec, spec, spec], out_specs=[spec, spec, spec],
        compiler_params=_params(("arbitrary",)),
    )(w, g, m, v)


def _pad_rows(a, rows):
    return a if a.shape[0] == rows else jnp.pad(a, ((0, rows - a.shape[0]), (0, 0)))


class _Packed:
    def __init__(self, kind, local_shape, D):
        self.kind, self.local_shape, self.D = kind, local_shape, D
        r, c = local_shape
        self.rows = {"T": c, "N": r, "F": r * c // D}[kind]
        self.rows_pad = -(-self.rows // ROW_ALIGN) * ROW_ALIGN

    def pack_local(self, w):
        if self.kind == "T":
            w = w.T
        elif self.kind == "F":
            w = w.reshape(self.rows, self.D)
        return _pad_rows(w, self.rows_pad)

    def full(self, gathered):
        g = gathered[:, :self.rows]
        if self.kind == "F":
            r, c = self.local_shape
            return g.reshape(N_DEV, r, c).transpose(1, 0, 2).reshape(r, N_DEV * c)
        return g.reshape(N_DEV * self.rows, self.D)

    def pack_grad(self, gfull):
        if self.kind == "F":
            r, c = self.local_shape
            g = gfull.reshape(r, N_DEV, c).transpose(1, 0, 2).reshape(N_DEV, self.rows, self.D)
        else:
            g = gfull.reshape(N_DEV, self.rows, self.D)
        if self.rows_pad != self.rows:
            g = jnp.pad(g, ((0, 0), (0, self.rows_pad - self.rows), (0, 0)))
        return g

    def unpack_local(self, g):
        g = g[:self.rows]
        if self.kind == "T":
            return g.T
        if self.kind == "F":
            return g.reshape(self.local_shape)
        return g


BIG = ["ffn1_w_gate", "ffn1_w_up", "ffn1_w_down", "w_in", "w_branch_a", "w_branch_b", "w_out",
       "ffn2_w_gate", "ffn2_w_up", "ffn2_w_down"]
BIG_KIND = {"ffn1_w_gate": "T", "ffn1_w_up": "T", "ffn1_w_down": "N", "w_in": "T", "w_branch_a": "F", "w_branch_b": "F",
            "w_out": "N", "ffn2_w_gate": "T", "ffn2_w_up": "T", "ffn2_w_down": "N"}
GROUPS = (("ffn1_w_gate", "ffn1_w_up", "ffn1_w_down"), ("w_in", "w_branch_a", "w_branch_b", "w_out"),
          ("ffn2_w_gate", "ffn2_w_up", "ffn2_w_down"))
SMALL = ["ada_b", "norm_ffn1", "norm_mix", "forget_bias", "norm_ffn2", "norm_final"]


def kernel(x, c, ada_w, ada_b, norm_ffn1, ffn1_w_gate, ffn1_w_up, ffn1_w_down, norm_mix, w_in, forget_bias, w_branch_a, w_branch_b, w_out, norm_ffn2, ffn2_w_gate, ffn2_w_up, ffn2_w_down, norm_final, loss_target, m_ada_w, m_ada_b, m_norm_ffn1, m_ffn1_w_gate, m_ffn1_w_up, m_ffn1_w_down, m_norm_mix, m_w_in, m_forget_bias, m_w_branch_a, m_w_branch_b, m_w_out, m_norm_ffn2, m_ffn2_w_gate, m_ffn2_w_up, m_ffn2_w_down, m_norm_final, v_ada_w, v_ada_b, v_norm_ffn1, v_ffn1_w_gate, v_ffn1_w_up, v_ffn1_w_down, v_norm_mix, v_w_in, v_forget_bias, v_w_branch_a, v_w_branch_b, v_w_out, v_norm_ffn2, v_ffn2_w_gate, v_ffn2_w_up, v_ffn2_w_down, v_norm_final):
    args = dict(locals())
    B, S, D = x.shape
    T = B * S
    cl = ada_w.shape[2]
    n_in = w_in.shape[2] * N_DEV
    nm = 2 * D + 3 * FOX_W + 3 * DIL_W
    nmp = -(-nm // 512) * 512
    GA, GB, QB, QA = 0, D, 2 * D, 2 * D + 3 * FOX_W
    xpos, ypos, cpos = _position()
    me = 4 * xpos + 2 * ypos + cpos

    packs = {n: _Packed(BIG_KIND[n], args[n].shape[1:], D) for n in BIG}
    offs, pads = {}, {}
    for names in GROUPS:
        r = 0
        for n in names:
            offs[n] = r
            r += packs[n].rows_pad
        pads[names] = -r % PACK_ROW_QUANTUM

    def pack_weights(names):
        return jnp.concatenate([packs[n].pack_local(args[n][0]).astype(BF16) for n in names]
                               + [jnp.zeros((pads[names], D), BF16)], axis=0)

    def unpack_weights(names, land):
        return {n: packs[n].full(land[:, offs[n]:offs[n] + packs[n].rows_pad]) for n in names}

    def pack_grads(names, gfull):
        return jnp.concatenate([packs[n].pack_grad(gfull[n]) for n in names] + [jnp.zeros((N_DEV, pads[names], D), F32)], axis=1)

    def unpack_grads(names, g_local):
        return {n: packs[n].unpack_local(g_local[offs[n]:offs[n] + packs[n].rows_pad])[None] for n in names}

    W = unpack_weights(GROUPS[0], _weight_allgather(pack_weights(GROUPS[0])))

    c_all = _small_allgather(c, "gather_c").reshape(N_DEV * B, D)
    b_cols = lax.dynamic_slice(ada_b, (0, me * cl), (1, cl))
    mod_cols, c_act = _ada_forward(c_all, ada_w[0], b_cols)
    mod_all = _small_allgather(mod_cols, "gather_mod").transpose(1, 0, 2).reshape(N_DEV * B, N_MOD * D)
    mod = lax.dynamic_slice(mod_all, (me * B, 0), (B, N_MOD * D)).reshape(B, N_MOD, 1, D)
    sh1, sc1, gt1, sh2, sc2, gt2, sh3, sc3, gt3 = [mod[:, i] for i in range(N_MOD)]

    x0 = x.reshape(T, D)
    x1, saved1, land = _ffn_forward("ffn1", x0, norm_ffn1, sh1, sc1, gt1, W["ffn1_w_gate"], W["ffn1_w_up"], W["ffn1_w_down"], S,
                                    gather=pack_weights(GROUPS[1]))
    W.update(unpack_weights(GROUPS[1], land))
    winT = W["w_in"]
    o_f = 3 * DIL_W + 3 * FOX_W
    wmT = jnp.concatenate([winT[o_f + 8:], winT[3 * DIL_W:o_f], winT[:3 * DIL_W], jnp.zeros((nmp - nm, D), BF16)], axis=0)
    wfT = jnp.concatenate([winT[o_f:o_f + 8], jnp.zeros((LANES - 8, D), BF16)], axis=0)

    tm1k = _div(T, 1024, 8)
    fb = jnp.pad(forget_bias, ((0, 0), (0, LANES - FOX_HEADS)))

    def proj(accs, fbv):
        fl = accs[1] + fbv
        lane = lax.broadcasted_iota(jnp.int32, fl.shape, 1)
        ls = jnp.minimum(fl, 0.0) - jnp.log(1.0 + jnp.exp(-jnp.abs(fl)))
        return [accs[0], jnp.where(lane < FOX_HEADS, ls, 0.0), fl]

    tms = _div(S, 512, 8)
    h2, pm, logsig, flog, land = _norm_matmul("mix_proj", x1, norm_mix, sc2, sh2, [wmT, wfT], proj,
                                              [(nmp, BF16), (LANES, F32), (LANES, F32)], S, vecs=[fb],
                                              rider=_gather_direct_rider(pack_weights(GROUPS[2])))
    cum = _cumsum(logsig.reshape(B, S, LANES))
    cumT = cum[:, :, :8].transpose(0, 2, 1)
    pm3 = pm.reshape(B, S, nmp)
    tq = _div(S, 512, LANES)
    qcol, kcol, vcol = QB // FOX_W, QB // FOX_W + 1, QB // FOX_W + 2
    o_b, o_b32, lse_b, land = _fox_fwd(pm3, cumT, qcol, kcol, vcol, tq, rider=_gather_forward_rider(land))
    W.update(unpack_weights(GROUPS[2], land))
    y_b = o_b.reshape(T, FOX_W)

    qa_blk = QA // DIL_GW
    dil_src, dil_cols = [], []
    for g, (_, d) in enumerate(DIL_GROUPS):
        if d == 1:
            dil_src.append(pm3)
            dil_cols.append((qa_blk + g, qa_blk + N_DIL + g, qa_blk + 2 * N_DIL + g))
        else:
            starts = [QA + (i * N_DIL + g) * DIL_GW for i in range(3)]
            qkv = jnp.concatenate([pm[:, c:c + DIL_GW] for c in starts], axis=1)
            dil_src.append(_residue_order(qkv, B, S, d))
            dil_cols.append((0, 1, 2))
    dil_o, dil_lse = [], []
    for g, (_, d) in enumerate(DIL_GROUPS):
        o_g, lse_g = _dil_fwd(g, dil_src[g], dil_cols[g])
        dil_o.append(_token_order(o_g, B, S, d))
        dil_lse.append(_token_order(lse_g, B, S, d))
    y_a, L_a = _dil_combine(dil_o, dil_lse, T, S)

    wa, wb, wout = W["w_branch_a"], W["w_branch_b"], W["w_out"]
    tnd = D
    tm5 = _div(T, 512, 8)
    yap = _matmul("mix_branch_a", "nn", [[(y_a, wa)]], T, D, DIL_GW, tm5, tnd, DIL_GW, [BF16])[0]

    def merge(accs, ex):
        yapv, gav, gbv = ex
        ybp = accs[0]
        return [ybp, _sigmoid(gav.astype(F32)) * yapv.astype(F32) + _sigmoid(gbv.astype(F32)) * ybp]

    ybp, merged = _matmul("mix_branch_b", "nn", [[(y_b, wb)]], T, D, FOX_W, tm5, tnd, FOX_W, [BF16, BF16],
                          extras=[(yap, "tile", 0), (pm, "tile", GA), (pm, "tile", GB)], epilogue=merge)

    def out_proj(accs, ex):
        xv, gtv = ex
        return [xv + gtv * accs[0], accs[0]]

    x2, ymix = _matmul("mix_out", "nn", [[(merged, wout)]], T, D, D, tms, tnd, D, [F32, BF16],
                       extras=[(x1, "tile", 0), (gt2, "brow", 0)], epilogue=out_proj, rows_per_example=S)

    x3, saved3, _ = _ffn_forward("ffn2", x2, norm_ffn2, sh3, sc3, gt3, W["ffn2_w_gate"], W["ffn2_w_up"], W["ffn2_w_down"], S)

    dx3, loss_b, dg_final = _loss_head(x3, loss_target.reshape(T, D), norm_final.reshape(1, D), S)
    dx2, (dsh3, dsc3, dgt3, dg3), (dwg2, dwu2, dwd2), _ = _ffn_backward(
        "ffn2", dx3, saved3, norm_ffn2, sc3, gt3, W["ffn2_w_gate"], W["ffn2_w_up"], W["ffn2_w_down"], S)
    g3 = pack_grads(GROUPS[2], {"ffn2_w_gate": dwg2, "ffn2_w_up": dwu2, "ffn2_w_down": dwd2})

    def merge_grad(dm, ex):
        gav, gbv, yapv, ybpv = [e.astype(F32) for e in ex]
        sga, sgb = _sigmoid(gav), _sigmoid(gbv)
        return [dm * sga, dm * sgb, dm * yapv * sga * (1.0 - sga), dm * ybpv * sgb * (1.0 - sgb)]

    dym, dgt2, dyap, dybp, dga, dgb, sib3 = _gated_grad_matmul(
        "mix_merge_grad", dx2, ymix, gt2, 1.0, wout, [(pm, GA // D), (pm, GB // D), (yap, 0), (ybp, 0)], merge_grad, [BF16] * 4, S,
        rider=_sibling_exchange_rider(g3))
    sums3, own3 = _chip_sums(g3, sib3)
    tkw = _div(T, 512, LANES)
    dwout = _matmul("mix_dw_out", "tn", [[(merged, dym)]], D, D, T, D, D, tkw, [F32])[0]
    dwa = _matmul("mix_dw_a", "tn", [[(y_a, dyap)]], DIL_GW, D, T, DIL_GW, D, tkw, [F32])[0]
    dwb = _matmul("mix_dw_b", "tn", [[(y_b, dybp)]], FOX_W, D, T, FOX_W, D, tkw, [F32])[0]
    dy_a = _matmul("mix_dy_a", "nt", [[(dyap, wa)]], T, DIL_GW, D, tm1k, DIL_GW, D, [F32])[0]
    dy_b = _matmul("mix_dy_b", "nt", [[(dybp, wb)]], T, FOX_W, D, tm1k, FOX_W, D, [BF16])[0]

    do3 = dy_b.reshape(B, S, FOX_W)
    delta_b = _fox_delta(dy_b, o_b32.reshape(T, FOX_W), T, S).reshape(B, S, LANES)
    dq_b, ds_rows, dk_b, dv_b, ds_cols, recv3 = _fox_bwd(pm3, do3, delta_b, lse_b, cumT, qcol, kcol, vcol, tq,
                                                         rider=_chip_exchange_rider(sums3))

    delta_a, dy_a16 = _dil_delta(dy_a, y_a, T, S)
    dqs, dks, dvs = [], [], []
    for g, (_, d) in enumerate(DIL_GROUPS):
        dq_g, dk_g, dv_g = _dil_bwd(g, dil_src[g], dil_cols[g], _residue_order(L_a, B, S, d),
                                    _residue_order(dy_a16, B, S, d), _residue_order(delta_a, B, S, d))
        dqs.append(_token_order(dq_g, B, S, d))
        dks.append(_token_order(dk_g, B, S, d))
        dvs.append(_token_order(dv_g, B, S, d))

    dcum = ds_rows - ds_cols
    dcum_run = _cumsum(dcum)
    dcum_tot = dcum_run[:, S - 1:S, :]

    def forget_grad_fn(run, dcv, fl, tot):
        lane = lax.broadcasted_iota(jnp.int32, fl.shape, 1)
        df = jnp.where(lane < FOX_HEADS, (tot - run + dcv) * _sigmoid(-fl), 0.0)
        return [df, _colsum(df)]

    df16, dfb = _rowwise("forget_gate_grad", forget_grad_fn, T, tms,
                         [(dcum_run.reshape(T, LANES), "row", None), (dcum.reshape(T, LANES), "row", None), (flog, "row", None),
                          (dcum_tot, "bvec", None)],
                         [("row", LANES, BF16), ("bacc", LANES, F32)], S)

    dpm = jnp.concatenate([dga, dgb, dq_b.reshape(T, FOX_W), dk_b.reshape(T, FOX_W), dv_b.reshape(T, FOX_W)]
                          + dqs + dks + dvs + ([jnp.zeros((T, nmp - nm), BF16)] if nmp > nm else []), axis=1)
    tmn = _div(nmp, 2048, LANES)
    dwmT = _matmul("mix_dw_in", "tn", [[(dpm, h2)]], nmp, D, T, tmn, D, tkw, [F32])[0]
    dwfT = _matmul("mix_dw_f", "tn", [[(df16, h2)]], LANES, D, T, LANES, D, tkw, [F32])[0]
    dwinT = jnp.concatenate([dwmT[QA:QA + 3 * DIL_W], dwmT[QB:QB + 3 * FOX_W], dwfT[:8], dwmT[GA:2 * D]], axis=0)
    g2 = pack_grads(GROUPS[1], {"w_in": dwinT, "w_branch_a": dwa, "w_branch_b": dwb, "w_out": dwout})
    dx1, dsh2, dsc2, dgmix, sib2 = _matmul_normmod_bwd("mix_dh", [(dpm, wmT), (df16, wfT)], x1, norm_mix, sc2, dx2, S,
                                                       rider=_sibling_exchange_rider(g2))
    sums2, own2 = _chip_sums(g2, sib2)

    own1 = []

    def ffn1_exchange(dwg, dwu, dwd):
        sums1, own = _chip_sums(pack_grads(GROUPS[0], {"ffn1_w_gate": dwg, "ffn1_w_up": dwu, "ffn1_w_down": dwd}))
        own1.append(own)
        return _chip_exchange_rider(sums1)

    dx0, (dsh1, dsc1, dgt1, dg1), _, (recv2, recv1) = _ffn_backward(
        "ffn1", dx1, saved1, norm_ffn1, sc1, gt1, W["ffn1_w_gate"], W["ffn1_w_up"], W["ffn1_w_down"], S,
        rider=_chip_exchange_rider(sums2), dh_rider=ffn1_exchange)
    own1 = own1[0]
    grad_x = dx0.reshape(B, S, D)

    dmod = jnp.concatenate([dsh1, dsc1, dgt1, dsh2, dsc2, dgt2, dsh3, dsc3, dgt3], axis=1).reshape(B, N_MOD * D)
    dmod_all = _small_allgather(dmod, "gather_dmod").reshape(N_DEV * B, N_MOD * D)
    dmod_cols = lax.dynamic_slice(dmod_all, (0, me * cl), (N_DEV * B, cl))
    g_ada_w, g_ada_b = _ada_backward(c_act, dmod_cols, dmod_all)

    fbg = jnp.sum(dfb, axis=0)
    small = jnp.concatenate([jnp.sum(dg1, axis=0), jnp.sum(dgmix, axis=0), jnp.sum(dg3, axis=0), jnp.sum(dg_final, axis=0),
                             fbg, jnp.sum(loss_b, axis=0)], axis=1)
    small = _sum_devices(_small_allgather(small, "gather_small"))
    g_small = {"norm_ffn1": small[:, 0:D], "norm_mix": small[:, D:2 * D], "norm_ffn2": small[:, 2 * D:3 * D],
               "norm_final": small[:, 3 * D:4 * D], "forget_bias": small[:, 4 * D:4 * D + FOX_HEADS], "ada_b": g_ada_b}
    loss = small[0, 4 * D + LANES]

    grads = {"ada_w": g_ada_w[None]}
    for names, own, recv in ((GROUPS[0], own1, recv1), (GROUPS[1], own2, recv2), (GROUPS[2], own3, recv3)):
        grads.update(unpack_grads(names, _final_grad_sum(own, recv)))

    delta, new_m, new_v = {}, {}, {}
    for n in ["ada_w"] + BIG:
        shp = args[n].shape
        d_, m_, v_ = _adamw(f"adamw_{n}", args[n][0], grads[n][0], args["m_" + n][0], args["v_" + n][0])
        delta[n], new_m[n], new_v[n] = d_.reshape(shp), m_.reshape(shp), v_.reshape(shp)
    sizes = [args[n].size for n in SMALL]
    tot = sum(sizes)
    padded = -(-tot // (8 * LANES)) * (8 * LANES)

    def flat(get):
        v = jnp.concatenate([get(n).reshape(-1) for n in SMALL])
        return jnp.pad(v, (0, padded - tot)).reshape(8, padded // 8)

    d_s, m_s, v_s = _adamw("adamw_small", flat(lambda n: args[n]), flat(lambda n: g_small[n]), flat(lambda n: args["m_" + n]),
                           flat(lambda n: args["v_" + n]))
    o = 0
    for n, sz in zip(SMALL, sizes):
        shp = args[n].shape
        grads[n] = g_small[n].reshape(shp)
        delta[n] = d_s.reshape(-1)[o:o + sz].reshape(shp)
        new_m[n] = m_s.reshape(-1)[o:o + sz].reshape(shp)
        new_v[n] = v_s.reshape(-1)[o:o + sz].reshape(shp)
        o += sz

    order = ["ada_w", "ada_b", "norm_ffn1", "ffn1_w_gate", "ffn1_w_up", "ffn1_w_down", "norm_mix", "w_in", "forget_bias",
             "w_branch_a", "w_branch_b", "w_out", "norm_ffn2", "ffn2_w_gate", "ffn2_w_up", "ffn2_w_down", "norm_final"]
    return (loss, grad_x, *[grads[n] for n in order], *[delta[n] for n in order], *[new_m[n] for n in order],
            *[new_v[n] for n in order])
```

```python
import functools
import math

import jax
import jax.numpy as jnp
from jax import lax
from jax.experimental import pallas as pl
from jax.experimental.pallas import tpu as pltpu

F32 = jnp.float32
BF16 = jnp.bfloat16
MESH = pl.DeviceIdType.MESH
ANY = pl.BlockSpec(memory_space=pl.ANY)
VMEM_SPEC = pl.BlockSpec(memory_space=pltpu.VMEM)

N_DEV = 8
HEAD_DIM = 64
BLOCK = 128
DIL_GROUPS = ((128, 1), (512, 4), (2048, 16))
N_DIL = len(DIL_GROUPS)
DIL_HPG = 4
DIL_GW = DIL_HPG * HEAD_DIM
DIL_W = N_DIL * DIL_GW
FOX_HEADS = 8
FOX_W = FOX_HEADS * HEAD_DIM
N_MOD = 9
RMS_EPS = 1e-6
ALIBI_MAX_BIAS = 8.0
NEG_INF = -1e30
ADAM_LR, ADAM_B1, ADAM_B2, ADAM_EPS, ADAM_WD, ADAM_STEP = 0.001, 0.9, 0.999, 1e-08, 0.01, 10
V7X_VMEM_LIMIT = 52 * 1024 * 1024
LANES = 128
ROW_ALIGN = 16
PACK_ROW_QUANTUM = 32
FOX_STRIP = 32
SCALE = 1.0 / math.sqrt(HEAD_DIM)


def _div(dim, target, quantum):
    best = None
    for t in range(quantum, min(dim, target) + 1, quantum):
        if dim % t == 0:
            best = t
    return best or dim


def _params(sem=None):
    return pltpu.CompilerParams(dimension_semantics=sem, vmem_limit_bytes=V7X_VMEM_LIMIT)


def _sigmoid(x):
    return 0.5 * jnp.tanh(0.5 * x) + 0.5


def _sigmoid_exp(x):
    return 1.0 / (1.0 + jnp.exp(-x))


def _position():
    x, y, c = lax.axis_index("x"), lax.axis_index("y"), lax.axis_index("c")
    return x, y, c


def _small_allgather(v, name):
    rows, cols = v.shape

    def body(v_ref, out_ref, send_sems, recv_sems):
        x, y, c = _position()
        me = 4 * x + 2 * y + c
        out_ref[me] = v_ref[...]

        def peer(k):
            return (1 - x if k & 4 else x, 1 - y if k & 2 else y, 1 - c if k & 1 else c)

        def copy(k, slot):
            return pltpu.make_async_remote_copy(
                src_ref=v_ref, dst_ref=out_ref.at[slot], send_sem=send_sems.at[k - 1], recv_sem=recv_sems.at[k - 1],
                device_id=peer(k), device_id_type=MESH)

        sends = [copy(k, me) for k in range(1, N_DEV)]
        for cp in sends:
            cp.start()
        for k in range(1, N_DEV):
            px, py, pc = peer(k)
            copy(k, 4 * px + 2 * py + pc).wait_recv()
        for cp in sends:
            cp.wait_send()

    return pl.pallas_call(
        body, name=name,
        out_shape=jax.ShapeDtypeStruct((N_DEV, rows, cols), v.dtype),
        in_specs=[VMEM_SPEC], out_specs=VMEM_SPEC,
        scratch_shapes=[pltpu.SemaphoreType.DMA((N_DEV - 1,)), pltpu.SemaphoreType.DMA((N_DEV - 1,))],
    )(v)


def _weight_allgather(p):
    rows, cols = p.shape

    def body(p_ref, out_ref, send_sems, recv_sems, local_sem):
        x, y, c = _position()
        me, sibling = (x, y, c), (x, y, 1 - c)
        chips = [(1 - x, y), (x, 1 - y), (1 - x, 1 - y)]

        def slot(px, py, pc):
            return out_ref.at[4 * px + 2 * py + pc]

        def copy(k, block, to, src=None):
            return pltpu.make_async_remote_copy(
                src_ref=slot(*block) if src is None else src, dst_ref=slot(*block),
                send_sem=send_sems.at[k], recv_sem=recv_sems.at[k], device_id=to, device_id_type=MESH)

        mine = pltpu.make_async_copy(p_ref, slot(*me), local_sem)
        mine.start()
        first = [copy(0, me, sibling, src=p_ref)]
        first += [copy(1 + j, me, (*chip, c), src=p_ref) for j, chip in enumerate(chips)]
        for cp in first:
            cp.start()
        passed = [copy(4 + j, (*chip, c), sibling) for j, chip in enumerate(chips)]
        for j, chip in enumerate(chips):
            copy(1 + j, (*chip, c), me).wait_recv()
            passed[j].start()
        copy(0, sibling, me).wait_recv()
        for j, chip in enumerate(chips):
            copy(4 + j, (*chip, 1 - c), me).wait_recv()
        for cp in first + passed:
            cp.wait_send()
        mine.wait()

    return pl.pallas_call(
        body, name="weight_allgather",
        out_shape=jax.ShapeDtypeStruct((N_DEV, rows, cols), p.dtype),
        in_specs=[ANY], out_specs=ANY,
        scratch_shapes=[pltpu.SemaphoreType.DMA((7,)), pltpu.SemaphoreType.DMA((7,)), pltpu.SemaphoreType.DMA],
    )(p)


def _grad_exchange_sibling(g):
    _, rows, cols = g.shape

    def body(g_ref, out_ref, send_sems, recv_sems):
        x, y, c = _position()
        sibling = (x, y, 1 - c)

        def copy(q):
            px, py = q >> 1, q & 1
            return pltpu.make_async_remote_copy(
                src_ref=g_ref.at[4 * px + 2 * py + (1 - c)], dst_ref=out_ref.at[q],
                send_sem=send_sems.at[q], recv_sem=recv_sems.at[q], device_id=sibling, device_id_type=MESH)

        copies = [copy(q) for q in range(4)]
        for cp in copies:
            cp.start()
        for cp in copies:
            cp.wait_recv()
        for cp in copies:
            cp.wait_send()

    return pl.pallas_call(
        body, name="grad_exchange_sibling",
        out_shape=jax.ShapeDtypeStruct((4, rows, cols), g.dtype),
        in_specs=[ANY], out_specs=ANY,
        scratch_shapes=[pltpu.SemaphoreType.DMA((4,)), pltpu.SemaphoreType.DMA((4,))],
    )(g)


class _Rider:
    def __init__(self, operands, out_shapes, n_send, n_recv, start, finish, aliases=None):
        self.operands, self.out_shapes = list(operands), list(out_shapes)
        self.n_send, self.n_recv, self.start, self.finish = n_send, n_recv, start, finish
        self.aliases = aliases or {}


def _pcall(body, *, name, grid, in_specs, operands, out_shape, out_specs, scratch_shapes, params, rider=None):
    if rider is None:
        return pl.pallas_call(body, name=name, out_shape=out_shape, grid=grid, in_specs=in_specs, out_specs=out_specs,
                              scratch_shapes=scratch_shapes, compiler_params=params)(*operands)
    n_in, n_out, n_sc = len(operands), len(out_shape), len(scratch_shapes)
    r_in, r_out = len(rider.operands), len(rider.out_shapes)

    def wrapped(*refs):
        ins, rins = refs[:n_in], refs[n_in:n_in + r_in]
        outs, routs = refs[n_in + r_in:n_in + r_in + n_out], refs[n_in + r_in + n_out:n_in + r_in + n_out + r_out]
        rest = refs[n_in + r_in + n_out + r_out:]
        scratch, sems = rest[:n_sc], rest[n_sc:]
        ids = [pl.program_id(a) for a in range(len(grid))]
        first, last = ids[0] == 0, ids[0] == grid[0] - 1
        for a in range(1, len(grid)):
            first, last = first & (ids[a] == 0), last & (ids[a] == grid[a] - 1)

        @pl.when(first)
        def _():
            rider.start(rins, routs, *sems)

        body(*ins, *outs, *scratch)

        @pl.when(last)
        def _():
            rider.finish(rins, routs, *sems)

    return pl.pallas_call(
        wrapped, name=name, out_shape=list(out_shape) + rider.out_shapes, grid=grid,
        in_specs=list(in_specs) + [ANY] * r_in, out_specs=list(out_specs) + [ANY] * r_out,
        scratch_shapes=list(scratch_shapes) + [pltpu.SemaphoreType.DMA((rider.n_send,)), pltpu.SemaphoreType.DMA((rider.n_recv,))],
        input_output_aliases={n_in + i: n_out + o for i, o in rider.aliases.items()},
        compiler_params=params,
    )(*operands, *rider.operands)


def _flips(x, y, c):
    return [(x, y, 1 - c), (1 - x, y, c), (x, 1 - y, c), (1 - x, 1 - y, c)]


def _gather_direct_rider(p):
    rows, cols = p.shape

    def copies(p_ref, land, send_sems, recv_sems):
        x, y, c = _position()
        me = 4 * x + 2 * y + c
        peers = _flips(x, y, c)
        sends = [pltpu.make_async_remote_copy(src_ref=p_ref, dst_ref=land.at[me], send_sem=send_sems.at[k], recv_sem=recv_sems.at[k],
                                              device_id=to, device_id_type=MESH) for k, to in enumerate(peers)]
        recvs = [pltpu.make_async_remote_copy(src_ref=p_ref, dst_ref=land.at[4 * px + 2 * py + pc], send_sem=send_sems.at[k],
                                              recv_sem=recv_sems.at[k], device_id=(px, py, pc), device_id_type=MESH)
                 for k, (px, py, pc) in enumerate(peers)]
        mine = pltpu.make_async_copy(p_ref, land.at[me], send_sems.at[len(peers)])
        return sends, recvs, mine

    def start(rins, routs, send_sems, recv_sems):
        sends, _, mine = copies(rins[0], routs[0], send_sems, recv_sems)
        mine.start()
        for cp in sends:
            cp.start()

    def finish(rins, routs, send_sems, recv_sems):
        sends, recvs, mine = copies(rins[0], routs[0], send_sems, recv_sems)
        for cp in recvs:
            cp.wait_recv()
        for cp in sends:
            cp.wait_send()
        mine.wait()

    return _Rider([p], [jax.ShapeDtypeStruct((N_DEV, rows, cols), p.dtype)], 5, 4, start, finish)


def _gather_forward_rider(land):
    def copies(buf, send_sems, recv_sems):
        x, y, c = _position()
        chips = [(1 - x, y), (x, 1 - y), (1 - x, 1 - y)]
        sends = [pltpu.make_async_remote_copy(src_ref=buf.at[4 * px + 2 * py + c], dst_ref=buf.at[4 * px + 2 * py + c],
                                              send_sem=send_sems.at[k], recv_sem=recv_sems.at[k], device_id=(x, y, 1 - c),
                                              device_id_type=MESH) for k, (px, py) in enumerate(chips)]
        recvs = [pltpu.make_async_remote_copy(src_ref=buf.at[4 * px + 2 * py + 1 - c], dst_ref=buf.at[4 * px + 2 * py + 1 - c],
                                              send_sem=send_sems.at[k], recv_sem=recv_sems.at[k], device_id=(x, y, 1 - c),
                                              device_id_type=MESH) for k, (px, py) in enumerate(chips)]
        return sends, recvs

    def start(rins, routs, send_sems, recv_sems):
        for cp in copies(routs[0], send_sems, recv_sems)[0]:
            cp.start()

    def finish(rins, routs, send_sems, recv_sems):
        sends, recvs = copies(routs[0], send_sems, recv_sems)
        for cp in recvs:
            cp.wait_recv()
        for cp in sends:
            cp.wait_send()

    return _Rider([land], [jax.ShapeDtypeStruct(land.shape, land.dtype)], 3, 3, start, finish, aliases={0: 0})


def _chip_exchange_rider(s):
    def copies(s_ref, out_ref, send_sems, recv_sems):
        x, y, c = _position()
        chips = [(1 - x, y), (x, 1 - y), (1 - x, 1 - y)]
        return [pltpu.make_async_remote_copy(src_ref=s_ref.at[k], dst_ref=out_ref.at[k], send_sem=send_sems.at[k],
                                             recv_sem=recv_sems.at[k], device_id=(*chips[k], c), device_id_type=MESH)
                for k in range(3)]

    def start(rins, routs, send_sems, recv_sems):
        for cp in copies(rins[0], routs[0], send_sems, recv_sems):
            cp.start()

    def finish(rins, routs, send_sems, recv_sems):
        cps = copies(rins[0], routs[0], send_sems, recv_sems)
        for cp in cps:
            cp.wait_recv()
        for cp in cps:
            cp.wait_send()

    return _Rider([s], [jax.ShapeDtypeStruct(s.shape, s.dtype)], 3, 3, start, finish)


def _chip_partial_sums(g, recv_sib, jj, qq):
    _, rows, cols = g.shape
    tr = _div(rows, 512, ROW_ALIGN)

    def body(jj_ref, qq_ref, g_ref, r_ref, o_ref):
        o_ref[...] = (g_ref[...] + r_ref[...]).astype(o_ref.dtype)

    return pl.pallas_call(
        body, name="chip_partial_sums",
        out_shape=jax.ShapeDtypeStruct((3, rows, cols), BF16),
        grid_spec=pltpu.PrefetchScalarGridSpec(
            num_scalar_prefetch=2, grid=(3, rows // tr),
            in_specs=[pl.BlockSpec((None, tr, cols), lambda k, i, jj, qq: (jj[k], i, 0)),
                      pl.BlockSpec((None, tr, cols), lambda k, i, jj, qq: (qq[k], i, 0))],
            out_specs=pl.BlockSpec((None, tr, cols), lambda k, i, jj, qq: (k, i, 0))),
        compiler_params=_params(("arbitrary", "arbitrary")),
    )(jj, qq, g, recv_sib)


def _own_partial_sum(g, recv_sib, jj, qq):
    _, rows, cols = g.shape
    tr = _div(rows, 512, ROW_ALIGN)

    def body(jj_ref, qq_ref, g_ref, r_ref, o_ref):
        o_ref[...] = g_ref[...] + r_ref[...]

    return pl.pallas_call(
        body, name="own_partial_sum",
        out_shape=jax.ShapeDtypeStruct((rows, cols), F32),
        grid_spec=pltpu.PrefetchScalarGridSpec(
            num_scalar_prefetch=2, grid=(rows // tr,),
            in_specs=[pl.BlockSpec((None, tr, cols), lambda i, jj, qq: (jj[0], i, 0)),
                      pl.BlockSpec((None, tr, cols), lambda i, jj, qq: (qq[0], i, 0))],
            out_specs=pl.BlockSpec((tr, cols), lambda i, jj, qq: (i, 0))),
        compiler_params=_params(("arbitrary",)),
    )(jj, qq, g, recv_sib)


def _final_grad_sum(own, recv):
    rows, cols = own.shape
    tr = _div(rows, 512, ROW_ALIGN)

    def body(o_ref, r_ref, out_ref):
        out_ref[...] = ((o_ref[...] + r_ref[0].astype(F32)) + r_ref[1].astype(F32)) + r_ref[2].astype(F32)

    return pl.pallas_call(
        body, name="final_grad_sum",
        out_shape=jax.ShapeDtypeStruct((rows, cols), F32),
        grid=(rows // tr,),
        in_specs=[pl.BlockSpec((tr, cols), lambda i: (i, 0)), pl.BlockSpec((3, tr, cols), lambda i: (0, i, 0))],
        out_specs=pl.BlockSpec((tr, cols), lambda i: (i, 0)),
        compiler_params=_params(("arbitrary",)),
    )(own, recv)


def _sibling_exchange_rider(g):
    _, rows, cols = g.shape

    def copies(g_ref, out_ref, send_sems, recv_sems):
        x, y, c = _position()
        return [pltpu.make_async_remote_copy(
            src_ref=g_ref.at[4 * (q >> 1) + 2 * (q & 1) + (1 - c)], dst_ref=out_ref.at[q], send_sem=send_sems.at[q],
            recv_sem=recv_sems.at[q], device_id=(x, y, 1 - c), device_id_type=MESH) for q in range(4)]

    def start(rins, routs, send_sems, recv_sems):
        for cp in copies(rins[0], routs[0], send_sems, recv_sems):
            cp.start()

    def finish(rins, routs, send_sems, recv_sems):
        cps = copies(rins[0], routs[0], send_sems, recv_sems)
        for cp in cps:
            cp.wait_recv()
        for cp in cps:
            cp.wait_send()

    return _Rider([g], [jax.ShapeDtypeStruct((4, rows, cols), g.dtype)], 4, 4, start, finish)


def _chip_sums(g, recv_sib=None):
    x, y, c = _position()
    chips = [(1 - x, y), (x, 1 - y), (1 - x, 1 - y)]
    jj = jnp.stack([4 * px + 2 * py + c for px, py in chips]).astype(jnp.int32)
    qq = jnp.stack([2 * px + py for px, py in chips]).astype(jnp.int32)
    jme = jnp.reshape(4 * x + 2 * y + c, (1,)).astype(jnp.int32)
    qme = jnp.reshape(2 * x + y, (1,)).astype(jnp.int32)
    if recv_sib is None:
        recv_sib = _grad_exchange_sibling(g)
    return _chip_partial_sums(g, recv_sib, jj, qq), _own_partial_sum(g, recv_sib, jme, qme)


def _matmul(name, form, prods, M, N, K, tm, tn, tk, out_dtypes, extras=(), epilogue=None, rows_per_example=None, rider=None):
    nk = K // tk
    n_acc = len(prods)
    flat = [ab for group in prods for ab in group]
    dims = {"nn": (((1,), (0,)), ((), ())), "nt": (((1,), (1,)), ((), ())), "tn": (((0,), (0,)), ((), ()))}[form]
    direct = nk > 1 and epilogue is None and n_acc == 1 and list(out_dtypes) == [F32]

    def spec(shape, index_map, whole):
        if whole:
            return pl.BlockSpec(shape, index_map, pipeline_mode=pl.Buffered(1))
        return pl.BlockSpec(shape, index_map)

    if form == "tn":
        a_spec = spec((tk, tm), lambda i, j, k: (k, i), nk == 1 and M == tm)
    else:
        a_spec = spec((tm, tk), lambda i, j, k: (i, k), nk == 1 and M == tm)
    if form == "nt":
        b_spec = spec((tn, tk), lambda i, j, k: (j, k), nk == 1 and N == tn)
    else:
        b_spec = spec((tk, tn), lambda i, j, k: (k, j), nk == 1 and N == tn)
    in_specs, operands = [], []
    for a, b in flat:
        in_specs += [a_spec, _weight_spec(b) if isinstance(b, _Slab) else b_spec]
        operands += [a, _weight_operand(b)]
    for arr, kind, off in extras:
        if kind == "tile":
            assert off % tn == 0
            in_specs.append(pl.BlockSpec((tm, tn), functools.partial(lambda i, j, k, o: (i, j + o), o=off // tn)))
        else:
            tiles = rows_per_example // tm
            in_specs.append(pl.BlockSpec((None, 1, tn), functools.partial(lambda i, j, k, t: (i // t, 0, j), t=tiles)))
        operands.append(arr)
    n_in, n_out = len(operands), len(out_dtypes)

    def body(*refs):
        in_refs, out_refs, acc_refs = refs[:n_in], refs[n_in:n_in + n_out], refs[n_in + n_out:]
        k = pl.program_id(2)
        partials, p = [], 0
        for group in prods:
            tot = None
            for _ in group:
                d = lax.dot_general(in_refs[2 * p][...], _weight_value(in_refs[2 * p + 1], flat[p][1]), dims,
                                    preferred_element_type=F32)
                tot = d if tot is None else tot + d
                p += 1
            partials.append(tot)

        def finish(accs):
            ex = [r[...] for r in in_refs[2 * len(flat):]]
            outs = epilogue(accs, ex) if epilogue is not None else accs
            for r, o in zip(out_refs, outs):
                r[...] = o.astype(r.dtype)

        if nk == 1:
            finish(partials)
        elif direct:
            @pl.when(k == 0)
            def _():
                out_refs[0][...] = partials[0]

            @pl.when(k > 0)
            def _():
                out_refs[0][...] += partials[0]
        else:
            @pl.when(k == 0)
            def _():
                for r, v in zip(acc_refs, partials):
                    r[...] = v

            @pl.when(k > 0)
            def _():
                for r, v in zip(acc_refs, partials):
                    r[...] += v

            @pl.when(k == nk - 1)
            def _():
                finish([r[...] for r in acc_refs])

    return _pcall(
        body, name=name,
        out_shape=[jax.ShapeDtypeStruct((M, N), dt) for dt in out_dtypes],
        grid=(M // tm, N // tn, nk),
        in_specs=in_specs, operands=operands,
        out_specs=[pl.BlockSpec((tm, tn), lambda i, j, k: (i, j)) for _ in out_dtypes],
        scratch_shapes=[pltpu.VMEM((tm, tn), F32) for _ in range(n_acc)] if nk > 1 and not direct else [],
        params=_params(("parallel", "parallel", "arbitrary")), rider=rider)


def _rowwise(name, fn, T, tm, ins, outs, rows_per_example):
    tiles = rows_per_example // tm
    n_ex = T // rows_per_example
    in_specs, operands = [], []
    for arr, kind, arg in ins:
        if kind == "row":
            if arg is None:
                in_specs.append(pl.BlockSpec((tm, arr.shape[1]), lambda i: (i, 0)))
            else:
                in_specs.append(pl.BlockSpec((tm, arg[0]), functools.partial(lambda i, cb: (i, cb), cb=arg[1])))
        elif kind == "bvec":
            in_specs.append(pl.BlockSpec((None, 1, arr.shape[2]), lambda i: (i // tiles, 0, 0)))
        else:
            in_specs.append(pl.BlockSpec((1, arr.shape[1]), lambda i: (0, 0)))
        operands.append(arr)
    out_shape, out_specs = [], []
    for kind, cols, dt in outs:
        if kind == "row":
            out_shape.append(jax.ShapeDtypeStruct((T, cols), dt))
            out_specs.append(pl.BlockSpec((tm, cols), lambda i: (i, 0)))
        else:
            out_shape.append(jax.ShapeDtypeStruct((n_ex, 1, cols), F32))
            out_specs.append(pl.BlockSpec((None, 1, cols), lambda i: (i // tiles, 0, 0)))
    n_in = len(operands)

    def body(*refs):
        i = pl.program_id(0)
        vals = fn(*[r[...] for r in refs[:n_in]])
        for (kind, _, _), r, v in zip(outs, refs[n_in:], vals):
            if kind == "row":
                r[...] = v.astype(r.dtype)
            else:
                @pl.when(i % tiles == 0)
                def _():
                    r[...] = jnp.zeros_like(r)

                r[...] += v

    return pl.pallas_call(
        body, name=name, out_shape=out_shape, grid=(T // tm,), in_specs=in_specs, out_specs=out_specs,
        compiler_params=_params(("arbitrary",)),
    )(*operands)


def _colsum(v):
    return jnp.sum(v, axis=0, keepdims=True)


def _rms_parts(x):
    rstd = lax.rsqrt(jnp.mean(x * x, axis=-1, keepdims=True) + RMS_EPS)
    return x * rstd, rstd


def _resident(shape):
    return pl.BlockSpec(shape, lambda i: (0, 0), pipeline_mode=pl.Buffered(1))


class _Slab:
    def __init__(self, land, off, rows):
        assert off % rows == 0 and rows % ROW_ALIGN == 0
        self.land, self.off, self.rows = land, off, rows
        self.shape = (N_DEV * rows, land.shape[2])


def _weight_spec(w):
    if isinstance(w, _Slab):
        return pl.BlockSpec((N_DEV, w.rows, w.shape[1]), lambda *_: (0, w.off // w.rows, 0), pipeline_mode=pl.Buffered(1))
    return pl.BlockSpec(w.shape, lambda *_: (0, 0), pipeline_mode=pl.Buffered(1))


def _weight_operand(w):
    return w.land if isinstance(w, _Slab) else w


def _weight_value(ref, w):
    return ref[...].reshape(w.shape) if isinstance(w, _Slab) else ref[...]


def _example_acc(r, i, tiles, v):
    @pl.when(i % tiles == 0)
    def _():
        r[...] = jnp.zeros_like(r)

    r[...] += v


def _norm_matmul(name, x, g, sc, sh, weights, epilogue, outs, S, vecs=(), rider=None):
    T, D = x.shape
    tm = _div(S, 256, 8)
    tiles = S // tm
    nw, nv = len(weights), len(vecs)

    def body(*refs):
        x_ref, g_ref, sc_ref, sh_ref = refs[:4]
        w_refs, v_refs = refs[4:4 + nw], refs[4 + nw:4 + nw + nv]
        h_ref, out_refs = refs[4 + nw + nv], refs[5 + nw + nv:]
        xhat, _ = _rms_parts(x_ref[...])
        h = ((xhat * g_ref[...]) * (1.0 + sc_ref[...]) + sh_ref[...]).astype(BF16)
        h_ref[...] = h
        accs = [lax.dot_general(h, _weight_value(r, w), NT, preferred_element_type=F32) for r, w in zip(w_refs, weights)]
        for r, o in zip(out_refs, epilogue(accs, *[v[...] for v in v_refs])):
            r[...] = o.astype(r.dtype)

    bvec = pl.BlockSpec((None, 1, D), lambda i: (i // tiles, 0, 0))
    return _pcall(
        body, name=name,
        out_shape=[jax.ShapeDtypeStruct((T, D), BF16)] + [jax.ShapeDtypeStruct((T, w), dt) for w, dt in outs],
        grid=(T // tm,),
        in_specs=[pl.BlockSpec((tm, D), lambda i: (i, 0)), pl.BlockSpec((1, D), lambda i: (0, 0)), bvec, bvec]
        + [_weight_spec(w) for w in weights] + [pl.BlockSpec(v.shape, lambda i: (0, 0)) for v in vecs],
        operands=[x, g, sc, sh, *[_weight_operand(w) for w in weights], *vecs],
        out_specs=[pl.BlockSpec((tm, D), lambda i: (i, 0))] + [pl.BlockSpec((tm, w), lambda i: (i, 0)) for w, _ in outs],
        scratch_shapes=[], params=_params(("arbitrary",)), rider=rider)


def _gated_grad_matmul(name, dx, y, gt, coeff, w, tiles_in, epilogue, outs, S, rider=None):
    T, D = dx.shape
    N = w.shape[0]
    tm = _div(S, 256, 8)
    tiles = S // tm
    nt = len(tiles_in)

    def body(*refs):
        dx_ref, y_ref, gt_ref, w_ref = refs[:4]
        t_refs, dy_ref, dgt_ref, out_refs = refs[4:4 + nt], refs[4 + nt], refs[5 + nt], refs[6 + nt:]
        i = pl.program_id(0)
        dxv = dx_ref[...]
        dy = (coeff * gt_ref[...] * dxv).astype(BF16)
        dy_ref[...] = dy
        _example_acc(dgt_ref, i, tiles, _colsum(coeff * dxv * y_ref[...].astype(F32)))
        acc = lax.dot_general(dy, _weight_value(w_ref, w), NT, preferred_element_type=F32)
        for r, o in zip(out_refs, epilogue(acc, [t[...] for t in t_refs])):
            r[...] = o.astype(r.dtype)

    row = pl.BlockSpec((tm, D), lambda i: (i, 0))
    bvec = pl.BlockSpec((None, 1, D), lambda i: (i // tiles, 0, 0))
    return _pcall(
        body, name=name,
        out_shape=[jax.ShapeDtypeStruct((T, D), BF16), jax.ShapeDtypeStruct((T // S, 1, D), F32)]
        + [jax.ShapeDtypeStruct((T, N), dt) for dt in outs],
        grid=(T // tm,),
        in_specs=[row, row, bvec, _weight_spec(w)]
        + [pl.BlockSpec((tm, N), functools.partial(lambda i, cb: (i, cb), cb=cb)) for _, cb in tiles_in],
        operands=[dx, y, gt, _weight_operand(w), *[t for t, _ in tiles_in]],
        out_specs=[row, bvec] + [pl.BlockSpec((tm, N), lambda i: (i, 0)) for _ in outs],
        scratch_shapes=[], params=_params(("arbitrary",)), rider=rider)


def _matmul_normmod_bwd(name, prods, x, g, sc, dres, S, rider=None):
    T, D = x.shape
    tm = _div(S, 256, 8)
    tiles = S // tm
    npr = len(prods)

    def body(*refs):
        ab = refs[:2 * npr]
        x_ref, g_ref, sc_ref, dr_ref = refs[2 * npr:2 * npr + 4]
        dx_ref, dsh_ref, dsc_ref, dg_ref = refs[2 * npr + 4:]
        i = pl.program_id(0)
        dh = None
        for p in range(npr):
            d = lax.dot_general(ab[2 * p][...], _weight_value(ab[2 * p + 1], prods[p][1]), NN, preferred_element_type=F32)
            dh = d if dh is None else dh + d
        xhat, rstd = _rms_parts(x_ref[...])
        gv = g_ref[...]
        dn = dh * (1.0 + sc_ref[...])
        dxh = dn * gv
        dx_ref[...] = dr_ref[...] + rstd * (dxh - xhat * jnp.mean(dxh * xhat, axis=-1, keepdims=True))
        _example_acc(dsh_ref, i, tiles, _colsum(dh))
        _example_acc(dsc_ref, i, tiles, _colsum(dh * (xhat * gv)))
        _example_acc(dg_ref, i, tiles, _colsum(dn * xhat))

    row = pl.BlockSpec((tm, D), lambda i: (i, 0))
    bvec = pl.BlockSpec((None, 1, D), lambda i: (i // tiles, 0, 0))
    in_specs, operands = [], []
    for a, b in prods:
        in_specs += [pl.BlockSpec((tm, a.shape[1]), lambda i: (i, 0)), _weight_spec(b)]
        operands += [a, _weight_operand(b)]
    acc_shape = jax.ShapeDtypeStruct((T // S, 1, D), F32)
    return _pcall(
        body, name=name,
        out_shape=[jax.ShapeDtypeStruct((T, D), F32), acc_shape, acc_shape, acc_shape],
        grid=(T // tm,),
        in_specs=in_specs + [row, pl.BlockSpec((1, D), lambda i: (0, 0)), bvec, row],
        operands=[*operands, x, g, sc, dres],
        out_specs=[row, bvec, bvec, bvec],
        scratch_shapes=[], params=_params(("arbitrary",)), rider=rider)


def _ffn_forward(tag, x, g, sh, sc, gt, wgT, wuT, wd, S, gather=None):
    T, D = x.shape
    F = wd.shape[0]

    def gateup(accs):
        a, u = accs
        return [a, u, a * _sigmoid(a) * u]

    h, a, u, s, *land = _norm_matmul(f"{tag}_gateup", x, g, sc, sh, [wgT, wuT], gateup, [(F, BF16)] * 3, S,
                                     rider=None if gather is None else _gather_direct_rider(gather))

    def down(accs, ex):
        xv, gtv = ex
        return [xv + 0.5 * gtv * accs[0], accs[0]]

    tmd = _div(S, 512, 8)
    x_new, y, *land = _matmul(f"{tag}_down", "nn", [[(s, wd)]], T, D, F, tmd, D, F, [F32, BF16],
                              extras=[(x, "tile", 0), (gt, "brow", 0)], epilogue=down, rows_per_example=S,
                              rider=None if gather is None else _gather_forward_rider(land[0]))
    return x_new, (x, h, a, u, s, y), (land[0] if land else None)


def _ffn_backward(tag, dx_out, saved, g, sc, gt, wgT, wuT, wd, S, rider=None, dh_rider=None):
    x, h, a, u, s, y = saved
    T, D = x.shape
    F = wd.shape[0]

    def act_grad(ds, ex):
        av, uv = ex[0].astype(F32), ex[1].astype(F32)
        sg = _sigmoid(av)
        return [ds * uv * (sg * (1.0 + av * (1.0 - sg))), ds * (av * sg)]

    dy, dgt, da, du, *rode = _gated_grad_matmul(f"{tag}_act_grad", dx_out, y, gt, 0.5, wd, [(a, 0), (u, 0)], act_grad,
                                                [BF16, BF16], S, rider=rider)
    tkw = _div(T, 1024, LANES)
    dwd = _matmul(f"{tag}_dw_down", "tn", [[(s, dy)]], F, D, T, F, D, tkw, [F32])[0]
    dwgT = _matmul(f"{tag}_dw_gate", "tn", [[(da, h)]], F, D, T, F, D, tkw, [F32])[0]
    dwuT = _matmul(f"{tag}_dw_up", "tn", [[(du, h)]], F, D, T, F, D, tkw, [F32])[0]
    dx_in, dsh, dsc, dg, *rode_dh = _matmul_normmod_bwd(
        f"{tag}_dh", [(da, wgT), (du, wuT)], x, g, sc, dx_out, S,
        rider=None if dh_rider is None else dh_rider(dwgT, dwuT, dwd))
    return dx_in, (dsh, dsc, dgt, dg), (dwgT, dwuT, dwd), rode + rode_dh


def _loss_head(x, tgt, g, S):
    T, D = x.shape

    def fn(xv, tv, gv):
        xhat, rstd = _rms_parts(xv)
        e = xhat * gv - tv
        loss = jnp.broadcast_to(0.5 / D * jnp.sum(_colsum(e * e), axis=1, keepdims=True), (1, LANES))
        dy = e * (1.0 / D)
        dxh = dy * gv
        dx = rstd * (dxh - xhat * jnp.mean(dxh * xhat, axis=-1, keepdims=True))
        return [dx, loss, _colsum(dy * xhat)]

    return _rowwise("loss_head", fn, T, _div(S, 512, 8), [(x, "row", None), (tgt, "row", None), (g, "vec", None)],
                    [("row", D, F32), ("bacc", LANES, F32), ("bacc", D, F32)], S)


def _cumsum(v):
    B, S, _ = v.shape
    rows = _div(S, 1024, BLOCK)

    def body(x_ref, o_ref, carry):
        i = pl.program_id(1)

        @pl.when(i == 0)
        def _():
            carry[...] = jnp.zeros_like(carry)

        r = lax.broadcasted_iota(jnp.int32, (BLOCK, BLOCK), 0)
        c = lax.broadcasted_iota(jnp.int32, (BLOCK, BLOCK), 1)
        tri = (c <= r).astype(F32)
        last = carry[0:1, :]
        for j in range(0, rows, BLOCK):
            cum = jnp.dot(tri, x_ref[j:j + BLOCK, :], precision=lax.Precision.HIGHEST, preferred_element_type=F32) + last
            o_ref[j:j + BLOCK, :] = cum
            last = cum[BLOCK - 1:BLOCK, :]
        carry[...] = jnp.broadcast_to(last, carry.shape)

    return pl.pallas_call(
        body, name="cumsum", out_shape=jax.ShapeDtypeStruct(v.shape, F32), grid=(B, S // rows),
        in_specs=[pl.BlockSpec((None, rows, LANES), lambda b, i: (b, i, 0))],
        out_specs=pl.BlockSpec((None, rows, LANES), lambda b, i: (b, i, 0)),
        scratch_shapes=[pltpu.VMEM((8, LANES), F32)],
        compiler_params=_params(("arbitrary", "arbitrary")),
    )(v)


def _with_ones(x):
    lane = lax.broadcasted_iota(jnp.int32, (x.shape[0], HEAD_DIM), 1)
    return jnp.concatenate([x, jnp.where(lane == 0, 1.0, 0.0).astype(x.dtype)], axis=1)


def _causal_strip(s, r):
    qpos = r + lax.broadcasted_iota(jnp.int32, s.shape, 0)
    kpos = lax.broadcasted_iota(jnp.int32, s.shape, 1)
    return jnp.where(kpos <= qpos, s, NEG_INF)


NT = (((1,), (1,)), ((), ()))
NN = (((1,), (0,)), ((), ()))
TN = (((0,), (0,)), ((), ()))


def _fox_fwd(pm3, cumT, qcol, kcol, vcol, tq, rider=None):
    B, S, _ = pm3.shape
    nq = S // tq
    strips = range(0, tq, FOX_STRIP)

    def body(q_ref, k_ref, v_ref, ck_ref, o_ref, o32_ref, lse_ref, s_sc, p_sc, al_sc, m_sc, acc_sc):
        qi, kj = pl.program_id(1), pl.program_id(2)

        @pl.when(kj == 0)
        def _():
            m_sc[...] = jnp.full_like(m_sc, NEG_INF)
            acc_sc[...] = jnp.zeros_like(acc_sc)

        def tile(diagonal):
            def scores(h):
                hs = slice(HEAD_DIM * h, HEAD_DIM * (h + 1))
                s_sc[h % 2] = lax.dot_general(q_ref[:, hs] * SCALE, k_ref[:, hs], NT, preferred_element_type=F32)

            def accumulate(h):
                hs = slice(HEAD_DIM * h, HEAD_DIM * (h + 1))
                acc_sc[h] = al_sc[h % 2] * acc_sc[h] + lax.dot_general(p_sc[h % 2], _with_ones(v_ref[:, hs]), NN,
                                                                       preferred_element_type=F32)

            scores(0)
            for h in range(FOX_HEADS):
                b = h % 2
                if h + 1 < FOX_HEADS:
                    scores(h + 1)
                if h >= 1:
                    accumulate(h - 1)
                ck = ck_ref[h:h + 1, :]
                for r in strips:
                    rows = slice(r, r + FOX_STRIP)
                    s = s_sc[b, rows, :] - ck
                    if diagonal:
                        s = _causal_strip(s, r)
                    m_prev = m_sc[h, rows, :]
                    m_new = jnp.maximum(m_prev, jnp.max(s, axis=-1, keepdims=True))
                    p_sc[b, rows, :] = jnp.exp(s - m_new).astype(BF16)
                    al_sc[b, rows, :] = jnp.exp(m_prev - m_new)
                    m_sc[h, rows, :] = m_new
            accumulate(FOX_HEADS - 1)

        @pl.when(kj < qi)
        def _():
            tile(False)

        @pl.when(kj == qi)
        def _():
            tile(True)

        @pl.when(kj == nq - 1)
        def _():
            lse_ref[...] = jnp.zeros_like(lse_ref)
            for h in range(FOX_HEADS):
                hs = slice(HEAD_DIM * h, HEAD_DIM * (h + 1))
                acc = acc_sc[h]
                l = acc[:, HEAD_DIM:HEAD_DIM + 1]
                oh = acc[:, :HEAD_DIM] / l
                o_ref[:, hs] = oh.astype(o_ref.dtype)
                o32_ref[:, hs] = oh
                lse_ref[:, h:h + 1] = m_sc[h] + jnp.log(l)

    ospec = pl.BlockSpec((None, tq, FOX_W), lambda b, i, j: (b, i, 0))
    return _pcall(
        body, name="fox_forward",
        out_shape=[jax.ShapeDtypeStruct((B, S, FOX_W), BF16), jax.ShapeDtypeStruct((B, S, FOX_W), F32),
                   jax.ShapeDtypeStruct((B, S, LANES), F32)],
        grid=(B, nq, nq),
        in_specs=[pl.BlockSpec((None, tq, FOX_W), lambda b, i, j: (b, i, qcol)),
                  pl.BlockSpec((None, tq, FOX_W), lambda b, i, j: (b, jnp.minimum(i, j), kcol)),
                  pl.BlockSpec((None, tq, FOX_W), lambda b, i, j: (b, jnp.minimum(i, j), vcol)),
                  pl.BlockSpec((None, 8, tq), lambda b, i, j: (b, 0, jnp.minimum(i, j)))],
        operands=[pm3, pm3, pm3, cumT],
        out_specs=[ospec, ospec, pl.BlockSpec((None, tq, LANES), lambda b, i, j: (b, i, 0))],
        scratch_shapes=[pltpu.VMEM((2, tq, tq), F32), pltpu.VMEM((2, tq, tq), BF16), pltpu.VMEM((2, tq, 1), F32),
                        pltpu.VMEM((FOX_HEADS, tq, 1), F32), pltpu.VMEM((FOX_HEADS, tq, LANES), F32)],
        params=_params(("parallel", "parallel", "arbitrary")), rider=rider)


def _fox_bwd(pm3, do, delta, lse, cumT, qcol, kcol, vcol, tq, rider=None):
    B, S, _ = pm3.shape
    nq = S // tq
    strips = range(0, tq, FOX_STRIP)

    def body(q_ref, k_ref, v_ref, do_ref, dl_ref, lse_ref, ck_ref, dq_ref, rs_ref, dk_ref, dv_ref, cs_ref,
             s_sc, dp_sc, p_sc, ds_sc, dq_sc, dk_sc, dv_sc):
        kj, qi = pl.program_id(1), pl.program_id(2)

        @pl.when((kj == 0) & (qi == 0))
        def _():
            dq_sc[...] = jnp.zeros_like(dq_sc)

        @pl.when(qi == 0)
        def _():
            dk_sc[...] = jnp.zeros_like(dk_sc)
            dv_sc[...] = jnp.zeros_like(dv_sc)

        def tile(diagonal):
            qrows = pl.ds(pl.multiple_of(qi * tq, tq), tq)
            for h in range(FOX_HEADS):
                hs = slice(HEAD_DIM * h, HEAD_DIM * (h + 1))
                qh, kh, doh = q_ref[:, hs] * SCALE, k_ref[:, hs], do_ref[:, hs]
                b = h % 2
                s_sc[b] = lax.dot_general(qh, kh, NT, preferred_element_type=F32)
                dp_sc[b] = lax.dot_general(doh, v_ref[:, hs], NT, preferred_element_type=F32)
                ck = ck_ref[h:h + 1, :]
                for r in strips:
                    rows = slice(r, r + FOX_STRIP)
                    s = s_sc[b, rows, :] - ck
                    if diagonal:
                        s = _causal_strip(s, r)
                    p = jnp.exp(s - lse_ref[rows, h:h + 1])
                    p_sc[b, rows, :] = p.astype(BF16)
                    ds_sc[b, rows, :] = (p * (dp_sc[b, rows, :] - dl_ref[rows, h:h + 1])).astype(BF16)
                dv_sc[:, hs] += lax.dot_general(p_sc[b], doh, TN, preferred_element_type=F32)
                dk_sc[h] += lax.dot_general(ds_sc[b], _with_ones(qh), TN, preferred_element_type=F32)
                dq_sc[h, qrows, :] += lax.dot_general(ds_sc[b], _with_ones(kh), NN, preferred_element_type=F32)

        @pl.when(qi > kj)
        def _():
            tile(False)

        @pl.when(qi == kj)
        def _():
            tile(True)

        @pl.when(qi == nq - 1)
        def _():
            dv_ref[...] = dv_sc[...].astype(dv_ref.dtype)
            cs_ref[...] = jnp.zeros_like(cs_ref)
            for h in range(FOX_HEADS):
                hs = slice(HEAD_DIM * h, HEAD_DIM * (h + 1))
                dk = dk_sc[h]
                dk_ref[:, hs] = dk[:, :HEAD_DIM].astype(dk_ref.dtype)
                cs_ref[:, h:h + 1] = dk[:, HEAD_DIM:HEAD_DIM + 1]

        @pl.when((kj == nq - 1) & (qi == nq - 1))
        def _():
            rs_ref[...] = jnp.zeros_like(rs_ref)
            for h in range(FOX_HEADS):
                hs = slice(HEAD_DIM * h, HEAD_DIM * (h + 1))
                dq_ref[:, hs] = (dq_sc[h, :, :HEAD_DIM] * SCALE).astype(dq_ref.dtype)
                rs_ref[:, h:h + 1] = dq_sc[h, :, HEAD_DIM:HEAD_DIM + 1]

    def qside(width, col=0):
        return pl.BlockSpec((None, tq, width), lambda b, j, i: (b, jnp.maximum(i, j), col))

    kspec = pl.BlockSpec((None, tq, FOX_W), lambda b, j, i: (b, j, 0))
    return _pcall(
        body, name="fox_backward",
        out_shape=[jax.ShapeDtypeStruct((B, S, FOX_W), BF16), jax.ShapeDtypeStruct((B, S, LANES), F32),
                   jax.ShapeDtypeStruct((B, S, FOX_W), BF16), jax.ShapeDtypeStruct((B, S, FOX_W), BF16),
                   jax.ShapeDtypeStruct((B, S, LANES), F32)],
        grid=(B, nq, nq),
        in_specs=[qside(FOX_W, qcol),
                  pl.BlockSpec((None, tq, FOX_W), lambda b, j, i: (b, j, kcol)),
                  pl.BlockSpec((None, tq, FOX_W), lambda b, j, i: (b, j, vcol)),
                  qside(FOX_W), qside(LANES), qside(LANES),
                  pl.BlockSpec((None, 8, tq), lambda b, j, i: (b, 0, j))],
        operands=[pm3, pm3, pm3, do, delta, lse, cumT],
        out_specs=[pl.BlockSpec((None, S, FOX_W), lambda b, j, i: (b, 0, 0)),
                   pl.BlockSpec((None, S, LANES), lambda b, j, i: (b, 0, 0)),
                   kspec, kspec, pl.BlockSpec((None, tq, LANES), lambda b, j, i: (b, j, 0))],
        scratch_shapes=[pltpu.VMEM((2, tq, tq), F32), pltpu.VMEM((2, tq, tq), F32), pltpu.VMEM((2, tq, tq), BF16),
                        pltpu.VMEM((2, tq, tq), BF16), pltpu.VMEM((FOX_HEADS, S, LANES), F32),
                        pltpu.VMEM((FOX_HEADS, tq, LANES), F32), pltpu.VMEM((tq, FOX_W), F32)],
        params=_params(("parallel", "arbitrary", "arbitrary")), rider=rider)


def _fox_delta(do, o32, T, S):
    def fn(dov, ov):
        prod = dov.astype(F32) * ov
        lane = lax.broadcasted_iota(jnp.int32, (dov.shape[0], LANES), 1)
        delta = jnp.zeros((dov.shape[0], LANES), F32)
        for h in range(FOX_HEADS):
            hs = slice(HEAD_DIM * h, HEAD_DIM * (h + 1))
            delta = jnp.where(lane == h, jnp.sum(prod[:, hs], axis=-1, keepdims=True), delta)
        return [delta]

    return _rowwise("fox_delta", fn, T, _div(S, 512, 8), [(do, "row", None), (o32, "row", None)], [("row", LANES, F32)], S)[0]


def _alibi_slope(group, head):
    return 2.0 ** (-ALIBI_MAX_BIAS * (group * DIL_HPG + head + 1) / (N_DIL * DIL_HPG))


def _residue_order(a, B, S, d):
    C = a.shape[-1]
    if d == 1:
        return a.reshape(B, S, C)
    return a.reshape(B, S // d, d, C).transpose(0, 2, 1, 3).reshape(B * d, S // d, C)


def _token_order(a, B, S, d):
    C = a.shape[-1]
    if d == 1:
        return a.reshape(B * S, C)
    return a.reshape(B, d, S // d, C).transpose(0, 2, 1, 3).reshape(B * S, C)


def _band_scores(qh, kcat, slope_d, has_prev):
    qi = lax.broadcasted_iota(jnp.int32, (BLOCK, 2 * BLOCK), 0)
    c = lax.broadcasted_iota(jnp.int32, (BLOCK, 2 * BLOCK), 1)
    s = lax.dot_general(qh, kcat, NT, preferred_element_type=F32) - slope_d * (BLOCK + qi - c).astype(F32)
    valid = (c >= qi) & (c <= qi + BLOCK)
    if has_prev is not None:
        valid = valid & ((c >= BLOCK) | has_prev)
    return jnp.where(valid, s, NEG_INF)


def _band_operands(j, cur_ref, prev_ref, hs):
    if j == 0:
        return jnp.concatenate([prev_ref[:, hs], cur_ref[0:BLOCK, hs]], axis=0)
    return cur_ref[(j - 1) * BLOCK:(j + 1) * BLOCK, hs]


def _dil_specs(Ls, qb, cols):
    nsub = qb // BLOCK
    qcol, kcol, vcol = cols

    def cur(col):
        return pl.BlockSpec((None, qb, DIL_GW), lambda s, n: (s, n, col))

    def prev(col):
        return pl.BlockSpec((None, BLOCK, DIL_GW), lambda s, n: (s, jnp.maximum(n * nsub - 1, 0), col))

    return [cur(qcol), cur(kcol), prev(kcol), cur(vcol), prev(vcol)]


def _dil_fwd(group, src, cols):
    _, dilation = DIL_GROUPS[group]
    nseq, Ls, _ = src.shape
    qb = _div(Ls, 512, BLOCK)
    nsub = qb // BLOCK

    def body(q_ref, kc_ref, kp_ref, vc_ref, vp_ref, o_ref, lse_ref):
        has_prev = pl.program_id(1) > 0
        lse_ref[...] = jnp.zeros_like(lse_ref)
        for h in range(DIL_HPG):
            hs = slice(HEAD_DIM * h, HEAD_DIM * (h + 1))
            scores = [_band_scores(q_ref[j * BLOCK:(j + 1) * BLOCK, hs] * SCALE, _band_operands(j, kc_ref, kp_ref, hs),
                                   _alibi_slope(group, h) * dilation, has_prev if j == 0 else None) for j in range(nsub)]
            pending = None

            def write(j, m, acc):
                rows = slice(j * BLOCK, (j + 1) * BLOCK)
                l = acc[:, HEAD_DIM:HEAD_DIM + 1]
                o_ref[rows, hs] = acc[:, :HEAD_DIM] / l
                lse_ref[rows, h:h + 1] = m + jnp.log(l)

            for j in range(nsub):
                m = jnp.max(scores[j], axis=-1, keepdims=True)
                p = jnp.exp(scores[j] - m).astype(BF16)
                acc = lax.dot_general(p, _with_ones(_band_operands(j, vc_ref, vp_ref, hs)), NN, preferred_element_type=F32)
                if pending is not None:
                    write(*pending)
                pending = (j, m, acc)
            write(*pending)

    return pl.pallas_call(
        body, name=f"dil_forward_{group}",
        out_shape=[jax.ShapeDtypeStruct((nseq, Ls, DIL_GW), F32), jax.ShapeDtypeStruct((nseq, Ls, LANES), F32)],
        grid=(nseq, Ls // qb),
        in_specs=_dil_specs(Ls, qb, cols),
        out_specs=[pl.BlockSpec((None, qb, DIL_GW), lambda s, n: (s, n, 0)),
                   pl.BlockSpec((None, qb, LANES), lambda s, n: (s, n, 0))],
        compiler_params=_params(("parallel", "arbitrary")),
    )(src, src, src, src, src)


def _dil_bwd(group, src, cols, Lr, dyr, dlr):
    _, dilation = DIL_GROUPS[group]
    nseq, Ls, _ = src.shape
    qb = _div(Ls, 512, BLOCK)
    nsub, nb = qb // BLOCK, Ls // qb

    def body(q_ref, kc_ref, kp_ref, vc_ref, vp_ref, L_ref, dy_ref, dl_ref, dq_ref, dk_ref, dv_ref, dk_sc, dv_sc):
        n = pl.program_id(1)
        has_prev = n > 0

        @pl.when(n == 0)
        def _():
            dk_sc[...] = jnp.zeros_like(dk_sc)
            dv_sc[...] = jnp.zeros_like(dv_sc)

        base = pl.multiple_of(n * qb, BLOCK)
        for h in range(DIL_HPG):
            hs = slice(HEAD_DIM * h, HEAD_DIM * (h + 1))
            blocks = [slice(j * BLOCK, (j + 1) * BLOCK) for j in range(nsub)]
            qhs = [q_ref[rows, hs] * SCALE for rows in blocks]
            kcats = [_band_operands(j, kc_ref, kp_ref, hs) for j in range(nsub)]
            dyhs = [dy_ref[rows, hs] for rows in blocks]
            scores = [_band_scores(qhs[j], kcats[j], _alibi_slope(group, h) * dilation, has_prev if j == 0 else None)
                      for j in range(nsub)]
            dps = [lax.dot_general(dyhs[j], _band_operands(j, vc_ref, vp_ref, hs), NT, preferred_element_type=F32)
                   for j in range(nsub)]
            pending = None

            def write(j, dq, dk, dv):
                dq_ref[blocks[j], hs] = (dq * SCALE).astype(dq_ref.dtype)
                win = pl.ds(base + j * BLOCK, 2 * BLOCK)
                dk_sc[win, hs] += dk
                dv_sc[win, hs] += dv

            for j in range(nsub):
                p = jnp.exp(scores[j] - L_ref[blocks[j], h:h + 1])
                ds = (p * (dps[j] - dl_ref[blocks[j], h:h + 1])).astype(BF16)
                dq = lax.dot_general(ds, kcats[j], NN, preferred_element_type=F32)
                dk = lax.dot_general(ds, qhs[j], TN, preferred_element_type=F32)
                dv = lax.dot_general(p.astype(BF16), dyhs[j], TN, preferred_element_type=F32)
                if pending is not None:
                    write(*pending)
                pending = (j, dq, dk, dv)
            write(*pending)

        @pl.when(n == nb - 1)
        def _():
            dk_ref[...] = dk_sc[BLOCK:, :].astype(dk_ref.dtype)
            dv_ref[...] = dv_sc[BLOCK:, :].astype(dv_ref.dtype)

    own = pl.BlockSpec((None, qb, DIL_GW), lambda s, n: (s, n, 0))
    own128 = pl.BlockSpec((None, qb, LANES), lambda s, n: (s, n, 0))
    whole = pl.BlockSpec((None, Ls, DIL_GW), lambda s, n: (s, 0, 0))
    shape = jax.ShapeDtypeStruct((nseq, Ls, DIL_GW), BF16)
    return pl.pallas_call(
        body, name=f"dil_backward_{group}",
        out_shape=[shape, shape, shape],
        grid=(nseq, nb),
        in_specs=_dil_specs(Ls, qb, cols) + [own128, own, own128],
        out_specs=[own, whole, whole],
        scratch_shapes=[pltpu.VMEM((Ls + BLOCK, DIL_GW), F32), pltpu.VMEM((Ls + BLOCK, DIL_GW), F32)],
        compiler_params=_params(("parallel", "arbitrary")),
    )(src, src, src, src, src, Lr, dyr, dlr)


def _dil_combine(os_, lses, T, S):
    def fn(o0, o1, o2, l0, l1, l2):
        m = jnp.maximum(jnp.maximum(l0, l1), l2)
        e0, e1, e2 = jnp.exp(l0 - m), jnp.exp(l1 - m), jnp.exp(l2 - m)
        tot = e0 + e1 + e2
        w0, w1, w2 = e0 / tot, e1 / tot, e2 / tot
        parts = []
        for h in range(DIL_HPG):
            hs = slice(HEAD_DIM * h, HEAD_DIM * (h + 1))
            parts.append(w0[:, h:h + 1] * o0[:, hs] + w1[:, h:h + 1] * o1[:, hs] + w2[:, h:h + 1] * o2[:, hs])
        return [jnp.concatenate(parts, axis=1), m + jnp.log(tot)]

    ins = [(a, "row", None) for a in os_] + [(a, "row", None) for a in lses]
    return _rowwise("dil_combine", fn, T, _div(S, 512, 8), ins, [("row", DIL_GW, BF16), ("row", LANES, F32)], S)


def _dil_delta(dy, y, T, S):
    def fn(dyv, yv):
        prod = dyv * yv.astype(F32)
        lane = lax.broadcasted_iota(jnp.int32, (dyv.shape[0], LANES), 1)
        delta = jnp.zeros((dyv.shape[0], LANES), F32)
        for h in range(DIL_HPG):
            hs = slice(HEAD_DIM * h, HEAD_DIM * (h + 1))
            delta = jnp.where(lane == h, jnp.sum(prod[:, hs], axis=-1, keepdims=True), delta)
        return [delta, dyv]

    return _rowwise("dil_delta", fn, T, _div(S, 512, 8), [(dy, "row", None), (y, "row", None)],
                    [("row", LANES, F32), ("row", DIL_GW, BF16)], S)


def _ada_forward(c_all, w, b):
    n, D = c_all.shape
    cl = w.shape[1]

    def body(c_ref, w_ref, b_ref, o_ref, ca_ref):
        cv = c_ref[...]
        ca = (cv * _sigmoid(cv)).astype(BF16)
        ca_ref[...] = ca
        o_ref[...] = jnp.dot(ca, w_ref[...].astype(BF16), preferred_element_type=F32) + b_ref[...]

    return pl.pallas_call(
        body, name="ada_forward",
        out_shape=[jax.ShapeDtypeStruct((n, cl), F32), jax.ShapeDtypeStruct((n, D), BF16)],
        compiler_params=_params(),
    )(c_all, w, b)


def _ada_backward(ca, dmod_cols, dmod_all):
    n, D = ca.shape
    cl = dmod_cols.shape[1]

    def body(ca_ref, dc_ref, da_ref, gw_ref, gb_ref):
        gw_ref[...] = lax.dot_general(ca_ref[...], dc_ref[...].astype(BF16), (((0,), (0,)), ((), ())), preferred_element_type=F32)
        gb_ref[...] = _colsum(da_ref[...])

    return pl.pallas_call(
        body, name="ada_backward",
        out_shape=[jax.ShapeDtypeStruct((D, cl), F32), jax.ShapeDtypeStruct((1, dmod_all.shape[1]), F32)],
        compiler_params=_params(),
    )(ca, dmod_cols, dmod_all)


def _sum_devices(v):
    def body(v_ref, o_ref):
        tot = v_ref[0]
        for k in range(1, N_DEV):
            tot = tot + v_ref[k]
        o_ref[...] = tot

    return pl.pallas_call(body, name="sum_devices", out_shape=jax.ShapeDtypeStruct(v.shape[1:], F32))(v)


def _adamw(name, w, g, m, v):
    rows, cols = w.shape
    tr = _div(rows, 256, 8)

    def body(w_ref, g_ref, m_ref, v_ref, d_ref, nm_ref, nv_ref):
        gv = g_ref[...]
        nm = ADAM_B1 * m_ref[...] + (1.0 - ADAM_B1) * gv
        nv = ADAM_B2 * v_ref[...] + (1.0 - ADAM_B2) * (gv * gv)
        m_hat = nm / (1.0 - ADAM_B1 ** ADAM_STEP)
        v_hat = nv / (1.0 - ADAM_B2 ** ADAM_STEP)
        d_ref[...] = -ADAM_LR * (m_hat / (jnp.sqrt(v_hat) + ADAM_EPS) + ADAM_WD * w_ref[...])
        nm_ref[...] = nm
        nv_ref[...] = nv

    spec = pl.BlockSpec((tr, cols), lambda i: (i, 0))
    shape = jax.ShapeDtypeStruct((rows, cols), F32)
    return pl.pallas_call(
        body, name=name, out_shape=[shape, shape, shape], grid=(rows // tr,),
        in_specs=[spec, spec, spec, spec], out_specs=[spec, spec, spec],
        compiler_params=_params(("arbitrary",)),
    )(w, g, m, v)


def _pad_rows(a, rows):
    return a if a.shape[0] == rows else jnp.pad(a, ((0, rows - a.shape[0]), (0, 0)))


class _Packed:
    def __init__(self, kind, local_shape, D):
        self.kind, self.local_shape, self.D = kind, local_shape, D
        r, c = local_shape
        self.rows = {"T": c, "N": r, "F": r * c // D}[kind]
        self.rows_pad = -(-self.rows // ROW_ALIGN) * ROW_ALIGN

    def pack_local(self, w):
        if self.kind == "T":
            w = w.T
        elif self.kind == "F":
            w = w.reshape(self.rows, self.D)
        return _pad_rows(w, self.rows_pad)

    def full(self, gathered):
        g = gathered[:, :self.rows]
        if self.kind == "F":
            r, c = self.local_shape
            return g.reshape(N_DEV, r, c).transpose(1, 0, 2).reshape(r, N_DEV * c)
        return g.reshape(N_DEV * self.rows, self.D)

    def pack_grad(self, gfull):
        if self.kind == "F":
            r, c = self.local_shape
            g = gfull.reshape(r, N_DEV, c).transpose(1, 0, 2).reshape(N_DEV, self.rows, self.D)
        else:
            g = gfull.reshape(N_DEV, self.rows, self.D)
        if self.rows_pad != self.rows:
            g = jnp.pad(g, ((0, 0), (0, self.rows_pad - self.rows), (0, 0)))
        return g

    def unpack_local(self, g):
        g = g[:self.rows]
        if self.kind == "T":
            return g.T
        if self.kind == "F":
            return g.reshape(self.local_shape)
        return g


BIG = ["ffn1_w_gate", "ffn1_w_up", "ffn1_w_down", "w_in", "w_branch_a", "w_branch_b", "w_out",
       "ffn2_w_gate", "ffn2_w_up", "ffn2_w_down"]
BIG_KIND = {"ffn1_w_gate": "T", "ffn1_w_up": "T", "ffn1_w_down": "N", "w_in": "T", "w_branch_a": "F", "w_branch_b": "F",
            "w_out": "N", "ffn2_w_gate": "T", "ffn2_w_up": "T", "ffn2_w_down": "N"}
GROUPS = (("ffn1_w_gate", "ffn1_w_up", "ffn1_w_down"), ("w_in", "w_branch_a", "w_branch_b", "w_out"),
          ("ffn2_w_gate", "ffn2_w_up", "ffn2_w_down"))
SMALL = ["ada_b", "norm_ffn1", "norm_mix", "forget_bias", "norm_ffn2", "norm_final"]


def kernel(x, c, ada_w, ada_b, norm_ffn1, ffn1_w_gate, ffn1_w_up, ffn1_w_down, norm_mix, w_in, forget_bias, w_branch_a, w_branch_b, w_out, norm_ffn2, ffn2_w_gate, ffn2_w_up, ffn2_w_down, norm_final, loss_target, m_ada_w, m_ada_b, m_norm_ffn1, m_ffn1_w_gate, m_ffn1_w_up, m_ffn1_w_down, m_norm_mix, m_w_in, m_forget_bias, m_w_branch_a, m_w_branch_b, m_w_out, m_norm_ffn2, m_ffn2_w_gate, m_ffn2_w_up, m_ffn2_w_down, m_norm_final, v_ada_w, v_ada_b, v_norm_ffn1, v_ffn1_w_gate, v_ffn1_w_up, v_ffn1_w_down, v_norm_mix, v_w_in, v_forget_bias, v_w_branch_a, v_w_branch_b, v_w_out, v_norm_ffn2, v_ffn2_w_gate, v_ffn2_w_up, v_ffn2_w_down, v_norm_final):
    args = dict(locals())
    B, S, D = x.shape
    T = B * S
    cl = ada_w.shape[2]
    n_in = w_in.shape[2] * N_DEV
    nm = 2 * D + 3 * FOX_W + 3 * DIL_W
    nmp = -(-nm // 512) * 512
    GA, GB, QB, QA = 0, D, 2 * D, 2 * D + 3 * FOX_W
    xpos, ypos, cpos = _position()
    me = 4 * xpos + 2 * ypos + cpos

    packs = {n: _Packed(BIG_KIND[n], args[n].shape[1:], D) for n in BIG}
    offs, pads = {}, {}
    for names in GROUPS:
        r = 0
        for n in names:
            offs[n] = r
            r += packs[n].rows_pad
        pads[names] = -r % PACK_ROW_QUANTUM

    def pack_weights(names):
        return jnp.concatenate([packs[n].pack_local(args[n][0]).astype(BF16) for n in names]
                               + [jnp.zeros((pads[names], D), BF16)], axis=0)

    def unpack_weights(names, land):
        out = {}
        for n in names:
            p = packs[n]
            if p.kind in "TN" and p.rows == p.rows_pad and offs[n] % p.rows == 0:
                out[n] = _Slab(land, offs[n], p.rows)
            else:
                out[n] = p.full(land[:, offs[n]:offs[n] + p.rows_pad])
        return out

    def pack_grads(names, gfull):
        return jnp.concatenate([packs[n].pack_grad(gfull[n]) for n in names] + [jnp.zeros((N_DEV, pads[names], D), F32)], axis=1)

    def unpack_grads(names, g_local):
        return {n: packs[n].unpack_local(g_local[offs[n]:offs[n] + packs[n].rows_pad])[None] for n in names}

    W = unpack_weights(GROUPS[0], _weight_allgather(pack_weights(GROUPS[0])))

    c_all = _small_allgather(c, "gather_c").reshape(N_DEV * B, D)
    b_cols = lax.dynamic_slice(ada_b, (0, me * cl), (1, cl))
    mod_cols, c_act = _ada_forward(c_all, ada_w[0], b_cols)
    mod_all = _small_allgather(mod_cols, "gather_mod").transpose(1, 0, 2).reshape(N_DEV * B, N_MOD * D)
    mod = lax.dynamic_slice(mod_all, (me * B, 0), (B, N_MOD * D)).reshape(B, N_MOD, 1, D)
    sh1, sc1, gt1, sh2, sc2, gt2, sh3, sc3, gt3 = [mod[:, i] for i in range(N_MOD)]

    x0 = x.reshape(T, D)
    x1, saved1, land = _ffn_forward("ffn1", x0, norm_ffn1, sh1, sc1, gt1, W["ffn1_w_gate"], W["ffn1_w_up"], W["ffn1_w_down"], S,
                                    gather=pack_weights(GROUPS[1]))
    W.update(unpack_weights(GROUPS[1], land))
    winT = W["w_in"]
    o_f = 3 * DIL_W + 3 * FOX_W
    wmT = jnp.concatenate([winT[o_f + 8:], winT[3 * DIL_W:o_f], winT[:3 * DIL_W], jnp.zeros((nmp - nm, D), BF16)], axis=0)
    wfT = jnp.concatenate([winT[o_f:o_f + 8], jnp.zeros((LANES - 8, D), BF16)], axis=0)

    tm1k = _div(T, 1024, 8)
    fb = jnp.pad(forget_bias, ((0, 0), (0, LANES - FOX_HEADS)))

    def proj(accs, fbv):
        fl = accs[1] + fbv
        lane = lax.broadcasted_iota(jnp.int32, fl.shape, 1)
        ls = jnp.minimum(fl, 0.0) - jnp.log(1.0 + jnp.exp(-jnp.abs(fl)))
        return [accs[0], jnp.where(lane < FOX_HEADS, ls, 0.0), fl]

    tms = _div(S, 512, 8)
    h2, pm, logsig, flog, land = _norm_matmul("mix_proj", x1, norm_mix, sc2, sh2, [wmT, wfT], proj,
                                              [(nmp, BF16), (LANES, F32), (LANES, F32)], S, vecs=[fb],
                                              rider=_gather_direct_rider(pack_weights(GROUPS[2])))
    cum = _cumsum(logsig.reshape(B, S, LANES))
    cumT = cum[:, :, :8].transpose(0, 2, 1)
    pm3 = pm.reshape(B, S, nmp)
    tq = _div(S, 512, LANES)
    qcol, kcol, vcol = QB // FOX_W, QB // FOX_W + 1, QB // FOX_W + 2
    o_b, o_b32, lse_b, land = _fox_fwd(pm3, cumT, qcol, kcol, vcol, tq, rider=_gather_forward_rider(land))
    W.update(unpack_weights(GROUPS[2], land))
    y_b = o_b.reshape(T, FOX_W)

    qa_blk = QA // DIL_GW
    dil_src, dil_cols = [], []
    for g, (_, d) in enumerate(DIL_GROUPS):
        if d == 1:
            dil_src.append(pm3)
            dil_cols.append((qa_blk + g, qa_blk + N_DIL + g, qa_blk + 2 * N_DIL + g))
        else:
            starts = [QA + (i * N_DIL + g) * DIL_GW for i in range(3)]
            qkv = jnp.concatenate([pm[:, c:c + DIL_GW] for c in starts], axis=1)
            dil_src.append(_residue_order(qkv, B, S, d))
            dil_cols.append((0, 1, 2))
    dil_o, dil_lse = [], []
    for g, (_, d) in enumerate(DIL_GROUPS):
        o_g, lse_g = _dil_fwd(g, dil_src[g], dil_cols[g])
        dil_o.append(_token_order(o_g, B, S, d))
        dil_lse.append(_token_order(lse_g, B, S, d))
    y_a, L_a = _dil_combine(dil_o, dil_lse, T, S)

    wa, wb, wout = W["w_branch_a"], W["w_branch_b"], W["w_out"]
    tnd = D
    tm5 = _div(T, 512, 8)
    yap = _matmul("mix_branch_a", "nn", [[(y_a, wa)]], T, D, DIL_GW, tm5, tnd, DIL_GW, [BF16])[0]

    def merge(accs, ex):
        yapv, gav, gbv = ex
        ybp = accs[0]
        return [ybp, _sigmoid(gav.astype(F32)) * yapv.astype(F32) + _sigmoid(gbv.astype(F32)) * ybp]

    ybp, merged = _matmul("mix_branch_b", "nn", [[(y_b, wb)]], T, D, FOX_W, tm5, tnd, FOX_W, [BF16, BF16],
                          extras=[(yap, "tile", 0), (pm, "tile", GA), (pm, "tile", GB)], epilogue=merge)

    def out_proj(accs, ex):
        xv, gtv = ex
        return [xv + gtv * accs[0], accs[0]]

    x2, ymix = _matmul("mix_out", "nn", [[(merged, wout)]], T, D, D, tms, tnd, D, [F32, BF16],
                       extras=[(x1, "tile", 0), (gt2, "brow", 0)], epilogue=out_proj, rows_per_example=S)

    x3, saved3, _ = _ffn_forward("ffn2", x2, norm_ffn2, sh3, sc3, gt3, W["ffn2_w_gate"], W["ffn2_w_up"], W["ffn2_w_down"], S)

    dx3, loss_b, dg_final = _loss_head(x3, loss_target.reshape(T, D), norm_final.reshape(1, D), S)
    dx2, (dsh3, dsc3, dgt3, dg3), (dwg2, dwu2, dwd2), _ = _ffn_backward(
        "ffn2", dx3, saved3, norm_ffn2, sc3, gt3, W["ffn2_w_gate"], W["ffn2_w_up"], W["ffn2_w_down"], S)
    g3 = pack_grads(GROUPS[2], {"ffn2_w_gate": dwg2, "ffn2_w_up": dwu2, "ffn2_w_down": dwd2})

    def merge_grad(dm, ex):
        gav, gbv, yapv, ybpv = [e.astype(F32) for e in ex]
        sga, sgb = _sigmoid(gav), _sigmoid(gbv)
        return [dm * sga, dm * sgb, dm * yapv * sga * (1.0 - sga), dm * ybpv * sgb * (1.0 - sgb)]

    dym, dgt2, dyap, dybp, dga, dgb, sib3 = _gated_grad_matmul(
        "mix_merge_grad", dx2, ymix, gt2, 1.0, wout, [(pm, GA // D), (pm, GB // D), (yap, 0), (ybp, 0)], merge_grad, [BF16] * 4, S,
        rider=_sibling_exchange_rider(g3))
    sums3, own3 = _chip_sums(g3, sib3)
    tkw = _div(T, 512, LANES)
    dwout = _matmul("mix_dw_out", "tn", [[(merged, dym)]], D, D, T, D, D, tkw, [F32])[0]
    dwa = _matmul("mix_dw_a", "tn", [[(y_a, dyap)]], DIL_GW, D, T, DIL_GW, D, tkw, [F32])[0]
    dwb = _matmul("mix_dw_b", "tn", [[(y_b, dybp)]], FOX_W, D, T, FOX_W, D, tkw, [F32])[0]
    dy_a = _matmul("mix_dy_a", "nt", [[(dyap, wa)]], T, DIL_GW, D, tm1k, DIL_GW, D, [F32])[0]
    dy_b = _matmul("mix_dy_b", "nt", [[(dybp, wb)]], T, FOX_W, D, tm1k, FOX_W, D, [BF16])[0]

    do3 = dy_b.reshape(B, S, FOX_W)
    delta_b = _fox_delta(dy_b, o_b32.reshape(T, FOX_W), T, S).reshape(B, S, LANES)
    dq_b, ds_rows, dk_b, dv_b, ds_cols, recv3 = _fox_bwd(pm3, do3, delta_b, lse_b, cumT, qcol, kcol, vcol, tq,
                                                         rider=_chip_exchange_rider(sums3))

    delta_a, dy_a16 = _dil_delta(dy_a, y_a, T, S)
    dqs, dks, dvs = [], [], []
    for g, (_, d) in enumerate(DIL_GROUPS):
        dq_g, dk_g, dv_g = _dil_bwd(g, dil_src[g], dil_cols[g], _residue_order(L_a, B, S, d),
                                    _residue_order(dy_a16, B, S, d), _residue_order(delta_a, B, S, d))
        dqs.append(_token_order(dq_g, B, S, d))
        dks.append(_token_order(dk_g, B, S, d))
        dvs.append(_token_order(dv_g, B, S, d))

    dcum = ds_rows - ds_cols
    dcum_run = _cumsum(dcum)
    dcum_tot = dcum_run[:, S - 1:S, :]

    def forget_grad_fn(run, dcv, fl, tot):
        lane = lax.broadcasted_iota(jnp.int32, fl.shape, 1)
        df = jnp.where(lane < FOX_HEADS, (tot - run + dcv) * _sigmoid_exp(-fl), 0.0)
        return [df, _colsum(df)]

    df16, dfb = _rowwise("forget_gate_grad", forget_grad_fn, T, tms,
                         [(dcum_run.reshape(T, LANES), "row", None), (dcum.reshape(T, LANES), "row", None), (flog, "row", None),
                          (dcum_tot, "bvec", None)],
                         [("row", LANES, BF16), ("bacc", LANES, F32)], S)

    dpm = jnp.concatenate([dga, dgb, dq_b.reshape(T, FOX_W), dk_b.reshape(T, FOX_W), dv_b.reshape(T, FOX_W)]
                          + dqs + dks + dvs + ([jnp.zeros((T, nmp - nm), BF16)] if nmp > nm else []), axis=1)
    tmn = _div(nmp, 2048, LANES)
    dwmT = _matmul("mix_dw_in", "tn", [[(dpm, h2)]], nmp, D, T, tmn, D, tkw, [F32])[0]
    dwfT = _matmul("mix_dw_f", "tn", [[(df16, h2)]], LANES, D, T, LANES, D, tkw, [F32])[0]
    dwinT = jnp.concatenate([dwmT[QA:QA + 3 * DIL_W], dwmT[QB:QB + 3 * FOX_W], dwfT[:8], dwmT[GA:2 * D]], axis=0)
    g2 = pack_grads(GROUPS[1], {"w_in": dwinT, "w_branch_a": dwa, "w_branch_b": dwb, "w_out": dwout})
    dx1, dsh2, dsc2, dgmix, sib2 = _matmul_normmod_bwd("mix_dh", [(dpm, wmT), (df16, wfT)], x1, norm_mix, sc2, dx2, S,
                                                       rider=_sibling_exchange_rider(g2))
    sums2, own2 = _chip_sums(g2, sib2)

    own1 = []

    def ffn1_exchange(dwg, dwu, dwd):
        sums1, own = _chip_sums(pack_grads(GROUPS[0], {"ffn1_w_gate": dwg, "ffn1_w_up": dwu, "ffn1_w_down": dwd}))
        own1.append(own)
        return _chip_exchange_rider(sums1)

    dx0, (dsh1, dsc1, dgt1, dg1), _, (recv2, recv1) = _ffn_backward(
        "ffn1", dx1, saved1, norm_ffn1, sc1, gt1, W["ffn1_w_gate"], W["ffn1_w_up"], W["ffn1_w_down"], S,
        rider=_chip_exchange_rider(sums2), dh_rider=ffn1_exchange)
    own1 = own1[0]
    grad_x = dx0.reshape(B, S, D)

    dmod = jnp.concatenate([dsh1, dsc1, dgt1, dsh2, dsc2, dgt2, dsh3, dsc3, dgt3], axis=1).reshape(B, N_MOD * D)
    dmod_all = _small_allgather(dmod, "gather_dmod").reshape(N_DEV * B, N_MOD * D)
    dmod_cols = lax.dynamic_slice(dmod_all, (0, me * cl), (N_DEV * B, cl))
    g_ada_w, g_ada_b = _ada_backward(c_act, dmod_cols, dmod_all)

    fbg = jnp.sum(dfb, axis=0)
    small = jnp.concatenate([jnp.sum(dg1, axis=0), jnp.sum(dgmix, axis=0), jnp.sum(dg3, axis=0), jnp.sum(dg_final, axis=0),
                             fbg, jnp.sum(loss_b, axis=0)], axis=1)
    small = _sum_devices(_small_allgather(small, "gather_small"))
    g_small = {"norm_ffn1": small[:, 0:D], "norm_mix": small[:, D:2 * D], "norm_ffn2": small[:, 2 * D:3 * D],
               "norm_final": small[:, 3 * D:4 * D], "forget_bias": small[:, 4 * D:4 * D + FOX_HEADS], "ada_b": g_ada_b}
    loss = small[0, 4 * D + LANES]

    grads = {"ada_w": g_ada_w[None]}
    for names, own, recv in ((GROUPS[0], own1, recv1), (GROUPS[1], own2, recv2), (GROUPS[2], own3, recv3)):
        grads.update(unpack_grads(names, _final_grad_sum(own, recv)))

    delta, new_m, new_v = {}, {}, {}
    for n in ["ada_w"] + BIG:
        shp = args[n].shape
        d_, m_, v_ = _adamw(f"adamw_{n}", args[n][0], grads[n][0], args["m_" + n][0], args["v_" + n][0])
        delta[n], new_m[n], new_v[n] = d_.reshape(shp), m_.reshape(shp), v_.reshape(shp)
    sizes = [args[n].size for n in SMALL]
    tot = sum(sizes)
    padded = -(-tot // (8 * LANES)) * (8 * LANES)

    def flat(get):
        v = jnp.concatenate([get(n).reshape(-1) for n in SMALL])
        return jnp.pad(v, (0, padded - tot)).reshape(8, padded // 8)

    d_s, m_s, v_s = _adamw("adamw_small", flat(lambda n: args[n]), flat(lambda n: g_small[n]), flat(lambda n: args["m_" + n]),
                           flat(lambda n: args["v_" + n]))
    o = 0
    for n, sz in zip(SMALL, sizes):
        shp = args[n].shape
        grads[n] = g_small[n].reshape(shp)
        delta[n] = d_s.reshape(-1)[o:o + sz].reshape(shp)
        new_m[n] = m_s.reshape(-1)[o:o + sz].reshape(shp)
        new_v[n] = v_s.reshape(-1)[o:o + sz].reshape(shp)
        o += sz

    order = ["ada_w", "ada_b", "norm_ffn1", "ffn1_w_gate", "ffn1_w_up", "ffn1_w_down", "norm_mix", "w_in", "forget_bias",
             "w_branch_a", "w_branch_b", "w_out", "norm_ffn2", "ffn2_w_gate", "ffn2_w_up", "ffn2_w_down", "norm_final"]
    return (loss, grad_x, *[grads[n] for n in order], *[delta[n] for n in order], *[new_m[n] for n in order],
            *[new_v[n] for n in order])
```

```python
import functools
import math

import jax
import jax.numpy as jnp
from jax import lax
from jax.experimental import pallas as pl
from jax.experimental.pallas import tpu as pltpu

F32 = jnp.float32
BF16 = jnp.bfloat16
MESH = pl.DeviceIdType.MESH
ANY = pl.BlockSpec(memory_space=pl.ANY)
VMEM_SPEC = pl.BlockSpec(memory_space=pltpu.VMEM)

N_DEV = 8
HEAD_DIM = 64
BLOCK = 128
DIL_GROUPS = ((128, 1), (512, 4), (2048, 16))
N_DIL = len(DIL_GROUPS)
DIL_HPG = 4
DIL_GW = DIL_HPG * HEAD_DIM
DIL_W = N_DIL * DIL_GW
FOX_HEADS = 8
FOX_W = FOX_HEADS * HEAD_DIM
N_MOD = 9
RMS_EPS = 1e-6
ALIBI_MAX_BIAS = 8.0
NEG_INF = -1e30
ADAM_LR, ADAM_B1, ADAM_B2, ADAM_EPS, ADAM_WD, ADAM_STEP = 0.001, 0.9, 0.999, 1e-08, 0.01, 10
V7X_VMEM_LIMIT = 52 * 1024 * 1024
LANES = 128
ROW_ALIGN = 16
PACK_ROW_QUANTUM = 32
FOX_STRIP = 32
SCALE = 1.0 / math.sqrt(HEAD_DIM)


def _div(dim, target, quantum):
    best = None
    for t in range(quantum, min(dim, target) + 1, quantum):
        if dim % t == 0:
            best = t
    return best or dim


def _params(sem=None):
    return pltpu.CompilerParams(dimension_semantics=sem, vmem_limit_bytes=V7X_VMEM_LIMIT)


def _sigmoid(x):
    return 0.5 * jnp.tanh(0.5 * x) + 0.5


def _sigmoid_exp(x):
    return 1.0 / (1.0 + jnp.exp(-x))


def _position():
    x, y, c = lax.axis_index("x"), lax.axis_index("y"), lax.axis_index("c")
    return x, y, c


def _small_allgather(v, name):
    rows, cols = v.shape

    def body(v_ref, out_ref, send_sems, recv_sems):
        x, y, c = _position()
        me = 4 * x + 2 * y + c
        out_ref[me] = v_ref[...]

        def peer(k):
            return (1 - x if k & 4 else x, 1 - y if k & 2 else y, 1 - c if k & 1 else c)

        def copy(k, slot):
            return pltpu.make_async_remote_copy(
                src_ref=v_ref, dst_ref=out_ref.at[slot], send_sem=send_sems.at[k - 1], recv_sem=recv_sems.at[k - 1],
                device_id=peer(k), device_id_type=MESH)

        sends = [copy(k, me) for k in range(1, N_DEV)]
        for cp in sends:
            cp.start()
        for k in range(1, N_DEV):
            px, py, pc = peer(k)
            copy(k, 4 * px + 2 * py + pc).wait_recv()
        for cp in sends:
            cp.wait_send()

    return pl.pallas_call(
        body, name=name,
        out_shape=jax.ShapeDtypeStruct((N_DEV, rows, cols), v.dtype),
        in_specs=[VMEM_SPEC], out_specs=VMEM_SPEC,
        scratch_shapes=[pltpu.SemaphoreType.DMA((N_DEV - 1,)), pltpu.SemaphoreType.DMA((N_DEV - 1,))],
    )(v)


def _weight_allgather(p):
    rows, cols = p.shape

    def body(p_ref, out_ref, send_sems, recv_sems, local_sem):
        x, y, c = _position()
        me, sibling = (x, y, c), (x, y, 1 - c)
        chips = [(1 - x, y), (x, 1 - y), (1 - x, 1 - y)]

        def slot(px, py, pc):
            return out_ref.at[4 * px + 2 * py + pc]

        def copy(k, block, to, src=None):
            return pltpu.make_async_remote_copy(
                src_ref=slot(*block) if src is None else src, dst_ref=slot(*block),
                send_sem=send_sems.at[k], recv_sem=recv_sems.at[k], device_id=to, device_id_type=MESH)

        mine = pltpu.make_async_copy(p_ref, slot(*me), local_sem)
        mine.start()
        first = [copy(0, me, sibling, src=p_ref)]
        first += [copy(1 + j, me, (*chip, c), src=p_ref) for j, chip in enumerate(chips)]
        for cp in first:
            cp.start()
        passed = [copy(4 + j, (*chip, c), sibling) for j, chip in enumerate(chips)]
        for j, chip in enumerate(chips):
            copy(1 + j, (*chip, c), me).wait_recv()
            passed[j].start()
        copy(0, sibling, me).wait_recv()
        for j, chip in enumerate(chips):
            copy(4 + j, (*chip, 1 - c), me).wait_recv()
        for cp in first + passed:
            cp.wait_send()
        mine.wait()

    return pl.pallas_call(
        body, name="weight_allgather",
        out_shape=jax.ShapeDtypeStruct((N_DEV, rows, cols), p.dtype),
        in_specs=[ANY], out_specs=ANY,
        scratch_shapes=[pltpu.SemaphoreType.DMA((7,)), pltpu.SemaphoreType.DMA((7,)), pltpu.SemaphoreType.DMA],
    )(p)


def _grad_exchange_sibling(g):
    _, rows, cols = g.shape

    def body(g_ref, out_ref, send_sems, recv_sems):
        x, y, c = _position()
        sibling = (x, y, 1 - c)

        def copy(q):
            px, py = q >> 1, q & 1
            return pltpu.make_async_remote_copy(
                src_ref=g_ref.at[4 * px + 2 * py + (1 - c)], dst_ref=out_ref.at[q],
                send_sem=send_sems.at[q], recv_sem=recv_sems.at[q], device_id=sibling, device_id_type=MESH)

        copies = [copy(q) for q in range(4)]
        for cp in copies:
            cp.start()
        for cp in copies:
            cp.wait_recv()
        for cp in copies:
            cp.wait_send()

    return pl.pallas_call(
        body, name="grad_exchange_sibling",
        out_shape=jax.ShapeDtypeStruct((4, rows, cols), g.dtype),
        in_specs=[ANY], out_specs=ANY,
        scratch_shapes=[pltpu.SemaphoreType.DMA((4,)), pltpu.SemaphoreType.DMA((4,))],
    )(g)


class _Rider:
    def __init__(self, operands, out_shapes, n_send, n_recv, start, finish, aliases=None):
        self.operands, self.out_shapes = list(operands), list(out_shapes)
        self.n_send, self.n_recv, self.start, self.finish = n_send, n_recv, start, finish
        self.aliases = aliases or {}


def _pcall(body, *, name, grid, in_specs, operands, out_shape, out_specs, scratch_shapes, params, rider=None):
    if rider is None:
        return pl.pallas_call(body, name=name, out_shape=out_shape, grid=grid, in_specs=in_specs, out_specs=out_specs,
                              scratch_shapes=scratch_shapes, compiler_params=params)(*operands)
    n_in, n_out, n_sc = len(operands), len(out_shape), len(scratch_shapes)
    r_in, r_out = len(rider.operands), len(rider.out_shapes)

    def wrapped(*refs):
        ins, rins = refs[:n_in], refs[n_in:n_in + r_in]
        outs, routs = refs[n_in + r_in:n_in + r_in + n_out], refs[n_in + r_in + n_out:n_in + r_in + n_out + r_out]
        rest = refs[n_in + r_in + n_out + r_out:]
        scratch, sems = rest[:n_sc], rest[n_sc:]
        ids = [pl.program_id(a) for a in range(len(grid))]
        first, last = ids[0] == 0, ids[0] == grid[0] - 1
        for a in range(1, len(grid)):
            first, last = first & (ids[a] == 0), last & (ids[a] == grid[a] - 1)

        @pl.when(first)
        def _():
            rider.start(rins, routs, *sems)

        body(*ins, *outs, *scratch)

        @pl.when(last)
        def _():
            rider.finish(rins, routs, *sems)

    return pl.pallas_call(
        wrapped, name=name, out_shape=list(out_shape) + rider.out_shapes, grid=grid,
        in_specs=list(in_specs) + [ANY] * r_in, out_specs=list(out_specs) + [ANY] * r_out,
        scratch_shapes=list(scratch_shapes) + [pltpu.SemaphoreType.DMA((rider.n_send,)), pltpu.SemaphoreType.DMA((rider.n_recv,))],
        input_output_aliases={n_in + i: n_out + o for i, o in rider.aliases.items()},
        compiler_params=params,
    )(*operands, *rider.operands)


def _flips(x, y, c):
    return [(x, y, 1 - c), (1 - x, y, c), (x, 1 - y, c), (1 - x, 1 - y, c)]


def _gather_direct_rider(p):
    rows, cols = p.shape

    def copies(p_ref, land, send_sems, recv_sems):
        x, y, c = _position()
        me = 4 * x + 2 * y + c
        peers = _flips(x, y, c)
        sends = [pltpu.make_async_remote_copy(src_ref=p_ref, dst_ref=land.at[me], send_sem=send_sems.at[k], recv_sem=recv_sems.at[k],
                                              device_id=to, device_id_type=MESH) for k, to in enumerate(peers)]
        recvs = [pltpu.make_async_remote_copy(src_ref=p_ref, dst_ref=land.at[4 * px + 2 * py + pc], send_sem=send_sems.at[k],
                                              recv_sem=recv_sems.at[k], device_id=(px, py, pc), device_id_type=MESH)
                 for k, (px, py, pc) in enumerate(peers)]
        mine = pltpu.make_async_copy(p_ref, land.at[me], send_sems.at[len(peers)])
        return sends, recvs, mine

    def start(rins, routs, send_sems, recv_sems):
        sends, _, mine = copies(rins[0], routs[0], send_sems, recv_sems)
        mine.start()
        for cp in sends:
            cp.start()

    def finish(rins, routs, send_sems, recv_sems):
        sends, recvs, mine = copies(rins[0], routs[0], send_sems, recv_sems)
        for cp in recvs:
            cp.wait_recv()
        for cp in sends:
            cp.wait_send()
        mine.wait()

    return _Rider([p], [jax.ShapeDtypeStruct((N_DEV, rows, cols), p.dtype)], 5, 4, start, finish)


def _gather_forward_rider(land):
    def copies(buf, send_sems, recv_sems):
        x, y, c = _position()
        chips = [(1 - x, y), (x, 1 - y), (1 - x, 1 - y)]
        sends = [pltpu.make_async_remote_copy(src_ref=buf.at[4 * px + 2 * py + c], dst_ref=buf.at[4 * px + 2 * py + c],
                                              send_sem=send_sems.at[k], recv_sem=recv_sems.at[k], device_id=(x, y, 1 - c),
                                              device_id_type=MESH) for k, (px, py) in enumerate(chips)]
        recvs = [pltpu.make_async_remote_copy(src_ref=buf.at[4 * px + 2 * py + 1 - c], dst_ref=buf.at[4 * px + 2 * py + 1 - c],
                                              send_sem=send_sems.at[k], recv_sem=recv_sems.at[k], device_id=(x, y, 1 - c),
                                              device_id_type=MESH) for k, (px, py) in enumerate(chips)]
        return sends, recvs

    def start(rins, routs, send_sems, recv_sems):
        for cp in copies(routs[0], send_sems, recv_sems)[0]:
            cp.start()

    def finish(rins, routs, send_sems, recv_sems):
        sends, recvs = copies(routs[0], send_sems, recv_sems)
        for cp in recvs:
            cp.wait_recv()
        for cp in sends:
            cp.wait_send()

    return _Rider([land], [jax.ShapeDtypeStruct(land.shape, land.dtype)], 3, 3, start, finish, aliases={0: 0})


def _chip_exchange_rider(s):
    def copies(s_ref, out_ref, send_sems, recv_sems):
        x, y, c = _position()
        chips = [(1 - x, y), (x, 1 - y), (1 - x, 1 - y)]
        return [pltpu.make_async_remote_copy(src_ref=s_ref.at[k], dst_ref=out_ref.at[k], send_sem=send_sems.at[k],
                                             recv_sem=recv_sems.at[k], device_id=(*chips[k], c), device_id_type=MESH)
                for k in range(3)]

    def start(rins, routs, send_sems, recv_sems):
        for cp in copies(rins[0], routs[0], send_sems, recv_sems):
            cp.start()

    def finish(rins, routs, send_sems, recv_sems):
        cps = copies(rins[0], routs[0], send_sems, recv_sems)
        for cp in cps:
            cp.wait_recv()
        for cp in cps:
            cp.wait_send()

    return _Rider([s], [jax.ShapeDtypeStruct(s.shape, s.dtype)], 3, 3, start, finish)


def _chip_partial_sums(g, recv_sib, jj, qq):
    _, rows, cols = g.shape
    tr = _div(rows, 512, ROW_ALIGN)

    def body(jj_ref, qq_ref, g_ref, r_ref, o_ref):
        o_ref[...] = (g_ref[...] + r_ref[...]).astype(o_ref.dtype)

    return pl.pallas_call(
        body, name="chip_partial_sums",
        out_shape=jax.ShapeDtypeStruct((3, rows, cols), BF16),
        grid_spec=pltpu.PrefetchScalarGridSpec(
            num_scalar_prefetch=2, grid=(3, rows // tr),
            in_specs=[pl.BlockSpec((None, tr, cols), lambda k, i, jj, qq: (jj[k], i, 0)),
                      pl.BlockSpec((None, tr, cols), lambda k, i, jj, qq: (qq[k], i, 0))],
            out_specs=pl.BlockSpec((None, tr, cols), lambda k, i, jj, qq: (k, i, 0))),
        compiler_params=_params(("arbitrary", "arbitrary")),
    )(jj, qq, g, recv_sib)


def _own_partial_sum(g, recv_sib, jj, qq):
    _, rows, cols = g.shape
    tr = _div(rows, 512, ROW_ALIGN)

    def body(jj_ref, qq_ref, g_ref, r_ref, o_ref):
        o_ref[...] = g_ref[...] + r_ref[...]

    return pl.pallas_call(
        body, name="own_partial_sum",
        out_shape=jax.ShapeDtypeStruct((rows, cols), F32),
        grid_spec=pltpu.PrefetchScalarGridSpec(
            num_scalar_prefetch=2, grid=(rows // tr,),
            in_specs=[pl.BlockSpec((None, tr, cols), lambda i, jj, qq: (jj[0], i, 0)),
                      pl.BlockSpec((None, tr, cols), lambda i, jj, qq: (qq[0], i, 0))],
            out_specs=pl.BlockSpec((tr, cols), lambda i, jj, qq: (i, 0))),
        compiler_params=_params(("arbitrary",)),
    )(jj, qq, g, recv_sib)


def _final_grad_sum(own, recv):
    rows, cols = own.shape
    tr = _div(rows, 512, ROW_ALIGN)

    def body(o_ref, r_ref, out_ref):
        out_ref[...] = ((o_ref[...] + r_ref[0].astype(F32)) + r_ref[1].astype(F32)) + r_ref[2].astype(F32)

    return pl.pallas_call(
        body, name="final_grad_sum",
        out_shape=jax.ShapeDtypeStruct((rows, cols), F32),
        grid=(rows // tr,),
        in_specs=[pl.BlockSpec((tr, cols), lambda i: (i, 0)), pl.BlockSpec((3, tr, cols), lambda i: (0, i, 0))],
        out_specs=pl.BlockSpec((tr, cols), lambda i: (i, 0)),
        compiler_params=_params(("arbitrary",)),
    )(own, recv)


def _sibling_exchange_rider(g):
    _, rows, cols = g.shape

    def copies(g_ref, out_ref, send_sems, recv_sems):
        x, y, c = _position()
        return [pltpu.make_async_remote_copy(
            src_ref=g_ref.at[4 * (q >> 1) + 2 * (q & 1) + (1 - c)], dst_ref=out_ref.at[q], send_sem=send_sems.at[q],
            recv_sem=recv_sems.at[q], device_id=(x, y, 1 - c), device_id_type=MESH) for q in range(4)]

    def start(rins, routs, send_sems, recv_sems):
        for cp in copies(rins[0], routs[0], send_sems, recv_sems):
            cp.start()

    def finish(rins, routs, send_sems, recv_sems):
        cps = copies(rins[0], routs[0], send_sems, recv_sems)
        for cp in cps:
            cp.wait_recv()
        for cp in cps:
            cp.wait_send()

    return _Rider([g], [jax.ShapeDtypeStruct((4, rows, cols), g.dtype)], 4, 4, start, finish)


def _chip_sums(g, recv_sib=None):
    x, y, c = _position()
    chips = [(1 - x, y), (x, 1 - y), (1 - x, 1 - y)]
    jj = jnp.stack([4 * px + 2 * py + c for px, py in chips]).astype(jnp.int32)
    qq = jnp.stack([2 * px + py for px, py in chips]).astype(jnp.int32)
    jme = jnp.reshape(4 * x + 2 * y + c, (1,)).astype(jnp.int32)
    qme = jnp.reshape(2 * x + y, (1,)).astype(jnp.int32)
    if recv_sib is None:
        recv_sib = _grad_exchange_sibling(g)
    return _chip_partial_sums(g, recv_sib, jj, qq), _own_partial_sum(g, recv_sib, jme, qme)


def _matmul(name, form, prods, M, N, K, tm, tn, tk, out_dtypes, extras=(), epilogue=None, rows_per_example=None, rider=None):
    nk = K // tk
    n_acc = len(prods)
    flat = [ab for group in prods for ab in group]
    dims = {"nn": (((1,), (0,)), ((), ())), "nt": (((1,), (1,)), ((), ())), "tn": (((0,), (0,)), ((), ()))}[form]
    direct = nk > 1 and epilogue is None and n_acc == 1 and list(out_dtypes) == [F32]

    def spec(shape, index_map, whole):
        if whole:
            return pl.BlockSpec(shape, index_map, pipeline_mode=pl.Buffered(1))
        return pl.BlockSpec(shape, index_map)

    if form == "tn":
        a_spec = spec((tk, tm), lambda i, j, k: (k, i), nk == 1 and M == tm)
    else:
        a_spec = spec((tm, tk), lambda i, j, k: (i, k), nk == 1 and M == tm)
    if form == "nt":
        b_spec = spec((tn, tk), lambda i, j, k: (j, k), nk == 1 and N == tn)
    else:
        b_spec = spec((tk, tn), lambda i, j, k: (k, j), nk == 1 and N == tn)
    in_specs, operands = [], []
    for a, b in flat:
        in_specs += [a_spec, _weight_spec(b) if isinstance(b, _Slab) else b_spec]
        operands += [a, _weight_operand(b)]
    for arr, kind, off in extras:
        if kind == "tile":
            assert off % tn == 0
            in_specs.append(pl.BlockSpec((tm, tn), functools.partial(lambda i, j, k, o: (i, j + o), o=off // tn)))
        else:
            tiles = rows_per_example // tm
            in_specs.append(pl.BlockSpec((None, 1, tn), functools.partial(lambda i, j, k, t: (i // t, 0, j), t=tiles)))
        operands.append(arr)
    n_in, n_out = len(operands), len(out_dtypes)

    def body(*refs):
        in_refs, out_refs, acc_refs = refs[:n_in], refs[n_in:n_in + n_out], refs[n_in + n_out:]
        k = pl.program_id(2)
        partials, p = [], 0
        for group in prods:
            tot = None
            for _ in group:
                d = lax.dot_general(in_refs[2 * p][...], _weight_value(in_refs[2 * p + 1], flat[p][1]), dims,
                                    preferred_element_type=F32)
                tot = d if tot is None else tot + d
                p += 1
            partials.append(tot)

        def finish(accs):
            ex = [r[...] for r in in_refs[2 * len(flat):]]
            outs = epilogue(accs, ex) if epilogue is not None else accs
            for r, o in zip(out_refs, outs):
                r[...] = o.astype(r.dtype)

        if nk == 1:
            finish(partials)
        elif direct:
            @pl.when(k == 0)
            def _():
                out_refs[0][...] = partials[0]

            @pl.when(k > 0)
            def _():
                out_refs[0][...] += partials[0]
        else:
            @pl.when(k == 0)
            def _():
                for r, v in zip(acc_refs, partials):
                    r[...] = v

            @pl.when(k > 0)
            def _():
                for r, v in zip(acc_refs, partials):
                    r[...] += v

            @pl.when(k == nk - 1)
            def _():
                finish([r[...] for r in acc_refs])

    return _pcall(
        body, name=name,
        out_shape=[jax.ShapeDtypeStruct((M, N), dt) for dt in out_dtypes],
        grid=(M // tm, N // tn, nk),
        in_specs=in_specs, operands=operands,
        out_specs=[pl.BlockSpec((tm, tn), lambda i, j, k: (i, j)) for _ in out_dtypes],
        scratch_shapes=[pltpu.VMEM((tm, tn), F32) for _ in range(n_acc)] if nk > 1 and not direct else [],
        params=_params(("parallel", "parallel", "arbitrary")), rider=rider)


def _rowwise(name, fn, T, tm, ins, outs, rows_per_example):
    tiles = rows_per_example // tm
    n_ex = T // rows_per_example
    in_specs, operands = [], []
    for arr, kind, arg in ins:
        if kind == "row":
            if arg is None:
                in_specs.append(pl.BlockSpec((tm, arr.shape[1]), lambda i: (i, 0)))
            else:
                in_specs.append(pl.BlockSpec((tm, arg[0]), functools.partial(lambda i, cb: (i, cb), cb=arg[1])))
        elif kind == "bvec":
            in_specs.append(pl.BlockSpec((None, 1, arr.shape[2]), lambda i: (i // tiles, 0, 0)))
        else:
            in_specs.append(pl.BlockSpec((1, arr.shape[1]), lambda i: (0, 0)))
        operands.append(arr)
    out_shape, out_specs = [], []
    for kind, cols, dt in outs:
        if kind == "row":
            out_shape.append(jax.ShapeDtypeStruct((T, cols), dt))
            out_specs.append(pl.BlockSpec((tm, cols), lambda i: (i, 0)))
        else:
            out_shape.append(jax.ShapeDtypeStruct((n_ex, 1, cols), F32))
            out_specs.append(pl.BlockSpec((None, 1, cols), lambda i: (i // tiles, 0, 0)))
    n_in = len(operands)

    def body(*refs):
        i = pl.program_id(0)
        vals = fn(*[r[...] for r in refs[:n_in]])
        for (kind, _, _), r, v in zip(outs, refs[n_in:], vals):
            if kind == "row":
                r[...] = v.astype(r.dtype)
            else:
                @pl.when(i % tiles == 0)
                def _():
                    r[...] = jnp.zeros_like(r)

                r[...] += v

    return pl.pallas_call(
        body, name=name, out_shape=out_shape, grid=(T // tm,), in_specs=in_specs, out_specs=out_specs,
        compiler_params=_params(("arbitrary",)),
    )(*operands)


def _colsum(v):
    return jnp.sum(v, axis=0, keepdims=True)


def _rms_parts(x):
    rstd = lax.rsqrt(jnp.mean(x * x, axis=-1, keepdims=True) + RMS_EPS)
    return x * rstd, rstd


def _resident(shape):
    return pl.BlockSpec(shape, lambda i: (0, 0), pipeline_mode=pl.Buffered(1))


class _Slab:
    def __init__(self, land, off, rows):
        assert off % rows == 0 and rows % ROW_ALIGN == 0
        self.land, self.off, self.rows = land, off, rows
        self.shape = (N_DEV * rows, land.shape[2])


def _weight_spec(w):
    if isinstance(w, _Slab):
        return pl.BlockSpec((N_DEV, w.rows, w.shape[1]), lambda *_: (0, w.off // w.rows, 0), pipeline_mode=pl.Buffered(1))
    return pl.BlockSpec(w.shape, lambda *_: (0, 0), pipeline_mode=pl.Buffered(1))


def _weight_operand(w):
    return w.land if isinstance(w, _Slab) else w


def _weight_value(ref, w):
    return ref[...].reshape(w.shape) if isinstance(w, _Slab) else ref[...]


def _example_acc(r, i, tiles, v):
    @pl.when(i % tiles == 0)
    def _():
        r[...] = jnp.zeros_like(r)

    r[...] += v


def _norm_matmul(name, x, g, sc, sh, weights, epilogue, outs, S, vecs=(), rider=None):
    T, D = x.shape
    tm = _div(S, 256, 8)
    tiles = S // tm
    nw, nv = len(weights), len(vecs)

    def body(*refs):
        x_ref, g_ref, sc_ref, sh_ref = refs[:4]
        w_refs, v_refs = refs[4:4 + nw], refs[4 + nw:4 + nw + nv]
        h_ref, out_refs = refs[4 + nw + nv], refs[5 + nw + nv:]
        xhat, _ = _rms_parts(x_ref[...])
        h = ((xhat * g_ref[...]) * (1.0 + sc_ref[...]) + sh_ref[...]).astype(BF16)
        h_ref[...] = h
        accs = [lax.dot_general(h, _weight_value(r, w), NT, preferred_element_type=F32) for r, w in zip(w_refs, weights)]
        for r, o in zip(out_refs, epilogue(accs, *[v[...] for v in v_refs])):
            r[...] = o.astype(r.dtype)

    bvec = pl.BlockSpec((None, 1, D), lambda i: (i // tiles, 0, 0))
    return _pcall(
        body, name=name,
        out_shape=[jax.ShapeDtypeStruct((T, D), BF16)] + [jax.ShapeDtypeStruct((T, w), dt) for w, dt in outs],
        grid=(T // tm,),
        in_specs=[pl.BlockSpec((tm, D), lambda i: (i, 0)), pl.BlockSpec((1, D), lambda i: (0, 0)), bvec, bvec]
        + [_weight_spec(w) for w in weights] + [pl.BlockSpec(v.shape, lambda i: (0, 0)) for v in vecs],
        operands=[x, g, sc, sh, *[_weight_operand(w) for w in weights], *vecs],
        out_specs=[pl.BlockSpec((tm, D), lambda i: (i, 0))] + [pl.BlockSpec((tm, w), lambda i: (i, 0)) for w, _ in outs],
        scratch_shapes=[], params=_params(("arbitrary",)), rider=rider)


def _gated_grad_matmul(name, dx, y, gt, coeff, w, tiles_in, epilogue, outs, S, rider=None):
    T, D = dx.shape
    N = w.shape[0]
    tm = _div(S, 256, 8)
    tiles = S // tm
    nt = len(tiles_in)

    def body(*refs):
        dx_ref, y_ref, gt_ref, w_ref = refs[:4]
        t_refs, dy_ref, dgt_ref, out_refs = refs[4:4 + nt], refs[4 + nt], refs[5 + nt], refs[6 + nt:]
        i = pl.program_id(0)
        dxv = dx_ref[...]
        dy = (coeff * gt_ref[...] * dxv).astype(BF16)
        dy_ref[...] = dy
        _example_acc(dgt_ref, i, tiles, _colsum(coeff * dxv * y_ref[...].astype(F32)))
        acc = lax.dot_general(dy, _weight_value(w_ref, w), NT, preferred_element_type=F32)
        for r, o in zip(out_refs, epilogue(acc, [t[...] for t in t_refs])):
            r[...] = o.astype(r.dtype)

    row = pl.BlockSpec((tm, D), lambda i: (i, 0))
    bvec = pl.BlockSpec((None, 1, D), lambda i: (i // tiles, 0, 0))
    return _pcall(
        body, name=name,
        out_shape=[jax.ShapeDtypeStruct((T, D), BF16), jax.ShapeDtypeStruct((T // S, 1, D), F32)]
        + [jax.ShapeDtypeStruct((T, N), dt) for dt in outs],
        grid=(T // tm,),
        in_specs=[row, row, bvec, _weight_spec(w)]
        + [pl.BlockSpec((tm, N), functools.partial(lambda i, cb: (i, cb), cb=cb)) for _, cb in tiles_in],
        operands=[dx, y, gt, _weight_operand(w), *[t for t, _ in tiles_in]],
        out_specs=[row, bvec] + [pl.BlockSpec((tm, N), lambda i: (i, 0)) for _ in outs],
        scratch_shapes=[], params=_params(("arbitrary",)), rider=rider)


def _matmul_normmod_bwd(name, prods, x, g, sc, dres, S, rider=None):
    T, D = x.shape
    tm = _div(S, 256, 8)
    tiles = S // tm
    npr = len(prods)

    def body(*refs):
        ab = refs[:2 * npr]
        x_ref, g_ref, sc_ref, dr_ref = refs[2 * npr:2 * npr + 4]
        dx_ref, dsh_ref, dsc_ref, dg_ref = refs[2 * npr + 4:]
        i = pl.program_id(0)
        dh = None
        for p in range(npr):
            d = lax.dot_general(ab[2 * p][...], _weight_value(ab[2 * p + 1], prods[p][1]), NN, preferred_element_type=F32)
            dh = d if dh is None else dh + d
        xhat, rstd = _rms_parts(x_ref[...])
        gv = g_ref[...]
        dn = dh * (1.0 + sc_ref[...])
        dxh = dn * gv
        dx_ref[...] = dr_ref[...] + rstd * (dxh - xhat * jnp.mean(dxh * xhat, axis=-1, keepdims=True))
        _example_acc(dsh_ref, i, tiles, _colsum(dh))
        _example_acc(dsc_ref, i, tiles, _colsum(dh * (xhat * gv)))
        _example_acc(dg_ref, i, tiles, _colsum(dn * xhat))

    row = pl.BlockSpec((tm, D), lambda i: (i, 0))
    bvec = pl.BlockSpec((None, 1, D), lambda i: (i // tiles, 0, 0))
    in_specs, operands = [], []
    for a, b in prods:
        in_specs += [pl.BlockSpec((tm, a.shape[1]), lambda i: (i, 0)), _weight_spec(b)]
        operands += [a, _weight_operand(b)]
    acc_shape = jax.ShapeDtypeStruct((T // S, 1, D), F32)
    return _pcall(
        body, name=name,
        out_shape=[jax.ShapeDtypeStruct((T, D), F32), acc_shape, acc_shape, acc_shape],
        grid=(T // tm,),
        in_specs=in_specs + [row, pl.BlockSpec((1, D), lambda i: (0, 0)), bvec, row],
        operands=[*operands, x, g, sc, dres],
        out_specs=[row, bvec, bvec, bvec],
        scratch_shapes=[], params=_params(("arbitrary",)), rider=rider)


def _ffn_forward(tag, x, g, sh, sc, gt, wgT, wuT, wd, S, gather=None):
    T, D = x.shape
    F = wd.shape[0]

    def gateup(accs):
        a, u = accs
        return [a, u, a * _sigmoid(a) * u]

    h, a, u, s, *land = _norm_matmul(f"{tag}_gateup", x, g, sc, sh, [wgT, wuT], gateup, [(F, BF16)] * 3, S,
                                     rider=None if gather is None else _gather_direct_rider(gather))

    def down(accs, ex):
        xv, gtv = ex
        return [xv + 0.5 * gtv * accs[0], accs[0]]

    tmd = _div(S, 512, 8)
    x_new, y, *land = _matmul(f"{tag}_down", "nn", [[(s, wd)]], T, D, F, tmd, D, F, [F32, BF16],
                              extras=[(x, "tile", 0), (gt, "brow", 0)], epilogue=down, rows_per_example=S,
                              rider=None if gather is None else _gather_forward_rider(land[0]))
    return x_new, (x, h, a, u, s, y), (land[0] if land else None)


def _ffn_backward(tag, dx_out, saved, g, sc, gt, wgT, wuT, wd, S, rider=None, dh_rider=None):
    x, h, a, u, s, y = saved
    T, D = x.shape
    F = wd.shape[0]

    def act_grad(ds, ex):
        av, uv = ex[0].astype(F32), ex[1].astype(F32)
        sg = _sigmoid(av)
        return [ds * uv * (sg * (1.0 + av * (1.0 - sg))), ds * (av * sg)]

    dy, dgt, da, du, *rode = _gated_grad_matmul(f"{tag}_act_grad", dx_out, y, gt, 0.5, wd, [(a, 0), (u, 0)], act_grad,
                                                [BF16, BF16], S, rider=rider)
    tkw = _div(T, 1024, LANES)
    dwd = _matmul(f"{tag}_dw_down", "tn", [[(s, dy)]], F, D, T, F, D, tkw, [F32])[0]
    dwgT = _matmul(f"{tag}_dw_gate", "tn", [[(da, h)]], F, D, T, F, D, tkw, [F32])[0]
    dwuT = _matmul(f"{tag}_dw_up", "tn", [[(du, h)]], F, D, T, F, D, tkw, [F32])[0]
    dx_in, dsh, dsc, dg, *rode_dh = _matmul_normmod_bwd(
        f"{tag}_dh", [(da, wgT), (du, wuT)], x, g, sc, dx_out, S,
        rider=None if dh_rider is None else dh_rider(dwgT, dwuT, dwd))
    return dx_in, (dsh, dsc, dgt, dg), (dwgT, dwuT, dwd), rode + rode_dh


def _loss_head(x, tgt, g, S):
    T, D = x.shape

    def fn(xv, tv, gv):
        xhat, rstd = _rms_parts(xv)
        e = xhat * gv - tv
        loss = jnp.broadcast_to(0.5 / D * jnp.sum(_colsum(e * e), axis=1, keepdims=True), (1, LANES))
        dy = e * (1.0 / D)
        dxh = dy * gv
        dx = rstd * (dxh - xhat * jnp.mean(dxh * xhat, axis=-1, keepdims=True))
        return [dx, loss, _colsum(dy * xhat)]

    return _rowwise("loss_head", fn, T, _div(S, 512, 8), [(x, "row", None), (tgt, "row", None), (g, "vec", None)],
                    [("row", D, F32), ("bacc", LANES, F32), ("bacc", D, F32)], S)


def _cumsum(v):
    B, S, _ = v.shape
    rows = _div(S, 1024, BLOCK)

    def body(x_ref, o_ref, carry):
        i = pl.program_id(1)

        @pl.when(i == 0)
        def _():
            carry[...] = jnp.zeros_like(carry)

        r = lax.broadcasted_iota(jnp.int32, (BLOCK, BLOCK), 0)
        c = lax.broadcasted_iota(jnp.int32, (BLOCK, BLOCK), 1)
        tri = (c <= r).astype(F32)
        last = carry[0:1, :]
        for j in range(0, rows, BLOCK):
            cum = jnp.dot(tri, x_ref[j:j + BLOCK, :], precision=lax.Precision.HIGHEST, preferred_element_type=F32) + last
            o_ref[j:j + BLOCK, :] = cum
            last = cum[BLOCK - 1:BLOCK, :]
        carry[...] = jnp.broadcast_to(last, carry.shape)

    return pl.pallas_call(
        body, name="cumsum", out_shape=jax.ShapeDtypeStruct(v.shape, F32), grid=(B, S // rows),
        in_specs=[pl.BlockSpec((None, rows, LANES), lambda b, i: (b, i, 0))],
        out_specs=pl.BlockSpec((None, rows, LANES), lambda b, i: (b, i, 0)),
        scratch_shapes=[pltpu.VMEM((8, LANES), F32)],
        compiler_params=_params(("arbitrary", "arbitrary")),
    )(v)


def _with_ones(x):
    lane = lax.broadcasted_iota(jnp.int32, (x.shape[0], HEAD_DIM), 1)
    return jnp.concatenate([x, jnp.where(lane == 0, 1.0, 0.0).astype(x.dtype)], axis=1)


def _causal_strip(s, r):
    qpos = r + lax.broadcasted_iota(jnp.int32, s.shape, 0)
    kpos = lax.broadcasted_iota(jnp.int32, s.shape, 1)
    return jnp.where(kpos <= qpos, s, NEG_INF)


NT = (((1,), (1,)), ((), ()))
NN = (((1,), (0,)), ((), ()))
TN = (((0,), (0,)), ((), ()))


def _fox_fwd(pm3, cumT, qcol, kcol, vcol, tq, rider=None):
    B, S, _ = pm3.shape
    nq = S // tq
    strips = range(0, tq, FOX_STRIP)

    def body(q_ref, k_ref, v_ref, ck_ref, o_ref, o32_ref, lse_ref, s_sc, p_sc, al_sc, m_sc, acc_sc):
        qi, kj = pl.program_id(1), pl.program_id(2)

        @pl.when(kj == 0)
        def _():
            m_sc[...] = jnp.full_like(m_sc, NEG_INF)
            acc_sc[...] = jnp.zeros_like(acc_sc)

        def tile(diagonal):
            def scores(h):
                hs = slice(HEAD_DIM * h, HEAD_DIM * (h + 1))
                s_sc[h % 2] = lax.dot_general(q_ref[:, hs] * SCALE, k_ref[:, hs], NT, preferred_element_type=F32)

            def accumulate(h):
                hs = slice(HEAD_DIM * h, HEAD_DIM * (h + 1))
                acc_sc[h] = al_sc[h % 2] * acc_sc[h] + lax.dot_general(p_sc[h % 2], _with_ones(v_ref[:, hs]), NN,
                                                                       preferred_element_type=F32)

            scores(0)
            for h in range(FOX_HEADS):
                b = h % 2
                if h + 1 < FOX_HEADS:
                    scores(h + 1)
                if h >= 1:
                    accumulate(h - 1)
                ck = ck_ref[h:h + 1, :]
                for r in strips:
                    rows = slice(r, r + FOX_STRIP)
                    s = s_sc[b, rows, :] - ck
                    if diagonal:
                        s = _causal_strip(s, r)
                    m_prev = m_sc[h, rows, :]
                    m_new = jnp.maximum(m_prev, jnp.max(s, axis=-1, keepdims=True))
                    p_sc[b, rows, :] = jnp.exp(s - m_new).astype(BF16)
                    al_sc[b, rows, :] = jnp.exp(m_prev - m_new)
                    m_sc[h, rows, :] = m_new
            accumulate(FOX_HEADS - 1)

        @pl.when(kj < qi)
        def _():
            tile(False)

        @pl.when(kj == qi)
        def _():
            tile(True)

        @pl.when(kj == nq - 1)
        def _():
            lse_ref[...] = jnp.zeros_like(lse_ref)
            for h in range(FOX_HEADS):
                hs = slice(HEAD_DIM * h, HEAD_DIM * (h + 1))
                acc = acc_sc[h]
                l = acc[:, HEAD_DIM:HEAD_DIM + 1]
                oh = acc[:, :HEAD_DIM] / l
                o_ref[:, hs] = oh.astype(o_ref.dtype)
                o32_ref[:, hs] = oh
                lse_ref[:, h:h + 1] = m_sc[h] + jnp.log(l)

    ospec = pl.BlockSpec((None, tq, FOX_W), lambda b, i, j: (b, i, 0))
    return _pcall(
        body, name="fox_forward",
        out_shape=[jax.ShapeDtypeStruct((B, S, FOX_W), BF16), jax.ShapeDtypeStruct((B, S, FOX_W), F32),
                   jax.ShapeDtypeStruct((B, S, LANES), F32)],
        grid=(B, nq, nq),
        in_specs=[pl.BlockSpec((None, tq, FOX_W), lambda b, i, j: (b, i, qcol)),
                  pl.BlockSpec((None, tq, FOX_W), lambda b, i, j: (b, jnp.minimum(i, j), kcol)),
                  pl.BlockSpec((None, tq, FOX_W), lambda b, i, j: (b, jnp.minimum(i, j), vcol)),
                  pl.BlockSpec((None, 8, tq), lambda b, i, j: (b, 0, jnp.minimum(i, j)))],
        operands=[pm3, pm3, pm3, cumT],
        out_specs=[ospec, ospec, pl.BlockSpec((None, tq, LANES), lambda b, i, j: (b, i, 0))],
        scratch_shapes=[pltpu.VMEM((2, tq, tq), F32), pltpu.VMEM((2, tq, tq), BF16), pltpu.VMEM((2, tq, 1), F32),
                        pltpu.VMEM((FOX_HEADS, tq, 1), F32), pltpu.VMEM((FOX_HEADS, tq, LANES), F32)],
        params=_params(("parallel", "parallel", "arbitrary")), rider=rider)


def _fox_bwd(pm3, do, delta, lse, cumT, qcol, kcol, vcol, tq, rider=None):
    B, S, _ = pm3.shape
    nq = S // tq
    strips = range(0, tq, FOX_STRIP)

    def body(q_ref, k_ref, v_ref, do_ref, dl_ref, lse_ref, ck_ref, dq_ref, rs_ref, dk_ref, dv_ref, cs_ref,
             s_sc, dp_sc, p_sc, ds_sc, dq_sc, dk_sc, dv_sc):
        kj, qi = pl.program_id(1), pl.program_id(2)

        @pl.when((kj == 0) & (qi == 0))
        def _():
            dq_sc[...] = jnp.zeros_like(dq_sc)

        @pl.when(qi == 0)
        def _():
            dk_sc[...] = jnp.zeros_like(dk_sc)
            dv_sc[...] = jnp.zeros_like(dv_sc)

        def tile(diagonal):
            qrows = pl.ds(pl.multiple_of(qi * tq, tq), tq)
            for h in range(FOX_HEADS):
                hs = slice(HEAD_DIM * h, HEAD_DIM * (h + 1))
                qh, kh, doh = q_ref[:, hs] * SCALE, k_ref[:, hs], do_ref[:, hs]
                b = h % 2
                s_sc[b] = lax.dot_general(qh, kh, NT, preferred_element_type=F32)
                dp_sc[b] = lax.dot_general(doh, v_ref[:, hs], NT, preferred_element_type=F32)
                ck = ck_ref[h:h + 1, :]
                for r in strips:
                    rows = slice(r, r + FOX_STRIP)
                    s = s_sc[b, rows, :] - ck
                    if diagonal:
                        s = _causal_strip(s, r)
                    p = jnp.exp(s - lse_ref[rows, h:h + 1])
                    p_sc[b, rows, :] = p.astype(BF16)
                    ds_sc[b, rows, :] = (p * (dp_sc[b, rows, :] - dl_ref[rows, h:h + 1])).astype(BF16)
                dv_sc[h] += lax.dot_general(doh, p_sc[b], TN, preferred_element_type=F32)
                dk_sc[h] += lax.dot_general(_with_ones(qh), ds_sc[b], TN, preferred_element_type=F32)
                dq_sc[h, qrows, :] += lax.dot_general(ds_sc[b], _with_ones(kh), NN, preferred_element_type=F32)

        @pl.when(qi > kj)
        def _():
            tile(False)

        @pl.when(qi == kj)
        def _():
            tile(True)

        @pl.when(qi == nq - 1)
        def _():
            cs_ref[...] = jnp.zeros_like(cs_ref)
            for h in range(FOX_HEADS):
                hs = slice(HEAD_DIM * h, HEAD_DIM * (h + 1))
                dv_ref[:, hs] = dv_sc[h].T.astype(dv_ref.dtype)
                dk = dk_sc[h].T
                dk_ref[:, hs] = dk[:, :HEAD_DIM].astype(dk_ref.dtype)
                cs_ref[:, h:h + 1] = dk[:, HEAD_DIM:HEAD_DIM + 1]

        @pl.when((kj == nq - 1) & (qi == nq - 1))
        def _():
            rs_ref[...] = jnp.zeros_like(rs_ref)
            for h in range(FOX_HEADS):
                hs = slice(HEAD_DIM * h, HEAD_DIM * (h + 1))
                dq_ref[:, hs] = (dq_sc[h, :, :HEAD_DIM] * SCALE).astype(dq_ref.dtype)
                rs_ref[:, h:h + 1] = dq_sc[h, :, HEAD_DIM:HEAD_DIM + 1]

    def qside(width, col=0):
        return pl.BlockSpec((None, tq, width), lambda b, j, i: (b, jnp.maximum(i, j), col))

    kspec = pl.BlockSpec((None, tq, FOX_W), lambda b, j, i: (b, j, 0))
    return _pcall(
        body, name="fox_backward",
        out_shape=[jax.ShapeDtypeStruct((B, S, FOX_W), BF16), jax.ShapeDtypeStruct((B, S, LANES), F32),
                   jax.ShapeDtypeStruct((B, S, FOX_W), BF16), jax.ShapeDtypeStruct((B, S, FOX_W), BF16),
                   jax.ShapeDtypeStruct((B, S, LANES), F32)],
        grid=(B, nq, nq),
        in_specs=[qside(FOX_W, qcol),
                  pl.BlockSpec((None, tq, FOX_W), lambda b, j, i: (b, j, kcol)),
                  pl.BlockSpec((None, tq, FOX_W), lambda b, j, i: (b, j, vcol)),
                  qside(FOX_W), qside(LANES), qside(LANES),
                  pl.BlockSpec((None, 8, tq), lambda b, j, i: (b, 0, j))],
        operands=[pm3, pm3, pm3, do, delta, lse, cumT],
        out_specs=[pl.BlockSpec((None, S, FOX_W), lambda b, j, i: (b, 0, 0)),
                   pl.BlockSpec((None, S, LANES), lambda b, j, i: (b, 0, 0)),
                   kspec, kspec, pl.BlockSpec((None, tq, LANES), lambda b, j, i: (b, j, 0))],
        scratch_shapes=[pltpu.VMEM((2, tq, tq), F32), pltpu.VMEM((2, tq, tq), F32), pltpu.VMEM((2, tq, tq), BF16),
                        pltpu.VMEM((2, tq, tq), BF16), pltpu.VMEM((FOX_HEADS, S, LANES), F32),
                        pltpu.VMEM((FOX_HEADS, LANES, tq), F32), pltpu.VMEM((FOX_HEADS, HEAD_DIM, tq), F32)],
        params=_params(("parallel", "arbitrary", "arbitrary")), rider=rider)


def _fox_delta(do, o32, T, S):
    def fn(dov, ov):
        prod = dov.astype(F32) * ov
        lane = lax.broadcasted_iota(jnp.int32, (dov.shape[0], LANES), 1)
        delta = jnp.zeros((dov.shape[0], LANES), F32)
        for h in range(FOX_HEADS):
            hs = slice(HEAD_DIM * h, HEAD_DIM * (h + 1))
            delta = jnp.where(lane == h, jnp.sum(prod[:, hs], axis=-1, keepdims=True), delta)
        return [delta]

    return _rowwise("fox_delta", fn, T, _div(S, 512, 8), [(do, "row", None), (o32, "row", None)], [("row", LANES, F32)], S)[0]


def _alibi_slope(group, head):
    return 2.0 ** (-ALIBI_MAX_BIAS * (group * DIL_HPG + head + 1) / (N_DIL * DIL_HPG))


def _residue_order(a, B, S, d):
    C = a.shape[-1]
    if d == 1:
        return a.reshape(B, S, C)
    return a.reshape(B, S // d, d, C).transpose(0, 2, 1, 3).reshape(B * d, S // d, C)


def _token_order(a, B, S, d):
    C = a.shape[-1]
    if d == 1:
        return a.reshape(B * S, C)
    return a.reshape(B, d, S // d, C).transpose(0, 2, 1, 3).reshape(B * S, C)


def _band_scores(qh, kcat, slope_d, has_prev):
    qi = lax.broadcasted_iota(jnp.int32, (BLOCK, 2 * BLOCK), 0)
    c = lax.broadcasted_iota(jnp.int32, (BLOCK, 2 * BLOCK), 1)
    s = lax.dot_general(qh, kcat, NT, preferred_element_type=F32) - slope_d * (BLOCK + qi - c).astype(F32)
    valid = (c >= qi) & (c <= qi + BLOCK)
    if has_prev is not None:
        valid = valid & ((c >= BLOCK) | has_prev)
    return jnp.where(valid, s, NEG_INF)


def _band_operands(j, cur_ref, prev_ref, hs):
    if j == 0:
        return jnp.concatenate([prev_ref[:, hs], cur_ref[0:BLOCK, hs]], axis=0)
    return cur_ref[(j - 1) * BLOCK:(j + 1) * BLOCK, hs]


def _dil_specs(Ls, qb, cols):
    nsub = qb // BLOCK
    qcol, kcol, vcol = cols

    def cur(col):
        return pl.BlockSpec((None, qb, DIL_GW), lambda s, n: (s, n, col))

    def prev(col):
        return pl.BlockSpec((None, BLOCK, DIL_GW), lambda s, n: (s, jnp.maximum(n * nsub - 1, 0), col))

    return [cur(qcol), cur(kcol), prev(kcol), cur(vcol), prev(vcol)]


def _dil_fwd(group, src, cols):
    _, dilation = DIL_GROUPS[group]
    nseq, Ls, _ = src.shape
    qb = _div(Ls, 512, BLOCK)
    nsub = qb // BLOCK

    def body(q_ref, kc_ref, kp_ref, vc_ref, vp_ref, o_ref, lse_ref):
        has_prev = pl.program_id(1) > 0
        lse_ref[...] = jnp.zeros_like(lse_ref)
        for h in range(DIL_HPG):
            hs = slice(HEAD_DIM * h, HEAD_DIM * (h + 1))
            scores = [_band_scores(q_ref[j * BLOCK:(j + 1) * BLOCK, hs] * SCALE, _band_operands(j, kc_ref, kp_ref, hs),
                                   _alibi_slope(group, h) * dilation, has_prev if j == 0 else None) for j in range(nsub)]
            pending = None

            def write(j, m, acc):
                rows = slice(j * BLOCK, (j + 1) * BLOCK)
                l = acc[:, HEAD_DIM:HEAD_DIM + 1]
                o_ref[rows, hs] = acc[:, :HEAD_DIM] / l
                lse_ref[rows, h:h + 1] = m + jnp.log(l)

            for j in range(nsub):
                m = jnp.max(scores[j], axis=-1, keepdims=True)
                p = jnp.exp(scores[j] - m).astype(BF16)
                acc = lax.dot_general(p, _with_ones(_band_operands(j, vc_ref, vp_ref, hs)), NN, preferred_element_type=F32)
                if pending is not None:
                    write(*pending)
                pending = (j, m, acc)
            write(*pending)

    return pl.pallas_call(
        body, name=f"dil_forward_{group}",
        out_shape=[jax.ShapeDtypeStruct((nseq, Ls, DIL_GW), F32), jax.ShapeDtypeStruct((nseq, Ls, LANES), F32)],
        grid=(nseq, Ls // qb),
        in_specs=_dil_specs(Ls, qb, cols),
        out_specs=[pl.BlockSpec((None, qb, DIL_GW), lambda s, n: (s, n, 0)),
                   pl.BlockSpec((None, qb, LANES), lambda s, n: (s, n, 0))],
        compiler_params=_params(("parallel", "arbitrary")),
    )(src, src, src, src, src)


def _dil_bwd(group, src, cols, Lr, dyr, dlr):
    _, dilation = DIL_GROUPS[group]
    nseq, Ls, _ = src.shape
    qb = _div(Ls, 512, BLOCK)
    nsub, nb = qb // BLOCK, Ls // qb

    def body(q_ref, kc_ref, kp_ref, vc_ref, vp_ref, L_ref, dy_ref, dl_ref, dq_ref, dk_ref, dv_ref, dk_sc, dv_sc):
        n = pl.program_id(1)
        has_prev = n > 0

        @pl.when(n == 0)
        def _():
            dk_sc[...] = jnp.zeros_like(dk_sc)
            dv_sc[...] = jnp.zeros_like(dv_sc)

        base = pl.multiple_of(n * qb, BLOCK)
        for h in range(DIL_HPG):
            hs = slice(HEAD_DIM * h, HEAD_DIM * (h + 1))
            blocks = [slice(j * BLOCK, (j + 1) * BLOCK) for j in range(nsub)]
            qhs = [q_ref[rows, hs] * SCALE for rows in blocks]
            kcats = [_band_operands(j, kc_ref, kp_ref, hs) for j in range(nsub)]
            dyhs = [dy_ref[rows, hs] for rows in blocks]
            scores = [_band_scores(qhs[j], kcats[j], _alibi_slope(group, h) * dilation, has_prev if j == 0 else None)
                      for j in range(nsub)]
            dps = [lax.dot_general(dyhs[j], _band_operands(j, vc_ref, vp_ref, hs), NT, preferred_element_type=F32)
                   for j in range(nsub)]
            pending = None

            def write(j, dq, dk, dv):
                dq_ref[blocks[j], hs] = (dq * SCALE).astype(dq_ref.dtype)
                win = pl.ds(base + j * BLOCK, 2 * BLOCK)
                dk_sc[win, hs] += dk
                dv_sc[win, hs] += dv

            for j in range(nsub):
                p = jnp.exp(scores[j] - L_ref[blocks[j], h:h + 1])
                ds = (p * (dps[j] - dl_ref[blocks[j], h:h + 1])).astype(BF16)
                dq = lax.dot_general(ds, kcats[j], NN, preferred_element_type=F32)
                dk = lax.dot_general(ds, qhs[j], TN, preferred_element_type=F32)
                dv = lax.dot_general(p.astype(BF16), dyhs[j], TN, preferred_element_type=F32)
                if pending is not None:
                    write(*pending)
                pending = (j, dq, dk, dv)
            write(*pending)

        @pl.when(n == nb - 1)
        def _():
            dk_ref[...] = dk_sc[BLOCK:, :].astype(dk_ref.dtype)
            dv_ref[...] = dv_sc[BLOCK:, :].astype(dv_ref.dtype)

    own = pl.BlockSpec((None, qb, DIL_GW), lambda s, n: (s, n, 0))
    own128 = pl.BlockSpec((None, qb, LANES), lambda s, n: (s, n, 0))
    whole = pl.BlockSpec((None, Ls, DIL_GW), lambda s, n: (s, 0, 0))
    shape = jax.ShapeDtypeStruct((nseq, Ls, DIL_GW), BF16)
    return pl.pallas_call(
        body, name=f"dil_backward_{group}",
        out_shape=[shape, shape, shape],
        grid=(nseq, nb),
        in_specs=_dil_specs(Ls, qb, cols) + [own128, own, own128],
        out_specs=[own, whole, whole],
        scratch_shapes=[pltpu.VMEM((Ls + BLOCK, DIL_GW), F32), pltpu.VMEM((Ls + BLOCK, DIL_GW), F32)],
        compiler_params=_params(("parallel", "arbitrary")),
    )(src, src, src, src, src, Lr, dyr, dlr)


def _dil_combine(os_, lses, T, S):
    def fn(o0, o1, o2, l0, l1, l2):
        m = jnp.maximum(jnp.maximum(l0, l1), l2)
        e0, e1, e2 = jnp.exp(l0 - m), jnp.exp(l1 - m), jnp.exp(l2 - m)
        tot = e0 + e1 + e2
        w0, w1, w2 = e0 / tot, e1 / tot, e2 / tot
        parts = []
        for h in range(DIL_HPG):
            hs = slice(HEAD_DIM * h, HEAD_DIM * (h + 1))
            parts.append(w0[:, h:h + 1] * o0[:, hs] + w1[:, h:h + 1] * o1[:, hs] + w2[:, h:h + 1] * o2[:, hs])
        return [jnp.concatenate(parts, axis=1), m + jnp.log(tot)]

    ins = [(a, "row", None) for a in os_] + [(a, "row", None) for a in lses]
    return _rowwise("dil_combine", fn, T, _div(S, 512, 8), ins, [("row", DIL_GW, BF16), ("row", LANES, F32)], S)


def _dil_delta(dy, y, T, S):
    def fn(dyv, yv):
        prod = dyv * yv.astype(F32)
        lane = lax.broadcasted_iota(jnp.int32, (dyv.shape[0], LANES), 1)
        delta = jnp.zeros((dyv.shape[0], LANES), F32)
        for h in range(DIL_HPG):
            hs = slice(HEAD_DIM * h, HEAD_DIM * (h + 1))
            delta = jnp.where(lane == h, jnp.sum(prod[:, hs], axis=-1, keepdims=True), delta)
        return [delta, dyv]

    return _rowwise("dil_delta", fn, T, _div(S, 512, 8), [(dy, "row", None), (y, "row", None)],
                    [("row", LANES, F32), ("row", DIL_GW, BF16)], S)


def _ada_forward(c_all, w, b):
    n, D = c_all.shape
    cl = w.shape[1]

    def body(c_ref, w_ref, b_ref, o_ref, ca_ref):
        cv = c_ref[...]
        ca = (cv * _sigmoid(cv)).astype(BF16)
        ca_ref[...] = ca
        o_ref[...] = jnp.dot(ca, w_ref[...].astype(BF16), preferred_element_type=F32) + b_ref[...]

    return pl.pallas_call(
        body, name="ada_forward",
        out_shape=[jax.ShapeDtypeStruct((n, cl), F32), jax.ShapeDtypeStruct((n, D), BF16)],
        compiler_params=_params(),
    )(c_all, w, b)


def _ada_backward(ca, dmod_cols, dmod_all):
    n, D = ca.shape
    cl = dmod_cols.shape[1]

    def body(ca_ref, dc_ref, da_ref, gw_ref, gb_ref):
        gw_ref[...] = lax.dot_general(ca_ref[...], dc_ref[...].astype(BF16), (((0,), (0,)), ((), ())), preferred_element_type=F32)
        gb_ref[...] = _colsum(da_ref[...])

    return pl.pallas_call(
        body, name="ada_backward",
        out_shape=[jax.ShapeDtypeStruct((D, cl), F32), jax.ShapeDtypeStruct((1, dmod_all.shape[1]), F32)],
        compiler_params=_params(),
    )(ca, dmod_cols, dmod_all)


def _sum_devices(v):
    def body(v_ref, o_ref):
        tot = v_ref[0]
        for k in range(1, N_DEV):
            tot = tot + v_ref[k]
        o_ref[...] = tot

    return pl.pallas_call(body, name="sum_devices", out_shape=jax.ShapeDtypeStruct(v.shape[1:], F32))(v)


def _adamw(name, w, g, m, v):
    rows, cols = w.shape
    tr = _div(rows, 256, 8)

    def body(w_ref, g_ref, m_ref, v_ref, d_ref, nm_ref, nv_ref):
        gv = g_ref[...]
        nm = ADAM_B1 * m_ref[...] + (1.0 - ADAM_B1) * gv
        nv = ADAM_B2 * v_ref[...] + (1.0 - ADAM_B2) * (gv * gv)
        m_hat = nm / (1.0 - ADAM_B1 ** ADAM_STEP)
        v_hat = nv / (1.0 - ADAM_B2 ** ADAM_STEP)
        d_ref[...] = -ADAM_LR * (m_hat / (jnp.sqrt(v_hat) + ADAM_EPS) + ADAM_WD * w_ref[...])
        nm_ref[...] = nm
        nv_ref[...] = nv

    spec = pl.BlockSpec((tr, cols), lambda i: (i, 0))
    shape = jax.ShapeDtypeStruct((rows, cols), F32)
    return pl.pallas_call(
        body, name=name, out_shape=[shape, shape, shape], grid=(rows // tr,),
        in_specs=[spec, spec, spec, spec], out_specs=[spec, spec, spec],
        compiler_params=_params(("arbitrary",)),
    )(w, g, m, v)


def _pad_rows(a, rows):
    return a if a.shape[0] == rows else jnp.pad(a, ((0, rows - a.shape[0]), (0, 0)))


class _Packed:
    def __init__(self, kind, local_shape, D):
        self.kind, self.local_shape, self.D = kind, local_shape, D
        r, c = local_shape
        self.rows = {"T": c, "N": r, "F": r * c // D}[kind]
        self.rows_pad = -(-self.rows // ROW_ALIGN) * ROW_ALIGN

    def pack_local(self, w):
        if self.kind == "T":
            w = w.T
        elif self.kind == "F":
            w = w.reshape(self.rows, self.D)
        return _pad_rows(w, self.rows_pad)

    def full(self, gathered):
        g = gathered[:, :self.rows]
        if self.kind == "F":
            r, c = self.local_shape
            return g.reshape(N_DEV, r, c).transpose(1, 0, 2).reshape(r, N_DEV * c)
        return g.reshape(N_DEV * self.rows, self.D)

    def pack_grad(self, gfull):
        if self.kind == "F":
            r, c = self.local_shape
            g = gfull.reshape(r, N_DEV, c).transpose(1, 0, 2).reshape(N_DEV, self.rows, self.D)
        else:
            g = gfull.reshape(N_DEV, self.rows, self.D)
        if self.rows_pad != self.rows:
            g = jnp.pad(g, ((0, 0), (0, self.rows_pad - self.rows), (0, 0)))
        return g

    def unpack_local(self, g):
        g = g[:self.rows]
        if self.kind == "T":
            return g.T
        if self.kind == "F":
            return g.reshape(self.local_shape)
        return g


BIG = ["ffn1_w_gate", "ffn1_w_up", "ffn1_w_down", "w_in", "w_branch_a", "w_branch_b", "w_out",
       "ffn2_w_gate", "ffn2_w_up", "ffn2_w_down"]
BIG_KIND = {"ffn1_w_gate": "T", "ffn1_w_up": "T", "ffn1_w_down": "N", "w_in": "T", "w_branch_a": "F", "w_branch_b": "F",
            "w_out": "N", "ffn2_w_gate": "T", "ffn2_w_up": "T", "ffn2_w_down": "N"}
GROUPS = (("ffn1_w_gate", "ffn1_w_up", "ffn1_w_down"), ("w_in", "w_branch_a", "w_branch_b", "w_out"),
          ("ffn2_w_gate", "ffn2_w_up", "ffn2_w_down"))
SMALL = ["ada_b", "norm_ffn1", "norm_mix", "forget_bias", "norm_ffn2", "norm_final"]


def kernel(x, c, ada_w, ada_b, norm_ffn1, ffn1_w_gate, ffn1_w_up, ffn1_w_down, norm_mix, w_in, forget_bias, w_branch_a, w_branch_b, w_out, norm_ffn2, ffn2_w_gate, ffn2_w_up, ffn2_w_down, norm_final, loss_target, m_ada_w, m_ada_b, m_norm_ffn1, m_ffn1_w_gate, m_ffn1_w_up, m_ffn1_w_down, m_norm_mix, m_w_in, m_forget_bias, m_w_branch_a, m_w_branch_b, m_w_out, m_norm_ffn2, m_ffn2_w_gate, m_ffn2_w_up, m_ffn2_w_down, m_norm_final, v_ada_w, v_ada_b, v_norm_ffn1, v_ffn1_w_gate, v_ffn1_w_up, v_ffn1_w_down, v_norm_mix, v_w_in, v_forget_bias, v_w_branch_a, v_w_branch_b, v_w_out, v_norm_ffn2, v_ffn2_w_gate, v_ffn2_w_up, v_ffn2_w_down, v_norm_final):
    args = dict(locals())
    B, S, D = x.shape
    T = B * S
    cl = ada_w.shape[2]
    n_in = w_in.shape[2] * N_DEV
    nm = 2 * D + 3 * FOX_W + 3 * DIL_W
    nmp = -(-nm // 512) * 512
    GA, GB, QB, QA = 0, D, 2 * D, 2 * D + 3 * FOX_W
    xpos, ypos, cpos = _position()
    me = 4 * xpos + 2 * ypos + cpos

    packs = {n: _Packed(BIG_KIND[n], args[n].shape[1:], D) for n in BIG}
    offs, pads = {}, {}
    for names in GROUPS:
        r = 0
        for n in names:
            offs[n] = r
            r += packs[n].rows_pad
        pads[names] = -r % PACK_ROW_QUANTUM

    def pack_weights(names):
        return jnp.concatenate([packs[n].pack_local(args[n][0]).astype(BF16) for n in names]
                               + [jnp.zeros((pads[names], D), BF16)], axis=0)

    def unpack_weights(names, land):
        out = {}
        for n in names:
            p = packs[n]
            if p.kind in "TN" and p.rows == p.rows_pad and offs[n] % p.rows == 0:
                out[n] = _Slab(land, offs[n], p.rows)
            else:
                out[n] = p.full(land[:, offs[n]:offs[n] + p.rows_pad])
        return out

    def pack_grads(names, gfull):
        return jnp.concatenate([packs[n].pack_grad(gfull[n]) for n in names] + [jnp.zeros((N_DEV, pads[names], D), F32)], axis=1)

    def unpack_grads(names, g_local):
        return {n: packs[n].unpack_local(g_local[offs[n]:offs[n] + packs[n].rows_pad])[None] for n in names}

    W = unpack_weights(GROUPS[0], _weight_allgather(pack_weights(GROUPS[0])))

    c_all = _small_allgather(c, "gather_c").reshape(N_DEV * B, D)
    b_cols = lax.dynamic_slice(ada_b, (0, me * cl), (1, cl))
    mod_cols, c_act = _ada_forward(c_all, ada_w[0], b_cols)
    mod_all = _small_allgather(mod_cols, "gather_mod").transpose(1, 0, 2).reshape(N_DEV * B, N_MOD * D)
    mod = lax.dynamic_slice(mod_all, (me * B, 0), (B, N_MOD * D)).reshape(B, N_MOD, 1, D)
    sh1, sc1, gt1, sh2, sc2, gt2, sh3, sc3, gt3 = [mod[:, i] for i in range(N_MOD)]

    x0 = x.reshape(T, D)
    x1, saved1, land = _ffn_forward("ffn1", x0, norm_ffn1, sh1, sc1, gt1, W["ffn1_w_gate"], W["ffn1_w_up"], W["ffn1_w_down"], S,
                                    gather=pack_weights(GROUPS[1]))
    W.update(unpack_weights(GROUPS[1], land))
    winT = W["w_in"]
    o_f = 3 * DIL_W + 3 * FOX_W
    wmT = jnp.concatenate([winT[o_f + 8:], winT[3 * DIL_W:o_f], winT[:3 * DIL_W], jnp.zeros((nmp - nm, D), BF16)], axis=0)
    wfT = jnp.concatenate([winT[o_f:o_f + 8], jnp.zeros((LANES - 8, D), BF16)], axis=0)

    tm1k = _div(T, 1024, 8)
    fb = jnp.pad(forget_bias, ((0, 0), (0, LANES - FOX_HEADS)))

    def proj(accs, fbv):
        fl = accs[1] + fbv
        lane = lax.broadcasted_iota(jnp.int32, fl.shape, 1)
        ls = jnp.minimum(fl, 0.0) - jnp.log(1.0 + jnp.exp(-jnp.abs(fl)))
        return [accs[0], jnp.where(lane < FOX_HEADS, ls, 0.0), fl]

    tms = _div(S, 512, 8)
    h2, pm, logsig, flog, land = _norm_matmul("mix_proj", x1, norm_mix, sc2, sh2, [wmT, wfT], proj,
                                              [(nmp, BF16), (LANES, F32), (LANES, F32)], S, vecs=[fb],
                                              rider=_gather_direct_rider(pack_weights(GROUPS[2])))
    cum = _cumsum(logsig.reshape(B, S, LANES))
    cumT = cum[:, :, :8].transpose(0, 2, 1)
    pm3 = pm.reshape(B, S, nmp)
    tq = _div(S, 512, LANES)
    qcol, kcol, vcol = QB // FOX_W, QB // FOX_W + 1, QB // FOX_W + 2
    o_b, o_b32, lse_b, land = _fox_fwd(pm3, cumT, qcol, kcol, vcol, tq, rider=_gather_forward_rider(land))
    W.update(unpack_weights(GROUPS[2], land))
    y_b = o_b.reshape(T, FOX_W)

    qa_blk = QA // DIL_GW
    dil_src, dil_cols = [], []
    for g, (_, d) in enumerate(DIL_GROUPS):
        if d == 1:
            dil_src.append(pm3)
            dil_cols.append((qa_blk + g, qa_blk + N_DIL + g, qa_blk + 2 * N_DIL + g))
        else:
            starts = [QA + (i * N_DIL + g) * DIL_GW for i in range(3)]
            qkv = jnp.concatenate([pm[:, c:c + DIL_GW] for c in starts], axis=1)
            dil_src.append(_residue_order(qkv, B, S, d))
            dil_cols.append((0, 1, 2))
    dil_o, dil_lse = [], []
    for g, (_, d) in enumerate(DIL_GROUPS):
        o_g, lse_g = _dil_fwd(g, dil_src[g], dil_cols[g])
        dil_o.append(_token_order(o_g, B, S, d))
        dil_lse.append(_token_order(lse_g, B, S, d))
    y_a, L_a = _dil_combine(dil_o, dil_lse, T, S)

    wa, wb, wout = W["w_branch_a"], W["w_branch_b"], W["w_out"]
    tnd = D
    tm5 = _div(T, 512, 8)
    yap = _matmul("mix_branch_a", "nn", [[(y_a, wa)]], T, D, DIL_GW, tm5, tnd, DIL_GW, [BF16])[0]

    def merge(accs, ex):
        yapv, gav, gbv = ex
        ybp = accs[0]
        return [ybp, _sigmoid(gav.astype(F32)) * yapv.astype(F32) + _sigmoid(gbv.astype(F32)) * ybp]

    ybp, merged = _matmul("mix_branch_b", "nn", [[(y_b, wb)]], T, D, FOX_W, tm5, tnd, FOX_W, [BF16, BF16],
                          extras=[(yap, "tile", 0), (pm, "tile", GA), (pm, "tile", GB)], epilogue=merge)

    def out_proj(accs, ex):
        xv, gtv = ex
        return [xv + gtv * accs[0], accs[0]]

    x2, ymix = _matmul("mix_out", "nn", [[(merged, wout)]], T, D, D, tms, tnd, D, [F32, BF16],
                       extras=[(x1, "tile", 0), (gt2, "brow", 0)], epilogue=out_proj, rows_per_example=S)

    x3, saved3, _ = _ffn_forward("ffn2", x2, norm_ffn2, sh3, sc3, gt3, W["ffn2_w_gate"], W["ffn2_w_up"], W["ffn2_w_down"], S)

    dx3, loss_b, dg_final = _loss_head(x3, loss_target.reshape(T, D), norm_final.reshape(1, D), S)
    dx2, (dsh3, dsc3, dgt3, dg3), (dwg2, dwu2, dwd2), _ = _ffn_backward(
        "ffn2", dx3, saved3, norm_ffn2, sc3, gt3, W["ffn2_w_gate"], W["ffn2_w_up"], W["ffn2_w_down"], S)
    g3 = pack_grads(GROUPS[2], {"ffn2_w_gate": dwg2, "ffn2_w_up": dwu2, "ffn2_w_down": dwd2})

    def merge_grad(dm, ex):
        gav, gbv, yapv, ybpv = [e.astype(F32) for e in ex]
        sga, sgb = _sigmoid(gav), _sigmoid(gbv)
        return [dm * sga, dm * sgb, dm * yapv * sga * (1.0 - sga), dm * ybpv * sgb * (1.0 - sgb)]

    dym, dgt2, dyap, dybp, dga, dgb, sib3 = _gated_grad_matmul(
        "mix_merge_grad", dx2, ymix, gt2, 1.0, wout, [(pm, GA // D), (pm, GB // D), (yap, 0), (ybp, 0)], merge_grad, [BF16] * 4, S,
        rider=_sibling_exchange_rider(g3))
    sums3, own3 = _chip_sums(g3, sib3)
    tkw = _div(T, 512, LANES)
    dwout = _matmul("mix_dw_out", "tn", [[(merged, dym)]], D, D, T, D, D, tkw, [F32])[0]
    dwa = _matmul("mix_dw_a", "tn", [[(y_a, dyap)]], DIL_GW, D, T, DIL_GW, D, tkw, [F32])[0]
    dwb = _matmul("mix_dw_b", "tn", [[(y_b, dybp)]], FOX_W, D, T, FOX_W, D, tkw, [F32])[0]
    dy_a = _matmul("mix_dy_a", "nt", [[(dyap, wa)]], T, DIL_GW, D, tm1k, DIL_GW, D, [F32])[0]
    dy_b = _matmul("mix_dy_b", "nt", [[(dybp, wb)]], T, FOX_W, D, tm1k, FOX_W, D, [BF16])[0]

    do3 = dy_b.reshape(B, S, FOX_W)
    delta_b = _fox_delta(dy_b, o_b32.reshape(T, FOX_W), T, S).reshape(B, S, LANES)
    dq_b, ds_rows, dk_b, dv_b, ds_cols, recv3 = _fox_bwd(pm3, do3, delta_b, lse_b, cumT, qcol, kcol, vcol, tq,
                                                         rider=_chip_exchange_rider(sums3))

    delta_a, dy_a16 = _dil_delta(dy_a, y_a, T, S)
    dqs, dks, dvs = [], [], []
    for g, (_, d) in enumerate(DIL_GROUPS):
        dq_g, dk_g, dv_g = _dil_bwd(g, dil_src[g], dil_cols[g], _residue_order(L_a, B, S, d),
                                    _residue_order(dy_a16, B, S, d), _residue_order(delta_a, B, S, d))
        dqs.append(_token_order(dq_g, B, S, d))
        dks.append(_token_order(dk_g, B, S, d))
        dvs.append(_token_order(dv_g, B, S, d))

    dcum = ds_rows - ds_cols
    dcum_run = _cumsum(dcum)
    dcum_tot = dcum_run[:, S - 1:S, :]

    def forget_grad_fn(run, dcv, fl, tot):
        lane = lax.broadcasted_iota(jnp.int32, fl.shape, 1)
        df = jnp.where(lane < FOX_HEADS, (tot - run + dcv) * _sigmoid_exp(-fl), 0.0)
        return [df, _colsum(df)]

    df16, dfb = _rowwise("forget_gate_grad", forget_grad_fn, T, tms,
                         [(dcum_run.reshape(T, LANES), "row", None), (dcum.reshape(T, LANES), "row", None), (flog, "row", None),
                          (dcum_tot, "bvec", None)],
                         [("row", LANES, BF16), ("bacc", LANES, F32)], S)

    dpm = jnp.concatenate([dga, dgb, dq_b.reshape(T, FOX_W), dk_b.reshape(T, FOX_W), dv_b.reshape(T, FOX_W)]
                          + dqs + dks + dvs + ([jnp.zeros((T, nmp - nm), BF16)] if nmp > nm else []), axis=1)
    tmn = _div(nmp, 2048, LANES)
    dwmT = _matmul("mix_dw_in", "tn", [[(dpm, h2)]], nmp, D, T, tmn, D, tkw, [F32])[0]
    dwfT = _matmul("mix_dw_f", "tn", [[(df16, h2)]], LANES, D, T, LANES, D, tkw, [F32])[0]
    dwinT = jnp.concatenate([dwmT[QA:QA + 3 * DIL_W], dwmT[QB:QB + 3 * FOX_W], dwfT[:8], dwmT[GA:2 * D]], axis=0)
    g2 = pack_grads(GROUPS[1], {"w_in": dwinT, "w_branch_a": dwa, "w_branch_b": dwb, "w_out": dwout})
    dx1, dsh2, dsc2, dgmix, sib2 = _matmul_normmod_bwd("mix_dh", [(dpm, wmT), (df16, wfT)], x1, norm_mix, sc2, dx2, S,
                                                       rider=_sibling_exchange_rider(g2))
    sums2, own2 = _chip_sums(g2, sib2)

    own1 = []

    def ffn1_exchange(dwg, dwu, dwd):
        sums1, own = _chip_sums(pack_grads(GROUPS[0], {"ffn1_w_gate": dwg, "ffn1_w_up": dwu, "ffn1_w_down": dwd}))
        own1.append(own)
        return _chip_exchange_rider(sums1)

    dx0, (dsh1, dsc1, dgt1, dg1), _, (recv2, recv1) = _ffn_backward(
        "ffn1", dx1, saved1, norm_ffn1, sc1, gt1, W["ffn1_w_gate"], W["ffn1_w_up"], W["ffn1_w_down"], S,
        rider=_chip_exchange_rider(sums2), dh_rider=ffn1_exchange)
    own1 = own1[0]
    grad_x = dx0.reshape(B, S, D)

    dmod = jnp.concatenate([dsh1, dsc1, dgt1, dsh2, dsc2, dgt2, dsh3, dsc3, dgt3], axis=1).reshape(B, N_MOD * D)
    dmod_all = _small_allgather(dmod, "gather_dmod").reshape(N_DEV * B, N_MOD * D)
    dmod_cols = lax.dynamic_slice(dmod_all, (0, me * cl), (N_DEV * B, cl))
    g_ada_w, g_ada_b = _ada_backward(c_act, dmod_cols, dmod_all)

    fbg = jnp.sum(dfb, axis=0)
    small = jnp.concatenate([jnp.sum(dg1, axis=0), jnp.sum(dgmix, axis=0), jnp.sum(dg3, axis=0), jnp.sum(dg_final, axis=0),
                             fbg, jnp.sum(loss_b, axis=0)], axis=1)
    small = _sum_devices(_small_allgather(small, "gather_small"))
    g_small = {"norm_ffn1": small[:, 0:D], "norm_mix": small[:, D:2 * D], "norm_ffn2": small[:, 2 * D:3 * D],
               "norm_final": small[:, 3 * D:4 * D], "forget_bias": small[:, 4 * D:4 * D + FOX_HEADS], "ada_b": g_ada_b}
    loss = small[0, 4 * D + LANES]

    grads = {"ada_w": g_ada_w[None]}
    for names, own, recv in ((GROUPS[0], own1, recv1), (GROUPS[1], own2, recv2), (GROUPS[2], own3, recv3)):
        grads.update(unpack_grads(names, _final_grad_sum(own, recv)))

    delta, new_m, new_v = {}, {}, {}
    for n in ["ada_w"] + BIG:
        shp = args[n].shape
        d_, m_, v_ = _adamw(f"adamw_{n}", args[n][0], grads[n][0], args["m_" + n][0], args["v_" + n][0])
        delta[n], new_m[n], new_v[n] = d_.reshape(shp), m_.reshape(shp), v_.reshape(shp)
    sizes = [args[n].size for n in SMALL]
    tot = sum(sizes)
    padded = -(-tot // (8 * LANES)) * (8 * LANES)

    def flat(get):
        v = jnp.concatenate([get(n).reshape(-1) for n in SMALL])
        return jnp.pad(v, (0, padded - tot)).reshape(8, padded // 8)

    d_s, m_s, v_s = _adamw("adamw_small", flat(lambda n: args[n]), flat(lambda n: g_small[n]), flat(lambda n: args["m_" + n]),
                           flat(lambda n: args["v_" + n]))
    o = 0
    for n, sz in zip(SMALL, sizes):
        shp = args[n].shape
        grads[n] = g_small[n].reshape(shp)
        delta[n] = d_s.reshape(-1)[o:o + sz].reshape(shp)
        new_m[n] = m_s.reshape(-1)[o:o + sz].reshape(shp)
        new_v[n] = v_s.reshape(-1)[o:o + sz].reshape(shp)
        o += sz

    order = ["ada_w", "ada_b", "norm_ffn1", "ffn1_w_gate", "ffn1_w_up", "ffn1_w_down", "norm_mix", "w_in", "forget_bias",
             "w_branch_a", "w_branch_b", "w_out", "norm_ffn2", "ffn2_w_gate", "ffn2_w_up", "ffn2_w_down", "norm_final"]
    return (loss, grad_x, *[grads[n] for n in order], *[delta[n] for n in order], *[new_m[n] for n in order],
            *[new_v[n] for n in order])
```

```python
import functools
import math

import jax
import jax.numpy as jnp
from jax import lax
from jax.experimental import pallas as pl
from jax.experimental.pallas import tpu as pltpu

F32 = jnp.float32
BF16 = jnp.bfloat16
MESH = pl.DeviceIdType.MESH
ANY = pl.BlockSpec(memory_space=pl.ANY)
VMEM_SPEC = pl.BlockSpec(memory_space=pltpu.VMEM)

N_DEV = 8
HEAD_DIM = 64
BLOCK = 128
DIL_GROUPS = ((128, 1), (512, 4), (2048, 16))
N_DIL = len(DIL_GROUPS)
DIL_HPG = 4
DIL_GW = DIL_HPG * HEAD_DIM
DIL_W = N_DIL * DIL_GW
FOX_HEADS = 8
FOX_W = FOX_HEADS * HEAD_DIM
N_MOD = 9
RMS_EPS = 1e-6
ALIBI_MAX_BIAS = 8.0
NEG_INF = -1e30
ADAM_LR, ADAM_B1, ADAM_B2, ADAM_EPS, ADAM_WD, ADAM_STEP = 0.001, 0.9, 0.999, 1e-08, 0.01, 10
V7X_VMEM_LIMIT = 52 * 1024 * 1024
LANES = 128
ROW_ALIGN = 16
PACK_ROW_QUANTUM = 32
FOX_STRIP = 32
SCALE = 1.0 / math.sqrt(HEAD_DIM)


def _div(dim, target, quantum):
    best = None
    for t in range(quantum, min(dim, target) + 1, quantum):
        if dim % t == 0:
            best = t
    return best or dim


def _params(sem=None):
    return pltpu.CompilerParams(dimension_semantics=sem, vmem_limit_bytes=V7X_VMEM_LIMIT)


def _sigmoid(x):
    return 0.5 * jnp.tanh(0.5 * x) + 0.5


def _sigmoid_exp(x):
    return 1.0 / (1.0 + jnp.exp(-x))


def _position():
    x, y, c = lax.axis_index("x"), lax.axis_index("y"), lax.axis_index("c")
    return x, y, c


def _small_allgather(v, name):
    rows, cols = v.shape

    def body(v_ref, out_ref, send_sems, recv_sems):
        x, y, c = _position()
        me = 4 * x + 2 * y + c
        out_ref[me] = v_ref[...]

        def peer(k):
            return (1 - x if k & 4 else x, 1 - y if k & 2 else y, 1 - c if k & 1 else c)

        def copy(k, slot):
            return pltpu.make_async_remote_copy(
                src_ref=v_ref, dst_ref=out_ref.at[slot], send_sem=send_sems.at[k - 1], recv_sem=recv_sems.at[k - 1],
                device_id=peer(k), device_id_type=MESH)

        sends = [copy(k, me) for k in range(1, N_DEV)]
        for cp in sends:
            cp.start()
        for k in range(1, N_DEV):
            px, py, pc = peer(k)
            copy(k, 4 * px + 2 * py + pc).wait_recv()
        for cp in sends:
            cp.wait_send()

    return pl.pallas_call(
        body, name=name,
        out_shape=jax.ShapeDtypeStruct((N_DEV, rows, cols), v.dtype),
        in_specs=[VMEM_SPEC], out_specs=VMEM_SPEC,
        scratch_shapes=[pltpu.SemaphoreType.DMA((N_DEV - 1,)), pltpu.SemaphoreType.DMA((N_DEV - 1,))],
    )(v)


def _weight_allgather(p):
    rows, cols = p.shape

    def body(p_ref, out_ref, send_sems, recv_sems, local_sem):
        x, y, c = _position()
        me, sibling = (x, y, c), (x, y, 1 - c)
        chips = [(1 - x, y), (x, 1 - y), (1 - x, 1 - y)]

        def slot(px, py, pc):
            return out_ref.at[4 * px + 2 * py + pc]

        def copy(k, block, to, src=None):
            return pltpu.make_async_remote_copy(
                src_ref=slot(*block) if src is None else src, dst_ref=slot(*block),
                send_sem=send_sems.at[k], recv_sem=recv_sems.at[k], device_id=to, device_id_type=MESH)

        mine = pltpu.make_async_copy(p_ref, slot(*me), local_sem)
        mine.start()
        first = [copy(0, me, sibling, src=p_ref)]
        first += [copy(1 + j, me, (*chip, c), src=p_ref) for j, chip in enumerate(chips)]
        for cp in first:
            cp.start()
        passed = [copy(4 + j, (*chip, c), sibling) for j, chip in enumerate(chips)]
        for j, chip in enumerate(chips):
            copy(1 + j, (*chip, c), me).wait_recv()
            passed[j].start()
        copy(0, sibling, me).wait_recv()
        for j, chip in enumerate(chips):
            copy(4 + j, (*chip, 1 - c), me).wait_recv()
        for cp in first + passed:
            cp.wait_send()
        mine.wait()

    return pl.pallas_call(
        body, name="weight_allgather",
        out_shape=jax.ShapeDtypeStruct((N_DEV, rows, cols), p.dtype),
        in_specs=[ANY], out_specs=ANY,
        scratch_shapes=[pltpu.SemaphoreType.DMA((7,)), pltpu.SemaphoreType.DMA((7,)), pltpu.SemaphoreType.DMA],
    )(p)


def _grad_exchange_sibling(g):
    _, rows, cols = g.shape

    def body(g_ref, out_ref, send_sems, recv_sems):
        x, y, c = _position()
        sibling = (x, y, 1 - c)

        def copy(q):
            px, py = q >> 1, q & 1
            return pltpu.make_async_remote_copy(
                src_ref=g_ref.at[4 * px + 2 * py + (1 - c)], dst_ref=out_ref.at[q],
                send_sem=send_sems.at[q], recv_sem=recv_sems.at[q], device_id=sibling, device_id_type=MESH)

        copies = [copy(q) for q in range(4)]
        for cp in copies:
            cp.start()
        for cp in copies:
            cp.wait_recv()
        for cp in copies:
            cp.wait_send()

    return pl.pallas_call(
        body, name="grad_exchange_sibling",
        out_shape=jax.ShapeDtypeStruct((4, rows, cols), g.dtype),
        in_specs=[ANY], out_specs=ANY,
        scratch_shapes=[pltpu.SemaphoreType.DMA((4,)), pltpu.SemaphoreType.DMA((4,))],
    )(g)


class _Rider:
    def __init__(self, operands, out_shapes, n_send, n_recv, start, finish, aliases=None):
        self.operands, self.out_shapes = list(operands), list(out_shapes)
        self.n_send, self.n_recv, self.start, self.finish = n_send, n_recv, start, finish
        self.aliases = aliases or {}


def _pcall(body, *, name, grid, in_specs, operands, out_shape, out_specs, scratch_shapes, params, rider=None):
    if rider is None:
        return pl.pallas_call(body, name=name, out_shape=out_shape, grid=grid, in_specs=in_specs, out_specs=out_specs,
                              scratch_shapes=scratch_shapes, compiler_params=params)(*operands)
    n_in, n_out, n_sc = len(operands), len(out_shape), len(scratch_shapes)
    r_in, r_out = len(rider.operands), len(rider.out_shapes)

    def wrapped(*refs):
        ins, rins = refs[:n_in], refs[n_in:n_in + r_in]
        outs, routs = refs[n_in + r_in:n_in + r_in + n_out], refs[n_in + r_in + n_out:n_in + r_in + n_out + r_out]
        rest = refs[n_in + r_in + n_out + r_out:]
        scratch, sems = rest[:n_sc], rest[n_sc:]
        ids = [pl.program_id(a) for a in range(len(grid))]
        first, last = ids[0] == 0, ids[0] == grid[0] - 1
        for a in range(1, len(grid)):
            first, last = first & (ids[a] == 0), last & (ids[a] == grid[a] - 1)

        @pl.when(first)
        def _():
            rider.start(rins, routs, *sems)

        body(*ins, *outs, *scratch)

        @pl.when(last)
        def _():
            rider.finish(rins, routs, *sems)

    return pl.pallas_call(
        wrapped, name=name, out_shape=list(out_shape) + rider.out_shapes, grid=grid,
        in_specs=list(in_specs) + [ANY] * r_in, out_specs=list(out_specs) + [ANY] * r_out,
        scratch_shapes=list(scratch_shapes) + [pltpu.SemaphoreType.DMA((rider.n_send,)), pltpu.SemaphoreType.DMA((rider.n_recv,))],
        input_output_aliases={n_in + i: n_out + o for i, o in rider.aliases.items()},
        compiler_params=params,
    )(*operands, *rider.operands)


def _flips(x, y, c):
    return [(x, y, 1 - c), (1 - x, y, c), (x, 1 - y, c), (1 - x, 1 - y, c)]


def _gather_direct_rider(p):
    rows, cols = p.shape

    def copies(p_ref, land, send_sems, recv_sems):
        x, y, c = _position()
        me = 4 * x + 2 * y + c
        peers = _flips(x, y, c)
        sends = [pltpu.make_async_remote_copy(src_ref=p_ref, dst_ref=land.at[me], send_sem=send_sems.at[k], recv_sem=recv_sems.at[k],
                                              device_id=to, device_id_type=MESH) for k, to in enumerate(peers)]
        recvs = [pltpu.make_async_remote_copy(src_ref=p_ref, dst_ref=land.at[4 * px + 2 * py + pc], send_sem=send_sems.at[k],
                                              recv_sem=recv_sems.at[k], device_id=(px, py, pc), device_id_type=MESH)
                 for k, (px, py, pc) in enumerate(peers)]
        mine = pltpu.make_async_copy(p_ref, land.at[me], send_sems.at[len(peers)])
        return sends, recvs, mine

    def start(rins, routs, send_sems, recv_sems):
        sends, _, mine = copies(rins[0], routs[0], send_sems, recv_sems)
        mine.start()
        for cp in sends:
            cp.start()

    def finish(rins, routs, send_sems, recv_sems):
        sends, recvs, mine = copies(rins[0], routs[0], send_sems, recv_sems)
        for cp in recvs:
            cp.wait_recv()
        for cp in sends:
            cp.wait_send()
        mine.wait()

    return _Rider([p], [jax.ShapeDtypeStruct((N_DEV, rows, cols), p.dtype)], 5, 4, start, finish)


def _gather_forward_rider(land):
    def copies(buf, send_sems, recv_sems):
        x, y, c = _position()
        chips = [(1 - x, y), (x, 1 - y), (1 - x, 1 - y)]
        sends = [pltpu.make_async_remote_copy(src_ref=buf.at[4 * px + 2 * py + c], dst_ref=buf.at[4 * px + 2 * py + c],
                                              send_sem=send_sems.at[k], recv_sem=recv_sems.at[k], device_id=(x, y, 1 - c),
                                              device_id_type=MESH) for k, (px, py) in enumerate(chips)]
        recvs = [pltpu.make_async_remote_copy(src_ref=buf.at[4 * px + 2 * py + 1 - c], dst_ref=buf.at[4 * px + 2 * py + 1 - c],
                                              send_sem=send_sems.at[k], recv_sem=recv_sems.at[k], device_id=(x, y, 1 - c),
                                              device_id_type=MESH) for k, (px, py) in enumerate(chips)]
        return sends, recvs

    def start(rins, routs, send_sems, recv_sems):
        for cp in copies(routs[0], send_sems, recv_sems)[0]:
            cp.start()

    def finish(rins, routs, send_sems, recv_sems):
        sends, recvs = copies(routs[0], send_sems, recv_sems)
        for cp in recvs:
            cp.wait_recv()
        for cp in sends:
            cp.wait_send()

    return _Rider([land], [jax.ShapeDtypeStruct(land.shape, land.dtype)], 3, 3, start, finish, aliases={0: 0})


def _chip_exchange_rider(s):
    def copies(s_ref, out_ref, send_sems, recv_sems):
        x, y, c = _position()
        chips = [(1 - x, y), (x, 1 - y), (1 - x, 1 - y)]
        return [pltpu.make_async_remote_copy(src_ref=s_ref.at[k], dst_ref=out_ref.at[k], send_sem=send_sems.at[k],
                                             recv_sem=recv_sems.at[k], device_id=(*chips[k], c), device_id_type=MESH)
                for k in range(3)]

    def start(rins, routs, send_sems, recv_sems):
        for cp in copies(rins[0], routs[0], send_sems, recv_sems):
            cp.start()

    def finish(rins, routs, send_sems, recv_sems):
        cps = copies(rins[0], routs[0], send_sems, recv_sems)
        for cp in cps:
            cp.wait_recv()
        for cp in cps:
            cp.wait_send()

    return _Rider([s], [jax.ShapeDtypeStruct(s.shape, s.dtype)], 3, 3, start, finish)


def _chip_partial_sums(g, recv_sib, jj, qq):
    _, rows, cols = g.shape
    tr = _div(rows, 512, ROW_ALIGN)

    def body(jj_ref, qq_ref, g_ref, r_ref, o_ref):
        o_ref[...] = (g_ref[...] + r_ref[...]).astype(o_ref.dtype)

    return pl.pallas_call(
        body, name="chip_partial_sums",
        out_shape=jax.ShapeDtypeStruct((3, rows, cols), BF16),
        grid_spec=pltpu.PrefetchScalarGridSpec(
            num_scalar_prefetch=2, grid=(3, rows // tr),
            in_specs=[pl.BlockSpec((None, tr, cols), lambda k, i, jj, qq: (jj[k], i, 0)),
                      pl.BlockSpec((None, tr, cols), lambda k, i, jj, qq: (qq[k], i, 0))],
            out_specs=pl.BlockSpec((None, tr, cols), lambda k, i, jj, qq: (k, i, 0))),
        compiler_params=_params(("arbitrary", "arbitrary")),
    )(jj, qq, g, recv_sib)


def _own_partial_sum(g, recv_sib, jj, qq):
    _, rows, cols = g.shape
    tr = _div(rows, 512, ROW_ALIGN)

    def body(jj_ref, qq_ref, g_ref, r_ref, o_ref):
        o_ref[...] = g_ref[...] + r_ref[...]

    return pl.pallas_call(
        body, name="own_partial_sum",
        out_shape=jax.ShapeDtypeStruct((rows, cols), F32),
        grid_spec=pltpu.PrefetchScalarGridSpec(
            num_scalar_prefetch=2, grid=(rows // tr,),
            in_specs=[pl.BlockSpec((None, tr, cols), lambda i, jj, qq: (jj[0], i, 0)),
                      pl.BlockSpec((None, tr, cols), lambda i, jj, qq: (qq[0], i, 0))],
            out_specs=pl.BlockSpec((tr, cols), lambda i, jj, qq: (i, 0))),
        compiler_params=_params(("arbitrary",)),
    )(jj, qq, g, recv_sib)


def _final_grad_sum(own, recv):
    rows, cols = own.shape
    tr = _div(rows, 512, ROW_ALIGN)

    def body(o_ref, r_ref, out_ref):
        out_ref[...] = ((o_ref[...] + r_ref[0].astype(F32)) + r_ref[1].astype(F32)) + r_ref[2].astype(F32)

    return pl.pallas_call(
        body, name="final_grad_sum",
        out_shape=jax.ShapeDtypeStruct((rows, cols), F32),
        grid=(rows // tr,),
        in_specs=[pl.BlockSpec((tr, cols), lambda i: (i, 0)), pl.BlockSpec((3, tr, cols), lambda i: (0, i, 0))],
        out_specs=pl.BlockSpec((tr, cols), lambda i: (i, 0)),
        compiler_params=_params(("arbitrary",)),
    )(own, recv)


def _sibling_exchange_rider(g):
    _, rows, cols = g.shape

    def copies(g_ref, out_ref, send_sems, recv_sems):
        x, y, c = _position()
        return [pltpu.make_async_remote_copy(
            src_ref=g_ref.at[4 * (q >> 1) + 2 * (q & 1) + (1 - c)], dst_ref=out_ref.at[q], send_sem=send_sems.at[q],
            recv_sem=recv_sems.at[q], device_id=(x, y, 1 - c), device_id_type=MESH) for q in range(4)]

    def start(rins, routs, send_sems, recv_sems):
        for cp in copies(rins[0], routs[0], send_sems, recv_sems):
            cp.start()

    def finish(rins, routs, send_sems, recv_sems):
        cps = copies(rins[0], routs[0], send_sems, recv_sems)
        for cp in cps:
            cp.wait_recv()
        for cp in cps:
            cp.wait_send()

    return _Rider([g], [jax.ShapeDtypeStruct((4, rows, cols), g.dtype)], 4, 4, start, finish)


def _chip_sums(g, recv_sib=None):
    x, y, c = _position()
    chips = [(1 - x, y), (x, 1 - y), (1 - x, 1 - y)]
    jj = jnp.stack([4 * px + 2 * py + c for px, py in chips]).astype(jnp.int32)
    qq = jnp.stack([2 * px + py for px, py in chips]).astype(jnp.int32)
    jme = jnp.reshape(4 * x + 2 * y + c, (1,)).astype(jnp.int32)
    qme = jnp.reshape(2 * x + y, (1,)).astype(jnp.int32)
    if recv_sib is None:
        recv_sib = _grad_exchange_sibling(g)
    return _chip_partial_sums(g, recv_sib, jj, qq), _own_partial_sum(g, recv_sib, jme, qme)


def _matmul(name, form, prods, M, N, K, tm, tn, tk, out_dtypes, extras=(), epilogue=None, rows_per_example=None, rider=None):
    nk = K // tk
    n_acc = len(prods)
    flat = [ab for group in prods for ab in group]
    dims = {"nn": (((1,), (0,)), ((), ())), "nt": (((1,), (1,)), ((), ())), "tn": (((0,), (0,)), ((), ()))}[form]
    direct = nk > 1 and epilogue is None and n_acc == 1 and list(out_dtypes) == [F32]

    def spec(shape, index_map, whole):
        if whole:
            return pl.BlockSpec(shape, index_map, pipeline_mode=pl.Buffered(1))
        return pl.BlockSpec(shape, index_map)

    if form == "tn":
        a_spec = spec((tk, tm), lambda i, j, k: (k, i), nk == 1 and M == tm)
    else:
        a_spec = spec((tm, tk), lambda i, j, k: (i, k), nk == 1 and M == tm)
    if form == "nt":
        b_spec = spec((tn, tk), lambda i, j, k: (j, k), nk == 1 and N == tn)
    else:
        b_spec = spec((tk, tn), lambda i, j, k: (k, j), nk == 1 and N == tn)
    in_specs, operands = [], []
    for a, b in flat:
        in_specs += [a_spec, _weight_spec(b) if isinstance(b, _Slab) else b_spec]
        operands += [a, _weight_operand(b)]
    for arr, kind, off in extras:
        if kind == "tile":
            assert off % tn == 0
            in_specs.append(pl.BlockSpec((tm, tn), functools.partial(lambda i, j, k, o: (i, j + o), o=off // tn)))
        else:
            tiles = rows_per_example // tm
            in_specs.append(pl.BlockSpec((None, 1, tn), functools.partial(lambda i, j, k, t: (i // t, 0, j), t=tiles)))
        operands.append(arr)
    n_in, n_out = len(operands), len(out_dtypes)

    def body(*refs):
        in_refs, out_refs, acc_refs = refs[:n_in], refs[n_in:n_in + n_out], refs[n_in + n_out:]
        k = pl.program_id(2)
        partials, p = [], 0
        for group in prods:
            tot = None
            for _ in group:
                d = lax.dot_general(in_refs[2 * p][...], _weight_value(in_refs[2 * p + 1], flat[p][1]), dims,
                                    preferred_element_type=F32)
                tot = d if tot is None else tot + d
                p += 1
            partials.append(tot)

        def finish(accs):
            ex = [r[...] for r in in_refs[2 * len(flat):]]
            outs = epilogue(accs, ex) if epilogue is not None else accs
            for r, o in zip(out_refs, outs):
                r[...] = o.astype(r.dtype)

        if nk == 1:
            finish(partials)
        elif direct:
            @pl.when(k == 0)
            def _():
                out_refs[0][...] = partials[0]

            @pl.when(k > 0)
            def _():
                out_refs[0][...] += partials[0]
        else:
            @pl.when(k == 0)
            def _():
                for r, v in zip(acc_refs, partials):
                    r[...] = v

            @pl.when(k > 0)
            def _():
                for r, v in zip(acc_refs, partials):
                    r[...] += v

            @pl.when(k == nk - 1)
            def _():
                finish([r[...] for r in acc_refs])

    return _pcall(
        body, name=name,
        out_shape=[jax.ShapeDtypeStruct((M, N), dt) for dt in out_dtypes],
        grid=(M // tm, N // tn, nk),
        in_specs=in_specs, operands=operands,
        out_specs=[pl.BlockSpec((tm, tn), lambda i, j, k: (i, j)) for _ in out_dtypes],
        scratch_shapes=[pltpu.VMEM((tm, tn), F32) for _ in range(n_acc)] if nk > 1 and not direct else [],
        params=_params(("parallel", "parallel", "arbitrary")), rider=rider)


def _rowwise(name, fn, T, tm, ins, outs, rows_per_example):
    tiles = rows_per_example // tm
    n_ex = T // rows_per_example
    in_specs, operands = [], []
    for arr, kind, arg in ins:
        if kind == "row":
            if arg is None:
                in_specs.append(pl.BlockSpec((tm, arr.shape[1]), lambda i: (i, 0)))
            else:
                in_specs.append(pl.BlockSpec((tm, arg[0]), functools.partial(lambda i, cb: (i, cb), cb=arg[1])))
        elif kind == "bvec":
            in_specs.append(pl.BlockSpec((None, 1, arr.shape[2]), lambda i: (i // tiles, 0, 0)))
        else:
            in_specs.append(pl.BlockSpec((1, arr.shape[1]), lambda i: (0, 0)))
        operands.append(arr)
    out_shape, out_specs = [], []
    for kind, cols, dt in outs:
        if kind == "row":
            out_shape.append(jax.ShapeDtypeStruct((T, cols), dt))
            out_specs.append(pl.BlockSpec((tm, cols), lambda i: (i, 0)))
        else:
            out_shape.append(jax.ShapeDtypeStruct((n_ex, 1, cols), F32))
            out_specs.append(pl.BlockSpec((None, 1, cols), lambda i: (i // tiles, 0, 0)))
    n_in = len(operands)

    def body(*refs):
        i = pl.program_id(0)
        vals = fn(*[r[...] for r in refs[:n_in]])
        for (kind, _, _), r, v in zip(outs, refs[n_in:], vals):
            if kind == "row":
                r[...] = v.astype(r.dtype)
            else:
                @pl.when(i % tiles == 0)
                def _():
                    r[...] = jnp.zeros_like(r)

                r[...] += v

    return pl.pallas_call(
        body, name=name, out_shape=out_shape, grid=(T // tm,), in_specs=in_specs, out_specs=out_specs,
        compiler_params=_params(("arbitrary",)),
    )(*operands)


def _colsum(v):
    return jnp.sum(v, axis=0, keepdims=True)


def _rms_parts(x):
    rstd = lax.rsqrt(jnp.mean(x * x, axis=-1, keepdims=True) + RMS_EPS)
    return x * rstd, rstd


def _resident(shape):
    return pl.BlockSpec(shape, lambda i: (0, 0), pipeline_mode=pl.Buffered(1))


class _Slab:
    def __init__(self, land, off, rows):
        assert off % rows == 0 and rows % ROW_ALIGN == 0
        self.land, self.off, self.rows = land, off, rows
        self.shape = (N_DEV * rows, land.shape[2])


def _weight_spec(w):
    if isinstance(w, _Slab):
        return pl.BlockSpec((N_DEV, w.rows, w.shape[1]), lambda *_: (0, w.off // w.rows, 0), pipeline_mode=pl.Buffered(1))
    return pl.BlockSpec(w.shape, lambda *_: (0, 0), pipeline_mode=pl.Buffered(1))


def _weight_operand(w):
    return w.land if isinstance(w, _Slab) else w


def _weight_value(ref, w):
    return ref[...].reshape(w.shape) if isinstance(w, _Slab) else ref[...]


def _example_acc(r, i, tiles, v):
    @pl.when(i % tiles == 0)
    def _():
        r[...] = jnp.zeros_like(r)

    r[...] += v


def _norm_matmul(name, x, g, sc, sh, weights, epilogue, outs, S, vecs=(), rider=None):
    T, D = x.shape
    tm = _div(S, 256, 8)
    tiles = S // tm
    nw, nv = len(weights), len(vecs)

    def body(*refs):
        x_ref, g_ref, sc_ref, sh_ref = refs[:4]
        w_refs, v_refs = refs[4:4 + nw], refs[4 + nw:4 + nw + nv]
        h_ref, out_refs = refs[4 + nw + nv], refs[5 + nw + nv:]
        xhat, _ = _rms_parts(x_ref[...])
        h = ((xhat * g_ref[...]) * (1.0 + sc_ref[...]) + sh_ref[...]).astype(BF16)
        h_ref[...] = h
        accs = [lax.dot_general(h, _weight_value(r, w), NT, preferred_element_type=F32) for r, w in zip(w_refs, weights)]
        for r, o in zip(out_refs, epilogue(accs, *[v[...] for v in v_refs])):
            r[...] = o.astype(r.dtype)

    bvec = pl.BlockSpec((None, 1, D), lambda i: (i // tiles, 0, 0))
    return _pcall(
        body, name=name,
        out_shape=[jax.ShapeDtypeStruct((T, D), BF16)] + [jax.ShapeDtypeStruct((T, w), dt) for w, dt in outs],
        grid=(T // tm,),
        in_specs=[pl.BlockSpec((tm, D), lambda i: (i, 0)), pl.BlockSpec((1, D), lambda i: (0, 0)), bvec, bvec]
        + [_weight_spec(w) for w in weights] + [pl.BlockSpec(v.shape, lambda i: (0, 0)) for v in vecs],
        operands=[x, g, sc, sh, *[_weight_operand(w) for w in weights], *vecs],
        out_specs=[pl.BlockSpec((tm, D), lambda i: (i, 0))] + [pl.BlockSpec((tm, w), lambda i: (i, 0)) for w, _ in outs],
        scratch_shapes=[], params=_params(("arbitrary",)), rider=rider)


def _gated_grad_matmul(name, dx, y, gt, coeff, w, tiles_in, epilogue, outs, S, rider=None):
    T, D = dx.shape
    N = w.shape[0]
    tm = _div(S, 256, 8)
    tiles = S // tm
    nt = len(tiles_in)

    def body(*refs):
        dx_ref, y_ref, gt_ref, w_ref = refs[:4]
        t_refs, dy_ref, dgt_ref, out_refs = refs[4:4 + nt], refs[4 + nt], refs[5 + nt], refs[6 + nt:]
        i = pl.program_id(0)
        dxv = dx_ref[...]
        dy = (coeff * gt_ref[...] * dxv).astype(BF16)
        dy_ref[...] = dy
        _example_acc(dgt_ref, i, tiles, _colsum(coeff * dxv * y_ref[...].astype(F32)))
        acc = lax.dot_general(dy, _weight_value(w_ref, w), NT, preferred_element_type=F32)
        for r, o in zip(out_refs, epilogue(acc, [t[...] for t in t_refs])):
            r[...] = o.astype(r.dtype)

    row = pl.BlockSpec((tm, D), lambda i: (i, 0))
    bvec = pl.BlockSpec((None, 1, D), lambda i: (i // tiles, 0, 0))
    return _pcall(
        body, name=name,
        out_shape=[jax.ShapeDtypeStruct((T, D), BF16), jax.ShapeDtypeStruct((T // S, 1, D), F32)]
        + [jax.ShapeDtypeStruct((T, N), dt) for dt in outs],
        grid=(T // tm,),
        in_specs=[row, row, bvec, _weight_spec(w)]
        + [pl.BlockSpec((tm, N), functools.partial(lambda i, cb: (i, cb), cb=cb)) for _, cb in tiles_in],
        operands=[dx, y, gt, _weight_operand(w), *[t for t, _ in tiles_in]],
        out_specs=[row, bvec] + [pl.BlockSpec((tm, N), lambda i: (i, 0)) for _ in outs],
        scratch_shapes=[], params=_params(("arbitrary",)), rider=rider)


def _matmul_normmod_bwd(name, prods, x, g, sc, dres, S, rider=None):
    T, D = x.shape
    tm = _div(S, 256, 8)
    tiles = S // tm
    npr = len(prods)

    def body(*refs):
        ab = refs[:2 * npr]
        x_ref, g_ref, sc_ref, dr_ref = refs[2 * npr:2 * npr + 4]
        dx_ref, dsh_ref, dsc_ref, dg_ref = refs[2 * npr + 4:]
        i = pl.program_id(0)
        dh = None
        for p in range(npr):
            d = lax.dot_general(ab[2 * p][...], _weight_value(ab[2 * p + 1], prods[p][1]), NN, preferred_element_type=F32)
            dh = d if dh is None else dh + d
        xhat, rstd = _rms_parts(x_ref[...])
        gv = g_ref[...]
        dn = dh * (1.0 + sc_ref[...])
        dxh = dn * gv
        dx_ref[...] = dr_ref[...] + rstd * (dxh - xhat * jnp.mean(dxh * xhat, axis=-1, keepdims=True))
        _example_acc(dsh_ref, i, tiles, _colsum(dh))
        _example_acc(dsc_ref, i, tiles, _colsum(dh * (xhat * gv)))
        _example_acc(dg_ref, i, tiles, _colsum(dn * xhat))

    row = pl.BlockSpec((tm, D), lambda i: (i, 0))
    bvec = pl.BlockSpec((None, 1, D), lambda i: (i // tiles, 0, 0))
    in_specs, operands = [], []
    for a, b in prods:
        in_specs += [pl.BlockSpec((tm, a.shape[1]), lambda i: (i, 0)), _weight_spec(b)]
        operands += [a, _weight_operand(b)]
    acc_shape = jax.ShapeDtypeStruct((T // S, 1, D), F32)
    return _pcall(
        body, name=name,
        out_shape=[jax.ShapeDtypeStruct((T, D), F32), acc_shape, acc_shape, acc_shape],
        grid=(T // tm,),
        in_specs=in_specs + [row, pl.BlockSpec((1, D), lambda i: (0, 0)), bvec, row],
        operands=[*operands, x, g, sc, dres],
        out_specs=[row, bvec, bvec, bvec],
        scratch_shapes=[], params=_params(("arbitrary",)), rider=rider)


def _ffn_forward(tag, x, g, sh, sc, gt, wgT, wuT, wd, S, gather=None):
    T, D = x.shape
    F = wd.shape[0]

    def gateup(accs):
        a, u = accs
        return [a, u, a * _sigmoid(a) * u]

    h, a, u, s, *land = _norm_matmul(f"{tag}_gateup", x, g, sc, sh, [wgT, wuT], gateup, [(F, BF16)] * 3, S,
                                     rider=None if gather is None else _gather_direct_rider(gather))

    def down(accs, ex):
        xv, gtv = ex
        return [xv + 0.5 * gtv * accs[0], accs[0]]

    tmd = _div(S, 512, 8)
    x_new, y, *land = _matmul(f"{tag}_down", "nn", [[(s, wd)]], T, D, F, tmd, D, F, [F32, BF16],
                              extras=[(x, "tile", 0), (gt, "brow", 0)], epilogue=down, rows_per_example=S,
                              rider=None if gather is None else _gather_forward_rider(land[0]))
    return x_new, (x, h, a, u, s, y), (land[0] if land else None)


def _ffn_backward(tag, dx_out, saved, g, sc, gt, wgT, wuT, wd, S, rider=None, dh_rider=None):
    x, h, a, u, s, y = saved
    T, D = x.shape
    F = wd.shape[0]

    def act_grad(ds, ex):
        av, uv = ex[0].astype(F32), ex[1].astype(F32)
        sg = _sigmoid(av)
        return [ds * uv * (sg * (1.0 + av * (1.0 - sg))), ds * (av * sg)]

    dy, dgt, da, du, *rode = _gated_grad_matmul(f"{tag}_act_grad", dx_out, y, gt, 0.5, wd, [(a, 0), (u, 0)], act_grad,
                                                [BF16, BF16], S, rider=rider)
    tkw = _div(T, 1024, LANES)
    dwd = _matmul(f"{tag}_dw_down", "tn", [[(s, dy)]], F, D, T, F, D, tkw, [F32])[0]
    dwgT = _matmul(f"{tag}_dw_gate", "tn", [[(da, h)]], F, D, T, F, D, tkw, [F32])[0]
    dwuT = _matmul(f"{tag}_dw_up", "tn", [[(du, h)]], F, D, T, F, D, tkw, [F32])[0]
    dx_in, dsh, dsc, dg, *rode_dh = _matmul_normmod_bwd(
        f"{tag}_dh", [(da, wgT), (du, wuT)], x, g, sc, dx_out, S,
        rider=None if dh_rider is None else dh_rider(dwgT, dwuT, dwd))
    return dx_in, (dsh, dsc, dgt, dg), (dwgT, dwuT, dwd), rode + rode_dh


def _loss_head(x, tgt, g, S):
    T, D = x.shape

    def fn(xv, tv, gv):
        xhat, rstd = _rms_parts(xv)
        e = xhat * gv - tv
        loss = jnp.broadcast_to(0.5 / D * jnp.sum(_colsum(e * e), axis=1, keepdims=True), (1, LANES))
        dy = e * (1.0 / D)
        dxh = dy * gv
        dx = rstd * (dxh - xhat * jnp.mean(dxh * xhat, axis=-1, keepdims=True))
        return [dx, loss, _colsum(dy * xhat)]

    return _rowwise("loss_head", fn, T, _div(S, 512, 8), [(x, "row", None), (tgt, "row", None), (g, "vec", None)],
                    [("row", D, F32), ("bacc", LANES, F32), ("bacc", D, F32)], S)


def _cumsum(v):
    B, S, _ = v.shape
    rows = _div(S, 1024, BLOCK)

    def body(x_ref, o_ref, carry):
        i = pl.program_id(1)

        @pl.when(i == 0)
        def _():
            carry[...] = jnp.zeros_like(carry)

        r = lax.broadcasted_iota(jnp.int32, (BLOCK, BLOCK), 0)
        c = lax.broadcasted_iota(jnp.int32, (BLOCK, BLOCK), 1)
        tri = (c <= r).astype(F32)
        last = carry[0:1, :]
        for j in range(0, rows, BLOCK):
            cum = jnp.dot(tri, x_ref[j:j + BLOCK, :], precision=lax.Precision.HIGHEST, preferred_element_type=F32) + last
            o_ref[j:j + BLOCK, :] = cum
            last = cum[BLOCK - 1:BLOCK, :]
        carry[...] = jnp.broadcast_to(last, carry.shape)

    return pl.pallas_call(
        body, name="cumsum", out_shape=jax.ShapeDtypeStruct(v.shape, F32), grid=(B, S // rows),
        in_specs=[pl.BlockSpec((None, rows, LANES), lambda b, i: (b, i, 0))],
        out_specs=pl.BlockSpec((None, rows, LANES), lambda b, i: (b, i, 0)),
        scratch_shapes=[pltpu.VMEM((8, LANES), F32)],
        compiler_params=_params(("arbitrary", "arbitrary")),
    )(v)


def _with_ones(x):
    lane = lax.broadcasted_iota(jnp.int32, (x.shape[0], HEAD_DIM), 1)
    return jnp.concatenate([x, jnp.where(lane == 0, 1.0, 0.0).astype(x.dtype)], axis=1)


def _causal_strip(s, r):
    qpos = r + lax.broadcasted_iota(jnp.int32, s.shape, 0)
    kpos = lax.broadcasted_iota(jnp.int32, s.shape, 1)
    return jnp.where(kpos <= qpos, s, NEG_INF)


NT = (((1,), (1,)), ((), ()))
NN = (((1,), (0,)), ((), ()))
TN = (((0,), (0,)), ((), ()))


def _fox_fwd(pm3, cumT, qcol, kcol, vcol, tq, rider=None):
    B, S, _ = pm3.shape
    nq = S // tq
    strips = range(0, tq, FOX_STRIP)

    def body(q_ref, k_ref, v_ref, ck_ref, o_ref, o32_ref, lse_ref, s_sc, p_sc, al_sc, m_sc, acc_sc):
        qi, kj = pl.program_id(1), pl.program_id(2)

        @pl.when(kj == 0)
        def _():
            m_sc[...] = jnp.full_like(m_sc, NEG_INF)
            acc_sc[...] = jnp.zeros_like(acc_sc)

        def tile(diagonal):
            def scores(h):
                hs = slice(HEAD_DIM * h, HEAD_DIM * (h + 1))
                s_sc[h % 2] = lax.dot_general(q_ref[:, hs] * SCALE, k_ref[:, hs], NT, preferred_element_type=F32)

            def accumulate(h):
                hs = slice(HEAD_DIM * h, HEAD_DIM * (h + 1))
                acc_sc[h] = al_sc[h % 2] * acc_sc[h] + lax.dot_general(p_sc[h % 2], _with_ones(v_ref[:, hs]), NN,
                                                                       preferred_element_type=F32)

            scores(0)
            for h in range(FOX_HEADS):
                b = h % 2
                if h + 1 < FOX_HEADS:
                    scores(h + 1)
                if h >= 1:
                    accumulate(h - 1)
                ck = ck_ref[h:h + 1, :]
                for r in strips:
                    rows = slice(r, r + FOX_STRIP)
                    s = s_sc[b, rows, :] - ck
                    if diagonal:
                        s = _causal_strip(s, r)
                    m_prev = m_sc[h, rows, :]
                    m_new = jnp.maximum(m_prev, jnp.max(s, axis=-1, keepdims=True))
                    p_sc[b, rows, :] = jnp.exp(s - m_new).astype(BF16)
                    al_sc[b, rows, :] = jnp.exp(m_prev - m_new)
                    m_sc[h, rows, :] = m_new
            accumulate(FOX_HEADS - 1)

        @pl.when(kj < qi)
        def _():
            tile(False)

        @pl.when(kj == qi)
        def _():
            tile(True)

        @pl.when(kj == nq - 1)
        def _():
            lse_ref[...] = jnp.zeros_like(lse_ref)
            for h in range(FOX_HEADS):
                hs = slice(HEAD_DIM * h, HEAD_DIM * (h + 1))
                acc = acc_sc[h]
                l = acc[:, HEAD_DIM:HEAD_DIM + 1]
                oh = acc[:, :HEAD_DIM] / l
                o_ref[:, hs] = oh.astype(o_ref.dtype)
                o32_ref[:, hs] = oh
                lse_ref[:, h:h + 1] = m_sc[h] + jnp.log(l)

    ospec = pl.BlockSpec((None, tq, FOX_W), lambda b, i, j: (b, i, 0))
    return _pcall(
        body, name="fox_forward",
        out_shape=[jax.ShapeDtypeStruct((B, S, FOX_W), BF16), jax.ShapeDtypeStruct((B, S, FOX_W), F32),
                   jax.ShapeDtypeStruct((B, S, LANES), F32)],
        grid=(B, nq, nq),
        in_specs=[pl.BlockSpec((None, tq, FOX_W), lambda b, i, j: (b, i, qcol)),
                  pl.BlockSpec((None, tq, FOX_W), lambda b, i, j: (b, jnp.minimum(i, j), kcol)),
                  pl.BlockSpec((None, tq, FOX_W), lambda b, i, j: (b, jnp.minimum(i, j), vcol)),
                  pl.BlockSpec((None, 8, tq), lambda b, i, j: (b, 0, jnp.minimum(i, j)))],
        operands=[pm3, pm3, pm3, cumT],
        out_specs=[ospec, ospec, pl.BlockSpec((None, tq, LANES), lambda b, i, j: (b, i, 0))],
        scratch_shapes=[pltpu.VMEM((2, tq, tq), F32), pltpu.VMEM((2, tq, tq), BF16), pltpu.VMEM((2, tq, 1), F32),
                        pltpu.VMEM((FOX_HEADS, tq, 1), F32), pltpu.VMEM((FOX_HEADS, tq, LANES), F32)],
        params=_params(("parallel", "parallel", "arbitrary")), rider=rider)


def _fox_bwd(pm3, do, delta, lse, cumT, qcol, kcol, vcol, tq, rider=None):
    B, S, _ = pm3.shape
    nq = S // tq
    strips = range(0, tq, FOX_STRIP)

    def body(q_ref, k_ref, v_ref, do_ref, dl_ref, lse_ref, ck_ref, dq_ref, rs_ref, dk_ref, dv_ref, cs_ref,
             s_sc, dp_sc, p_sc, ds_sc, dq_sc, dk_sc, dv_sc):
        kj, qi = pl.program_id(1), pl.program_id(2)

        @pl.when((kj == 0) & (qi == 0))
        def _():
            dq_sc[...] = jnp.zeros_like(dq_sc)

        @pl.when(qi == 0)
        def _():
            dk_sc[...] = jnp.zeros_like(dk_sc)
            dv_sc[...] = jnp.zeros_like(dv_sc)

        def tile(diagonal):
            qrows = pl.ds(pl.multiple_of(qi * tq, tq), tq)
            for h in range(FOX_HEADS):
                hs = slice(HEAD_DIM * h, HEAD_DIM * (h + 1))
                qh, kh, doh = q_ref[:, hs] * SCALE, k_ref[:, hs], do_ref[:, hs]
                b = h % 2
                s_sc[b] = lax.dot_general(qh, kh, NT, preferred_element_type=F32)
                dp_sc[b] = lax.dot_general(doh, v_ref[:, hs], NT, preferred_element_type=F32)
                ck = ck_ref[h:h + 1, :]
                for r in strips:
                    rows = slice(r, r + FOX_STRIP)
                    s = s_sc[b, rows, :] - ck
                    if diagonal:
                        s = _causal_strip(s, r)
                    p = jnp.exp(s - lse_ref[rows, h:h + 1])
                    p_sc[b, rows, :] = p.astype(BF16)
                    ds_sc[b, rows, :] = (p * (dp_sc[b, rows, :] - dl_ref[rows, h:h + 1])).astype(BF16)
                dv_sc[h] += lax.dot_general(doh, p_sc[b], TN, preferred_element_type=F32)
                dk_sc[h] += lax.dot_general(_with_ones(qh), ds_sc[b], TN, preferred_element_type=F32)
                dq_sc[h, qrows, :] += lax.dot_general(ds_sc[b], _with_ones(kh), NN, preferred_element_type=F32)

        @pl.when(qi > kj)
        def _():
            tile(False)

        @pl.when(qi == kj)
        def _():
            tile(True)

        @pl.when(qi == nq - 1)
        def _():
            cs_ref[...] = jnp.zeros_like(cs_ref)
            for h in range(FOX_HEADS):
                hs = slice(HEAD_DIM * h, HEAD_DIM * (h + 1))
                dv_ref[:, hs] = dv_sc[h].T.astype(dv_ref.dtype)
                dk = dk_sc[h].T
                dk_ref[:, hs] = dk[:, :HEAD_DIM].astype(dk_ref.dtype)
                cs_ref[:, h:h + 1] = dk[:, HEAD_DIM:HEAD_DIM + 1]

        @pl.when((kj == nq - 1) & (qi == nq - 1))
        def _():
            rs_ref[...] = jnp.zeros_like(rs_ref)
            for h in range(FOX_HEADS):
                hs = slice(HEAD_DIM * h, HEAD_DIM * (h + 1))
                dq_ref[:, hs] = (dq_sc[h, :, :HEAD_DIM] * SCALE).astype(dq_ref.dtype)
                rs_ref[:, h:h + 1] = dq_sc[h, :, HEAD_DIM:HEAD_DIM + 1]

    def qside(width, col=0):
        return pl.BlockSpec((None, tq, width), lambda b, j, i: (b, jnp.maximum(i, j), col))

    kspec = pl.BlockSpec((None, tq, FOX_W), lambda b, j, i: (b, j, 0))
    return _pcall(
        body, name="fox_backward",
        out_shape=[jax.ShapeDtypeStruct((B, S, FOX_W), BF16), jax.ShapeDtypeStruct((B, S, LANES), F32),
                   jax.ShapeDtypeStruct((B, S, FOX_W), BF16), jax.ShapeDtypeStruct((B, S, FOX_W), BF16),
                   jax.ShapeDtypeStruct((B, S, LANES), F32)],
        grid=(B, nq, nq),
        in_specs=[qside(FOX_W, qcol),
                  pl.BlockSpec((None, tq, FOX_W), lambda b, j, i: (b, j, kcol)),
                  pl.BlockSpec((None, tq, FOX_W), lambda b, j, i: (b, j, vcol)),
                  qside(FOX_W), qside(LANES), qside(LANES),
                  pl.BlockSpec((None, 8, tq), lambda b, j, i: (b, 0, j))],
        operands=[pm3, pm3, pm3, do, delta, lse, cumT],
        out_specs=[pl.BlockSpec((None, S, FOX_W), lambda b, j, i: (b, 0, 0)),
                   pl.BlockSpec((None, S, LANES), lambda b, j, i: (b, 0, 0)),
                   kspec, kspec, pl.BlockSpec((None, tq, LANES), lambda b, j, i: (b, j, 0))],
        scratch_shapes=[pltpu.VMEM((2, tq, tq), F32), pltpu.VMEM((2, tq, tq), F32), pltpu.VMEM((2, tq, tq), BF16),
                        pltpu.VMEM((2, tq, tq), BF16), pltpu.VMEM((FOX_HEADS, S, LANES), F32),
                        pltpu.VMEM((FOX_HEADS, LANES, tq), F32), pltpu.VMEM((FOX_HEADS, HEAD_DIM, tq), F32)],
        params=_params(("parallel", "arbitrary", "arbitrary")), rider=rider)


def _fox_delta(do, o32, T, S):
    def fn(dov, ov):
        prod = dov.astype(F32) * ov
        lane = lax.broadcasted_iota(jnp.int32, (dov.shape[0], LANES), 1)
        delta = jnp.zeros((dov.shape[0], LANES), F32)
        for h in range(FOX_HEADS):
            hs = slice(HEAD_DIM * h, HEAD_DIM * (h + 1))
            delta = jnp.where(lane == h, jnp.sum(prod[:, hs], axis=-1, keepdims=True), delta)
        return [delta]

    return _rowwise("fox_delta", fn, T, _div(S, 512, 8), [(do, "row", None), (o32, "row", None)], [("row", LANES, F32)], S)[0]


def _alibi_slope(group, head):
    return 2.0 ** (-ALIBI_MAX_BIAS * (group * DIL_HPG + head + 1) / (N_DIL * DIL_HPG))


def _residue_order(a, B, S, d):
    C = a.shape[-1]
    if d == 1:
        return a.reshape(B, S, C)
    return a.reshape(B, S // d, d, C).transpose(0, 2, 1, 3).reshape(B * d, S // d, C)


def _token_order(a, B, S, d):
    C = a.shape[-1]
    if d == 1:
        return a.reshape(B * S, C)
    return a.reshape(B, d, S // d, C).transpose(0, 2, 1, 3).reshape(B * S, C)


def _band_scores(qh, kcat, slope_d, has_prev):
    qi = lax.broadcasted_iota(jnp.int32, (BLOCK, 2 * BLOCK), 0)
    c = lax.broadcasted_iota(jnp.int32, (BLOCK, 2 * BLOCK), 1)
    s = lax.dot_general(qh, kcat, NT, preferred_element_type=F32) - slope_d * (BLOCK + qi - c).astype(F32)
    valid = (c >= qi) & (c <= qi + BLOCK)
    if has_prev is not None:
        valid = valid & ((c >= BLOCK) | has_prev)
    return jnp.where(valid, s, NEG_INF)


def _band_operands(j, cur_ref, prev_ref, hs):
    if j == 0:
        return jnp.concatenate([prev_ref[:, hs], cur_ref[0:BLOCK, hs]], axis=0)
    return cur_ref[(j - 1) * BLOCK:(j + 1) * BLOCK, hs]


def _dil_specs(Ls, qb, cols):
    nsub = qb // BLOCK
    qcol, kcol, vcol = cols

    def cur(col):
        return pl.BlockSpec((None, qb, DIL_GW), lambda s, n: (s, n, col))

    def prev(col):
        return pl.BlockSpec((None, BLOCK, DIL_GW), lambda s, n: (s, jnp.maximum(n * nsub - 1, 0), col))

    return [cur(qcol), cur(kcol), prev(kcol), cur(vcol), prev(vcol)]


def _dil_fwd(group, src, cols):
    _, dilation = DIL_GROUPS[group]
    nseq, Ls, _ = src.shape
    qb = _div(Ls, 512, BLOCK)
    nsub = qb // BLOCK

    def body(q_ref, kc_ref, kp_ref, vc_ref, vp_ref, o_ref, lse_ref):
        has_prev = pl.program_id(1) > 0
        lse_ref[...] = jnp.zeros_like(lse_ref)
        for h in range(DIL_HPG):
            hs = slice(HEAD_DIM * h, HEAD_DIM * (h + 1))
            scores = [_band_scores(q_ref[j * BLOCK:(j + 1) * BLOCK, hs] * SCALE, _band_operands(j, kc_ref, kp_ref, hs),
                                   _alibi_slope(group, h) * dilation, has_prev if j == 0 else None) for j in range(nsub)]
            pending = None

            def write(j, m, acc):
                rows = slice(j * BLOCK, (j + 1) * BLOCK)
                l = acc[:, HEAD_DIM:HEAD_DIM + 1]
                o_ref[rows, hs] = acc[:, :HEAD_DIM] / l
                lse_ref[rows, h:h + 1] = m + jnp.log(l)

            for j in range(nsub):
                m = jnp.max(scores[j], axis=-1, keepdims=True)
                p = jnp.exp(scores[j] - m).astype(BF16)
                acc = lax.dot_general(p, _with_ones(_band_operands(j, vc_ref, vp_ref, hs)), NN, preferred_element_type=F32)
                if pending is not None:
                    write(*pending)
                pending = (j, m, acc)
            write(*pending)

    return pl.pallas_call(
        body, name=f"dil_forward_{group}",
        out_shape=[jax.ShapeDtypeStruct((nseq, Ls, DIL_GW), F32), jax.ShapeDtypeStruct((nseq, Ls, LANES), F32)],
        grid=(nseq, Ls // qb),
        in_specs=_dil_specs(Ls, qb, cols),
        out_specs=[pl.BlockSpec((None, qb, DIL_GW), lambda s, n: (s, n, 0)),
                   pl.BlockSpec((None, qb, LANES), lambda s, n: (s, n, 0))],
        compiler_params=_params(("parallel", "arbitrary")),
    )(src, src, src, src, src)


def _dil_bwd(group, src, cols, Lr, dyr, dlr):
    _, dilation = DIL_GROUPS[group]
    nseq, Ls, _ = src.shape
    qb = _div(Ls, 512, BLOCK)
    nsub, nb = qb // BLOCK, Ls // qb

    def body(q_ref, kc_ref, kp_ref, vc_ref, vp_ref, L_ref, dy_ref, dl_ref, dq_ref, dk_ref, dv_ref, dk_sc, dv_sc):
        n = pl.program_id(1)
        has_prev = n > 0

        @pl.when(n == 0)
        def _():
            dk_sc[...] = jnp.zeros_like(dk_sc)
            dv_sc[...] = jnp.zeros_like(dv_sc)

        base = pl.multiple_of(n * qb, BLOCK)
        for h in range(DIL_HPG):
            hs = slice(HEAD_DIM * h, HEAD_DIM * (h + 1))
            blocks = [slice(j * BLOCK, (j + 1) * BLOCK) for j in range(nsub)]
            qhs = [q_ref[rows, hs] * SCALE for rows in blocks]
            kcats = [_band_operands(j, kc_ref, kp_ref, hs) for j in range(nsub)]
            dyhs = [dy_ref[rows, hs] for rows in blocks]
            scores = [_band_scores(qhs[j], kcats[j], _alibi_slope(group, h) * dilation, has_prev if j == 0 else None)
                      for j in range(nsub)]
            dps = [lax.dot_general(dyhs[j], _band_operands(j, vc_ref, vp_ref, hs), NT, preferred_element_type=F32)
                   for j in range(nsub)]
            pending = None

            def write(j, dq, dk, dv):
                dq_ref[blocks[j], hs] = (dq * SCALE).astype(dq_ref.dtype)
                win = pl.ds(base + j * BLOCK, 2 * BLOCK)
                dk_sc[win, hs] += dk
                dv_sc[win, hs] += dv

            for j in range(nsub):
                p = jnp.exp(scores[j] - L_ref[blocks[j], h:h + 1])
                ds = (p * (dps[j] - dl_ref[blocks[j], h:h + 1])).astype(BF16)
                dq = lax.dot_general(ds, kcats[j], NN, preferred_element_type=F32)
                dk = lax.dot_general(ds, qhs[j], TN, preferred_element_type=F32)
                dv = lax.dot_general(p.astype(BF16), dyhs[j], TN, preferred_element_type=F32)
                if pending is not None:
                    write(*pending)
                pending = (j, dq, dk, dv)
            write(*pending)

        @pl.when(n == nb - 1)
        def _():
            dk_ref[...] = dk_sc[BLOCK:, :].astype(dk_ref.dtype)
            dv_ref[...] = dv_sc[BLOCK:, :].astype(dv_ref.dtype)

    own = pl.BlockSpec((None, qb, DIL_GW), lambda s, n: (s, n, 0))
    own128 = pl.BlockSpec((None, qb, LANES), lambda s, n: (s, n, 0))
    whole = pl.BlockSpec((None, Ls, DIL_GW), lambda s, n: (s, 0, 0))
    shape = jax.ShapeDtypeStruct((nseq, Ls, DIL_GW), BF16)
    return pl.pallas_call(
        body, name=f"dil_backward_{group}",
        out_shape=[shape, shape, shape],
        grid=(nseq, nb),
        in_specs=_dil_specs(Ls, qb, cols) + [own128, own, own128],
        out_specs=[own, whole, whole],
        scratch_shapes=[pltpu.VMEM((Ls + BLOCK, DIL_GW), F32), pltpu.VMEM((Ls + BLOCK, DIL_GW), F32)],
        compiler_params=_params(("parallel", "arbitrary")),
    )(src, src, src, src, src, Lr, dyr, dlr)


def _dil_combine(os_, lses, T, S):
    def fn(o0, o1, o2, l0, l1, l2):
        m = jnp.maximum(jnp.maximum(l0, l1), l2)
        e0, e1, e2 = jnp.exp(l0 - m), jnp.exp(l1 - m), jnp.exp(l2 - m)
        tot = e0 + e1 + e2
        w0, w1, w2 = e0 / tot, e1 / tot, e2 / tot
        parts = []
        for h in range(DIL_HPG):
            hs = slice(HEAD_DIM * h, HEAD_DIM * (h + 1))
            parts.append(w0[:, h:h + 1] * o0[:, hs] + w1[:, h:h + 1] * o1[:, hs] + w2[:, h:h + 1] * o2[:, hs])
        return [jnp.concatenate(parts, axis=1), m + jnp.log(tot)]

    ins = [(a, "row", None) for a in os_] + [(a, "row", None) for a in lses]
    return _rowwise("dil_combine", fn, T, _div(S, 512, 8), ins, [("row", DIL_GW, BF16), ("row", LANES, F32)], S)


def _dil_delta(dy, y, T, S):
    def fn(dyv, yv):
        prod = dyv * yv.astype(F32)
        lane = lax.broadcasted_iota(jnp.int32, (dyv.shape[0], LANES), 1)
        delta = jnp.zeros((dyv.shape[0], LANES), F32)
        for h in range(DIL_HPG):
            hs = slice(HEAD_DIM * h, HEAD_DIM * (h + 1))
            delta = jnp.where(lane == h, jnp.sum(prod[:, hs], axis=-1, keepdims=True), delta)
        return [delta, dyv]

    return _rowwise("dil_delta", fn, T, _div(S, 512, 8), [(dy, "row", None), (y, "row", None)],
                    [("row", LANES, F32), ("row", DIL_GW, BF16)], S)


def _ada_forward(c_all, w, b):
    n, D = c_all.shape
    cl = w.shape[1]

    def body(c_ref, w_ref, b_ref, o_ref, ca_ref):
        cv = c_ref[...]
        ca = (cv * _sigmoid(cv)).astype(BF16)
        ca_ref[...] = ca
        o_ref[...] = jnp.dot(ca, w_ref[...].astype(BF16), preferred_element_type=F32) + b_ref[...]

    return pl.pallas_call(
        body, name="ada_forward",
        out_shape=[jax.ShapeDtypeStruct((n, cl), F32), jax.ShapeDtypeStruct((n, D), BF16)],
        compiler_params=_params(),
    )(c_all, w, b)


def _ada_backward(ca, dmod_cols, dmod_all):
    n, D = ca.shape
    cl = dmod_cols.shape[1]

    def body(ca_ref, dc_ref, da_ref, gw_ref, gb_ref):
        gw_ref[...] = lax.dot_general(ca_ref[...], dc_ref[...].astype(BF16), (((0,), (0,)), ((), ())), preferred_element_type=F32)
        gb_ref[...] = _colsum(da_ref[...])

    return pl.pallas_call(
        body, name="ada_backward",
        out_shape=[jax.ShapeDtypeStruct((D, cl), F32), jax.ShapeDtypeStruct((1, dmod_all.shape[1]), F32)],
        compiler_params=_params(),
    )(ca, dmod_cols, dmod_all)


def _sum_devices(v):
    def body(v_ref, o_ref):
        tot = v_ref[0]
        for k in range(1, N_DEV):
            tot = tot + v_ref[k]
        o_ref[...] = tot

    return pl.pallas_call(body, name="sum_devices", out_shape=jax.ShapeDtypeStruct(v.shape[1:], F32))(v)


def _adamw(name, w, g, m, v):
    rows, cols = w.shape
    tr = _div(rows, 256, 8)

    def body(w_ref, g_ref, m_ref, v_ref, d_ref, nm_ref, nv_ref):
        gv = g_ref[...]
        nm = ADAM_B1 * m_ref[...] + (1.0 - ADAM_B1) * gv
        nv = ADAM_B2 * v_ref[...] + (1.0 - ADAM_B2) * (gv * gv)
        m_hat = nm / (1.0 - ADAM_B1 ** ADAM_STEP)
        v_hat = nv / (1.0 - ADAM_B2 ** ADAM_STEP)
        d_ref[...] = -ADAM_LR * (m_hat / (jnp.sqrt(v_hat) + ADAM_EPS) + ADAM_WD * w_ref[...])
        nm_ref[...] = nm
        nv_ref[...] = nv

    spec = pl.BlockSpec((tr, cols), lambda i: (i, 0))
    shape = jax.ShapeDtypeStruct((rows, cols), F32)
    return pl.pallas_call(
        body, name=name, out_shape=[shape, shape, shape], grid=(rows // tr,),
        in_specs=[spec, spec, spec, spec], out_specs=[spec, spec, spec],
        compiler_params=_params(("arbitrary",)),
    )(w, g, m, v)


def _pad_rows(a, rows):
    return a if a.shape[0] == rows else jnp.pad(a, ((0, rows - a.shape[0]), (0, 0)))


class _Packed:
    def __init__(self, kind, local_shape, D):
        self.kind, self.local_shape, self.D = kind, local_shape, D
        r, c = local_shape
        self.rows = {"T": c, "N": r, "F": r * c // D}[kind]
        self.rows_pad = -(-self.rows // ROW_ALIGN) * ROW_ALIGN

    def pack_local(self, w):
        if self.kind == "T":
            w = w.T
        elif self.kind == "F":
            w = w.reshape(self.rows, self.D)
        return _pad_rows(w, self.rows_pad)

    def full(self, gathered):
        g = gathered[:, :self.rows]
        if self.kind == "F":
            r, c = self.local_shape
            return g.reshape(N_DEV, r, c).transpose(1, 0, 2).reshape(r, N_DEV * c)
        return g.reshape(N_DEV * self.rows, self.D)

    def pack_grad(self, gfull):
        if self.kind == "F":
            r, c = self.local_shape
            g = gfull.reshape(r, N_DEV, c).transpose(1, 0, 2).reshape(N_DEV, self.rows, self.D)
        else:
            g = gfull.reshape(N_DEV, self.rows, self.D)
        if self.rows_pad != self.rows:
            g = jnp.pad(g, ((0, 0), (0, self.rows_pad - self.rows), (0, 0)))
        return g

    def unpack_local(self, g):
        g = g[:self.rows]
        if self.kind == "T":
            return g.T
        if self.kind == "F":
            return g.reshape(self.local_shape)
        return g


BIG = ["ffn1_w_gate", "ffn1_w_up", "ffn1_w_down", "w_in", "w_branch_a", "w_branch_b", "w_out",
       "ffn2_w_gate", "ffn2_w_up", "ffn2_w_down"]
BIG_KIND = {"ffn1_w_gate": "T", "ffn1_w_up": "T", "ffn1_w_down": "N", "w_in": "T", "w_branch_a": "F", "w_branch_b": "F",
            "w_out": "N", "ffn2_w_gate": "T", "ffn2_w_up": "T", "ffn2_w_down": "N"}
GROUPS = (("ffn1_w_gate", "ffn1_w_up", "ffn1_w_down"), ("w_in", "w_branch_a", "w_branch_b", "w_out"),
          ("ffn2_w_gate", "ffn2_w_up", "ffn2_w_down"))
SMALL = ["ada_b", "norm_ffn1", "norm_mix", "forget_bias", "norm_ffn2", "norm_final"]


def kernel(x, c, ada_w, ada_b, norm_ffn1, ffn1_w_gate, ffn1_w_up, ffn1_w_down, norm_mix, w_in, forget_bias, w_branch_a, w_branch_b, w_out, norm_ffn2, ffn2_w_gate, ffn2_w_up, ffn2_w_down, norm_final, loss_target, m_ada_w, m_ada_b, m_norm_ffn1, m_ffn1_w_gate, m_ffn1_w_up, m_ffn1_w_down, m_norm_mix, m_w_in, m_forget_bias, m_w_branch_a, m_w_branch_b, m_w_out, m_norm_ffn2, m_ffn2_w_gate, m_ffn2_w_up, m_ffn2_w_down, m_norm_final, v_ada_w, v_ada_b, v_norm_ffn1, v_ffn1_w_gate, v_ffn1_w_up, v_ffn1_w_down, v_norm_mix, v_w_in, v_forget_bias, v_w_branch_a, v_w_branch_b, v_w_out, v_norm_ffn2, v_ffn2_w_gate, v_ffn2_w_up, v_ffn2_w_down, v_norm_final):
    args = dict(locals())
    B, S, D = x.shape
    T = B * S
    cl = ada_w.shape[2]
    n_in = w_in.shape[2] * N_DEV
    nm = 2 * D + 3 * FOX_W + 3 * DIL_W
    nmp = -(-nm // 512) * 512
    GA, GB, QB, QA = 0, D, 2 * D, 2 * D + 3 * FOX_W
    xpos, ypos, cpos = _position()
    me = 4 * xpos + 2 * ypos + cpos

    packs = {n: _Packed(BIG_KIND[n], args[n].shape[1:], D) for n in BIG}
    offs, pads = {}, {}
    for names in GROUPS:
        r = 0
        for n in names:
            offs[n] = r
            r += packs[n].rows_pad
        pads[names] = -r % PACK_ROW_QUANTUM

    def pack_weights(names):
        return jnp.concatenate([packs[n].pack_local(args[n][0]).astype(BF16) for n in names]
                               + [jnp.zeros((pads[names], D), BF16)], axis=0)

    def unpack_weights(names, land):
        out = {}
        for n in names:
            p = packs[n]
            if p.kind in "TN" and p.rows == p.rows_pad and offs[n] % p.rows == 0:
                out[n] = _Slab(land, offs[n], p.rows)
            else:
                out[n] = p.full(land[:, offs[n]:offs[n] + p.rows_pad])
        return out

    def pack_grads(names, gfull):
        return jnp.concatenate([packs[n].pack_grad(gfull[n]) for n in names] + [jnp.zeros((N_DEV, pads[names], D), F32)], axis=1)

    def unpack_grads(names, g_local):
        return {n: packs[n].unpack_local(g_local[offs[n]:offs[n] + packs[n].rows_pad])[None] for n in names}

    W = unpack_weights(GROUPS[0], _weight_allgather(pack_weights(GROUPS[0])))

    c_all = _small_allgather(c, "gather_c").reshape(N_DEV * B, D)
    b_cols = lax.dynamic_slice(ada_b, (0, me * cl), (1, cl))
    mod_cols, c_act = _ada_forward(c_all, ada_w[0], b_cols)
    mod_all = _small_allgather(mod_cols, "gather_mod").transpose(1, 0, 2).reshape(N_DEV * B, N_MOD * D)
    mod = lax.dynamic_slice(mod_all, (me * B, 0), (B, N_MOD * D)).reshape(B, N_MOD, 1, D)
    sh1, sc1, gt1, sh2, sc2, gt2, sh3, sc3, gt3 = [mod[:, i] for i in range(N_MOD)]

    x0 = x.reshape(T, D)
    x1, saved1, land = _ffn_forward("ffn1", x0, norm_ffn1, sh1, sc1, gt1, W["ffn1_w_gate"], W["ffn1_w_up"], W["ffn1_w_down"], S,
                                    gather=pack_weights(GROUPS[1]))
    W.update(unpack_weights(GROUPS[1], land))
    winT = W["w_in"]
    o_f = 3 * DIL_W + 3 * FOX_W
    wmT = jnp.concatenate([winT[o_f + 8:], winT[3 * DIL_W:o_f], winT[:3 * DIL_W], jnp.zeros((nmp - nm, D), BF16)], axis=0)
    wfT = jnp.concatenate([winT[o_f:o_f + 8], jnp.zeros((LANES - 8, D), BF16)], axis=0)

    tm1k = _div(T, 1024, 8)
    fb = jnp.pad(forget_bias, ((0, 0), (0, LANES - FOX_HEADS)))

    def proj(accs, fbv):
        fl = accs[1] + fbv
        lane = lax.broadcasted_iota(jnp.int32, fl.shape, 1)
        ls = jnp.minimum(fl, 0.0) - jnp.log(1.0 + jnp.exp(-jnp.abs(fl)))
        return [accs[0], jnp.where(lane < FOX_HEADS, ls, 0.0), fl]

    tms = _div(S, 512, 8)
    h2, pm, logsig, flog, land = _norm_matmul("mix_proj", x1, norm_mix, sc2, sh2, [wmT, wfT], proj,
                                              [(nmp, BF16), (LANES, F32), (LANES, F32)], S, vecs=[fb],
                                              rider=_gather_direct_rider(pack_weights(GROUPS[2])))
    cum = _cumsum(logsig.reshape(B, S, LANES))
    cumT = cum[:, :, :8].transpose(0, 2, 1)
    pm3 = pm.reshape(B, S, nmp)
    tq = _div(S, 512, LANES)
    qcol, kcol, vcol = QB // FOX_W, QB // FOX_W + 1, QB // FOX_W + 2
    o_b, o_b32, lse_b, land = _fox_fwd(pm3, cumT, qcol, kcol, vcol, tq, rider=_gather_forward_rider(land))
    W.update(unpack_weights(GROUPS[2], land))
    y_b = o_b.reshape(T, FOX_W)

    qa_blk = QA // DIL_GW
    dil_src, dil_cols = [], []
    for g, (_, d) in enumerate(DIL_GROUPS):
        if d == 1:
            dil_src.append(pm3)
            dil_cols.append((qa_blk + g, qa_blk + N_DIL + g, qa_blk + 2 * N_DIL + g))
        else:
            starts = [QA + (i * N_DIL + g) * DIL_GW for i in range(3)]
            qkv = jnp.concatenate([pm[:, c:c + DIL_GW] for c in starts], axis=1)
            dil_src.append(_residue_order(qkv, B, S, d))
            dil_cols.append((0, 1, 2))
    dil_o, dil_lse = [], []
    for g, (_, d) in enumerate(DIL_GROUPS):
        o_g, lse_g = _dil_fwd(g, dil_src[g], dil_cols[g])
        dil_o.append(_token_order(o_g, B, S, d))
        dil_lse.append(_token_order(lse_g, B, S, d))
    y_a, L_a = _dil_combine(dil_o, dil_lse, T, S)

    wa, wb, wout = W["w_branch_a"], W["w_branch_b"], W["w_out"]
    tnd = D
    tm5 = _div(T, 512, 8)
    yap = _matmul("mix_branch_a", "nn", [[(y_a, wa)]], T, D, DIL_GW, tm5, tnd, DIL_GW, [BF16])[0]

    def merge(accs, ex):
        yapv, gav, gbv = ex
        ybp = accs[0]
        return [ybp, _sigmoid(gav.astype(F32)) * yapv.astype(F32) + _sigmoid(gbv.astype(F32)) * ybp]

    ybp, merged = _matmul("mix_branch_b", "nn", [[(y_b, wb)]], T, D, FOX_W, tm5, tnd, FOX_W, [BF16, BF16],
                          extras=[(yap, "tile", 0), (pm, "tile", GA), (pm, "tile", GB)], epilogue=merge)

    def out_proj(accs, ex):
        xv, gtv = ex
        return [xv + gtv * accs[0], accs[0]]

    x2, ymix = _matmul("mix_out", "nn", [[(merged, wout)]], T, D, D, tms, tnd, D, [F32, BF16],
                       extras=[(x1, "tile", 0), (gt2, "brow", 0)], epilogue=out_proj, rows_per_example=S)

    x3, saved3, _ = _ffn_forward("ffn2", x2, norm_ffn2, sh3, sc3, gt3, W["ffn2_w_gate"], W["ffn2_w_up"], W["ffn2_w_down"], S)

    dx3, loss_b, dg_final = _loss_head(x3, loss_target.reshape(T, D), norm_final.reshape(1, D), S)
    dx2, (dsh3, dsc3, dgt3, dg3), (dwg2, dwu2, dwd2), _ = _ffn_backward(
        "ffn2", dx3, saved3, norm_ffn2, sc3, gt3, W["ffn2_w_gate"], W["ffn2_w_up"], W["ffn2_w_down"], S)
    g3 = pack_grads(GROUPS[2], {"ffn2_w_gate": dwg2, "ffn2_w_up": dwu2, "ffn2_w_down": dwd2})

    def merge_grad(dm, ex):
        gav, gbv, yapv, ybpv = [e.astype(F32) for e in ex]
        sga, sgb = _sigmoid(gav), _sigmoid(gbv)
        return [dm * sga, dm * sgb, dm * yapv * sga * (1.0 - sga), dm * ybpv * sgb * (1.0 - sgb)]

    dym, dgt2, dyap, dybp, dga, dgb, sib3 = _gated_grad_matmul(
        "mix_merge_grad", dx2, ymix, gt2, 1.0, wout, [(pm, GA // D), (pm, GB // D), (yap, 0), (ybp, 0)], merge_grad, [BF16] * 4, S,
        rider=_sibling_exchange_rider(g3))
    sums3, own3 = _chip_sums(g3, sib3)
    tkw = _div(T, 512, LANES)
    dwout = _matmul("mix_dw_out", "tn", [[(merged, dym)]], D, D, T, D, D, tkw, [F32])[0]
    dwa = _matmul("mix_dw_a", "tn", [[(y_a, dyap)]], DIL_GW, D, T, DIL_GW, D, tkw, [F32])[0]
    dwb = _matmul("mix_dw_b", "tn", [[(y_b, dybp)]], FOX_W, D, T, FOX_W, D, tkw, [F32])[0]
    dy_a = _matmul("mix_dy_a", "nt", [[(dyap, wa)]], T, DIL_GW, D, tm1k, DIL_GW, D, [F32])[0]
    dy_b = _matmul("mix_dy_b", "nt", [[(dybp, wb)]], T, FOX_W, D, tm1k, FOX_W, D, [BF16])[0]

    do3 = dy_b.reshape(B, S, FOX_W)
    delta_b = _fox_delta(dy_b, o_b32.reshape(T, FOX_W), T, S).reshape(B, S, LANES)
    dq_b, ds_rows, dk_b, dv_b, ds_cols, recv3 = _fox_bwd(pm3, do3, delta_b, lse_b, cumT, qcol, kcol, vcol, tq,
                                                         rider=_chip_exchange_rider(sums3))

    delta_a, dy_a16 = _dil_delta(dy_a, y_a, T, S)
    dqs, dks, dvs = [], [], []
    for g, (_, d) in enumerate(DIL_GROUPS):
        dq_g, dk_g, dv_g = _dil_bwd(g, dil_src[g], dil_cols[g], _residue_order(L_a, B, S, d),
                                    _residue_order(dy_a16, B, S, d), _residue_order(delta_a, B, S, d))
        dqs.append(_token_order(dq_g, B, S, d))
        dks.append(_token_order(dk_g, B, S, d))
        dvs.append(_token_order(dv_g, B, S, d))

    dcum = ds_rows - ds_cols
    dcum_run = _cumsum(dcum)
    dcum_tot = dcum_run[:, S - 1:S, :]

    def forget_grad_fn(run, dcv, fl, tot):
        lane = lax.broadcasted_iota(jnp.int32, fl.shape, 1)
        df = jnp.where(lane < FOX_HEADS, (tot - run + dcv) * _sigmoid_exp(-fl), 0.0)
        return [df, _colsum(df)]

    df16, dfb = _rowwise("forget_gate_grad", forget_grad_fn, T, tms,
                         [(dcum_run.reshape(T, LANES), "row", None), (dcum.reshape(T, LANES), "row", None), (flog, "row", None),
                          (dcum_tot, "bvec", None)],
                         [("row", LANES, BF16), ("bacc", LANES, F32)], S)

    dpm = jnp.concatenate([dga, dgb, dq_b.reshape(T, FOX_W), dk_b.reshape(T, FOX_W), dv_b.reshape(T, FOX_W)]
                          + dqs + dks + dvs + ([jnp.zeros((T, nmp - nm), BF16)] if nmp > nm else []), axis=1)
    tmn = _div(nmp, 2048, LANES)
    dwmT = _matmul("mix_dw_in", "tn", [[(dpm, h2)]], nmp, D, T, tmn, D, tkw, [F32])[0]
    dwfT = _matmul("mix_dw_f", "tn", [[(df16, h2)]], LANES, D, T, LANES, D, tkw, [F32])[0]
    dwinT = jnp.concatenate([dwmT[QA:QA + 3 * DIL_W], dwmT[QB:QB + 3 * FOX_W], dwfT[:8], dwmT[GA:2 * D]], axis=0)
    g2 = pack_grads(GROUPS[1], {"w_in": dwinT, "w_branch_a": dwa, "w_branch_b": dwb, "w_out": dwout})
    dx1, dsh2, dsc2, dgmix, sib2 = _matmul_normmod_bwd("mix_dh", [(dpm, wmT), (df16, wfT)], x1, norm_mix, sc2, dx2, S,
                                                       rider=_sibling_exchange_rider(g2))
    sums2, own2 = _chip_sums(g2, sib2)

    own1 = []

    def ffn1_exchange(dwg, dwu, dwd):
        sums1, own = _chip_sums(pack_grads(GROUPS[0], {"ffn1_w_gate": dwg, "ffn1_w_up": dwu, "ffn1_w_down": dwd}))
        own1.append(own)
        return _chip_exchange_rider(sums1)

    dx0, (dsh1, dsc1, dgt1, dg1), _, (recv2, recv1) = _ffn_backward(
        "ffn1", dx1, saved1, norm_ffn1, sc1, gt1, W["ffn1_w_gate"], W["ffn1_w_up"], W["ffn1_w_down"], S,
        rider=_chip_exchange_rider(sums2), dh_rider=ffn1_exchange)
    own1 = own1[0]
    grad_x = dx0.reshape(B, S, D)

    dmod = jnp.concatenate([dsh1, dsc1, dgt1, dsh2, dsc2, dgt2, dsh3, dsc3, dgt3], axis=1).reshape(B, N_MOD * D)
    fbg = jnp.sum(dfb, axis=0)
    small = jnp.concatenate([jnp.sum(dg1, axis=0), jnp.sum(dgmix, axis=0), jnp.sum(dg3, axis=0), jnp.sum(dg_final, axis=0),
                             fbg, jnp.sum(loss_b, axis=0)], axis=1)
    n_small = small.shape[1]
    tail = _small_allgather(jnp.concatenate([dmod, jnp.pad(small, ((0, 0), (0, N_MOD * D - n_small)))], axis=0), "gather_tail")
    dmod_all = tail[:, :B].reshape(N_DEV * B, N_MOD * D)
    dmod_cols = lax.dynamic_slice(dmod_all, (0, me * cl), (N_DEV * B, cl))
    g_ada_w, g_ada_b = _ada_backward(c_act, dmod_cols, dmod_all)

    small = _sum_devices(tail[:, B:, :n_small])
    g_small = {"norm_ffn1": small[:, 0:D], "norm_mix": small[:, D:2 * D], "norm_ffn2": small[:, 2 * D:3 * D],
               "norm_final": small[:, 3 * D:4 * D], "forget_bias": small[:, 4 * D:4 * D + FOX_HEADS], "ada_b": g_ada_b}
    loss = small[0, 4 * D + LANES]

    grads = {"ada_w": g_ada_w[None]}
    for names, own, recv in ((GROUPS[0], own1, recv1), (GROUPS[1], own2, recv2), (GROUPS[2], own3, recv3)):
        grads.update(unpack_grads(names, _final_grad_sum(own, recv)))

    delta, new_m, new_v = {}, {}, {}
    for n in ["ada_w"] + BIG:
        shp = args[n].shape
        d_, m_, v_ = _adamw(f"adamw_{n}", args[n][0], grads[n][0], args["m_" + n][0], args["v_" + n][0])
        delta[n], new_m[n], new_v[n] = d_.reshape(shp), m_.reshape(shp), v_.reshape(shp)
    sizes = [args[n].size for n in SMALL]
    tot = sum(sizes)
    padded = -(-tot // (8 * LANES)) * (8 * LANES)

    def flat(get):
        v = jnp.concatenate([get(n).reshape(-1) for n in SMALL])
        return jnp.pad(v, (0, padded - tot)).reshape(8, padded // 8)

    d_s, m_s, v_s = _adamw("adamw_small", flat(lambda n: args[n]), flat(lambda n: g_small[n]), flat(lambda n: args["m_" + n]),
                           flat(lambda n: args["v_" + n]))
    o = 0
    for n, sz in zip(SMALL, sizes):
        shp = args[n].shape
        grads[n] = g_small[n].reshape(shp)
        delta[n] = d_s.reshape(-1)[o:o + sz].reshape(shp)
        new_m[n] = m_s.reshape(-1)[o:o + sz].reshape(shp)
        new_v[n] = v_s.reshape(-1)[o:o + sz].reshape(shp)
        o += sz

    order = ["ada_w", "ada_b", "norm_ffn1", "ffn1_w_gate", "ffn1_w_up", "ffn1_w_down", "norm_mix", "w_in", "forget_bias",
             "w_branch_a", "w_branch_b", "w_out", "norm_ffn2", "ffn2_w_gate", "ffn2_w_up", "ffn2_w_down", "norm_final"]
    return (loss, grad_x, *[grads[n] for n in order], *[delta[n] for n in order], *[new_m[n] for n in order],
            *[new_v[n] for n in order])
```

```python
import functools
import math

import jax
import jax.numpy as jnp
from jax import lax
from jax.experimental import pallas as pl
from jax.experimental.pallas import tpu as pltpu

F32 = jnp.float32
BF16 = jnp.bfloat16
MESH = pl.DeviceIdType.MESH
ANY = pl.BlockSpec(memory_space=pl.ANY)
VMEM_SPEC = pl.BlockSpec(memory_space=pltpu.VMEM)

N_DEV = 8
HEAD_DIM = 64
BLOCK = 128
DIL_GROUPS = ((128, 1), (512, 4), (2048, 16))
N_DIL = len(DIL_GROUPS)
DIL_HPG = 4
DIL_GW = DIL_HPG * HEAD_DIM
DIL_W = N_DIL * DIL_GW
FOX_HEADS = 8
FOX_W = FOX_HEADS * HEAD_DIM
N_MOD = 9
RMS_EPS = 1e-6
ALIBI_MAX_BIAS = 8.0
NEG_INF = -1e30
ADAM_LR, ADAM_B1, ADAM_B2, ADAM_EPS, ADAM_WD, ADAM_STEP = 0.001, 0.9, 0.999, 1e-08, 0.01, 10
V7X_VMEM_LIMIT = 52 * 1024 * 1024
LANES = 128
ROW_ALIGN = 16
PACK_ROW_QUANTUM = 32
FOX_STRIP = 32
SCALE = 1.0 / math.sqrt(HEAD_DIM)


def _div(dim, target, quantum):
    best = None
    for t in range(quantum, min(dim, target) + 1, quantum):
        if dim % t == 0:
            best = t
    return best or dim


def _params(sem=None):
    return pltpu.CompilerParams(dimension_semantics=sem, vmem_limit_bytes=V7X_VMEM_LIMIT)


def _sigmoid(x):
    return 0.5 * jnp.tanh(0.5 * x) + 0.5


def _sigmoid_exp(x):
    return 1.0 / (1.0 + jnp.exp(-x))


def _position():
    x, y, c = lax.axis_index("x"), lax.axis_index("y"), lax.axis_index("c")
    return x, y, c


def _small_allgather(v, name):
    rows, cols = v.shape

    def body(v_ref, out_ref, send_sems, recv_sems):
        x, y, c = _position()
        me = 4 * x + 2 * y + c
        out_ref[me] = v_ref[...]

        def peer(k):
            return (1 - x if k & 4 else x, 1 - y if k & 2 else y, 1 - c if k & 1 else c)

        def copy(k, slot):
            return pltpu.make_async_remote_copy(
                src_ref=v_ref, dst_ref=out_ref.at[slot], send_sem=send_sems.at[k - 1], recv_sem=recv_sems.at[k - 1],
                device_id=peer(k), device_id_type=MESH)

        sends = [copy(k, me) for k in range(1, N_DEV)]
        for cp in sends:
            cp.start()
        for k in range(1, N_DEV):
            px, py, pc = peer(k)
            copy(k, 4 * px + 2 * py + pc).wait_recv()
        for cp in sends:
            cp.wait_send()

    return pl.pallas_call(
        body, name=name,
        out_shape=jax.ShapeDtypeStruct((N_DEV, rows, cols), v.dtype),
        in_specs=[VMEM_SPEC], out_specs=VMEM_SPEC,
        scratch_shapes=[pltpu.SemaphoreType.DMA((N_DEV - 1,)), pltpu.SemaphoreType.DMA((N_DEV - 1,))],
    )(v)


def _weight_allgather(p):
    rows, cols = p.shape

    def body(p_ref, out_ref, send_sems, recv_sems, local_sem):
        x, y, c = _position()
        me, sibling = (x, y, c), (x, y, 1 - c)
        chips = [(1 - x, y), (x, 1 - y), (1 - x, 1 - y)]

        def slot(px, py, pc):
            return out_ref.at[4 * px + 2 * py + pc]

        def copy(k, block, to, src=None):
            return pltpu.make_async_remote_copy(
                src_ref=slot(*block) if src is None else src, dst_ref=slot(*block),
                send_sem=send_sems.at[k], recv_sem=recv_sems.at[k], device_id=to, device_id_type=MESH)

        mine = pltpu.make_async_copy(p_ref, slot(*me), local_sem)
        mine.start()
        first = [copy(0, me, sibling, src=p_ref)]
        first += [copy(1 + j, me, (*chip, c), src=p_ref) for j, chip in enumerate(chips)]
        for cp in first:
            cp.start()
        passed = [copy(4 + j, (*chip, c), sibling) for j, chip in enumerate(chips)]
        for j, chip in enumerate(chips):
            copy(1 + j, (*chip, c), me).wait_recv()
            passed[j].start()
        copy(0, sibling, me).wait_recv()
        for j, chip in enumerate(chips):
            copy(4 + j, (*chip, 1 - c), me).wait_recv()
        for cp in first + passed:
            cp.wait_send()
        mine.wait()

    return pl.pallas_call(
        body, name="weight_allgather",
        out_shape=jax.ShapeDtypeStruct((N_DEV, rows, cols), p.dtype),
        in_specs=[ANY], out_specs=ANY,
        scratch_shapes=[pltpu.SemaphoreType.DMA((7,)), pltpu.SemaphoreType.DMA((7,)), pltpu.SemaphoreType.DMA],
    )(p)


def _grad_exchange_sibling(g):
    _, rows, cols = g.shape

    def body(g_ref, out_ref, send_sems, recv_sems):
        x, y, c = _position()
        sibling = (x, y, 1 - c)

        def copy(q):
            px, py = q >> 1, q & 1
            return pltpu.make_async_remote_copy(
                src_ref=g_ref.at[4 * px + 2 * py + (1 - c)], dst_ref=out_ref.at[q],
                send_sem=send_sems.at[q], recv_sem=recv_sems.at[q], device_id=sibling, device_id_type=MESH)

        copies = [copy(q) for q in range(4)]
        for cp in copies:
            cp.start()
        for cp in copies:
            cp.wait_recv()
        for cp in copies:
            cp.wait_send()

    return pl.pallas_call(
        body, name="grad_exchange_sibling",
        out_shape=jax.ShapeDtypeStruct((4, rows, cols), g.dtype),
        in_specs=[ANY], out_specs=ANY,
        scratch_shapes=[pltpu.SemaphoreType.DMA((4,)), pltpu.SemaphoreType.DMA((4,))],
    )(g)


class _Rider:
    def __init__(self, operands, out_shapes, n_send, n_recv, start, finish, aliases=None):
        self.operands, self.out_shapes = list(operands), list(out_shapes)
        self.n_send, self.n_recv, self.start, self.finish = n_send, n_recv, start, finish
        self.aliases = aliases or {}


def _pcall(body, *, name, grid, in_specs, operands, out_shape, out_specs, scratch_shapes, params, rider=None):
    if rider is None:
        return pl.pallas_call(body, name=name, out_shape=out_shape, grid=grid, in_specs=in_specs, out_specs=out_specs,
                              scratch_shapes=scratch_shapes, compiler_params=params)(*operands)
    n_in, n_out, n_sc = len(operands), len(out_shape), len(scratch_shapes)
    r_in, r_out = len(rider.operands), len(rider.out_shapes)

    def wrapped(*refs):
        ins, rins = refs[:n_in], refs[n_in:n_in + r_in]
        outs, routs = refs[n_in + r_in:n_in + r_in + n_out], refs[n_in + r_in + n_out:n_in + r_in + n_out + r_out]
        rest = refs[n_in + r_in + n_out + r_out:]
        scratch, sems = rest[:n_sc], rest[n_sc:]
        ids = [pl.program_id(a) for a in range(len(grid))]
        first, last = ids[0] == 0, ids[0] == grid[0] - 1
        for a in range(1, len(grid)):
            first, last = first & (ids[a] == 0), last & (ids[a] == grid[a] - 1)

        @pl.when(first)
        def _():
            rider.start(rins, routs, *sems)

        body(*ins, *outs, *scratch)

        @pl.when(last)
        def _():
            rider.finish(rins, routs, *sems)

    return pl.pallas_call(
        wrapped, name=name, out_shape=list(out_shape) + rider.out_shapes, grid=grid,
        in_specs=list(in_specs) + [ANY] * r_in, out_specs=list(out_specs) + [ANY] * r_out,
        scratch_shapes=list(scratch_shapes) + [pltpu.SemaphoreType.DMA((rider.n_send,)), pltpu.SemaphoreType.DMA((rider.n_recv,))],
        input_output_aliases={n_in + i: n_out + o for i, o in rider.aliases.items()},
        compiler_params=params,
    )(*operands, *rider.operands)


def _flips(x, y, c):
    return [(x, y, 1 - c), (1 - x, y, c), (x, 1 - y, c), (1 - x, 1 - y, c)]


def _gather_direct_rider(p):
    rows, cols = p.shape

    def copies(p_ref, land, send_sems, recv_sems):
        x, y, c = _position()
        me = 4 * x + 2 * y + c
        peers = _flips(x, y, c)
        sends = [pltpu.make_async_remote_copy(src_ref=p_ref, dst_ref=land.at[me], send_sem=send_sems.at[k], recv_sem=recv_sems.at[k],
                                              device_id=to, device_id_type=MESH) for k, to in enumerate(peers)]
        recvs = [pltpu.make_async_remote_copy(src_ref=p_ref, dst_ref=land.at[4 * px + 2 * py + pc], send_sem=send_sems.at[k],
                                              recv_sem=recv_sems.at[k], device_id=(px, py, pc), device_id_type=MESH)
                 for k, (px, py, pc) in enumerate(peers)]
        mine = pltpu.make_async_copy(p_ref, land.at[me], send_sems.at[len(peers)])
        return sends, recvs, mine

    def start(rins, routs, send_sems, recv_sems):
        sends, _, mine = copies(rins[0], routs[0], send_sems, recv_sems)
        mine.start()
        for cp in sends:
            cp.start()

    def finish(rins, routs, send_sems, recv_sems):
        sends, recvs, mine = copies(rins[0], routs[0], send_sems, recv_sems)
        for cp in recvs:
            cp.wait_recv()
        for cp in sends:
            cp.wait_send()
        mine.wait()

    return _Rider([p], [jax.ShapeDtypeStruct((N_DEV, rows, cols), p.dtype)], 5, 4, start, finish)


def _gather_forward_rider(land):
    def copies(buf, send_sems, recv_sems):
        x, y, c = _position()
        chips = [(1 - x, y), (x, 1 - y), (1 - x, 1 - y)]
        sends = [pltpu.make_async_remote_copy(src_ref=buf.at[4 * px + 2 * py + c], dst_ref=buf.at[4 * px + 2 * py + c],
                                              send_sem=send_sems.at[k], recv_sem=recv_sems.at[k], device_id=(x, y, 1 - c),
                                              device_id_type=MESH) for k, (px, py) in enumerate(chips)]
        recvs = [pltpu.make_async_remote_copy(src_ref=buf.at[4 * px + 2 * py + 1 - c], dst_ref=buf.at[4 * px + 2 * py + 1 - c],
                                              send_sem=send_sems.at[k], recv_sem=recv_sems.at[k], device_id=(x, y, 1 - c),
                                              device_id_type=MESH) for k, (px, py) in enumerate(chips)]
        return sends, recvs

    def start(rins, routs, send_sems, recv_sems):
        for cp in copies(routs[0], send_sems, recv_sems)[0]:
            cp.start()

    def finish(rins, routs, send_sems, recv_sems):
        sends, recvs = copies(routs[0], send_sems, recv_sems)
        for cp in recvs:
            cp.wait_recv()
        for cp in sends:
            cp.wait_send()

    return _Rider([land], [jax.ShapeDtypeStruct(land.shape, land.dtype)], 3, 3, start, finish, aliases={0: 0})


def _chip_exchange_rider(s):
    def copies(s_ref, out_ref, send_sems, recv_sems):
        x, y, c = _position()
        chips = [(1 - x, y), (x, 1 - y), (1 - x, 1 - y)]
        return [pltpu.make_async_remote_copy(src_ref=s_ref.at[k], dst_ref=out_ref.at[k], send_sem=send_sems.at[k],
                                             recv_sem=recv_sems.at[k], device_id=(*chips[k], c), device_id_type=MESH)
                for k in range(3)]

    def start(rins, routs, send_sems, recv_sems):
        for cp in copies(rins[0], routs[0], send_sems, recv_sems):
            cp.start()

    def finish(rins, routs, send_sems, recv_sems):
        cps = copies(rins[0], routs[0], send_sems, recv_sems)
        for cp in cps:
            cp.wait_recv()
        for cp in cps:
            cp.wait_send()

    return _Rider([s], [jax.ShapeDtypeStruct(s.shape, s.dtype)], 3, 3, start, finish)


def _chip_partial_sums(g, recv_sib, jj, qq):
    _, rows, cols = g.shape
    tr = _div(rows, 512, ROW_ALIGN)

    def body(jj_ref, qq_ref, g_ref, r_ref, o_ref):
        o_ref[...] = (g_ref[...] + r_ref[...]).astype(o_ref.dtype)

    return pl.pallas_call(
        body, name="chip_partial_sums",
        out_shape=jax.ShapeDtypeStruct((3, rows, cols), BF16),
        grid_spec=pltpu.PrefetchScalarGridSpec(
            num_scalar_prefetch=2, grid=(3, rows // tr),
            in_specs=[pl.BlockSpec((None, tr, cols), lambda k, i, jj, qq: (jj[k], i, 0)),
                      pl.BlockSpec((None, tr, cols), lambda k, i, jj, qq: (qq[k], i, 0))],
            out_specs=pl.BlockSpec((None, tr, cols), lambda k, i, jj, qq: (k, i, 0))),
        compiler_params=_params(("arbitrary", "arbitrary")),
    )(jj, qq, g, recv_sib)


def _own_partial_sum(g, recv_sib, jj, qq):
    _, rows, cols = g.shape
    tr = _div(rows, 512, ROW_ALIGN)

    def body(jj_ref, qq_ref, g_ref, r_ref, o_ref):
        o_ref[...] = g_ref[...] + r_ref[...]

    return pl.pallas_call(
        body, name="own_partial_sum",
        out_shape=jax.ShapeDtypeStruct((rows, cols), F32),
        grid_spec=pltpu.PrefetchScalarGridSpec(
            num_scalar_prefetch=2, grid=(rows // tr,),
            in_specs=[pl.BlockSpec((None, tr, cols), lambda i, jj, qq: (jj[0], i, 0)),
                      pl.BlockSpec((None, tr, cols), lambda i, jj, qq: (qq[0], i, 0))],
            out_specs=pl.BlockSpec((tr, cols), lambda i, jj, qq: (i, 0))),
        compiler_params=_params(("arbitrary",)),
    )(jj, qq, g, recv_sib)


def _final_grad_sum(own, recv):
    rows, cols = own.shape
    tr = _div(rows, 512, ROW_ALIGN)

    def body(o_ref, r_ref, out_ref):
        out_ref[...] = ((o_ref[...] + r_ref[0].astype(F32)) + r_ref[1].astype(F32)) + r_ref[2].astype(F32)

    return pl.pallas_call(
        body, name="final_grad_sum",
        out_shape=jax.ShapeDtypeStruct((rows, cols), F32),
        grid=(rows // tr,),
        in_specs=[pl.BlockSpec((tr, cols), lambda i: (i, 0)), pl.BlockSpec((3, tr, cols), lambda i: (0, i, 0))],
        out_specs=pl.BlockSpec((tr, cols), lambda i: (i, 0)),
        compiler_params=_params(("arbitrary",)),
    )(own, recv)


def _sibling_exchange_rider(g):
    _, rows, cols = g.shape

    def copies(g_ref, out_ref, send_sems, recv_sems):
        x, y, c = _position()
        return [pltpu.make_async_remote_copy(
            src_ref=g_ref.at[4 * (q >> 1) + 2 * (q & 1) + (1 - c)], dst_ref=out_ref.at[q], send_sem=send_sems.at[q],
            recv_sem=recv_sems.at[q], device_id=(x, y, 1 - c), device_id_type=MESH) for q in range(4)]

    def start(rins, routs, send_sems, recv_sems):
        for cp in copies(rins[0], routs[0], send_sems, recv_sems):
            cp.start()

    def finish(rins, routs, send_sems, recv_sems):
        cps = copies(rins[0], routs[0], send_sems, recv_sems)
        for cp in cps:
            cp.wait_recv()
        for cp in cps:
            cp.wait_send()

    return _Rider([g], [jax.ShapeDtypeStruct((4, rows, cols), g.dtype)], 4, 4, start, finish)


def _chip_sums(g, recv_sib=None):
    x, y, c = _position()
    chips = [(1 - x, y), (x, 1 - y), (1 - x, 1 - y)]
    jj = jnp.stack([4 * px + 2 * py + c for px, py in chips]).astype(jnp.int32)
    qq = jnp.stack([2 * px + py for px, py in chips]).astype(jnp.int32)
    jme = jnp.reshape(4 * x + 2 * y + c, (1,)).astype(jnp.int32)
    qme = jnp.reshape(2 * x + y, (1,)).astype(jnp.int32)
    if recv_sib is None:
        recv_sib = _grad_exchange_sibling(g)
    return _chip_partial_sums(g, recv_sib, jj, qq), _own_partial_sum(g, recv_sib, jme, qme)


def _matmul(name, form, prods, M, N, K, tm, tn, tk, out_dtypes, extras=(), epilogue=None, rows_per_example=None, rider=None):
    nk = K // tk
    n_acc = len(prods)
    flat = [ab for group in prods for ab in group]
    dims = {"nn": (((1,), (0,)), ((), ())), "nt": (((1,), (1,)), ((), ())), "tn": (((0,), (0,)), ((), ()))}[form]
    direct = nk > 1 and epilogue is None and n_acc == 1 and list(out_dtypes) == [F32]

    def spec(shape, index_map, whole):
        if whole:
            return pl.BlockSpec(shape, index_map, pipeline_mode=pl.Buffered(1))
        return pl.BlockSpec(shape, index_map)

    if form == "tn":
        a_spec = spec((tk, tm), lambda i, j, k: (k, i), nk == 1 and M == tm)
    else:
        a_spec = spec((tm, tk), lambda i, j, k: (i, k), nk == 1 and M == tm)
    if form == "nt":
        b_spec = spec((tn, tk), lambda i, j, k: (j, k), nk == 1 and N == tn)
    else:
        b_spec = spec((tk, tn), lambda i, j, k: (k, j), nk == 1 and N == tn)
    in_specs, operands = [], []
    for a, b in flat:
        in_specs += [a_spec, _weight_spec(b) if isinstance(b, _Slab) else b_spec]
        operands += [a, _weight_operand(b)]
    for arr, kind, off in extras:
        if kind == "tile":
            assert off % tn == 0
            in_specs.append(pl.BlockSpec((tm, tn), functools.partial(lambda i, j, k, o: (i, j + o), o=off // tn)))
        else:
            tiles = rows_per_example // tm
            in_specs.append(pl.BlockSpec((None, 1, tn), functools.partial(lambda i, j, k, t: (i // t, 0, j), t=tiles)))
        operands.append(arr)
    n_in, n_out = len(operands), len(out_dtypes)

    def body(*refs):
        in_refs, out_refs, acc_refs = refs[:n_in], refs[n_in:n_in + n_out], refs[n_in + n_out:]
        k = pl.program_id(2)
        partials, p = [], 0
        for group in prods:
            tot = None
            for _ in group:
                d = lax.dot_general(in_refs[2 * p][...], _weight_value(in_refs[2 * p + 1], flat[p][1]), dims,
                                    preferred_element_type=F32)
                tot = d if tot is None else tot + d
                p += 1
            partials.append(tot)

        def finish(accs):
            ex = [r[...] for r in in_refs[2 * len(flat):]]
            outs = epilogue(accs, ex) if epilogue is not None else accs
            for r, o in zip(out_refs, outs):
                r[...] = o.astype(r.dtype)

        if nk == 1:
            finish(partials)
        elif direct:
            @pl.when(k == 0)
            def _():
                out_refs[0][...] = partials[0]

            @pl.when(k > 0)
            def _():
                out_refs[0][...] += partials[0]
        else:
            @pl.when(k == 0)
            def _():
                for r, v in zip(acc_refs, partials):
                    r[...] = v

            @pl.when(k > 0)
            def _():
                for r, v in zip(acc_refs, partials):
                    r[...] += v

            @pl.when(k == nk - 1)
            def _():
                finish([r[...] for r in acc_refs])

    return _pcall(
        body, name=name,
        out_shape=[jax.ShapeDtypeStruct((M, N), dt) for dt in out_dtypes],
        grid=(M // tm, N // tn, nk),
        in_specs=in_specs, operands=operands,
        out_specs=[pl.BlockSpec((tm, tn), lambda i, j, k: (i, j)) for _ in out_dtypes],
        scratch_shapes=[pltpu.VMEM((tm, tn), F32) for _ in range(n_acc)] if nk > 1 and not direct else [],
        params=_params(("parallel", "parallel", "arbitrary")), rider=rider)


def _rowwise(name, fn, T, tm, ins, outs, rows_per_example):
    tiles = rows_per_example // tm
    n_ex = T // rows_per_example
    in_specs, operands = [], []
    for arr, kind, arg in ins:
        if kind == "row":
            if arg is None:
                in_specs.append(pl.BlockSpec((tm, arr.shape[1]), lambda i: (i, 0)))
            else:
                in_specs.append(pl.BlockSpec((tm, arg[0]), functools.partial(lambda i, cb: (i, cb), cb=arg[1])))
        elif kind == "bvec":
            in_specs.append(pl.BlockSpec((None, 1, arr.shape[2]), lambda i: (i // tiles, 0, 0)))
        else:
            in_specs.append(pl.BlockSpec((1, arr.shape[1]), lambda i: (0, 0)))
        operands.append(arr)
    out_shape, out_specs = [], []
    for kind, cols, dt in outs:
        if kind == "row":
            out_shape.append(jax.ShapeDtypeStruct((T, cols), dt))
            out_specs.append(pl.BlockSpec((tm, cols), lambda i: (i, 0)))
        else:
            out_shape.append(jax.ShapeDtypeStruct((n_ex, 1, cols), F32))
            out_specs.append(pl.BlockSpec((None, 1, cols), lambda i: (i // tiles, 0, 0)))
    n_in = len(operands)

    def body(*refs):
        i = pl.program_id(0)
        vals = fn(*[r[...] for r in refs[:n_in]])
        for (kind, _, _), r, v in zip(outs, refs[n_in:], vals):
            if kind == "row":
                r[...] = v.astype(r.dtype)
            else:
                @pl.when(i % tiles == 0)
                def _():
                    r[...] = jnp.zeros_like(r)

                r[...] += v

    return pl.pallas_call(
        body, name=name, out_shape=out_shape, grid=(T // tm,), in_specs=in_specs, out_specs=out_specs,
        compiler_params=_params(("arbitrary",)),
    )(*operands)


def _colsum(v):
    return jnp.sum(v, axis=0, keepdims=True)


def _rms_parts(x):
    rstd = lax.rsqrt(jnp.mean(x * x, axis=-1, keepdims=True) + RMS_EPS)
    return x * rstd, rstd


def _resident(shape):
    return pl.BlockSpec(shape, lambda i: (0, 0), pipeline_mode=pl.Buffered(1))


class _Slab:
    def __init__(self, land, off, rows):
        assert off % rows == 0 and rows % ROW_ALIGN == 0
        self.land, self.off, self.rows = land, off, rows
        self.shape = (N_DEV * rows, land.shape[2])


def _weight_spec(w):
    if isinstance(w, _Slab):
        return pl.BlockSpec((N_DEV, w.rows, w.shape[1]), lambda *_: (0, w.off // w.rows, 0), pipeline_mode=pl.Buffered(1))
    return pl.BlockSpec(w.shape, lambda *_: (0, 0), pipeline_mode=pl.Buffered(1))


def _weight_operand(w):
    return w.land if isinstance(w, _Slab) else w


def _weight_value(ref, w):
    return ref[...].reshape(w.shape) if isinstance(w, _Slab) else ref[...]


def _example_acc(r, i, tiles, v):
    @pl.when(i % tiles == 0)
    def _():
        r[...] = jnp.zeros_like(r)

    r[...] += v


def _norm_matmul(name, x, g, sc, sh, weights, epilogue, outs, S, vecs=(), rider=None):
    T, D = x.shape
    tm = _div(S, 256, 8)
    tiles = S // tm
    nw, nv = len(weights), len(vecs)

    def body(*refs):
        x_ref, g_ref, sc_ref, sh_ref = refs[:4]
        w_refs, v_refs = refs[4:4 + nw], refs[4 + nw:4 + nw + nv]
        h_ref, out_refs = refs[4 + nw + nv], refs[5 + nw + nv:]
        xhat, _ = _rms_parts(x_ref[...])
        h = ((xhat * g_ref[...]) * (1.0 + sc_ref[...]) + sh_ref[...]).astype(BF16)
        h_ref[...] = h
        accs = [lax.dot_general(h, _weight_value(r, w), NT, preferred_element_type=F32) for r, w in zip(w_refs, weights)]
        for r, o in zip(out_refs, epilogue(accs, *[v[...] for v in v_refs])):
            r[...] = o.astype(r.dtype)

    bvec = pl.BlockSpec((None, 1, D), lambda i: (i // tiles, 0, 0))
    return _pcall(
        body, name=name,
        out_shape=[jax.ShapeDtypeStruct((T, D), BF16)] + [jax.ShapeDtypeStruct((T, w), dt) for w, dt in outs],
        grid=(T // tm,),
        in_specs=[pl.BlockSpec((tm, D), lambda i: (i, 0)), pl.BlockSpec((1, D), lambda i: (0, 0)), bvec, bvec]
        + [_weight_spec(w) for w in weights] + [pl.BlockSpec(v.shape, lambda i: (0, 0)) for v in vecs],
        operands=[x, g, sc, sh, *[_weight_operand(w) for w in weights], *vecs],
        out_specs=[pl.BlockSpec((tm, D), lambda i: (i, 0))] + [pl.BlockSpec((tm, w), lambda i: (i, 0)) for w, _ in outs],
        scratch_shapes=[], params=_params(("arbitrary",)), rider=rider)


def _gated_grad_matmul(name, dx, y, gt, coeff, w, tiles_in, epilogue, outs, S, rider=None):
    T, D = dx.shape
    N = w.shape[0]
    tm = _div(S, 256, 8)
    tiles = S // tm
    nt = len(tiles_in)

    def body(*refs):
        dx_ref, y_ref, gt_ref, w_ref = refs[:4]
        t_refs, dy_ref, dgt_ref, out_refs = refs[4:4 + nt], refs[4 + nt], refs[5 + nt], refs[6 + nt:]
        i = pl.program_id(0)
        dxv = dx_ref[...]
        dy = (coeff * gt_ref[...] * dxv).astype(BF16)
        dy_ref[...] = dy
        _example_acc(dgt_ref, i, tiles, _colsum(coeff * dxv * y_ref[...].astype(F32)))
        acc = lax.dot_general(dy, _weight_value(w_ref, w), NT, preferred_element_type=F32)
        for r, o in zip(out_refs, epilogue(acc, [t[...] for t in t_refs])):
            r[...] = o.astype(r.dtype)

    row = pl.BlockSpec((tm, D), lambda i: (i, 0))
    bvec = pl.BlockSpec((None, 1, D), lambda i: (i // tiles, 0, 0))
    return _pcall(
        body, name=name,
        out_shape=[jax.ShapeDtypeStruct((T, D), BF16), jax.ShapeDtypeStruct((T // S, 1, D), F32)]
        + [jax.ShapeDtypeStruct((T, N), dt) for dt in outs],
        grid=(T // tm,),
        in_specs=[row, row, bvec, _weight_spec(w)]
        + [pl.BlockSpec((tm, N), functools.partial(lambda i, cb: (i, cb), cb=cb)) for _, cb in tiles_in],
        operands=[dx, y, gt, _weight_operand(w), *[t for t, _ in tiles_in]],
        out_specs=[row, bvec] + [pl.BlockSpec((tm, N), lambda i: (i, 0)) for _ in outs],
        scratch_shapes=[], params=_params(("arbitrary",)), rider=rider)


def _matmul_normmod_bwd(name, prods, x, g, sc, dres, S, rider=None):
    T, D = x.shape
    tm = _div(S, 256, 8)
    tiles = S // tm
    npr = len(prods)

    def body(*refs):
        ab = refs[:2 * npr]
        x_ref, g_ref, sc_ref, dr_ref = refs[2 * npr:2 * npr + 4]
        dx_ref, dsh_ref, dsc_ref, dg_ref = refs[2 * npr + 4:]
        i = pl.program_id(0)
        dh = None
        for p in range(npr):
            d = lax.dot_general(ab[2 * p][...], _weight_value(ab[2 * p + 1], prods[p][1]), NN, preferred_element_type=F32)
            dh = d if dh is None else dh + d
        xhat, rstd = _rms_parts(x_ref[...])
        gv = g_ref[...]
        dn = dh * (1.0 + sc_ref[...])
        dxh = dn * gv
        dx_ref[...] = dr_ref[...] + rstd * (dxh - xhat * jnp.mean(dxh * xhat, axis=-1, keepdims=True))
        _example_acc(dsh_ref, i, tiles, _colsum(dh))
        _example_acc(dsc_ref, i, tiles, _colsum(dh * (xhat * gv)))
        _example_acc(dg_ref, i, tiles, _colsum(dn * xhat))

    row = pl.BlockSpec((tm, D), lambda i: (i, 0))
    bvec = pl.BlockSpec((None, 1, D), lambda i: (i // tiles, 0, 0))
    in_specs, operands = [], []
    for a, b in prods:
        in_specs += [pl.BlockSpec((tm, a.shape[1]), lambda i: (i, 0)), _weight_spec(b)]
        operands += [a, _weight_operand(b)]
    acc_shape = jax.ShapeDtypeStruct((T // S, 1, D), F32)
    return _pcall(
        body, name=name,
        out_shape=[jax.ShapeDtypeStruct((T, D), F32), acc_shape, acc_shape, acc_shape],
        grid=(T // tm,),
        in_specs=in_specs + [row, pl.BlockSpec((1, D), lambda i: (0, 0)), bvec, row],
        operands=[*operands, x, g, sc, dres],
        out_specs=[row, bvec, bvec, bvec],
        scratch_shapes=[], params=_params(("arbitrary",)), rider=rider)


def _ffn_forward(tag, x, g, sh, sc, gt, wgT, wuT, wd, S, gather=None):
    T, D = x.shape
    F = wd.shape[0]

    def gateup(accs):
        a, u = accs
        return [a, u, a * _sigmoid(a) * u]

    h, a, u, s, *land = _norm_matmul(f"{tag}_gateup", x, g, sc, sh, [wgT, wuT], gateup, [(F, BF16)] * 3, S,
                                     rider=None if gather is None else _gather_direct_rider(gather))

    def down(accs, ex):
        xv, gtv = ex
        return [xv + 0.5 * gtv * accs[0], accs[0]]

    tmd = _div(S, 512, 8)
    x_new, y, *land = _matmul(f"{tag}_down", "nn", [[(s, wd)]], T, D, F, tmd, D, F, [F32, BF16],
                              extras=[(x, "tile", 0), (gt, "brow", 0)], epilogue=down, rows_per_example=S,
                              rider=None if gather is None else _gather_forward_rider(land[0]))
    return x_new, (x, h, a, u, s, y), (land[0] if land else None)


def _ffn_backward(tag, dx_out, saved, g, sc, gt, wgT, wuT, wd, S, rider=None, dh_rider=None):
    x, h, a, u, s, y = saved
    T, D = x.shape
    F = wd.shape[0]

    def act_grad(ds, ex):
        av, uv = ex[0].astype(F32), ex[1].astype(F32)
        sg = _sigmoid(av)
        return [ds * uv * (sg * (1.0 + av * (1.0 - sg))), ds * (av * sg)]

    dy, dgt, da, du, *rode = _gated_grad_matmul(f"{tag}_act_grad", dx_out, y, gt, 0.5, wd, [(a, 0), (u, 0)], act_grad,
                                                [BF16, BF16], S, rider=rider)
    tkw = _div(T, 1024, LANES)
    dwd = _matmul(f"{tag}_dw_down", "tn", [[(s, dy)]], F, D, T, F, D, tkw, [F32])[0]
    dwgT = _matmul(f"{tag}_dw_gate", "tn", [[(da, h)]], F, D, T, F, D, tkw, [F32])[0]
    dwuT = _matmul(f"{tag}_dw_up", "tn", [[(du, h)]], F, D, T, F, D, tkw, [F32])[0]
    dx_in, dsh, dsc, dg, *rode_dh = _matmul_normmod_bwd(
        f"{tag}_dh", [(da, wgT), (du, wuT)], x, g, sc, dx_out, S,
        rider=None if dh_rider is None else dh_rider(dwgT, dwuT, dwd))
    return dx_in, (dsh, dsc, dgt, dg), (dwgT, dwuT, dwd), rode + rode_dh


def _loss_head(x, tgt, g, S):
    T, D = x.shape

    def fn(xv, tv, gv):
        xhat, rstd = _rms_parts(xv)
        e = xhat * gv - tv
        loss = jnp.broadcast_to(0.5 / D * jnp.sum(_colsum(e * e), axis=1, keepdims=True), (1, LANES))
        dy = e * (1.0 / D)
        dxh = dy * gv
        dx = rstd * (dxh - xhat * jnp.mean(dxh * xhat, axis=-1, keepdims=True))
        return [dx, loss, _colsum(dy * xhat)]

    return _rowwise("loss_head", fn, T, _div(S, 512, 8), [(x, "row", None), (tgt, "row", None), (g, "vec", None)],
                    [("row", D, F32), ("bacc", LANES, F32), ("bacc", D, F32)], S)


def _cumsum(v):
    B, S, _ = v.shape
    rows = _div(S, 1024, BLOCK)

    def body(x_ref, o_ref, carry):
        i = pl.program_id(1)

        @pl.when(i == 0)
        def _():
            carry[...] = jnp.zeros_like(carry)

        r = lax.broadcasted_iota(jnp.int32, (BLOCK, BLOCK), 0)
        c = lax.broadcasted_iota(jnp.int32, (BLOCK, BLOCK), 1)
        tri = (c <= r).astype(F32)
        last = carry[0:1, :]
        for j in range(0, rows, BLOCK):
            cum = jnp.dot(tri, x_ref[j:j + BLOCK, :], precision=lax.Precision.HIGHEST, preferred_element_type=F32) + last
            o_ref[j:j + BLOCK, :] = cum
            last = cum[BLOCK - 1:BLOCK, :]
        carry[...] = jnp.broadcast_to(last, carry.shape)

    return pl.pallas_call(
        body, name="cumsum", out_shape=jax.ShapeDtypeStruct(v.shape, F32), grid=(B, S // rows),
        in_specs=[pl.BlockSpec((None, rows, LANES), lambda b, i: (b, i, 0))],
        out_specs=pl.BlockSpec((None, rows, LANES), lambda b, i: (b, i, 0)),
        scratch_shapes=[pltpu.VMEM((8, LANES), F32)],
        compiler_params=_params(("arbitrary", "arbitrary")),
    )(v)


def _with_ones(x):
    lane = lax.broadcasted_iota(jnp.int32, (x.shape[0], HEAD_DIM), 1)
    return jnp.concatenate([x, jnp.where(lane == 0, 1.0, 0.0).astype(x.dtype)], axis=1)


def _causal_strip(s, r):
    qpos = r + lax.broadcasted_iota(jnp.int32, s.shape, 0)
    kpos = lax.broadcasted_iota(jnp.int32, s.shape, 1)
    return jnp.where(kpos <= qpos, s, NEG_INF)


NT = (((1,), (1,)), ((), ()))
NN = (((1,), (0,)), ((), ()))
TN = (((0,), (0,)), ((), ()))


def _fox_fwd(pm3, cumT, qcol, kcol, vcol, tq, rider=None):
    B, S, _ = pm3.shape
    nq = S // tq
    strips = range(0, tq, FOX_STRIP)

    def body(q_ref, k_ref, v_ref, ck_ref, o_ref, o32_ref, lse_ref, s_sc, p_sc, al_sc, m_sc, acc_sc):
        qi, kj = pl.program_id(1), pl.program_id(2)

        @pl.when(kj == 0)
        def _():
            m_sc[...] = jnp.full_like(m_sc, NEG_INF)
            acc_sc[...] = jnp.zeros_like(acc_sc)

        def tile(diagonal):
            def scores(h):
                hs = slice(HEAD_DIM * h, HEAD_DIM * (h + 1))
                s_sc[h % 2] = lax.dot_general(q_ref[:, hs] * SCALE, k_ref[:, hs], NT, preferred_element_type=F32)

            def accumulate(h):
                hs = slice(HEAD_DIM * h, HEAD_DIM * (h + 1))
                acc_sc[h] = al_sc[h % 2] * acc_sc[h] + lax.dot_general(p_sc[h % 2], _with_ones(v_ref[:, hs]), NN,
                                                                       preferred_element_type=F32)

            scores(0)
            for h in range(FOX_HEADS):
                b = h % 2
                if h + 1 < FOX_HEADS:
                    scores(h + 1)
                if h >= 1:
                    accumulate(h - 1)
                ck = ck_ref[h:h + 1, :]
                for r in strips:
                    rows = slice(r, r + FOX_STRIP)
                    s = s_sc[b, rows, :] - ck
                    if diagonal:
                        s = _causal_strip(s, r)
                    m_prev = m_sc[h, rows, :]
                    m_new = jnp.maximum(m_prev, jnp.max(s, axis=-1, keepdims=True))
                    p_sc[b, rows, :] = jnp.exp(s - m_new).astype(BF16)
                    al_sc[b, rows, :] = jnp.exp(m_prev - m_new)
                    m_sc[h, rows, :] = m_new
            accumulate(FOX_HEADS - 1)

        @pl.when(kj < qi)
        def _():
            tile(False)

        @pl.when(kj == qi)
        def _():
            tile(True)

        @pl.when(kj == nq - 1)
        def _():
            lse_ref[...] = jnp.zeros_like(lse_ref)
            for h in range(FOX_HEADS):
                hs = slice(HEAD_DIM * h, HEAD_DIM * (h + 1))
                acc = acc_sc[h]
                l = acc[:, HEAD_DIM:HEAD_DIM + 1]
                oh = acc[:, :HEAD_DIM] / l
                o_ref[:, hs] = oh.astype(o_ref.dtype)
                o32_ref[:, hs] = oh
                lse_ref[:, h:h + 1] = m_sc[h] + jnp.log(l)

    ospec = pl.BlockSpec((None, tq, FOX_W), lambda b, i, j: (b, i, 0))
    return _pcall(
        body, name="fox_forward",
        out_shape=[jax.ShapeDtypeStruct((B, S, FOX_W), BF16), jax.ShapeDtypeStruct((B, S, FOX_W), F32),
                   jax.ShapeDtypeStruct((B, S, LANES), F32)],
        grid=(B, nq, nq),
        in_specs=[pl.BlockSpec((None, tq, FOX_W), lambda b, i, j: (b, i, qcol)),
                  pl.BlockSpec((None, tq, FOX_W), lambda b, i, j: (b, jnp.minimum(i, j), kcol)),
                  pl.BlockSpec((None, tq, FOX_W), lambda b, i, j: (b, jnp.minimum(i, j), vcol)),
                  pl.BlockSpec((None, 8, tq), lambda b, i, j: (b, 0, jnp.minimum(i, j)))],
        operands=[pm3, pm3, pm3, cumT],
        out_specs=[ospec, ospec, pl.BlockSpec((None, tq, LANES), lambda b, i, j: (b, i, 0))],
        scratch_shapes=[pltpu.VMEM((2, tq, tq), F32), pltpu.VMEM((2, tq, tq), BF16), pltpu.VMEM((2, tq, 1), F32),
                        pltpu.VMEM((FOX_HEADS, tq, 1), F32), pltpu.VMEM((FOX_HEADS, tq, LANES), F32)],
        params=_params(("parallel", "parallel", "arbitrary")), rider=rider)


def _fox_bwd(pm3, do, delta, lse, cumT, qcol, kcol, vcol, tq, rider=None):
    B, S, _ = pm3.shape
    nq = S // tq
    strips = range(0, tq, FOX_STRIP)

    def body(q_ref, k_ref, v_ref, do_ref, dl_ref, lse_ref, ck_ref, dq_ref, rs_ref, dk_ref, dv_ref, cs_ref,
             s_sc, dp_sc, p_sc, ds_sc, dq_sc, dk_sc, dv_sc):
        kj, qi = pl.program_id(1), pl.program_id(2)

        @pl.when((kj == 0) & (qi == 0))
        def _():
            dq_sc[...] = jnp.zeros_like(dq_sc)

        @pl.when(qi == 0)
        def _():
            dk_sc[...] = jnp.zeros_like(dk_sc)
            dv_sc[...] = jnp.zeros_like(dv_sc)

        def tile(diagonal):
            qrows = pl.ds(pl.multiple_of(qi * tq, tq), tq)
            for h in range(FOX_HEADS):
                hs = slice(HEAD_DIM * h, HEAD_DIM * (h + 1))
                qh, kh, doh = q_ref[:, hs] * SCALE, k_ref[:, hs], do_ref[:, hs]
                b = h % 2
                s_sc[b] = lax.dot_general(qh, kh, NT, preferred_element_type=F32)
                dp_sc[b] = lax.dot_general(doh, v_ref[:, hs], NT, preferred_element_type=F32)
                ck = ck_ref[h:h + 1, :]
                for r in strips:
                    rows = slice(r, r + FOX_STRIP)
                    s = s_sc[b, rows, :] - ck
                    if diagonal:
                        s = _causal_strip(s, r)
                    p = jnp.exp(s - lse_ref[rows, h:h + 1])
                    p_sc[b, rows, :] = p.astype(BF16)
                    ds_sc[b, rows, :] = (p * (dp_sc[b, rows, :] - dl_ref[rows, h:h + 1])).astype(BF16)
                dv_sc[h] += lax.dot_general(doh, p_sc[b], TN, preferred_element_type=F32)
                dk_sc[h] += lax.dot_general(_with_ones(qh), ds_sc[b], TN, preferred_element_type=F32)
                dq_sc[h, qrows, :] += lax.dot_general(ds_sc[b], _with_ones(kh), NN, preferred_element_type=F32)

        @pl.when(qi > kj)
        def _():
            tile(False)

        @pl.when(qi == kj)
        def _():
            tile(True)

        @pl.when(qi == nq - 1)
        def _():
            cs_ref[...] = jnp.zeros_like(cs_ref)
            for h in range(FOX_HEADS):
                hs = slice(HEAD_DIM * h, HEAD_DIM * (h + 1))
                dv_ref[:, hs] = dv_sc[h].T.astype(dv_ref.dtype)
                dk = dk_sc[h].T
                dk_ref[:, hs] = dk[:, :HEAD_DIM].astype(dk_ref.dtype)
                cs_ref[:, h:h + 1] = dk[:, HEAD_DIM:HEAD_DIM + 1]

        @pl.when((kj == nq - 1) & (qi == nq - 1))
        def _():
            rs_ref[...] = jnp.zeros_like(rs_ref)
            for h in range(FOX_HEADS):
                hs = slice(HEAD_DIM * h, HEAD_DIM * (h + 1))
                dq_ref[:, hs] = (dq_sc[h, :, :HEAD_DIM] * SCALE).astype(dq_ref.dtype)
                rs_ref[:, h:h + 1] = dq_sc[h, :, HEAD_DIM:HEAD_DIM + 1]

    def qside(width, col=0):
        return pl.BlockSpec((None, tq, width), lambda b, j, i: (b, jnp.maximum(i, j), col))

    kspec = pl.BlockSpec((None, tq, FOX_W), lambda b, j, i: (b, j, 0))
    return _pcall(
        body, name="fox_backward",
        out_shape=[jax.ShapeDtypeStruct((B, S, FOX_W), BF16), jax.ShapeDtypeStruct((B, S, LANES), F32),
                   jax.ShapeDtypeStruct((B, S, FOX_W), BF16), jax.ShapeDtypeStruct((B, S, FOX_W), BF16),
                   jax.ShapeDtypeStruct((B, S, LANES), F32)],
        grid=(B, nq, nq),
        in_specs=[qside(FOX_W, qcol),
                  pl.BlockSpec((None, tq, FOX_W), lambda b, j, i: (b, j, kcol)),
                  pl.BlockSpec((None, tq, FOX_W), lambda b, j, i: (b, j, vcol)),
                  qside(FOX_W), qside(LANES), qside(LANES),
                  pl.BlockSpec((None, 8, tq), lambda b, j, i: (b, 0, j))],
        operands=[pm3, pm3, pm3, do, delta, lse, cumT],
        out_specs=[pl.BlockSpec((None, S, FOX_W), lambda b, j, i: (b, 0, 0)),
                   pl.BlockSpec((None, S, LANES), lambda b, j, i: (b, 0, 0)),
                   kspec, kspec, pl.BlockSpec((None, tq, LANES), lambda b, j, i: (b, j, 0))],
        scratch_shapes=[pltpu.VMEM((2, tq, tq), F32), pltpu.VMEM((2, tq, tq), F32), pltpu.VMEM((2, tq, tq), BF16),
                        pltpu.VMEM((2, tq, tq), BF16), pltpu.VMEM((FOX_HEADS, S, LANES), F32),
                        pltpu.VMEM((FOX_HEADS, LANES, tq), F32), pltpu.VMEM((FOX_HEADS, HEAD_DIM, tq), F32)],
        params=_params(("parallel", "arbitrary", "arbitrary")), rider=rider)


def _fox_delta(do, o32, T, S):
    def fn(dov, ov):
        prod = dov.astype(F32) * ov
        lane = lax.broadcasted_iota(jnp.int32, (dov.shape[0], LANES), 1)
        delta = jnp.zeros((dov.shape[0], LANES), F32)
        for h in range(FOX_HEADS):
            hs = slice(HEAD_DIM * h, HEAD_DIM * (h + 1))
            delta = jnp.where(lane == h, jnp.sum(prod[:, hs], axis=-1, keepdims=True), delta)
        return [delta]

    return _rowwise("fox_delta", fn, T, _div(S, 512, 8), [(do, "row", None), (o32, "row", None)], [("row", LANES, F32)], S)[0]


def _alibi_slope(group, head):
    return 2.0 ** (-ALIBI_MAX_BIAS * (group * DIL_HPG + head + 1) / (N_DIL * DIL_HPG))


def _residue_order(a, B, S, d):
    C = a.shape[-1]
    if d == 1:
        return a.reshape(B, S, C)
    return a.reshape(B, S // d, d, C).transpose(0, 2, 1, 3).reshape(B * d, S // d, C)


def _token_order(a, B, S, d):
    C = a.shape[-1]
    if d == 1:
        return a.reshape(B * S, C)
    return a.reshape(B, d, S // d, C).transpose(0, 2, 1, 3).reshape(B * S, C)


def _band_scores(qh, kcat, slope_d, has_prev):
    qi = lax.broadcasted_iota(jnp.int32, (BLOCK, 2 * BLOCK), 0)
    c = lax.broadcasted_iota(jnp.int32, (BLOCK, 2 * BLOCK), 1)
    s = lax.dot_general(qh, kcat, NT, preferred_element_type=F32) - slope_d * (BLOCK + qi - c).astype(F32)
    valid = (c >= qi) & (c <= qi + BLOCK)
    if has_prev is not None:
        valid = valid & ((c >= BLOCK) | has_prev)
    return jnp.where(valid, s, NEG_INF)


def _band_operands(j, cur_ref, prev_ref, hs):
    if j == 0:
        return jnp.concatenate([prev_ref[:, hs], cur_ref[0:BLOCK, hs]], axis=0)
    return cur_ref[(j - 1) * BLOCK:(j + 1) * BLOCK, hs]


def _dil_specs(Ls, qb, cols):
    nsub = qb // BLOCK
    qcol, kcol, vcol = cols

    def cur(col):
        return pl.BlockSpec((None, qb, DIL_GW), lambda s, n: (s, n, col))

    def prev(col):
        return pl.BlockSpec((None, BLOCK, DIL_GW), lambda s, n: (s, jnp.maximum(n * nsub - 1, 0), col))

    return [cur(qcol), cur(kcol), prev(kcol), cur(vcol), prev(vcol)]


def _dil_fwd(group, src, cols):
    _, dilation = DIL_GROUPS[group]
    nseq, Ls, _ = src.shape
    qb = _div(Ls, 512, BLOCK)
    nsub = qb // BLOCK

    def body(q_ref, kc_ref, kp_ref, vc_ref, vp_ref, o_ref, lse_ref):
        has_prev = pl.program_id(1) > 0
        lse_ref[...] = jnp.zeros_like(lse_ref)
        for h in range(DIL_HPG):
            hs = slice(HEAD_DIM * h, HEAD_DIM * (h + 1))
            scores = [_band_scores(q_ref[j * BLOCK:(j + 1) * BLOCK, hs] * SCALE, _band_operands(j, kc_ref, kp_ref, hs),
                                   _alibi_slope(group, h) * dilation, has_prev if j == 0 else None) for j in range(nsub)]
            pending = None

            def write(j, m, acc):
                rows = slice(j * BLOCK, (j + 1) * BLOCK)
                l = acc[:, HEAD_DIM:HEAD_DIM + 1]
                o_ref[rows, hs] = acc[:, :HEAD_DIM] / l
                lse_ref[rows, h:h + 1] = m + jnp.log(l)

            for j in range(nsub):
                m = jnp.max(scores[j], axis=-1, keepdims=True)
                p = jnp.exp(scores[j] - m).astype(BF16)
                acc = lax.dot_general(p, _with_ones(_band_operands(j, vc_ref, vp_ref, hs)), NN, preferred_element_type=F32)
                if pending is not None:
                    write(*pending)
                pending = (j, m, acc)
            write(*pending)

    return pl.pallas_call(
        body, name=f"dil_forward_{group}",
        out_shape=[jax.ShapeDtypeStruct((nseq, Ls, DIL_GW), F32), jax.ShapeDtypeStruct((nseq, Ls, LANES), F32)],
        grid=(nseq, Ls // qb),
        in_specs=_dil_specs(Ls, qb, cols),
        out_specs=[pl.BlockSpec((None, qb, DIL_GW), lambda s, n: (s, n, 0)),
                   pl.BlockSpec((None, qb, LANES), lambda s, n: (s, n, 0))],
        compiler_params=_params(("parallel", "arbitrary")),
    )(src, src, src, src, src)


def _dil_bwd(group, src, cols, Lr, dyr, dlr):
    _, dilation = DIL_GROUPS[group]
    nseq, Ls, _ = src.shape
    qb = _div(Ls, 512, BLOCK)
    nsub, nb = qb // BLOCK, Ls // qb

    def body(q_ref, kc_ref, kp_ref, vc_ref, vp_ref, L_ref, dy_ref, dl_ref, dq_ref, dk_ref, dv_ref, dk_sc, dv_sc):
        n = pl.program_id(1)
        has_prev = n > 0

        @pl.when(n == 0)
        def _():
            dk_sc[...] = jnp.zeros_like(dk_sc)
            dv_sc[...] = jnp.zeros_like(dv_sc)

        base = pl.multiple_of(n * qb, BLOCK)
        for h in range(DIL_HPG):
            hs = slice(HEAD_DIM * h, HEAD_DIM * (h + 1))
            blocks = [slice(j * BLOCK, (j + 1) * BLOCK) for j in range(nsub)]
            qhs = [q_ref[rows, hs] * SCALE for rows in blocks]
            kcats = [_band_operands(j, kc_ref, kp_ref, hs) for j in range(nsub)]
            dyhs = [dy_ref[rows, hs] for rows in blocks]
            scores = [_band_scores(qhs[j], kcats[j], _alibi_slope(group, h) * dilation, has_prev if j == 0 else None)
                      for j in range(nsub)]
            dps = [lax.dot_general(dyhs[j], _band_operands(j, vc_ref, vp_ref, hs), NT, preferred_element_type=F32)
                   for j in range(nsub)]
            pending = None

            def write(j, dq, dk, dv):
                dq_ref[blocks[j], hs] = (dq * SCALE).astype(dq_ref.dtype)
                win = pl.ds(base + j * BLOCK, 2 * BLOCK)
                dk_sc[win, hs] += dk
                dv_sc[win, hs] += dv

            for j in range(nsub):
                p = jnp.exp(scores[j] - L_ref[blocks[j], h:h + 1])
                ds = (p * (dps[j] - dl_ref[blocks[j], h:h + 1])).astype(BF16)
                dq = lax.dot_general(ds, kcats[j], NN, preferred_element_type=F32)
                dk = lax.dot_general(ds, qhs[j], TN, preferred_element_type=F32)
                dv = lax.dot_general(p.astype(BF16), dyhs[j], TN, preferred_element_type=F32)
                if pending is not None:
                    write(*pending)
                pending = (j, dq, dk, dv)
            write(*pending)

        @pl.when(n == nb - 1)
        def _():
            dk_ref[...] = dk_sc[BLOCK:, :].astype(dk_ref.dtype)
            dv_ref[...] = dv_sc[BLOCK:, :].astype(dv_ref.dtype)

    own = pl.BlockSpec((None, qb, DIL_GW), lambda s, n: (s, n, 0))
    own128 = pl.BlockSpec((None, qb, LANES), lambda s, n: (s, n, 0))
    whole = pl.BlockSpec((None, Ls, DIL_GW), lambda s, n: (s, 0, 0))
    shape = jax.ShapeDtypeStruct((nseq, Ls, DIL_GW), BF16)
    return pl.pallas_call(
        body, name=f"dil_backward_{group}",
        out_shape=[shape, shape, shape],
        grid=(nseq, nb),
        in_specs=_dil_specs(Ls, qb, cols) + [own128, own, own128],
        out_specs=[own, whole, whole],
        scratch_shapes=[pltpu.VMEM((Ls + BLOCK, DIL_GW), F32), pltpu.VMEM((Ls + BLOCK, DIL_GW), F32)],
        compiler_params=_params(("parallel", "arbitrary")),
    )(src, src, src, src, src, Lr, dyr, dlr)


def _dil_combine(os_, lses, T, S):
    def fn(o0, o1, o2, l0, l1, l2):
        m = jnp.maximum(jnp.maximum(l0, l1), l2)
        e0, e1, e2 = jnp.exp(l0 - m), jnp.exp(l1 - m), jnp.exp(l2 - m)
        tot = e0 + e1 + e2
        w0, w1, w2 = e0 / tot, e1 / tot, e2 / tot
        parts = []
        for h in range(DIL_HPG):
            hs = slice(HEAD_DIM * h, HEAD_DIM * (h + 1))
            parts.append(w0[:, h:h + 1] * o0[:, hs] + w1[:, h:h + 1] * o1[:, hs] + w2[:, h:h + 1] * o2[:, hs])
        return [jnp.concatenate(parts, axis=1), m + jnp.log(tot)]

    ins = [(a, "row", None) for a in os_] + [(a, "row", None) for a in lses]
    return _rowwise("dil_combine", fn, T, _div(S, 512, 8), ins, [("row", DIL_GW, BF16), ("row", LANES, F32)], S)


def _dil_delta(dy, y, T, S):
    def fn(dyv, yv):
        prod = dyv * yv.astype(F32)
        lane = lax.broadcasted_iota(jnp.int32, (dyv.shape[0], LANES), 1)
        delta = jnp.zeros((dyv.shape[0], LANES), F32)
        for h in range(DIL_HPG):
            hs = slice(HEAD_DIM * h, HEAD_DIM * (h + 1))
            delta = jnp.where(lane == h, jnp.sum(prod[:, hs], axis=-1, keepdims=True), delta)
        return [delta, dyv]

    return _rowwise("dil_delta", fn, T, _div(S, 512, 8), [(dy, "row", None), (y, "row", None)],
                    [("row", LANES, F32), ("row", DIL_GW, BF16)], S)


def _ada_forward(c_all, w, b):
    n, D = c_all.shape
    cl = w.shape[1]

    def body(c_ref, w_ref, b_ref, o_ref, ca_ref):
        cv = c_ref[...]
        ca = (cv * _sigmoid(cv)).astype(BF16)
        ca_ref[...] = ca
        o_ref[...] = jnp.dot(ca, w_ref[...].astype(BF16), preferred_element_type=F32) + b_ref[...]

    return pl.pallas_call(
        body, name="ada_forward",
        out_shape=[jax.ShapeDtypeStruct((n, cl), F32), jax.ShapeDtypeStruct((n, D), BF16)],
        compiler_params=_params(),
    )(c_all, w, b)


def _ada_backward(ca, dmod_cols, dmod_all):
    n, D = ca.shape
    cl = dmod_cols.shape[1]

    def body(ca_ref, dc_ref, da_ref, gw_ref, gb_ref):
        gw_ref[...] = lax.dot_general(ca_ref[...], dc_ref[...].astype(BF16), (((0,), (0,)), ((), ())), preferred_element_type=F32)
        gb_ref[...] = _colsum(da_ref[...])

    return pl.pallas_call(
        body, name="ada_backward",
        out_shape=[jax.ShapeDtypeStruct((D, cl), F32), jax.ShapeDtypeStruct((1, dmod_all.shape[1]), F32)],
        compiler_params=_params(),
    )(ca, dmod_cols, dmod_all)


def _sum_devices(v):
    def body(v_ref, o_ref):
        tot = v_ref[0]
        for k in range(1, N_DEV):
            tot = tot + v_ref[k]
        o_ref[...] = tot

    return pl.pallas_call(body, name="sum_devices", out_shape=jax.ShapeDtypeStruct(v.shape[1:], F32))(v)


def _adamw(name, w, g, m, v):
    rows, cols = w.shape
    tr = _div(rows, 256, 8)

    def body(w_ref, g_ref, m_ref, v_ref, d_ref, nm_ref, nv_ref):
        gv = g_ref[...]
        nm = ADAM_B1 * m_ref[...] + (1.0 - ADAM_B1) * gv
        nv = ADAM_B2 * v_ref[...] + (1.0 - ADAM_B2) * (gv * gv)
        m_hat = nm / (1.0 - ADAM_B1 ** ADAM_STEP)
        v_hat = nv / (1.0 - ADAM_B2 ** ADAM_STEP)
        d_ref[...] = -ADAM_LR * (m_hat / (jnp.sqrt(v_hat) + ADAM_EPS) + ADAM_WD * w_ref[...])
        nm_ref[...] = nm
        nv_ref[...] = nv

    spec = pl.BlockSpec((tr, cols), lambda i: (i, 0))
    shape = jax.ShapeDtypeStruct((rows, cols), F32)
    return pl.pallas_call(
        body, name=name, out_shape=[shape, shape, shape], grid=(rows // tr,),
        in_specs=[spec, spec, spec, spec], out_specs=[spec, spec, spec],
        compiler_params=_params(("arbitrary",)),
    )(w, g, m, v)


def _pad_rows(a, rows):
    return a if a.shape[0] == rows else jnp.pad(a, ((0, rows - a.shape[0]), (0, 0)))


class _Packed:
    def __init__(self, kind, local_shape, D):
        self.kind, self.local_shape, self.D = kind, local_shape, D
        r, c = local_shape
        self.rows = {"T": c, "N": r, "F": r * c // D}[kind]
        self.rows_pad = -(-self.rows // ROW_ALIGN) * ROW_ALIGN

    def pack_local(self, w):
        if self.kind == "T":
            w = w.T
        elif self.kind == "F":
            w = w.reshape(self.rows, self.D)
        return _pad_rows(w, self.rows_pad)

    def full(self, gathered):
        g = gathered[:, :self.rows]
        if self.kind == "F":
            r, c = self.local_shape
            return g.reshape(N_DEV, r, c).transpose(1, 0, 2).reshape(r, N_DEV * c)
        return g.reshape(N_DEV * self.rows, self.D)

    def pack_grad(self, gfull):
        if self.kind == "F":
            r, c = self.local_shape
            g = gfull.reshape(r, N_DEV, c).transpose(1, 0, 2).reshape(N_DEV, self.rows, self.D)
        else:
            g = gfull.reshape(N_DEV, self.rows, self.D)
        if self.rows_pad != self.rows:
            g = jnp.pad(g, ((0, 0), (0, self.rows_pad - self.rows), (0, 0)))
        return g

    def unpack_local(self, g):
        g = g[:self.rows]
        if self.kind == "T":
            return g.T
        if self.kind == "F":
            return g.reshape(self.local_shape)
        return g


BIG = ["ffn1_w_gate", "ffn1_w_up", "ffn1_w_down", "w_in", "w_branch_a", "w_branch_b", "w_out",
       "ffn2_w_gate", "ffn2_w_up", "ffn2_w_down"]
BIG_KIND = {"ffn1_w_gate": "T", "ffn1_w_up": "T", "ffn1_w_down": "N", "w_in": "T", "w_branch_a": "F", "w_branch_b": "F",
            "w_out": "N", "ffn2_w_gate": "T", "ffn2_w_up": "T", "ffn2_w_down": "N"}
GROUPS = (("ffn1_w_gate", "ffn1_w_up", "ffn1_w_down"), ("w_in", "w_branch_a", "w_branch_b", "w_out"),
          ("ffn2_w_gate", "ffn2_w_up", "ffn2_w_down"))
SMALL = ["ada_b", "norm_ffn1", "norm_mix", "forget_bias", "norm_ffn2", "norm_final"]


def kernel(x, c, ada_w, ada_b, norm_ffn1, ffn1_w_gate, ffn1_w_up, ffn1_w_down, norm_mix, w_in, forget_bias, w_branch_a, w_branch_b, w_out, norm_ffn2, ffn2_w_gate, ffn2_w_up, ffn2_w_down, norm_final, loss_target, m_ada_w, m_ada_b, m_norm_ffn1, m_ffn1_w_gate, m_ffn1_w_up, m_ffn1_w_down, m_norm_mix, m_w_in, m_forget_bias, m_w_branch_a, m_w_branch_b, m_w_out, m_norm_ffn2, m_ffn2_w_gate, m_ffn2_w_up, m_ffn2_w_down, m_norm_final, v_ada_w, v_ada_b, v_norm_ffn1, v_ffn1_w_gate, v_ffn1_w_up, v_ffn1_w_down, v_norm_mix, v_w_in, v_forget_bias, v_w_branch_a, v_w_branch_b, v_w_out, v_norm_ffn2, v_ffn2_w_gate, v_ffn2_w_up, v_ffn2_w_down, v_norm_final):
    args = dict(locals())
    B, S, D = x.shape
    T = B * S
    cl = ada_w.shape[2]
    n_in = w_in.shape[2] * N_DEV
    nm = 2 * D + 3 * FOX_W + 3 * DIL_W
    nmp = -(-nm // 512) * 512
    GA, GB, QB, QA = 0, D, 2 * D, 2 * D + 3 * FOX_W
    xpos, ypos, cpos = _position()
    me = 4 * xpos + 2 * ypos + cpos

    packs = {n: _Packed(BIG_KIND[n], args[n].shape[1:], D) for n in BIG}
    offs, pads = {}, {}
    for names in GROUPS:
        r = 0
        for n in names:
            offs[n] = r
            r += packs[n].rows_pad
        pads[names] = -r % PACK_ROW_QUANTUM

    def pack_weights(names):
        return jnp.concatenate([packs[n].pack_local(args[n][0]).astype(BF16) for n in names]
                               + [jnp.zeros((pads[names], D), BF16)], axis=0)

    def unpack_weights(names, land):
        out = {}
        for n in names:
            p = packs[n]
            if p.kind in "TN" and p.rows == p.rows_pad and offs[n] % p.rows == 0:
                out[n] = _Slab(land, offs[n], p.rows)
            else:
                out[n] = p.full(land[:, offs[n]:offs[n] + p.rows_pad])
        return out

    def pack_grads(names, gfull):
        return jnp.concatenate([packs[n].pack_grad(gfull[n]) for n in names] + [jnp.zeros((N_DEV, pads[names], D), F32)], axis=1)

    def unpack_grads(names, g_local):
        return {n: packs[n].unpack_local(g_local[offs[n]:offs[n] + packs[n].rows_pad])[None] for n in names}

    W = unpack_weights(GROUPS[0], _weight_allgather(pack_weights(GROUPS[0])))

    c_all = _small_allgather(c, "gather_c").reshape(N_DEV * B, D)
    b_cols = lax.dynamic_slice(ada_b, (0, me * cl), (1, cl))
    mod_cols, c_act = _ada_forward(c_all, ada_w[0], b_cols)
    mod_all = _small_allgather(mod_cols, "gather_mod").transpose(1, 0, 2).reshape(N_DEV * B, N_MOD * D)
    mod = lax.dynamic_slice(mod_all, (me * B, 0), (B, N_MOD * D)).reshape(B, N_MOD, 1, D)
    sh1, sc1, gt1, sh2, sc2, gt2, sh3, sc3, gt3 = [mod[:, i] for i in range(N_MOD)]

    x0 = x.reshape(T, D)
    x1, saved1, land = _ffn_forward("ffn1", x0, norm_ffn1, sh1, sc1, gt1, W["ffn1_w_gate"], W["ffn1_w_up"], W["ffn1_w_down"], S,
                                    gather=pack_weights(GROUPS[1]))
    W.update(unpack_weights(GROUPS[1], land))
    winT = W["w_in"]
    o_f = 3 * DIL_W + 3 * FOX_W
    wmT = jnp.concatenate([winT[o_f + 8:], winT[3 * DIL_W:o_f], winT[:3 * DIL_W], jnp.zeros((nmp - nm, D), BF16)], axis=0)
    wfT = jnp.concatenate([winT[o_f:o_f + 8], jnp.zeros((LANES - 8, D), BF16)], axis=0)

    tm1k = _div(T, 1024, 8)
    fb = jnp.pad(forget_bias, ((0, 0), (0, LANES - FOX_HEADS)))

    def proj(accs, fbv):
        fl = accs[1] + fbv
        lane = lax.broadcasted_iota(jnp.int32, fl.shape, 1)
        ls = jnp.minimum(fl, 0.0) - jnp.log(1.0 + jnp.exp(-jnp.abs(fl)))
        return [accs[0], jnp.where(lane < FOX_HEADS, ls, 0.0), fl]

    tms = _div(S, 512, 8)
    h2, pm, logsig, flog, land = _norm_matmul("mix_proj", x1, norm_mix, sc2, sh2, [wmT, wfT], proj,
                                              [(nmp, BF16), (LANES, F32), (LANES, F32)], S, vecs=[fb],
                                              rider=_gather_direct_rider(pack_weights(GROUPS[2])))
    cum = _cumsum(logsig.reshape(B, S, LANES))
    cumT = cum[:, :, :8].transpose(0, 2, 1)
    pm3 = pm.reshape(B, S, nmp)
    tq = _div(S, 512, LANES)
    qcol, kcol, vcol = QB // FOX_W, QB // FOX_W + 1, QB // FOX_W + 2
    o_b, o_b32, lse_b, land = _fox_fwd(pm3, cumT, qcol, kcol, vcol, tq, rider=_gather_forward_rider(land))
    W.update(unpack_weights(GROUPS[2], land))
    y_b = o_b.reshape(T, FOX_W)

    qa_blk = QA // DIL_GW
    dil_src, dil_cols = [], []
    for g, (_, d) in enumerate(DIL_GROUPS):
        if d == 1:
            dil_src.append(pm3)
            dil_cols.append((qa_blk + g, qa_blk + N_DIL + g, qa_blk + 2 * N_DIL + g))
        else:
            starts = [QA + (i * N_DIL + g) * DIL_GW for i in range(3)]
            qkv = jnp.concatenate([pm[:, c:c + DIL_GW] for c in starts], axis=1)
            dil_src.append(_residue_order(qkv, B, S, d))
            dil_cols.append((0, 1, 2))
    dil_o, dil_lse = [], []
    for g, (_, d) in enumerate(DIL_GROUPS):
        o_g, lse_g = _dil_fwd(g, dil_src[g], dil_cols[g])
        dil_o.append(_token_order(o_g, B, S, d))
        dil_lse.append(_token_order(lse_g, B, S, d))
    y_a, L_a = _dil_combine(dil_o, dil_lse, T, S)

    wa, wb, wout = W["w_branch_a"], W["w_branch_b"], W["w_out"]
    tnd = D
    tm5 = _div(T, 512, 8)
    yap = _matmul("mix_branch_a", "nn", [[(y_a, wa)]], T, D, DIL_GW, tm5, tnd, DIL_GW, [BF16])[0]

    def merge(accs, ex):
        yapv, gav, gbv = ex
        ybp = accs[0]
        return [ybp, _sigmoid(gav.astype(F32)) * yapv.astype(F32) + _sigmoid(gbv.astype(F32)) * ybp]

    ybp, merged = _matmul("mix_branch_b", "nn", [[(y_b, wb)]], T, D, FOX_W, tm5, tnd, FOX_W, [BF16, BF16],
                          extras=[(yap, "tile", 0), (pm, "tile", GA), (pm, "tile", GB)], epilogue=merge)

    def out_proj(accs, ex):
        xv, gtv = ex
        return [xv + gtv * accs[0], accs[0]]

    x2, ymix = _matmul("mix_out", "nn", [[(merged, wout)]], T, D, D, tms, tnd, D, [F32, BF16],
                       extras=[(x1, "tile", 0), (gt2, "brow", 0)], epilogue=out_proj, rows_per_example=S)

    x3, saved3, _ = _ffn_forward("ffn2", x2, norm_ffn2, sh3, sc3, gt3, W["ffn2_w_gate"], W["ffn2_w_up"], W["ffn2_w_down"], S)

    dx3, loss_b, dg_final = _loss_head(x3, loss_target.reshape(T, D), norm_final.reshape(1, D), S)
    dx2, (dsh3, dsc3, dgt3, dg3), (dwg2, dwu2, dwd2), _ = _ffn_backward(
        "ffn2", dx3, saved3, norm_ffn2, sc3, gt3, W["ffn2_w_gate"], W["ffn2_w_up"], W["ffn2_w_down"], S)
    g3 = pack_grads(GROUPS[2], {"ffn2_w_gate": dwg2, "ffn2_w_up": dwu2, "ffn2_w_down": dwd2})

    def merge_grad(dm, ex):
        gav, gbv, yapv, ybpv = [e.astype(F32) for e in ex]
        sga, sgb = _sigmoid(gav), _sigmoid(gbv)
        return [dm * sga, dm * sgb, dm * yapv * sga * (1.0 - sga), dm * ybpv * sgb * (1.0 - sgb)]

    dym, dgt2, dyap, dybp, dga, dgb, sib3 = _gated_grad_matmul(
        "mix_merge_grad", dx2, ymix, gt2, 1.0, wout, [(pm, GA // D), (pm, GB // D), (yap, 0), (ybp, 0)], merge_grad, [BF16] * 4, S,
        rider=_sibling_exchange_rider(g3))
    sums3, own3 = _chip_sums(g3, sib3)
    tkw = _div(T, 1024, LANES)
    dwout = _matmul("mix_dw_out", "tn", [[(merged, dym)]], D, D, T, D, D, tkw, [F32])[0]
    dwa = _matmul("mix_dw_a", "tn", [[(y_a, dyap)]], DIL_GW, D, T, DIL_GW, D, tkw, [F32])[0]
    dwb = _matmul("mix_dw_b", "tn", [[(y_b, dybp)]], FOX_W, D, T, FOX_W, D, tkw, [F32])[0]
    dy_a = _matmul("mix_dy_a", "nt", [[(dyap, wa)]], T, DIL_GW, D, tm1k, DIL_GW, D, [F32])[0]
    dy_b = _matmul("mix_dy_b", "nt", [[(dybp, wb)]], T, FOX_W, D, tm1k, FOX_W, D, [BF16])[0]

    do3 = dy_b.reshape(B, S, FOX_W)
    delta_b = _fox_delta(dy_b, o_b32.reshape(T, FOX_W), T, S).reshape(B, S, LANES)
    dq_b, ds_rows, dk_b, dv_b, ds_cols, recv3 = _fox_bwd(pm3, do3, delta_b, lse_b, cumT, qcol, kcol, vcol, tq,
                                                         rider=_chip_exchange_rider(sums3))

    delta_a, dy_a16 = _dil_delta(dy_a, y_a, T, S)
    dqs, dks, dvs = [], [], []
    for g, (_, d) in enumerate(DIL_GROUPS):
        dq_g, dk_g, dv_g = _dil_bwd(g, dil_src[g], dil_cols[g], _residue_order(L_a, B, S, d),
                                    _residue_order(dy_a16, B, S, d), _residue_order(delta_a, B, S, d))
        dqs.append(_token_order(dq_g, B, S, d))
        dks.append(_token_order(dk_g, B, S, d))
        dvs.append(_token_order(dv_g, B, S, d))

    dcum = ds_rows - ds_cols
    dcum_run = _cumsum(dcum)
    dcum_tot = dcum_run[:, S - 1:S, :]

    def forget_grad_fn(run, dcv, fl, tot):
        lane = lax.broadcasted_iota(jnp.int32, fl.shape, 1)
        df = jnp.where(lane < FOX_HEADS, (tot - run + dcv) * _sigmoid_exp(-fl), 0.0)
        return [df, _colsum(df)]

    df16, dfb = _rowwise("forget_gate_grad", forget_grad_fn, T, tms,
                         [(dcum_run.reshape(T, LANES), "row", None), (dcum.reshape(T, LANES), "row", None), (flog, "row", None),
                          (dcum_tot, "bvec", None)],
                         [("row", LANES, BF16), ("bacc", LANES, F32)], S)

    dpm = jnp.concatenate([dga, dgb, dq_b.reshape(T, FOX_W), dk_b.reshape(T, FOX_W), dv_b.reshape(T, FOX_W)]
                          + dqs + dks + dvs + ([jnp.zeros((T, nmp - nm), BF16)] if nmp > nm else []), axis=1)
    tmn = _div(nmp, 2048, LANES)
    dwmT = _matmul("mix_dw_in", "tn", [[(dpm, h2)]], nmp, D, T, tmn, D, tkw, [F32])[0]
    dwfT = _matmul("mix_dw_f", "tn", [[(df16, h2)]], LANES, D, T, LANES, D, tkw, [F32])[0]
    dwinT = jnp.concatenate([dwmT[QA:QA + 3 * DIL_W], dwmT[QB:QB + 3 * FOX_W], dwfT[:8], dwmT[GA:2 * D]], axis=0)
    g2 = pack_grads(GROUPS[1], {"w_in": dwinT, "w_branch_a": dwa, "w_branch_b": dwb, "w_out": dwout})
    dx1, dsh2, dsc2, dgmix, sib2 = _matmul_normmod_bwd("mix_dh", [(dpm, wmT), (df16, wfT)], x1, norm_mix, sc2, dx2, S,
                                                       rider=_sibling_exchange_rider(g2))
    sums2, own2 = _chip_sums(g2, sib2)

    own1 = []

    def ffn1_exchange(dwg, dwu, dwd):
        sums1, own = _chip_sums(pack_grads(GROUPS[0], {"ffn1_w_gate": dwg, "ffn1_w_up": dwu, "ffn1_w_down": dwd}))
        own1.append(own)
        return _chip_exchange_rider(sums1)

    dx0, (dsh1, dsc1, dgt1, dg1), _, (recv2, recv1) = _ffn_backward(
        "ffn1", dx1, saved1, norm_ffn1, sc1, gt1, W["ffn1_w_gate"], W["ffn1_w_up"], W["ffn1_w_down"], S,
        rider=_chip_exchange_rider(sums2), dh_rider=ffn1_exchange)
    own1 = own1[0]
    grad_x = dx0.reshape(B, S, D)

    dmod = jnp.concatenate([dsh1, dsc1, dgt1, dsh2, dsc2, dgt2, dsh3, dsc3, dgt3], axis=1).reshape(B, N_MOD * D)
    fbg = jnp.sum(dfb, axis=0)
    small = jnp.concatenate([jnp.sum(dg1, axis=0), jnp.sum(dgmix, axis=0), jnp.sum(dg3, axis=0), jnp.sum(dg_final, axis=0),
                             fbg, jnp.sum(loss_b, axis=0)], axis=1)
    n_small = small.shape[1]
    tail = _small_allgather(jnp.concatenate([dmod, jnp.pad(small, ((0, 0), (0, N_MOD * D - n_small)))], axis=0), "gather_tail")
    dmod_all = tail[:, :B].reshape(N_DEV * B, N_MOD * D)
    dmod_cols = lax.dynamic_slice(dmod_all, (0, me * cl), (N_DEV * B, cl))
    g_ada_w, g_ada_b = _ada_backward(c_act, dmod_cols, dmod_all)

    small = _sum_devices(tail[:, B:, :n_small])
    g_small = {"norm_ffn1": small[:, 0:D], "norm_mix": small[:, D:2 * D], "norm_ffn2": small[:, 2 * D:3 * D],
               "norm_final": small[:, 3 * D:4 * D], "forget_bias": small[:, 4 * D:4 * D + FOX_HEADS], "ada_b": g_ada_b}
    loss = small[0, 4 * D + LANES]

    grads = {"ada_w": g_ada_w[None]}
    for names, own, recv in ((GROUPS[0], own1, recv1), (GROUPS[1], own2, recv2), (GROUPS[2], own3, recv3)):
        grads.update(unpack_grads(names, _final_grad_sum(own, recv)))

    delta, new_m, new_v = {}, {}, {}
    for n in ["ada_w"] + BIG:
        shp = args[n].shape
        d_, m_, v_ = _adamw(f"adamw_{n}", args[n][0], grads[n][0], args["m_" + n][0], args["v_" + n][0])
        delta[n], new_m[n], new_v[n] = d_.reshape(shp), m_.reshape(shp), v_.reshape(shp)
    sizes = [args[n].size for n in SMALL]
    tot = sum(sizes)
    padded = -(-tot // (8 * LANES)) * (8 * LANES)

    def flat(get):
        v = jnp.concatenate([get(n).reshape(-1) for n in SMALL])
        return jnp.pad(v, (0, padded - tot)).reshape(8, padded // 8)

    d_s, m_s, v_s = _adamw("adamw_small", flat(lambda n: args[n]), flat(lambda n: g_small[n]), flat(lambda n: args["m_" + n]),
                           flat(lambda n: args["v_" + n]))
    o = 0
    for n, sz in zip(SMALL, sizes):
        shp = args[n].shape
        grads[n] = g_small[n].reshape(shp)
        delta[n] = d_s.reshape(-1)[o:o + sz].reshape(shp)
        new_m[n] = m_s.reshape(-1)[o:o + sz].reshape(shp)
        new_v[n] = v_s.reshape(-1)[o:o + sz].reshape(shp)
        o += sz

    order = ["ada_w", "ada_b", "norm_ffn1", "ffn1_w_gate", "ffn1_w_up", "ffn1_w_down", "norm_mix", "w_in", "forget_bias",
             "w_branch_a", "w_branch_b", "w_out", "norm_ffn2", "ffn2_w_gate", "ffn2_w_up", "ffn2_w_down", "norm_final"]
    return (loss, grad_x, *[grads[n] for n in order], *[delta[n] for n in order], *[new_m[n] for n in order],
            *[new_v[n] for n in order])
```

```python
import functools
import math

import jax
import jax.numpy as jnp
from jax import lax
from jax.experimental import pallas as pl
from jax.experimental.pallas import tpu as pltpu

F32 = jnp.float32
BF16 = jnp.bfloat16
MESH = pl.DeviceIdType.MESH
ANY = pl.BlockSpec(memory_space=pl.ANY)
VMEM_SPEC = pl.BlockSpec(memory_space=pltpu.VMEM)

N_DEV = 8
HEAD_DIM = 64
BLOCK = 128
DIL_GROUPS = ((128, 1), (512, 4), (2048, 16))
N_DIL = len(DIL_GROUPS)
DIL_HPG = 4
DIL_GW = DIL_HPG * HEAD_DIM
DIL_W = N_DIL * DIL_GW
FOX_HEADS = 8
FOX_W = FOX_HEADS * HEAD_DIM
N_MOD = 9
RMS_EPS = 1e-6
ALIBI_MAX_BIAS = 8.0
NEG_INF = -1e30
ADAM_LR, ADAM_B1, ADAM_B2, ADAM_EPS, ADAM_WD, ADAM_STEP = 0.001, 0.9, 0.999, 1e-08, 0.01, 10
V7X_VMEM_LIMIT = 52 * 1024 * 1024
LANES = 128
ROW_ALIGN = 16
PACK_ROW_QUANTUM = 32
FOX_STRIP = 32
SCALE = 1.0 / math.sqrt(HEAD_DIM)


def _div(dim, target, quantum):
    best = None
    for t in range(quantum, min(dim, target) + 1, quantum):
        if dim % t == 0:
            best = t
    return best or dim


def _params(sem=None):
    return pltpu.CompilerParams(dimension_semantics=sem, vmem_limit_bytes=V7X_VMEM_LIMIT)


def _sigmoid(x):
    return 0.5 * jnp.tanh(0.5 * x) + 0.5


def _sigmoid_exp(x):
    return 1.0 / (1.0 + jnp.exp(-x))


def _position():
    x, y, c = lax.axis_index("x"), lax.axis_index("y"), lax.axis_index("c")
    return x, y, c


def _small_allgather(v, name):
    rows, cols = v.shape

    def body(v_ref, out_ref, send_sems, recv_sems):
        x, y, c = _position()
        me = 4 * x + 2 * y + c
        out_ref[me] = v_ref[...]

        def peer(k):
            return (1 - x if k & 4 else x, 1 - y if k & 2 else y, 1 - c if k & 1 else c)

        def copy(k, slot):
            return pltpu.make_async_remote_copy(
                src_ref=v_ref, dst_ref=out_ref.at[slot], send_sem=send_sems.at[k - 1], recv_sem=recv_sems.at[k - 1],
                device_id=peer(k), device_id_type=MESH)

        sends = [copy(k, me) for k in range(1, N_DEV)]
        for cp in sends:
            cp.start()
        for k in range(1, N_DEV):
            px, py, pc = peer(k)
            copy(k, 4 * px + 2 * py + pc).wait_recv()
        for cp in sends:
            cp.wait_send()

    return pl.pallas_call(
        body, name=name,
        out_shape=jax.ShapeDtypeStruct((N_DEV, rows, cols), v.dtype),
        in_specs=[VMEM_SPEC], out_specs=VMEM_SPEC,
        scratch_shapes=[pltpu.SemaphoreType.DMA((N_DEV - 1,)), pltpu.SemaphoreType.DMA((N_DEV - 1,))],
    )(v)


def _weight_allgather(p):
    rows, cols = p.shape

    def body(p_ref, out_ref, send_sems, recv_sems, local_sem):
        x, y, c = _position()
        me, sibling = (x, y, c), (x, y, 1 - c)
        chips = [(1 - x, y), (x, 1 - y), (1 - x, 1 - y)]

        def slot(px, py, pc):
            return out_ref.at[4 * px + 2 * py + pc]

        def copy(k, block, to, src=None):
            return pltpu.make_async_remote_copy(
                src_ref=slot(*block) if src is None else src, dst_ref=slot(*block),
                send_sem=send_sems.at[k], recv_sem=recv_sems.at[k], device_id=to, device_id_type=MESH)

        mine = pltpu.make_async_copy(p_ref, slot(*me), local_sem)
        mine.start()
        first = [copy(0, me, sibling, src=p_ref)]
        first += [copy(1 + j, me, (*chip, c), src=p_ref) for j, chip in enumerate(chips)]
        for cp in first:
            cp.start()
        passed = [copy(4 + j, (*chip, c), sibling) for j, chip in enumerate(chips)]
        for j, chip in enumerate(chips):
            copy(1 + j, (*chip, c), me).wait_recv()
            passed[j].start()
        copy(0, sibling, me).wait_recv()
        for j, chip in enumerate(chips):
            copy(4 + j, (*chip, 1 - c), me).wait_recv()
        for cp in first + passed:
            cp.wait_send()
        mine.wait()

    return pl.pallas_call(
        body, name="weight_allgather",
        out_shape=jax.ShapeDtypeStruct((N_DEV, rows, cols), p.dtype),
        in_specs=[ANY], out_specs=ANY,
        scratch_shapes=[pltpu.SemaphoreType.DMA((7,)), pltpu.SemaphoreType.DMA((7,)), pltpu.SemaphoreType.DMA],
    )(p)


def _grad_exchange_sibling(g):
    _, rows, cols = g.shape

    def body(g_ref, out_ref, send_sems, recv_sems):
        x, y, c = _position()
        sibling = (x, y, 1 - c)

        def copy(q):
            px, py = q >> 1, q & 1
            return pltpu.make_async_remote_copy(
                src_ref=g_ref.at[4 * px + 2 * py + (1 - c)], dst_ref=out_ref.at[q],
                send_sem=send_sems.at[q], recv_sem=recv_sems.at[q], device_id=sibling, device_id_type=MESH)

        copies = [copy(q) for q in range(4)]
        for cp in copies:
            cp.start()
        for cp in copies:
            cp.wait_recv()
        for cp in copies:
            cp.wait_send()

    return pl.pallas_call(
        body, name="grad_exchange_sibling",
        out_shape=jax.ShapeDtypeStruct((4, rows, cols), g.dtype),
        in_specs=[ANY], out_specs=ANY,
        scratch_shapes=[pltpu.SemaphoreType.DMA((4,)), pltpu.SemaphoreType.DMA((4,))],
    )(g)


class _Rider:
    def __init__(self, operands, out_shapes, n_send, n_recv, start, finish, aliases=None):
        self.operands, self.out_shapes = list(operands), list(out_shapes)
        self.n_send, self.n_recv, self.start, self.finish = n_send, n_recv, start, finish
        self.aliases = aliases or {}


def _pcall(body, *, name, grid, in_specs, operands, out_shape, out_specs, scratch_shapes, params, rider=None):
    if rider is None:
        return pl.pallas_call(body, name=name, out_shape=out_shape, grid=grid, in_specs=in_specs, out_specs=out_specs,
                              scratch_shapes=scratch_shapes, compiler_params=params)(*operands)
    n_in, n_out, n_sc = len(operands), len(out_shape), len(scratch_shapes)
    r_in, r_out = len(rider.operands), len(rider.out_shapes)

    def wrapped(*refs):
        ins, rins = refs[:n_in], refs[n_in:n_in + r_in]
        outs, routs = refs[n_in + r_in:n_in + r_in + n_out], refs[n_in + r_in + n_out:n_in + r_in + n_out + r_out]
        rest = refs[n_in + r_in + n_out + r_out:]
        scratch, sems = rest[:n_sc], rest[n_sc:]
        ids = [pl.program_id(a) for a in range(len(grid))]
        first, last = ids[0] == 0, ids[0] == grid[0] - 1
        for a in range(1, len(grid)):
            first, last = first & (ids[a] == 0), last & (ids[a] == grid[a] - 1)

        @pl.when(first)
        def _():
            rider.start(rins, routs, *sems)

        body(*ins, *outs, *scratch)

        @pl.when(last)
        def _():
            rider.finish(rins, routs, *sems)

    return pl.pallas_call(
        wrapped, name=name, out_shape=list(out_shape) + rider.out_shapes, grid=grid,
        in_specs=list(in_specs) + [ANY] * r_in, out_specs=list(out_specs) + [ANY] * r_out,
        scratch_shapes=list(scratch_shapes) + [pltpu.SemaphoreType.DMA((rider.n_send,)), pltpu.SemaphoreType.DMA((rider.n_recv,))],
        input_output_aliases={n_in + i: n_out + o for i, o in rider.aliases.items()},
        compiler_params=params,
    )(*operands, *rider.operands)


def _flips(x, y, c):
    return [(x, y, 1 - c), (1 - x, y, c), (x, 1 - y, c), (1 - x, 1 - y, c)]


def _gather_direct_rider(p):
    rows, cols = p.shape

    def copies(p_ref, land, send_sems, recv_sems):
        x, y, c = _position()
        me = 4 * x + 2 * y + c
        peers = _flips(x, y, c)
        sends = [pltpu.make_async_remote_copy(src_ref=p_ref, dst_ref=land.at[me], send_sem=send_sems.at[k], recv_sem=recv_sems.at[k],
                                              device_id=to, device_id_type=MESH) for k, to in enumerate(peers)]
        recvs = [pltpu.make_async_remote_copy(src_ref=p_ref, dst_ref=land.at[4 * px + 2 * py + pc], send_sem=send_sems.at[k],
                                              recv_sem=recv_sems.at[k], device_id=(px, py, pc), device_id_type=MESH)
                 for k, (px, py, pc) in enumerate(peers)]
        mine = pltpu.make_async_copy(p_ref, land.at[me], send_sems.at[len(peers)])
        return sends, recvs, mine

    def start(rins, routs, send_sems, recv_sems):
        sends, _, mine = copies(rins[0], routs[0], send_sems, recv_sems)
        mine.start()
        for cp in sends:
            cp.start()

    def finish(rins, routs, send_sems, recv_sems):
        sends, recvs, mine = copies(rins[0], routs[0], send_sems, recv_sems)
        for cp in recvs:
            cp.wait_recv()
        for cp in sends:
            cp.wait_send()
        mine.wait()

    return _Rider([p], [jax.ShapeDtypeStruct((N_DEV, rows, cols), p.dtype)], 5, 4, start, finish)


def _gather_forward_rider(land):
    def copies(buf, send_sems, recv_sems):
        x, y, c = _position()
        chips = [(1 - x, y), (x, 1 - y), (1 - x, 1 - y)]
        sends = [pltpu.make_async_remote_copy(src_ref=buf.at[4 * px + 2 * py + c], dst_ref=buf.at[4 * px + 2 * py + c],
                                              send_sem=send_sems.at[k], recv_sem=recv_sems.at[k], device_id=(x, y, 1 - c),
                                              device_id_type=MESH) for k, (px, py) in enumerate(chips)]
        recvs = [pltpu.make_async_remote_copy(src_ref=buf.at[4 * px + 2 * py + 1 - c], dst_ref=buf.at[4 * px + 2 * py + 1 - c],
                                              send_sem=send_sems.at[k], recv_sem=recv_sems.at[k], device_id=(x, y, 1 - c),
                                              device_id_type=MESH) for k, (px, py) in enumerate(chips)]
        return sends, recvs

    def start(rins, routs, send_sems, recv_sems):
        for cp in copies(routs[0], send_sems, recv_sems)[0]:
            cp.start()

    def finish(rins, routs, send_sems, recv_sems):
        sends, recvs = copies(routs[0], send_sems, recv_sems)
        for cp in recvs:
            cp.wait_recv()
        for cp in sends:
            cp.wait_send()

    return _Rider([land], [jax.ShapeDtypeStruct(land.shape, land.dtype)], 3, 3, start, finish, aliases={0: 0})


def _chip_exchange_rider(s):
    def copies(s_ref, out_ref, send_sems, recv_sems):
        x, y, c = _position()
        chips = [(1 - x, y), (x, 1 - y), (1 - x, 1 - y)]
        return [pltpu.make_async_remote_copy(src_ref=s_ref.at[k], dst_ref=out_ref.at[k], send_sem=send_sems.at[k],
                                             recv_sem=recv_sems.at[k], device_id=(*chips[k], c), device_id_type=MESH)
                for k in range(3)]

    def start(rins, routs, send_sems, recv_sems):
        for cp in copies(rins[0], routs[0], send_sems, recv_sems):
            cp.start()

    def finish(rins, routs, send_sems, recv_sems):
        cps = copies(rins[0], routs[0], send_sems, recv_sems)
        for cp in cps:
            cp.wait_recv()
        for cp in cps:
            cp.wait_send()

    return _Rider([s], [jax.ShapeDtypeStruct(s.shape, s.dtype)], 3, 3, start, finish)


def _chip_partial_sums(g, recv_sib, jj, qq):
    _, rows, cols = g.shape
    tr = _div(rows, 512, ROW_ALIGN)

    def body(jj_ref, qq_ref, g_ref, r_ref, o_ref):
        o_ref[...] = (g_ref[...] + r_ref[...]).astype(o_ref.dtype)

    return pl.pallas_call(
        body, name="chip_partial_sums",
        out_shape=jax.ShapeDtypeStruct((3, rows, cols), BF16),
        grid_spec=pltpu.PrefetchScalarGridSpec(
            num_scalar_prefetch=2, grid=(3, rows // tr),
            in_specs=[pl.BlockSpec((None, tr, cols), lambda k, i, jj, qq: (jj[k], i, 0)),
                      pl.BlockSpec((None, tr, cols), lambda k, i, jj, qq: (qq[k], i, 0))],
            out_specs=pl.BlockSpec((None, tr, cols), lambda k, i, jj, qq: (k, i, 0))),
        compiler_params=_params(("arbitrary", "arbitrary")),
    )(jj, qq, g, recv_sib)


def _own_partial_sum(g, recv_sib, jj, qq):
    _, rows, cols = g.shape
    tr = _div(rows, 512, ROW_ALIGN)

    def body(jj_ref, qq_ref, g_ref, r_ref, o_ref):
        o_ref[...] = g_ref[...] + r_ref[...]

    return pl.pallas_call(
        body, name="own_partial_sum",
        out_shape=jax.ShapeDtypeStruct((rows, cols), F32),
        grid_spec=pltpu.PrefetchScalarGridSpec(
            num_scalar_prefetch=2, grid=(rows // tr,),
            in_specs=[pl.BlockSpec((None, tr, cols), lambda i, jj, qq: (jj[0], i, 0)),
                      pl.BlockSpec((None, tr, cols), lambda i, jj, qq: (qq[0], i, 0))],
            out_specs=pl.BlockSpec((tr, cols), lambda i, jj, qq: (i, 0))),
        compiler_params=_params(("arbitrary",)),
    )(jj, qq, g, recv_sib)


def _final_grad_sum(own, recv):
    rows, cols = own.shape
    tr = _div(rows, 512, ROW_ALIGN)

    def body(o_ref, r_ref, out_ref):
        out_ref[...] = ((o_ref[...] + r_ref[0].astype(F32)) + r_ref[1].astype(F32)) + r_ref[2].astype(F32)

    return pl.pallas_call(
        body, name="final_grad_sum",
        out_shape=jax.ShapeDtypeStruct((rows, cols), F32),
        grid=(rows // tr,),
        in_specs=[pl.BlockSpec((tr, cols), lambda i: (i, 0)), pl.BlockSpec((3, tr, cols), lambda i: (0, i, 0))],
        out_specs=pl.BlockSpec((tr, cols), lambda i: (i, 0)),
        compiler_params=_params(("arbitrary",)),
    )(own, recv)


def _sibling_exchange_rider(g):
    _, rows, cols = g.shape

    def copies(g_ref, out_ref, send_sems, recv_sems):
        x, y, c = _position()
        return [pltpu.make_async_remote_copy(
            src_ref=g_ref.at[4 * (q >> 1) + 2 * (q & 1) + (1 - c)], dst_ref=out_ref.at[q], send_sem=send_sems.at[q],
            recv_sem=recv_sems.at[q], device_id=(x, y, 1 - c), device_id_type=MESH) for q in range(4)]

    def start(rins, routs, send_sems, recv_sems):
        for cp in copies(rins[0], routs[0], send_sems, recv_sems):
            cp.start()

    def finish(rins, routs, send_sems, recv_sems):
        cps = copies(rins[0], routs[0], send_sems, recv_sems)
        for cp in cps:
            cp.wait_recv()
        for cp in cps:
            cp.wait_send()

    return _Rider([g], [jax.ShapeDtypeStruct((4, rows, cols), g.dtype)], 4, 4, start, finish)


def _chip_sums(g, recv_sib=None):
    x, y, c = _position()
    chips = [(1 - x, y), (x, 1 - y), (1 - x, 1 - y)]
    jj = jnp.stack([4 * px + 2 * py + c for px, py in chips]).astype(jnp.int32)
    qq = jnp.stack([2 * px + py for px, py in chips]).astype(jnp.int32)
    jme = jnp.reshape(4 * x + 2 * y + c, (1,)).astype(jnp.int32)
    qme = jnp.reshape(2 * x + y, (1,)).astype(jnp.int32)
    if recv_sib is None:
        recv_sib = _grad_exchange_sibling(g)
    return _chip_partial_sums(g, recv_sib, jj, qq), _own_partial_sum(g, recv_sib, jme, qme)


def _matmul(name, form, prods, M, N, K, tm, tn, tk, out_dtypes, extras=(), epilogue=None, rows_per_example=None, rider=None):
    nk = K // tk
    n_acc = len(prods)
    flat = [ab for group in prods for ab in group]
    dims = {"nn": (((1,), (0,)), ((), ())), "nt": (((1,), (1,)), ((), ())), "tn": (((0,), (0,)), ((), ()))}[form]
    direct = nk > 1 and epilogue is None and n_acc == 1 and list(out_dtypes) == [F32]

    def spec(shape, index_map, whole):
        if whole:
            return pl.BlockSpec(shape, index_map, pipeline_mode=pl.Buffered(1))
        return pl.BlockSpec(shape, index_map)

    if form == "tn":
        a_spec = spec((tk, tm), lambda i, j, k: (k, i), nk == 1 and M == tm)
    else:
        a_spec = spec((tm, tk), lambda i, j, k: (i, k), nk == 1 and M == tm)
    if form == "nt":
        b_spec = spec((tn, tk), lambda i, j, k: (j, k), nk == 1 and N == tn)
    else:
        b_spec = spec((tk, tn), lambda i, j, k: (k, j), nk == 1 and N == tn)
    in_specs, operands = [], []
    for a, b in flat:
        in_specs += [a_spec, _weight_spec(b) if isinstance(b, _Slab) else b_spec]
        operands += [a, _weight_operand(b)]
    for arr, kind, off in extras:
        if kind == "tile":
            assert off % tn == 0
            in_specs.append(pl.BlockSpec((tm, tn), functools.partial(lambda i, j, k, o: (i, j + o), o=off // tn)))
        else:
            tiles = rows_per_example // tm
            in_specs.append(pl.BlockSpec((None, 1, tn), functools.partial(lambda i, j, k, t: (i // t, 0, j), t=tiles)))
        operands.append(arr)
    n_in, n_out = len(operands), len(out_dtypes)

    def body(*refs):
        in_refs, out_refs, acc_refs = refs[:n_in], refs[n_in:n_in + n_out], refs[n_in + n_out:]
        k = pl.program_id(2)
        partials, p = [], 0
        for group in prods:
            tot = None
            for _ in group:
                d = lax.dot_general(in_refs[2 * p][...], _weight_value(in_refs[2 * p + 1], flat[p][1]), dims,
                                    preferred_element_type=F32)
                tot = d if tot is None else tot + d
                p += 1
            partials.append(tot)

        def finish(accs):
            ex = [r[...] for r in in_refs[2 * len(flat):]]
            outs = epilogue(accs, ex) if epilogue is not None else accs
            for r, o in zip(out_refs, outs):
                r[...] = o.astype(r.dtype)

        if nk == 1:
            finish(partials)
        elif direct:
            @pl.when(k == 0)
            def _():
                out_refs[0][...] = partials[0]

            @pl.when(k > 0)
            def _():
                out_refs[0][...] += partials[0]
        else:
            @pl.when(k == 0)
            def _():
                for r, v in zip(acc_refs, partials):
                    r[...] = v

            @pl.when(k > 0)
            def _():
                for r, v in zip(acc_refs, partials):
                    r[...] += v

            @pl.when(k == nk - 1)
            def _():
                finish([r[...] for r in acc_refs])

    return _pcall(
        body, name=name,
        out_shape=[jax.ShapeDtypeStruct((M, N), dt) for dt in out_dtypes],
        grid=(M // tm, N // tn, nk),
        in_specs=in_specs, operands=operands,
        out_specs=[pl.BlockSpec((tm, tn), lambda i, j, k: (i, j)) for _ in out_dtypes],
        scratch_shapes=[pltpu.VMEM((tm, tn), F32) for _ in range(n_acc)] if nk > 1 and not direct else [],
        params=_params(("parallel", "parallel", "arbitrary")), rider=rider)


def _rowwise(name, fn, T, tm, ins, outs, rows_per_example):
    tiles = rows_per_example // tm
    n_ex = T // rows_per_example
    in_specs, operands = [], []
    for arr, kind, arg in ins:
        if kind == "row":
            if arg is None:
                in_specs.append(pl.BlockSpec((tm, arr.shape[1]), lambda i: (i, 0)))
            else:
                in_specs.append(pl.BlockSpec((tm, arg[0]), functools.partial(lambda i, cb: (i, cb), cb=arg[1])))
        elif kind == "bvec":
            in_specs.append(pl.BlockSpec((None, 1, arr.shape[2]), lambda i: (i // tiles, 0, 0)))
        else:
            in_specs.append(pl.BlockSpec((1, arr.shape[1]), lambda i: (0, 0)))
        operands.append(arr)
    out_shape, out_specs = [], []
    for kind, cols, dt in outs:
        if kind == "row":
            out_shape.append(jax.ShapeDtypeStruct((T, cols), dt))
            out_specs.append(pl.BlockSpec((tm, cols), lambda i: (i, 0)))
        else:
            out_shape.append(jax.ShapeDtypeStruct((n_ex, 1, cols), F32))
            out_specs.append(pl.BlockSpec((None, 1, cols), lambda i: (i // tiles, 0, 0)))
    n_in = len(operands)

    def body(*refs):
        i = pl.program_id(0)
        vals = fn(*[r[...] for r in refs[:n_in]])
        for (kind, _, _), r, v in zip(outs, refs[n_in:], vals):
            if kind == "row":
                r[...] = v.astype(r.dtype)
            else:
                @pl.when(i % tiles == 0)
                def _():
                    r[...] = jnp.zeros_like(r)

                r[...] += v

    return pl.pallas_call(
        body, name=name, out_shape=out_shape, grid=(T // tm,), in_specs=in_specs, out_specs=out_specs,
        compiler_params=_params(("arbitrary",)),
    )(*operands)


def _colsum(v):
    return jnp.sum(v, axis=0, keepdims=True)


def _rms_parts(x):
    rstd = lax.rsqrt(jnp.mean(x * x, axis=-1, keepdims=True) + RMS_EPS)
    return x * rstd, rstd


def _resident(shape):
    return pl.BlockSpec(shape, lambda i: (0, 0), pipeline_mode=pl.Buffered(1))


class _Slab:
    def __init__(self, land, off, rows):
        assert off % rows == 0 and rows % ROW_ALIGN == 0
        self.land, self.off, self.rows = land, off, rows
        self.shape = (N_DEV * rows, land.shape[2])


def _weight_spec(w):
    if isinstance(w, _Slab):
        return pl.BlockSpec((N_DEV, w.rows, w.shape[1]), lambda *_: (0, w.off // w.rows, 0), pipeline_mode=pl.Buffered(1))
    return pl.BlockSpec(w.shape, lambda *_: (0, 0), pipeline_mode=pl.Buffered(1))


def _weight_operand(w):
    return w.land if isinstance(w, _Slab) else w


def _weight_value(ref, w):
    return ref[...].reshape(w.shape) if isinstance(w, _Slab) else ref[...]


def _example_acc(r, i, tiles, v):
    @pl.when(i % tiles == 0)
    def _():
        r[...] = jnp.zeros_like(r)

    r[...] += v


def _norm_matmul(name, x, g, sc, sh, weights, epilogue, outs, S, vecs=(), rider=None):
    T, D = x.shape
    tm = _div(S, 256, 8)
    tiles = S // tm
    nw, nv = len(weights), len(vecs)

    def body(*refs):
        x_ref, g_ref, sc_ref, sh_ref = refs[:4]
        w_refs, v_refs = refs[4:4 + nw], refs[4 + nw:4 + nw + nv]
        h_ref, out_refs = refs[4 + nw + nv], refs[5 + nw + nv:]
        xhat, _ = _rms_parts(x_ref[...])
        h = ((xhat * g_ref[...]) * (1.0 + sc_ref[...]) + sh_ref[...]).astype(BF16)
        h_ref[...] = h
        accs = [lax.dot_general(h, _weight_value(r, w), NT, preferred_element_type=F32) for r, w in zip(w_refs, weights)]
        for r, o in zip(out_refs, epilogue(accs, *[v[...] for v in v_refs])):
            r[...] = o.astype(r.dtype)

    bvec = pl.BlockSpec((None, 1, D), lambda i: (i // tiles, 0, 0))
    return _pcall(
        body, name=name,
        out_shape=[jax.ShapeDtypeStruct((T, D), BF16)] + [jax.ShapeDtypeStruct((T, w), dt) for w, dt in outs],
        grid=(T // tm,),
        in_specs=[pl.BlockSpec((tm, D), lambda i: (i, 0)), pl.BlockSpec((1, D), lambda i: (0, 0)), bvec, bvec]
        + [_weight_spec(w) for w in weights] + [pl.BlockSpec(v.shape, lambda i: (0, 0)) for v in vecs],
        operands=[x, g, sc, sh, *[_weight_operand(w) for w in weights], *vecs],
        out_specs=[pl.BlockSpec((tm, D), lambda i: (i, 0))] + [pl.BlockSpec((tm, w), lambda i: (i, 0)) for w, _ in outs],
        scratch_shapes=[], params=_params(("arbitrary",)), rider=rider)


def _gated_grad_matmul(name, dx, y, gt, coeff, w, tiles_in, epilogue, outs, S, rider=None):
    T, D = dx.shape
    N = w.shape[0]
    tm = _div(S, 256, 8)
    tiles = S // tm
    nt = len(tiles_in)

    def body(*refs):
        dx_ref, y_ref, gt_ref, w_ref = refs[:4]
        t_refs, dy_ref, dgt_ref, out_refs = refs[4:4 + nt], refs[4 + nt], refs[5 + nt], refs[6 + nt:]
        i = pl.program_id(0)
        dxv = dx_ref[...]
        dy = (coeff * gt_ref[...] * dxv).astype(BF16)
        dy_ref[...] = dy
        _example_acc(dgt_ref, i, tiles, _colsum(coeff * dxv * y_ref[...].astype(F32)))
        acc = lax.dot_general(dy, _weight_value(w_ref, w), NT, preferred_element_type=F32)
        for r, o in zip(out_refs, epilogue(acc, [t[...] for t in t_refs])):
            r[...] = o.astype(r.dtype)

    row = pl.BlockSpec((tm, D), lambda i: (i, 0))
    bvec = pl.BlockSpec((None, 1, D), lambda i: (i // tiles, 0, 0))
    return _pcall(
        body, name=name,
        out_shape=[jax.ShapeDtypeStruct((T, D), BF16), jax.ShapeDtypeStruct((T // S, 1, D), F32)]
        + [jax.ShapeDtypeStruct((T, N), dt) for dt in outs],
        grid=(T // tm,),
        in_specs=[row, row, bvec, _weight_spec(w)]
        + [pl.BlockSpec((tm, N), functools.partial(lambda i, cb: (i, cb), cb=cb)) for _, cb in tiles_in],
        operands=[dx, y, gt, _weight_operand(w), *[t for t, _ in tiles_in]],
        out_specs=[row, bvec] + [pl.BlockSpec((tm, N), lambda i: (i, 0)) for _ in outs],
        scratch_shapes=[], params=_params(("arbitrary",)), rider=rider)


def _matmul_normmod_bwd(name, prods, x, g, sc, dres, S, rider=None):
    T, D = x.shape
    tm = _div(S, 256, 8)
    tiles = S // tm
    npr = len(prods)

    def body(*refs):
        ab = refs[:2 * npr]
        x_ref, g_ref, sc_ref, dr_ref = refs[2 * npr:2 * npr + 4]
        dx_ref, dsh_ref, dsc_ref, dg_ref = refs[2 * npr + 4:]
        i = pl.program_id(0)
        dh = None
        for p in range(npr):
            d = lax.dot_general(ab[2 * p][...], _weight_value(ab[2 * p + 1], prods[p][1]), NN, preferred_element_type=F32)
            dh = d if dh is None else dh + d
        xhat, rstd = _rms_parts(x_ref[...])
        gv = g_ref[...]
        dn = dh * (1.0 + sc_ref[...])
        dxh = dn * gv
        dx_ref[...] = dr_ref[...] + rstd * (dxh - xhat * jnp.mean(dxh * xhat, axis=-1, keepdims=True))
        _example_acc(dsh_ref, i, tiles, _colsum(dh))
        _example_acc(dsc_ref, i, tiles, _colsum(dh * (xhat * gv)))
        _example_acc(dg_ref, i, tiles, _colsum(dn * xhat))

    row = pl.BlockSpec((tm, D), lambda i: (i, 0))
    bvec = pl.BlockSpec((None, 1, D), lambda i: (i // tiles, 0, 0))
    in_specs, operands = [], []
    for a, b in prods:
        in_specs += [pl.BlockSpec((tm, a.shape[1]), lambda i: (i, 0)), _weight_spec(b)]
        operands += [a, _weight_operand(b)]
    acc_shape = jax.ShapeDtypeStruct((T // S, 1, D), F32)
    return _pcall(
        body, name=name,
        out_shape=[jax.ShapeDtypeStruct((T, D), F32), acc_shape, acc_shape, acc_shape],
        grid=(T // tm,),
        in_specs=in_specs + [row, pl.BlockSpec((1, D), lambda i: (0, 0)), bvec, row],
        operands=[*operands, x, g, sc, dres],
        out_specs=[row, bvec, bvec, bvec],
        scratch_shapes=[], params=_params(("arbitrary",)), rider=rider)


def _ffn_forward(tag, x, g, sh, sc, gt, wgT, wuT, wd, S, gather=None):
    T, D = x.shape
    F = wd.shape[0]

    def gateup(accs):
        a, u = accs
        return [a, u, a * _sigmoid(a) * u]

    h, a, u, s, *land = _norm_matmul(f"{tag}_gateup", x, g, sc, sh, [wgT, wuT], gateup, [(F, BF16)] * 3, S,
                                     rider=None if gather is None else _gather_direct_rider(gather))

    def down(accs, ex):
        xv, gtv = ex
        return [xv + 0.5 * gtv * accs[0], accs[0]]

    tmd = _div(S, 512, 8)
    x_new, y, *land = _matmul(f"{tag}_down", "nn", [[(s, wd)]], T, D, F, tmd, D, F, [F32, BF16],
                              extras=[(x, "tile", 0), (gt, "brow", 0)], epilogue=down, rows_per_example=S,
                              rider=None if gather is None else _gather_forward_rider(land[0]))
    return x_new, (x, h, a, u, s, y), (land[0] if land else None)


def _ffn_backward(tag, dx_out, saved, g, sc, gt, wgT, wuT, wd, S, rider=None, dh_rider=None):
    x, h, a, u, s, y = saved
    T, D = x.shape
    F = wd.shape[0]

    def act_grad(ds, ex):
        av, uv = ex[0].astype(F32), ex[1].astype(F32)
        sg = _sigmoid(av)
        return [ds * uv * (sg * (1.0 + av * (1.0 - sg))), ds * (av * sg)]

    dy, dgt, da, du, *rode = _gated_grad_matmul(f"{tag}_act_grad", dx_out, y, gt, 0.5, wd, [(a, 0), (u, 0)], act_grad,
                                                [BF16, BF16], S, rider=rider)
    tkw = _div(T, 2048, LANES)
    tmw = _div(F, 1408, LANES)
    dwd = _matmul(f"{tag}_dw_down", "tn", [[(s, dy)]], F, D, T, tmw, D, tkw, [F32])[0]
    dwgT = _matmul(f"{tag}_dw_gate", "tn", [[(da, h)]], F, D, T, tmw, D, tkw, [F32])[0]
    dwuT = _matmul(f"{tag}_dw_up", "tn", [[(du, h)]], F, D, T, tmw, D, tkw, [F32])[0]
    dx_in, dsh, dsc, dg, *rode_dh = _matmul_normmod_bwd(
        f"{tag}_dh", [(da, wgT), (du, wuT)], x, g, sc, dx_out, S,
        rider=None if dh_rider is None else dh_rider(dwgT, dwuT, dwd))
    return dx_in, (dsh, dsc, dgt, dg), (dwgT, dwuT, dwd), rode + rode_dh


def _loss_head(x, tgt, g, S):
    T, D = x.shape

    def fn(xv, tv, gv):
        xhat, rstd = _rms_parts(xv)
        e = xhat * gv - tv
        loss = jnp.broadcast_to(0.5 / D * jnp.sum(_colsum(e * e), axis=1, keepdims=True), (1, LANES))
        dy = e * (1.0 / D)
        dxh = dy * gv
        dx = rstd * (dxh - xhat * jnp.mean(dxh * xhat, axis=-1, keepdims=True))
        return [dx, loss, _colsum(dy * xhat)]

    return _rowwise("loss_head", fn, T, _div(S, 512, 8), [(x, "row", None), (tgt, "row", None), (g, "vec", None)],
                    [("row", D, F32), ("bacc", LANES, F32), ("bacc", D, F32)], S)


def _cumsum(v):
    B, S, _ = v.shape
    rows = _div(S, 1024, BLOCK)

    def body(x_ref, o_ref, carry):
        i = pl.program_id(1)

        @pl.when(i == 0)
        def _():
            carry[...] = jnp.zeros_like(carry)

        r = lax.broadcasted_iota(jnp.int32, (BLOCK, BLOCK), 0)
        c = lax.broadcasted_iota(jnp.int32, (BLOCK, BLOCK), 1)
        tri = (c <= r).astype(F32)
        last = carry[0:1, :]
        for j in range(0, rows, BLOCK):
            cum = jnp.dot(tri, x_ref[j:j + BLOCK, :], precision=lax.Precision.HIGHEST, preferred_element_type=F32) + last
            o_ref[j:j + BLOCK, :] = cum
            last = cum[BLOCK - 1:BLOCK, :]
        carry[...] = jnp.broadcast_to(last, carry.shape)

    return pl.pallas_call(
        body, name="cumsum", out_shape=jax.ShapeDtypeStruct(v.shape, F32), grid=(B, S // rows),
        in_specs=[pl.BlockSpec((None, rows, LANES), lambda b, i: (b, i, 0))],
        out_specs=pl.BlockSpec((None, rows, LANES), lambda b, i: (b, i, 0)),
        scratch_shapes=[pltpu.VMEM((8, LANES), F32)],
        compiler_params=_params(("arbitrary", "arbitrary")),
    )(v)


def _with_ones(x):
    lane = lax.broadcasted_iota(jnp.int32, (x.shape[0], HEAD_DIM), 1)
    return jnp.concatenate([x, jnp.where(lane == 0, 1.0, 0.0).astype(x.dtype)], axis=1)


def _causal_strip(s, r):
    qpos = r + lax.broadcasted_iota(jnp.int32, s.shape, 0)
    kpos = lax.broadcasted_iota(jnp.int32, s.shape, 1)
    return jnp.where(kpos <= qpos, s, NEG_INF)


NT = (((1,), (1,)), ((), ()))
NN = (((1,), (0,)), ((), ()))
TN = (((0,), (0,)), ((), ()))


def _fox_fwd(pm3, cumT, qcol, kcol, vcol, tq, rider=None):
    B, S, _ = pm3.shape
    nq = S // tq
    strips = range(0, tq, FOX_STRIP)

    def body(q_ref, k_ref, v_ref, ck_ref, o_ref, o32_ref, lse_ref, s_sc, p_sc, al_sc, m_sc, acc_sc):
        qi, kj = pl.program_id(1), pl.program_id(2)

        @pl.when(kj == 0)
        def _():
            m_sc[...] = jnp.full_like(m_sc, NEG_INF)
            acc_sc[...] = jnp.zeros_like(acc_sc)

        def tile(diagonal):
            def scores(h):
                hs = slice(HEAD_DIM * h, HEAD_DIM * (h + 1))
                s_sc[h % 2] = lax.dot_general(q_ref[:, hs] * SCALE, k_ref[:, hs], NT, preferred_element_type=F32)

            def accumulate(h):
                hs = slice(HEAD_DIM * h, HEAD_DIM * (h + 1))
                acc_sc[h] = al_sc[h % 2] * acc_sc[h] + lax.dot_general(p_sc[h % 2], _with_ones(v_ref[:, hs]), NN,
                                                                       preferred_element_type=F32)

            scores(0)
            for h in range(FOX_HEADS):
                b = h % 2
                if h + 1 < FOX_HEADS:
                    scores(h + 1)
                if h >= 1:
                    accumulate(h - 1)
                ck = ck_ref[h:h + 1, :]
                for r in strips:
                    rows = slice(r, r + FOX_STRIP)
                    s = s_sc[b, rows, :] - ck
                    if diagonal:
                        s = _causal_strip(s, r)
                    m_prev = m_sc[h, rows, :]
                    m_new = jnp.maximum(m_prev, jnp.max(s, axis=-1, keepdims=True))
                    p_sc[b, rows, :] = jnp.exp(s - m_new).astype(BF16)
                    al_sc[b, rows, :] = jnp.exp(m_prev - m_new)
                    m_sc[h, rows, :] = m_new
            accumulate(FOX_HEADS - 1)

        @pl.when(kj < qi)
        def _():
            tile(False)

        @pl.when(kj == qi)
        def _():
            tile(True)

        @pl.when(kj == nq - 1)
        def _():
            lse_ref[...] = jnp.zeros_like(lse_ref)
            for h in range(FOX_HEADS):
                hs = slice(HEAD_DIM * h, HEAD_DIM * (h + 1))
                acc = acc_sc[h]
                l = acc[:, HEAD_DIM:HEAD_DIM + 1]
                oh = acc[:, :HEAD_DIM] / l
                o_ref[:, hs] = oh.astype(o_ref.dtype)
                o32_ref[:, hs] = oh
                lse_ref[:, h:h + 1] = m_sc[h] + jnp.log(l)

    ospec = pl.BlockSpec((None, tq, FOX_W), lambda b, i, j: (b, i, 0))
    return _pcall(
        body, name="fox_forward",
        out_shape=[jax.ShapeDtypeStruct((B, S, FOX_W), BF16), jax.ShapeDtypeStruct((B, S, FOX_W), F32),
                   jax.ShapeDtypeStruct((B, S, LANES), F32)],
        grid=(B, nq, nq),
        in_specs=[pl.BlockSpec((None, tq, FOX_W), lambda b, i, j: (b, i, qcol)),
                  pl.BlockSpec((None, tq, FOX_W), lambda b, i, j: (b, jnp.minimum(i, j), kcol)),
                  pl.BlockSpec((None, tq, FOX_W), lambda b, i, j: (b, jnp.minimum(i, j), vcol)),
                  pl.BlockSpec((None, 8, tq), lambda b, i, j: (b, 0, jnp.minimum(i, j)))],
        operands=[pm3, pm3, pm3, cumT],
        out_specs=[ospec, ospec, pl.BlockSpec((None, tq, LANES), lambda b, i, j: (b, i, 0))],
        scratch_shapes=[pltpu.VMEM((2, tq, tq), F32), pltpu.VMEM((2, tq, tq), BF16), pltpu.VMEM((2, tq, 1), F32),
                        pltpu.VMEM((FOX_HEADS, tq, 1), F32), pltpu.VMEM((FOX_HEADS, tq, LANES), F32)],
        params=_params(("parallel", "parallel", "arbitrary")), rider=rider)


def _fox_bwd(pm3, do, delta, lse, cumT, qcol, kcol, vcol, tq, rider=None):
    B, S, _ = pm3.shape
    nq = S // tq
    strips = range(0, tq, FOX_STRIP)

    def body(q_ref, k_ref, v_ref, do_ref, dl_ref, lse_ref, ck_ref, dq_ref, rs_ref, dk_ref, dv_ref, cs_ref,
             s_sc, dp_sc, p_sc, ds_sc, dq_sc, dk_sc, dv_sc):
        kj, qi = pl.program_id(1), pl.program_id(2)

        @pl.when((kj == 0) & (qi == 0))
        def _():
            dq_sc[...] = jnp.zeros_like(dq_sc)

        @pl.when(qi == 0)
        def _():
            dk_sc[...] = jnp.zeros_like(dk_sc)
            dv_sc[...] = jnp.zeros_like(dv_sc)

        def tile(diagonal):
            qrows = pl.ds(pl.multiple_of(qi * tq, tq), tq)
            for h in range(FOX_HEADS):
                hs = slice(HEAD_DIM * h, HEAD_DIM * (h + 1))
                qh, kh, doh = q_ref[:, hs] * SCALE, k_ref[:, hs], do_ref[:, hs]
                b = h % 2
                s_sc[b] = lax.dot_general(qh, kh, NT, preferred_element_type=F32)
                dp_sc[b] = lax.dot_general(doh, v_ref[:, hs], NT, preferred_element_type=F32)
                ck = ck_ref[h:h + 1, :]
                for r in strips:
                    rows = slice(r, r + FOX_STRIP)
                    s = s_sc[b, rows, :] - ck
                    if diagonal:
                        s = _causal_strip(s, r)
                    p = jnp.exp(s - lse_ref[rows, h:h + 1])
                    p_sc[b, rows, :] = p.astype(BF16)
                    ds_sc[b, rows, :] = (p * (dp_sc[b, rows, :] - dl_ref[rows, h:h + 1])).astype(BF16)
                dv_sc[h] += lax.dot_general(doh, p_sc[b], TN, preferred_element_type=F32)
                dk_sc[h] += lax.dot_general(_with_ones(qh), ds_sc[b], TN, preferred_element_type=F32)
                dq_sc[h, qrows, :] += lax.dot_general(ds_sc[b], _with_ones(kh), NN, preferred_element_type=F32)

        @pl.when(qi > kj)
        def _():
            tile(False)

        @pl.when(qi == kj)
        def _():
            tile(True)

        @pl.when(qi == nq - 1)
        def _():
            cs_ref[...] = jnp.zeros_like(cs_ref)
            for h in range(FOX_HEADS):
                hs = slice(HEAD_DIM * h, HEAD_DIM * (h + 1))
                dv_ref[:, hs] = dv_sc[h].T.astype(dv_ref.dtype)
                dk = dk_sc[h].T
                dk_ref[:, hs] = dk[:, :HEAD_DIM].astype(dk_ref.dtype)
                cs_ref[:, h:h + 1] = dk[:, HEAD_DIM:HEAD_DIM + 1]

        @pl.when((kj == nq - 1) & (qi == nq - 1))
        def _():
            rs_ref[...] = jnp.zeros_like(rs_ref)
            for h in range(FOX_HEADS):
                hs = slice(HEAD_DIM * h, HEAD_DIM * (h + 1))
                dq_ref[:, hs] = (dq_sc[h, :, :HEAD_DIM] * SCALE).astype(dq_ref.dtype)
                rs_ref[:, h:h + 1] = dq_sc[h, :, HEAD_DIM:HEAD_DIM + 1]

    def qside(width, col=0):
        return pl.BlockSpec((None, tq, width), lambda b, j, i: (b, jnp.maximum(i, j), col))

    kspec = pl.BlockSpec((None, tq, FOX_W), lambda b, j, i: (b, j, 0))
    return _pcall(
        body, name="fox_backward",
        out_shape=[jax.ShapeDtypeStruct((B, S, FOX_W), BF16), jax.ShapeDtypeStruct((B, S, LANES), F32),
                   jax.ShapeDtypeStruct((B, S, FOX_W), BF16), jax.ShapeDtypeStruct((B, S, FOX_W), BF16),
                   jax.ShapeDtypeStruct((B, S, LANES), F32)],
        grid=(B, nq, nq),
        in_specs=[qside(FOX_W, qcol),
                  pl.BlockSpec((None, tq, FOX_W), lambda b, j, i: (b, j, kcol)),
                  pl.BlockSpec((None, tq, FOX_W), lambda b, j, i: (b, j, vcol)),
                  qside(FOX_W), qside(LANES), qside(LANES),
                  pl.BlockSpec((None, 8, tq), lambda b, j, i: (b, 0, j))],
        operands=[pm3, pm3, pm3, do, delta, lse, cumT],
        out_specs=[pl.BlockSpec((None, S, FOX_W), lambda b, j, i: (b, 0, 0)),
                   pl.BlockSpec((None, S, LANES), lambda b, j, i: (b, 0, 0)),
                   kspec, kspec, pl.BlockSpec((None, tq, LANES), lambda b, j, i: (b, j, 0))],
        scratch_shapes=[pltpu.VMEM((2, tq, tq), F32), pltpu.VMEM((2, tq, tq), F32), pltpu.VMEM((2, tq, tq), BF16),
                        pltpu.VMEM((2, tq, tq), BF16), pltpu.VMEM((FOX_HEADS, S, LANES), F32),
                        pltpu.VMEM((FOX_HEADS, LANES, tq), F32), pltpu.VMEM((FOX_HEADS, HEAD_DIM, tq), F32)],
        params=_params(("parallel", "arbitrary", "arbitrary")), rider=rider)


def _fox_delta(do, o32, T, S):
    def fn(dov, ov):
        prod = dov.astype(F32) * ov
        lane = lax.broadcasted_iota(jnp.int32, (dov.shape[0], LANES), 1)
        delta = jnp.zeros((dov.shape[0], LANES), F32)
        for h in range(FOX_HEADS):
            hs = slice(HEAD_DIM * h, HEAD_DIM * (h + 1))
            delta = jnp.where(lane == h, jnp.sum(prod[:, hs], axis=-1, keepdims=True), delta)
        return [delta]

    return _rowwise("fox_delta", fn, T, _div(S, 512, 8), [(do, "row", None), (o32, "row", None)], [("row", LANES, F32)], S)[0]


def _alibi_slope(group, head):
    return 2.0 ** (-ALIBI_MAX_BIAS * (group * DIL_HPG + head + 1) / (N_DIL * DIL_HPG))


def _residue_order(a, B, S, d):
    C = a.shape[-1]
    if d == 1:
        return a.reshape(B, S, C)
    return a.reshape(B, S // d, d, C).transpose(0, 2, 1, 3).reshape(B * d, S // d, C)


def _token_order(a, B, S, d):
    C = a.shape[-1]
    if d == 1:
        return a.reshape(B * S, C)
    return a.reshape(B, d, S // d, C).transpose(0, 2, 1, 3).reshape(B * S, C)


def _band_scores(qh, kcat, slope_d, has_prev):
    qi = lax.broadcasted_iota(jnp.int32, (BLOCK, 2 * BLOCK), 0)
    c = lax.broadcasted_iota(jnp.int32, (BLOCK, 2 * BLOCK), 1)
    s = lax.dot_general(qh, kcat, NT, preferred_element_type=F32) - slope_d * (BLOCK + qi - c).astype(F32)
    valid = (c >= qi) & (c <= qi + BLOCK)
    if has_prev is not None:
        valid = valid & ((c >= BLOCK) | has_prev)
    return jnp.where(valid, s, NEG_INF)


def _band_operands(j, cur_ref, prev_ref, hs):
    if j == 0:
        return jnp.concatenate([prev_ref[:, hs], cur_ref[0:BLOCK, hs]], axis=0)
    return cur_ref[(j - 1) * BLOCK:(j + 1) * BLOCK, hs]


def _dil_specs(Ls, qb, cols):
    nsub = qb // BLOCK
    qcol, kcol, vcol = cols

    def cur(col):
        return pl.BlockSpec((None, qb, DIL_GW), lambda s, n: (s, n, col))

    def prev(col):
        return pl.BlockSpec((None, BLOCK, DIL_GW), lambda s, n: (s, jnp.maximum(n * nsub - 1, 0), col))

    return [cur(qcol), cur(kcol), prev(kcol), cur(vcol), prev(vcol)]


def _dil_fwd(group, src, cols):
    _, dilation = DIL_GROUPS[group]
    nseq, Ls, _ = src.shape
    qb = _div(Ls, 512, BLOCK)
    nsub = qb // BLOCK

    def body(q_ref, kc_ref, kp_ref, vc_ref, vp_ref, o_ref, lse_ref):
        has_prev = pl.program_id(1) > 0
        lse_ref[...] = jnp.zeros_like(lse_ref)
        for h in range(DIL_HPG):
            hs = slice(HEAD_DIM * h, HEAD_DIM * (h + 1))
            scores = [_band_scores(q_ref[j * BLOCK:(j + 1) * BLOCK, hs] * SCALE, _band_operands(j, kc_ref, kp_ref, hs),
                                   _alibi_slope(group, h) * dilation, has_prev if j == 0 else None) for j in range(nsub)]
            pending = None

            def write(j, m, acc):
                rows = slice(j * BLOCK, (j + 1) * BLOCK)
                l = acc[:, HEAD_DIM:HEAD_DIM + 1]
                o_ref[rows, hs] = acc[:, :HEAD_DIM] / l
                lse_ref[rows, h:h + 1] = m + jnp.log(l)

            for j in range(nsub):
                m = jnp.max(scores[j], axis=-1, keepdims=True)
                p = jnp.exp(scores[j] - m).astype(BF16)
                acc = lax.dot_general(p, _with_ones(_band_operands(j, vc_ref, vp_ref, hs)), NN, preferred_element_type=F32)
                if pending is not None:
                    write(*pending)
                pending = (j, m, acc)
            write(*pending)

    return pl.pallas_call(
        body, name=f"dil_forward_{group}",
        out_shape=[jax.ShapeDtypeStruct((nseq, Ls, DIL_GW), F32), jax.ShapeDtypeStruct((nseq, Ls, LANES), F32)],
        grid=(nseq, Ls // qb),
        in_specs=_dil_specs(Ls, qb, cols),
        out_specs=[pl.BlockSpec((None, qb, DIL_GW), lambda s, n: (s, n, 0)),
                   pl.BlockSpec((None, qb, LANES), lambda s, n: (s, n, 0))],
        compiler_params=_params(("parallel", "arbitrary")),
    )(src, src, src, src, src)


def _dil_bwd(group, src, cols, Lr, dyr, dlr):
    _, dilation = DIL_GROUPS[group]
    nseq, Ls, _ = src.shape
    qb = _div(Ls, 512, BLOCK)
    nsub, nb = qb // BLOCK, Ls // qb

    def body(q_ref, kc_ref, kp_ref, vc_ref, vp_ref, L_ref, dy_ref, dl_ref, dq_ref, dk_ref, dv_ref, dk_sc, dv_sc):
        n = pl.program_id(1)
        has_prev = n > 0

        @pl.when(n == 0)
        def _():
            dk_sc[...] = jnp.zeros_like(dk_sc)
            dv_sc[...] = jnp.zeros_like(dv_sc)

        base = pl.multiple_of(n * qb, BLOCK)
        for h in range(DIL_HPG):
            hs = slice(HEAD_DIM * h, HEAD_DIM * (h + 1))
            blocks = [slice(j * BLOCK, (j + 1) * BLOCK) for j in range(nsub)]
            qhs = [q_ref[rows, hs] * SCALE for rows in blocks]
            kcats = [_band_operands(j, kc_ref, kp_ref, hs) for j in range(nsub)]
            dyhs = [dy_ref[rows, hs] for rows in blocks]
            scores = [_band_scores(qhs[j], kcats[j], _alibi_slope(group, h) * dilation, has_prev if j == 0 else None)
                      for j in range(nsub)]
            dps = [lax.dot_general(dyhs[j], _band_operands(j, vc_ref, vp_ref, hs), NT, preferred_element_type=F32)
                   for j in range(nsub)]
            pending = None

            def write(j, dq, dk, dv):
                dq_ref[blocks[j], hs] = (dq * SCALE).astype(dq_ref.dtype)
                win = pl.ds(base + j * BLOCK, 2 * BLOCK)
                dk_sc[win, hs] += dk
                dv_sc[win, hs] += dv

            for j in range(nsub):
                p = jnp.exp(scores[j] - L_ref[blocks[j], h:h + 1])
                ds = (p * (dps[j] - dl_ref[blocks[j], h:h + 1])).astype(BF16)
                dq = lax.dot_general(ds, kcats[j], NN, preferred_element_type=F32)
                dk = lax.dot_general(ds, qhs[j], TN, preferred_element_type=F32)
                dv = lax.dot_general(p.astype(BF16), dyhs[j], TN, preferred_element_type=F32)
                if pending is not None:
                    write(*pending)
                pending = (j, dq, dk, dv)
            write(*pending)

        @pl.when(n == nb - 1)
        def _():
            dk_ref[...] = dk_sc[BLOCK:, :].astype(dk_ref.dtype)
            dv_ref[...] = dv_sc[BLOCK:, :].astype(dv_ref.dtype)

    own = pl.BlockSpec((None, qb, DIL_GW), lambda s, n: (s, n, 0))
    own128 = pl.BlockSpec((None, qb, LANES), lambda s, n: (s, n, 0))
    whole = pl.BlockSpec((None, Ls, DIL_GW), lambda s, n: (s, 0, 0))
    shape = jax.ShapeDtypeStruct((nseq, Ls, DIL_GW), BF16)
    return pl.pallas_call(
        body, name=f"dil_backward_{group}",
        out_shape=[shape, shape, shape],
        grid=(nseq, nb),
        in_specs=_dil_specs(Ls, qb, cols) + [own128, own, own128],
        out_specs=[own, whole, whole],
        scratch_shapes=[pltpu.VMEM((Ls + BLOCK, DIL_GW), F32), pltpu.VMEM((Ls + BLOCK, DIL_GW), F32)],
        compiler_params=_params(("parallel", "arbitrary")),
    )(src, src, src, src, src, Lr, dyr, dlr)


def _dil_combine(os_, lses, T, S):
    def fn(o0, o1, o2, l0, l1, l2):
        m = jnp.maximum(jnp.maximum(l0, l1), l2)
        e0, e1, e2 = jnp.exp(l0 - m), jnp.exp(l1 - m), jnp.exp(l2 - m)
        tot = e0 + e1 + e2
        w0, w1, w2 = e0 / tot, e1 / tot, e2 / tot
        parts = []
        for h in range(DIL_HPG):
            hs = slice(HEAD_DIM * h, HEAD_DIM * (h + 1))
            parts.append(w0[:, h:h + 1] * o0[:, hs] + w1[:, h:h + 1] * o1[:, hs] + w2[:, h:h + 1] * o2[:, hs])
        return [jnp.concatenate(parts, axis=1), m + jnp.log(tot)]

    ins = [(a, "row", None) for a in os_] + [(a, "row", None) for a in lses]
    return _rowwise("dil_combine", fn, T, _div(S, 512, 8), ins, [("row", DIL_GW, BF16), ("row", LANES, F32)], S)


def _dil_delta(dy, y, T, S):
    def fn(dyv, yv):
        prod = dyv * yv.astype(F32)
        lane = lax.broadcasted_iota(jnp.int32, (dyv.shape[0], LANES), 1)
        delta = jnp.zeros((dyv.shape[0], LANES), F32)
        for h in range(DIL_HPG):
            hs = slice(HEAD_DIM * h, HEAD_DIM * (h + 1))
            delta = jnp.where(lane == h, jnp.sum(prod[:, hs], axis=-1, keepdims=True), delta)
        return [delta, dyv]

    return _rowwise("dil_delta", fn, T, _div(S, 512, 8), [(dy, "row", None), (y, "row", None)],
                    [("row", LANES, F32), ("row", DIL_GW, BF16)], S)


def _ada_forward(c_all, w, b):
    n, D = c_all.shape
    cl = w.shape[1]

    def body(c_ref, w_ref, b_ref, o_ref, ca_ref):
        cv = c_ref[...]
        ca = (cv * _sigmoid(cv)).astype(BF16)
        ca_ref[...] = ca
        o_ref[...] = jnp.dot(ca, w_ref[...].astype(BF16), preferred_element_type=F32) + b_ref[...]

    return pl.pallas_call(
        body, name="ada_forward",
        out_shape=[jax.ShapeDtypeStruct((n, cl), F32), jax.ShapeDtypeStruct((n, D), BF16)],
        compiler_params=_params(),
    )(c_all, w, b)


def _ada_backward(ca, dmod_cols, dmod_all):
    n, D = ca.shape
    cl = dmod_cols.shape[1]

    def body(ca_ref, dc_ref, da_ref, gw_ref, gb_ref):
        gw_ref[...] = lax.dot_general(ca_ref[...], dc_ref[...].astype(BF16), (((0,), (0,)), ((), ())), preferred_element_type=F32)
        gb_ref[...] = _colsum(da_ref[...])

    return pl.pallas_call(
        body, name="ada_backward",
        out_shape=[jax.ShapeDtypeStruct((D, cl), F32), jax.ShapeDtypeStruct((1, dmod_all.shape[1]), F32)],
        compiler_params=_params(),
    )(ca, dmod_cols, dmod_all)


def _sum_devices(v):
    def body(v_ref, o_ref):
        tot = v_ref[0]
        for k in range(1, N_DEV):
            tot = tot + v_ref[k]
        o_ref[...] = tot

    return pl.pallas_call(body, name="sum_devices", out_shape=jax.ShapeDtypeStruct(v.shape[1:], F32))(v)


def _adamw(name, w, g, m, v):
    rows, cols = w.shape
    tr = _div(rows, 256, 8)

    def body(w_ref, g_ref, m_ref, v_ref, d_ref, nm_ref, nv_ref):
        gv = g_ref[...]
        nm = ADAM_B1 * m_ref[...] + (1.0 - ADAM_B1) * gv
        nv = ADAM_B2 * v_ref[...] + (1.0 - ADAM_B2) * (gv * gv)
        m_hat = nm / (1.0 - ADAM_B1 ** ADAM_STEP)
        v_hat = nv / (1.0 - ADAM_B2 ** ADAM_STEP)
        d_ref[...] = -ADAM_LR * (m_hat / (jnp.sqrt(v_hat) + ADAM_EPS) + ADAM_WD * w_ref[...])
        nm_ref[...] = nm
        nv_ref[...] = nv

    spec = pl.BlockSpec((tr, cols), lambda i: (i, 0))
    shape = jax.ShapeDtypeStruct((rows, cols), F32)
    return pl.pallas_call(
        body, name=name, out_shape=[shape, shape, shape], grid=(rows // tr,),
        in_specs=[spec, spec, spec, spec], out_specs=[spec, spec, spec],
        compiler_params=_params(("arbitrary",)),
    )(w, g, m, v)


def _pad_rows(a, rows):
    return a if a.shape[0] == rows else jnp.pad(a, ((0, rows - a.shape[0]), (0, 0)))


class _Packed:
    def __init__(self, kind, local_shape, D):
        self.kind, self.local_shape, self.D = kind, local_shape, D
        r, c = local_shape
        self.rows = {"T": c, "N": r, "F": r * c // D}[kind]
        self.rows_pad = -(-self.rows // ROW_ALIGN) * ROW_ALIGN

    def pack_local(self, w):
        if self.kind == "T":
            w = w.T
        elif self.kind == "F":
            w = w.reshape(self.rows, self.D)
        return _pad_rows(w, self.rows_pad)

    def full(self, gathered):
        g = gathered[:, :self.rows]
        if self.kind == "F":
            r, c = self.local_shape
            return g.reshape(N_DEV, r, c).transpose(1, 0, 2).reshape(r, N_DEV * c)
        return g.reshape(N_DEV * self.rows, self.D)

    def pack_grad(self, gfull):
        if self.kind == "F":
            r, c = self.local_shape
            g = gfull.reshape(r, N_DEV, c).transpose(1, 0, 2).reshape(N_DEV, self.rows, self.D)
        else:
            g = gfull.reshape(N_DEV, self.rows, self.D)
        if self.rows_pad != self.rows:
            g = jnp.pad(g, ((0, 0), (0, self.rows_pad - self.rows), (0, 0)))
        return g

    def unpack_local(self, g):
        g = g[:self.rows]
        if self.kind == "T":
            return g.T
        if self.kind == "F":
            return g.reshape(self.local_shape)
        return g


BIG = ["ffn1_w_gate", "ffn1_w_up", "ffn1_w_down", "w_in", "w_branch_a", "w_branch_b", "w_out",
       "ffn2_w_gate", "ffn2_w_up", "ffn2_w_down"]
BIG_KIND = {"ffn1_w_gate": "T", "ffn1_w_up": "T", "ffn1_w_down": "N", "w_in": "T", "w_branch_a": "F", "w_branch_b": "F",
            "w_out": "N", "ffn2_w_gate": "T", "ffn2_w_up": "T", "ffn2_w_down": "N"}
GROUPS = (("ffn1_w_gate", "ffn1_w_up", "ffn1_w_down"), ("w_in", "w_branch_a", "w_branch_b", "w_out"),
          ("ffn2_w_gate", "ffn2_w_up", "ffn2_w_down"))
SMALL = ["ada_b", "norm_ffn1", "norm_mix", "forget_bias", "norm_ffn2", "norm_final"]


def kernel(x, c, ada_w, ada_b, norm_ffn1, ffn1_w_gate, ffn1_w_up, ffn1_w_down, norm_mix, w_in, forget_bias, w_branch_a, w_branch_b, w_out, norm_ffn2, ffn2_w_gate, ffn2_w_up, ffn2_w_down, norm_final, loss_target, m_ada_w, m_ada_b, m_norm_ffn1, m_ffn1_w_gate, m_ffn1_w_up, m_ffn1_w_down, m_norm_mix, m_w_in, m_forget_bias, m_w_branch_a, m_w_branch_b, m_w_out, m_norm_ffn2, m_ffn2_w_gate, m_ffn2_w_up, m_ffn2_w_down, m_norm_final, v_ada_w, v_ada_b, v_norm_ffn1, v_ffn1_w_gate, v_ffn1_w_up, v_ffn1_w_down, v_norm_mix, v_w_in, v_forget_bias, v_w_branch_a, v_w_branch_b, v_w_out, v_norm_ffn2, v_ffn2_w_gate, v_ffn2_w_up, v_ffn2_w_down, v_norm_final):
    args = dict(locals())
    B, S, D = x.shape
    T = B * S
    cl = ada_w.shape[2]
    n_in = w_in.shape[2] * N_DEV
    nm = 2 * D + 3 * FOX_W + 3 * DIL_W
    nmp = -(-nm // 512) * 512
    GA, GB, QB, QA = 0, D, 2 * D, 2 * D + 3 * FOX_W
    xpos, ypos, cpos = _position()
    me = 4 * xpos + 2 * ypos + cpos

    packs = {n: _Packed(BIG_KIND[n], args[n].shape[1:], D) for n in BIG}
    offs, pads = {}, {}
    for names in GROUPS:
        r = 0
        for n in names:
            offs[n] = r
            r += packs[n].rows_pad
        pads[names] = -r % PACK_ROW_QUANTUM

    def pack_weights(names):
        return jnp.concatenate([packs[n].pack_local(args[n][0]).astype(BF16) for n in names]
                               + [jnp.zeros((pads[names], D), BF16)], axis=0)

    def unpack_weights(names, land):
        out = {}
        for n in names:
            p = packs[n]
            if p.kind in "TN" and p.rows == p.rows_pad and offs[n] % p.rows == 0:
                out[n] = _Slab(land, offs[n], p.rows)
            else:
                out[n] = p.full(land[:, offs[n]:offs[n] + p.rows_pad])
        return out

    def pack_grads(names, gfull):
        return jnp.concatenate([packs[n].pack_grad(gfull[n]) for n in names] + [jnp.zeros((N_DEV, pads[names], D), F32)], axis=1)

    def unpack_grads(names, g_local):
        return {n: packs[n].unpack_local(g_local[offs[n]:offs[n] + packs[n].rows_pad])[None] for n in names}

    W = unpack_weights(GROUPS[0], _weight_allgather(pack_weights(GROUPS[0])))

    c_all = _small_allgather(c, "gather_c").reshape(N_DEV * B, D)
    b_cols = lax.dynamic_slice(ada_b, (0, me * cl), (1, cl))
    mod_cols, c_act = _ada_forward(c_all, ada_w[0], b_cols)
    mod_all = _small_allgather(mod_cols, "gather_mod").transpose(1, 0, 2).reshape(N_DEV * B, N_MOD * D)
    mod = lax.dynamic_slice(mod_all, (me * B, 0), (B, N_MOD * D)).reshape(B, N_MOD, 1, D)
    sh1, sc1, gt1, sh2, sc2, gt2, sh3, sc3, gt3 = [mod[:, i] for i in range(N_MOD)]

    x0 = x.reshape(T, D)
    x1, saved1, land = _ffn_forward("ffn1", x0, norm_ffn1, sh1, sc1, gt1, W["ffn1_w_gate"], W["ffn1_w_up"], W["ffn1_w_down"], S,
                                    gather=pack_weights(GROUPS[1]))
    W.update(unpack_weights(GROUPS[1], land))
    winT = W["w_in"]
    o_f = 3 * DIL_W + 3 * FOX_W
    wmT = jnp.concatenate([winT[o_f + 8:], winT[3 * DIL_W:o_f], winT[:3 * DIL_W], jnp.zeros((nmp - nm, D), BF16)], axis=0)
    wfT = jnp.concatenate([winT[o_f:o_f + 8], jnp.zeros((LANES - 8, D), BF16)], axis=0)

    tm1k = _div(T, 1024, 8)
    fb = jnp.pad(forget_bias, ((0, 0), (0, LANES - FOX_HEADS)))

    def proj(accs, fbv):
        fl = accs[1] + fbv
        lane = lax.broadcasted_iota(jnp.int32, fl.shape, 1)
        ls = jnp.minimum(fl, 0.0) - jnp.log(1.0 + jnp.exp(-jnp.abs(fl)))
        return [accs[0], jnp.where(lane < FOX_HEADS, ls, 0.0), fl]

    tms = _div(S, 512, 8)
    h2, pm, logsig, flog, land = _norm_matmul("mix_proj", x1, norm_mix, sc2, sh2, [wmT, wfT], proj,
                                              [(nmp, BF16), (LANES, F32), (LANES, F32)], S, vecs=[fb],
                                              rider=_gather_direct_rider(pack_weights(GROUPS[2])))
    cum = _cumsum(logsig.reshape(B, S, LANES))
    cumT = cum[:, :, :8].transpose(0, 2, 1)
    pm3 = pm.reshape(B, S, nmp)
    tq = _div(S, 512, LANES)
    qcol, kcol, vcol = QB // FOX_W, QB // FOX_W + 1, QB // FOX_W + 2
    o_b, o_b32, lse_b, land = _fox_fwd(pm3, cumT, qcol, kcol, vcol, tq, rider=_gather_forward_rider(land))
    W.update(unpack_weights(GROUPS[2], land))
    y_b = o_b.reshape(T, FOX_W)

    qa_blk = QA // DIL_GW
    dil_src, dil_cols = [], []
    for g, (_, d) in enumerate(DIL_GROUPS):
        if d == 1:
            dil_src.append(pm3)
            dil_cols.append((qa_blk + g, qa_blk + N_DIL + g, qa_blk + 2 * N_DIL + g))
        else:
            starts = [QA + (i * N_DIL + g) * DIL_GW for i in range(3)]
            qkv = jnp.concatenate([pm[:, c:c + DIL_GW] for c in starts], axis=1)
            dil_src.append(_residue_order(qkv, B, S, d))
            dil_cols.append((0, 1, 2))
    dil_o, dil_lse = [], []
    for g, (_, d) in enumerate(DIL_GROUPS):
        o_g, lse_g = _dil_fwd(g, dil_src[g], dil_cols[g])
        dil_o.append(_token_order(o_g, B, S, d))
        dil_lse.append(_token_order(lse_g, B, S, d))
    y_a, L_a = _dil_combine(dil_o, dil_lse, T, S)

    wa, wb, wout = W["w_branch_a"], W["w_branch_b"], W["w_out"]
    tnd = D
    tm5 = _div(T, 512, 8)
    yap = _matmul("mix_branch_a", "nn", [[(y_a, wa)]], T, D, DIL_GW, tm5, tnd, DIL_GW, [BF16])[0]

    def merge(accs, ex):
        yapv, gav, gbv = ex
        ybp = accs[0]
        return [ybp, _sigmoid(gav.astype(F32)) * yapv.astype(F32) + _sigmoid(gbv.astype(F32)) * ybp]

    ybp, merged = _matmul("mix_branch_b", "nn", [[(y_b, wb)]], T, D, FOX_W, tm5, tnd, FOX_W, [BF16, BF16],
                          extras=[(yap, "tile", 0), (pm, "tile", GA), (pm, "tile", GB)], epilogue=merge)

    def out_proj(accs, ex):
        xv, gtv = ex
        return [xv + gtv * accs[0], accs[0]]

    x2, ymix = _matmul("mix_out", "nn", [[(merged, wout)]], T, D, D, tms, tnd, D, [F32, BF16],
                       extras=[(x1, "tile", 0), (gt2, "brow", 0)], epilogue=out_proj, rows_per_example=S)

    x3, saved3, _ = _ffn_forward("ffn2", x2, norm_ffn2, sh3, sc3, gt3, W["ffn2_w_gate"], W["ffn2_w_up"], W["ffn2_w_down"], S)

    dx3, loss_b, dg_final = _loss_head(x3, loss_target.reshape(T, D), norm_final.reshape(1, D), S)
    dx2, (dsh3, dsc3, dgt3, dg3), (dwg2, dwu2, dwd2), _ = _ffn_backward(
        "ffn2", dx3, saved3, norm_ffn2, sc3, gt3, W["ffn2_w_gate"], W["ffn2_w_up"], W["ffn2_w_down"], S)
    g3 = pack_grads(GROUPS[2], {"ffn2_w_gate": dwg2, "ffn2_w_up": dwu2, "ffn2_w_down": dwd2})

    def merge_grad(dm, ex):
        gav, gbv, yapv, ybpv = [e.astype(F32) for e in ex]
        sga, sgb = _sigmoid(gav), _sigmoid(gbv)
        return [dm * sga, dm * sgb, dm * yapv * sga * (1.0 - sga), dm * ybpv * sgb * (1.0 - sgb)]

    dym, dgt2, dyap, dybp, dga, dgb, sib3 = _gated_grad_matmul(
        "mix_merge_grad", dx2, ymix, gt2, 1.0, wout, [(pm, GA // D), (pm, GB // D), (yap, 0), (ybp, 0)], merge_grad, [BF16] * 4, S,
        rider=_sibling_exchange_rider(g3))
    sums3, own3 = _chip_sums(g3, sib3)
    tkw = _div(T, 1024, LANES)
    dwout = _matmul("mix_dw_out", "tn", [[(merged, dym)]], D, D, T, D, D, tkw, [F32])[0]
    dwa = _matmul("mix_dw_a", "tn", [[(y_a, dyap)]], DIL_GW, D, T, DIL_GW, D, tkw, [F32])[0]
    dwb = _matmul("mix_dw_b", "tn", [[(y_b, dybp)]], FOX_W, D, T, FOX_W, D, tkw, [F32])[0]
    dy_a = _matmul("mix_dy_a", "nt", [[(dyap, wa)]], T, DIL_GW, D, tm1k, DIL_GW, D, [F32])[0]
    dy_b = _matmul("mix_dy_b", "nt", [[(dybp, wb)]], T, FOX_W, D, tm1k, FOX_W, D, [BF16])[0]

    do3 = dy_b.reshape(B, S, FOX_W)
    delta_b = _fox_delta(dy_b, o_b32.reshape(T, FOX_W), T, S).reshape(B, S, LANES)
    dq_b, ds_rows, dk_b, dv_b, ds_cols, recv3 = _fox_bwd(pm3, do3, delta_b, lse_b, cumT, qcol, kcol, vcol, tq,
                                                         rider=_chip_exchange_rider(sums3))

    delta_a, dy_a16 = _dil_delta(dy_a, y_a, T, S)
    dqs, dks, dvs = [], [], []
    for g, (_, d) in enumerate(DIL_GROUPS):
        dq_g, dk_g, dv_g = _dil_bwd(g, dil_src[g], dil_cols[g], _residue_order(L_a, B, S, d),
                                    _residue_order(dy_a16, B, S, d), _residue_order(delta_a, B, S, d))
        dqs.append(_token_order(dq_g, B, S, d))
        dks.append(_token_order(dk_g, B, S, d))
        dvs.append(_token_order(dv_g, B, S, d))

    dcum = ds_rows - ds_cols
    dcum_run = _cumsum(dcum)
    dcum_tot = dcum_run[:, S - 1:S, :]

    def forget_grad_fn(run, dcv, fl, tot):
        lane = lax.broadcasted_iota(jnp.int32, fl.shape, 1)
        df = jnp.where(lane < FOX_HEADS, (tot - run + dcv) * _sigmoid_exp(-fl), 0.0)
        return [df, _colsum(df)]

    df16, dfb = _rowwise("forget_gate_grad", forget_grad_fn, T, tms,
                         [(dcum_run.reshape(T, LANES), "row", None), (dcum.reshape(T, LANES), "row", None), (flog, "row", None),
                          (dcum_tot, "bvec", None)],
                         [("row", LANES, BF16), ("bacc", LANES, F32)], S)

    dpm = jnp.concatenate([dga, dgb, dq_b.reshape(T, FOX_W), dk_b.reshape(T, FOX_W), dv_b.reshape(T, FOX_W)]
                          + dqs + dks + dvs + ([jnp.zeros((T, nmp - nm), BF16)] if nmp > nm else []), axis=1)
    tmn = _div(nmp, 2048, LANES)
    dwmT = _matmul("mix_dw_in", "tn", [[(dpm, h2)]], nmp, D, T, tmn, D, tkw, [F32])[0]
    dwfT = _matmul("mix_dw_f", "tn", [[(df16, h2)]], LANES, D, T, LANES, D, tkw, [F32])[0]
    dwinT = jnp.concatenate([dwmT[QA:QA + 3 * DIL_W], dwmT[QB:QB + 3 * FOX_W], dwfT[:8], dwmT[GA:2 * D]], axis=0)
    g2 = pack_grads(GROUPS[1], {"w_in": dwinT, "w_branch_a": dwa, "w_branch_b": dwb, "w_out": dwout})
    dx1, dsh2, dsc2, dgmix, sib2 = _matmul_normmod_bwd("mix_dh", [(dpm, wmT), (df16, wfT)], x1, norm_mix, sc2, dx2, S,
                                                       rider=_sibling_exchange_rider(g2))
    sums2, own2 = _chip_sums(g2, sib2)

    own1 = []

    def ffn1_exchange(dwg, dwu, dwd):
        sums1, own = _chip_sums(pack_grads(GROUPS[0], {"ffn1_w_gate": dwg, "ffn1_w_up": dwu, "ffn1_w_down": dwd}))
        own1.append(own)
        return _chip_exchange_rider(sums1)

    dx0, (dsh1, dsc1, dgt1, dg1), _, (recv2, recv1) = _ffn_backward(
        "ffn1", dx1, saved1, norm_ffn1, sc1, gt1, W["ffn1_w_gate"], W["ffn1_w_up"], W["ffn1_w_down"], S,
        rider=_chip_exchange_rider(sums2), dh_rider=ffn1_exchange)
    own1 = own1[0]
    grad_x = dx0.reshape(B, S, D)

    dmod = jnp.concatenate([dsh1, dsc1, dgt1, dsh2, dsc2, dgt2, dsh3, dsc3, dgt3], axis=1).reshape(B, N_MOD * D)
    fbg = jnp.sum(dfb, axis=0)
    small = jnp.concatenate([jnp.sum(dg1, axis=0), jnp.sum(dgmix, axis=0), jnp.sum(dg3, axis=0), jnp.sum(dg_final, axis=0),
                             fbg, jnp.sum(loss_b, axis=0)], axis=1)
    n_small = small.shape[1]
    tail = _small_allgather(jnp.concatenate([dmod, jnp.pad(small, ((0, 0), (0, N_MOD * D - n_small)))], axis=0), "gather_tail")
    dmod_all = tail[:, :B].reshape(N_DEV * B, N_MOD * D)
    dmod_cols = lax.dynamic_slice(dmod_all, (0, me * cl), (N_DEV * B, cl))
    g_ada_w, g_ada_b = _ada_backward(c_act, dmod_cols, dmod_all)

    small = _sum_devices(tail[:, B:, :n_small])
    g_small = {"norm_ffn1": small[:, 0:D], "norm_mix": small[:, D:2 * D], "norm_ffn2": small[:, 2 * D:3 * D],
               "norm_final": small[:, 3 * D:4 * D], "forget_bias": small[:, 4 * D:4 * D + FOX_HEADS], "ada_b": g_ada_b}
    loss = small[0, 4 * D + LANES]

    grads = {"ada_w": g_ada_w[None]}
    for names, own, recv in ((GROUPS[0], own1, recv1), (GROUPS[1], own2, recv2), (GROUPS[2], own3, recv3)):
        grads.update(unpack_grads(names, _final_grad_sum(own, recv)))

    delta, new_m, new_v = {}, {}, {}
    for n in ["ada_w"] + BIG:
        shp = args[n].shape
        d_, m_, v_ = _adamw(f"adamw_{n}", args[n][0], grads[n][0], args["m_" + n][0], args["v_" + n][0])
        delta[n], new_m[n], new_v[n] = d_.reshape(shp), m_.reshape(shp), v_.reshape(shp)
    sizes = [args[n].size for n in SMALL]
    tot = sum(sizes)
    padded = -(-tot // (8 * LANES)) * (8 * LANES)

    def flat(get):
        v = jnp.concatenate([get(n).reshape(-1) for n in SMALL])
        return jnp.pad(v, (0, padded - tot)).reshape(8, padded // 8)

    d_s, m_s, v_s = _adamw("adamw_small", flat(lambda n: args[n]), flat(lambda n: g_small[n]), flat(lambda n: args["m_" + n]),
                           flat(lambda n: args["v_" + n]))
    o = 0
    for n, sz in zip(SMALL, sizes):
        shp = args[n].shape
        grads[n] = g_small[n].reshape(shp)
        delta[n] = d_s.reshape(-1)[o:o + sz].reshape(shp)
        new_m[n] = m_s.reshape(-1)[o:o + sz].reshape(shp)
        new_v[n] = v_s.reshape(-1)[o:o + sz].reshape(shp)
        o += sz

    order = ["ada_w", "ada_b", "norm_ffn1", "ffn1_w_gate", "ffn1_w_up", "ffn1_w_down", "norm_mix", "w_in", "forget_bias",
             "w_branch_a", "w_branch_b", "w_out", "norm_ffn2", "ffn2_w_gate", "ffn2_w_up", "ffn2_w_down", "norm_final"]
    return (loss, grad_x, *[grads[n] for n in order], *[delta[n] for n in order], *[new_m[n] for n in order],
            *[new_v[n] for n in order])
```

```python
import functools
import math

import jax
import jax.numpy as jnp
from jax import lax
from jax.experimental import pallas as pl
from jax.experimental.pallas import tpu as pltpu

F32 = jnp.float32
BF16 = jnp.bfloat16
MESH = pl.DeviceIdType.MESH
ANY = pl.BlockSpec(memory_space=pl.ANY)
VMEM_SPEC = pl.BlockSpec(memory_space=pltpu.VMEM)

N_DEV = 8
HEAD_DIM = 64
BLOCK = 128
DIL_GROUPS = ((128, 1), (512, 4), (2048, 16))
N_DIL = len(DIL_GROUPS)
DIL_HPG = 4
DIL_GW = DIL_HPG * HEAD_DIM
DIL_W = N_DIL * DIL_GW
FOX_HEADS = 8
FOX_W = FOX_HEADS * HEAD_DIM
N_MOD = 9
RMS_EPS = 1e-6
ALIBI_MAX_BIAS = 8.0
NEG_INF = -1e30
ADAM_LR, ADAM_B1, ADAM_B2, ADAM_EPS, ADAM_WD, ADAM_STEP = 0.001, 0.9, 0.999, 1e-08, 0.01, 10
V7X_VMEM_LIMIT = 52 * 1024 * 1024
LANES = 128
ROW_ALIGN = 16
PACK_ROW_QUANTUM = 32
FOX_STRIP = 32
SCALE = 1.0 / math.sqrt(HEAD_DIM)


def _div(dim, target, quantum):
    best = None
    for t in range(quantum, min(dim, target) + 1, quantum):
        if dim % t == 0:
            best = t
    return best or dim


def _params(sem=None):
    return pltpu.CompilerParams(dimension_semantics=sem, vmem_limit_bytes=V7X_VMEM_LIMIT)


def _sigmoid(x):
    return 0.5 * jnp.tanh(0.5 * x) + 0.5


def _sigmoid_exp(x):
    return 1.0 / (1.0 + jnp.exp(-x))


def _position():
    x, y, c = lax.axis_index("x"), lax.axis_index("y"), lax.axis_index("c")
    return x, y, c


def _small_allgather(v, name):
    rows, cols = v.shape

    def body(v_ref, out_ref, send_sems, recv_sems):
        x, y, c = _position()
        me = 4 * x + 2 * y + c
        out_ref[me] = v_ref[...]

        def peer(k):
            return (1 - x if k & 4 else x, 1 - y if k & 2 else y, 1 - c if k & 1 else c)

        def copy(k, slot):
            return pltpu.make_async_remote_copy(
                src_ref=v_ref, dst_ref=out_ref.at[slot], send_sem=send_sems.at[k - 1], recv_sem=recv_sems.at[k - 1],
                device_id=peer(k), device_id_type=MESH)

        sends = [copy(k, me) for k in range(1, N_DEV)]
        for cp in sends:
            cp.start()
        for k in range(1, N_DEV):
            px, py, pc = peer(k)
            copy(k, 4 * px + 2 * py + pc).wait_recv()
        for cp in sends:
            cp.wait_send()

    return pl.pallas_call(
        body, name=name,
        out_shape=jax.ShapeDtypeStruct((N_DEV, rows, cols), v.dtype),
        in_specs=[VMEM_SPEC], out_specs=VMEM_SPEC,
        scratch_shapes=[pltpu.SemaphoreType.DMA((N_DEV - 1,)), pltpu.SemaphoreType.DMA((N_DEV - 1,))],
    )(v)


def _weight_allgather(p):
    rows, cols = p.shape

    def body(p_ref, out_ref, send_sems, recv_sems, local_sem):
        x, y, c = _position()
        me, sibling = (x, y, c), (x, y, 1 - c)
        chips = [(1 - x, y), (x, 1 - y), (1 - x, 1 - y)]

        def slot(px, py, pc):
            return out_ref.at[4 * px + 2 * py + pc]

        def copy(k, block, to, src=None):
            return pltpu.make_async_remote_copy(
                src_ref=slot(*block) if src is None else src, dst_ref=slot(*block),
                send_sem=send_sems.at[k], recv_sem=recv_sems.at[k], device_id=to, device_id_type=MESH)

        mine = pltpu.make_async_copy(p_ref, slot(*me), local_sem)
        mine.start()
        first = [copy(0, me, sibling, src=p_ref)]
        first += [copy(1 + j, me, (*chip, c), src=p_ref) for j, chip in enumerate(chips)]
        for cp in first:
            cp.start()
        passed = [copy(4 + j, (*chip, c), sibling) for j, chip in enumerate(chips)]
        for j, chip in enumerate(chips):
            copy(1 + j, (*chip, c), me).wait_recv()
            passed[j].start()
        copy(0, sibling, me).wait_recv()
        for j, chip in enumerate(chips):
            copy(4 + j, (*chip, 1 - c), me).wait_recv()
        for cp in first + passed:
            cp.wait_send()
        mine.wait()

    return pl.pallas_call(
        body, name="weight_allgather",
        out_shape=jax.ShapeDtypeStruct((N_DEV, rows, cols), p.dtype),
        in_specs=[ANY], out_specs=ANY,
        scratch_shapes=[pltpu.SemaphoreType.DMA((7,)), pltpu.SemaphoreType.DMA((7,)), pltpu.SemaphoreType.DMA],
    )(p)


def _grad_exchange_sibling(g):
    _, rows, cols = g.shape

    def body(g_ref, out_ref, send_sems, recv_sems):
        x, y, c = _position()
        sibling = (x, y, 1 - c)

        def copy(q):
            px, py = q >> 1, q & 1
            return pltpu.make_async_remote_copy(
                src_ref=g_ref.at[4 * px + 2 * py + (1 - c)], dst_ref=out_ref.at[q],
                send_sem=send_sems.at[q], recv_sem=recv_sems.at[q], device_id=sibling, device_id_type=MESH)

        copies = [copy(q) for q in range(4)]
        for cp in copies:
            cp.start()
        for cp in copies:
            cp.wait_recv()
        for cp in copies:
            cp.wait_send()

    return pl.pallas_call(
        body, name="grad_exchange_sibling",
        out_shape=jax.ShapeDtypeStruct((4, rows, cols), g.dtype),
        in_specs=[ANY], out_specs=ANY,
        scratch_shapes=[pltpu.SemaphoreType.DMA((4,)), pltpu.SemaphoreType.DMA((4,))],
    )(g)


class _Rider:
    def __init__(self, operands, out_shapes, n_send, n_recv, start, finish, aliases=None):
        self.operands, self.out_shapes = list(operands), list(out_shapes)
        self.n_send, self.n_recv, self.start, self.finish = n_send, n_recv, start, finish
        self.aliases = aliases or {}


def _pcall(body, *, name, grid, in_specs, operands, out_shape, out_specs, scratch_shapes, params, rider=None):
    if rider is None:
        return pl.pallas_call(body, name=name, out_shape=out_shape, grid=grid, in_specs=in_specs, out_specs=out_specs,
                              scratch_shapes=scratch_shapes, compiler_params=params)(*operands)
    n_in, n_out, n_sc = len(operands), len(out_shape), len(scratch_shapes)
    r_in, r_out = len(rider.operands), len(rider.out_shapes)

    def wrapped(*refs):
        ins, rins = refs[:n_in], refs[n_in:n_in + r_in]
        outs, routs = refs[n_in + r_in:n_in + r_in + n_out], refs[n_in + r_in + n_out:n_in + r_in + n_out + r_out]
        rest = refs[n_in + r_in + n_out + r_out:]
        scratch, sems = rest[:n_sc], rest[n_sc:]
        ids = [pl.program_id(a) for a in range(len(grid))]
        first, last = ids[0] == 0, ids[0] == grid[0] - 1
        for a in range(1, len(grid)):
            first, last = first & (ids[a] == 0), last & (ids[a] == grid[a] - 1)

        @pl.when(first)
        def _():
            rider.start(rins, routs, *sems)

        body(*ins, *outs, *scratch)

        @pl.when(last)
        def _():
            rider.finish(rins, routs, *sems)

    return pl.pallas_call(
        wrapped, name=name, out_shape=list(out_shape) + rider.out_shapes, grid=grid,
        in_specs=list(in_specs) + [ANY] * r_in, out_specs=list(out_specs) + [ANY] * r_out,
        scratch_shapes=list(scratch_shapes) + [pltpu.SemaphoreType.DMA((rider.n_send,)), pltpu.SemaphoreType.DMA((rider.n_recv,))],
        input_output_aliases={n_in + i: n_out + o for i, o in rider.aliases.items()},
        compiler_params=params,
    )(*operands, *rider.operands)


def _flips(x, y, c):
    return [(x, y, 1 - c), (1 - x, y, c), (x, 1 - y, c), (1 - x, 1 - y, c)]


def _gather_direct_rider(p):
    rows, cols = p.shape

    def copies(p_ref, land, send_sems, recv_sems):
        x, y, c = _position()
        me = 4 * x + 2 * y + c
        peers = _flips(x, y, c)
        sends = [pltpu.make_async_remote_copy(src_ref=p_ref, dst_ref=land.at[me], send_sem=send_sems.at[k], recv_sem=recv_sems.at[k],
                                              device_id=to, device_id_type=MESH) for k, to in enumerate(peers)]
        recvs = [pltpu.make_async_remote_copy(src_ref=p_ref, dst_ref=land.at[4 * px + 2 * py + pc], send_sem=send_sems.at[k],
                                              recv_sem=recv_sems.at[k], device_id=(px, py, pc), device_id_type=MESH)
                 for k, (px, py, pc) in enumerate(peers)]
        mine = pltpu.make_async_copy(p_ref, land.at[me], send_sems.at[len(peers)])
        return sends, recvs, mine

    def start(rins, routs, send_sems, recv_sems):
        sends, _, mine = copies(rins[0], routs[0], send_sems, recv_sems)
        mine.start()
        for cp in sends:
            cp.start()

    def finish(rins, routs, send_sems, recv_sems):
        sends, recvs, mine = copies(rins[0], routs[0], send_sems, recv_sems)
        for cp in recvs:
            cp.wait_recv()
        for cp in sends:
            cp.wait_send()
        mine.wait()

    return _Rider([p], [jax.ShapeDtypeStruct((N_DEV, rows, cols), p.dtype)], 5, 4, start, finish)


def _gather_forward_rider(land):
    def copies(buf, send_sems, recv_sems):
        x, y, c = _position()
        chips = [(1 - x, y), (x, 1 - y), (1 - x, 1 - y)]
        sends = [pltpu.make_async_remote_copy(src_ref=buf.at[4 * px + 2 * py + c], dst_ref=buf.at[4 * px + 2 * py + c],
                                              send_sem=send_sems.at[k], recv_sem=recv_sems.at[k], device_id=(x, y, 1 - c),
                                              device_id_type=MESH) for k, (px, py) in enumerate(chips)]
        recvs = [pltpu.make_async_remote_copy(src_ref=buf.at[4 * px + 2 * py + 1 - c], dst_ref=buf.at[4 * px + 2 * py + 1 - c],
                                              send_sem=send_sems.at[k], recv_sem=recv_sems.at[k], device_id=(x, y, 1 - c),
                                              device_id_type=MESH) for k, (px, py) in enumerate(chips)]
        return sends, recvs

    def start(rins, routs, send_sems, recv_sems):
        for cp in copies(routs[0], send_sems, recv_sems)[0]:
            cp.start()

    def finish(rins, routs, send_sems, recv_sems):
        sends, recvs = copies(routs[0], send_sems, recv_sems)
        for cp in recvs:
            cp.wait_recv()
        for cp in sends:
            cp.wait_send()

    return _Rider([land], [jax.ShapeDtypeStruct(land.shape, land.dtype)], 3, 3, start, finish, aliases={0: 0})


def _chip_exchange_rider(s):
    def copies(s_ref, out_ref, send_sems, recv_sems):
        x, y, c = _position()
        chips = [(1 - x, y), (x, 1 - y), (1 - x, 1 - y)]
        return [pltpu.make_async_remote_copy(src_ref=s_ref.at[k], dst_ref=out_ref.at[k], send_sem=send_sems.at[k],
                                             recv_sem=recv_sems.at[k], device_id=(*chips[k], c), device_id_type=MESH)
                for k in range(3)]

    def start(rins, routs, send_sems, recv_sems):
        for cp in copies(rins[0], routs[0], send_sems, recv_sems):
            cp.start()

    def finish(rins, routs, send_sems, recv_sems):
        cps = copies(rins[0], routs[0], send_sems, recv_sems)
        for cp in cps:
            cp.wait_recv()
        for cp in cps:
            cp.wait_send()

    return _Rider([s], [jax.ShapeDtypeStruct(s.shape, s.dtype)], 3, 3, start, finish)


def _chip_partial_sums(g, recv_sib, jj, qq):
    _, rows, cols = g.shape
    tr = _div(rows, 512, ROW_ALIGN)

    def body(jj_ref, qq_ref, g_ref, r_ref, o_ref):
        o_ref[...] = (g_ref[...] + r_ref[...]).astype(o_ref.dtype)

    return pl.pallas_call(
        body, name="chip_partial_sums",
        out_shape=jax.ShapeDtypeStruct((3, rows, cols), BF16),
        grid_spec=pltpu.PrefetchScalarGridSpec(
            num_scalar_prefetch=2, grid=(3, rows // tr),
            in_specs=[pl.BlockSpec((None, tr, cols), lambda k, i, jj, qq: (jj[k], i, 0)),
                      pl.BlockSpec((None, tr, cols), lambda k, i, jj, qq: (qq[k], i, 0))],
            out_specs=pl.BlockSpec((None, tr, cols), lambda k, i, jj, qq: (k, i, 0))),
        compiler_params=_params(("arbitrary", "arbitrary")),
    )(jj, qq, g, recv_sib)


def _own_partial_sum(g, recv_sib, jj, qq):
    _, rows, cols = g.shape
    tr = _div(rows, 512, ROW_ALIGN)

    def body(jj_ref, qq_ref, g_ref, r_ref, o_ref):
        o_ref[...] = g_ref[...] + r_ref[...]

    return pl.pallas_call(
        body, name="own_partial_sum",
        out_shape=jax.ShapeDtypeStruct((rows, cols), F32),
        grid_spec=pltpu.PrefetchScalarGridSpec(
            num_scalar_prefetch=2, grid=(rows // tr,),
            in_specs=[pl.BlockSpec((None, tr, cols), lambda i, jj, qq: (jj[0], i, 0)),
                      pl.BlockSpec((None, tr, cols), lambda i, jj, qq: (qq[0], i, 0))],
            out_specs=pl.BlockSpec((tr, cols), lambda i, jj, qq: (i, 0))),
        compiler_params=_params(("arbitrary",)),
    )(jj, qq, g, recv_sib)


def _final_grad_sum(own, recv):
    rows, cols = own.shape
    tr = _div(rows, 512, ROW_ALIGN)

    def body(o_ref, r_ref, out_ref):
        out_ref[...] = ((o_ref[...] + r_ref[0].astype(F32)) + r_ref[1].astype(F32)) + r_ref[2].astype(F32)

    return pl.pallas_call(
        body, name="final_grad_sum",
        out_shape=jax.ShapeDtypeStruct((rows, cols), F32),
        grid=(rows // tr,),
        in_specs=[pl.BlockSpec((tr, cols), lambda i: (i, 0)), pl.BlockSpec((3, tr, cols), lambda i: (0, i, 0))],
        out_specs=pl.BlockSpec((tr, cols), lambda i: (i, 0)),
        compiler_params=_params(("arbitrary",)),
    )(own, recv)


def _sibling_exchange_rider(g):
    _, rows, cols = g.shape

    def copies(g_ref, out_ref, send_sems, recv_sems):
        x, y, c = _position()
        return [pltpu.make_async_remote_copy(
            src_ref=g_ref.at[4 * (q >> 1) + 2 * (q & 1) + (1 - c)], dst_ref=out_ref.at[q], send_sem=send_sems.at[q],
            recv_sem=recv_sems.at[q], device_id=(x, y, 1 - c), device_id_type=MESH) for q in range(4)]

    def start(rins, routs, send_sems, recv_sems):
        for cp in copies(rins[0], routs[0], send_sems, recv_sems):
            cp.start()

    def finish(rins, routs, send_sems, recv_sems):
        cps = copies(rins[0], routs[0], send_sems, recv_sems)
        for cp in cps:
            cp.wait_recv()
        for cp in cps:
            cp.wait_send()

    return _Rider([g], [jax.ShapeDtypeStruct((4, rows, cols), g.dtype)], 4, 4, start, finish)


def _chip_sums(g, recv_sib=None):
    x, y, c = _position()
    chips = [(1 - x, y), (x, 1 - y), (1 - x, 1 - y)]
    jj = jnp.stack([4 * px + 2 * py + c for px, py in chips]).astype(jnp.int32)
    qq = jnp.stack([2 * px + py for px, py in chips]).astype(jnp.int32)
    jme = jnp.reshape(4 * x + 2 * y + c, (1,)).astype(jnp.int32)
    qme = jnp.reshape(2 * x + y, (1,)).astype(jnp.int32)
    if recv_sib is None:
        recv_sib = _grad_exchange_sibling(g)
    return _chip_partial_sums(g, recv_sib, jj, qq), _own_partial_sum(g, recv_sib, jme, qme)


def _matmul(name, form, prods, M, N, K, tm, tn, tk, out_dtypes, extras=(), epilogue=None, rows_per_example=None, rider=None):
    nk = K // tk
    n_acc = len(prods)
    flat = [ab for group in prods for ab in group]
    dims = {"nn": (((1,), (0,)), ((), ())), "nt": (((1,), (1,)), ((), ())), "tn": (((0,), (0,)), ((), ()))}[form]
    direct = nk > 1 and epilogue is None and n_acc == 1 and list(out_dtypes) == [F32]

    def spec(shape, index_map, whole):
        if whole:
            return pl.BlockSpec(shape, index_map, pipeline_mode=pl.Buffered(1))
        return pl.BlockSpec(shape, index_map)

    if form == "tn":
        a_spec = spec((tk, tm), lambda i, j, k: (k, i), nk == 1 and M == tm)
    else:
        a_spec = spec((tm, tk), lambda i, j, k: (i, k), nk == 1 and M == tm)
    if form == "nt":
        b_spec = spec((tn, tk), lambda i, j, k: (j, k), nk == 1 and N == tn)
    else:
        b_spec = spec((tk, tn), lambda i, j, k: (k, j), nk == 1 and N == tn)
    in_specs, operands = [], []
    for a, b in flat:
        in_specs += [a_spec, _weight_spec(b) if isinstance(b, _Slab) else b_spec]
        operands += [a, _weight_operand(b)]
    for arr, kind, off in extras:
        if kind == "tile":
            assert off % tn == 0
            in_specs.append(pl.BlockSpec((tm, tn), functools.partial(lambda i, j, k, o: (i, j + o), o=off // tn)))
        else:
            tiles = rows_per_example // tm
            in_specs.append(pl.BlockSpec((None, 1, tn), functools.partial(lambda i, j, k, t: (i // t, 0, j), t=tiles)))
        operands.append(arr)
    n_in, n_out = len(operands), len(out_dtypes)

    def body(*refs):
        in_refs, out_refs, acc_refs = refs[:n_in], refs[n_in:n_in + n_out], refs[n_in + n_out:]
        k = pl.program_id(2)
        partials, p = [], 0
        for group in prods:
            tot = None
            for _ in group:
                d = lax.dot_general(in_refs[2 * p][...], _weight_value(in_refs[2 * p + 1], flat[p][1]), dims,
                                    preferred_element_type=F32)
                tot = d if tot is None else tot + d
                p += 1
            partials.append(tot)

        def finish(accs):
            ex = [r[...] for r in in_refs[2 * len(flat):]]
            outs = epilogue(accs, ex) if epilogue is not None else accs
            for r, o in zip(out_refs, outs):
                r[...] = o.astype(r.dtype)

        if nk == 1:
            finish(partials)
        elif direct:
            @pl.when(k == 0)
            def _():
                out_refs[0][...] = partials[0]

            @pl.when(k > 0)
            def _():
                out_refs[0][...] += partials[0]
        else:
            @pl.when(k == 0)
            def _():
                for r, v in zip(acc_refs, partials):
                    r[...] = v

            @pl.when(k > 0)
            def _():
                for r, v in zip(acc_refs, partials):
                    r[...] += v

            @pl.when(k == nk - 1)
            def _():
                finish([r[...] for r in acc_refs])

    return _pcall(
        body, name=name,
        out_shape=[jax.ShapeDtypeStruct((M, N), dt) for dt in out_dtypes],
        grid=(M // tm, N // tn, nk),
        in_specs=in_specs, operands=operands,
        out_specs=[pl.BlockSpec((tm, tn), lambda i, j, k: (i, j)) for _ in out_dtypes],
        scratch_shapes=[pltpu.VMEM((tm, tn), F32) for _ in range(n_acc)] if nk > 1 and not direct else [],
        params=_params(("parallel", "parallel", "arbitrary")), rider=rider)


def _rowwise(name, fn, T, tm, ins, outs, rows_per_example):
    tiles = rows_per_example // tm
    n_ex = T // rows_per_example
    in_specs, operands = [], []
    for arr, kind, arg in ins:
        if kind == "row":
            if arg is None:
                in_specs.append(pl.BlockSpec((tm, arr.shape[1]), lambda i: (i, 0)))
            else:
                in_specs.append(pl.BlockSpec((tm, arg[0]), functools.partial(lambda i, cb: (i, cb), cb=arg[1])))
        elif kind == "bvec":
            in_specs.append(pl.BlockSpec((None, 1, arr.shape[2]), lambda i: (i // tiles, 0, 0)))
        else:
            in_specs.append(pl.BlockSpec((1, arr.shape[1]), lambda i: (0, 0)))
        operands.append(arr)
    out_shape, out_specs = [], []
    for kind, cols, dt in outs:
        if kind == "row":
            out_shape.append(jax.ShapeDtypeStruct((T, cols), dt))
            out_specs.append(pl.BlockSpec((tm, cols), lambda i: (i, 0)))
        else:
            out_shape.append(jax.ShapeDtypeStruct((n_ex, 1, cols), F32))
            out_specs.append(pl.BlockSpec((None, 1, cols), lambda i: (i // tiles, 0, 0)))
    n_in = len(operands)

    def body(*refs):
        i = pl.program_id(0)
        vals = fn(*[r[...] for r in refs[:n_in]])
        for (kind, _, _), r, v in zip(outs, refs[n_in:], vals):
            if kind == "row":
                r[...] = v.astype(r.dtype)
            else:
                @pl.when(i % tiles == 0)
                def _():
                    r[...] = jnp.zeros_like(r)

                r[...] += v

    return pl.pallas_call(
        body, name=name, out_shape=out_shape, grid=(T // tm,), in_specs=in_specs, out_specs=out_specs,
        compiler_params=_params(("arbitrary",)),
    )(*operands)


def _colsum(v):
    return jnp.sum(v, axis=0, keepdims=True)


def _rms_parts(x):
    rstd = lax.rsqrt(jnp.mean(x * x, axis=-1, keepdims=True) + RMS_EPS)
    return x * rstd, rstd


def _resident(shape):
    return pl.BlockSpec(shape, lambda i: (0, 0), pipeline_mode=pl.Buffered(1))


class _Slab:
    def __init__(self, land, off, rows):
        assert off % rows == 0 and rows % ROW_ALIGN == 0
        self.land, self.off, self.rows = land, off, rows
        self.shape = (N_DEV * rows, land.shape[2])


def _weight_spec(w):
    if isinstance(w, _Slab):
        return pl.BlockSpec((N_DEV, w.rows, w.shape[1]), lambda *_: (0, w.off // w.rows, 0), pipeline_mode=pl.Buffered(1))
    return pl.BlockSpec(w.shape, lambda *_: (0, 0), pipeline_mode=pl.Buffered(1))


def _weight_operand(w):
    return w.land if isinstance(w, _Slab) else w


def _weight_value(ref, w):
    return ref[...].reshape(w.shape) if isinstance(w, _Slab) else ref[...]


def _example_acc(r, i, tiles, v):
    @pl.when(i % tiles == 0)
    def _():
        r[...] = jnp.zeros_like(r)

    r[...] += v


def _norm_matmul(name, x, g, sc, sh, weights, epilogue, outs, S, vecs=(), rider=None):
    T, D = x.shape
    tm = _div(S, 256, 8)
    tiles = S // tm
    nw, nv = len(weights), len(vecs)

    def body(*refs):
        x_ref, g_ref, sc_ref, sh_ref = refs[:4]
        w_refs, v_refs = refs[4:4 + nw], refs[4 + nw:4 + nw + nv]
        h_ref, out_refs = refs[4 + nw + nv], refs[5 + nw + nv:]
        xhat, _ = _rms_parts(x_ref[...])
        h = ((xhat * g_ref[...]) * (1.0 + sc_ref[...]) + sh_ref[...]).astype(BF16)
        h_ref[...] = h
        accs = [lax.dot_general(h, _weight_value(r, w), NT, preferred_element_type=F32) for r, w in zip(w_refs, weights)]
        for r, o in zip(out_refs, epilogue(accs, *[v[...] for v in v_refs])):
            r[...] = o.astype(r.dtype)

    bvec = pl.BlockSpec((None, 1, D), lambda i: (i // tiles, 0, 0))
    return _pcall(
        body, name=name,
        out_shape=[jax.ShapeDtypeStruct((T, D), BF16)] + [jax.ShapeDtypeStruct((T, w), dt) for w, dt in outs],
        grid=(T // tm,),
        in_specs=[pl.BlockSpec((tm, D), lambda i: (i, 0)), pl.BlockSpec((1, D), lambda i: (0, 0)), bvec, bvec]
        + [_weight_spec(w) for w in weights] + [pl.BlockSpec(v.shape, lambda i: (0, 0)) for v in vecs],
        operands=[x, g, sc, sh, *[_weight_operand(w) for w in weights], *vecs],
        out_specs=[pl.BlockSpec((tm, D), lambda i: (i, 0))] + [pl.BlockSpec((tm, w), lambda i: (i, 0)) for w, _ in outs],
        scratch_shapes=[], params=_params(("arbitrary",)), rider=rider)


def _gated_grad_matmul(name, dx, y, gt, coeff, w, tiles_in, epilogue, outs, S, rider=None):
    T, D = dx.shape
    N = w.shape[0]
    tm = _div(S, 256, 8)
    tiles = S // tm
    nt = len(tiles_in)

    def body(*refs):
        dx_ref, y_ref, gt_ref, w_ref = refs[:4]
        t_refs, dy_ref, dgt_ref, out_refs = refs[4:4 + nt], refs[4 + nt], refs[5 + nt], refs[6 + nt:]
        i = pl.program_id(0)
        dxv = dx_ref[...]
        dy = (coeff * gt_ref[...] * dxv).astype(BF16)
        dy_ref[...] = dy
        _example_acc(dgt_ref, i, tiles, _colsum(coeff * dxv * y_ref[...].astype(F32)))
        acc = lax.dot_general(dy, _weight_value(w_ref, w), NT, preferred_element_type=F32)
        for r, o in zip(out_refs, epilogue(acc, [t[...] for t in t_refs])):
            r[...] = o.astype(r.dtype)

    row = pl.BlockSpec((tm, D), lambda i: (i, 0))
    bvec = pl.BlockSpec((None, 1, D), lambda i: (i // tiles, 0, 0))
    return _pcall(
        body, name=name,
        out_shape=[jax.ShapeDtypeStruct((T, D), BF16), jax.ShapeDtypeStruct((T // S, 1, D), F32)]
        + [jax.ShapeDtypeStruct((T, N), dt) for dt in outs],
        grid=(T // tm,),
        in_specs=[row, row, bvec, _weight_spec(w)]
        + [pl.BlockSpec((tm, N), functools.partial(lambda i, cb: (i, cb), cb=cb)) for _, cb in tiles_in],
        operands=[dx, y, gt, _weight_operand(w), *[t for t, _ in tiles_in]],
        out_specs=[row, bvec] + [pl.BlockSpec((tm, N), lambda i: (i, 0)) for _ in outs],
        scratch_shapes=[], params=_params(("arbitrary",)), rider=rider)


def _matmul_normmod_bwd(name, prods, x, g, sc, dres, S, rider=None):
    T, D = x.shape
    tm = _div(S, 256, 8)
    tiles = S // tm
    npr = len(prods)

    def body(*refs):
        ab = refs[:2 * npr]
        x_ref, g_ref, sc_ref, dr_ref = refs[2 * npr:2 * npr + 4]
        dx_ref, dsh_ref, dsc_ref, dg_ref = refs[2 * npr + 4:]
        i = pl.program_id(0)
        dh = None
        for p in range(npr):
            d = lax.dot_general(ab[2 * p][...], _weight_value(ab[2 * p + 1], prods[p][1]), NN, preferred_element_type=F32)
            dh = d if dh is None else dh + d
        xhat, rstd = _rms_parts(x_ref[...])
        gv = g_ref[...]
        dn = dh * (1.0 + sc_ref[...])
        dxh = dn * gv
        dx_ref[...] = dr_ref[...] + rstd * (dxh - xhat * jnp.mean(dxh * xhat, axis=-1, keepdims=True))
        _example_acc(dsh_ref, i, tiles, _colsum(dh))
        _example_acc(dsc_ref, i, tiles, _colsum(dh * (xhat * gv)))
        _example_acc(dg_ref, i, tiles, _colsum(dn * xhat))

    row = pl.BlockSpec((tm, D), lambda i: (i, 0))
    bvec = pl.BlockSpec((None, 1, D), lambda i: (i // tiles, 0, 0))
    in_specs, operands = [], []
    for a, b in prods:
        in_specs += [pl.BlockSpec((tm, a.shape[1]), lambda i: (i, 0)), _weight_spec(b)]
        operands += [a, _weight_operand(b)]
    acc_shape = jax.ShapeDtypeStruct((T // S, 1, D), F32)
    return _pcall(
        body, name=name,
        out_shape=[jax.ShapeDtypeStruct((T, D), F32), acc_shape, acc_shape, acc_shape],
        grid=(T // tm,),
        in_specs=in_specs + [row, pl.BlockSpec((1, D), lambda i: (0, 0)), bvec, row],
        operands=[*operands, x, g, sc, dres],
        out_specs=[row, bvec, bvec, bvec],
        scratch_shapes=[], params=_params(("arbitrary",)), rider=rider)


def _down_loss(name, s, wd, x, gt, tgt, g_final, S):
    T, D = x.shape
    tm = _div(S, 512, 8)
    tiles = S // tm

    def body(s_ref, w_ref, x_ref, gt_ref, t_ref, g_ref, y_ref, dx_ref, loss_ref, dg_ref):
        i = pl.program_id(0)
        acc = lax.dot_general(s_ref[...], _weight_value(w_ref, wd), NN, preferred_element_type=F32)
        y_ref[...] = acc.astype(y_ref.dtype)
        xhat, rstd = _rms_parts(x_ref[...] + 0.5 * gt_ref[...] * acc)
        gv = g_ref[...]
        e = xhat * gv - t_ref[...]
        _example_acc(loss_ref, i, tiles, jnp.broadcast_to(0.5 / D * jnp.sum(_colsum(e * e), axis=1, keepdims=True), (1, LANES)))
        dy = e * (1.0 / D)
        dxh = dy * gv
        dx_ref[...] = rstd * (dxh - xhat * jnp.mean(dxh * xhat, axis=-1, keepdims=True))
        _example_acc(dg_ref, i, tiles, _colsum(dy * xhat))

    row = pl.BlockSpec((tm, D), lambda i: (i, 0))
    return pl.pallas_call(
        body, name=name,
        out_shape=[jax.ShapeDtypeStruct((T, D), BF16), jax.ShapeDtypeStruct((T, D), F32),
                   jax.ShapeDtypeStruct((T // S, 1, LANES), F32), jax.ShapeDtypeStruct((T // S, 1, D), F32)],
        grid=(T // tm,),
        in_specs=[pl.BlockSpec((tm, s.shape[1]), lambda i: (i, 0)), _weight_spec(wd), row,
                  pl.BlockSpec((None, 1, D), lambda i: (i // tiles, 0, 0)), row, pl.BlockSpec((1, D), lambda i: (0, 0))],
        out_specs=[row, row, pl.BlockSpec((None, 1, LANES), lambda i: (i // tiles, 0, 0)),
                   pl.BlockSpec((None, 1, D), lambda i: (i // tiles, 0, 0))],
        compiler_params=_params(("arbitrary",)),
    )(s, _weight_operand(wd), x, gt, tgt, g_final)


def _ffn_forward(tag, x, g, sh, sc, gt, wgT, wuT, wd, S, gather=None, loss=None):
    T, D = x.shape
    F = wd.shape[0]

    def gateup(accs):
        a, u = accs
        return [a, u, a * _sigmoid(a) * u]

    h, a, u, s, *land = _norm_matmul(f"{tag}_gateup", x, g, sc, sh, [wgT, wuT], gateup, [(F, BF16)] * 3, S,
                                     rider=None if gather is None else _gather_direct_rider(gather))

    def down(accs, ex):
        xv, gtv = ex
        return [xv + 0.5 * gtv * accs[0], accs[0]]

    tmd = _div(S, 512, 8)
    if loss is not None:
        y, dx_new, loss_b, dg_final = _down_loss(f"{tag}_down_loss", s, wd, x, gt, loss[0], loss[1], S)
        return (dx_new, loss_b, dg_final), (x, h, a, u, s, y), None
    x_new, y, *land = _matmul(f"{tag}_down", "nn", [[(s, wd)]], T, D, F, tmd, D, F, [F32, BF16],
                              extras=[(x, "tile", 0), (gt, "brow", 0)], epilogue=down, rows_per_example=S,
                              rider=None if gather is None else _gather_forward_rider(land[0]))
    return x_new, (x, h, a, u, s, y), (land[0] if land else None)


def _ffn_backward(tag, dx_out, saved, g, sc, gt, wgT, wuT, wd, S, rider=None, dh_rider=None):
    x, h, a, u, s, y = saved
    T, D = x.shape
    F = wd.shape[0]

    def act_grad(ds, ex):
        av, uv = ex[0].astype(F32), ex[1].astype(F32)
        sg = _sigmoid(av)
        return [ds * uv * (sg * (1.0 + av * (1.0 - sg))), ds * (av * sg)]

    dy, dgt, da, du, *rode = _gated_grad_matmul(f"{tag}_act_grad", dx_out, y, gt, 0.5, wd, [(a, 0), (u, 0)], act_grad,
                                                [BF16, BF16], S, rider=rider)
    tkw = _div(T, 2048, LANES)
    tmw = _div(F, 1408, LANES)
    dwd = _matmul(f"{tag}_dw_down", "tn", [[(s, dy)]], F, D, T, tmw, D, tkw, [F32])[0]
    dwgT = _matmul(f"{tag}_dw_gate", "tn", [[(da, h)]], F, D, T, tmw, D, tkw, [F32])[0]
    dwuT = _matmul(f"{tag}_dw_up", "tn", [[(du, h)]], F, D, T, tmw, D, tkw, [F32])[0]
    dx_in, dsh, dsc, dg, *rode_dh = _matmul_normmod_bwd(
        f"{tag}_dh", [(da, wgT), (du, wuT)], x, g, sc, dx_out, S,
        rider=None if dh_rider is None else dh_rider(dwgT, dwuT, dwd))
    return dx_in, (dsh, dsc, dgt, dg), (dwgT, dwuT, dwd), rode + rode_dh


def _cumsum(v):
    B, S, _ = v.shape
    rows = _div(S, 1024, BLOCK)

    def body(x_ref, o_ref, carry):
        i = pl.program_id(1)

        @pl.when(i == 0)
        def _():
            carry[...] = jnp.zeros_like(carry)

        r = lax.broadcasted_iota(jnp.int32, (BLOCK, BLOCK), 0)
        c = lax.broadcasted_iota(jnp.int32, (BLOCK, BLOCK), 1)
        tri = (c <= r).astype(F32)
        last = carry[0:1, :]
        for j in range(0, rows, BLOCK):
            cum = jnp.dot(tri, x_ref[j:j + BLOCK, :], precision=lax.Precision.HIGHEST, preferred_element_type=F32) + last
            o_ref[j:j + BLOCK, :] = cum
            last = cum[BLOCK - 1:BLOCK, :]
        carry[...] = jnp.broadcast_to(last, carry.shape)

    return pl.pallas_call(
        body, name="cumsum", out_shape=jax.ShapeDtypeStruct(v.shape, F32), grid=(B, S // rows),
        in_specs=[pl.BlockSpec((None, rows, LANES), lambda b, i: (b, i, 0))],
        out_specs=pl.BlockSpec((None, rows, LANES), lambda b, i: (b, i, 0)),
        scratch_shapes=[pltpu.VMEM((8, LANES), F32)],
        compiler_params=_params(("arbitrary", "arbitrary")),
    )(v)


def _with_ones(x):
    lane = lax.broadcasted_iota(jnp.int32, (x.shape[0], HEAD_DIM), 1)
    return jnp.concatenate([x, jnp.where(lane == 0, 1.0, 0.0).astype(x.dtype)], axis=1)


def _causal_strip(s, r):
    qpos = r + lax.broadcasted_iota(jnp.int32, s.shape, 0)
    kpos = lax.broadcasted_iota(jnp.int32, s.shape, 1)
    return jnp.where(kpos <= qpos, s, NEG_INF)


NT = (((1,), (1,)), ((), ()))
NN = (((1,), (0,)), ((), ()))
TN = (((0,), (0,)), ((), ()))


def _fox_fwd(pm3, cumT, qcol, kcol, vcol, tq, rider=None):
    B, S, _ = pm3.shape
    nq = S // tq
    strips = range(0, tq, FOX_STRIP)

    def body(q_ref, k_ref, v_ref, ck_ref, o_ref, o32_ref, lse_ref, s_sc, p_sc, al_sc, m_sc, acc_sc):
        qi, kj = pl.program_id(1), pl.program_id(2)

        @pl.when(kj == 0)
        def _():
            m_sc[...] = jnp.full_like(m_sc, NEG_INF)
            acc_sc[...] = jnp.zeros_like(acc_sc)

        def tile(diagonal):
            def scores(h):
                hs = slice(HEAD_DIM * h, HEAD_DIM * (h + 1))
                s_sc[h % 2] = lax.dot_general(q_ref[:, hs] * SCALE, k_ref[:, hs], NT, preferred_element_type=F32)

            def accumulate(h):
                hs = slice(HEAD_DIM * h, HEAD_DIM * (h + 1))
                acc_sc[h] = al_sc[h % 2] * acc_sc[h] + lax.dot_general(p_sc[h % 2], _with_ones(v_ref[:, hs]), NN,
                                                                       preferred_element_type=F32)

            scores(0)
            for h in range(FOX_HEADS):
                b = h % 2
                if h + 1 < FOX_HEADS:
                    scores(h + 1)
                if h >= 1:
                    accumulate(h - 1)
                ck = ck_ref[h:h + 1, :]
                for r in strips:
                    rows = slice(r, r + FOX_STRIP)
                    s = s_sc[b, rows, :] - ck
                    if diagonal:
                        s = _causal_strip(s, r)
                    m_prev = m_sc[h, rows, :]
                    m_new = jnp.maximum(m_prev, jnp.max(s, axis=-1, keepdims=True))
                    p_sc[b, rows, :] = jnp.exp(s - m_new).astype(BF16)
                    al_sc[b, rows, :] = jnp.exp(m_prev - m_new)
                    m_sc[h, rows, :] = m_new
            accumulate(FOX_HEADS - 1)

        @pl.when(kj < qi)
        def _():
            tile(False)

        @pl.when(kj == qi)
        def _():
            tile(True)

        @pl.when(kj == nq - 1)
        def _():
            lse_ref[...] = jnp.zeros_like(lse_ref)
            for h in range(FOX_HEADS):
                hs = slice(HEAD_DIM * h, HEAD_DIM * (h + 1))
                acc = acc_sc[h]
                l = acc[:, HEAD_DIM:HEAD_DIM + 1]
                oh = acc[:, :HEAD_DIM] / l
                o_ref[:, hs] = oh.astype(o_ref.dtype)
                o32_ref[:, hs] = oh
                lse_ref[:, h:h + 1] = m_sc[h] + jnp.log(l)

    ospec = pl.BlockSpec((None, tq, FOX_W), lambda b, i, j: (b, i, 0))
    return _pcall(
        body, name="fox_forward",
        out_shape=[jax.ShapeDtypeStruct((B, S, FOX_W), BF16), jax.ShapeDtypeStruct((B, S, FOX_W), F32),
                   jax.ShapeDtypeStruct((B, S, LANES), F32)],
        grid=(B, nq, nq),
        in_specs=[pl.BlockSpec((None, tq, FOX_W), lambda b, i, j: (b, i, qcol)),
                  pl.BlockSpec((None, tq, FOX_W), lambda b, i, j: (b, jnp.minimum(i, j), kcol)),
                  pl.BlockSpec((None, tq, FOX_W), lambda b, i, j: (b, jnp.minimum(i, j), vcol)),
                  pl.BlockSpec((None, 8, tq), lambda b, i, j: (b, 0, jnp.minimum(i, j)))],
        operands=[pm3, pm3, pm3, cumT],
        out_specs=[ospec, ospec, pl.BlockSpec((None, tq, LANES), lambda b, i, j: (b, i, 0))],
        scratch_shapes=[pltpu.VMEM((2, tq, tq), F32), pltpu.VMEM((2, tq, tq), BF16), pltpu.VMEM((2, tq, 1), F32),
                        pltpu.VMEM((FOX_HEADS, tq, 1), F32), pltpu.VMEM((FOX_HEADS, tq, LANES), F32)],
        params=_params(("parallel", "parallel", "arbitrary")), rider=rider)


def _fox_bwd(pm3, do, delta, lse, cumT, qcol, kcol, vcol, tq, rider=None):
    B, S, _ = pm3.shape
    nq = S // tq
    strips = range(0, tq, FOX_STRIP)

    def body(q_ref, k_ref, v_ref, do_ref, dl_ref, lse_ref, ck_ref, dq_ref, rs_ref, dk_ref, dv_ref, cs_ref,
             s_sc, dp_sc, p_sc, ds_sc, dq_sc, dk_sc, dv_sc):
        kj, qi = pl.program_id(1), pl.program_id(2)

        @pl.when((kj == 0) & (qi == 0))
        def _():
            dq_sc[...] = jnp.zeros_like(dq_sc)

        @pl.when(qi == 0)
        def _():
            dk_sc[...] = jnp.zeros_like(dk_sc)
            dv_sc[...] = jnp.zeros_like(dv_sc)

        def tile(diagonal):
            qrows = pl.ds(pl.multiple_of(qi * tq, tq), tq)
            for h in range(FOX_HEADS):
                hs = slice(HEAD_DIM * h, HEAD_DIM * (h + 1))
                qh, kh, doh = q_ref[:, hs] * SCALE, k_ref[:, hs], do_ref[:, hs]
                b = h % 2
                s_sc[b] = lax.dot_general(qh, kh, NT, preferred_element_type=F32)
                dp_sc[b] = lax.dot_general(doh, v_ref[:, hs], NT, preferred_element_type=F32)
                ck = ck_ref[h:h + 1, :]
                for r in strips:
                    rows = slice(r, r + FOX_STRIP)
                    s = s_sc[b, rows, :] - ck
                    if diagonal:
                        s = _causal_strip(s, r)
                    p = jnp.exp(s - lse_ref[rows, h:h + 1])
                    p_sc[b, rows, :] = p.astype(BF16)
                    ds_sc[b, rows, :] = (p * (dp_sc[b, rows, :] - dl_ref[rows, h:h + 1])).astype(BF16)
                dv_sc[h] += lax.dot_general(doh, p_sc[b], TN, preferred_element_type=F32)
                dk_sc[h] += lax.dot_general(_with_ones(qh), ds_sc[b], TN, preferred_element_type=F32)
                dq_sc[h, qrows, :] += lax.dot_general(ds_sc[b], _with_ones(kh), NN, preferred_element_type=F32)

        @pl.when(qi > kj)
        def _():
            tile(False)

        @pl.when(qi == kj)
        def _():
            tile(True)

        @pl.when(qi == nq - 1)
        def _():
            cs_ref[...] = jnp.zeros_like(cs_ref)
            for h in range(FOX_HEADS):
                hs = slice(HEAD_DIM * h, HEAD_DIM * (h + 1))
                dv_ref[:, hs] = dv_sc[h].T.astype(dv_ref.dtype)
                dk = dk_sc[h].T
                dk_ref[:, hs] = dk[:, :HEAD_DIM].astype(dk_ref.dtype)
                cs_ref[:, h:h + 1] = dk[:, HEAD_DIM:HEAD_DIM + 1]

        @pl.when((kj == nq - 1) & (qi == nq - 1))
        def _():
            rs_ref[...] = jnp.zeros_like(rs_ref)
            for h in range(FOX_HEADS):
                hs = slice(HEAD_DIM * h, HEAD_DIM * (h + 1))
                dq_ref[:, hs] = (dq_sc[h, :, :HEAD_DIM] * SCALE).astype(dq_ref.dtype)
                rs_ref[:, h:h + 1] = dq_sc[h, :, HEAD_DIM:HEAD_DIM + 1]

    def qside(width, col=0):
        return pl.BlockSpec((None, tq, width), lambda b, j, i: (b, jnp.maximum(i, j), col))

    kspec = pl.BlockSpec((None, tq, FOX_W), lambda b, j, i: (b, j, 0))
    return _pcall(
        body, name="fox_backward",
        out_shape=[jax.ShapeDtypeStruct((B, S, FOX_W), BF16), jax.ShapeDtypeStruct((B, S, LANES), F32),
                   jax.ShapeDtypeStruct((B, S, FOX_W), BF16), jax.ShapeDtypeStruct((B, S, FOX_W), BF16),
                   jax.ShapeDtypeStruct((B, S, LANES), F32)],
        grid=(B, nq, nq),
        in_specs=[qside(FOX_W, qcol),
                  pl.BlockSpec((None, tq, FOX_W), lambda b, j, i: (b, j, kcol)),
                  pl.BlockSpec((None, tq, FOX_W), lambda b, j, i: (b, j, vcol)),
                  qside(FOX_W), qside(LANES), qside(LANES),
                  pl.BlockSpec((None, 8, tq), lambda b, j, i: (b, 0, j))],
        operands=[pm3, pm3, pm3, do, delta, lse, cumT],
        out_specs=[pl.BlockSpec((None, S, FOX_W), lambda b, j, i: (b, 0, 0)),
                   pl.BlockSpec((None, S, LANES), lambda b, j, i: (b, 0, 0)),
                   kspec, kspec, pl.BlockSpec((None, tq, LANES), lambda b, j, i: (b, j, 0))],
        scratch_shapes=[pltpu.VMEM((2, tq, tq), F32), pltpu.VMEM((2, tq, tq), F32), pltpu.VMEM((2, tq, tq), BF16),
                        pltpu.VMEM((2, tq, tq), BF16), pltpu.VMEM((FOX_HEADS, S, LANES), F32),
                        pltpu.VMEM((FOX_HEADS, LANES, tq), F32), pltpu.VMEM((FOX_HEADS, HEAD_DIM, tq), F32)],
        params=_params(("parallel", "arbitrary", "arbitrary")), rider=rider)


def _fox_delta(do, o32, T, S):
    def fn(dov, ov):
        prod = dov.astype(F32) * ov
        lane = lax.broadcasted_iota(jnp.int32, (dov.shape[0], LANES), 1)
        delta = jnp.zeros((dov.shape[0], LANES), F32)
        for h in range(FOX_HEADS):
            hs = slice(HEAD_DIM * h, HEAD_DIM * (h + 1))
            delta = jnp.where(lane == h, jnp.sum(prod[:, hs], axis=-1, keepdims=True), delta)
        return [delta]

    return _rowwise("fox_delta", fn, T, _div(S, 512, 8), [(do, "row", None), (o32, "row", None)], [("row", LANES, F32)], S)[0]


def _alibi_slope(group, head):
    return 2.0 ** (-ALIBI_MAX_BIAS * (group * DIL_HPG + head + 1) / (N_DIL * DIL_HPG))


def _residue_order(a, B, S, d):
    C = a.shape[-1]
    if d == 1:
        return a.reshape(B, S, C)
    return a.reshape(B, S // d, d, C).transpose(0, 2, 1, 3).reshape(B * d, S // d, C)


def _token_order(a, B, S, d):
    C = a.shape[-1]
    if d == 1:
        return a.reshape(B * S, C)
    return a.reshape(B, d, S // d, C).transpose(0, 2, 1, 3).reshape(B * S, C)


def _band_scores(qh, kcat, slope_d, has_prev):
    qi = lax.broadcasted_iota(jnp.int32, (BLOCK, 2 * BLOCK), 0)
    c = lax.broadcasted_iota(jnp.int32, (BLOCK, 2 * BLOCK), 1)
    s = lax.dot_general(qh, kcat, NT, preferred_element_type=F32) - slope_d * (BLOCK + qi - c).astype(F32)
    valid = (c >= qi) & (c <= qi + BLOCK)
    if has_prev is not None:
        valid = valid & ((c >= BLOCK) | has_prev)
    return jnp.where(valid, s, NEG_INF)


def _band_operands(j, cur_ref, prev_ref, hs):
    if j == 0:
        return jnp.concatenate([prev_ref[:, hs], cur_ref[0:BLOCK, hs]], axis=0)
    return cur_ref[(j - 1) * BLOCK:(j + 1) * BLOCK, hs]


def _dil_specs(Ls, qb, cols):
    nsub = qb // BLOCK
    qcol, kcol, vcol = cols

    def cur(col):
        return pl.BlockSpec((None, qb, DIL_GW), lambda s, n: (s, n, col))

    def prev(col):
        return pl.BlockSpec((None, BLOCK, DIL_GW), lambda s, n: (s, jnp.maximum(n * nsub - 1, 0), col))

    return [cur(qcol), cur(kcol), prev(kcol), cur(vcol), prev(vcol)]


def _dil_fwd(group, src, cols):
    _, dilation = DIL_GROUPS[group]
    nseq, Ls, _ = src.shape
    qb = _div(Ls, 512, BLOCK)
    nsub = qb // BLOCK

    def body(q_ref, kc_ref, kp_ref, vc_ref, vp_ref, o_ref, lse_ref):
        has_prev = pl.program_id(1) > 0
        lse_ref[...] = jnp.zeros_like(lse_ref)
        for h in range(DIL_HPG):
            hs = slice(HEAD_DIM * h, HEAD_DIM * (h + 1))
            scores = [_band_scores(q_ref[j * BLOCK:(j + 1) * BLOCK, hs] * SCALE, _band_operands(j, kc_ref, kp_ref, hs),
                                   _alibi_slope(group, h) * dilation, has_prev if j == 0 else None) for j in range(nsub)]
            pending = None

            def write(j, m, acc):
                rows = slice(j * BLOCK, (j + 1) * BLOCK)
                l = acc[:, HEAD_DIM:HEAD_DIM + 1]
                o_ref[rows, hs] = acc[:, :HEAD_DIM] / l
                lse_ref[rows, h:h + 1] = m + jnp.log(l)

            for j in range(nsub):
                m = jnp.max(scores[j], axis=-1, keepdims=True)
                p = jnp.exp(scores[j] - m).astype(BF16)
                acc = lax.dot_general(p, _with_ones(_band_operands(j, vc_ref, vp_ref, hs)), NN, preferred_element_type=F32)
                if pending is not None:
                    write(*pending)
                pending = (j, m, acc)
            write(*pending)

    return pl.pallas_call(
        body, name=f"dil_forward_{group}",
        out_shape=[jax.ShapeDtypeStruct((nseq, Ls, DIL_GW), F32), jax.ShapeDtypeStruct((nseq, Ls, LANES), F32)],
        grid=(nseq, Ls // qb),
        in_specs=_dil_specs(Ls, qb, cols),
        out_specs=[pl.BlockSpec((None, qb, DIL_GW), lambda s, n: (s, n, 0)),
                   pl.BlockSpec((None, qb, LANES), lambda s, n: (s, n, 0))],
        compiler_params=_params(("parallel", "arbitrary")),
    )(src, src, src, src, src)


def _dil_bwd(group, src, cols, Lr, dyr, dlr):
    _, dilation = DIL_GROUPS[group]
    nseq, Ls, _ = src.shape
    qb = _div(Ls, 512, BLOCK)
    nsub, nb = qb // BLOCK, Ls // qb

    def body(q_ref, kc_ref, kp_ref, vc_ref, vp_ref, L_ref, dy_ref, dl_ref, dq_ref, dk_ref, dv_ref, dk_sc, dv_sc):
        n = pl.program_id(1)
        has_prev = n > 0

        @pl.when(n == 0)
        def _():
            dk_sc[...] = jnp.zeros_like(dk_sc)
            dv_sc[...] = jnp.zeros_like(dv_sc)

        base = pl.multiple_of(n * qb, BLOCK)
        for h in range(DIL_HPG):
            hs = slice(HEAD_DIM * h, HEAD_DIM * (h + 1))
            blocks = [slice(j * BLOCK, (j + 1) * BLOCK) for j in range(nsub)]
            qhs = [q_ref[rows, hs] * SCALE for rows in blocks]
            kcats = [_band_operands(j, kc_ref, kp_ref, hs) for j in range(nsub)]
            dyhs = [dy_ref[rows, hs] for rows in blocks]
            scores = [_band_scores(qhs[j], kcats[j], _alibi_slope(group, h) * dilation, has_prev if j == 0 else None)
                      for j in range(nsub)]
            dps = [lax.dot_general(dyhs[j], _band_operands(j, vc_ref, vp_ref, hs), NT, preferred_element_type=F32)
                   for j in range(nsub)]
            pending = None

            def write(j, dq, dk, dv):
                dq_ref[blocks[j], hs] = (dq * SCALE).astype(dq_ref.dtype)
                win = pl.ds(base + j * BLOCK, 2 * BLOCK)
                dk_sc[win, hs] += dk
                dv_sc[win, hs] += dv

            for j in range(nsub):
                p = jnp.exp(scores[j] - L_ref[blocks[j], h:h + 1])
                ds = (p * (dps[j] - dl_ref[blocks[j], h:h + 1])).astype(BF16)
                dq = lax.dot_general(ds, kcats[j], NN, preferred_element_type=F32)
                dk = lax.dot_general(ds, qhs[j], TN, preferred_element_type=F32)
                dv = lax.dot_general(p.astype(BF16), dyhs[j], TN, preferred_element_type=F32)
                if pending is not None:
                    write(*pending)
                pending = (j, dq, dk, dv)
            write(*pending)

        @pl.when(n == nb - 1)
        def _():
            dk_ref[...] = dk_sc[BLOCK:, :].astype(dk_ref.dtype)
            dv_ref[...] = dv_sc[BLOCK:, :].astype(dv_ref.dtype)

    own = pl.BlockSpec((None, qb, DIL_GW), lambda s, n: (s, n, 0))
    own128 = pl.BlockSpec((None, qb, LANES), lambda s, n: (s, n, 0))
    whole = pl.BlockSpec((None, Ls, DIL_GW), lambda s, n: (s, 0, 0))
    shape = jax.ShapeDtypeStruct((nseq, Ls, DIL_GW), BF16)
    return pl.pallas_call(
        body, name=f"dil_backward_{group}",
        out_shape=[shape, shape, shape],
        grid=(nseq, nb),
        in_specs=_dil_specs(Ls, qb, cols) + [own128, own, own128],
        out_specs=[own, whole, whole],
        scratch_shapes=[pltpu.VMEM((Ls + BLOCK, DIL_GW), F32), pltpu.VMEM((Ls + BLOCK, DIL_GW), F32)],
        compiler_params=_params(("parallel", "arbitrary")),
    )(src, src, src, src, src, Lr, dyr, dlr)


def _dil_combine(os_, lses, T, S):
    def fn(o0, o1, o2, l0, l1, l2):
        m = jnp.maximum(jnp.maximum(l0, l1), l2)
        e0, e1, e2 = jnp.exp(l0 - m), jnp.exp(l1 - m), jnp.exp(l2 - m)
        tot = e0 + e1 + e2
        w0, w1, w2 = e0 / tot, e1 / tot, e2 / tot
        parts = []
        for h in range(DIL_HPG):
            hs = slice(HEAD_DIM * h, HEAD_DIM * (h + 1))
            parts.append(w0[:, h:h + 1] * o0[:, hs] + w1[:, h:h + 1] * o1[:, hs] + w2[:, h:h + 1] * o2[:, hs])
        return [jnp.concatenate(parts, axis=1), m + jnp.log(tot)]

    ins = [(a, "row", None) for a in os_] + [(a, "row", None) for a in lses]
    return _rowwise("dil_combine", fn, T, _div(S, 512, 8), ins, [("row", DIL_GW, BF16), ("row", LANES, F32)], S)


def _dil_delta(dy, y, T, S):
    def fn(dyv, yv):
        prod = dyv * yv.astype(F32)
        lane = lax.broadcasted_iota(jnp.int32, (dyv.shape[0], LANES), 1)
        delta = jnp.zeros((dyv.shape[0], LANES), F32)
        for h in range(DIL_HPG):
            hs = slice(HEAD_DIM * h, HEAD_DIM * (h + 1))
            delta = jnp.where(lane == h, jnp.sum(prod[:, hs], axis=-1, keepdims=True), delta)
        return [delta, dyv]

    return _rowwise("dil_delta", fn, T, _div(S, 512, 8), [(dy, "row", None), (y, "row", None)],
                    [("row", LANES, F32), ("row", DIL_GW, BF16)], S)


def _ada_forward(c_all, w, b):
    n, D = c_all.shape
    cl = w.shape[1]

    def body(c_ref, w_ref, b_ref, o_ref, ca_ref):
        cv = c_ref[...]
        ca = (cv * _sigmoid(cv)).astype(BF16)
        ca_ref[...] = ca
        o_ref[...] = jnp.dot(ca, w_ref[...].astype(BF16), preferred_element_type=F32) + b_ref[...]

    return pl.pallas_call(
        body, name="ada_forward",
        out_shape=[jax.ShapeDtypeStruct((n, cl), F32), jax.ShapeDtypeStruct((n, D), BF16)],
        compiler_params=_params(),
    )(c_all, w, b)


def _ada_backward(ca, dmod_cols, dmod_all):
    n, D = ca.shape
    cl = dmod_cols.shape[1]

    def body(ca_ref, dc_ref, da_ref, gw_ref, gb_ref):
        gw_ref[...] = lax.dot_general(ca_ref[...], dc_ref[...].astype(BF16), (((0,), (0,)), ((), ())), preferred_element_type=F32)
        gb_ref[...] = _colsum(da_ref[...])

    return pl.pallas_call(
        body, name="ada_backward",
        out_shape=[jax.ShapeDtypeStruct((D, cl), F32), jax.ShapeDtypeStruct((1, dmod_all.shape[1]), F32)],
        compiler_params=_params(),
    )(ca, dmod_cols, dmod_all)


def _sum_devices(v):
    def body(v_ref, o_ref):
        tot = v_ref[0]
        for k in range(1, N_DEV):
            tot = tot + v_ref[k]
        o_ref[...] = tot

    return pl.pallas_call(body, name="sum_devices", out_shape=jax.ShapeDtypeStruct(v.shape[1:], F32))(v)


def _adamw(name, w, g, m, v):
    rows, cols = w.shape
    tr = _div(rows, 256, 8)

    def body(w_ref, g_ref, m_ref, v_ref, d_ref, nm_ref, nv_ref):
        gv = g_ref[...]
        nm = ADAM_B1 * m_ref[...] + (1.0 - ADAM_B1) * gv
        nv = ADAM_B2 * v_ref[...] + (1.0 - ADAM_B2) * (gv * gv)
        m_hat = nm / (1.0 - ADAM_B1 ** ADAM_STEP)
        v_hat = nv / (1.0 - ADAM_B2 ** ADAM_STEP)
        d_ref[...] = -ADAM_LR * (m_hat / (jnp.sqrt(v_hat) + ADAM_EPS) + ADAM_WD * w_ref[...])
        nm_ref[...] = nm
        nv_ref[...] = nv

    spec = pl.BlockSpec((tr, cols), lambda i: (i, 0))
    shape = jax.ShapeDtypeStruct((rows, cols), F32)
    return pl.pallas_call(
        body, name=name, out_shape=[shape, shape, shape], grid=(rows // tr,),
        in_specs=[spec, spec, spec, spec], out_specs=[spec, spec, spec],
        compiler_params=_params(("arbitrary",)),
    )(w, g, m, v)


def _pad_rows(a, rows):
    return a if a.shape[0] == rows else jnp.pad(a, ((0, rows - a.shape[0]), (0, 0)))


class _Packed:
    def __init__(self, kind, local_shape, D):
        self.kind, self.local_shape, self.D = kind, local_shape, D
        r, c = local_shape
        self.rows = {"T": c, "N": r, "F": r * c // D}[kind]
        self.rows_pad = -(-self.rows // ROW_ALIGN) * ROW_ALIGN

    def pack_local(self, w):
        if self.kind == "T":
            w = w.T
        elif self.kind == "F":
            w = w.reshape(self.rows, self.D)
        return _pad_rows(w, self.rows_pad)

    def full(self, gathered):
        g = gathered[:, :self.rows]
        if self.kind == "F":
            r, c = self.local_shape
            return g.reshape(N_DEV, r, c).transpose(1, 0, 2).reshape(r, N_DEV * c)
        return g.reshape(N_DEV * self.rows, self.D)

    def pack_grad(self, gfull):
        if self.kind == "F":
            r, c = self.local_shape
            g = gfull.reshape(r, N_DEV, c).transpose(1, 0, 2).reshape(N_DEV, self.rows, self.D)
        else:
            g = gfull.reshape(N_DEV, self.rows, self.D)
        if self.rows_pad != self.rows:
            g = jnp.pad(g, ((0, 0), (0, self.rows_pad - self.rows), (0, 0)))
        return g

    def unpack_local(self, g):
        g = g[:self.rows]
        if self.kind == "T":
            return g.T
        if self.kind == "F":
            return g.reshape(self.local_shape)
        return g


BIG = ["ffn1_w_gate", "ffn1_w_up", "ffn1_w_down", "w_in", "w_branch_a", "w_branch_b", "w_out",
       "ffn2_w_gate", "ffn2_w_up", "ffn2_w_down"]
BIG_KIND = {"ffn1_w_gate": "T", "ffn1_w_up": "T", "ffn1_w_down": "N", "w_in": "T", "w_branch_a": "F", "w_branch_b": "F",
            "w_out": "N", "ffn2_w_gate": "T", "ffn2_w_up": "T", "ffn2_w_down": "N"}
GROUPS = (("ffn1_w_gate", "ffn1_w_up", "ffn1_w_down"), ("w_in", "w_branch_a", "w_branch_b", "w_out"),
          ("ffn2_w_gate", "ffn2_w_up", "ffn2_w_down"))
SMALL = ["ada_b", "norm_ffn1", "norm_mix", "forget_bias", "norm_ffn2", "norm_final"]


def kernel(x, c, ada_w, ada_b, norm_ffn1, ffn1_w_gate, ffn1_w_up, ffn1_w_down, norm_mix, w_in, forget_bias, w_branch_a, w_branch_b, w_out, norm_ffn2, ffn2_w_gate, ffn2_w_up, ffn2_w_down, norm_final, loss_target, m_ada_w, m_ada_b, m_norm_ffn1, m_ffn1_w_gate, m_ffn1_w_up, m_ffn1_w_down, m_norm_mix, m_w_in, m_forget_bias, m_w_branch_a, m_w_branch_b, m_w_out, m_norm_ffn2, m_ffn2_w_gate, m_ffn2_w_up, m_ffn2_w_down, m_norm_final, v_ada_w, v_ada_b, v_norm_ffn1, v_ffn1_w_gate, v_ffn1_w_up, v_ffn1_w_down, v_norm_mix, v_w_in, v_forget_bias, v_w_branch_a, v_w_branch_b, v_w_out, v_norm_ffn2, v_ffn2_w_gate, v_ffn2_w_up, v_ffn2_w_down, v_norm_final):
    args = dict(locals())
    B, S, D = x.shape
    T = B * S
    cl = ada_w.shape[2]
    n_in = w_in.shape[2] * N_DEV
    nm = 2 * D + 3 * FOX_W + 3 * DIL_W
    nmp = -(-nm // 512) * 512
    GA, GB, QB, QA = 0, D, 2 * D, 2 * D + 3 * FOX_W
    xpos, ypos, cpos = _position()
    me = 4 * xpos + 2 * ypos + cpos

    packs = {n: _Packed(BIG_KIND[n], args[n].shape[1:], D) for n in BIG}
    offs, pads = {}, {}
    for names in GROUPS:
        r = 0
        for n in names:
            offs[n] = r
            r += packs[n].rows_pad
        pads[names] = -r % PACK_ROW_QUANTUM

    def pack_weights(names):
        return jnp.concatenate([packs[n].pack_local(args[n][0]).astype(BF16) for n in names]
                               + [jnp.zeros((pads[names], D), BF16)], axis=0)

    def unpack_weights(names, land):
        out = {}
        for n in names:
            p = packs[n]
            if p.kind in "TN" and p.rows == p.rows_pad and offs[n] % p.rows == 0:
                out[n] = _Slab(land, offs[n], p.rows)
            else:
                out[n] = p.full(land[:, offs[n]:offs[n] + p.rows_pad])
        return out

    def pack_grads(names, gfull):
        return jnp.concatenate([packs[n].pack_grad(gfull[n]) for n in names] + [jnp.zeros((N_DEV, pads[names], D), F32)], axis=1)

    def unpack_grads(names, g_local):
        return {n: packs[n].unpack_local(g_local[offs[n]:offs[n] + packs[n].rows_pad])[None] for n in names}

    W = unpack_weights(GROUPS[0], _weight_allgather(pack_weights(GROUPS[0])))

    c_all = _small_allgather(c, "gather_c").reshape(N_DEV * B, D)
    b_cols = lax.dynamic_slice(ada_b, (0, me * cl), (1, cl))
    mod_cols, c_act = _ada_forward(c_all, ada_w[0], b_cols)
    mod_all = _small_allgather(mod_cols, "gather_mod").transpose(1, 0, 2).reshape(N_DEV * B, N_MOD * D)
    mod = lax.dynamic_slice(mod_all, (me * B, 0), (B, N_MOD * D)).reshape(B, N_MOD, 1, D)
    sh1, sc1, gt1, sh2, sc2, gt2, sh3, sc3, gt3 = [mod[:, i] for i in range(N_MOD)]

    x0 = x.reshape(T, D)
    x1, saved1, land = _ffn_forward("ffn1", x0, norm_ffn1, sh1, sc1, gt1, W["ffn1_w_gate"], W["ffn1_w_up"], W["ffn1_w_down"], S,
                                    gather=pack_weights(GROUPS[1]))
    W.update(unpack_weights(GROUPS[1], land))
    winT = W["w_in"]
    o_f = 3 * DIL_W + 3 * FOX_W
    wmT = jnp.concatenate([winT[o_f + 8:], winT[3 * DIL_W:o_f], winT[:3 * DIL_W], jnp.zeros((nmp - nm, D), BF16)], axis=0)
    wfT = jnp.concatenate([winT[o_f:o_f + 8], jnp.zeros((LANES - 8, D), BF16)], axis=0)

    tm1k = _div(T, 1024, 8)
    fb = jnp.pad(forget_bias, ((0, 0), (0, LANES - FOX_HEADS)))

    def proj(accs, fbv):
        fl = accs[1] + fbv
        lane = lax.broadcasted_iota(jnp.int32, fl.shape, 1)
        ls = jnp.minimum(fl, 0.0) - jnp.log(1.0 + jnp.exp(-jnp.abs(fl)))
        return [accs[0], jnp.where(lane < FOX_HEADS, ls, 0.0), fl]

    tms = _div(S, 512, 8)
    h2, pm, logsig, flog, land = _norm_matmul("mix_proj", x1, norm_mix, sc2, sh2, [wmT, wfT], proj,
                                              [(nmp, BF16), (LANES, F32), (LANES, F32)], S, vecs=[fb],
                                              rider=_gather_direct_rider(pack_weights(GROUPS[2])))
    cum = _cumsum(logsig.reshape(B, S, LANES))
    cumT = cum[:, :, :8].transpose(0, 2, 1)
    pm3 = pm.reshape(B, S, nmp)
    tq = _div(S, 512, LANES)
    qcol, kcol, vcol = QB // FOX_W, QB // FOX_W + 1, QB // FOX_W + 2
    o_b, o_b32, lse_b, land = _fox_fwd(pm3, cumT, qcol, kcol, vcol, tq, rider=_gather_forward_rider(land))
    W.update(unpack_weights(GROUPS[2], land))
    y_b = o_b.reshape(T, FOX_W)

    qa_blk = QA // DIL_GW
    dil_src, dil_cols = [], []
    for g, (_, d) in enumerate(DIL_GROUPS):
        if d == 1:
            dil_src.append(pm3)
            dil_cols.append((qa_blk + g, qa_blk + N_DIL + g, qa_blk + 2 * N_DIL + g))
        else:
            starts = [QA + (i * N_DIL + g) * DIL_GW for i in range(3)]
            qkv = jnp.concatenate([pm[:, c:c + DIL_GW] for c in starts], axis=1)
            dil_src.append(_residue_order(qkv, B, S, d))
            dil_cols.append((0, 1, 2))
    dil_o, dil_lse = [], []
    for g, (_, d) in enumerate(DIL_GROUPS):
        o_g, lse_g = _dil_fwd(g, dil_src[g], dil_cols[g])
        dil_o.append(_token_order(o_g, B, S, d))
        dil_lse.append(_token_order(lse_g, B, S, d))
    y_a, L_a = _dil_combine(dil_o, dil_lse, T, S)

    wa, wb, wout = W["w_branch_a"], W["w_branch_b"], W["w_out"]
    tnd = D
    tm5 = _div(T, 512, 8)
    yap = _matmul("mix_branch_a", "nn", [[(y_a, wa)]], T, D, DIL_GW, tm5, tnd, DIL_GW, [BF16])[0]

    def merge(accs, ex):
        yapv, gav, gbv = ex
        ybp = accs[0]
        return [ybp, _sigmoid(gav.astype(F32)) * yapv.astype(F32) + _sigmoid(gbv.astype(F32)) * ybp]

    ybp, merged = _matmul("mix_branch_b", "nn", [[(y_b, wb)]], T, D, FOX_W, tm5, tnd, FOX_W, [BF16, BF16],
                          extras=[(yap, "tile", 0), (pm, "tile", GA), (pm, "tile", GB)], epilogue=merge)

    def out_proj(accs, ex):
        xv, gtv = ex
        return [xv + gtv * accs[0], accs[0]]

    x2, ymix = _matmul("mix_out", "nn", [[(merged, wout)]], T, D, D, tms, tnd, D, [F32, BF16],
                       extras=[(x1, "tile", 0), (gt2, "brow", 0)], epilogue=out_proj, rows_per_example=S)

    (dx3, loss_b, dg_final), saved3, _ = _ffn_forward(
        "ffn2", x2, norm_ffn2, sh3, sc3, gt3, W["ffn2_w_gate"], W["ffn2_w_up"], W["ffn2_w_down"], S,
        loss=(loss_target.reshape(T, D), norm_final.reshape(1, D)))

    dx2, (dsh3, dsc3, dgt3, dg3), (dwg2, dwu2, dwd2), _ = _ffn_backward(
        "ffn2", dx3, saved3, norm_ffn2, sc3, gt3, W["ffn2_w_gate"], W["ffn2_w_up"], W["ffn2_w_down"], S)
    g3 = pack_grads(GROUPS[2], {"ffn2_w_gate": dwg2, "ffn2_w_up": dwu2, "ffn2_w_down": dwd2})

    def merge_grad(dm, ex):
        gav, gbv, yapv, ybpv = [e.astype(F32) for e in ex]
        sga, sgb = _sigmoid(gav), _sigmoid(gbv)
        return [dm * sga, dm * sgb, dm * yapv * sga * (1.0 - sga), dm * ybpv * sgb * (1.0 - sgb)]

    dym, dgt2, dyap, dybp, dga, dgb, sib3 = _gated_grad_matmul(
        "mix_merge_grad", dx2, ymix, gt2, 1.0, wout, [(pm, GA // D), (pm, GB // D), (yap, 0), (ybp, 0)], merge_grad, [BF16] * 4, S,
        rider=_sibling_exchange_rider(g3))
    sums3, own3 = _chip_sums(g3, sib3)
    tkw = _div(T, 1024, LANES)
    dwout = _matmul("mix_dw_out", "tn", [[(merged, dym)]], D, D, T, D, D, tkw, [F32])[0]
    dwa = _matmul("mix_dw_a", "tn", [[(y_a, dyap)]], DIL_GW, D, T, DIL_GW, D, tkw, [F32])[0]
    dwb = _matmul("mix_dw_b", "tn", [[(y_b, dybp)]], FOX_W, D, T, FOX_W, D, tkw, [F32])[0]
    dy_a = _matmul("mix_dy_a", "nt", [[(dyap, wa)]], T, DIL_GW, D, tm1k, DIL_GW, D, [F32])[0]
    dy_b = _matmul("mix_dy_b", "nt", [[(dybp, wb)]], T, FOX_W, D, tm1k, FOX_W, D, [BF16])[0]

    do3 = dy_b.reshape(B, S, FOX_W)
    delta_b = _fox_delta(dy_b, o_b32.reshape(T, FOX_W), T, S).reshape(B, S, LANES)
    dq_b, ds_rows, dk_b, dv_b, ds_cols, recv3 = _fox_bwd(pm3, do3, delta_b, lse_b, cumT, qcol, kcol, vcol, tq,
                                                         rider=_chip_exchange_rider(sums3))

    delta_a, dy_a16 = _dil_delta(dy_a, y_a, T, S)
    dqs, dks, dvs = [], [], []
    for g, (_, d) in enumerate(DIL_GROUPS):
        dq_g, dk_g, dv_g = _dil_bwd(g, dil_src[g], dil_cols[g], _residue_order(L_a, B, S, d),
                                    _residue_order(dy_a16, B, S, d), _residue_order(delta_a, B, S, d))
        dqs.append(_token_order(dq_g, B, S, d))
        dks.append(_token_order(dk_g, B, S, d))
        dvs.append(_token_order(dv_g, B, S, d))

    dcum = ds_rows - ds_cols
    dcum_run = _cumsum(dcum)
    dcum_tot = dcum_run[:, S - 1:S, :]

    def forget_grad_fn(run, dcv, fl, tot):
        lane = lax.broadcasted_iota(jnp.int32, fl.shape, 1)
        df = jnp.where(lane < FOX_HEADS, (tot - run + dcv) * _sigmoid_exp(-fl), 0.0)
        return [df, _colsum(df)]

    df16, dfb = _rowwise("forget_gate_grad", forget_grad_fn, T, tms,
                         [(dcum_run.reshape(T, LANES), "row", None), (dcum.reshape(T, LANES), "row", None), (flog, "row", None),
                          (dcum_tot, "bvec", None)],
                         [("row", LANES, BF16), ("bacc", LANES, F32)], S)

    dpm = jnp.concatenate([dga, dgb, dq_b.reshape(T, FOX_W), dk_b.reshape(T, FOX_W), dv_b.reshape(T, FOX_W)]
                          + dqs + dks + dvs + ([jnp.zeros((T, nmp - nm), BF16)] if nmp > nm else []), axis=1)
    tmn = _div(nmp, 2048, LANES)
    dwmT = _matmul("mix_dw_in", "tn", [[(dpm, h2)]], nmp, D, T, tmn, D, tkw, [F32])[0]
    dwfT = _matmul("mix_dw_f", "tn", [[(df16, h2)]], LANES, D, T, LANES, D, tkw, [F32])[0]
    dwinT = jnp.concatenate([dwmT[QA:QA + 3 * DIL_W], dwmT[QB:QB + 3 * FOX_W], dwfT[:8], dwmT[GA:2 * D]], axis=0)
    g2 = pack_grads(GROUPS[1], {"w_in": dwinT, "w_branch_a": dwa, "w_branch_b": dwb, "w_out": dwout})
    dx1, dsh2, dsc2, dgmix, sib2 = _matmul_normmod_bwd("mix_dh", [(dpm, wmT), (df16, wfT)], x1, norm_mix, sc2, dx2, S,
                                                       rider=_sibling_exchange_rider(g2))
    sums2, own2 = _chip_sums(g2, sib2)

    own1 = []

    def ffn1_exchange(dwg, dwu, dwd):
        sums1, own = _chip_sums(pack_grads(GROUPS[0], {"ffn1_w_gate": dwg, "ffn1_w_up": dwu, "ffn1_w_down": dwd}))
        own1.append(own)
        return _chip_exchange_rider(sums1)

    dx0, (dsh1, dsc1, dgt1, dg1), _, (recv2, recv1) = _ffn_backward(
        "ffn1", dx1, saved1, norm_ffn1, sc1, gt1, W["ffn1_w_gate"], W["ffn1_w_up"], W["ffn1_w_down"], S,
        rider=_chip_exchange_rider(sums2), dh_rider=ffn1_exchange)
    own1 = own1[0]
    grad_x = dx0.reshape(B, S, D)

    dmod = jnp.concatenate([dsh1, dsc1, dgt1, dsh2, dsc2, dgt2, dsh3, dsc3, dgt3], axis=1).reshape(B, N_MOD * D)
    fbg = jnp.sum(dfb, axis=0)
    small = jnp.concatenate([jnp.sum(dg1, axis=0), jnp.sum(dgmix, axis=0), jnp.sum(dg3, axis=0), jnp.sum(dg_final, axis=0),
                             fbg, jnp.sum(loss_b, axis=0)], axis=1)
    n_small = small.shape[1]
    tail = _small_allgather(jnp.concatenate([dmod, jnp.pad(small, ((0, 0), (0, N_MOD * D - n_small)))], axis=0), "gather_tail")
    dmod_all = tail[:, :B].reshape(N_DEV * B, N_MOD * D)
    dmod_cols = lax.dynamic_slice(dmod_all, (0, me * cl), (N_DEV * B, cl))
    g_ada_w, g_ada_b = _ada_backward(c_act, dmod_cols, dmod_all)

    small = _sum_devices(tail[:, B:, :n_small])
    g_small = {"norm_ffn1": small[:, 0:D], "norm_mix": small[:, D:2 * D], "norm_ffn2": small[:, 2 * D:3 * D],
               "norm_final": small[:, 3 * D:4 * D], "forget_bias": small[:, 4 * D:4 * D + FOX_HEADS], "ada_b": g_ada_b}
    loss = small[0, 4 * D + LANES]

    grads = {"ada_w": g_ada_w[None]}
    for names, own, recv in ((GROUPS[0], own1, recv1), (GROUPS[1], own2, recv2), (GROUPS[2], own3, recv3)):
        grads.update(unpack_grads(names, _final_grad_sum(own, recv)))

    delta, new_m, new_v = {}, {}, {}
    for n in ["ada_w"] + BIG:
        shp = args[n].shape
        d_, m_, v_ = _adamw(f"adamw_{n}", args[n][0], grads[n][0], args["m_" + n][0], args["v_" + n][0])
        delta[n], new_m[n], new_v[n] = d_.reshape(shp), m_.reshape(shp), v_.reshape(shp)
    sizes = [args[n].size for n in SMALL]
    tot = sum(sizes)
    padded = -(-tot // (8 * LANES)) * (8 * LANES)

    def flat(get):
        v = jnp.concatenate([get(n).reshape(-1) for n in SMALL])
        return jnp.pad(v, (0, padded - tot)).reshape(8, padded // 8)

    d_s, m_s, v_s = _adamw("adamw_small", flat(lambda n: args[n]), flat(lambda n: g_small[n]), flat(lambda n: args["m_" + n]),
                           flat(lambda n: args["v_" + n]))
    o = 0
    for n, sz in zip(SMALL, sizes):
        shp = args[n].shape
        grads[n] = g_small[n].reshape(shp)
        delta[n] = d_s.reshape(-1)[o:o + sz].reshape(shp)
        new_m[n] = m_s.reshape(-1)[o:o + sz].reshape(shp)
        new_v[n] = v_s.reshape(-1)[o:o + sz].reshape(shp)
        o += sz

    order = ["ada_w", "ada_b", "norm_ffn1", "ffn1_w_gate", "ffn1_w_up", "ffn1_w_down", "norm_mix", "w_in", "forget_bias",
             "w_branch_a", "w_branch_b", "w_out", "norm_ffn2", "ffn2_w_gate", "ffn2_w_up", "ffn2_w_down", "norm_final"]
    return (loss, grad_x, *[grads[n] for n in order], *[delta[n] for n in order], *[new_m[n] for n in order],
            *[new_v[n] for n in order])
```

```python
import functools
import math

import jax
import jax.numpy as jnp
from jax import lax
from jax.experimental import pallas as pl
from jax.experimental.pallas import tpu as pltpu

F32 = jnp.float32
BF16 = jnp.bfloat16
MESH = pl.DeviceIdType.MESH
ANY = pl.BlockSpec(memory_space=pl.ANY)
VMEM_SPEC = pl.BlockSpec(memory_space=pltpu.VMEM)

N_DEV = 8
HEAD_DIM = 64
BLOCK = 128
DIL_GROUPS = ((128, 1), (512, 4), (2048, 16))
N_DIL = len(DIL_GROUPS)
DIL_HPG = 4
DIL_GW = DIL_HPG * HEAD_DIM
DIL_W = N_DIL * DIL_GW
FOX_HEADS = 8
FOX_W = FOX_HEADS * HEAD_DIM
N_MOD = 9
RMS_EPS = 1e-6
ALIBI_MAX_BIAS = 8.0
NEG_INF = -1e30
ADAM_LR, ADAM_B1, ADAM_B2, ADAM_EPS, ADAM_WD, ADAM_STEP = 0.001, 0.9, 0.999, 1e-08, 0.01, 10
V7X_VMEM_LIMIT = 52 * 1024 * 1024
LANES = 128
ROW_ALIGN = 16
PACK_ROW_QUANTUM = 32
FOX_STRIP = 32
SCALE = 1.0 / math.sqrt(HEAD_DIM)


def _div(dim, target, quantum):
    best = None
    for t in range(quantum, min(dim, target) + 1, quantum):
        if dim % t == 0:
            best = t
    return best or dim


def _params(sem=None):
    return pltpu.CompilerParams(dimension_semantics=sem, vmem_limit_bytes=V7X_VMEM_LIMIT)


def _sigmoid(x):
    return 0.5 * jnp.tanh(0.5 * x) + 0.5


def _sigmoid_exp(x):
    return 1.0 / (1.0 + jnp.exp(-x))


def _position():
    x, y, c = lax.axis_index("x"), lax.axis_index("y"), lax.axis_index("c")
    return x, y, c


def _small_allgather(v, name):
    rows, cols = v.shape

    def body(v_ref, out_ref, send_sems, recv_sems):
        x, y, c = _position()
        me = 4 * x + 2 * y + c
        out_ref[me] = v_ref[...]

        def peer(k):
            return (1 - x if k & 4 else x, 1 - y if k & 2 else y, 1 - c if k & 1 else c)

        def copy(k, slot):
            return pltpu.make_async_remote_copy(
                src_ref=v_ref, dst_ref=out_ref.at[slot], send_sem=send_sems.at[k - 1], recv_sem=recv_sems.at[k - 1],
                device_id=peer(k), device_id_type=MESH)

        sends = [copy(k, me) for k in range(1, N_DEV)]
        for cp in sends:
            cp.start()
        for k in range(1, N_DEV):
            px, py, pc = peer(k)
            copy(k, 4 * px + 2 * py + pc).wait_recv()
        for cp in sends:
            cp.wait_send()

    return pl.pallas_call(
        body, name=name,
        out_shape=jax.ShapeDtypeStruct((N_DEV, rows, cols), v.dtype),
        in_specs=[VMEM_SPEC], out_specs=VMEM_SPEC,
        scratch_shapes=[pltpu.SemaphoreType.DMA((N_DEV - 1,)), pltpu.SemaphoreType.DMA((N_DEV - 1,))],
    )(v)


def _weight_allgather(p):
    rows, cols = p.shape

    def body(p_ref, out_ref, send_sems, recv_sems, local_sem):
        x, y, c = _position()
        me, sibling = (x, y, c), (x, y, 1 - c)
        chips = [(1 - x, y), (x, 1 - y), (1 - x, 1 - y)]

        def slot(px, py, pc):
            return out_ref.at[4 * px + 2 * py + pc]

        def copy(k, block, to, src=None):
            return pltpu.make_async_remote_copy(
                src_ref=slot(*block) if src is None else src, dst_ref=slot(*block),
                send_sem=send_sems.at[k], recv_sem=recv_sems.at[k], device_id=to, device_id_type=MESH)

        mine = pltpu.make_async_copy(p_ref, slot(*me), local_sem)
        mine.start()
        first = [copy(0, me, sibling, src=p_ref)]
        first += [copy(1 + j, me, (*chip, c), src=p_ref) for j, chip in enumerate(chips)]
        for cp in first:
            cp.start()
        passed = [copy(4 + j, (*chip, c), sibling) for j, chip in enumerate(chips)]
        for j, chip in enumerate(chips):
            copy(1 + j, (*chip, c), me).wait_recv()
            passed[j].start()
        copy(0, sibling, me).wait_recv()
        for j, chip in enumerate(chips):
            copy(4 + j, (*chip, 1 - c), me).wait_recv()
        for cp in first + passed:
            cp.wait_send()
        mine.wait()

    return pl.pallas_call(
        body, name="weight_allgather",
        out_shape=jax.ShapeDtypeStruct((N_DEV, rows, cols), p.dtype),
        in_specs=[ANY], out_specs=ANY,
        scratch_shapes=[pltpu.SemaphoreType.DMA((7,)), pltpu.SemaphoreType.DMA((7,)), pltpu.SemaphoreType.DMA],
    )(p)


def _grad_exchange_sibling(g):
    _, rows, cols = g.shape

    def body(g_ref, out_ref, send_sems, recv_sems):
        x, y, c = _position()
        sibling = (x, y, 1 - c)

        def copy(q):
            px, py = q >> 1, q & 1
            return pltpu.make_async_remote_copy(
                src_ref=g_ref.at[4 * px + 2 * py + (1 - c)], dst_ref=out_ref.at[q],
                send_sem=send_sems.at[q], recv_sem=recv_sems.at[q], device_id=sibling, device_id_type=MESH)

        copies = [copy(q) for q in range(4)]
        for cp in copies:
            cp.start()
        for cp in copies:
            cp.wait_recv()
        for cp in copies:
            cp.wait_send()

    return pl.pallas_call(
        body, name="grad_exchange_sibling",
        out_shape=jax.ShapeDtypeStruct((4, rows, cols), g.dtype),
        in_specs=[ANY], out_specs=ANY,
        scratch_shapes=[pltpu.SemaphoreType.DMA((4,)), pltpu.SemaphoreType.DMA((4,))],
    )(g)


class _Rider:
    def __init__(self, operands, out_shapes, n_send, n_recv, start, finish, aliases=None):
        self.operands, self.out_shapes = list(operands), list(out_shapes)
        self.n_send, self.n_recv, self.start, self.finish = n_send, n_recv, start, finish
        self.aliases = aliases or {}


def _pcall(body, *, name, grid, in_specs, operands, out_shape, out_specs, scratch_shapes, params, rider=None):
    if rider is None:
        return pl.pallas_call(body, name=name, out_shape=out_shape, grid=grid, in_specs=in_specs, out_specs=out_specs,
                              scratch_shapes=scratch_shapes, compiler_params=params)(*operands)
    n_in, n_out, n_sc = len(operands), len(out_shape), len(scratch_shapes)
    r_in, r_out = len(rider.operands), len(rider.out_shapes)

    def wrapped(*refs):
        ins, rins = refs[:n_in], refs[n_in:n_in + r_in]
        outs, routs = refs[n_in + r_in:n_in + r_in + n_out], refs[n_in + r_in + n_out:n_in + r_in + n_out + r_out]
        rest = refs[n_in + r_in + n_out + r_out:]
        scratch, sems = rest[:n_sc], rest[n_sc:]
        ids = [pl.program_id(a) for a in range(len(grid))]
        first, last = ids[0] == 0, ids[0] == grid[0] - 1
        for a in range(1, len(grid)):
            first, last = first & (ids[a] == 0), last & (ids[a] == grid[a] - 1)

        @pl.when(first)
        def _():
            rider.start(rins, routs, *sems)

        body(*ins, *outs, *scratch)

        @pl.when(last)
        def _():
            rider.finish(rins, routs, *sems)

    return pl.pallas_call(
        wrapped, name=name, out_shape=list(out_shape) + rider.out_shapes, grid=grid,
        in_specs=list(in_specs) + [ANY] * r_in, out_specs=list(out_specs) + [ANY] * r_out,
        scratch_shapes=list(scratch_shapes) + [pltpu.SemaphoreType.DMA((rider.n_send,)), pltpu.SemaphoreType.DMA((rider.n_recv,))],
        input_output_aliases={n_in + i: n_out + o for i, o in rider.aliases.items()},
        compiler_params=params,
    )(*operands, *rider.operands)


def _flips(x, y, c):
    return [(x, y, 1 - c), (1 - x, y, c), (x, 1 - y, c), (1 - x, 1 - y, c)]


def _gather_direct_rider(p):
    rows, cols = p.shape

    def copies(p_ref, land, send_sems, recv_sems):
        x, y, c = _position()
        me = 4 * x + 2 * y + c
        peers = _flips(x, y, c)
        sends = [pltpu.make_async_remote_copy(src_ref=p_ref, dst_ref=land.at[me], send_sem=send_sems.at[k], recv_sem=recv_sems.at[k],
                                              device_id=to, device_id_type=MESH) for k, to in enumerate(peers)]
        recvs = [pltpu.make_async_remote_copy(src_ref=p_ref, dst_ref=land.at[4 * px + 2 * py + pc], send_sem=send_sems.at[k],
                                              recv_sem=recv_sems.at[k], device_id=(px, py, pc), device_id_type=MESH)
                 for k, (px, py, pc) in enumerate(peers)]
        mine = pltpu.make_async_copy(p_ref, land.at[me], send_sems.at[len(peers)])
        return sends, recvs, mine

    def start(rins, routs, send_sems, recv_sems):
        sends, _, mine = copies(rins[0], routs[0], send_sems, recv_sems)
        mine.start()
        for cp in sends:
            cp.start()

    def finish(rins, routs, send_sems, recv_sems):
        sends, recvs, mine = copies(rins[0], routs[0], send_sems, recv_sems)
        for cp in recvs:
            cp.wait_recv()
        for cp in sends:
            cp.wait_send()
        mine.wait()

    return _Rider([p], [jax.ShapeDtypeStruct((N_DEV, rows, cols), p.dtype)], 5, 4, start, finish)


def _gather_forward_rider(land):
    def copies(buf, send_sems, recv_sems):
        x, y, c = _position()
        chips = [(1 - x, y), (x, 1 - y), (1 - x, 1 - y)]
        sends = [pltpu.make_async_remote_copy(src_ref=buf.at[4 * px + 2 * py + c], dst_ref=buf.at[4 * px + 2 * py + c],
                                              send_sem=send_sems.at[k], recv_sem=recv_sems.at[k], device_id=(x, y, 1 - c),
                                              device_id_type=MESH) for k, (px, py) in enumerate(chips)]
        recvs = [pltpu.make_async_remote_copy(src_ref=buf.at[4 * px + 2 * py + 1 - c], dst_ref=buf.at[4 * px + 2 * py + 1 - c],
                                              send_sem=send_sems.at[k], recv_sem=recv_sems.at[k], device_id=(x, y, 1 - c),
                                              device_id_type=MESH) for k, (px, py) in enumerate(chips)]
        return sends, recvs

    def start(rins, routs, send_sems, recv_sems):
        for cp in copies(routs[0], send_sems, recv_sems)[0]:
            cp.start()

    def finish(rins, routs, send_sems, recv_sems):
        sends, recvs = copies(routs[0], send_sems, recv_sems)
        for cp in recvs:
            cp.wait_recv()
        for cp in sends:
            cp.wait_send()

    return _Rider([land], [jax.ShapeDtypeStruct(land.shape, land.dtype)], 3, 3, start, finish, aliases={0: 0})


def _chip_exchange_rider(s):
    def copies(s_ref, out_ref, send_sems, recv_sems):
        x, y, c = _position()
        chips = [(1 - x, y), (x, 1 - y), (1 - x, 1 - y)]
        return [pltpu.make_async_remote_copy(src_ref=s_ref.at[k], dst_ref=out_ref.at[k], send_sem=send_sems.at[k],
                                             recv_sem=recv_sems.at[k], device_id=(*chips[k], c), device_id_type=MESH)
                for k in range(3)]

    def start(rins, routs, send_sems, recv_sems):
        for cp in copies(rins[0], routs[0], send_sems, recv_sems):
            cp.start()

    def finish(rins, routs, send_sems, recv_sems):
        cps = copies(rins[0], routs[0], send_sems, recv_sems)
        for cp in cps:
            cp.wait_recv()
        for cp in cps:
            cp.wait_send()

    return _Rider([s], [jax.ShapeDtypeStruct(s.shape, s.dtype)], 3, 3, start, finish)


def _chip_partial_sums(g, recv_sib, jj, qq):
    _, rows, cols = g.shape
    tr = _div(rows, 512, ROW_ALIGN)

    def body(jj_ref, qq_ref, g_ref, r_ref, o_ref):
        o_ref[...] = (g_ref[...] + r_ref[...]).astype(o_ref.dtype)

    return pl.pallas_call(
        body, name="chip_partial_sums",
        out_shape=jax.ShapeDtypeStruct((3, rows, cols), BF16),
        grid_spec=pltpu.PrefetchScalarGridSpec(
            num_scalar_prefetch=2, grid=(3, rows // tr),
            in_specs=[pl.BlockSpec((None, tr, cols), lambda k, i, jj, qq: (jj[k], i, 0)),
                      pl.BlockSpec((None, tr, cols), lambda k, i, jj, qq: (qq[k], i, 0))],
            out_specs=pl.BlockSpec((None, tr, cols), lambda k, i, jj, qq: (k, i, 0))),
        compiler_params=_params(("arbitrary", "arbitrary")),
    )(jj, qq, g, recv_sib)


def _own_partial_sum(g, recv_sib, jj, qq):
    _, rows, cols = g.shape
    tr = _div(rows, 512, ROW_ALIGN)

    def body(jj_ref, qq_ref, g_ref, r_ref, o_ref):
        o_ref[...] = g_ref[...] + r_ref[...]

    return pl.pallas_call(
        body, name="own_partial_sum",
        out_shape=jax.ShapeDtypeStruct((rows, cols), F32),
        grid_spec=pltpu.PrefetchScalarGridSpec(
            num_scalar_prefetch=2, grid=(rows // tr,),
            in_specs=[pl.BlockSpec((None, tr, cols), lambda i, jj, qq: (jj[0], i, 0)),
                      pl.BlockSpec((None, tr, cols), lambda i, jj, qq: (qq[0], i, 0))],
            out_specs=pl.BlockSpec((tr, cols), lambda i, jj, qq: (i, 0))),
        compiler_params=_params(("arbitrary",)),
    )(jj, qq, g, recv_sib)


def _final_grad_sum(own, recv):
    rows, cols = own.shape
    tr = _div(rows, 512, ROW_ALIGN)

    def body(o_ref, r_ref, out_ref):
        out_ref[...] = ((o_ref[...] + r_ref[0].astype(F32)) + r_ref[1].astype(F32)) + r_ref[2].astype(F32)

    return pl.pallas_call(
        body, name="final_grad_sum",
        out_shape=jax.ShapeDtypeStruct((rows, cols), F32),
        grid=(rows // tr,),
        in_specs=[pl.BlockSpec((tr, cols), lambda i: (i, 0)), pl.BlockSpec((3, tr, cols), lambda i: (0, i, 0))],
        out_specs=pl.BlockSpec((tr, cols), lambda i: (i, 0)),
        compiler_params=_params(("arbitrary",)),
    )(own, recv)


def _sibling_exchange_rider(g):
    _, rows, cols = g.shape

    def copies(g_ref, out_ref, send_sems, recv_sems):
        x, y, c = _position()
        return [pltpu.make_async_remote_copy(
            src_ref=g_ref.at[4 * (q >> 1) + 2 * (q & 1) + (1 - c)], dst_ref=out_ref.at[q], send_sem=send_sems.at[q],
            recv_sem=recv_sems.at[q], device_id=(x, y, 1 - c), device_id_type=MESH) for q in range(4)]

    def start(rins, routs, send_sems, recv_sems):
        for cp in copies(rins[0], routs[0], send_sems, recv_sems):
            cp.start()

    def finish(rins, routs, send_sems, recv_sems):
        cps = copies(rins[0], routs[0], send_sems, recv_sems)
        for cp in cps:
            cp.wait_recv()
        for cp in cps:
            cp.wait_send()

    return _Rider([g], [jax.ShapeDtypeStruct((4, rows, cols), g.dtype)], 4, 4, start, finish)


def _chip_sums(g, recv_sib=None):
    x, y, c = _position()
    chips = [(1 - x, y), (x, 1 - y), (1 - x, 1 - y)]
    jj = jnp.stack([4 * px + 2 * py + c for px, py in chips]).astype(jnp.int32)
    qq = jnp.stack([2 * px + py for px, py in chips]).astype(jnp.int32)
    jme = jnp.reshape(4 * x + 2 * y + c, (1,)).astype(jnp.int32)
    qme = jnp.reshape(2 * x + y, (1,)).astype(jnp.int32)
    if recv_sib is None:
        recv_sib = _grad_exchange_sibling(g)
    return _chip_partial_sums(g, recv_sib, jj, qq), _own_partial_sum(g, recv_sib, jme, qme)


def _matmul(name, form, prods, M, N, K, tm, tn, tk, out_dtypes, extras=(), epilogue=None, rows_per_example=None, rider=None):
    nk = K // tk
    n_acc = len(prods)
    flat = [ab for group in prods for ab in group]
    dims = {"nn": (((1,), (0,)), ((), ())), "nt": (((1,), (1,)), ((), ())), "tn": (((0,), (0,)), ((), ()))}[form]
    direct = nk > 1 and epilogue is None and n_acc == 1 and list(out_dtypes) == [F32]

    def spec(shape, index_map, whole):
        if whole:
            return pl.BlockSpec(shape, index_map, pipeline_mode=pl.Buffered(1))
        return pl.BlockSpec(shape, index_map)

    if form == "tn":
        a_spec = spec((tk, tm), lambda i, j, k: (k, i), nk == 1 and M == tm)
    else:
        a_spec = spec((tm, tk), lambda i, j, k: (i, k), nk == 1 and M == tm)
    if form == "nt":
        b_spec = spec((tn, tk), lambda i, j, k: (j, k), nk == 1 and N == tn)
    else:
        b_spec = spec((tk, tn), lambda i, j, k: (k, j), nk == 1 and N == tn)
    in_specs, operands = [], []
    for a, b in flat:
        in_specs += [a_spec, _weight_spec(b) if isinstance(b, _Slab) else b_spec]
        operands += [a, _weight_operand(b)]
    for arr, kind, off in extras:
        if kind == "tile":
            assert off % tn == 0
            in_specs.append(pl.BlockSpec((tm, tn), functools.partial(lambda i, j, k, o: (i, j + o), o=off // tn)))
        else:
            tiles = rows_per_example // tm
            in_specs.append(pl.BlockSpec((None, 1, tn), functools.partial(lambda i, j, k, t: (i // t, 0, j), t=tiles)))
        operands.append(arr)
    n_in, n_out = len(operands), len(out_dtypes)

    def body(*refs):
        in_refs, out_refs, acc_refs = refs[:n_in], refs[n_in:n_in + n_out], refs[n_in + n_out:]
        k = pl.program_id(2)
        partials, p = [], 0
        for group in prods:
            tot = None
            for _ in group:
                d = lax.dot_general(in_refs[2 * p][...], _weight_value(in_refs[2 * p + 1], flat[p][1]), dims,
                                    preferred_element_type=F32)
                tot = d if tot is None else tot + d
                p += 1
            partials.append(tot)

        def finish(accs):
            ex = [r[...] for r in in_refs[2 * len(flat):]]
            outs = epilogue(accs, ex) if epilogue is not None else accs
            for r, o in zip(out_refs, outs):
                r[...] = o.astype(r.dtype)

        if nk == 1:
            finish(partials)
        elif direct:
            @pl.when(k == 0)
            def _():
                out_refs[0][...] = partials[0]

            @pl.when(k > 0)
            def _():
                out_refs[0][...] += partials[0]
        else:
            @pl.when(k == 0)
            def _():
                for r, v in zip(acc_refs, partials):
                    r[...] = v

            @pl.when(k > 0)
            def _():
                for r, v in zip(acc_refs, partials):
                    r[...] += v

            @pl.when(k == nk - 1)
            def _():
                finish([r[...] for r in acc_refs])

    return _pcall(
        body, name=name,
        out_shape=[jax.ShapeDtypeStruct((M, N), dt) for dt in out_dtypes],
        grid=(M // tm, N // tn, nk),
        in_specs=in_specs, operands=operands,
        out_specs=[pl.BlockSpec((tm, tn), lambda i, j, k: (i, j)) for _ in out_dtypes],
        scratch_shapes=[pltpu.VMEM((tm, tn), F32) for _ in range(n_acc)] if nk > 1 and not direct else [],
        params=_params(("parallel", "parallel", "arbitrary")), rider=rider)


def _rowwise(name, fn, T, tm, ins, outs, rows_per_example):
    tiles = rows_per_example // tm
    n_ex = T // rows_per_example
    in_specs, operands = [], []
    for arr, kind, arg in ins:
        if kind == "row":
            if arg is None:
                in_specs.append(pl.BlockSpec((tm, arr.shape[1]), lambda i: (i, 0)))
            else:
                in_specs.append(pl.BlockSpec((tm, arg[0]), functools.partial(lambda i, cb: (i, cb), cb=arg[1])))
        elif kind == "bvec":
            in_specs.append(pl.BlockSpec((None, 1, arr.shape[2]), lambda i: (i // tiles, 0, 0)))
        else:
            in_specs.append(pl.BlockSpec((1, arr.shape[1]), lambda i: (0, 0)))
        operands.append(arr)
    out_shape, out_specs = [], []
    for kind, cols, dt in outs:
        if kind == "row":
            out_shape.append(jax.ShapeDtypeStruct((T, cols), dt))
            out_specs.append(pl.BlockSpec((tm, cols), lambda i: (i, 0)))
        else:
            out_shape.append(jax.ShapeDtypeStruct((n_ex, 1, cols), F32))
            out_specs.append(pl.BlockSpec((None, 1, cols), lambda i: (i // tiles, 0, 0)))
    n_in = len(operands)

    def body(*refs):
        i = pl.program_id(0)
        vals = fn(*[r[...] for r in refs[:n_in]])
        for (kind, _, _), r, v in zip(outs, refs[n_in:], vals):
            if kind == "row":
                r[...] = v.astype(r.dtype)
            else:
                @pl.when(i % tiles == 0)
                def _():
                    r[...] = jnp.zeros_like(r)

                r[...] += v

    return pl.pallas_call(
        body, name=name, out_shape=out_shape, grid=(T // tm,), in_specs=in_specs, out_specs=out_specs,
        compiler_params=_params(("arbitrary",)),
    )(*operands)


def _colsum(v):
    return jnp.sum(v, axis=0, keepdims=True)


def _rms_parts(x):
    rstd = lax.rsqrt(jnp.mean(x * x, axis=-1, keepdims=True) + RMS_EPS)
    return x * rstd, rstd


def _resident(shape):
    return pl.BlockSpec(shape, lambda i: (0, 0), pipeline_mode=pl.Buffered(1))


class _Slab:
    def __init__(self, land, off, rows):
        assert off % rows == 0 and rows % ROW_ALIGN == 0
        self.land, self.off, self.rows = land, off, rows
        self.shape = (N_DEV * rows, land.shape[2])


def _weight_spec(w):
    if isinstance(w, _Slab):
        return pl.BlockSpec((N_DEV, w.rows, w.shape[1]), lambda *_: (0, w.off // w.rows, 0), pipeline_mode=pl.Buffered(1))
    return pl.BlockSpec(w.shape, lambda *_: (0, 0), pipeline_mode=pl.Buffered(1))


def _weight_operand(w):
    return w.land if isinstance(w, _Slab) else w


def _weight_value(ref, w):
    return ref[...].reshape(w.shape) if isinstance(w, _Slab) else ref[...]


def _example_acc(r, i, tiles, v):
    @pl.when(i % tiles == 0)
    def _():
        r[...] = jnp.zeros_like(r)

    r[...] += v


def _norm_matmul(name, x, g, sc, sh, weights, epilogue, outs, S, vecs=(), rider=None):
    T, D = x.shape
    tm = _div(S, 256, 8)
    tiles = S // tm
    nw, nv = len(weights), len(vecs)

    def body(*refs):
        x_ref, g_ref, sc_ref, sh_ref = refs[:4]
        w_refs, v_refs = refs[4:4 + nw], refs[4 + nw:4 + nw + nv]
        h_ref, out_refs = refs[4 + nw + nv], refs[5 + nw + nv:]
        xhat, _ = _rms_parts(x_ref[...])
        h = ((xhat * g_ref[...]) * (1.0 + sc_ref[...]) + sh_ref[...]).astype(BF16)
        h_ref[...] = h
        accs = [lax.dot_general(h, _weight_value(r, w), NT, preferred_element_type=F32) for r, w in zip(w_refs, weights)]
        for r, o in zip(out_refs, epilogue(accs, *[v[...] for v in v_refs])):
            r[...] = o.astype(r.dtype)

    bvec = pl.BlockSpec((None, 1, D), lambda i: (i // tiles, 0, 0))
    return _pcall(
        body, name=name,
        out_shape=[jax.ShapeDtypeStruct((T, D), BF16)] + [jax.ShapeDtypeStruct((T, w), dt) for w, dt in outs],
        grid=(T // tm,),
        in_specs=[pl.BlockSpec((tm, D), lambda i: (i, 0)), pl.BlockSpec((1, D), lambda i: (0, 0)), bvec, bvec]
        + [_weight_spec(w) for w in weights] + [pl.BlockSpec(v.shape, lambda i: (0, 0)) for v in vecs],
        operands=[x, g, sc, sh, *[_weight_operand(w) for w in weights], *vecs],
        out_specs=[pl.BlockSpec((tm, D), lambda i: (i, 0))] + [pl.BlockSpec((tm, w), lambda i: (i, 0)) for w, _ in outs],
        scratch_shapes=[], params=_params(("arbitrary",)), rider=rider)


def _gated_grad_matmul(name, dx, y, gt, coeff, w, tiles_in, epilogue, outs, S, rider=None):
    T, D = dx.shape
    N = w.shape[0]
    tm = _div(S, 256, 8)
    tiles = S // tm
    nt = len(tiles_in)

    def body(*refs):
        dx_ref, y_ref, gt_ref, w_ref = refs[:4]
        t_refs, dy_ref, dgt_ref, out_refs = refs[4:4 + nt], refs[4 + nt], refs[5 + nt], refs[6 + nt:]
        i = pl.program_id(0)
        dxv = dx_ref[...]
        dy = (coeff * gt_ref[...] * dxv).astype(BF16)
        dy_ref[...] = dy
        _example_acc(dgt_ref, i, tiles, _colsum(coeff * dxv * y_ref[...].astype(F32)))
        acc = lax.dot_general(dy, _weight_value(w_ref, w), NT, preferred_element_type=F32)
        for r, o in zip(out_refs, epilogue(acc, [t[...] for t in t_refs])):
            r[...] = o.astype(r.dtype)

    row = pl.BlockSpec((tm, D), lambda i: (i, 0))
    bvec = pl.BlockSpec((None, 1, D), lambda i: (i // tiles, 0, 0))
    return _pcall(
        body, name=name,
        out_shape=[jax.ShapeDtypeStruct((T, D), BF16), jax.ShapeDtypeStruct((T // S, 1, D), F32)]
        + [jax.ShapeDtypeStruct((T, N), dt) for dt in outs],
        grid=(T // tm,),
        in_specs=[row, row, bvec, _weight_spec(w)]
        + [pl.BlockSpec((tm, N), functools.partial(lambda i, cb: (i, cb), cb=cb)) for _, cb in tiles_in],
        operands=[dx, y, gt, _weight_operand(w), *[t for t, _ in tiles_in]],
        out_specs=[row, bvec] + [pl.BlockSpec((tm, N), lambda i: (i, 0)) for _ in outs],
        scratch_shapes=[], params=_params(("arbitrary",)), rider=rider)


def _matmul_normmod_bwd(name, prods, x, g, sc, dres, S, rider=None):
    T, D = x.shape
    tm = _div(S, 512 if sum(a.shape[1] for a, _ in prods) <= 6 * D else 256, 8)
    tiles = S // tm
    npr = len(prods)

    def body(*refs):
        ab = refs[:2 * npr]
        x_ref, g_ref, sc_ref, dr_ref = refs[2 * npr:2 * npr + 4]
        dx_ref, dsh_ref, dsc_ref, dg_ref = refs[2 * npr + 4:]
        i = pl.program_id(0)
        dh = None
        for p in range(npr):
            d = lax.dot_general(ab[2 * p][...], _weight_value(ab[2 * p + 1], prods[p][1]), NN, preferred_element_type=F32)
            dh = d if dh is None else dh + d
        xhat, rstd = _rms_parts(x_ref[...])
        gv = g_ref[...]
        dn = dh * (1.0 + sc_ref[...])
        dxh = dn * gv
        dx_ref[...] = dr_ref[...] + rstd * (dxh - xhat * jnp.mean(dxh * xhat, axis=-1, keepdims=True))
        _example_acc(dsh_ref, i, tiles, _colsum(dh))
        _example_acc(dsc_ref, i, tiles, _colsum(dh * (xhat * gv)))
        _example_acc(dg_ref, i, tiles, _colsum(dn * xhat))

    row = pl.BlockSpec((tm, D), lambda i: (i, 0))
    bvec = pl.BlockSpec((None, 1, D), lambda i: (i // tiles, 0, 0))
    in_specs, operands = [], []
    for a, b in prods:
        in_specs += [pl.BlockSpec((tm, a.shape[1]), lambda i: (i, 0)), _weight_spec(b)]
        operands += [a, _weight_operand(b)]
    acc_shape = jax.ShapeDtypeStruct((T // S, 1, D), F32)
    return _pcall(
        body, name=name,
        out_shape=[jax.ShapeDtypeStruct((T, D), F32), acc_shape, acc_shape, acc_shape],
        grid=(T // tm,),
        in_specs=in_specs + [row, pl.BlockSpec((1, D), lambda i: (0, 0)), bvec, row],
        operands=[*operands, x, g, sc, dres],
        out_specs=[row, bvec, bvec, bvec],
        scratch_shapes=[], params=_params(("arbitrary",)), rider=rider)


def _down_loss(name, s, wd, x, gt, tgt, g_final, S):
    T, D = x.shape
    tm = _div(S, 512, 8)
    tiles = S // tm

    def body(s_ref, w_ref, x_ref, gt_ref, t_ref, g_ref, y_ref, dx_ref, loss_ref, dg_ref):
        i = pl.program_id(0)
        acc = lax.dot_general(s_ref[...], _weight_value(w_ref, wd), NN, preferred_element_type=F32)
        y_ref[...] = acc.astype(y_ref.dtype)
        xhat, rstd = _rms_parts(x_ref[...] + 0.5 * gt_ref[...] * acc)
        gv = g_ref[...]
        e = xhat * gv - t_ref[...]
        _example_acc(loss_ref, i, tiles, jnp.broadcast_to(0.5 / D * jnp.sum(_colsum(e * e), axis=1, keepdims=True), (1, LANES)))
        dy = e * (1.0 / D)
        dxh = dy * gv
        dx_ref[...] = rstd * (dxh - xhat * jnp.mean(dxh * xhat, axis=-1, keepdims=True))
        _example_acc(dg_ref, i, tiles, _colsum(dy * xhat))

    row = pl.BlockSpec((tm, D), lambda i: (i, 0))
    return pl.pallas_call(
        body, name=name,
        out_shape=[jax.ShapeDtypeStruct((T, D), BF16), jax.ShapeDtypeStruct((T, D), F32),
                   jax.ShapeDtypeStruct((T // S, 1, LANES), F32), jax.ShapeDtypeStruct((T // S, 1, D), F32)],
        grid=(T // tm,),
        in_specs=[pl.BlockSpec((tm, s.shape[1]), lambda i: (i, 0)), _weight_spec(wd), row,
                  pl.BlockSpec((None, 1, D), lambda i: (i // tiles, 0, 0)), row, pl.BlockSpec((1, D), lambda i: (0, 0))],
        out_specs=[row, row, pl.BlockSpec((None, 1, LANES), lambda i: (i // tiles, 0, 0)),
                   pl.BlockSpec((None, 1, D), lambda i: (i // tiles, 0, 0))],
        compiler_params=_params(("arbitrary",)),
    )(s, _weight_operand(wd), x, gt, tgt, g_final)


def _ffn_forward(tag, x, g, sh, sc, gt, wgT, wuT, wd, S, gather=None, loss=None):
    T, D = x.shape
    F = wd.shape[0]

    def gateup(accs):
        a, u = accs
        return [a, u, a * _sigmoid(a) * u]

    h, a, u, s, *land = _norm_matmul(f"{tag}_gateup", x, g, sc, sh, [wgT, wuT], gateup, [(F, BF16)] * 3, S,
                                     rider=None if gather is None else _gather_direct_rider(gather))

    def down(accs, ex):
        xv, gtv = ex
        return [xv + 0.5 * gtv * accs[0], accs[0]]

    tmd = _div(S, 512, 8)
    if loss is not None:
        y, dx_new, loss_b, dg_final = _down_loss(f"{tag}_down_loss", s, wd, x, gt, loss[0], loss[1], S)
        return (dx_new, loss_b, dg_final), (x, h, a, u, s, y), None
    x_new, y, *land = _matmul(f"{tag}_down", "nn", [[(s, wd)]], T, D, F, tmd, D, F, [F32, BF16],
                              extras=[(x, "tile", 0), (gt, "brow", 0)], epilogue=down, rows_per_example=S,
                              rider=None if gather is None else _gather_forward_rider(land[0]))
    return x_new, (x, h, a, u, s, y), (land[0] if land else None)


def _ffn_backward(tag, dx_out, saved, g, sc, gt, wgT, wuT, wd, S, rider=None, dh_rider=None):
    x, h, a, u, s, y = saved
    T, D = x.shape
    F = wd.shape[0]

    def act_grad(ds, ex):
        av, uv = ex[0].astype(F32), ex[1].astype(F32)
        sg = _sigmoid(av)
        return [ds * uv * (sg * (1.0 + av * (1.0 - sg))), ds * (av * sg)]

    dy, dgt, da, du, *rode = _gated_grad_matmul(f"{tag}_act_grad", dx_out, y, gt, 0.5, wd, [(a, 0), (u, 0)], act_grad,
                                                [BF16, BF16], S, rider=rider)
    tkw = _div(T, 2048, LANES)
    tmw = _div(F, 1408, LANES)
    dwd = _matmul(f"{tag}_dw_down", "tn", [[(s, dy)]], F, D, T, tmw, D, tkw, [F32])[0]
    dwgT = _matmul(f"{tag}_dw_gate", "tn", [[(da, h)]], F, D, T, tmw, D, tkw, [F32])[0]
    dwuT = _matmul(f"{tag}_dw_up", "tn", [[(du, h)]], F, D, T, tmw, D, tkw, [F32])[0]
    dx_in, dsh, dsc, dg, *rode_dh = _matmul_normmod_bwd(
        f"{tag}_dh", [(da, wgT), (du, wuT)], x, g, sc, dx_out, S,
        rider=None if dh_rider is None else dh_rider(dwgT, dwuT, dwd))
    return dx_in, (dsh, dsc, dgt, dg), (dwgT, dwuT, dwd), rode + rode_dh


def _cumsum(v):
    B, S, _ = v.shape
    rows = _div(S, 1024, BLOCK)

    def body(x_ref, o_ref, carry):
        i = pl.program_id(1)

        @pl.when(i == 0)
        def _():
            carry[...] = jnp.zeros_like(carry)

        r = lax.broadcasted_iota(jnp.int32, (BLOCK, BLOCK), 0)
        c = lax.broadcasted_iota(jnp.int32, (BLOCK, BLOCK), 1)
        tri = (c <= r).astype(F32)
        last = carry[0:1, :]
        for j in range(0, rows, BLOCK):
            cum = jnp.dot(tri, x_ref[j:j + BLOCK, :], precision=lax.Precision.HIGHEST, preferred_element_type=F32) + last
            o_ref[j:j + BLOCK, :] = cum
            last = cum[BLOCK - 1:BLOCK, :]
        carry[...] = jnp.broadcast_to(last, carry.shape)

    return pl.pallas_call(
        body, name="cumsum", out_shape=jax.ShapeDtypeStruct(v.shape, F32), grid=(B, S // rows),
        in_specs=[pl.BlockSpec((None, rows, LANES), lambda b, i: (b, i, 0))],
        out_specs=pl.BlockSpec((None, rows, LANES), lambda b, i: (b, i, 0)),
        scratch_shapes=[pltpu.VMEM((8, LANES), F32)],
        compiler_params=_params(("arbitrary", "arbitrary")),
    )(v)


def _with_ones(x):
    lane = lax.broadcasted_iota(jnp.int32, (x.shape[0], HEAD_DIM), 1)
    return jnp.concatenate([x, jnp.where(lane == 0, 1.0, 0.0).astype(x.dtype)], axis=1)


def _causal_strip(s, r):
    qpos = r + lax.broadcasted_iota(jnp.int32, s.shape, 0)
    kpos = lax.broadcasted_iota(jnp.int32, s.shape, 1)
    return jnp.where(kpos <= qpos, s, NEG_INF)


NT = (((1,), (1,)), ((), ()))
NN = (((1,), (0,)), ((), ()))
TN = (((0,), (0,)), ((), ()))


def _fox_fwd(pm3, cumT, qcol, kcol, vcol, tq, rider=None):
    B, S, _ = pm3.shape
    nq = S // tq
    strips = range(0, tq, FOX_STRIP)

    def body(q_ref, k_ref, v_ref, ck_ref, o_ref, o32_ref, lse_ref, s_sc, p_sc, al_sc, m_sc, acc_sc):
        qi, kj = pl.program_id(1), pl.program_id(2)

        @pl.when(kj == 0)
        def _():
            m_sc[...] = jnp.full_like(m_sc, NEG_INF)
            acc_sc[...] = jnp.zeros_like(acc_sc)

        def tile(diagonal):
            def scores(h):
                hs = slice(HEAD_DIM * h, HEAD_DIM * (h + 1))
                s_sc[h % 2] = lax.dot_general(q_ref[:, hs] * SCALE, k_ref[:, hs], NT, preferred_element_type=F32)

            def accumulate(h):
                hs = slice(HEAD_DIM * h, HEAD_DIM * (h + 1))
                acc_sc[h] = al_sc[h % 2] * acc_sc[h] + lax.dot_general(p_sc[h % 2], _with_ones(v_ref[:, hs]), NN,
                                                                       preferred_element_type=F32)

            scores(0)
            for h in range(FOX_HEADS):
                b = h % 2
                if h + 1 < FOX_HEADS:
                    scores(h + 1)
                if h >= 1:
                    accumulate(h - 1)
                ck = ck_ref[h:h + 1, :]
                for r in strips:
                    rows = slice(r, r + FOX_STRIP)
                    s = s_sc[b, rows, :] - ck
                    if diagonal:
                        s = _causal_strip(s, r)
                    m_prev = m_sc[h, rows, :]
                    m_new = jnp.maximum(m_prev, jnp.max(s, axis=-1, keepdims=True))
                    p_sc[b, rows, :] = jnp.exp(s - m_new).astype(BF16)
                    al_sc[b, rows, :] = jnp.exp(m_prev - m_new)
                    m_sc[h, rows, :] = m_new
            accumulate(FOX_HEADS - 1)

        @pl.when(kj < qi)
        def _():
            tile(False)

        @pl.when(kj == qi)
        def _():
            tile(True)

        @pl.when(kj == nq - 1)
        def _():
            lse_ref[...] = jnp.zeros_like(lse_ref)
            for h in range(FOX_HEADS):
                hs = slice(HEAD_DIM * h, HEAD_DIM * (h + 1))
                acc = acc_sc[h]
                l = acc[:, HEAD_DIM:HEAD_DIM + 1]
                oh = acc[:, :HEAD_DIM] / l
                o_ref[:, hs] = oh.astype(o_ref.dtype)
                o32_ref[:, hs] = oh
                lse_ref[:, h:h + 1] = m_sc[h] + jnp.log(l)

    ospec = pl.BlockSpec((None, tq, FOX_W), lambda b, i, j: (b, i, 0))
    return _pcall(
        body, name="fox_forward",
        out_shape=[jax.ShapeDtypeStruct((B, S, FOX_W), BF16), jax.ShapeDtypeStruct((B, S, FOX_W), F32),
                   jax.ShapeDtypeStruct((B, S, LANES), F32)],
        grid=(B, nq, nq),
        in_specs=[pl.BlockSpec((None, tq, FOX_W), lambda b, i, j: (b, i, qcol)),
                  pl.BlockSpec((None, tq, FOX_W), lambda b, i, j: (b, jnp.minimum(i, j), kcol)),
                  pl.BlockSpec((None, tq, FOX_W), lambda b, i, j: (b, jnp.minimum(i, j), vcol)),
                  pl.BlockSpec((None, 8, tq), lambda b, i, j: (b, 0, jnp.minimum(i, j)))],
        operands=[pm3, pm3, pm3, cumT],
        out_specs=[ospec, ospec, pl.BlockSpec((None, tq, LANES), lambda b, i, j: (b, i, 0))],
        scratch_shapes=[pltpu.VMEM((2, tq, tq), F32), pltpu.VMEM((2, tq, tq), BF16), pltpu.VMEM((2, tq, 1), F32),
                        pltpu.VMEM((FOX_HEADS, tq, 1), F32), pltpu.VMEM((FOX_HEADS, tq, LANES), F32)],
        params=_params(("parallel", "parallel", "arbitrary")), rider=rider)


def _fox_bwd(pm3, do, delta, lse, cumT, qcol, kcol, vcol, tq, rider=None):
    B, S, _ = pm3.shape
    nq = S // tq
    strips = range(0, tq, FOX_STRIP)

    def body(q_ref, k_ref, v_ref, do_ref, dl_ref, lse_ref, ck_ref, dq_ref, rs_ref, dk_ref, dv_ref, cs_ref,
             s_sc, dp_sc, p_sc, ds_sc, dq_sc, dk_sc, dv_sc):
        kj, qi = pl.program_id(1), pl.program_id(2)

        @pl.when((kj == 0) & (qi == 0))
        def _():
            dq_sc[...] = jnp.zeros_like(dq_sc)

        @pl.when(qi == 0)
        def _():
            dk_sc[...] = jnp.zeros_like(dk_sc)
            dv_sc[...] = jnp.zeros_like(dv_sc)

        def tile(diagonal):
            qrows = pl.ds(pl.multiple_of(qi * tq, tq), tq)
            for h in range(FOX_HEADS):
                hs = slice(HEAD_DIM * h, HEAD_DIM * (h + 1))
                qh, kh, doh = q_ref[:, hs] * SCALE, k_ref[:, hs], do_ref[:, hs]
                b = h % 2
                s_sc[b] = lax.dot_general(qh, kh, NT, preferred_element_type=F32)
                dp_sc[b] = lax.dot_general(doh, v_ref[:, hs], NT, preferred_element_type=F32)
                ck = ck_ref[h:h + 1, :]
                for r in strips:
                    rows = slice(r, r + FOX_STRIP)
                    s = s_sc[b, rows, :] - ck
                    if diagonal:
                        s = _causal_strip(s, r)
                    p = jnp.exp(s - lse_ref[rows, h:h + 1])
                    p_sc[b, rows, :] = p.astype(BF16)
                    ds_sc[b, rows, :] = (p * (dp_sc[b, rows, :] - dl_ref[rows, h:h + 1])).astype(BF16)
                dv_sc[h] += lax.dot_general(doh, p_sc[b], TN, preferred_element_type=F32)
                dk_sc[h] += lax.dot_general(_with_ones(qh), ds_sc[b], TN, preferred_element_type=F32)
                dq_sc[h, qrows, :] += lax.dot_general(ds_sc[b], _with_ones(kh), NN, preferred_element_type=F32)

        @pl.when(qi > kj)
        def _():
            tile(False)

        @pl.when(qi == kj)
        def _():
            tile(True)

        @pl.when(qi == nq - 1)
        def _():
            cs_ref[...] = jnp.zeros_like(cs_ref)
            for h in range(FOX_HEADS):
                hs = slice(HEAD_DIM * h, HEAD_DIM * (h + 1))
                dv_ref[:, hs] = dv_sc[h].T.astype(dv_ref.dtype)
                dk = dk_sc[h].T
                dk_ref[:, hs] = dk[:, :HEAD_DIM].astype(dk_ref.dtype)
                cs_ref[:, h:h + 1] = dk[:, HEAD_DIM:HEAD_DIM + 1]

        @pl.when((kj == nq - 1) & (qi == nq - 1))
        def _():
            rs_ref[...] = jnp.zeros_like(rs_ref)
            for h in range(FOX_HEADS):
                hs = slice(HEAD_DIM * h, HEAD_DIM * (h + 1))
                dq_ref[:, hs] = (dq_sc[h, :, :HEAD_DIM] * SCALE).astype(dq_ref.dtype)
                rs_ref[:, h:h + 1] = dq_sc[h, :, HEAD_DIM:HEAD_DIM + 1]

    def qside(width, col=0):
        return pl.BlockSpec((None, tq, width), lambda b, j, i: (b, jnp.maximum(i, j), col))

    kspec = pl.BlockSpec((None, tq, FOX_W), lambda b, j, i: (b, j, 0))
    return _pcall(
        body, name="fox_backward",
        out_shape=[jax.ShapeDtypeStruct((B, S, FOX_W), BF16), jax.ShapeDtypeStruct((B, S, LANES), F32),
                   jax.ShapeDtypeStruct((B, S, FOX_W), BF16), jax.ShapeDtypeStruct((B, S, FOX_W), BF16),
                   jax.ShapeDtypeStruct((B, S, LANES), F32)],
        grid=(B, nq, nq),
        in_specs=[qside(FOX_W, qcol),
                  pl.BlockSpec((None, tq, FOX_W), lambda b, j, i: (b, j, kcol)),
                  pl.BlockSpec((None, tq, FOX_W), lambda b, j, i: (b, j, vcol)),
                  qside(FOX_W), qside(LANES), qside(LANES),
                  pl.BlockSpec((None, 8, tq), lambda b, j, i: (b, 0, j))],
        operands=[pm3, pm3, pm3, do, delta, lse, cumT],
        out_specs=[pl.BlockSpec((None, S, FOX_W), lambda b, j, i: (b, 0, 0)),
                   pl.BlockSpec((None, S, LANES), lambda b, j, i: (b, 0, 0)),
                   kspec, kspec, pl.BlockSpec((None, tq, LANES), lambda b, j, i: (b, j, 0))],
        scratch_shapes=[pltpu.VMEM((2, tq, tq), F32), pltpu.VMEM((2, tq, tq), F32), pltpu.VMEM((2, tq, tq), BF16),
                        pltpu.VMEM((2, tq, tq), BF16), pltpu.VMEM((FOX_HEADS, S, LANES), F32),
                        pltpu.VMEM((FOX_HEADS, LANES, tq), F32), pltpu.VMEM((FOX_HEADS, HEAD_DIM, tq), F32)],
        params=_params(("parallel", "arbitrary", "arbitrary")), rider=rider)


def _fox_delta(do, o32, T, S):
    def fn(dov, ov):
        prod = dov.astype(F32) * ov
        lane = lax.broadcasted_iota(jnp.int32, (dov.shape[0], LANES), 1)
        delta = jnp.zeros((dov.shape[0], LANES), F32)
        for h in range(FOX_HEADS):
            hs = slice(HEAD_DIM * h, HEAD_DIM * (h + 1))
            delta = jnp.where(lane == h, jnp.sum(prod[:, hs], axis=-1, keepdims=True), delta)
        return [delta]

    return _rowwise("fox_delta", fn, T, _div(S, 512, 8), [(do, "row", None), (o32, "row", None)], [("row", LANES, F32)], S)[0]


def _alibi_slope(group, head):
    return 2.0 ** (-ALIBI_MAX_BIAS * (group * DIL_HPG + head + 1) / (N_DIL * DIL_HPG))


def _residue_order(a, B, S, d):
    C = a.shape[-1]
    if d == 1:
        return a.reshape(B, S, C)
    return a.reshape(B, S // d, d, C).transpose(0, 2, 1, 3).reshape(B * d, S // d, C)


def _token_order(a, B, S, d):
    C = a.shape[-1]
    if d == 1:
        return a.reshape(B * S, C)
    return a.reshape(B, d, S // d, C).transpose(0, 2, 1, 3).reshape(B * S, C)


def _band_scores(qh, kcat, slope_d, has_prev):
    qi = lax.broadcasted_iota(jnp.int32, (BLOCK, 2 * BLOCK), 0)
    c = lax.broadcasted_iota(jnp.int32, (BLOCK, 2 * BLOCK), 1)
    s = lax.dot_general(qh, kcat, NT, preferred_element_type=F32) - slope_d * (BLOCK + qi - c).astype(F32)
    valid = (c >= qi) & (c <= qi + BLOCK)
    if has_prev is not None:
        valid = valid & ((c >= BLOCK) | has_prev)
    return jnp.where(valid, s, NEG_INF)


def _band_operands(j, cur_ref, prev_ref, hs):
    if j == 0:
        return jnp.concatenate([prev_ref[:, hs], cur_ref[0:BLOCK, hs]], axis=0)
    return cur_ref[(j - 1) * BLOCK:(j + 1) * BLOCK, hs]


def _dil_specs(Ls, qb, cols):
    nsub = qb // BLOCK
    qcol, kcol, vcol = cols

    def cur(col):
        return pl.BlockSpec((None, qb, DIL_GW), lambda s, n: (s, n, col))

    def prev(col):
        return pl.BlockSpec((None, BLOCK, DIL_GW), lambda s, n: (s, jnp.maximum(n * nsub - 1, 0), col))

    return [cur(qcol), cur(kcol), prev(kcol), cur(vcol), prev(vcol)]


def _dil_fwd(group, src, cols):
    _, dilation = DIL_GROUPS[group]
    nseq, Ls, _ = src.shape
    qb = _div(Ls, 512, BLOCK)
    nsub = qb // BLOCK

    def body(q_ref, kc_ref, kp_ref, vc_ref, vp_ref, o_ref, lse_ref):
        has_prev = pl.program_id(1) > 0
        lse_ref[...] = jnp.zeros_like(lse_ref)
        for h in range(DIL_HPG):
            hs = slice(HEAD_DIM * h, HEAD_DIM * (h + 1))
            scores = [_band_scores(q_ref[j * BLOCK:(j + 1) * BLOCK, hs] * SCALE, _band_operands(j, kc_ref, kp_ref, hs),
                                   _alibi_slope(group, h) * dilation, has_prev if j == 0 else None) for j in range(nsub)]
            pending = None

            def write(j, m, acc):
                rows = slice(j * BLOCK, (j + 1) * BLOCK)
                l = acc[:, HEAD_DIM:HEAD_DIM + 1]
                o_ref[rows, hs] = acc[:, :HEAD_DIM] / l
                lse_ref[rows, h:h + 1] = m + jnp.log(l)

            for j in range(nsub):
                m = jnp.max(scores[j], axis=-1, keepdims=True)
                p = jnp.exp(scores[j] - m).astype(BF16)
                acc = lax.dot_general(p, _with_ones(_band_operands(j, vc_ref, vp_ref, hs)), NN, preferred_element_type=F32)
                if pending is not None:
                    write(*pending)
                pending = (j, m, acc)
            write(*pending)

    return pl.pallas_call(
        body, name=f"dil_forward_{group}",
        out_shape=[jax.ShapeDtypeStruct((nseq, Ls, DIL_GW), F32), jax.ShapeDtypeStruct((nseq, Ls, LANES), F32)],
        grid=(nseq, Ls // qb),
        in_specs=_dil_specs(Ls, qb, cols),
        out_specs=[pl.BlockSpec((None, qb, DIL_GW), lambda s, n: (s, n, 0)),
                   pl.BlockSpec((None, qb, LANES), lambda s, n: (s, n, 0))],
        compiler_params=_params(("parallel", "arbitrary")),
    )(src, src, src, src, src)


def _dil_bwd(group, src, cols, Lr, dyr, dlr):
    _, dilation = DIL_GROUPS[group]
    nseq, Ls, _ = src.shape
    qb = _div(Ls, 512, BLOCK)
    nsub, nb = qb // BLOCK, Ls // qb

    def body(q_ref, kc_ref, kp_ref, vc_ref, vp_ref, L_ref, dy_ref, dl_ref, dq_ref, dk_ref, dv_ref, dk_sc, dv_sc):
        n = pl.program_id(1)
        has_prev = n > 0

        @pl.when(n == 0)
        def _():
            dk_sc[...] = jnp.zeros_like(dk_sc)
            dv_sc[...] = jnp.zeros_like(dv_sc)

        base = pl.multiple_of(n * qb, BLOCK)
        for h in range(DIL_HPG):
            hs = slice(HEAD_DIM * h, HEAD_DIM * (h + 1))
            blocks = [slice(j * BLOCK, (j + 1) * BLOCK) for j in range(nsub)]
            qhs = [q_ref[rows, hs] * SCALE for rows in blocks]
            kcats = [_band_operands(j, kc_ref, kp_ref, hs) for j in range(nsub)]
            dyhs = [dy_ref[rows, hs] for rows in blocks]
            scores = [_band_scores(qhs[j], kcats[j], _alibi_slope(group, h) * dilation, has_prev if j == 0 else None)
                      for j in range(nsub)]
            dps = [lax.dot_general(dyhs[j], _band_operands(j, vc_ref, vp_ref, hs), NT, preferred_element_type=F32)
                   for j in range(nsub)]
            pending = None

            def write(j, dq, dk, dv):
                dq_ref[blocks[j], hs] = (dq * SCALE).astype(dq_ref.dtype)
                win = pl.ds(base + j * BLOCK, 2 * BLOCK)
                dk_sc[win, hs] += dk
                dv_sc[win, hs] += dv

            for j in range(nsub):
                p = jnp.exp(scores[j] - L_ref[blocks[j], h:h + 1])
                ds = (p * (dps[j] - dl_ref[blocks[j], h:h + 1])).astype(BF16)
                dq = lax.dot_general(ds, kcats[j], NN, preferred_element_type=F32)
                dk = lax.dot_general(ds, qhs[j], TN, preferred_element_type=F32)
                dv = lax.dot_general(p.astype(BF16), dyhs[j], TN, preferred_element_type=F32)
                if pending is not None:
                    write(*pending)
                pending = (j, dq, dk, dv)
            write(*pending)

        @pl.when(n == nb - 1)
        def _():
            dk_ref[...] = dk_sc[BLOCK:, :].astype(dk_ref.dtype)
            dv_ref[...] = dv_sc[BLOCK:, :].astype(dv_ref.dtype)

    own = pl.BlockSpec((None, qb, DIL_GW), lambda s, n: (s, n, 0))
    own128 = pl.BlockSpec((None, qb, LANES), lambda s, n: (s, n, 0))
    whole = pl.BlockSpec((None, Ls, DIL_GW), lambda s, n: (s, 0, 0))
    shape = jax.ShapeDtypeStruct((nseq, Ls, DIL_GW), BF16)
    return pl.pallas_call(
        body, name=f"dil_backward_{group}",
        out_shape=[shape, shape, shape],
        grid=(nseq, nb),
        in_specs=_dil_specs(Ls, qb, cols) + [own128, own, own128],
        out_specs=[own, whole, whole],
        scratch_shapes=[pltpu.VMEM((Ls + BLOCK, DIL_GW), F32), pltpu.VMEM((Ls + BLOCK, DIL_GW), F32)],
        compiler_params=_params(("parallel", "arbitrary")),
    )(src, src, src, src, src, Lr, dyr, dlr)


def _dil_combine(os_, lses, T, S):
    def fn(o0, o1, o2, l0, l1, l2):
        m = jnp.maximum(jnp.maximum(l0, l1), l2)
        e0, e1, e2 = jnp.exp(l0 - m), jnp.exp(l1 - m), jnp.exp(l2 - m)
        tot = e0 + e1 + e2
        w0, w1, w2 = e0 / tot, e1 / tot, e2 / tot
        parts = []
        for h in range(DIL_HPG):
            hs = slice(HEAD_DIM * h, HEAD_DIM * (h + 1))
            parts.append(w0[:, h:h + 1] * o0[:, hs] + w1[:, h:h + 1] * o1[:, hs] + w2[:, h:h + 1] * o2[:, hs])
        return [jnp.concatenate(parts, axis=1), m + jnp.log(tot)]

    ins = [(a, "row", None) for a in os_] + [(a, "row", None) for a in lses]
    return _rowwise("dil_combine", fn, T, _div(S, 512, 8), ins, [("row", DIL_GW, BF16), ("row", LANES, F32)], S)


def _dil_delta(dy, y, T, S):
    def fn(dyv, yv):
        prod = dyv * yv.astype(F32)
        lane = lax.broadcasted_iota(jnp.int32, (dyv.shape[0], LANES), 1)
        delta = jnp.zeros((dyv.shape[0], LANES), F32)
        for h in range(DIL_HPG):
            hs = slice(HEAD_DIM * h, HEAD_DIM * (h + 1))
            delta = jnp.where(lane == h, jnp.sum(prod[:, hs], axis=-1, keepdims=True), delta)
        return [delta, dyv]

    return _rowwise("dil_delta", fn, T, _div(S, 512, 8), [(dy, "row", None), (y, "row", None)],
                    [("row", LANES, F32), ("row", DIL_GW, BF16)], S)


def _ada_forward(c_all, w, b):
    n, D = c_all.shape
    cl = w.shape[1]

    def body(c_ref, w_ref, b_ref, o_ref, ca_ref):
        cv = c_ref[...]
        ca = (cv * _sigmoid(cv)).astype(BF16)
        ca_ref[...] = ca
        o_ref[...] = jnp.dot(ca, w_ref[...].astype(BF16), preferred_element_type=F32) + b_ref[...]

    return pl.pallas_call(
        body, name="ada_forward",
        out_shape=[jax.ShapeDtypeStruct((n, cl), F32), jax.ShapeDtypeStruct((n, D), BF16)],
        compiler_params=_params(),
    )(c_all, w, b)


def _ada_backward(ca, dmod_cols, dmod_all):
    n, D = ca.shape
    cl = dmod_cols.shape[1]

    def body(ca_ref, dc_ref, da_ref, gw_ref, gb_ref):
        gw_ref[...] = lax.dot_general(ca_ref[...], dc_ref[...].astype(BF16), (((0,), (0,)), ((), ())), preferred_element_type=F32)
        gb_ref[...] = _colsum(da_ref[...])

    return pl.pallas_call(
        body, name="ada_backward",
        out_shape=[jax.ShapeDtypeStruct((D, cl), F32), jax.ShapeDtypeStruct((1, dmod_all.shape[1]), F32)],
        compiler_params=_params(),
    )(ca, dmod_cols, dmod_all)


def _sum_devices(v):
    def body(v_ref, o_ref):
        tot = v_ref[0]
        for k in range(1, N_DEV):
            tot = tot + v_ref[k]
        o_ref[...] = tot

    return pl.pallas_call(body, name="sum_devices", out_shape=jax.ShapeDtypeStruct(v.shape[1:], F32))(v)


def _adamw(name, w, g, m, v):
    rows, cols = w.shape
    tr = _div(rows, 256, 8)

    def body(w_ref, g_ref, m_ref, v_ref, d_ref, nm_ref, nv_ref):
        gv = g_ref[...]
        nm = ADAM_B1 * m_ref[...] + (1.0 - ADAM_B1) * gv
        nv = ADAM_B2 * v_ref[...] + (1.0 - ADAM_B2) * (gv * gv)
        m_hat = nm / (1.0 - ADAM_B1 ** ADAM_STEP)
        v_hat = nv / (1.0 - ADAM_B2 ** ADAM_STEP)
        d_ref[...] = -ADAM_LR * (m_hat / (jnp.sqrt(v_hat) + ADAM_EPS) + ADAM_WD * w_ref[...])
        nm_ref[...] = nm
        nv_ref[...] = nv

    spec = pl.BlockSpec((tr, cols), lambda i: (i, 0))
    shape = jax.ShapeDtypeStruct((rows, cols), F32)
    return pl.pallas_call(
        body, name=name, out_shape=[shape, shape, shape], grid=(rows // tr,),
        in_specs=[spec, spec, spec, spec], out_specs=[spec, spec, spec],
        compiler_params=_params(("arbitrary",)),
    )(w, g, m, v)


def _pad_rows(a, rows):
    return a if a.shape[0] == rows else jnp.pad(a, ((0, rows - a.shape[0]), (0, 0)))


class _Packed:
    def __init__(self, kind, local_shape, D):
        self.kind, self.local_shape, self.D = kind, local_shape, D
        r, c = local_shape
        self.rows = {"T": c, "N": r, "F": r * c // D}[kind]
        self.rows_pad = -(-self.rows // ROW_ALIGN) * ROW_ALIGN

    def pack_local(self, w):
        if self.kind == "T":
            w = w.T
        elif self.kind == "F":
            w = w.reshape(self.rows, self.D)
        return _pad_rows(w, self.rows_pad)

    def full(self, gathered):
        g = gathered[:, :self.rows]
        if self.kind == "F":
            r, c = self.local_shape
            return g.reshape(N_DEV, r, c).transpose(1, 0, 2).reshape(r, N_DEV * c)
        return g.reshape(N_DEV * self.rows, self.D)

    def pack_grad(self, gfull):
        if self.kind == "F":
            r, c = self.local_shape
            g = gfull.reshape(r, N_DEV, c).transpose(1, 0, 2).reshape(N_DEV, self.rows, self.D)
        else:
            g = gfull.reshape(N_DEV, self.rows, self.D)
        if self.rows_pad != self.rows:
            g = jnp.pad(g, ((0, 0), (0, self.rows_pad - self.rows), (0, 0)))
        return g

    def unpack_local(self, g):
        g = g[:self.rows]
        if self.kind == "T":
            return g.T
        if self.kind == "F":
            return g.reshape(self.local_shape)
        return g


BIG = ["ffn1_w_gate", "ffn1_w_up", "ffn1_w_down", "w_in", "w_branch_a", "w_branch_b", "w_out",
       "ffn2_w_gate", "ffn2_w_up", "ffn2_w_down"]
BIG_KIND = {"ffn1_w_gate": "T", "ffn1_w_up": "T", "ffn1_w_down": "N", "w_in": "T", "w_branch_a": "F", "w_branch_b": "F",
            "w_out": "N", "ffn2_w_gate": "T", "ffn2_w_up": "T", "ffn2_w_down": "N"}
GROUPS = (("ffn1_w_gate", "ffn1_w_up", "ffn1_w_down"), ("w_in", "w_branch_a", "w_branch_b", "w_out"),
          ("ffn2_w_gate", "ffn2_w_up", "ffn2_w_down"))
SMALL = ["ada_b", "norm_ffn1", "norm_mix", "forget_bias", "norm_ffn2", "norm_final"]


def kernel(x, c, ada_w, ada_b, norm_ffn1, ffn1_w_gate, ffn1_w_up, ffn1_w_down, norm_mix, w_in, forget_bias, w_branch_a, w_branch_b, w_out, norm_ffn2, ffn2_w_gate, ffn2_w_up, ffn2_w_down, norm_final, loss_target, m_ada_w, m_ada_b, m_norm_ffn1, m_ffn1_w_gate, m_ffn1_w_up, m_ffn1_w_down, m_norm_mix, m_w_in, m_forget_bias, m_w_branch_a, m_w_branch_b, m_w_out, m_norm_ffn2, m_ffn2_w_gate, m_ffn2_w_up, m_ffn2_w_down, m_norm_final, v_ada_w, v_ada_b, v_norm_ffn1, v_ffn1_w_gate, v_ffn1_w_up, v_ffn1_w_down, v_norm_mix, v_w_in, v_forget_bias, v_w_branch_a, v_w_branch_b, v_w_out, v_norm_ffn2, v_ffn2_w_gate, v_ffn2_w_up, v_ffn2_w_down, v_norm_final):
    args = dict(locals())
    B, S, D = x.shape
    T = B * S
    cl = ada_w.shape[2]
    n_in = w_in.shape[2] * N_DEV
    nm = 2 * D + 3 * FOX_W + 3 * DIL_W
    nmp = -(-nm // 512) * 512
    GA, GB, QB, QA = 0, D, 2 * D, 2 * D + 3 * FOX_W
    xpos, ypos, cpos = _position()
    me = 4 * xpos + 2 * ypos + cpos

    packs = {n: _Packed(BIG_KIND[n], args[n].shape[1:], D) for n in BIG}
    offs, pads = {}, {}
    for names in GROUPS:
        r = 0
        for n in names:
            offs[n] = r
            r += packs[n].rows_pad
        pads[names] = -r % PACK_ROW_QUANTUM

    def pack_weights(names):
        return jnp.concatenate([packs[n].pack_local(args[n][0]).astype(BF16) for n in names]
                               + [jnp.zeros((pads[names], D), BF16)], axis=0)

    def unpack_weights(names, land):
        out = {}
        for n in names:
            p = packs[n]
            if p.kind in "TN" and p.rows == p.rows_pad and offs[n] % p.rows == 0:
                out[n] = _Slab(land, offs[n], p.rows)
            else:
                out[n] = p.full(land[:, offs[n]:offs[n] + p.rows_pad])
        return out

    def pack_grads(names, gfull):
        return jnp.concatenate([packs[n].pack_grad(gfull[n]) for n in names] + [jnp.zeros((N_DEV, pads[names], D), F32)], axis=1)

    def unpack_grads(names, g_local):
        return {n: packs[n].unpack_local(g_local[offs[n]:offs[n] + packs[n].rows_pad])[None] for n in names}

    W = unpack_weights(GROUPS[0], _weight_allgather(pack_weights(GROUPS[0])))

    c_all = _small_allgather(c, "gather_c").reshape(N_DEV * B, D)
    b_cols = lax.dynamic_slice(ada_b, (0, me * cl), (1, cl))
    mod_cols, c_act = _ada_forward(c_all, ada_w[0], b_cols)
    mod_all = _small_allgather(mod_cols, "gather_mod").transpose(1, 0, 2).reshape(N_DEV * B, N_MOD * D)
    mod = lax.dynamic_slice(mod_all, (me * B, 0), (B, N_MOD * D)).reshape(B, N_MOD, 1, D)
    sh1, sc1, gt1, sh2, sc2, gt2, sh3, sc3, gt3 = [mod[:, i] for i in range(N_MOD)]

    x0 = x.reshape(T, D)
    x1, saved1, land = _ffn_forward("ffn1", x0, norm_ffn1, sh1, sc1, gt1, W["ffn1_w_gate"], W["ffn1_w_up"], W["ffn1_w_down"], S,
                                    gather=pack_weights(GROUPS[1]))
    W.update(unpack_weights(GROUPS[1], land))
    winT = W["w_in"]
    o_f = 3 * DIL_W + 3 * FOX_W
    wmT = jnp.concatenate([winT[o_f + 8:], winT[3 * DIL_W:o_f], winT[:3 * DIL_W], jnp.zeros((nmp - nm, D), BF16)], axis=0)
    wfT = jnp.concatenate([winT[o_f:o_f + 8], jnp.zeros((LANES - 8, D), BF16)], axis=0)

    tm1k = _div(T, 1024, 8)
    fb = jnp.pad(forget_bias, ((0, 0), (0, LANES - FOX_HEADS)))

    def proj(accs, fbv):
        fl = accs[1] + fbv
        lane = lax.broadcasted_iota(jnp.int32, fl.shape, 1)
        ls = jnp.minimum(fl, 0.0) - jnp.log(1.0 + jnp.exp(-jnp.abs(fl)))
        return [accs[0], jnp.where(lane < FOX_HEADS, ls, 0.0), fl]

    tms = _div(S, 512, 8)
    h2, pm, logsig, flog, land = _norm_matmul("mix_proj", x1, norm_mix, sc2, sh2, [wmT, wfT], proj,
                                              [(nmp, BF16), (LANES, F32), (LANES, F32)], S, vecs=[fb],
                                              rider=_gather_direct_rider(pack_weights(GROUPS[2])))
    cum = _cumsum(logsig.reshape(B, S, LANES))
    cumT = cum[:, :, :8].transpose(0, 2, 1)
    pm3 = pm.reshape(B, S, nmp)
    tq = _div(S, 512, LANES)
    qcol, kcol, vcol = QB // FOX_W, QB // FOX_W + 1, QB // FOX_W + 2
    o_b, o_b32, lse_b, land = _fox_fwd(pm3, cumT, qcol, kcol, vcol, tq, rider=_gather_forward_rider(land))
    W.update(unpack_weights(GROUPS[2], land))
    y_b = o_b.reshape(T, FOX_W)

    qa_blk = QA // DIL_GW
    dil_src, dil_cols = [], []
    for g, (_, d) in enumerate(DIL_GROUPS):
        if d == 1:
            dil_src.append(pm3)
            dil_cols.append((qa_blk + g, qa_blk + N_DIL + g, qa_blk + 2 * N_DIL + g))
        else:
            starts = [QA + (i * N_DIL + g) * DIL_GW for i in range(3)]
            qkv = jnp.concatenate([pm[:, c:c + DIL_GW] for c in starts], axis=1)
            dil_src.append(_residue_order(qkv, B, S, d))
            dil_cols.append((0, 1, 2))
    dil_o, dil_lse = [], []
    for g, (_, d) in enumerate(DIL_GROUPS):
        o_g, lse_g = _dil_fwd(g, dil_src[g], dil_cols[g])
        dil_o.append(_token_order(o_g, B, S, d))
        dil_lse.append(_token_order(lse_g, B, S, d))
    y_a, L_a = _dil_combine(dil_o, dil_lse, T, S)

    wa, wb, wout = W["w_branch_a"], W["w_branch_b"], W["w_out"]
    tnd = D
    tm5 = _div(T, 512, 8)
    yap = _matmul("mix_branch_a", "nn", [[(y_a, wa)]], T, D, DIL_GW, tm5, tnd, DIL_GW, [BF16])[0]

    def merge(accs, ex):
        yapv, gav, gbv = ex
        ybp = accs[0]
        return [ybp, _sigmoid(gav.astype(F32)) * yapv.astype(F32) + _sigmoid(gbv.astype(F32)) * ybp]

    ybp, merged = _matmul("mix_branch_b", "nn", [[(y_b, wb)]], T, D, FOX_W, tm5, tnd, FOX_W, [BF16, BF16],
                          extras=[(yap, "tile", 0), (pm, "tile", GA), (pm, "tile", GB)], epilogue=merge)

    def out_proj(accs, ex):
        xv, gtv = ex
        return [xv + gtv * accs[0], accs[0]]

    x2, ymix = _matmul("mix_out", "nn", [[(merged, wout)]], T, D, D, tms, tnd, D, [F32, BF16],
                       extras=[(x1, "tile", 0), (gt2, "brow", 0)], epilogue=out_proj, rows_per_example=S)

    (dx3, loss_b, dg_final), saved3, _ = _ffn_forward(
        "ffn2", x2, norm_ffn2, sh3, sc3, gt3, W["ffn2_w_gate"], W["ffn2_w_up"], W["ffn2_w_down"], S,
        loss=(loss_target.reshape(T, D), norm_final.reshape(1, D)))

    dx2, (dsh3, dsc3, dgt3, dg3), (dwg2, dwu2, dwd2), _ = _ffn_backward(
        "ffn2", dx3, saved3, norm_ffn2, sc3, gt3, W["ffn2_w_gate"], W["ffn2_w_up"], W["ffn2_w_down"], S)
    g3 = pack_grads(GROUPS[2], {"ffn2_w_gate": dwg2, "ffn2_w_up": dwu2, "ffn2_w_down": dwd2})

    def merge_grad(dm, ex):
        gav, gbv, yapv, ybpv = [e.astype(F32) for e in ex]
        sga, sgb = _sigmoid(gav), _sigmoid(gbv)
        return [dm * sga, dm * sgb, dm * yapv * sga * (1.0 - sga), dm * ybpv * sgb * (1.0 - sgb)]

    dym, dgt2, dyap, dybp, dga, dgb, sib3 = _gated_grad_matmul(
        "mix_merge_grad", dx2, ymix, gt2, 1.0, wout, [(pm, GA // D), (pm, GB // D), (yap, 0), (ybp, 0)], merge_grad, [BF16] * 4, S,
        rider=_sibling_exchange_rider(g3))
    sums3, own3 = _chip_sums(g3, sib3)
    tkw = _div(T, 1024, LANES)
    dwout = _matmul("mix_dw_out", "tn", [[(merged, dym)]], D, D, T, D, D, tkw, [F32])[0]
    dwa = _matmul("mix_dw_a", "tn", [[(y_a, dyap)]], DIL_GW, D, T, DIL_GW, D, tkw, [F32])[0]
    dwb = _matmul("mix_dw_b", "tn", [[(y_b, dybp)]], FOX_W, D, T, FOX_W, D, tkw, [F32])[0]
    dy_a = _matmul("mix_dy_a", "nt", [[(dyap, wa)]], T, DIL_GW, D, tm1k, DIL_GW, D, [F32])[0]
    dy_b = _matmul("mix_dy_b", "nt", [[(dybp, wb)]], T, FOX_W, D, tm1k, FOX_W, D, [BF16])[0]

    do3 = dy_b.reshape(B, S, FOX_W)
    delta_b = _fox_delta(dy_b, o_b32.reshape(T, FOX_W), T, S).reshape(B, S, LANES)
    dq_b, ds_rows, dk_b, dv_b, ds_cols, recv3 = _fox_bwd(pm3, do3, delta_b, lse_b, cumT, qcol, kcol, vcol, tq,
                                                         rider=_chip_exchange_rider(sums3))

    delta_a, dy_a16 = _dil_delta(dy_a, y_a, T, S)
    dqs, dks, dvs = [], [], []
    for g, (_, d) in enumerate(DIL_GROUPS):
        dq_g, dk_g, dv_g = _dil_bwd(g, dil_src[g], dil_cols[g], _residue_order(L_a, B, S, d),
                                    _residue_order(dy_a16, B, S, d), _residue_order(delta_a, B, S, d))
        dqs.append(_token_order(dq_g, B, S, d))
        dks.append(_token_order(dk_g, B, S, d))
        dvs.append(_token_order(dv_g, B, S, d))

    dcum = ds_rows - ds_cols
    dcum_run = _cumsum(dcum)
    dcum_tot = dcum_run[:, S - 1:S, :]

    def forget_grad_fn(run, dcv, fl, tot):
        lane = lax.broadcasted_iota(jnp.int32, fl.shape, 1)
        df = jnp.where(lane < FOX_HEADS, (tot - run + dcv) * _sigmoid_exp(-fl), 0.0)
        return [df, _colsum(df)]

    df16, dfb = _rowwise("forget_gate_grad", forget_grad_fn, T, tms,
                         [(dcum_run.reshape(T, LANES), "row", None), (dcum.reshape(T, LANES), "row", None), (flog, "row", None),
                          (dcum_tot, "bvec", None)],
                         [("row", LANES, BF16), ("bacc", LANES, F32)], S)

    dpm = jnp.concatenate([dga, dgb, dq_b.reshape(T, FOX_W), dk_b.reshape(T, FOX_W), dv_b.reshape(T, FOX_W)]
                          + dqs + dks + dvs + ([jnp.zeros((T, nmp - nm), BF16)] if nmp > nm else []), axis=1)
    tmn = _div(nmp, 2048, LANES)
    dwmT = _matmul("mix_dw_in", "tn", [[(dpm, h2)]], nmp, D, T, tmn, D, tkw, [F32])[0]
    dwfT = _matmul("mix_dw_f", "tn", [[(df16, h2)]], LANES, D, T, LANES, D, tkw, [F32])[0]
    dwinT = jnp.concatenate([dwmT[QA:QA + 3 * DIL_W], dwmT[QB:QB + 3 * FOX_W], dwfT[:8], dwmT[GA:2 * D]], axis=0)
    g2 = pack_grads(GROUPS[1], {"w_in": dwinT, "w_branch_a": dwa, "w_branch_b": dwb, "w_out": dwout})
    dx1, dsh2, dsc2, dgmix, sib2 = _matmul_normmod_bwd("mix_dh", [(dpm, wmT), (df16, wfT)], x1, norm_mix, sc2, dx2, S,
                                                       rider=_sibling_exchange_rider(g2))
    sums2, own2 = _chip_sums(g2, sib2)

    own1 = []

    def ffn1_exchange(dwg, dwu, dwd):
        sums1, own = _chip_sums(pack_grads(GROUPS[0], {"ffn1_w_gate": dwg, "ffn1_w_up": dwu, "ffn1_w_down": dwd}))
        own1.append(own)
        return _chip_exchange_rider(sums1)

    dx0, (dsh1, dsc1, dgt1, dg1), _, (recv2, recv1) = _ffn_backward(
        "ffn1", dx1, saved1, norm_ffn1, sc1, gt1, W["ffn1_w_gate"], W["ffn1_w_up"], W["ffn1_w_down"], S,
        rider=_chip_exchange_rider(sums2), dh_rider=ffn1_exchange)
    own1 = own1[0]
    grad_x = dx0.reshape(B, S, D)

    dmod = jnp.concatenate([dsh1, dsc1, dgt1, dsh2, dsc2, dgt2, dsh3, dsc3, dgt3], axis=1).reshape(B, N_MOD * D)
    fbg = jnp.sum(dfb, axis=0)
    small = jnp.concatenate([jnp.sum(dg1, axis=0), jnp.sum(dgmix, axis=0), jnp.sum(dg3, axis=0), jnp.sum(dg_final, axis=0),
                             fbg, jnp.sum(loss_b, axis=0)], axis=1)
    n_small = small.shape[1]
    tail = _small_allgather(jnp.concatenate([dmod, jnp.pad(small, ((0, 0), (0, N_MOD * D - n_small)))], axis=0), "gather_tail")
    dmod_all = tail[:, :B].reshape(N_DEV * B, N_MOD * D)
    dmod_cols = lax.dynamic_slice(dmod_all, (0, me * cl), (N_DEV * B, cl))
    g_ada_w, g_ada_b = _ada_backward(c_act, dmod_cols, dmod_all)

    small = _sum_devices(tail[:, B:, :n_small])
    g_small = {"norm_ffn1": small[:, 0:D], "norm_mix": small[:, D:2 * D], "norm_ffn2": small[:, 2 * D:3 * D],
               "norm_final": small[:, 3 * D:4 * D], "forget_bias": small[:, 4 * D:4 * D + FOX_HEADS], "ada_b": g_ada_b}
    loss = small[0, 4 * D + LANES]

    grads = {"ada_w": g_ada_w[None]}
    for names, own, recv in ((GROUPS[0], own1, recv1), (GROUPS[1], own2, recv2), (GROUPS[2], own3, recv3)):
        grads.update(unpack_grads(names, _final_grad_sum(own, recv)))

    delta, new_m, new_v = {}, {}, {}
    for n in ["ada_w"] + BIG:
        shp = args[n].shape
        d_, m_, v_ = _adamw(f"adamw_{n}", args[n][0], grads[n][0], args["m_" + n][0], args["v_" + n][0])
        delta[n], new_m[n], new_v[n] = d_.reshape(shp), m_.reshape(shp), v_.reshape(shp)
    sizes = [args[n].size for n in SMALL]
    tot = sum(sizes)
    padded = -(-tot // (8 * LANES)) * (8 * LANES)

    def flat(get):
        v = jnp.concatenate([get(n).reshape(-1) for n in SMALL])
        return jnp.pad(v, (0, padded - tot)).reshape(8, padded // 8)

    d_s, m_s, v_s = _adamw("adamw_small", flat(lambda n: args[n]), flat(lambda n: g_small[n]), flat(lambda n: args["m_" + n]),
                           flat(lambda n: args["v_" + n]))
    o = 0
    for n, sz in zip(SMALL, sizes):
        shp = args[n].shape
        grads[n] = g_small[n].reshape(shp)
        delta[n] = d_s.reshape(-1)[o:o + sz].reshape(shp)
        new_m[n] = m_s.reshape(-1)[o:o + sz].reshape(shp)
        new_v[n] = v_s.reshape(-1)[o:o + sz].reshape(shp)
        o += sz

    order = ["ada_w", "ada_b", "norm_ffn1", "ffn1_w_gate", "ffn1_w_up", "ffn1_w_down", "norm_mix", "w_in", "forget_bias",
             "w_branch_a", "w_branch_b", "w_out", "norm_ffn2", "ffn2_w_gate", "ffn2_w_up", "ffn2_w_down", "norm_final"]
    return (loss, grad_x, *[grads[n] for n in order], *[delta[n] for n in order], *[new_m[n] for n in order],
            *[new_v[n] for n in order])
```
